```python
import math
import jax, jax.numpy as jnp
from jax import lax
import numpy as np

D_MODEL = 1024
BATCH = 8
SEQ = 4096
DEPTH = 1

D_MIX = D_MODEL
CONV_WIDTH = D_MIX // 2
ATTN_WIDTH = D_MIX - CONV_WIDTH
HEAD_DIM = 64
N_ATTN_HEADS = ATTN_WIDTH // HEAD_DIM
CONV_K = 3
Q_BLOCK = 128
LN_EPS = 1e-5
DEEPNORM_ALPHA = (2.0 * DEPTH) ** 0.25
DEEPNORM_BETA = (8.0 * DEPTH) ** -0.25
PROJ_WIDTH = 4 * CONV_WIDTH + 4 * ATTN_WIDTH

kernel_name = "hymba_shortconv_stickbreaking_deepnorm"


def _layer_norm(y, gain, bias):
    y32 = y.astype(jnp.float32)
    mu = jnp.mean(y32, axis=-1, keepdims=True)
    var = jnp.mean(jnp.square(y32 - mu), axis=-1, keepdims=True)
    out = (y32 - mu) * lax.rsqrt(var + LN_EPS) * gain.astype(jnp.float32) + bias.astype(jnp.float32)
    return out.astype(y.dtype)


def _short_conv(b_gate, c_gate, h, conv_w):
    u = c_gate * h
    seq = u.shape[1]
    u_pad = jnp.pad(u, ((0, 0), (CONV_K - 1, 0), (0, 0)))
    y = conv_w[CONV_K - 1] * u_pad[:, CONV_K - 1:CONV_K - 1 + seq]
    for tap in range(CONV_K - 1):
        y = y + conv_w[tap] * u_pad[:, tap:tap + seq]
    return b_gate * y


def _stick_breaking_attention(q, k, v):
    bsz, seq, _ = q.shape
    out_dtype = v.dtype
    def heads(t):
        return t.reshape(bsz, seq, N_ATTN_HEADS, HEAD_DIM).transpose(0, 2, 1, 3).astype(jnp.float32)
    qh = heads(q) * (HEAD_DIM ** -0.5)
    kh, vh = heads(k), heads(v)
    n_blocks = seq // Q_BLOCK
    outs = []
    for blk in range(n_blocks):
        q0 = blk * Q_BLOCK
        kv_len = q0 + Q_BLOCK
        qb = qh[:, :, q0:kv_len]
        kb = kh[:, :, :kv_len]
        vb = vh[:, :, :kv_len]
        z = jnp.einsum('bhqd,bhkd->bhqk', qb, kb)
        t_pos = q0 + jnp.arange(Q_BLOCK)[:, None]
        s_pos = jnp.arange(kv_len)[None, :]
        strict = s_pos < t_pos
        neg_log_keep = jnp.where(strict, jax.nn.softplus(z), 0.0)
        after = lax.cumsum(neg_log_keep, axis=3, reverse=True) - neg_log_keep
        log_w = jax.nn.log_sigmoid(z) - after
        w = jnp.where(strict, jnp.exp(log_w), 0.0)
        outs.append(jnp.einsum('bhqk,bhkd->bhqd', w, vb))
    o = jnp.concatenate(outs, axis=2)
    return o.transpose(0, 2, 1, 3).reshape(bsz, seq, ATTN_WIDTH).astype(out_dtype)


def _fwd_setup_inputs(seed: int = 0) -> dict:
    key = jax.random.key(seed)
    k_x, k_in, k_conv, k_out, k_g, k_b = jax.random.split(key, 6)
    x = jax.random.normal(k_x, (BATCH, SEQ, D_MODEL), jnp.float32)
    w_in = jax.random.normal(k_in, (DEPTH, D_MODEL, PROJ_WIDTH), jnp.float32) * (D_MODEL ** -0.5)
    conv_w = jax.random.normal(k_conv, (DEPTH, CONV_K, CONV_WIDTH), jnp.float32) * (CONV_K ** -0.5)
    w_out = jax.random.normal(k_out, (DEPTH, D_MIX, D_MODEL), jnp.float32) * (D_MIX ** -0.5) * DEEPNORM_BETA
    ln_gain = 1.0 + 0.05 * jax.random.normal(k_g, (DEPTH, D_MODEL), jnp.float32)
    ln_bias = 0.02 * jax.random.normal(k_b, (DEPTH, D_MODEL), jnp.float32)
    return {"x": x, "w_in": w_in, "conv_w": conv_w, "w_out": w_out,
            "ln_gain": ln_gain, "ln_bias": ln_bias}


def _fwd_reference(x, w_in, conv_w, w_out, ln_gain, ln_bias):
    h_res = x
    for layer in range(DEPTH):
        proj = jnp.einsum('bsd,de->bse', h_res, w_in[layer])
        b_c, c_c, h_c, z_c, q, k, v, z_a = jnp.split(proj, 8, axis=-1)
        conv_out = jax.nn.silu(z_c) * _short_conv(b_c, c_c, h_c, conv_w[layer])
        attn_out = jax.nn.silu(z_a) * _stick_breaking_attention(q, k, v)
        mix = jnp.concatenate([conv_out, attn_out], axis=-1)
        sub = jnp.einsum('bse,ed->bsd', mix, w_out[layer])
        h_res = _layer_norm(DEEPNORM_ALPHA * h_res + sub, ln_gain[layer], ln_bias[layer])
    return h_res


import jax as _jax
import jax.numpy as _jnp

TWIN_FORMAT = 'train_step'
FWD_PARAMS = ['x', 'w_in', 'conv_w', 'w_out', 'ln_gain', 'ln_bias']
TWIN_WEIGHTS = ['w_in', 'conv_w', 'w_out', 'ln_gain', 'ln_bias']
TWIN_DIFF_INPUT = 'x'
TWIN_INPUTS = ['x', 'w_in', 'conv_w', 'w_out', 'ln_gain', 'ln_bias', 'loss_target', 'm_w_in', 'm_conv_w', 'm_w_out', 'm_ln_gain', 'm_ln_bias', 'v_w_in', 'v_conv_w', 'v_w_out', 'v_ln_gain', 'v_ln_bias']
TWIN_OUTPUTS = ['loss', 'grad_x', 'grad_w_in', 'grad_conv_w', 'grad_w_out', 'grad_ln_gain', 'grad_ln_bias', 'delta_w_in', 'delta_conv_w', 'delta_w_out', 'delta_ln_gain', 'delta_ln_bias', 'new_m_w_in', 'new_m_conv_w', 'new_m_w_out', 'new_m_ln_gain', 'new_m_ln_bias', 'new_v_w_in', 'new_v_conv_w', 'new_v_w_out', 'new_v_ln_gain', 'new_v_ln_bias']
TWIN_LEAF_KINDS = {'loss': 'loss', 'grad_x': 'grad_x', 'grad_w_in': 'grad_w', 'grad_conv_w': 'grad_w', 'grad_w_out': 'grad_w', 'grad_ln_gain': 'grad_w', 'grad_ln_bias': 'grad_w', 'delta_w_in': 'delta_w', 'delta_conv_w': 'delta_w', 'delta_w_out': 'delta_w', 'delta_ln_gain': 'delta_w', 'delta_ln_bias': 'delta_w', 'new_m_w_in': 'new_m', 'new_m_conv_w': 'new_m', 'new_m_w_out': 'new_m', 'new_m_ln_gain': 'new_m', 'new_m_ln_bias': 'new_m', 'new_v_w_in': 'new_v', 'new_v_conv_w': 'new_v', 'new_v_w_out': 'new_v', 'new_v_ln_gain': 'new_v', 'new_v_ln_bias': 'new_v'}


def _forward(args):
    return _fwd_reference(*[args[k] for k in FWD_PARAMS])


def _output_shape():
    out = _jax.eval_shape(lambda: _forward(_fwd_setup_inputs(0)))
    return out.shape, out.dtype

N_MICROBATCH = 1
ADAM_LR = 0.001
ADAM_B1 = 0.9
ADAM_B2 = 0.999
ADAM_EPS = 1e-08
ADAM_WD = 0.01
ADAM_STEP = 10
PER_EXAMPLE_BATCH_AXIS = {'x': 0, 'loss_target': 0}
SHARED_INPUTS = []
_WEIGHT_DTYPES = {'w_in': _jnp.float32, 'conv_w': _jnp.float32, 'w_out': _jnp.float32, 'ln_gain': _jnp.float32, 'ln_bias': _jnp.float32}
MOMENT_SCALE = {'w_in': 4.108937e-02, 'conv_w': 5.186907e-02, 'w_out': 7.344354e-02, 'ln_gain': 3.194146e+01, 'ln_bias': 4.855635e-01}


def _to_microbatches(a, axis):
    t = _jnp.moveaxis(a, axis, 0)
    t = t.reshape((N_MICROBATCH, t.shape[0] // N_MICROBATCH) + t.shape[1:])
    return _jnp.moveaxis(t, 1, axis + 1)


def setup_inputs(seed: int = 0) -> dict:
    inp = _fwd_setup_inputs(seed)
    key = _jax.random.fold_in(_jax.random.key(seed), 7919)
    shape, _ = _output_shape()
    out = dict(inp)
    out["loss_target"] = _jax.random.normal(_jax.random.fold_in(key, 0), shape, _jnp.float32)
    for i, name in enumerate(TWIN_WEIGHTS):
        w = inp[name].astype(_jnp.float32)
        if MOMENT_SCALE is None:
            s = _jnp.sqrt(_jnp.mean(_jnp.square(w)) + 1e-30)
        else:
            s = MOMENT_SCALE[name]
        km, kv = _jax.random.split(_jax.random.fold_in(key, i + 1))
        out[name] = w
        out["m_" + name] = s * _jax.random.normal(km, w.shape, _jnp.float32)
        out["v_" + name] = (s * s) * _jax.random.uniform(kv, w.shape, _jnp.float32, 0.5, 1.5)
    if N_MICROBATCH > 1:
        for name, axis in PER_EXAMPLE_BATCH_AXIS.items():
            out[name] = _to_microbatches(out[name], axis)
    return {'x': out['x'], 'w_in': out['w_in'], 'conv_w': out['conv_w'], 'w_out': out['w_out'], 'ln_gain': out['ln_gain'], 'ln_bias': out['ln_bias'], 'loss_target': out['loss_target'], 'm_w_in': out['m_w_in'], 'm_conv_w': out['m_conv_w'], 'm_w_out': out['m_w_out'], 'm_ln_gain': out['m_ln_gain'], 'm_ln_bias': out['m_ln_bias'], 'v_w_in': out['v_w_in'], 'v_conv_w': out['v_conv_w'], 'v_w_out': out['v_w_out'], 'v_ln_gain': out['v_ln_gain'], 'v_ln_bias': out['v_ln_bias']}


def _loss(weights, diff, rest, loss_target):
    with _jax.named_scope("forward"):
        args = {**rest, TWIN_DIFF_INPUT: diff, **{k: w.astype(_WEIGHT_DTYPES[k]) for k, w in weights.items()}}
        y = _forward(args)
    with _jax.named_scope("loss_head"):
        err = _jnp.square(y.astype(_jnp.float32) - loss_target)
        return 0.5 * _jnp.sum(_jnp.mean(err, axis=-1)) if err.ndim else 0.5 * err


def _adamw(w, g, m, v):
    m = ADAM_B1 * m + (1.0 - ADAM_B1) * g
    v = ADAM_B2 * v + (1.0 - ADAM_B2) * _jnp.square(g)
    m_hat = m / (1.0 - ADAM_B1 ** ADAM_STEP)
    v_hat = v / (1.0 - ADAM_B2 ** ADAM_STEP)
    delta = -ADAM_LR * (m_hat / (_jnp.sqrt(v_hat) + ADAM_EPS) + ADAM_WD * w)
    return delta, m, v


def reference(x, w_in, conv_w, w_out, ln_gain, ln_bias, loss_target, m_w_in, m_conv_w, m_w_out, m_ln_gain, m_ln_bias, v_w_in, v_conv_w, v_w_out, v_ln_gain, v_ln_bias):
    given = dict(x=x, w_in=w_in, conv_w=conv_w, w_out=w_out, ln_gain=ln_gain, ln_bias=ln_bias, loss_target=loss_target, m_w_in=m_w_in, m_conv_w=m_conv_w, m_w_out=m_w_out, m_ln_gain=m_ln_gain, m_ln_bias=m_ln_bias, v_w_in=v_w_in, v_conv_w=v_conv_w, v_w_out=v_w_out, v_ln_gain=v_ln_gain, v_ln_bias=v_ln_bias)
    weights = {n: given[n] for n in TWIN_WEIGHTS}
    shared = {n: given[n] for n in SHARED_INPUTS}
    per_example = {n: given[n] for n in ['x']}
    grad_fn = _jax.value_and_grad(_loss, argnums=(0, 1))

    def one_microbatch(ex, loss_target):
        ex = dict(ex)
        diff = ex.pop(TWIN_DIFF_INPUT)
        return grad_fn(weights, diff, {**shared, **ex}, loss_target)

    if N_MICROBATCH == 1:
        loss, (grad_w, grad_x) = one_microbatch(per_example, given["loss_target"])
    else:
        def body(carry, xs):
            loss_sum, grad_sum = carry
            l_k, (gw_k, gx_k) = one_microbatch(xs[0], xs[1])
            with _jax.named_scope("update"):
                return (loss_sum + l_k, _jax.tree.map(_jnp.add, grad_sum, gw_k)), gx_k

        init = (_jnp.zeros((), _jnp.float32), _jax.tree.map(_jnp.zeros_like, weights))
        (loss, grad_w), grad_x = _jax.lax.scan(body, init, (per_example, given["loss_target"]))
    with _jax.named_scope("update"):
        delta_w, new_m, new_v = {}, {}, {}
        for n in TWIN_WEIGHTS:
            delta_w[n], new_m[n], new_v[n] = _adamw(weights[n], grad_w[n], given["m_" + n], given["v_" + n])
    return (loss, grad_x, *[grad_w[n] for n in TWIN_WEIGHTS], *[delta_w[n] for n in TWIN_WEIGHTS],
            *[new_m[n] for n in TWIN_WEIGHTS], *[new_v[n] for n in TWIN_WEIGHTS])
```

```python
import functools

import jax
import jax.numpy as jnp
from jax import lax
from jax.experimental import pallas as pl
from jax.experimental.pallas import tpu as pltpu

F32 = jnp.float32
BF16 = jnp.bfloat16
MESH = pl.DeviceIdType.MESH

N_DEV = 8
HEAD_DIM = 64
PAIR = 128
SUBLANES = 8
LN_EPS = 1e-5
ALPHA = 2.0 ** 0.25
ADAM_LR, ADAM_B1, ADAM_B2, ADAM_EPS, ADAM_WD, ADAM_STEP = 0.001, 0.9, 0.999, 1e-08, 0.01, 10

ROW_GAIN, ROW_BIAS, ROW_CONV, ROW_LOSS = 0, 1, 2, 5

NT = (((1,), (1,)), ((), ()))
TN = (((0,), (0,)), ((), ()))


V7X_VMEM_BYTES = 64 * 1024 * 1024


def _params(vmem_mib):
    assert vmem_mib * 1024 * 1024 < V7X_VMEM_BYTES
    return pltpu.CompilerParams(vmem_limit_bytes=vmem_mib * 1024 * 1024)


def _dot(a, b, dims=None):
    if dims is None:
        return jnp.dot(a, b, preferred_element_type=F32)
    return lax.dot_general(a, b, dims, preferred_element_type=F32)


def _sigmoid(z):
    return 1.0 / (1.0 + jnp.exp(-z))


def _mesh_pos():
    return lax.axis_index("x"), lax.axis_index("y"), lax.axis_index("c")


def _adamw(w, g, m, v):
    nm = ADAM_B1 * m + (1.0 - ADAM_B1) * g
    nv = ADAM_B2 * v + (1.0 - ADAM_B2) * (g * g)
    m_hat = nm * (1.0 / (1.0 - ADAM_B1 ** ADAM_STEP))
    v_hat = nv * (1.0 / (1.0 - ADAM_B2 ** ADAM_STEP))
    delta = -ADAM_LR * (m_hat / (jnp.sqrt(v_hat) + ADAM_EPS) + ADAM_WD * w)
    return delta, nm, nv


def _gather_weights(w_in_s, w_out_s, conv_s):
    d_model, cw = w_in_s.shape
    rows_out = w_out_s.shape[0]
    n_arr = 3

    def body(win_ref, wout_ref, conv_ref, win_all, wout_all, conv_all, send_sems, recv_sems):
        x, y, c = _mesh_pos()
        me = (x, y, c)
        sibling = (x, y, 1 - c)
        chips = [(1 - x, y), (x, 1 - y), (1 - x, 1 - y)]
        bufs = (win_all, wout_all, conv_all)

        def slot(pos):
            return 4 * pos[0] + 2 * pos[1] + pos[2]

        win_all[slot(me)] = win_ref[...].astype(BF16)
        wout_all[slot(me)] = wout_ref[...].astype(BF16)
        conv_all[slot(me)] = conv_ref[...]

        def copy(a, k, block, to):
            ref = bufs[a].at[slot(block)]
            return pltpu.make_async_remote_copy(
                src_ref=ref, dst_ref=ref, send_sem=send_sems.at[a, k], recv_sem=recv_sems.at[a, k],
                device_id=to, device_id_type=MESH)

        first, passed = [], []
        for a in range(n_arr):
            first.append(copy(a, 0, me, sibling))
            first += [copy(a, 1 + j, me, (*chip, c)) for j, chip in enumerate(chips)]
        for cp in first:
            cp.start()
        for j, chip in enumerate(chips):
            for a in range(n_arr):
                copy(a, 1 + j, (*chip, c), me).wait_recv()
                cp = copy(a, 4 + j, (*chip, c), sibling)
                cp.start()
                passed.append(cp)
        for a in range(n_arr):
            copy(a, 0, sibling, me).wait_recv()
            for j, chip in enumerate(chips):
                copy(a, 4 + j, (*chip, 1 - c), me).wait_recv()
        for cp in first + passed:
            cp.wait_send()

    vmem = pl.BlockSpec(memory_space=pltpu.VMEM)
    return pl.pallas_call(
        body, name="gather_weights",
        out_shape=(jax.ShapeDtypeStruct((N_DEV, d_model, cw), BF16),
                   jax.ShapeDtypeStruct((N_DEV, rows_out, d_model), BF16),
                   jax.ShapeDtypeStruct((N_DEV,) + conv_s.shape, F32)),
        in_specs=[vmem, vmem, vmem], out_specs=(vmem, vmem, vmem),
        scratch_shapes=[pltpu.SemaphoreType.DMA((n_arr, 7)), pltpu.SemaphoreType.DMA((n_arr, 7))],
    )(w_in_s, w_out_s, conv_s)


def _proj(x, win_all, tm):
    seq, d_model = x.shape
    nch, _, cw = win_all.shape

    def body(x_ref, w_ref, o_ref):
        xb = x_ref[...].astype(BF16)
        for j in range(nch):
            o_ref[j] = _dot(xb, w_ref[j]).astype(BF16)

    return pl.pallas_call(
        body, name="proj", grid=(seq // tm,),
        in_specs=[pl.BlockSpec((tm, d_model), lambda i: (i, 0)),
                  pl.BlockSpec((nch, d_model, cw), lambda i: (0, 0, 0))],
        out_specs=pl.BlockSpec((nch, tm, cw), lambda i: (0, i, 0)),
        out_shape=jax.ShapeDtypeStruct((nch, seq, cw), BF16),
        compiler_params=_params(48),
    )(x, win_all)


def _conv_taps(ext, w_ref, rc):
    u0 = ext[SUBLANES:SUBLANES + rc]
    u1 = pltpu.roll(ext, 1, 0)[SUBLANES:SUBLANES + rc]
    u2 = pltpu.roll(ext, 2, 0)[SUBLANES:SUBLANES + rc]
    return w_ref[2:3, :] * u0 + w_ref[1:2, :] * u1 + w_ref[0:1, :] * u2, u0, u1, u2


def _conv_fwd(proj, conv_full, rc):
    _, seq, cw = proj.shape

    def body(b_ref, c_ref, h_ref, z_ref, w_ref, o_ref, u_scr):
        u_scr[0:SUBLANES, :] = jnp.zeros((SUBLANES, PAIR), F32)

        def fill(r, _):
            base = pl.multiple_of(r * rc, rc)
            rows = pl.ds(base, rc)
            u_scr[pl.ds(base + SUBLANES, rc), :] = c_ref[rows, :].astype(F32) * h_ref[rows, :].astype(F32)
            return 0

        lax.fori_loop(0, seq // rc, fill, 0)

        def out(r, _):
            base = pl.multiple_of(r * rc, rc)
            rows = pl.ds(base, rc)
            ext = u_scr[pl.ds(base, rc + SUBLANES), :]
            y, _, _, _ = _conv_taps(ext, w_ref, rc)
            z = z_ref[rows, :].astype(F32)
            o_ref[rows, :] = (z * _sigmoid(z) * b_ref[rows, :].astype(F32) * y).astype(BF16)
            return 0

        lax.fori_loop(0, seq // rc, out, 0)

    def chunk(j):
        return pl.BlockSpec((None, seq, PAIR), lambda cb, j=j: (j, 0, cb))

    return pl.pallas_call(
        body, name="conv_fwd", grid=(cw // PAIR,),
        in_specs=[chunk(0), chunk(1), chunk(2), chunk(3), pl.BlockSpec((SUBLANES, PAIR), lambda cb: (0, cb))],
        out_specs=pl.BlockSpec((seq, PAIR), lambda cb: (0, cb)),
        out_shape=jax.ShapeDtypeStruct((seq, cw), BF16),
        scratch_shapes=[pltpu.VMEM((seq + SUBLANES, PAIR), F32)],
    )(proj, proj, proj, proj, conv_full)


def _conv_bwd(proj, conv_full, d_mix_conv, rc):
    _, seq, cw = proj.shape

    def body(b_ref, c_ref, h_ref, z_ref, w_ref, g_ref, dp_ref, dw_ref, u_scr, dy_scr):
        u_scr[0:SUBLANES, :] = jnp.zeros((SUBLANES, PAIR), F32)
        dy_scr[seq:seq + SUBLANES, :] = jnp.zeros((SUBLANES, PAIR), F32)

        def fill(r, _):
            base = pl.multiple_of(r * rc, rc)
            rows = pl.ds(base, rc)
            u_scr[pl.ds(base + SUBLANES, rc), :] = c_ref[rows, :].astype(F32) * h_ref[rows, :].astype(F32)
            return 0

        lax.fori_loop(0, seq // rc, fill, 0)

        def gate(r, acc):
            base = pl.multiple_of(r * rc, rc)
            rows = pl.ds(base, rc)
            ext = u_scr[pl.ds(base, rc + SUBLANES), :]
            y, u0, u1, u2 = _conv_taps(ext, w_ref, rc)
            z = z_ref[rows, :].astype(F32)
            b = b_ref[rows, :].astype(F32)
            g = g_ref[rows, :].astype(F32)
            sig = _sigmoid(z)
            dp_ref[3, rows, :] = (g * b * y * (sig * (1.0 + z * (1.0 - sig)))).astype(BF16)
            gs = g * (z * sig)
            dp_ref[0, rows, :] = (gs * y).astype(BF16)
            dy = gs * b
            dy_scr[rows, :] = dy
            a0, a1, a2 = acc
            return (a0 + jnp.sum(dy * u2, axis=0, keepdims=True),
                    a1 + jnp.sum(dy * u1, axis=0, keepdims=True),
                    a2 + jnp.sum(dy * u0, axis=0, keepdims=True))

        zero = jnp.zeros((1, PAIR), F32)
        a0, a1, a2 = lax.fori_loop(0, seq // rc, gate, (zero, zero, zero))
        dw_ref[...] = jnp.zeros((SUBLANES, PAIR), F32)
        dw_ref[0:1, :] = a0
        dw_ref[1:2, :] = a1
        dw_ref[2:3, :] = a2

        def back(r, _):
            base = pl.multiple_of(r * rc, rc)
            rows = pl.ds(base, rc)
            ext = dy_scr[pl.ds(base, rc + SUBLANES), :]
            n = rc + SUBLANES
            d0 = ext[0:rc]
            d1 = pltpu.roll(ext, n - 1, 0)[0:rc]
            d2 = pltpu.roll(ext, n - 2, 0)[0:rc]
            du = w_ref[2:3, :] * d0 + w_ref[1:2, :] * d1 + w_ref[0:1, :] * d2
            dp_ref[1, rows, :] = (du * h_ref[rows, :].astype(F32)).astype(BF16)
            dp_ref[2, rows, :] = (du * c_ref[rows, :].astype(F32)).astype(BF16)
            return 0

        lax.fori_loop(0, seq // rc, back, 0)

    def chunk(j):
        return pl.BlockSpec((None, seq, PAIR), lambda cb, j=j: (j, 0, cb))

    return pl.pallas_call(
        body, name="conv_bwd", grid=(cw // PAIR,),
        in_specs=[chunk(0), chunk(1), chunk(2), chunk(3), pl.BlockSpec((SUBLANES, PAIR), lambda cb: (0, cb)),
                  pl.BlockSpec((seq, PAIR), lambda cb: (0, cb))],
        out_specs=(pl.BlockSpec((4, seq, PAIR), lambda cb: (0, 0, cb)),
                   pl.BlockSpec((SUBLANES, PAIR), lambda cb: (0, cb))),
        out_shape=(jax.ShapeDtypeStruct((4, seq, cw), BF16), jax.ShapeDtypeStruct((SUBLANES, cw), F32)),
        scratch_shapes=[pltpu.VMEM((seq + SUBLANES, PAIR), F32), pltpu.VMEM((seq + SUBLANES, PAIR), F32)],
    )(proj, proj, proj, proj, conv_full, d_mix_conv)


def _split_dot(a, tri):
    hi = a.astype(BF16)
    lo = (a - hi.astype(F32)).astype(BF16)
    return _dot(hi, tri) + _dot(lo, tri)


def _softplus_parts(z):
    l = jnp.log1p(jnp.exp(-jnp.abs(z)))
    return jnp.maximum(z, 0.0) + l, jnp.minimum(z, 0.0) - l


def _attn_fwd(proj, t):
    _, seq, cw = proj.shape
    scale = HEAD_DIM ** -0.5

    def body(q_ref, k_ref, v_ref, za_ref, o_ref, mix_ref, tot_ref):
        i = pl.program_id(1)
        lane = lax.broadcasted_iota(jnp.int32, (t, PAIR), 1)
        row = lax.broadcasted_iota(jnp.int32, (t, t), 0)
        col = lax.broadcasted_iota(jnp.int32, (t, t), 1)
        strict = col < row
        upper = (row > col).astype(BF16)
        q = q_ref[...] * scale
        zero = jnp.zeros_like(q)
        qh = (jnp.where(lane < HEAD_DIM, q, zero), jnp.where(lane >= HEAD_DIM, q, zero))

        def block(j, state, masked):
            start = pl.multiple_of(j * t, t)
            kb = k_ref[pl.ds(start, t), :]
            vb = v_ref[pl.ds(start, t), :]
            new = []
            for h in range(2):
                carry, acc = state[h]
                z = _dot(qh[h], kb, NT)
                sp, ls = _softplus_parts(z)
                if masked:
                    sp = jnp.where(strict, sp, 0.0)
                after = _split_dot(sp, upper)
                w = jnp.exp(ls - after - carry)
                if masked:
                    w = jnp.where(strict, w, 0.0)
                acc = acc + _dot(w.astype(BF16), vb)
                carry = carry + (after[:, 0:1] + sp[:, 0:1])
                new.append((carry, acc))
            return tuple(new)

        init = ((jnp.zeros((t, 1), F32), jnp.zeros((t, PAIR), F32)),) * 2
        state = block(i, init, True)
        state = lax.fori_loop(0, i, lambda n, s: block(i - 1 - n, s, False), state)
        (c0, a0), (c1, a1) = state
        o = jnp.where(lane < HEAD_DIM, a0, a1)
        za = za_ref[...].astype(F32)
        o_ref[...] = o.astype(BF16)
        mix_ref[...] = (za * _sigmoid(za) * o).astype(BF16)
        tot_ref[...] = jnp.where(lane < HEAD_DIM, c0, c1)

    def tile(j):
        return pl.BlockSpec((None, t, PAIR), lambda p, i, j=j: (j, i, p))

    def full(j):
        return pl.BlockSpec((None, seq, PAIR), lambda p, i, j=j: (j, 0, p))

    out_tile = pl.BlockSpec((t, PAIR), lambda p, i: (i, p))
    return pl.pallas_call(
        body, name="attn_fwd", grid=(cw // PAIR, seq // t),
        in_specs=[tile(4), full(5), full(6), tile(7)],
        out_specs=(out_tile, out_tile, out_tile),
        out_shape=(jax.ShapeDtypeStruct((seq, cw), BF16), jax.ShapeDtypeStruct((seq, cw), BF16),
                   jax.ShapeDtypeStruct((seq, cw), F32)),
    )(proj, proj, proj, proj)


def _attn_bwd(proj, o, tot, d_mix_attn, t):
    _, seq, cw = proj.shape
    nb = seq // t
    scale = HEAD_DIM ** -0.5

    def body(q_ref, k_ref, v_ref, za_ref, o_ref, tot_ref, g_ref, dqz_ref, dkv_ref, dk_acc, dv_acc):
        i = pl.program_id(1)

        @pl.when(i == 0)
        def _():
            dk_acc[...] = jnp.zeros_like(dk_acc)
            dv_acc[...] = jnp.zeros_like(dv_acc)

        lane = lax.broadcasted_iota(jnp.int32, (t, PAIR), 1)
        row = lax.broadcasted_iota(jnp.int32, (t, t), 0)
        col = lax.broadcasted_iota(jnp.int32, (t, t), 1)
        strict = col < row
        upper = (row > col).astype(BF16)
        lower = (row < col).astype(BF16)
        head0 = lane < HEAD_DIM

        za = za_ref[...].astype(F32)
        g = g_ref[...].astype(F32)
        sig = _sigmoid(za)
        dqz_ref[1] = (g * o_ref[...].astype(F32) * (sig * (1.0 + za * (1.0 - sig)))).astype(BF16)
        do = (g * (za * sig)).astype(BF16)
        q = q_ref[...] * scale
        zero = jnp.zeros_like(q)
        qh = (jnp.where(head0, q, zero), jnp.where(head0, zero, q))
        doh = (jnp.where(head0, do, zero), jnp.where(head0, zero, do))
        tot_v = tot_ref[...]

        def block(j, state, masked):
            start = pl.multiple_of(j * t, t)
            kb = k_ref[pl.ds(start, t), :]
            vb = v_ref[pl.ds(start, t), :]
            dk_blk = jnp.zeros((t, PAIR), F32)
            dv_blk = jnp.zeros((t, PAIR), F32)
            new = []
            for h in range(2):
                rest, before, dq = state[h]
                z = _dot(qh[h], kb, NT)
                sp, ls = _softplus_parts(z)
                if masked:
                    sp = jnp.where(strict, sp, 0.0)
                after = _split_dot(sp, upper)
                rest = rest - (after[:, 0:1] + sp[:, 0:1])
                a = jnp.exp(ls - after - rest)
                if masked:
                    a = jnp.where(strict, a, 0.0)
                gg = a * _dot(doh[h], vb, NT)
                pre = _split_dot(gg, lower)
                dz = gg - jnp.exp(ls) * (gg + pre + before)
                if masked:
                    dz = jnp.where(strict, dz, 0.0)
                before = before + (pre[:, t - 1:t] + gg[:, t - 1:t])
                dzb = dz.astype(BF16)
                dq = dq + _dot(dzb, kb)
                dk_blk = dk_blk + _dot(dzb, qh[h], TN)
                dv_blk = dv_blk + _dot(a.astype(BF16), doh[h], TN)
                new.append((rest, before, dq))
            dk_acc[pl.ds(start, t), :] += dk_blk
            dv_acc[pl.ds(start, t), :] += dv_blk
            return tuple(new)

        zcol = jnp.zeros((t, 1), F32)
        zacc = jnp.zeros((t, PAIR), F32)
        init = ((tot_v[:, 0:1], zcol, zacc), (tot_v[:, HEAD_DIM:HEAD_DIM + 1], zcol, zacc))
        state = lax.fori_loop(0, i, lambda j, s: block(j, s, False), init)
        state = block(i, state, True)
        dq = jnp.where(head0, state[0][2], state[1][2]) * scale
        dqz_ref[0] = dq.astype(BF16)

        @pl.when(i == nb - 1)
        def _():
            dkv_ref[0] = dk_acc[...].astype(BF16)
            dkv_ref[1] = dv_acc[...].astype(BF16)

    def tile(j):
        return pl.BlockSpec((None, t, PAIR), lambda p, i, j=j: (j, i, p))

    def full(j):
        return pl.BlockSpec((None, seq, PAIR), lambda p, i, j=j: (j, 0, p))

    flat_tile = pl.BlockSpec((t, PAIR), lambda p, i: (i, p))
    return pl.pallas_call(
        body, name="attn_bwd", grid=(cw // PAIR, nb),
        in_specs=[tile(4), full(5), full(6), tile(7), flat_tile, flat_tile, flat_tile],
        out_specs=(pl.BlockSpec((2, t, PAIR), lambda p, i: (0, i, p)),
                   pl.BlockSpec((2, seq, PAIR), lambda p, i: (0, 0, p))),
        out_shape=(jax.ShapeDtypeStruct((2, seq, cw), BF16), jax.ShapeDtypeStruct((2, seq, cw), BF16)),
        scratch_shapes=[pltpu.VMEM((seq, PAIR), F32), pltpu.VMEM((seq, PAIR), F32)],
    )(proj, proj, proj, proj, o, tot, d_mix_attn)


def _out_ln(mix_conv, mix_attn, x, target, gain, bias, w_out, tm):
    seq, d_model = x.shape
    cw = mix_conv.shape[1]
    inv_d = 1.0 / d_model

    def body(mc_ref, ma_ref, x_ref, t_ref, gain_ref, bias_ref, w_ref, dr_ref, dmc_ref, dma_ref, gwo_ref, small_ref):
        @pl.when(pl.program_id(0) == 0)
        def _():
            gwo_ref[...] = jnp.zeros_like(gwo_ref)
            small_ref[...] = jnp.zeros_like(small_ref)

        mix = jnp.concatenate([mc_ref[...], ma_ref[...]], axis=1)
        w = w_ref[...]
        r = ALPHA * x_ref[...] + _dot(mix, w)
        mu = jnp.sum(r, axis=1, keepdims=True) * inv_d
        xc = r - mu
        var = jnp.sum(xc * xc, axis=1, keepdims=True) * inv_d
        rstd = lax.rsqrt(var + LN_EPS)
        xhat = xc * rstd
        gain_v = gain_ref[...]
        err = xhat * gain_v + bias_ref[...] - t_ref[...]
        row_loss = jnp.sum(err * err, axis=1, keepdims=True)
        loss = (0.5 * inv_d) * jnp.sum(row_loss, axis=0, keepdims=True)
        dy = err * inv_d
        small_ref[ROW_GAIN:ROW_GAIN + 1, :] += jnp.sum(dy * xhat, axis=0, keepdims=True)
        small_ref[ROW_BIAS:ROW_BIAS + 1, :] += jnp.sum(dy, axis=0, keepdims=True)
        small_ref[ROW_LOSS:ROW_LOSS + 1, :] += jnp.broadcast_to(loss, (1, d_model))
        dxhat = dy * gain_v
        m1 = jnp.sum(dxhat, axis=1, keepdims=True) * inv_d
        m2 = jnp.sum(dxhat * xhat, axis=1, keepdims=True) * inv_d
        dr = rstd * (dxhat - m1 - xhat * m2)
        dr_ref[...] = dr
        drb = dr.astype(BF16)
        dmix = _dot(drb, w, NT)
        dmc_ref[...] = dmix[:, :cw].astype(BF16)
        dma_ref[...] = dmix[:, cw:].astype(BF16)
        gwo_ref[...] += _dot(mix, drb, TN)

    def rows(width):
        return pl.BlockSpec((tm, width), lambda i: (i, 0))

    def whole(shape):
        return pl.BlockSpec(shape, lambda i: (0, 0))

    return pl.pallas_call(
        body, name="out_ln", grid=(seq // tm,),
        in_specs=[rows(cw), rows(cw), rows(d_model), rows(d_model), whole((1, d_model)), whole((1, d_model)),
                  whole((d_model, d_model))],
        out_specs=(rows(d_model), rows(cw), rows(cw), whole((d_model, d_model)), whole((SUBLANES, d_model))),
        out_shape=(jax.ShapeDtypeStruct((seq, d_model), F32), jax.ShapeDtypeStruct((seq, cw), BF16),
                   jax.ShapeDtypeStruct((seq, cw), BF16), jax.ShapeDtypeStruct((d_model, d_model), F32),
                   jax.ShapeDtypeStruct((SUBLANES, d_model), F32)),
        compiler_params=_params(48),
    )(mix_conv, mix_attn, x, target, gain, bias, w_out)


_DP_OF_GROUP = ((0, 0), (0, 1), (0, 2), (0, 3), (1, 0), (2, 0), (2, 1), (1, 1))


def _grad_x(dr, dp_conv, dp_qz, dp_kv, win_all, tm):
    seq, d_model = dr.shape
    nch, _, cw = win_all.shape

    def body(dr_ref, dc_ref, dqz_ref, dkv_ref, w_ref, o_ref):
        parts = (dc_ref, dqz_ref, dkv_ref)
        acc = ALPHA * dr_ref[...]
        for j in range(nch):
            arr, idx = _DP_OF_GROUP[j]
            acc = acc + _dot(parts[arr][idx], w_ref[j], NT)
        o_ref[...] = acc

    def part(n):
        return pl.BlockSpec((n, tm, cw), lambda i: (0, i, 0))

    return pl.pallas_call(
        body, name="grad_x", grid=(seq // tm,),
        in_specs=[pl.BlockSpec((tm, d_model), lambda i: (i, 0)), part(4), part(2), part(2),
                  pl.BlockSpec((nch, d_model, cw), lambda i: (0, 0, 0))],
        out_specs=pl.BlockSpec((tm, d_model), lambda i: (i, 0)),
        out_shape=jax.ShapeDtypeStruct((seq, d_model), F32),
        compiler_params=_params(48),
    )(dr, dp_conv, dp_qz, dp_kv, win_all)


def _grad_w_in(x, dp, tk, name):
    seq, d_model = x.shape
    n, _, cw = dp.shape

    def body(x_ref, d_ref, o_ref):
        @pl.when(pl.program_id(1) == 0)
        def _():
            o_ref[...] = jnp.zeros_like(o_ref)

        o_ref[...] += _dot(x_ref[...].astype(BF16), d_ref[...], TN)

    return pl.pallas_call(
        body, name=name, grid=(n, seq // tk),
        in_specs=[pl.BlockSpec((tk, d_model), lambda j, k: (k, 0)),
                  pl.BlockSpec((None, tk, cw), lambda j, k: (j, k, 0))],
        out_specs=pl.BlockSpec((None, d_model, cw), lambda j, k: (j, 0, 0)),
        out_shape=jax.ShapeDtypeStruct((n, d_model, cw), F32),
    )(x, dp)


def _reduce_update(gw_parts, gwo, small, w_in, m_in, v_in, w_out, m_out, v_out, row_chunk):
    d_model, cw = w_in.shape
    rows_out = w_out.shape[0]

    def body(ga, gb, gc, gwo_ref, small_ref, w_in_ref, m_in_ref, v_in_ref, w_out_ref, m_out_ref, v_out_ref,
             g_in_o, d_in_o, nm_in_o, nv_in_o, g_out_o, d_out_o, nm_out_o, nv_out_o, small_o,
             own_in, got_in, send_in, recv_in, own_out, got_out, send_out, recv_out, small_all,
             loc_sems, s1_send, s1_recv, s2_send, s2_recv, sm_send, sm_recv):
        x, y, c = _mesh_pos()
        me = 4 * x + 2 * y + c
        sibling = (x, y, 1 - c)
        chips = [(1 - x, y), (x, 1 - y), (1 - x, 1 - y)]
        parts = (ga, gb, gc)

        def block_in(k):
            arr, idx = _DP_OF_GROUP[k]
            return parts[arr].at[idx]

        def block_out(k):
            return gwo_ref.at[pl.ds(k * rows_out, rows_out), :]

        arrays = ((block_in, own_in, got_in, send_in, recv_in), (block_out, own_out, got_out, send_out, recv_out))

        small_all[me] = small_ref[...]
        for d in range(N_DEV):
            @pl.when(d != me)
            def _(d=d):
                pltpu.make_async_remote_copy(
                    src_ref=small_ref, dst_ref=small_all.at[me], send_sem=sm_send.at[d], recv_sem=sm_recv.at[me],
                    device_id=(d // 4, (d // 2) % 2, d % 2), device_id_type=MESH).start()

        for a, (block, own, got, _, _) in enumerate(arrays):
            for k in range(N_DEV):
                s = k // 2

                @pl.when(k % 2 != c)
                def _(a=a, k=k, s=s, block=block, got=got):
                    pltpu.make_async_remote_copy(
                        src_ref=block(k), dst_ref=got.at[s], send_sem=s1_send.at[a, s], recv_sem=s1_recv.at[a, s],
                        device_id=sibling, device_id_type=MESH).start()

                @pl.when(k % 2 == c)
                def _(a=a, k=k, s=s, block=block, own=own):
                    pltpu.make_async_copy(block(k), own.at[s], loc_sems.at[a, s]).start()

        def level1_wait(a, s):
            _, own, got, _, _ = arrays[a]
            pltpu.make_async_copy(own.at[s], own.at[s], loc_sems.at[a, s]).wait()
            pltpu.make_async_remote_copy(
                src_ref=got.at[s], dst_ref=got.at[s], send_sem=s1_send.at[a, s], recv_sem=s1_recv.at[a, s],
                device_id=sibling, device_id_type=MESH).wait()

        def level2_copy(a, j):
            _, _, _, send, recv = arrays[a]
            return pltpu.make_async_remote_copy(
                src_ref=send.at[j], dst_ref=recv.at[j], send_sem=s2_send.at[a, j], recv_sem=s2_recv.at[a, j],
                device_id=(*chips[j], c), device_id_type=MESH)

        def chip_sum(a, s, n_rows, fn):
            _, own, got, _, _ = arrays[a]
            step = min(row_chunk, n_rows)

            def rows_body(r, _):
                rows = pl.ds(pl.multiple_of(r * step, step), step)
                fn(rows, own[s, rows, :] + got[s, rows, :])
                return 0

            lax.fori_loop(0, n_rows // step, rows_body, 0)

        for a in range(2):
            for s in range(4):
                level1_wait(a, s)
        for a, n_rows in ((0, d_model), (1, rows_out)):
            send = arrays[a][3]
            for j, chip in enumerate(chips):
                s = 2 * chip[0] + chip[1]

                def to_send(rows, val, send=send, j=j):
                    send[j, rows, :] = val.astype(BF16)

                chip_sum(a, s, n_rows, to_send)
                level2_copy(a, j).start()

        for d in range(N_DEV):
            @pl.when(d != me)
            def _(d=d):
                pltpu.make_async_remote_copy(
                    src_ref=small_ref, dst_ref=small_all.at[d], send_sem=sm_send.at[d], recv_sem=sm_recv.at[d],
                    device_id=(d // 4, (d // 2) % 2, d % 2), device_id_type=MESH).wait()
        total = small_all[0]
        for d in range(1, N_DEV):
            total = total + small_all[d]
        small_o[...] = total

        outs = ((g_in_o, d_in_o, nm_in_o, nv_in_o, w_in_ref, m_in_ref, v_in_ref),
                (g_out_o, d_out_o, nm_out_o, nv_out_o, w_out_ref, m_out_ref, v_out_ref))
        mine = 2 * x + y
        for a, n_rows in ((0, d_model), (1, rows_out)):
            recv = arrays[a][4]
            for j in range(3):
                level2_copy(a, j).wait()
            g_o, d_o, nm_o, nv_o, w_r, m_r, v_r = outs[a]

            def update(rows, val, recv=recv, g_o=g_o, d_o=d_o, nm_o=nm_o, nv_o=nv_o, w_r=w_r, m_r=m_r, v_r=v_r):
                g = val
                for j in range(3):
                    g = g + recv[j, rows, :].astype(F32)
                delta, nm, nv = _adamw(w_r[rows, :], g, m_r[rows, :], v_r[rows, :])
                g_o[rows, :] = g
                d_o[rows, :] = delta
                nm_o[rows, :] = nm
                nv_o[rows, :] = nv

            chip_sum(a, mine, n_rows, update)

    vmem = pl.BlockSpec(memory_space=pltpu.VMEM)
    hbm = pl.BlockSpec(memory_space=pl.ANY)
    shard_in = jax.ShapeDtypeStruct((d_model, cw), F32)
    shard_out = jax.ShapeDtypeStruct((rows_out, d_model), F32)
    return pl.pallas_call(
        body, name="reduce_update",
        in_specs=[hbm, hbm, hbm, hbm] + [vmem] * 7,
        out_specs=(vmem,) * 9,
        out_shape=(shard_in,) * 4 + (shard_out,) * 4 + (jax.ShapeDtypeStruct(small.shape, F32),),
        scratch_shapes=[
            pltpu.VMEM((4, d_model, cw), F32), pltpu.VMEM((4, d_model, cw), F32),
            pltpu.VMEM((3, d_model, cw), BF16), pltpu.VMEM((3, d_model, cw), BF16),
            pltpu.VMEM((4, rows_out, d_model), F32), pltpu.VMEM((4, rows_out, d_model), F32),
            pltpu.VMEM((3, rows_out, d_model), BF16), pltpu.VMEM((3, rows_out, d_model), BF16),
            pltpu.VMEM((N_DEV,) + small.shape, F32),
            pltpu.SemaphoreType.DMA((2, 4)), pltpu.SemaphoreType.DMA((2, 4)), pltpu.SemaphoreType.DMA((2, 4)),
            pltpu.SemaphoreType.DMA((2, 3)), pltpu.SemaphoreType.DMA((2, 3)),
            pltpu.SemaphoreType.DMA((N_DEV,)), pltpu.SemaphoreType.DMA((N_DEV,)),
        ],
        compiler_params=_params(56),
    )(*gw_parts, gwo, small, w_in, m_in, v_in, w_out, m_out, v_out)


def _small_update(grads, weights, ms, vs):
    n = len(grads)

    def body(*refs):
        g_refs, w_refs, m_refs, v_refs = (refs[i * n:(i + 1) * n] for i in range(4))
        outs = refs[4 * n:]
        for i in range(n):
            delta, nm, nv = _adamw(w_refs[i][...], g_refs[i][...], m_refs[i][...], v_refs[i][...])
            outs[3 * i][...] = delta
            outs[3 * i + 1][...] = nm
            outs[3 * i + 2][...] = nv

    vmem = pl.BlockSpec(memory_space=pltpu.VMEM)
    out_shape = []
    for w in weights:
        out_shape += [jax.ShapeDtypeStruct(w.shape, F32)] * 3
    return pl.pallas_call(
        body, name="small_update", in_specs=[vmem] * (4 * n), out_specs=(vmem,) * (3 * n), out_shape=tuple(out_shape),
    )(*grads, *weights, *ms, *vs)


def _tile_sizes(seq):
    return dict(tm=min(512, seq), t_ln=min(256, seq), t_attn=min(128, seq), rc=min(256, seq))


def kernel(x, w_in, conv_w, w_out, ln_gain, ln_bias, loss_target, m_w_in, m_conv_w, m_w_out, m_ln_gain, m_ln_bias,
           v_w_in, v_conv_w, v_w_out, v_ln_gain, v_ln_bias):
    assert x.shape[0] == 1 and w_in.shape[0] == 1, "one sequence per device, depth 1"
    _, seq, d_model = x.shape
    cw = w_in.shape[2]
    conv_k, conv_cols = conv_w.shape[1], conv_w.shape[2]
    rows_out = w_out.shape[1]
    assert cw == d_model // 2 and cw % PAIR == 0 and conv_cols * N_DEV == cw and rows_out * N_DEV == d_model
    ts = _tile_sizes(seq)

    x2 = x.reshape(seq, d_model)
    target = loss_target.reshape(seq, d_model)
    me = 4 * lax.axis_index("x") + 2 * lax.axis_index("y") + lax.axis_index("c")

    conv_pad = jnp.pad(conv_w[0], ((0, SUBLANES - conv_k), (0, PAIR - conv_cols)))
    win_all, wout_all, conv_all = _gather_weights(w_in[0], w_out[0], conv_pad)
    w_out_full = wout_all.reshape(d_model, d_model)
    conv_full = conv_all[:, :conv_k, :conv_cols].transpose(1, 0, 2).reshape(conv_k, cw)
    conv_full = jnp.pad(conv_full, ((0, SUBLANES - conv_k), (0, 0)))

    proj = _proj(x2, win_all, ts["tm"])
    mix_conv = _conv_fwd(proj, conv_full, ts["rc"])
    o, mix_attn, tot = _attn_fwd(proj, ts["t_attn"])
    dr, d_mix_conv, d_mix_attn, gwo, small = _out_ln(mix_conv, mix_attn, x2, target, ln_gain, ln_bias, w_out_full,
                                                     ts["t_ln"])
    dp_conv, d_taps = _conv_bwd(proj, conv_full, d_mix_conv, ts["rc"])
    dp_qz, dp_kv = _attn_bwd(proj, o, tot, d_mix_attn, ts["t_attn"])
    grad_x = _grad_x(dr, dp_conv, dp_qz, dp_kv, win_all, ts["tm"])
    gw_parts = (_grad_w_in(x2, dp_conv, ts["tm"], "grad_w_in_conv"), _grad_w_in(x2, dp_qz, ts["tm"], "grad_w_in_qz"),
                _grad_w_in(x2, dp_kv, ts["tm"], "grad_w_in_kv"))

    small = small.at[ROW_CONV:ROW_CONV + conv_k, :cw].set(d_taps[:conv_k])
    (g_in, d_in, nm_in, nv_in, g_out, d_out, nm_out, nv_out, small_sum) = _reduce_update(
        gw_parts, gwo, small, w_in[0], m_w_in[0], v_w_in[0], w_out[0], m_w_out[0], v_w_out[0], 128)

    loss = small_sum[ROW_LOSS, 0]
    g_gain = small_sum[ROW_GAIN:ROW_GAIN + 1]
    g_bias = small_sum[ROW_BIAS:ROW_BIAS + 1]
    g_conv = lax.dynamic_slice(small_sum, (ROW_CONV, me * conv_cols), (conv_k, conv_cols))
    upd = _small_update((g_conv, g_gain, g_bias), (conv_w[0], ln_gain, ln_bias),
                        (m_conv_w[0], m_ln_gain, m_ln_bias), (v_conv_w[0], v_ln_gain, v_ln_bias))
    d_conv, nm_conv, nv_conv, d_gain, nm_gain, nv_gain, d_bias, nm_bias, nv_bias = upd

    lead = lambda a: a[None]
    return (loss, grad_x.reshape(1, seq, d_model), lead(g_in), lead(g_conv), lead(g_out), g_gain, g_bias,
            lead(d_in), lead(d_conv), lead(d_out), d_gain, d_bias,
            lead(nm_in), lead(nm_conv), lead(nm_out), nm_gain, nm_bias,
            lead(nv_in), lead(nv_conv), lead(nv_out), nv_gain, nv_bias)
```

```python
import functools

import jax
import jax.numpy as jnp
from jax import lax
from jax.experimental import pallas as pl
from jax.experimental.pallas import tpu as pltpu

F32 = jnp.float32
BF16 = jnp.bfloat16
MESH = pl.DeviceIdType.MESH

N_DEV = 8
HEAD_DIM = 64
PAIR = 128
SUBLANES = 8
LN_EPS = 1e-5
ALPHA = 2.0 ** 0.25
ADAM_LR, ADAM_B1, ADAM_B2, ADAM_EPS, ADAM_WD, ADAM_STEP = 0.001, 0.9, 0.999, 1e-08, 0.01, 10

ROW_GAIN, ROW_BIAS, ROW_CONV, ROW_LOSS = 0, 1, 2, 5

NT = (((1,), (1,)), ((), ()))
TN = (((0,), (0,)), ((), ()))


V7X_VMEM_BYTES = 64 * 1024 * 1024


def _params(vmem_mib):
    assert vmem_mib * 1024 * 1024 < V7X_VMEM_BYTES
    return pltpu.CompilerParams(vmem_limit_bytes=vmem_mib * 1024 * 1024)


def _dot(a, b, dims=None):
    if dims is None:
        return jnp.dot(a, b, preferred_element_type=F32)
    return lax.dot_general(a, b, dims, preferred_element_type=F32)


def _sigmoid(z):
    return 1.0 / (1.0 + jnp.exp(-z))


def _mesh_pos():
    return lax.axis_index("x"), lax.axis_index("y"), lax.axis_index("c")


def _adamw(w, g, m, v):
    nm = ADAM_B1 * m + (1.0 - ADAM_B1) * g
    nv = ADAM_B2 * v + (1.0 - ADAM_B2) * (g * g)
    m_hat = nm * (1.0 / (1.0 - ADAM_B1 ** ADAM_STEP))
    v_hat = nv * (1.0 / (1.0 - ADAM_B2 ** ADAM_STEP))
    delta = -ADAM_LR * (m_hat / (jnp.sqrt(v_hat) + ADAM_EPS) + ADAM_WD * w)
    return delta, nm, nv


def _gather_weights(w_in_s, w_out_s, conv_s):
    d_model, cw = w_in_s.shape
    rows_out = w_out_s.shape[0]
    n_arr = 3

    def body(win_ref, wout_ref, conv_ref, win_all, wout_all, conv_all, send_sems, recv_sems):
        x, y, c = _mesh_pos()
        me = (x, y, c)
        sibling = (x, y, 1 - c)
        chips = [(1 - x, y), (x, 1 - y), (1 - x, 1 - y)]
        bufs = (win_all, wout_all, conv_all)

        def slot(pos):
            return 4 * pos[0] + 2 * pos[1] + pos[2]

        win_all[slot(me)] = win_ref[...].astype(BF16)
        wout_all[slot(me)] = wout_ref[...].astype(BF16)
        conv_all[slot(me)] = conv_ref[...]

        def copy(a, k, block, to):
            ref = bufs[a].at[slot(block)]
            return pltpu.make_async_remote_copy(
                src_ref=ref, dst_ref=ref, send_sem=send_sems.at[a, k], recv_sem=recv_sems.at[a, k],
                device_id=to, device_id_type=MESH)

        first, passed = [], []
        for a in range(n_arr):
            first.append(copy(a, 0, me, sibling))
            first += [copy(a, 1 + j, me, (*chip, c)) for j, chip in enumerate(chips)]
        for cp in first:
            cp.start()
        for j, chip in enumerate(chips):
            for a in range(n_arr):
                copy(a, 1 + j, (*chip, c), me).wait_recv()
                cp = copy(a, 4 + j, (*chip, c), sibling)
                cp.start()
                passed.append(cp)
        for a in range(n_arr):
            copy(a, 0, sibling, me).wait_recv()
            for j, chip in enumerate(chips):
                copy(a, 4 + j, (*chip, 1 - c), me).wait_recv()
        for cp in first + passed:
            cp.wait_send()

    vmem = pl.BlockSpec(memory_space=pltpu.VMEM)
    return pl.pallas_call(
        body, name="gather_weights",
        out_shape=(jax.ShapeDtypeStruct((N_DEV, d_model, cw), BF16),
                   jax.ShapeDtypeStruct((N_DEV, rows_out, d_model), BF16),
                   jax.ShapeDtypeStruct((N_DEV,) + conv_s.shape, F32)),
        in_specs=[vmem, vmem, vmem], out_specs=(vmem, vmem, vmem),
        scratch_shapes=[pltpu.SemaphoreType.DMA((n_arr, 7)), pltpu.SemaphoreType.DMA((n_arr, 7))],
    )(w_in_s, w_out_s, conv_s)


def _proj(x, win_all, tm):
    seq, d_model = x.shape
    nch, _, cw = win_all.shape

    def body(x_ref, w_ref, o_ref):
        xb = x_ref[...].astype(BF16)
        for j in range(nch):
            o_ref[j] = _dot(xb, w_ref[j]).astype(BF16)

    return pl.pallas_call(
        body, name="proj", grid=(seq // tm,),
        in_specs=[pl.BlockSpec((tm, d_model), lambda i: (i, 0)),
                  pl.BlockSpec((nch, d_model, cw), lambda i: (0, 0, 0))],
        out_specs=pl.BlockSpec((nch, tm, cw), lambda i: (0, i, 0)),
        out_shape=jax.ShapeDtypeStruct((nch, seq, cw), BF16),
        compiler_params=_params(48),
    )(x, win_all)


def _conv_taps(ext, w_ref, rc):
    u0 = ext[SUBLANES:SUBLANES + rc]
    u1 = pltpu.roll(ext, 1, 0)[SUBLANES:SUBLANES + rc]
    u2 = pltpu.roll(ext, 2, 0)[SUBLANES:SUBLANES + rc]
    return w_ref[2:3, :] * u0 + w_ref[1:2, :] * u1 + w_ref[0:1, :] * u2, u0, u1, u2


def _conv_fwd(proj, conv_full, rc):
    _, seq, cw = proj.shape

    def body(b_ref, c_ref, h_ref, z_ref, w_ref, o_ref, u_scr):
        u_scr[0:SUBLANES, :] = jnp.zeros((SUBLANES, PAIR), F32)

        def fill(r, _):
            base = pl.multiple_of(r * rc, rc)
            rows = pl.ds(base, rc)
            u_scr[pl.ds(base + SUBLANES, rc), :] = c_ref[rows, :].astype(F32) * h_ref[rows, :].astype(F32)
            return 0

        lax.fori_loop(0, seq // rc, fill, 0)

        def out(r, _):
            base = pl.multiple_of(r * rc, rc)
            rows = pl.ds(base, rc)
            ext = u_scr[pl.ds(base, rc + SUBLANES), :]
            y, _, _, _ = _conv_taps(ext, w_ref, rc)
            z = z_ref[rows, :].astype(F32)
            o_ref[rows, :] = (z * _sigmoid(z) * b_ref[rows, :].astype(F32) * y).astype(BF16)
            return 0

        lax.fori_loop(0, seq // rc, out, 0)

    def chunk(j):
        return pl.BlockSpec((None, seq, PAIR), lambda cb, j=j: (j, 0, cb))

    return pl.pallas_call(
        body, name="conv_fwd", grid=(cw // PAIR,),
        in_specs=[chunk(0), chunk(1), chunk(2), chunk(3), pl.BlockSpec((SUBLANES, PAIR), lambda cb: (0, cb))],
        out_specs=pl.BlockSpec((seq, PAIR), lambda cb: (0, cb)),
        out_shape=jax.ShapeDtypeStruct((seq, cw), BF16),
        scratch_shapes=[pltpu.VMEM((seq + SUBLANES, PAIR), F32)],
    )(proj, proj, proj, proj, conv_full)


def _conv_bwd(proj, conv_full, d_mix_conv, rc):
    _, seq, cw = proj.shape

    def body(b_ref, c_ref, h_ref, z_ref, w_ref, g_ref, dp_ref, dw_ref, u_scr, dy_scr):
        u_scr[0:SUBLANES, :] = jnp.zeros((SUBLANES, PAIR), F32)
        dy_scr[seq:seq + SUBLANES, :] = jnp.zeros((SUBLANES, PAIR), F32)

        def fill(r, _):
            base = pl.multiple_of(r * rc, rc)
            rows = pl.ds(base, rc)
            u_scr[pl.ds(base + SUBLANES, rc), :] = c_ref[rows, :].astype(F32) * h_ref[rows, :].astype(F32)
            return 0

        lax.fori_loop(0, seq // rc, fill, 0)

        def gate(r, acc):
            base = pl.multiple_of(r * rc, rc)
            rows = pl.ds(base, rc)
            ext = u_scr[pl.ds(base, rc + SUBLANES), :]
            y, u0, u1, u2 = _conv_taps(ext, w_ref, rc)
            z = z_ref[rows, :].astype(F32)
            b = b_ref[rows, :].astype(F32)
            g = g_ref[rows, :].astype(F32)
            sig = _sigmoid(z)
            dp_ref[3, rows, :] = (g * b * y * (sig * (1.0 + z * (1.0 - sig)))).astype(BF16)
            gs = g * (z * sig)
            dp_ref[0, rows, :] = (gs * y).astype(BF16)
            dy = gs * b
            dy_scr[rows, :] = dy
            a0, a1, a2 = acc
            return (a0 + jnp.sum(dy * u2, axis=0, keepdims=True),
                    a1 + jnp.sum(dy * u1, axis=0, keepdims=True),
                    a2 + jnp.sum(dy * u0, axis=0, keepdims=True))

        zero = jnp.zeros((1, PAIR), F32)
        a0, a1, a2 = lax.fori_loop(0, seq // rc, gate, (zero, zero, zero))
        dw_ref[...] = jnp.zeros((SUBLANES, PAIR), F32)
        dw_ref[0:1, :] = a0
        dw_ref[1:2, :] = a1
        dw_ref[2:3, :] = a2

        def back(r, _):
            base = pl.multiple_of(r * rc, rc)
            rows = pl.ds(base, rc)
            ext = dy_scr[pl.ds(base, rc + SUBLANES), :]
            n = rc + SUBLANES
            d0 = ext[0:rc]
            d1 = pltpu.roll(ext, n - 1, 0)[0:rc]
            d2 = pltpu.roll(ext, n - 2, 0)[0:rc]
            du = w_ref[2:3, :] * d0 + w_ref[1:2, :] * d1 + w_ref[0:1, :] * d2
            dp_ref[1, rows, :] = (du * h_ref[rows, :].astype(F32)).astype(BF16)
            dp_ref[2, rows, :] = (du * c_ref[rows, :].astype(F32)).astype(BF16)
            return 0

        lax.fori_loop(0, seq // rc, back, 0)

    def chunk(j):
        return pl.BlockSpec((None, seq, PAIR), lambda cb, j=j: (j, 0, cb))

    return pl.pallas_call(
        body, name="conv_bwd", grid=(cw // PAIR,),
        in_specs=[chunk(0), chunk(1), chunk(2), chunk(3), pl.BlockSpec((SUBLANES, PAIR), lambda cb: (0, cb)),
                  pl.BlockSpec((seq, PAIR), lambda cb: (0, cb))],
        out_specs=(pl.BlockSpec((4, seq, PAIR), lambda cb: (0, 0, cb)),
                   pl.BlockSpec((SUBLANES, PAIR), lambda cb: (0, cb))),
        out_shape=(jax.ShapeDtypeStruct((4, seq, cw), BF16), jax.ShapeDtypeStruct((SUBLANES, cw), F32)),
        scratch_shapes=[pltpu.VMEM((seq + SUBLANES, PAIR), F32), pltpu.VMEM((seq + SUBLANES, PAIR), F32)],
    )(proj, proj, proj, proj, conv_full, d_mix_conv)


def _split_dot(a, tri):
    hi = a.astype(BF16)
    lo = (a - hi.astype(F32)).astype(BF16)
    return _dot(hi, tri) + _dot(lo, tri)


def _softplus_parts(z):
    l = jnp.log1p(jnp.exp(-jnp.abs(z)))
    return jnp.maximum(z, 0.0) + l, jnp.minimum(z, 0.0) - l


SKIP_CARRY = 104.0
LANE_TOT0, LANE_TOT1, LANE_FIRST = 0, 1, 2


def _pair_consts(t):
    lane = lax.broadcasted_iota(jnp.int32, (t, PAIR), 1)
    row = lax.broadcasted_iota(jnp.int32, (2 * t, 2 * t), 0)
    col = lax.broadcasted_iota(jnp.int32, (2 * t, 2 * t), 1)
    same = (row < t) == (col < t)
    upper = jnp.logical_and(same, row > col).astype(BF16)
    lower = jnp.logical_and(same, row < col).astype(BF16)
    qrow = lax.broadcasted_iota(jnp.int32, (t, 2 * t), 0)
    kcol = lax.broadcasted_iota(jnp.int32, (t, 2 * t), 1)
    strict = jnp.where(kcol < t, kcol, kcol - t) < qrow
    return lane, lane < HEAD_DIM, strict, upper, lower


def _by_head(x, head0):
    zero = jnp.zeros_like(x)
    return jnp.concatenate([jnp.where(head0, x, zero), jnp.where(head0, zero, x)], axis=0)


def _attn_fwd(proj, t, pp):
    _, seq, cw = proj.shape
    scale = HEAD_DIM ** -0.5
    width = pp * PAIR

    def body(q_ref, k_ref, v_ref, za_ref, o_ref, mix_ref, tot_ref):
        i = pl.program_id(1)
        lane, head0, strict, upper, _ = _pair_consts(t)
        q = q_ref[...] * scale

        def block(j, state, masked):
            start = pl.multiple_of(j * t, t)
            kb = k_ref[pl.ds(start, t), :]
            vb = v_ref[pl.ds(start, t), :]
            new = []
            for p in range(pp):
                cols = slice(p * PAIR, (p + 1) * PAIR)
                (c0, c1), acc = state[p]
                z = _dot(q[:, cols], _by_head(kb[:, cols], head0), NT)
                sp, ls = _softplus_parts(z)
                if masked:
                    sp = jnp.where(strict, sp, 0.0)
                after = _split_dot(sp, upper)
                x = ls - after
                w = jnp.exp(jnp.concatenate([x[:, :t] - c0, x[:, t:] - c1], axis=1))
                if masked:
                    w = jnp.where(strict, w, 0.0)
                acc = acc + _dot(w.astype(BF16), _by_head(vb[:, cols], head0))
                c0 = c0 + (after[:, 0:1] + sp[:, 0:1])
                c1 = c1 + (after[:, t:t + 1] + sp[:, t:t + 1])
                new.append(((c0, c1), acc))
            return tuple(new)

        def unfinished(state):
            m = state[0][0][0]
            for p in range(pp):
                m = jnp.minimum(m, jnp.minimum(state[p][0][0], state[p][0][1]))
            return jnp.min(m) < SKIP_CARRY

        def step(js):
            state = block(js[0], js[1], False)
            return js[0] - 1, state, unfinished(state)

        zcol = jnp.zeros((t, 1), F32)
        init = tuple(((zcol, zcol), jnp.zeros((t, PAIR), F32)) for _ in range(pp))
        state = block(i, init, True)
        j_end, state, _ = lax.while_loop(
            lambda js: jnp.logical_and(js[0] >= 0, js[2]), step, (i - 1, state, unfinished(state)))
        first = (j_end + 1).astype(F32)
        za = za_ref[...].astype(F32)
        for p in range(pp):
            (c0, c1), acc = state[p]
            cols = slice(p * PAIR, (p + 1) * PAIR)
            zp = za[:, cols]
            o_ref[:, cols] = acc.astype(BF16)
            mix_ref[:, cols] = (zp * _sigmoid(zp) * acc).astype(BF16)
            tot_ref[:, cols] = jnp.where(lane == LANE_TOT0, c0, jnp.where(lane == LANE_TOT1, c1, first))

    def tile(j):
        return pl.BlockSpec((None, t, width), lambda g, i, j=j: (j, i, g))

    def full(j):
        return pl.BlockSpec((None, seq, width), lambda g, i, j=j: (j, 0, g))

    out_tile = pl.BlockSpec((t, width), lambda g, i: (i, g))
    return pl.pallas_call(
        body, name="attn_fwd", grid=(cw // width, seq // t),
        in_specs=[tile(4), full(5), full(6), tile(7)],
        out_specs=(out_tile, out_tile, out_tile),
        out_shape=(jax.ShapeDtypeStruct((seq, cw), BF16), jax.ShapeDtypeStruct((seq, cw), BF16),
                   jax.ShapeDtypeStruct((seq, cw), F32)),
    )(proj, proj, proj, proj)


def _attn_bwd(proj, o, tot, d_mix_attn, t, pp):
    _, seq, cw = proj.shape
    nb = seq // t
    scale = HEAD_DIM ** -0.5
    width = pp * PAIR

    def body(q_ref, k_ref, v_ref, za_ref, o_ref, tot_ref, g_ref, dqz_ref, dkv_ref, dk_acc, dv_acc):
        i = pl.program_id(1)

        @pl.when(i == 0)
        def _():
            dk_acc[...] = jnp.zeros_like(dk_acc)
            dv_acc[...] = jnp.zeros_like(dv_acc)

        _, head0, strict, upper, lower = _pair_consts(t)
        za = za_ref[...].astype(F32)
        g = g_ref[...].astype(F32)
        sig = _sigmoid(za)
        dqz_ref[1] = (g * o_ref[...].astype(F32) * (sig * (1.0 + za * (1.0 - sig)))).astype(BF16)
        do = (g * (za * sig)).astype(BF16)
        q = q_ref[...] * scale
        tot_v = tot_ref[...]
        q2, do2, init = [], [], []
        zcol = jnp.zeros((t, 1), F32)
        for p in range(pp):
            cols = slice(p * PAIR, (p + 1) * PAIR)
            q2.append(_by_head(q[:, cols], head0))
            do2.append(_by_head(do[:, cols], head0))
            tp = tot_v[:, cols]
            init.append(((tp[:, LANE_TOT0:LANE_TOT0 + 1], tp[:, LANE_TOT1:LANE_TOT1 + 1]), (zcol, zcol),
                         jnp.zeros((t, PAIR), F32)))
        first = jnp.clip(tot_v[0:1, LANE_FIRST:LANE_FIRST + 1], 0.0, i.astype(F32)).astype(jnp.int32)[0, 0]

        def block(j, state, masked):
            start = pl.multiple_of(j * t, t)
            kb = k_ref[pl.ds(start, t), :]
            vb = v_ref[pl.ds(start, t), :]
            new = []
            for p in range(pp):
                cols = slice(p * PAIR, (p + 1) * PAIR)
                k2 = _by_head(kb[:, cols], head0)
                v2 = _by_head(vb[:, cols], head0)
                (r0, r1), (b0, b1), dq = state[p]
                z = _dot(q[:, cols], k2, NT)
                sp, ls = _softplus_parts(z)
                if masked:
                    sp = jnp.where(strict, sp, 0.0)
                after = _split_dot(sp, upper)
                r0 = r0 - (after[:, 0:1] + sp[:, 0:1])
                r1 = r1 - (after[:, t:t + 1] + sp[:, t:t + 1])
                x = ls - after
                a = jnp.exp(jnp.concatenate([x[:, :t] - r0, x[:, t:] - r1], axis=1))
                if masked:
                    a = jnp.where(strict, a, 0.0)
                gg = a * _dot(do[:, cols], v2, NT)
                pre = _split_dot(gg, lower)
                y = gg + pre
                dz = gg - jnp.exp(ls) * jnp.concatenate([y[:, :t] + b0, y[:, t:] + b1], axis=1)
                if masked:
                    dz = jnp.where(strict, dz, 0.0)
                b0 = b0 + y[:, t - 1:t]
                b1 = b1 + y[:, 2 * t - 1:2 * t]
                dzb = dz.astype(BF16)
                ab = a.astype(BF16)
                dq = dq + _dot(dzb, k2)
                dz2 = jnp.concatenate([dzb[:, :t], dzb[:, t:]], axis=0)
                a2 = jnp.concatenate([ab[:, :t], ab[:, t:]], axis=0)
                dk_acc[pl.ds(start, t), cols] += _dot(dz2, q2[p], TN)
                dv_acc[pl.ds(start, t), cols] += _dot(a2, do2[p], TN)
                new.append(((r0, r1), (b0, b1), dq))
            return tuple(new)

        state = lax.fori_loop(first, i, lambda j, s: block(j, s, False), tuple(init))
        state = block(i, state, True)
        for p in range(pp):
            dqz_ref[0, :, p * PAIR:(p + 1) * PAIR] = (state[p][2] * scale).astype(BF16)

        @pl.when(i == nb - 1)
        def _():
            dkv_ref[0] = dk_acc[...].astype(BF16)
            dkv_ref[1] = dv_acc[...].astype(BF16)

    def tile(j):
        return pl.BlockSpec((None, t, width), lambda g, i, j=j: (j, i, g))

    def full(j):
        return pl.BlockSpec((None, seq, width), lambda g, i, j=j: (j, 0, g))

    flat_tile = pl.BlockSpec((t, width), lambda g, i: (i, g))
    return pl.pallas_call(
        body, name="attn_bwd", grid=(cw // width, nb),
        in_specs=[tile(4), full(5), full(6), tile(7), flat_tile, flat_tile, flat_tile],
        out_specs=(pl.BlockSpec((2, t, width), lambda g, i: (0, i, g)),
                   pl.BlockSpec((2, seq, width), lambda g, i: (0, 0, g))),
        out_shape=(jax.ShapeDtypeStruct((2, seq, cw), BF16), jax.ShapeDtypeStruct((2, seq, cw), BF16)),
        scratch_shapes=[pltpu.VMEM((seq, width), F32), pltpu.VMEM((seq, width), F32)],
        compiler_params=_params(48),
    )(proj, proj, proj, proj, o, tot, d_mix_attn)


def _out_ln(mix_conv, mix_attn, x, target, gain, bias, w_out, tm):
    seq, d_model = x.shape
    cw = mix_conv.shape[1]
    inv_d = 1.0 / d_model

    def body(mc_ref, ma_ref, x_ref, t_ref, gain_ref, bias_ref, w_ref, dr_ref, dmc_ref, dma_ref, gwo_ref, small_ref):
        @pl.when(pl.program_id(0) == 0)
        def _():
            gwo_ref[...] = jnp.zeros_like(gwo_ref)
            small_ref[...] = jnp.zeros_like(small_ref)

        mix = jnp.concatenate([mc_ref[...], ma_ref[...]], axis=1)
        w = w_ref[...]
        r = ALPHA * x_ref[...] + _dot(mix, w)
        mu = jnp.sum(r, axis=1, keepdims=True) * inv_d
        xc = r - mu
        var = jnp.sum(xc * xc, axis=1, keepdims=True) * inv_d
        rstd = lax.rsqrt(var + LN_EPS)
        xhat = xc * rstd
        gain_v = gain_ref[...]
        err = xhat * gain_v + bias_ref[...] - t_ref[...]
        row_loss = jnp.sum(err * err, axis=1, keepdims=True)
        loss = (0.5 * inv_d) * jnp.sum(row_loss, axis=0, keepdims=True)
        dy = err * inv_d
        small_ref[ROW_GAIN:ROW_GAIN + 1, :] += jnp.sum(dy * xhat, axis=0, keepdims=True)
        small_ref[ROW_BIAS:ROW_BIAS + 1, :] += jnp.sum(dy, axis=0, keepdims=True)
        small_ref[ROW_LOSS:ROW_LOSS + 1, :] += jnp.broadcast_to(loss, (1, d_model))
        dxhat = dy * gain_v
        m1 = jnp.sum(dxhat, axis=1, keepdims=True) * inv_d
        m2 = jnp.sum(dxhat * xhat, axis=1, keepdims=True) * inv_d
        dr = rstd * (dxhat - m1 - xhat * m2)
        dr_ref[...] = dr
        drb = dr.astype(BF16)
        dmix = _dot(drb, w, NT)
        dmc_ref[...] = dmix[:, :cw].astype(BF16)
        dma_ref[...] = dmix[:, cw:].astype(BF16)
        gwo_ref[...] += _dot(mix, drb, TN)

    def rows(width):
        return pl.BlockSpec((tm, width), lambda i: (i, 0))

    def whole(shape):
        return pl.BlockSpec(shape, lambda i: (0, 0))

    return pl.pallas_call(
        body, name="out_ln", grid=(seq // tm,),
        in_specs=[rows(cw), rows(cw), rows(d_model), rows(d_model), whole((1, d_model)), whole((1, d_model)),
                  whole((d_model, d_model))],
        out_specs=(rows(d_model), rows(cw), rows(cw), whole((d_model, d_model)), whole((SUBLANES, d_model))),
        out_shape=(jax.ShapeDtypeStruct((seq, d_model), F32), jax.ShapeDtypeStruct((seq, cw), BF16),
                   jax.ShapeDtypeStruct((seq, cw), BF16), jax.ShapeDtypeStruct((d_model, d_model), F32),
                   jax.ShapeDtypeStruct((SUBLANES, d_model), F32)),
        compiler_params=_params(48),
    )(mix_conv, mix_attn, x, target, gain, bias, w_out)


_DP_OF_GROUP = ((0, 0), (0, 1), (0, 2), (0, 3), (1, 0), (2, 0), (2, 1), (1, 1))


def _grad_x(dr, dp_conv, dp_qz, dp_kv, win_all, tm):
    seq, d_model = dr.shape
    nch, _, cw = win_all.shape

    def body(dr_ref, dc_ref, dqz_ref, dkv_ref, w_ref, o_ref):
        parts = (dc_ref, dqz_ref, dkv_ref)
        acc = ALPHA * dr_ref[...]
        for j in range(nch):
            arr, idx = _DP_OF_GROUP[j]
            acc = acc + _dot(parts[arr][idx], w_ref[j], NT)
        o_ref[...] = acc

    def part(n):
        return pl.BlockSpec((n, tm, cw), lambda i: (0, i, 0))

    return pl.pallas_call(
        body, name="grad_x", grid=(seq // tm,),
        in_specs=[pl.BlockSpec((tm, d_model), lambda i: (i, 0)), part(4), part(2), part(2),
                  pl.BlockSpec((nch, d_model, cw), lambda i: (0, 0, 0))],
        out_specs=pl.BlockSpec((tm, d_model), lambda i: (i, 0)),
        out_shape=jax.ShapeDtypeStruct((seq, d_model), F32),
        compiler_params=_params(48),
    )(dr, dp_conv, dp_qz, dp_kv, win_all)


def _grad_w_in(x, dp, tk, name):
    seq, d_model = x.shape
    n, _, cw = dp.shape

    def body(x_ref, d_ref, o_ref):
        @pl.when(pl.program_id(1) == 0)
        def _():
            o_ref[...] = jnp.zeros_like(o_ref)

        o_ref[...] += _dot(x_ref[...].astype(BF16), d_ref[...], TN)

    return pl.pallas_call(
        body, name=name, grid=(n, seq // tk),
        in_specs=[pl.BlockSpec((tk, d_model), lambda j, k: (k, 0)),
                  pl.BlockSpec((None, tk, cw), lambda j, k: (j, k, 0))],
        out_specs=pl.BlockSpec((None, d_model, cw), lambda j, k: (j, 0, 0)),
        out_shape=jax.ShapeDtypeStruct((n, d_model, cw), F32),
    )(x, dp)


def _reduce_update(gw_parts, gwo, small, w_in, m_in, v_in, w_out, m_out, v_out, row_chunk):
    d_model, cw = w_in.shape
    rows_out = w_out.shape[0]

    def body(ga, gb, gc, gwo_ref, small_ref, w_in_ref, m_in_ref, v_in_ref, w_out_ref, m_out_ref, v_out_ref,
             g_in_o, d_in_o, nm_in_o, nv_in_o, g_out_o, d_out_o, nm_out_o, nv_out_o, small_o,
             own_in, got_in, send_in, recv_in, own_out, got_out, send_out, recv_out, small_all,
             loc_sems, s1_send, s1_recv, s2_send, s2_recv, sm_send, sm_recv):
        x, y, c = _mesh_pos()
        me = 4 * x + 2 * y + c
        sibling = (x, y, 1 - c)
        chips = [(1 - x, y), (x, 1 - y), (1 - x, 1 - y)]
        parts = (ga, gb, gc)

        def block_in(k):
            arr, idx = _DP_OF_GROUP[k]
            return parts[arr].at[idx]

        def block_out(k):
            return gwo_ref.at[pl.ds(k * rows_out, rows_out), :]

        arrays = ((block_in, own_in, got_in, send_in, recv_in), (block_out, own_out, got_out, send_out, recv_out))

        small_all[me] = small_ref[...]
        for d in range(N_DEV):
            @pl.when(d != me)
            def _(d=d):
                pltpu.make_async_remote_copy(
                    src_ref=small_ref, dst_ref=small_all.at[me], send_sem=sm_send.at[d], recv_sem=sm_recv.at[me],
                    device_id=(d // 4, (d // 2) % 2, d % 2), device_id_type=MESH).start()

        for a, (block, own, got, _, _) in enumerate(arrays):
            for k in range(N_DEV):
                s = k // 2

                @pl.when(k % 2 != c)
                def _(a=a, k=k, s=s, block=block, got=got):
                    pltpu.make_async_remote_copy(
                        src_ref=block(k), dst_ref=got.at[s], send_sem=s1_send.at[a, s], recv_sem=s1_recv.at[a, s],
                        device_id=sibling, device_id_type=MESH).start()

                @pl.when(k % 2 == c)
                def _(a=a, k=k, s=s, block=block, own=own):
                    pltpu.make_async_copy(block(k), own.at[s], loc_sems.at[a, s]).start()

        def level1_wait(a, s):
            _, own, got, _, _ = arrays[a]
            pltpu.make_async_copy(own.at[s], own.at[s], loc_sems.at[a, s]).wait()
            pltpu.make_async_remote_copy(
                src_ref=got.at[s], dst_ref=got.at[s], send_sem=s1_send.at[a, s], recv_sem=s1_recv.at[a, s],
                device_id=sibling, device_id_type=MESH).wait()

        def level2_copy(a, j):
            _, _, _, send, recv = arrays[a]
            return pltpu.make_async_remote_copy(
                src_ref=send.at[j], dst_ref=recv.at[j], send_sem=s2_send.at[a, j], recv_sem=s2_recv.at[a, j],
                device_id=(*chips[j], c), device_id_type=MESH)

        def chip_sum(a, s, n_rows, fn):
            _, own, got, _, _ = arrays[a]
            step = min(row_chunk, n_rows)

            def rows_body(r, _):
                rows = pl.ds(pl.multiple_of(r * step, step), step)
                fn(rows, own[s, rows, :] + got[s, rows, :])
                return 0

            lax.fori_loop(0, n_rows // step, rows_body, 0)

        for a in range(2):
            for s in range(4):
                level1_wait(a, s)
        for a, n_rows in ((0, d_model), (1, rows_out)):
            send = arrays[a][3]
            for j, chip in enumerate(chips):
                s = 2 * chip[0] + chip[1]

                def to_send(rows, val, send=send, j=j):
                    send[j, rows, :] = val.astype(BF16)

                chip_sum(a, s, n_rows, to_send)
                level2_copy(a, j).start()

        for d in range(N_DEV):
            @pl.when(d != me)
            def _(d=d):
                pltpu.make_async_remote_copy(
                    src_ref=small_ref, dst_ref=small_all.at[d], send_sem=sm_send.at[d], recv_sem=sm_recv.at[d],
                    device_id=(d // 4, (d // 2) % 2, d % 2), device_id_type=MESH).wait()
        total = small_all[0]
        for d in range(1, N_DEV):
            total = total + small_all[d]
        small_o[...] = total

        outs = ((g_in_o, d_in_o, nm_in_o, nv_in_o, w_in_ref, m_in_ref, v_in_ref),
                (g_out_o, d_out_o, nm_out_o, nv_out_o, w_out_ref, m_out_ref, v_out_ref))
        mine = 2 * x + y
        for a, n_rows in ((0, d_model), (1, rows_out)):
            recv = arrays[a][4]
            for j in range(3):
                level2_copy(a, j).wait()
            g_o, d_o, nm_o, nv_o, w_r, m_r, v_r = outs[a]

            def update(rows, val, recv=recv, g_o=g_o, d_o=d_o, nm_o=nm_o, nv_o=nv_o, w_r=w_r, m_r=m_r, v_r=v_r):
                g = val
                for j in range(3):
                    g = g + recv[j, rows, :].astype(F32)
                delta, nm, nv = _adamw(w_r[rows, :], g, m_r[rows, :], v_r[rows, :])
                g_o[rows, :] = g
                d_o[rows, :] = delta
                nm_o[rows, :] = nm
                nv_o[rows, :] = nv

            chip_sum(a, mine, n_rows, update)

    vmem = pl.BlockSpec(memory_space=pltpu.VMEM)
    hbm = pl.BlockSpec(memory_space=pl.ANY)
    shard_in = jax.ShapeDtypeStruct((d_model, cw), F32)
    shard_out = jax.ShapeDtypeStruct((rows_out, d_model), F32)
    return pl.pallas_call(
        body, name="reduce_update",
        in_specs=[hbm, hbm, hbm, hbm] + [vmem] * 7,
        out_specs=(vmem,) * 9,
        out_shape=(shard_in,) * 4 + (shard_out,) * 4 + (jax.ShapeDtypeStruct(small.shape, F32),),
        scratch_shapes=[
            pltpu.VMEM((4, d_model, cw), F32), pltpu.VMEM((4, d_model, cw), F32),
            pltpu.VMEM((3, d_model, cw), BF16), pltpu.VMEM((3, d_model, cw), BF16),
            pltpu.VMEM((4, rows_out, d_model), F32), pltpu.VMEM((4, rows_out, d_model), F32),
            pltpu.VMEM((3, rows_out, d_model), BF16), pltpu.VMEM((3, rows_out, d_model), BF16),
            pltpu.VMEM((N_DEV,) + small.shape, F32),
            pltpu.SemaphoreType.DMA((2, 4)), pltpu.SemaphoreType.DMA((2, 4)), pltpu.SemaphoreType.DMA((2, 4)),
            pltpu.SemaphoreType.DMA((2, 3)), pltpu.SemaphoreType.DMA((2, 3)),
            pltpu.SemaphoreType.DMA((N_DEV,)), pltpu.SemaphoreType.DMA((N_DEV,)),
        ],
        compiler_params=_params(56),
    )(*gw_parts, gwo, small, w_in, m_in, v_in, w_out, m_out, v_out)


def _small_update(grads, weights, ms, vs):
    n = len(grads)

    def body(*refs):
        g_refs, w_refs, m_refs, v_refs = (refs[i * n:(i + 1) * n] for i in range(4))
        outs = refs[4 * n:]
        for i in range(n):
            delta, nm, nv = _adamw(w_refs[i][...], g_refs[i][...], m_refs[i][...], v_refs[i][...])
            outs[3 * i][...] = delta
            outs[3 * i + 1][...] = nm
            outs[3 * i + 2][...] = nv

    vmem = pl.BlockSpec(memory_space=pltpu.VMEM)
    out_shape = []
    for w in weights:
        out_shape += [jax.ShapeDtypeStruct(w.shape, F32)] * 3
    return pl.pallas_call(
        body, name="small_update", in_specs=[vmem] * (4 * n), out_specs=(vmem,) * (3 * n), out_shape=tuple(out_shape),
    )(*grads, *weights, *ms, *vs)


def _tile_sizes(seq):
    return dict(tm=min(512, seq), t_ln=min(256, seq), t_attn=min(128, seq), rc=min(256, seq), pairs=2)


def kernel(x, w_in, conv_w, w_out, ln_gain, ln_bias, loss_target, m_w_in, m_conv_w, m_w_out, m_ln_gain, m_ln_bias,
           v_w_in, v_conv_w, v_w_out, v_ln_gain, v_ln_bias):
    assert x.shape[0] == 1 and w_in.shape[0] == 1, "one sequence per device, depth 1"
    _, seq, d_model = x.shape
    cw = w_in.shape[2]
    conv_k, conv_cols = conv_w.shape[1], conv_w.shape[2]
    rows_out = w_out.shape[1]
    assert cw == d_model // 2 and cw % PAIR == 0 and conv_cols * N_DEV == cw and rows_out * N_DEV == d_model
    ts = _tile_sizes(seq)

    x2 = x.reshape(seq, d_model)
    target = loss_target.reshape(seq, d_model)
    me = 4 * lax.axis_index("x") + 2 * lax.axis_index("y") + lax.axis_index("c")

    conv_pad = jnp.pad(conv_w[0], ((0, SUBLANES - conv_k), (0, PAIR - conv_cols)))
    win_all, wout_all, conv_all = _gather_weights(w_in[0], w_out[0], conv_pad)
    w_out_full = wout_all.reshape(d_model, d_model)
    conv_full = conv_all[:, :conv_k, :conv_cols].transpose(1, 0, 2).reshape(conv_k, cw)
    conv_full = jnp.pad(conv_full, ((0, SUBLANES - conv_k), (0, 0)))

    proj = _proj(x2, win_all, ts["tm"])
    mix_conv = _conv_fwd(proj, conv_full, ts["rc"])
    pairs = min(ts["pairs"], cw // PAIR)
    o, mix_attn, tot = _attn_fwd(proj, ts["t_attn"], pairs)
    dr, d_mix_conv, d_mix_attn, gwo, small = _out_ln(mix_conv, mix_attn, x2, target, ln_gain, ln_bias, w_out_full,
                                                     ts["t_ln"])
    dp_conv, d_taps = _conv_bwd(proj, conv_full, d_mix_conv, ts["rc"])
    dp_qz, dp_kv = _attn_bwd(proj, o, tot, d_mix_attn, ts["t_attn"], pairs)
    grad_x = _grad_x(dr, dp_conv, dp_qz, dp_kv, win_all, ts["tm"])
    gw_parts = (_grad_w_in(x2, dp_conv, ts["tm"], "grad_w_in_conv"), _grad_w_in(x2, dp_qz, ts["tm"], "grad_w_in_qz"),
                _grad_w_in(x2, dp_kv, ts["tm"], "grad_w_in_kv"))

    small = small.at[ROW_CONV:ROW_CONV + conv_k, :cw].set(d_taps[:conv_k])
    (g_in, d_in, nm_in, nv_in, g_out, d_out, nm_out, nv_out, small_sum) = _reduce_update(
        gw_parts, gwo, small, w_in[0], m_w_in[0], v_w_in[0], w_out[0], m_w_out[0], v_w_out[0], 128)

    loss = small_sum[ROW_LOSS, 0]
    g_gain = small_sum[ROW_GAIN:ROW_GAIN + 1]
    g_bias = small_sum[ROW_BIAS:ROW_BIAS + 1]
    g_conv = lax.dynamic_slice(small_sum, (ROW_CONV, me * conv_cols), (conv_k, conv_cols))
    upd = _small_update((g_conv, g_gain, g_bias), (conv_w[0], ln_gain, ln_bias),
                        (m_conv_w[0], m_ln_gain, m_ln_bias), (v_conv_w[0], v_ln_gain, v_ln_bias))
    d_conv, nm_conv, nv_conv, d_gain, nm_gain, nv_gain, d_bias, nm_bias, nv_bias = upd

    lead = lambda a: a[None]
    return (loss, grad_x.reshape(1, seq, d_model), lead(g_in), lead(g_conv), lead(g_out), g_gain, g_bias,
            lead(d_in), lead(d_conv), lead(d_out), d_gain, d_bias,
            lead(nm_in), lead(nm_conv), lead(nm_out), nm_gain, nm_bias,
            lead(nv_in), lead(nv_conv), lead(nv_out), nv_gain, nv_bias)
```

```python
import functools

import jax
import jax.numpy as jnp
from jax import lax
from jax.experimental import pallas as pl
from jax.experimental.pallas import tpu as pltpu

F32 = jnp.float32
BF16 = jnp.bfloat16
MESH = pl.DeviceIdType.MESH

N_DEV = 8
HEAD_DIM = 64
PAIR = 128
SUBLANES = 8
LN_EPS = 1e-5
ALPHA = 2.0 ** 0.25
ADAM_LR, ADAM_B1, ADAM_B2, ADAM_EPS, ADAM_WD, ADAM_STEP = 0.001, 0.9, 0.999, 1e-08, 0.01, 10

ROW_GAIN, ROW_BIAS, ROW_CONV, ROW_LOSS = 0, 1, 2, 5

NT = (((1,), (1,)), ((), ()))
TN = (((0,), (0,)), ((), ()))


V7X_VMEM_BYTES = 64 * 1024 * 1024


def _params(vmem_mib):
    assert vmem_mib * 1024 * 1024 < V7X_VMEM_BYTES
    return pltpu.CompilerParams(vmem_limit_bytes=vmem_mib * 1024 * 1024)


def _dot(a, b, dims=None):
    if dims is None:
        return jnp.dot(a, b, preferred_element_type=F32)
    return lax.dot_general(a, b, dims, preferred_element_type=F32)


def _sigmoid(z):
    return 1.0 / (1.0 + jnp.exp(-z))


def _mesh_pos():
    return lax.axis_index("x"), lax.axis_index("y"), lax.axis_index("c")


def _adamw(w, g, m, v):
    nm = ADAM_B1 * m + (1.0 - ADAM_B1) * g
    nv = ADAM_B2 * v + (1.0 - ADAM_B2) * (g * g)
    m_hat = nm * (1.0 / (1.0 - ADAM_B1 ** ADAM_STEP))
    v_hat = nv * (1.0 / (1.0 - ADAM_B2 ** ADAM_STEP))
    delta = -ADAM_LR * (m_hat / (jnp.sqrt(v_hat) + ADAM_EPS) + ADAM_WD * w)
    return delta, nm, nv


def _gather_weights(w_in_s, w_out_s, conv_s):
    d_model, cw = w_in_s.shape
    rows_out = w_out_s.shape[0]
    n_arr = 3

    def body(win_ref, wout_ref, conv_ref, win_all, wout_all, conv_all, send_sems, recv_sems):
        x, y, c = _mesh_pos()
        me = (x, y, c)
        sibling = (x, y, 1 - c)
        chips = [(1 - x, y), (x, 1 - y), (1 - x, 1 - y)]
        bufs = (win_all, wout_all, conv_all)

        def slot(pos):
            return 4 * pos[0] + 2 * pos[1] + pos[2]

        win_all[slot(me)] = win_ref[...].astype(BF16)
        wout_all[slot(me)] = wout_ref[...].astype(BF16)
        conv_all[slot(me)] = conv_ref[...]

        def copy(a, k, block, to):
            ref = bufs[a].at[slot(block)]
            return pltpu.make_async_remote_copy(
                src_ref=ref, dst_ref=ref, send_sem=send_sems.at[a, k], recv_sem=recv_sems.at[a, k],
                device_id=to, device_id_type=MESH)

        first, passed = [], []
        for a in range(n_arr):
            first.append(copy(a, 0, me, sibling))
            first += [copy(a, 1 + j, me, (*chip, c)) for j, chip in enumerate(chips)]
        for cp in first:
            cp.start()
        for j, chip in enumerate(chips):
            for a in range(n_arr):
                copy(a, 1 + j, (*chip, c), me).wait_recv()
                cp = copy(a, 4 + j, (*chip, c), sibling)
                cp.start()
                passed.append(cp)
        for a in range(n_arr):
            copy(a, 0, sibling, me).wait_recv()
            for j, chip in enumerate(chips):
                copy(a, 4 + j, (*chip, 1 - c), me).wait_recv()
        for cp in first + passed:
            cp.wait_send()

    vmem = pl.BlockSpec(memory_space=pltpu.VMEM)
    return pl.pallas_call(
        body, name="gather_weights",
        out_shape=(jax.ShapeDtypeStruct((N_DEV, d_model, cw), BF16),
                   jax.ShapeDtypeStruct((N_DEV, rows_out, d_model), BF16),
                   jax.ShapeDtypeStruct((N_DEV,) + conv_s.shape, F32)),
        in_specs=[vmem, vmem, vmem], out_specs=(vmem, vmem, vmem),
        scratch_shapes=[pltpu.SemaphoreType.DMA((n_arr, 7)), pltpu.SemaphoreType.DMA((n_arr, 7))],
    )(w_in_s, w_out_s, conv_s)


def _proj(x, win_all, tm):
    seq, d_model = x.shape
    nch, _, cw = win_all.shape

    def body(x_ref, w_ref, o_ref):
        xb = x_ref[...].astype(BF16)
        for j in range(nch):
            o_ref[j] = _dot(xb, w_ref[j]).astype(BF16)

    return pl.pallas_call(
        body, name="proj", grid=(seq // tm,),
        in_specs=[pl.BlockSpec((tm, d_model), lambda i: (i, 0)),
                  pl.BlockSpec((nch, d_model, cw), lambda i: (0, 0, 0))],
        out_specs=pl.BlockSpec((nch, tm, cw), lambda i: (0, i, 0)),
        out_shape=jax.ShapeDtypeStruct((nch, seq, cw), BF16),
        compiler_params=_params(48),
    )(x, win_all)


def _conv_taps(ext, w_ref, rc):
    u0 = ext[SUBLANES:SUBLANES + rc]
    u1 = pltpu.roll(ext, 1, 0)[SUBLANES:SUBLANES + rc]
    u2 = pltpu.roll(ext, 2, 0)[SUBLANES:SUBLANES + rc]
    return w_ref[2:3, :] * u0 + w_ref[1:2, :] * u1 + w_ref[0:1, :] * u2, u0, u1, u2


def _conv_fwd(proj, conv_full, rc):
    _, seq, cw = proj.shape

    def body(b_ref, c_ref, h_ref, z_ref, w_ref, o_ref, u_scr):
        u_scr[0:SUBLANES, :] = jnp.zeros((SUBLANES, PAIR), F32)

        def fill(r, _):
            base = pl.multiple_of(r * rc, rc)
            rows = pl.ds(base, rc)
            u_scr[pl.ds(base + SUBLANES, rc), :] = c_ref[rows, :].astype(F32) * h_ref[rows, :].astype(F32)
            return 0

        lax.fori_loop(0, seq // rc, fill, 0)

        def out(r, _):
            base = pl.multiple_of(r * rc, rc)
            rows = pl.ds(base, rc)
            ext = u_scr[pl.ds(base, rc + SUBLANES), :]
            y, _, _, _ = _conv_taps(ext, w_ref, rc)
            z = z_ref[rows, :].astype(F32)
            o_ref[rows, :] = (z * _sigmoid(z) * b_ref[rows, :].astype(F32) * y).astype(BF16)
            return 0

        lax.fori_loop(0, seq // rc, out, 0)

    def chunk(j):
        return pl.BlockSpec((None, seq, PAIR), lambda cb, j=j: (j, 0, cb))

    return pl.pallas_call(
        body, name="conv_fwd", grid=(cw // PAIR,),
        in_specs=[chunk(0), chunk(1), chunk(2), chunk(3), pl.BlockSpec((SUBLANES, PAIR), lambda cb: (0, cb))],
        out_specs=pl.BlockSpec((seq, PAIR), lambda cb: (0, cb)),
        out_shape=jax.ShapeDtypeStruct((seq, cw), BF16),
        scratch_shapes=[pltpu.VMEM((seq + SUBLANES, PAIR), F32)],
    )(proj, proj, proj, proj, conv_full)


def _conv_bwd(proj, conv_full, d_mix_conv, rc):
    _, seq, cw = proj.shape

    def body(b_ref, c_ref, h_ref, z_ref, w_ref, g_ref, dp_ref, dw_ref, u_scr, dy_scr):
        u_scr[0:SUBLANES, :] = jnp.zeros((SUBLANES, PAIR), F32)
        dy_scr[seq:seq + SUBLANES, :] = jnp.zeros((SUBLANES, PAIR), F32)

        def fill(r, _):
            base = pl.multiple_of(r * rc, rc)
            rows = pl.ds(base, rc)
            u_scr[pl.ds(base + SUBLANES, rc), :] = c_ref[rows, :].astype(F32) * h_ref[rows, :].astype(F32)
            return 0

        lax.fori_loop(0, seq // rc, fill, 0)

        def gate(r, acc):
            base = pl.multiple_of(r * rc, rc)
            rows = pl.ds(base, rc)
            ext = u_scr[pl.ds(base, rc + SUBLANES), :]
            y, u0, u1, u2 = _conv_taps(ext, w_ref, rc)
            z = z_ref[rows, :].astype(F32)
            b = b_ref[rows, :].astype(F32)
            g = g_ref[rows, :].astype(F32)
            sig = _sigmoid(z)
            dp_ref[3, rows, :] = (g * b * y * (sig * (1.0 + z * (1.0 - sig)))).astype(BF16)
            gs = g * (z * sig)
            dp_ref[0, rows, :] = (gs * y).astype(BF16)
            dy = gs * b
            dy_scr[rows, :] = dy
            a0, a1, a2 = acc
            return (a0 + jnp.sum(dy * u2, axis=0, keepdims=True),
                    a1 + jnp.sum(dy * u1, axis=0, keepdims=True),
                    a2 + jnp.sum(dy * u0, axis=0, keepdims=True))

        zero = jnp.zeros((1, PAIR), F32)
        a0, a1, a2 = lax.fori_loop(0, seq // rc, gate, (zero, zero, zero))
        dw_ref[...] = jnp.zeros((SUBLANES, PAIR), F32)
        dw_ref[0:1, :] = a0
        dw_ref[1:2, :] = a1
        dw_ref[2:3, :] = a2

        def back(r, _):
            base = pl.multiple_of(r * rc, rc)
            rows = pl.ds(base, rc)
            ext = dy_scr[pl.ds(base, rc + SUBLANES), :]
            n = rc + SUBLANES
            d0 = ext[0:rc]
            d1 = pltpu.roll(ext, n - 1, 0)[0:rc]
            d2 = pltpu.roll(ext, n - 2, 0)[0:rc]
            du = w_ref[2:3, :] * d0 + w_ref[1:2, :] * d1 + w_ref[0:1, :] * d2
            dp_ref[1, rows, :] = (du * h_ref[rows, :].astype(F32)).astype(BF16)
            dp_ref[2, rows, :] = (du * c_ref[rows, :].astype(F32)).astype(BF16)
            return 0

        lax.fori_loop(0, seq // rc, back, 0)

    def chunk(j):
        return pl.BlockSpec((None, seq, PAIR), lambda cb, j=j: (j, 0, cb))

    return pl.pallas_call(
        body, name="conv_bwd", grid=(cw // PAIR,),
        in_specs=[chunk(0), chunk(1), chunk(2), chunk(3), pl.BlockSpec((SUBLANES, PAIR), lambda cb: (0, cb)),
                  pl.BlockSpec((seq, PAIR), lambda cb: (0, cb))],
        out_specs=(pl.BlockSpec((4, seq, PAIR), lambda cb: (0, 0, cb)),
                   pl.BlockSpec((SUBLANES, PAIR), lambda cb: (0, cb))),
        out_shape=(jax.ShapeDtypeStruct((4, seq, cw), BF16), jax.ShapeDtypeStruct((SUBLANES, cw), F32)),
        scratch_shapes=[pltpu.VMEM((seq + SUBLANES, PAIR), F32), pltpu.VMEM((seq + SUBLANES, PAIR), F32)],
    )(proj, proj, proj, proj, conv_full, d_mix_conv)


SKIP_CARRY = 104.0
LANE_TOT0, LANE_TOT1, LANE_FIRST = 0, 1, 2
FAST_BLOCKS = 3


def _triangles(t):
    row = lax.broadcasted_iota(jnp.int32, (2 * t, 2 * t), 0)
    col = lax.broadcasted_iota(jnp.int32, (2 * t, 2 * t), 1)
    same = (row < t) == (col < t)
    upper = jnp.logical_and(same, row > col).astype(BF16)
    lower = jnp.logical_and(same, row < col).astype(BF16)
    return jnp.stack([jnp.concatenate([upper, upper], axis=0), jnp.concatenate([lower, lower], axis=0)])


def _pair_masks(t):
    lane = lax.broadcasted_iota(jnp.int32, (t, PAIR), 1)
    qrow = lax.broadcasted_iota(jnp.int32, (t, 2 * t), 0)
    kcol = lax.broadcasted_iota(jnp.int32, (t, 2 * t), 1)
    strict = jnp.where(kcol < t, kcol, kcol - t) < qrow
    return lane, lane < HEAD_DIM, strict


def _by_head(x, head0):
    zero = jnp.zeros_like(x)
    return jnp.concatenate([jnp.where(head0, x, zero), jnp.where(head0, zero, x)], axis=0)


def _split_dot(a, tri):
    hi = a.astype(BF16)
    lo = (a - hi.astype(F32)).astype(BF16)
    return _dot(jnp.concatenate([hi, lo], axis=1), tri)


def _scores(z, strict, masked, upper, t):
    spu = jnp.maximum(z, 0.0) + jnp.log(1.0 + jnp.exp(-jnp.abs(z)))
    ls = z - spu
    sp = jnp.where(strict, spu, 0.0) if masked else spu
    after = _split_dot(sp, upper)
    return ls, ls - after, after[:, 0:1] + sp[:, 0:1], after[:, t:t + 1] + sp[:, t:t + 1]


def _attn_fwd(proj, tri, t, pp):
    _, seq, cw = proj.shape
    scale = HEAD_DIM ** -0.5
    width = pp * PAIR

    def body(q_ref, k_ref, v_ref, za_ref, tri_ref, o_ref, mix_ref, tot_ref):
        i = pl.program_id(1)
        lane, head0, strict = _pair_masks(t)
        upper = tri_ref[0]
        q = q_ref[...] * scale

        def sweep(blocks, state):
            staged = []
            for j, masked in blocks:
                start = pl.multiple_of(j * t, t)
                kb = k_ref[pl.ds(start, t), :]
                vb = v_ref[pl.ds(start, t), :]
                per_pair = []
                for p in range(pp):
                    cols = slice(p * PAIR, (p + 1) * PAIR)
                    z = _dot(q[:, cols], _by_head(kb[:, cols], head0), NT)
                    _, x, t0, t1 = _scores(z, strict, masked, upper, t)
                    per_pair.append((x, t0, t1, _by_head(vb[:, cols], head0)))
                staged.append(per_pair)
            state = list(state)
            for (_, masked), per_pair in zip(blocks, staged):
                for p in range(pp):
                    x, t0, t1, v2 = per_pair[p]
                    (c0, c1), acc = state[p]
                    w = jnp.exp(jnp.concatenate([x[:, :t] - c0, x[:, t:] - c1], axis=1))
                    if masked:
                        w = jnp.where(strict, w, 0.0)
                    state[p] = ((c0 + t0, c1 + t1), acc + _dot(w.astype(BF16), v2))
            return tuple(state)

        def unfinished(state):
            m = state[0][0][0]
            for p in range(pp):
                m = jnp.minimum(m, jnp.minimum(state[p][0][0], state[p][0][1]))
            return jnp.min(m) < SKIP_CARRY

        def step(js):
            state = sweep(((js[0], False),), js[1])
            return js[0] - 1, state, unfinished(state)

        zcol = jnp.zeros((t, 1), F32)
        init = tuple(((zcol, zcol), jnp.zeros((t, PAIR), F32)) for _ in range(pp))
        many = i >= FAST_BLOCKS - 1
        state = lax.cond(
            many,
            lambda: sweep(((i, True),) + tuple((i - b, False) for b in range(1, FAST_BLOCKS)), init),
            lambda: sweep(((i, True),), init))
        j_end, state, _ = lax.while_loop(
            lambda js: jnp.logical_and(js[0] >= 0, js[2]), step,
            (jnp.where(many, i - FAST_BLOCKS, i - 1), state, unfinished(state)))
        first = (j_end + 1).astype(F32)
        za = za_ref[...].astype(F32)
        for p in range(pp):
            (c0, c1), acc = state[p]
            cols = slice(p * PAIR, (p + 1) * PAIR)
            zp = za[:, cols]
            o_ref[:, cols] = acc.astype(BF16)
            mix_ref[:, cols] = (zp * _sigmoid(zp) * acc).astype(BF16)
            tot_ref[:, cols] = jnp.where(lane == LANE_TOT0, c0, jnp.where(lane == LANE_TOT1, c1, first))

    def tile(j):
        return pl.BlockSpec((None, t, width), lambda g, i, j=j: (j, i, g))

    def full(j):
        return pl.BlockSpec((None, seq, width), lambda g, i, j=j: (j, 0, g))

    out_tile = pl.BlockSpec((t, width), lambda g, i: (i, g))
    return pl.pallas_call(
        body, name="attn_fwd", grid=(cw // width, seq // t),
        in_specs=[tile(4), full(5), full(6), tile(7), pl.BlockSpec(tri.shape, lambda g, i: (0, 0, 0))],
        out_specs=(out_tile, out_tile, out_tile),
        out_shape=(jax.ShapeDtypeStruct((seq, cw), BF16), jax.ShapeDtypeStruct((seq, cw), BF16),
                   jax.ShapeDtypeStruct((seq, cw), F32)),
    )(proj, proj, proj, proj, tri)


def _attn_bwd(proj, tri, o, tot, d_mix_attn, t, pp):
    _, seq, cw = proj.shape
    nb = seq // t
    scale = HEAD_DIM ** -0.5
    width = pp * PAIR

    def body(q_ref, k_ref, v_ref, za_ref, tri_ref, o_ref, tot_ref, g_ref, dqz_ref, dkv_ref, dk_acc, dv_acc):
        i = pl.program_id(1)

        @pl.when(i == 0)
        def _():
            dk_acc[...] = jnp.zeros_like(dk_acc)
            dv_acc[...] = jnp.zeros_like(dv_acc)

        _, head0, strict = _pair_masks(t)
        upper, lower = tri_ref[0], tri_ref[1]
        za = za_ref[...].astype(F32)
        g = g_ref[...].astype(F32)
        sig = _sigmoid(za)
        dqz_ref[1] = (g * o_ref[...].astype(F32) * (sig * (1.0 + za * (1.0 - sig)))).astype(BF16)
        do = (g * (za * sig)).astype(BF16)
        q = q_ref[...] * scale
        tot_v = tot_ref[...]
        q2, do2, init = [], [], []
        zcol = jnp.zeros((t, 1), F32)
        for p in range(pp):
            cols = slice(p * PAIR, (p + 1) * PAIR)
            q2.append(_by_head(q[:, cols], head0))
            do2.append(_by_head(do[:, cols], head0))
            tp = tot_v[:, cols]
            init.append(((tp[:, LANE_TOT0:LANE_TOT0 + 1], tp[:, LANE_TOT1:LANE_TOT1 + 1]), (zcol, zcol),
                         jnp.zeros((t, PAIR), F32)))
        first = jnp.clip(tot_v[0:1, LANE_FIRST:LANE_FIRST + 1], 0.0, i.astype(F32)).astype(jnp.int32)[0, 0]

        def sweep(blocks, state):
            staged = []
            for j, masked in blocks:
                start = pl.multiple_of(j * t, t)
                kb = k_ref[pl.ds(start, t), :]
                vb = v_ref[pl.ds(start, t), :]
                per_pair = []
                for p in range(pp):
                    cols = slice(p * PAIR, (p + 1) * PAIR)
                    k2 = _by_head(kb[:, cols], head0)
                    z = _dot(q[:, cols], k2, NT)
                    ls, x, t0, t1 = _scores(z, strict, masked, upper, t)
                    da = _dot(do[:, cols], _by_head(vb[:, cols], head0), NT)
                    per_pair.append((start, k2, ls, x, t0, t1, da))
                staged.append(per_pair)
            state = list(state)
            for (_, masked), per_pair in zip(blocks, staged):
                for p in range(pp):
                    cols = slice(p * PAIR, (p + 1) * PAIR)
                    start, k2, ls, x, t0, t1, da = per_pair[p]
                    (r0, r1), (b0, b1), dq = state[p]
                    r0, r1 = r0 - t0, r1 - t1
                    a = jnp.exp(jnp.concatenate([x[:, :t] - r0, x[:, t:] - r1], axis=1))
                    if masked:
                        a = jnp.where(strict, a, 0.0)
                    gg = a * da
                    y = gg + _split_dot(gg, lower)
                    dz = gg - jnp.exp(ls) * jnp.concatenate([y[:, :t] + b0, y[:, t:] + b1], axis=1)
                    if masked:
                        dz = jnp.where(strict, dz, 0.0)
                    dzb = dz.astype(BF16)
                    ab = a.astype(BF16)
                    dz2 = jnp.concatenate([dzb[:, :t], dzb[:, t:]], axis=0)
                    a2 = jnp.concatenate([ab[:, :t], ab[:, t:]], axis=0)
                    dk_acc[pl.ds(start, t), cols] += _dot(dz2, q2[p], TN)
                    dv_acc[pl.ds(start, t), cols] += _dot(a2, do2[p], TN)
                    state[p] = ((r0, r1), (b0 + y[:, t - 1:t], b1 + y[:, 2 * t - 1:2 * t]), dq + _dot(dzb, k2))
            return tuple(state)

        many = i >= FAST_BLOCKS - 1
        last_single = jnp.where(many, i - (FAST_BLOCKS - 1), i)
        state = lax.fori_loop(first, last_single, lambda j, s: sweep(((j, False),), s), tuple(init))
        state = lax.cond(
            many,
            lambda: sweep(tuple((i - b, False) for b in range(FAST_BLOCKS - 1, 0, -1)) + ((i, True),), state),
            lambda: sweep(((i, True),), state))
        for p in range(pp):
            dqz_ref[0, :, p * PAIR:(p + 1) * PAIR] = (state[p][2] * scale).astype(BF16)

        @pl.when(i == nb - 1)
        def _():
            dkv_ref[0] = dk_acc[...].astype(BF16)
            dkv_ref[1] = dv_acc[...].astype(BF16)

    def tile(j):
        return pl.BlockSpec((None, t, width), lambda g, i, j=j: (j, i, g))

    def full(j):
        return pl.BlockSpec((None, seq, width), lambda g, i, j=j: (j, 0, g))

    flat_tile = pl.BlockSpec((t, width), lambda g, i: (i, g))
    return pl.pallas_call(
        body, name="attn_bwd", grid=(cw // width, nb),
        in_specs=[tile(4), full(5), full(6), tile(7), pl.BlockSpec(tri.shape, lambda g, i: (0, 0, 0)),
                  flat_tile, flat_tile, flat_tile],
        out_specs=(pl.BlockSpec((2, t, width), lambda g, i: (0, i, g)),
                   pl.BlockSpec((2, seq, width), lambda g, i: (0, 0, g))),
        out_shape=(jax.ShapeDtypeStruct((2, seq, cw), BF16), jax.ShapeDtypeStruct((2, seq, cw), BF16)),
        scratch_shapes=[pltpu.VMEM((seq, width), F32), pltpu.VMEM((seq, width), F32)],
        compiler_params=_params(48),
    )(proj, proj, proj, proj, tri, o, tot, d_mix_attn)


def _out_ln(mix_conv, mix_attn, x, target, gain, bias, w_out, tm):
    seq, d_model = x.shape
    cw = mix_conv.shape[1]
    inv_d = 1.0 / d_model

    def body(mc_ref, ma_ref, x_ref, t_ref, gain_ref, bias_ref, w_ref, dr_ref, dmc_ref, dma_ref, gwo_ref, small_ref):
        @pl.when(pl.program_id(0) == 0)
        def _():
            gwo_ref[...] = jnp.zeros_like(gwo_ref)
            small_ref[...] = jnp.zeros_like(small_ref)

        mix = jnp.concatenate([mc_ref[...], ma_ref[...]], axis=1)
        w = w_ref[...]
        r = ALPHA * x_ref[...] + _dot(mix, w)
        mu = jnp.sum(r, axis=1, keepdims=True) * inv_d
        xc = r - mu
        var = jnp.sum(xc * xc, axis=1, keepdims=True) * inv_d
        rstd = lax.rsqrt(var + LN_EPS)
        xhat = xc * rstd
        gain_v = gain_ref[...]
        err = xhat * gain_v + bias_ref[...] - t_ref[...]
        row_loss = jnp.sum(err * err, axis=1, keepdims=True)
        loss = (0.5 * inv_d) * jnp.sum(row_loss, axis=0, keepdims=True)
        dy = err * inv_d
        small_ref[ROW_GAIN:ROW_GAIN + 1, :] += jnp.sum(dy * xhat, axis=0, keepdims=True)
        small_ref[ROW_BIAS:ROW_BIAS + 1, :] += jnp.sum(dy, axis=0, keepdims=True)
        small_ref[ROW_LOSS:ROW_LOSS + 1, :] += jnp.broadcast_to(loss, (1, d_model))
        dxhat = dy * gain_v
        m1 = jnp.sum(dxhat, axis=1, keepdims=True) * inv_d
        m2 = jnp.sum(dxhat * xhat, axis=1, keepdims=True) * inv_d
        dr = rstd * (dxhat - m1 - xhat * m2)
        dr_ref[...] = dr
        drb = dr.astype(BF16)
        dmix = _dot(drb, w, NT)
        dmc_ref[...] = dmix[:, :cw].astype(BF16)
        dma_ref[...] = dmix[:, cw:].astype(BF16)
        gwo_ref[...] += _dot(mix, drb, TN)

    def rows(width):
        return pl.BlockSpec((tm, width), lambda i: (i, 0))

    def whole(shape):
        return pl.BlockSpec(shape, lambda i: (0, 0))

    return pl.pallas_call(
        body, name="out_ln", grid=(seq // tm,),
        in_specs=[rows(cw), rows(cw), rows(d_model), rows(d_model), whole((1, d_model)), whole((1, d_model)),
                  whole((d_model, d_model))],
        out_specs=(rows(d_model), rows(cw), rows(cw), whole((d_model, d_model)), whole((SUBLANES, d_model))),
        out_shape=(jax.ShapeDtypeStruct((seq, d_model), F32), jax.ShapeDtypeStruct((seq, cw), BF16),
                   jax.ShapeDtypeStruct((seq, cw), BF16), jax.ShapeDtypeStruct((d_model, d_model), F32),
                   jax.ShapeDtypeStruct((SUBLANES, d_model), F32)),
        compiler_params=_params(48),
    )(mix_conv, mix_attn, x, target, gain, bias, w_out)


_DP_OF_GROUP = ((0, 0), (0, 1), (0, 2), (0, 3), (1, 0), (2, 0), (2, 1), (1, 1))


def _grad_x(dr, dp_conv, dp_qz, dp_kv, win_all, tm):
    seq, d_model = dr.shape
    nch, _, cw = win_all.shape

    def body(dr_ref, dc_ref, dqz_ref, dkv_ref, w_ref, o_ref):
        parts = (dc_ref, dqz_ref, dkv_ref)
        acc = ALPHA * dr_ref[...]
        for j in range(nch):
            arr, idx = _DP_OF_GROUP[j]
            acc = acc + _dot(parts[arr][idx], w_ref[j], NT)
        o_ref[...] = acc

    def part(n):
        return pl.BlockSpec((n, tm, cw), lambda i: (0, i, 0))

    return pl.pallas_call(
        body, name="grad_x", grid=(seq // tm,),
        in_specs=[pl.BlockSpec((tm, d_model), lambda i: (i, 0)), part(4), part(2), part(2),
                  pl.BlockSpec((nch, d_model, cw), lambda i: (0, 0, 0))],
        out_specs=pl.BlockSpec((tm, d_model), lambda i: (i, 0)),
        out_shape=jax.ShapeDtypeStruct((seq, d_model), F32),
        compiler_params=_params(48),
    )(dr, dp_conv, dp_qz, dp_kv, win_all)


def _grad_w_in(x, dp, tk, name):
    seq, d_model = x.shape
    n, _, cw = dp.shape

    def body(x_ref, d_ref, o_ref):
        @pl.when(pl.program_id(1) == 0)
        def _():
            o_ref[...] = jnp.zeros_like(o_ref)

        o_ref[...] += _dot(x_ref[...].astype(BF16), d_ref[...], TN)

    return pl.pallas_call(
        body, name=name, grid=(n, seq // tk),
        in_specs=[pl.BlockSpec((tk, d_model), lambda j, k: (k, 0)),
                  pl.BlockSpec((None, tk, cw), lambda j, k: (j, k, 0))],
        out_specs=pl.BlockSpec((None, d_model, cw), lambda j, k: (j, 0, 0)),
        out_shape=jax.ShapeDtypeStruct((n, d_model, cw), F32),
    )(x, dp)


def _reduce_update(gw_parts, gwo, small, w_in, m_in, v_in, w_out, m_out, v_out, row_chunk):
    d_model, cw = w_in.shape
    rows_out = w_out.shape[0]

    def body(ga, gb, gc, gwo_ref, small_ref, w_in_ref, m_in_ref, v_in_ref, w_out_ref, m_out_ref, v_out_ref,
             g_in_o, d_in_o, nm_in_o, nv_in_o, g_out_o, d_out_o, nm_out_o, nv_out_o, small_o,
             own_in, got_in, send_in, recv_in, own_out, got_out, send_out, recv_out, small_all,
             loc_sems, s1_send, s1_recv, s2_send, s2_recv, sm_send, sm_recv):
        x, y, c = _mesh_pos()
        me = 4 * x + 2 * y + c
        sibling = (x, y, 1 - c)
        chips = [(1 - x, y), (x, 1 - y), (1 - x, 1 - y)]
        parts = (ga, gb, gc)

        def block_in(k):
            arr, idx = _DP_OF_GROUP[k]
            return parts[arr].at[idx]

        def block_out(k):
            return gwo_ref.at[pl.ds(k * rows_out, rows_out), :]

        arrays = ((block_in, own_in, got_in, send_in, recv_in), (block_out, own_out, got_out, send_out, recv_out))

        small_all[me] = small_ref[...]
        for d in range(N_DEV):
            @pl.when(d != me)
            def _(d=d):
                pltpu.make_async_remote_copy(
                    src_ref=small_ref, dst_ref=small_all.at[me], send_sem=sm_send.at[d], recv_sem=sm_recv.at[me],
                    device_id=(d // 4, (d // 2) % 2, d % 2), device_id_type=MESH).start()

        for a, (block, own, got, _, _) in enumerate(arrays):
            for k in range(N_DEV):
                s = k // 2

                @pl.when(k % 2 != c)
                def _(a=a, k=k, s=s, block=block, got=got):
                    pltpu.make_async_remote_copy(
                        src_ref=block(k), dst_ref=got.at[s], send_sem=s1_send.at[a, s], recv_sem=s1_recv.at[a, s],
                        device_id=sibling, device_id_type=MESH).start()

                @pl.when(k % 2 == c)
                def _(a=a, k=k, s=s, block=block, own=own):
                    pltpu.make_async_copy(block(k), own.at[s], loc_sems.at[a, s]).start()

        def level1_wait(a, s):
            _, own, got, _, _ = arrays[a]
            pltpu.make_async_copy(own.at[s], own.at[s], loc_sems.at[a, s]).wait()
            pltpu.make_async_remote_copy(
                src_ref=got.at[s], dst_ref=got.at[s], send_sem=s1_send.at[a, s], recv_sem=s1_recv.at[a, s],
                device_id=sibling, device_id_type=MESH).wait()

        def level2_copy(a, j):
            _, _, _, send, recv = arrays[a]
            return pltpu.make_async_remote_copy(
                src_ref=send.at[j], dst_ref=recv.at[j], send_sem=s2_send.at[a, j], recv_sem=s2_recv.at[a, j],
                device_id=(*chips[j], c), device_id_type=MESH)

        def chip_sum(a, s, n_rows, fn):
            _, own, got, _, _ = arrays[a]
            step = min(row_chunk, n_rows)

            def rows_body(r, _):
                rows = pl.ds(pl.multiple_of(r * step, step), step)
                fn(rows, own[s, rows, :] + got[s, rows, :])
                return 0

            lax.fori_loop(0, n_rows // step, rows_body, 0)

        for a in range(2):
            for s in range(4):
                level1_wait(a, s)
        for a, n_rows in ((0, d_model), (1, rows_out)):
            send = arrays[a][3]
            for j, chip in enumerate(chips):
                s = 2 * chip[0] + chip[1]

                def to_send(rows, val, send=send, j=j):
                    send[j, rows, :] = val.astype(BF16)

                chip_sum(a, s, n_rows, to_send)
                level2_copy(a, j).start()

        for d in range(N_DEV):
            @pl.when(d != me)
            def _(d=d):
                pltpu.make_async_remote_copy(
                    src_ref=small_ref, dst_ref=small_all.at[d], send_sem=sm_send.at[d], recv_sem=sm_recv.at[d],
                    device_id=(d // 4, (d // 2) % 2, d % 2), device_id_type=MESH).wait()
        total = small_all[0]
        for d in range(1, N_DEV):
            total = total + small_all[d]
        small_o[...] = total

        outs = ((g_in_o, d_in_o, nm_in_o, nv_in_o, w_in_ref, m_in_ref, v_in_ref),
                (g_out_o, d_out_o, nm_out_o, nv_out_o, w_out_ref, m_out_ref, v_out_ref))
        mine = 2 * x + y
        for a, n_rows in ((0, d_model), (1, rows_out)):
            recv = arrays[a][4]
            for j in range(3):
                level2_copy(a, j).wait()
            g_o, d_o, nm_o, nv_o, w_r, m_r, v_r = outs[a]

            def update(rows, val, recv=recv, g_o=g_o, d_o=d_o, nm_o=nm_o, nv_o=nv_o, w_r=w_r, m_r=m_r, v_r=v_r):
                g = val
                for j in range(3):
                    g = g + recv[j, rows, :].astype(F32)
                delta, nm, nv = _adamw(w_r[rows, :], g, m_r[rows, :], v_r[rows, :])
                g_o[rows, :] = g
                d_o[rows, :] = delta
                nm_o[rows, :] = nm
                nv_o[rows, :] = nv

            chip_sum(a, mine, n_rows, update)

    vmem = pl.BlockSpec(memory_space=pltpu.VMEM)
    hbm = pl.BlockSpec(memory_space=pl.ANY)
    shard_in = jax.ShapeDtypeStruct((d_model, cw), F32)
    shard_out = jax.ShapeDtypeStruct((rows_out, d_model), F32)
    return pl.pallas_call(
        body, name="reduce_update",
        in_specs=[hbm, hbm, hbm, hbm] + [vmem] * 7,
        out_specs=(vmem,) * 9,
        out_shape=(shard_in,) * 4 + (shard_out,) * 4 + (jax.ShapeDtypeStruct(small.shape, F32),),
        scratch_shapes=[
            pltpu.VMEM((4, d_model, cw), F32), pltpu.VMEM((4, d_model, cw), F32),
            pltpu.VMEM((3, d_model, cw), BF16), pltpu.VMEM((3, d_model, cw), BF16),
            pltpu.VMEM((4, rows_out, d_model), F32), pltpu.VMEM((4, rows_out, d_model), F32),
            pltpu.VMEM((3, rows_out, d_model), BF16), pltpu.VMEM((3, rows_out, d_model), BF16),
            pltpu.VMEM((N_DEV,) + small.shape, F32),
            pltpu.SemaphoreType.DMA((2, 4)), pltpu.SemaphoreType.DMA((2, 4)), pltpu.SemaphoreType.DMA((2, 4)),
            pltpu.SemaphoreType.DMA((2, 3)), pltpu.SemaphoreType.DMA((2, 3)),
            pltpu.SemaphoreType.DMA((N_DEV,)), pltpu.SemaphoreType.DMA((N_DEV,)),
        ],
        compiler_params=_params(56),
    )(*gw_parts, gwo, small, w_in, m_in, v_in, w_out, m_out, v_out)


def _small_update(grads, weights, ms, vs):
    n = len(grads)

    def body(*refs):
        g_refs, w_refs, m_refs, v_refs = (refs[i * n:(i + 1) * n] for i in range(4))
        outs = refs[4 * n:]
        for i in range(n):
            delta, nm, nv = _adamw(w_refs[i][...], g_refs[i][...], m_refs[i][...], v_refs[i][...])
            outs[3 * i][...] = delta
            outs[3 * i + 1][...] = nm
            outs[3 * i + 2][...] = nv

    vmem = pl.BlockSpec(memory_space=pltpu.VMEM)
    out_shape = []
    for w in weights:
        out_shape += [jax.ShapeDtypeStruct(w.shape, F32)] * 3
    return pl.pallas_call(
        body, name="small_update", in_specs=[vmem] * (4 * n), out_specs=(vmem,) * (3 * n), out_shape=tuple(out_shape),
    )(*grads, *weights, *ms, *vs)


def _tile_sizes(seq):
    return dict(tm=min(512, seq), t_ln=min(256, seq), t_attn=min(128, seq), rc=min(256, seq), pairs=2)


def kernel(x, w_in, conv_w, w_out, ln_gain, ln_bias, loss_target, m_w_in, m_conv_w, m_w_out, m_ln_gain, m_ln_bias,
           v_w_in, v_conv_w, v_w_out, v_ln_gain, v_ln_bias):
    assert x.shape[0] == 1 and w_in.shape[0] == 1, "one sequence per device, depth 1"
    _, seq, d_model = x.shape
    cw = w_in.shape[2]
    conv_k, conv_cols = conv_w.shape[1], conv_w.shape[2]
    rows_out = w_out.shape[1]
    assert cw == d_model // 2 and cw % PAIR == 0 and conv_cols * N_DEV == cw and rows_out * N_DEV == d_model
    ts = _tile_sizes(seq)

    x2 = x.reshape(seq, d_model)
    target = loss_target.reshape(seq, d_model)
    me = 4 * lax.axis_index("x") + 2 * lax.axis_index("y") + lax.axis_index("c")

    conv_pad = jnp.pad(conv_w[0], ((0, SUBLANES - conv_k), (0, PAIR - conv_cols)))
    win_all, wout_all, conv_all = _gather_weights(w_in[0], w_out[0], conv_pad)
    w_out_full = wout_all.reshape(d_model, d_model)
    conv_full = conv_all[:, :conv_k, :conv_cols].transpose(1, 0, 2).reshape(conv_k, cw)
    conv_full = jnp.pad(conv_full, ((0, SUBLANES - conv_k), (0, 0)))

    proj = _proj(x2, win_all, ts["tm"])
    mix_conv = _conv_fwd(proj, conv_full, ts["rc"])
    pairs = min(ts["pairs"], cw // PAIR)
    tri = _triangles(ts["t_attn"])
    o, mix_attn, tot = _attn_fwd(proj, tri, ts["t_attn"], pairs)
    dr, d_mix_conv, d_mix_attn, gwo, small = _out_ln(mix_conv, mix_attn, x2, target, ln_gain, ln_bias, w_out_full,
                                                     ts["t_ln"])
    dp_conv, d_taps = _conv_bwd(proj, conv_full, d_mix_conv, ts["rc"])
    dp_qz, dp_kv = _attn_bwd(proj, tri, o, tot, d_mix_attn, ts["t_attn"], pairs)
    grad_x = _grad_x(dr, dp_conv, dp_qz, dp_kv, win_all, ts["tm"])
    gw_parts = (_grad_w_in(x2, dp_conv, ts["tm"], "grad_w_in_conv"), _grad_w_in(x2, dp_qz, ts["tm"], "grad_w_in_qz"),
                _grad_w_in(x2, dp_kv, ts["tm"], "grad_w_in_kv"))

    small = small.at[ROW_CONV:ROW_CONV + conv_k, :cw].set(d_taps[:conv_k])
    (g_in, d_in, nm_in, nv_in, g_out, d_out, nm_out, nv_out, small_sum) = _reduce_update(
        gw_parts, gwo, small, w_in[0], m_w_in[0], v_w_in[0], w_out[0], m_w_out[0], v_w_out[0], 128)

    loss = small_sum[ROW_LOSS, 0]
    g_gain = small_sum[ROW_GAIN:ROW_GAIN + 1]
    g_bias = small_sum[ROW_BIAS:ROW_BIAS + 1]
    g_conv = lax.dynamic_slice(small_sum, (ROW_CONV, me * conv_cols), (conv_k, conv_cols))
    upd = _small_update((g_conv, g_gain, g_bias), (conv_w[0], ln_gain, ln_bias),
                        (m_conv_w[0], m_ln_gain, m_ln_bias), (v_conv_w[0], v_ln_gain, v_ln_bias))
    d_conv, nm_conv, nv_conv, d_gain, nm_gain, nv_gain, d_bias, nm_bias, nv_bias = upd

    lead = lambda a: a[None]
    return (loss, grad_x.reshape(1, seq, d_model), lead(g_in), lead(g_conv), lead(g_out), g_gain, g_bias,
            lead(d_in), lead(d_conv), lead(d_out), d_gain, d_bias,
            lead(nm_in), lead(nm_conv), lead(nm_out), nm_gain, nm_bias,
            lead(nv_in), lead(nv_conv), lead(nv_out), nv_gain, nv_bias)
```

```python
import functools

import jax
import jax.numpy as jnp
from jax import lax
from jax.experimental import pallas as pl
from jax.experimental.pallas import tpu as pltpu

F32 = jnp.float32
BF16 = jnp.bfloat16
MESH = pl.DeviceIdType.MESH

N_DEV = 8
HEAD_DIM = 64
PAIR = 128
SUBLANES = 8
LN_EPS = 1e-5
ALPHA = 2.0 ** 0.25
ADAM_LR, ADAM_B1, ADAM_B2, ADAM_EPS, ADAM_WD, ADAM_STEP = 0.001, 0.9, 0.999, 1e-08, 0.01, 10

ROW_GAIN, ROW_BIAS, ROW_CONV, ROW_LOSS = 0, 1, 2, 5

NT = (((1,), (1,)), ((), ()))
TN = (((0,), (0,)), ((), ()))


V7X_VMEM_BYTES = 64 * 1024 * 1024


def _params(vmem_mib):
    assert vmem_mib * 1024 * 1024 < V7X_VMEM_BYTES
    return pltpu.CompilerParams(vmem_limit_bytes=vmem_mib * 1024 * 1024)


def _dot(a, b, dims=None):
    if dims is None:
        return jnp.dot(a, b, preferred_element_type=F32)
    return lax.dot_general(a, b, dims, preferred_element_type=F32)


def _sigmoid(z):
    return 1.0 / (1.0 + jnp.exp(-z))


def _mesh_pos():
    return lax.axis_index("x"), lax.axis_index("y"), lax.axis_index("c")


def _adamw(w, g, m, v):
    nm = ADAM_B1 * m + (1.0 - ADAM_B1) * g
    nv = ADAM_B2 * v + (1.0 - ADAM_B2) * (g * g)
    m_hat = nm * (1.0 / (1.0 - ADAM_B1 ** ADAM_STEP))
    v_hat = nv * (1.0 / (1.0 - ADAM_B2 ** ADAM_STEP))
    delta = -ADAM_LR * (m_hat / (jnp.sqrt(v_hat) + ADAM_EPS) + ADAM_WD * w)
    return delta, nm, nv


def _gather_weights(w_in_s, w_out_s, conv_s):
    d_model, cw = w_in_s.shape
    rows_out = w_out_s.shape[0]
    n_arr = 3

    def body(win_ref, wout_ref, conv_ref, win_all, wout_all, conv_all, send_sems, recv_sems):
        x, y, c = _mesh_pos()
        me = (x, y, c)
        sibling = (x, y, 1 - c)
        chips = [(1 - x, y), (x, 1 - y), (1 - x, 1 - y)]
        bufs = (win_all, wout_all, conv_all)

        def slot(pos):
            return 4 * pos[0] + 2 * pos[1] + pos[2]

        win_all[slot(me)] = win_ref[...].astype(BF16)
        wout_all[slot(me)] = wout_ref[...].astype(BF16)
        conv_all[slot(me)] = conv_ref[...]

        def copy(a, k, block, to):
            ref = bufs[a].at[slot(block)]
            return pltpu.make_async_remote_copy(
                src_ref=ref, dst_ref=ref, send_sem=send_sems.at[a, k], recv_sem=recv_sems.at[a, k],
                device_id=to, device_id_type=MESH)

        first, passed = [], []
        for a in range(n_arr):
            first.append(copy(a, 0, me, sibling))
            first += [copy(a, 1 + j, me, (*chip, c)) for j, chip in enumerate(chips)]
        for cp in first:
            cp.start()
        for j, chip in enumerate(chips):
            for a in range(n_arr):
                copy(a, 1 + j, (*chip, c), me).wait_recv()
                cp = copy(a, 4 + j, (*chip, c), sibling)
                cp.start()
                passed.append(cp)
        for a in range(n_arr):
            copy(a, 0, sibling, me).wait_recv()
            for j, chip in enumerate(chips):
                copy(a, 4 + j, (*chip, 1 - c), me).wait_recv()
        for cp in first + passed:
            cp.wait_send()

    vmem = pl.BlockSpec(memory_space=pltpu.VMEM)
    return pl.pallas_call(
        body, name="gather_weights",
        out_shape=(jax.ShapeDtypeStruct((N_DEV, d_model, cw), BF16),
                   jax.ShapeDtypeStruct((N_DEV, rows_out, d_model), BF16),
                   jax.ShapeDtypeStruct((N_DEV,) + conv_s.shape, F32)),
        in_specs=[vmem, vmem, vmem], out_specs=(vmem, vmem, vmem),
        scratch_shapes=[pltpu.SemaphoreType.DMA((n_arr, 7)), pltpu.SemaphoreType.DMA((n_arr, 7))],
    )(w_in_s, w_out_s, conv_s)


def _proj(x, win_all, tm):
    seq, d_model = x.shape
    nch, _, cw = win_all.shape

    def body(x_ref, w_ref, o_ref, xt_ref):
        xv = x_ref[...]
        xb = xv.astype(BF16)
        for j in range(nch):
            o_ref[j] = _dot(xb, w_ref[j]).astype(BF16)
        xt_ref[...] = xv.T.astype(BF16)

    return pl.pallas_call(
        body, name="proj", grid=(seq // tm,),
        in_specs=[pl.BlockSpec((tm, d_model), lambda i: (i, 0)),
                  pl.BlockSpec((nch, d_model, cw), lambda i: (0, 0, 0))],
        out_specs=(pl.BlockSpec((nch, tm, cw), lambda i: (0, i, 0)),
                   pl.BlockSpec((None, d_model, tm), lambda i: (i, 0, 0))),
        out_shape=(jax.ShapeDtypeStruct((nch, seq, cw), BF16),
                   jax.ShapeDtypeStruct((seq // tm, d_model, tm), BF16)),
        compiler_params=_params(48),
    )(x, win_all)


def _conv_taps(ext, w_ref, rc):
    u0 = ext[SUBLANES:SUBLANES + rc]
    u1 = pltpu.roll(ext, 1, 0)[SUBLANES:SUBLANES + rc]
    u2 = pltpu.roll(ext, 2, 0)[SUBLANES:SUBLANES + rc]
    return w_ref[2:3, :] * u0 + w_ref[1:2, :] * u1 + w_ref[0:1, :] * u2, u0, u1, u2


def _conv_fwd(proj, conv_full, rc):
    _, seq, cw = proj.shape

    def body(b_ref, c_ref, h_ref, z_ref, w_ref, o_ref, u_scr):
        u_scr[0:SUBLANES, :] = jnp.zeros((SUBLANES, PAIR), F32)

        def fill(r, _):
            base = pl.multiple_of(r * rc, rc)
            rows = pl.ds(base, rc)
            u_scr[pl.ds(base + SUBLANES, rc), :] = c_ref[rows, :].astype(F32) * h_ref[rows, :].astype(F32)
            return 0

        lax.fori_loop(0, seq // rc, fill, 0)

        def out(r, _):
            base = pl.multiple_of(r * rc, rc)
            rows = pl.ds(base, rc)
            ext = u_scr[pl.ds(base, rc + SUBLANES), :]
            y, _, _, _ = _conv_taps(ext, w_ref, rc)
            z = z_ref[rows, :].astype(F32)
            o_ref[rows, :] = (z * _sigmoid(z) * b_ref[rows, :].astype(F32) * y).astype(BF16)
            return 0

        lax.fori_loop(0, seq // rc, out, 0)

    def chunk(j):
        return pl.BlockSpec((None, seq, PAIR), lambda cb, j=j: (j, 0, cb))

    return pl.pallas_call(
        body, name="conv_fwd", grid=(cw // PAIR,),
        in_specs=[chunk(0), chunk(1), chunk(2), chunk(3), pl.BlockSpec((SUBLANES, PAIR), lambda cb: (0, cb))],
        out_specs=pl.BlockSpec((seq, PAIR), lambda cb: (0, cb)),
        out_shape=jax.ShapeDtypeStruct((seq, cw), BF16),
        scratch_shapes=[pltpu.VMEM((seq + SUBLANES, PAIR), F32)],
    )(proj, proj, proj, proj, conv_full)


def _conv_bwd(proj, conv_full, d_mix_conv, rc):
    _, seq, cw = proj.shape

    def body(b_ref, c_ref, h_ref, z_ref, w_ref, g_ref, dp_ref, dw_ref, u_scr, dy_scr):
        u_scr[0:SUBLANES, :] = jnp.zeros((SUBLANES, PAIR), F32)
        dy_scr[seq:seq + SUBLANES, :] = jnp.zeros((SUBLANES, PAIR), F32)

        def fill(r, _):
            base = pl.multiple_of(r * rc, rc)
            rows = pl.ds(base, rc)
            u_scr[pl.ds(base + SUBLANES, rc), :] = c_ref[rows, :].astype(F32) * h_ref[rows, :].astype(F32)
            return 0

        lax.fori_loop(0, seq // rc, fill, 0)

        def gate(r, acc):
            base = pl.multiple_of(r * rc, rc)
            rows = pl.ds(base, rc)
            ext = u_scr[pl.ds(base, rc + SUBLANES), :]
            y, u0, u1, u2 = _conv_taps(ext, w_ref, rc)
            z = z_ref[rows, :].astype(F32)
            b = b_ref[rows, :].astype(F32)
            g = g_ref[rows, :].astype(F32)
            sig = _sigmoid(z)
            dp_ref[3, rows, :] = (g * b * y * (sig * (1.0 + z * (1.0 - sig)))).astype(BF16)
            gs = g * (z * sig)
            dp_ref[0, rows, :] = (gs * y).astype(BF16)
            dy = gs * b
            dy_scr[rows, :] = dy
            a0, a1, a2 = acc
            return (a0 + jnp.sum(dy * u2, axis=0, keepdims=True),
                    a1 + jnp.sum(dy * u1, axis=0, keepdims=True),
                    a2 + jnp.sum(dy * u0, axis=0, keepdims=True))

        zero = jnp.zeros((1, PAIR), F32)
        a0, a1, a2 = lax.fori_loop(0, seq // rc, gate, (zero, zero, zero))
        dw_ref[...] = jnp.zeros((SUBLANES, PAIR), F32)
        dw_ref[0:1, :] = a0
        dw_ref[1:2, :] = a1
        dw_ref[2:3, :] = a2

        def back(r, _):
            base = pl.multiple_of(r * rc, rc)
            rows = pl.ds(base, rc)
            ext = dy_scr[pl.ds(base, rc + SUBLANES), :]
            n = rc + SUBLANES
            d0 = ext[0:rc]
            d1 = pltpu.roll(ext, n - 1, 0)[0:rc]
            d2 = pltpu.roll(ext, n - 2, 0)[0:rc]
            du = w_ref[2:3, :] * d0 + w_ref[1:2, :] * d1 + w_ref[0:1, :] * d2
            dp_ref[1, rows, :] = (du * h_ref[rows, :].astype(F32)).astype(BF16)
            dp_ref[2, rows, :] = (du * c_ref[rows, :].astype(F32)).astype(BF16)
            return 0

        lax.fori_loop(0, seq // rc, back, 0)

    def chunk(j):
        return pl.BlockSpec((None, seq, PAIR), lambda cb, j=j: (j, 0, cb))

    return pl.pallas_call(
        body, name="conv_bwd", grid=(cw // PAIR,),
        in_specs=[chunk(0), chunk(1), chunk(2), chunk(3), pl.BlockSpec((SUBLANES, PAIR), lambda cb: (0, cb)),
                  pl.BlockSpec((seq, PAIR), lambda cb: (0, cb))],
        out_specs=(pl.BlockSpec((4, seq, PAIR), lambda cb: (0, 0, cb)),
                   pl.BlockSpec((SUBLANES, PAIR), lambda cb: (0, cb))),
        out_shape=(jax.ShapeDtypeStruct((4, seq, cw), BF16), jax.ShapeDtypeStruct((SUBLANES, cw), F32)),
        scratch_shapes=[pltpu.VMEM((seq + SUBLANES, PAIR), F32), pltpu.VMEM((seq + SUBLANES, PAIR), F32)],
    )(proj, proj, proj, proj, conv_full, d_mix_conv)


SKIP_CARRY = 104.0
LANE_TOT0, LANE_TOT1, LANE_FIRST = 0, 1, 2
FAST_BLOCKS = 3


def _triangles(t):
    row = lax.broadcasted_iota(jnp.int32, (2 * t, 2 * t), 0)
    col = lax.broadcasted_iota(jnp.int32, (2 * t, 2 * t), 1)
    same = (row < t) == (col < t)
    upper = jnp.logical_and(same, row > col).astype(BF16)
    lower = jnp.logical_and(same, row < col).astype(BF16)
    return jnp.stack([jnp.concatenate([upper, upper], axis=0), jnp.concatenate([lower, lower], axis=0)])


def _pair_masks(t):
    lane = lax.broadcasted_iota(jnp.int32, (t, PAIR), 1)
    qrow = lax.broadcasted_iota(jnp.int32, (t, 2 * t), 0)
    kcol = lax.broadcasted_iota(jnp.int32, (t, 2 * t), 1)
    strict = jnp.where(kcol < t, kcol, kcol - t) < qrow
    return lane, lane < HEAD_DIM, strict


def _by_head(x, head0):
    zero = jnp.zeros_like(x)
    return jnp.concatenate([jnp.where(head0, x, zero), jnp.where(head0, zero, x)], axis=0)


def _split_dot(a, tri):
    hi = a.astype(BF16)
    lo = (a - hi.astype(F32)).astype(BF16)
    return _dot(jnp.concatenate([hi, lo], axis=1), tri)


def _scores(z, strict, masked, upper, t):
    spu = jnp.maximum(z, 0.0) + jnp.log(1.0 + jnp.exp(-jnp.abs(z)))
    ls = z - spu
    sp = jnp.where(strict, spu, 0.0) if masked else spu
    after = _split_dot(sp, upper)
    return ls, ls - after, after[:, 0:1] + sp[:, 0:1], after[:, t:t + 1] + sp[:, t:t + 1]


def _attn_fwd(proj, tri, t, pp):
    _, seq, cw = proj.shape
    scale = HEAD_DIM ** -0.5
    width = pp * PAIR

    def body(q_ref, k_ref, v_ref, za_ref, tri_ref, o_ref, mix_ref, tot_ref):
        i = pl.program_id(1)
        lane, head0, strict = _pair_masks(t)
        upper = tri_ref[0]
        q = q_ref[...] * scale

        def sweep(blocks, state):
            staged = []
            for j, masked in blocks:
                start = pl.multiple_of(j * t, t)
                kb = k_ref[pl.ds(start, t), :]
                vb = v_ref[pl.ds(start, t), :]
                per_pair = []
                for p in range(pp):
                    cols = slice(p * PAIR, (p + 1) * PAIR)
                    z = _dot(q[:, cols], _by_head(kb[:, cols], head0), NT)
                    _, x, t0, t1 = _scores(z, strict, masked, upper, t)
                    per_pair.append((x, t0, t1, _by_head(vb[:, cols], head0)))
                staged.append(per_pair)
            state = list(state)
            for (_, masked), per_pair in zip(blocks, staged):
                for p in range(pp):
                    x, t0, t1, v2 = per_pair[p]
                    (c0, c1), acc = state[p]
                    w = jnp.exp(jnp.concatenate([x[:, :t] - c0, x[:, t:] - c1], axis=1))
                    if masked:
                        w = jnp.where(strict, w, 0.0)
                    state[p] = ((c0 + t0, c1 + t1), acc + _dot(w.astype(BF16), v2))
            return tuple(state)

        def unfinished(state):
            m = state[0][0][0]
            for p in range(pp):
                m = jnp.minimum(m, jnp.minimum(state[p][0][0], state[p][0][1]))
            return jnp.min(m) < SKIP_CARRY

        def step(js):
            state = sweep(((js[0], False),), js[1])
            return js[0] - 1, state, unfinished(state)

        zcol = jnp.zeros((t, 1), F32)
        init = tuple(((zcol, zcol), jnp.zeros((t, PAIR), F32)) for _ in range(pp))
        many = i >= FAST_BLOCKS - 1
        state = lax.cond(
            many,
            lambda: sweep(((i, True),) + tuple((i - b, False) for b in range(1, FAST_BLOCKS)), init),
            lambda: sweep(((i, True),), init))
        j_end, state, _ = lax.while_loop(
            lambda js: jnp.logical_and(js[0] >= 0, js[2]), step,
            (jnp.where(many, i - FAST_BLOCKS, i - 1), state, unfinished(state)))
        first = (j_end + 1).astype(F32)
        za = za_ref[...].astype(F32)
        for p in range(pp):
            (c0, c1), acc = state[p]
            cols = slice(p * PAIR, (p + 1) * PAIR)
            zp = za[:, cols]
            o_ref[:, cols] = acc.astype(BF16)
            mix_ref[:, cols] = (zp * _sigmoid(zp) * acc).astype(BF16)
            tot_ref[:, cols] = jnp.where(lane == LANE_TOT0, c0, jnp.where(lane == LANE_TOT1, c1, first))

    def tile(j):
        return pl.BlockSpec((None, t, width), lambda g, i, j=j: (j, i, g))

    def full(j):
        return pl.BlockSpec((None, seq, width), lambda g, i, j=j: (j, 0, g))

    out_tile = pl.BlockSpec((t, width), lambda g, i: (i, g))
    return pl.pallas_call(
        body, name="attn_fwd", grid=(cw // width, seq // t),
        in_specs=[tile(4), full(5), full(6), tile(7), pl.BlockSpec(tri.shape, lambda g, i: (0, 0, 0))],
        out_specs=(out_tile, out_tile, out_tile),
        out_shape=(jax.ShapeDtypeStruct((seq, cw), BF16), jax.ShapeDtypeStruct((seq, cw), BF16),
                   jax.ShapeDtypeStruct((seq, cw), F32)),
    )(proj, proj, proj, proj, tri)


def _attn_bwd(proj, tri, o, tot, d_mix_attn, t, pp):
    _, seq, cw = proj.shape
    nb = seq // t
    scale = HEAD_DIM ** -0.5
    width = pp * PAIR

    def body(q_ref, k_ref, v_ref, za_ref, tri_ref, o_ref, tot_ref, g_ref, dqz_ref, dkv_ref, dk_acc, dv_acc):
        i = pl.program_id(1)

        @pl.when(i == 0)
        def _():
            dk_acc[...] = jnp.zeros_like(dk_acc)
            dv_acc[...] = jnp.zeros_like(dv_acc)

        _, head0, strict = _pair_masks(t)
        upper, lower = tri_ref[0], tri_ref[1]
        za = za_ref[...].astype(F32)
        g = g_ref[...].astype(F32)
        sig = _sigmoid(za)
        dqz_ref[1] = (g * o_ref[...].astype(F32) * (sig * (1.0 + za * (1.0 - sig)))).astype(BF16)
        do = (g * (za * sig)).astype(BF16)
        q = q_ref[...] * scale
        tot_v = tot_ref[...]
        q2, do2, init = [], [], []
        zcol = jnp.zeros((t, 1), F32)
        for p in range(pp):
            cols = slice(p * PAIR, (p + 1) * PAIR)
            q2.append(_by_head(q[:, cols], head0))
            do2.append(_by_head(do[:, cols], head0))
            tp = tot_v[:, cols]
            init.append(((tp[:, LANE_TOT0:LANE_TOT0 + 1], tp[:, LANE_TOT1:LANE_TOT1 + 1]), (zcol, zcol),
                         jnp.zeros((t, PAIR), F32)))
        first = jnp.clip(tot_v[0:1, LANE_FIRST:LANE_FIRST + 1], 0.0, i.astype(F32)).astype(jnp.int32)[0, 0]

        def sweep(blocks, state):
            staged = []
            for j, masked in blocks:
                start = pl.multiple_of(j * t, t)
                kb = k_ref[pl.ds(start, t), :]
                vb = v_ref[pl.ds(start, t), :]
                per_pair = []
                for p in range(pp):
                    cols = slice(p * PAIR, (p + 1) * PAIR)
                    k2 = _by_head(kb[:, cols], head0)
                    z = _dot(q[:, cols], k2, NT)
                    ls, x, t0, t1 = _scores(z, strict, masked, upper, t)
                    da = _dot(do[:, cols], _by_head(vb[:, cols], head0), NT)
                    per_pair.append((start, k2, ls, x, t0, t1, da))
                staged.append(per_pair)
            state = list(state)
            for (_, masked), per_pair in zip(blocks, staged):
                for p in range(pp):
                    cols = slice(p * PAIR, (p + 1) * PAIR)
                    start, k2, ls, x, t0, t1, da = per_pair[p]
                    (r0, r1), (b0, b1), dq = state[p]
                    r0, r1 = r0 - t0, r1 - t1
                    a = jnp.exp(jnp.concatenate([x[:, :t] - r0, x[:, t:] - r1], axis=1))
                    if masked:
                        a = jnp.where(strict, a, 0.0)
                    gg = a * da
                    y = gg + _split_dot(gg, lower)
                    dz = gg - jnp.exp(ls) * jnp.concatenate([y[:, :t] + b0, y[:, t:] + b1], axis=1)
                    if masked:
                        dz = jnp.where(strict, dz, 0.0)
                    dzb = dz.astype(BF16)
                    ab = a.astype(BF16)
                    dz2 = jnp.concatenate([dzb[:, :t], dzb[:, t:]], axis=0)
                    a2 = jnp.concatenate([ab[:, :t], ab[:, t:]], axis=0)
                    dk_acc[pl.ds(start, t), cols] += _dot(dz2, q2[p], TN)
                    dv_acc[pl.ds(start, t), cols] += _dot(a2, do2[p], TN)
                    state[p] = ((r0, r1), (b0 + y[:, t - 1:t], b1 + y[:, 2 * t - 1:2 * t]), dq + _dot(dzb, k2))
            return tuple(state)

        many = i >= FAST_BLOCKS - 1
        last_single = jnp.where(many, i - (FAST_BLOCKS - 1), i)
        state = lax.fori_loop(first, last_single, lambda j, s: sweep(((j, False),), s), tuple(init))
        state = lax.cond(
            many,
            lambda: sweep(tuple((i - b, False) for b in range(FAST_BLOCKS - 1, 0, -1)) + ((i, True),), state),
            lambda: sweep(((i, True),), state))
        for p in range(pp):
            dqz_ref[0, :, p * PAIR:(p + 1) * PAIR] = (state[p][2] * scale).astype(BF16)

        @pl.when(i == nb - 1)
        def _():
            dkv_ref[0] = dk_acc[...].astype(BF16)
            dkv_ref[1] = dv_acc[...].astype(BF16)

    def tile(j):
        return pl.BlockSpec((None, t, width), lambda g, i, j=j: (j, i, g))

    def full(j):
        return pl.BlockSpec((None, seq, width), lambda g, i, j=j: (j, 0, g))

    flat_tile = pl.BlockSpec((t, width), lambda g, i: (i, g))
    return pl.pallas_call(
        body, name="attn_bwd", grid=(cw // width, nb),
        in_specs=[tile(4), full(5), full(6), tile(7), pl.BlockSpec(tri.shape, lambda g, i: (0, 0, 0)),
                  flat_tile, flat_tile, flat_tile],
        out_specs=(pl.BlockSpec((2, t, width), lambda g, i: (0, i, g)),
                   pl.BlockSpec((2, seq, width), lambda g, i: (0, 0, g))),
        out_shape=(jax.ShapeDtypeStruct((2, seq, cw), BF16), jax.ShapeDtypeStruct((2, seq, cw), BF16)),
        scratch_shapes=[pltpu.VMEM((seq, width), F32), pltpu.VMEM((seq, width), F32)],
        compiler_params=_params(48),
    )(proj, proj, proj, proj, tri, o, tot, d_mix_attn)


def _out_ln(mix_conv, mix_attn, x, target, gain, bias, w_out, tm):
    seq, d_model = x.shape
    cw = mix_conv.shape[1]
    inv_d = 1.0 / d_model

    def body(mc_ref, ma_ref, x_ref, t_ref, gain_ref, bias_ref, w_ref, dr_ref, dmc_ref, dma_ref, gwo_ref, small_ref):
        @pl.when(pl.program_id(0) == 0)
        def _():
            gwo_ref[...] = jnp.zeros_like(gwo_ref)
            small_ref[...] = jnp.zeros_like(small_ref)

        mix = jnp.concatenate([mc_ref[...], ma_ref[...]], axis=1)
        w = w_ref[...]
        r = ALPHA * x_ref[...] + _dot(mix, w)
        mu = jnp.sum(r, axis=1, keepdims=True) * inv_d
        xc = r - mu
        var = jnp.sum(xc * xc, axis=1, keepdims=True) * inv_d
        rstd = lax.rsqrt(var + LN_EPS)
        xhat = xc * rstd
        gain_v = gain_ref[...]
        err = xhat * gain_v + bias_ref[...] - t_ref[...]
        row_loss = jnp.sum(err * err, axis=1, keepdims=True)
        loss = (0.5 * inv_d) * jnp.sum(row_loss, axis=0, keepdims=True)
        dy = err * inv_d
        small_ref[ROW_GAIN:ROW_GAIN + 1, :] += jnp.sum(dy * xhat, axis=0, keepdims=True)
        small_ref[ROW_BIAS:ROW_BIAS + 1, :] += jnp.sum(dy, axis=0, keepdims=True)
        small_ref[ROW_LOSS:ROW_LOSS + 1, :] += jnp.broadcast_to(loss, (1, d_model))
        dxhat = dy * gain_v
        m1 = jnp.sum(dxhat, axis=1, keepdims=True) * inv_d
        m2 = jnp.sum(dxhat * xhat, axis=1, keepdims=True) * inv_d
        dr = rstd * (dxhat - m1 - xhat * m2)
        dr_ref[...] = dr
        drb = dr.astype(BF16)
        dmix = _dot(drb, w, NT)
        dmc_ref[...] = dmix[:, :cw].astype(BF16)
        dma_ref[...] = dmix[:, cw:].astype(BF16)
        gwo_ref[...] += _dot(mix, drb, TN)

    def rows(width):
        return pl.BlockSpec((tm, width), lambda i: (i, 0))

    def whole(shape):
        return pl.BlockSpec(shape, lambda i: (0, 0))

    return pl.pallas_call(
        body, name="out_ln", grid=(seq // tm,),
        in_specs=[rows(cw), rows(cw), rows(d_model), rows(d_model), whole((1, d_model)), whole((1, d_model)),
                  whole((d_model, d_model))],
        out_specs=(rows(d_model), rows(cw), rows(cw), whole((d_model, d_model)), whole((SUBLANES, d_model))),
        out_shape=(jax.ShapeDtypeStruct((seq, d_model), F32), jax.ShapeDtypeStruct((seq, cw), BF16),
                   jax.ShapeDtypeStruct((seq, cw), BF16), jax.ShapeDtypeStruct((d_model, d_model), F32),
                   jax.ShapeDtypeStruct((SUBLANES, d_model), F32)),
        compiler_params=_params(48),
    )(mix_conv, mix_attn, x, target, gain, bias, w_out)


_DP_OF_GROUP = ((0, 0), (0, 1), (0, 2), (0, 3), (1, 0), (2, 0), (2, 1), (1, 1))


def _grad_x(dr, dp_conv, dp_qz, dp_kv, win_all, tm):
    seq, d_model = dr.shape
    nch, _, cw = win_all.shape

    def body(dr_ref, dc_ref, dqz_ref, dkv_ref, w_ref, o_ref):
        parts = (dc_ref, dqz_ref, dkv_ref)
        acc = ALPHA * dr_ref[...]
        for j in range(nch):
            arr, idx = _DP_OF_GROUP[j]
            acc = acc + _dot(parts[arr][idx], w_ref[j], NT)
        o_ref[...] = acc

    def part(n):
        return pl.BlockSpec((n, tm, cw), lambda i: (0, i, 0))

    return pl.pallas_call(
        body, name="grad_x", grid=(seq // tm,),
        in_specs=[pl.BlockSpec((tm, d_model), lambda i: (i, 0)), part(4), part(2), part(2),
                  pl.BlockSpec((nch, d_model, cw), lambda i: (0, 0, 0))],
        out_specs=pl.BlockSpec((tm, d_model), lambda i: (i, 0)),
        out_shape=jax.ShapeDtypeStruct((seq, d_model), F32),
        compiler_params=_params(48),
    )(dr, dp_conv, dp_qz, dp_kv, win_all)


def _grad_w_reduce(xt, dp_parts, gwo, small, w_in, m_in, v_in, w_out, m_out, v_out, row_chunk):
    n_tiles, d_model, tk = xt.shape
    cw = w_in.shape[1]
    rows_out = w_out.shape[0]

    def body(xt_hbm, dpa, dpb, dpc, gwo_ref, small_ref, w_in_ref, m_in_ref, v_in_ref, w_out_ref, m_out_ref, v_out_ref,
             g_in_o, d_in_o, nm_in_o, nv_in_o, g_out_o, d_out_o, nm_out_o, nv_out_o, small_o,
             xt_v, dp_buf, acc, got_in, send_in, recv_in, own_out, got_out, send_out, recv_out, small_all,
             xt_sem, dp_sems, loc_sems, d2d_send, d2d_recv, ici_send, ici_recv, sm_send, sm_recv):
        x, y, c = _mesh_pos()
        me = 4 * x + 2 * y + c
        sibling = (x, y, 1 - c)
        chips = [(1 - x, 1 - y), (1 - x, y), (x, 1 - y)]
        owners = [(*chip, cc) for chip in chips for cc in (1 - c, c)] + [sibling, (x, y, c)]
        group_of = [4 * o[0] + 2 * o[1] + o[2] for o in owners]
        dp_parts_ = (dpa, dpb, dpc)
        dp_groups = [dp_parts_[arr].at[idx] for arr, idx in _DP_OF_GROUP]

        xt_copy = pltpu.make_async_copy(xt_hbm, xt_v, xt_sem)
        xt_copy.start()

        def dp_start(step, tile, buf):
            for k in range(N_DEV):
                @pl.when(group_of[step] == k)
                def _(k=k):
                    pltpu.make_async_copy(dp_groups[k].at[pl.ds(tile * tk, tk), :], dp_buf.at[buf],
                                          dp_sems.at[buf]).start()

        def dp_wait(buf):
            pltpu.make_async_copy(dp_groups[0].at[pl.ds(0, tk), :], dp_buf.at[buf], dp_sems.at[buf]).wait()

        dp_start(0, 0, 0)

        small_all[me] = small_ref[...]
        for d in range(N_DEV):
            @pl.when(d != me)
            def _(d=d):
                pltpu.make_async_remote_copy(
                    src_ref=small_ref, dst_ref=small_all.at[me], send_sem=sm_send.at[d], recv_sem=sm_recv.at[me],
                    device_id=(d // 4, (d // 2) % 2, d % 2), device_id_type=MESH).start()

        def block_out(k):
            return gwo_ref.at[pl.ds(k * rows_out, rows_out), :]

        for k in range(N_DEV):
            s = k // 2

            @pl.when(k % 2 != c)
            def _(k=k, s=s):
                pltpu.make_async_remote_copy(
                    src_ref=block_out(k), dst_ref=got_out.at[s], send_sem=d2d_send.at[1, s],
                    recv_sem=d2d_recv.at[1, s], device_id=sibling, device_id_type=MESH).start()

            @pl.when(k % 2 == c)
            def _(k=k, s=s):
                pltpu.make_async_copy(block_out(k), own_out.at[s], loc_sems.at[s]).start()

        def ici_copy(a, j):
            send, recv = ((send_in, recv_in), (send_out, recv_out))[a]
            return pltpu.make_async_remote_copy(
                src_ref=send.at[j], dst_ref=recv.at[j], send_sem=ici_send.at[a, j], recv_sem=ici_recv.at[a, j],
                device_id=(*chips[j], c), device_id_type=MESH)

        def by_rows(n_rows, fn):
            step = min(row_chunk, n_rows)

            def rows_body(r, _):
                fn(pl.ds(pl.multiple_of(r * step, step), step))
                return 0

            lax.fori_loop(0, n_rows // step, rows_body, 0)

        for s in range(4):
            pltpu.make_async_copy(own_out.at[s], own_out.at[s], loc_sems.at[s]).wait()
            pltpu.make_async_remote_copy(
                src_ref=got_out.at[s], dst_ref=got_out.at[s], send_sem=d2d_send.at[1, s], recv_sem=d2d_recv.at[1, s],
                device_id=sibling, device_id_type=MESH).wait()
        for j, chip in enumerate(chips):
            s = 2 * chip[0] + chip[1]

            def to_send(rows, s=s, j=j):
                send_out[j, rows, :] = (own_out[s, rows, :] + got_out[s, rows, :]).astype(BF16)

            by_rows(rows_out, to_send)
            ici_copy(1, j).start()

        xt_copy.wait()

        def d2d_copy(slot, pair):
            return pltpu.make_async_remote_copy(
                src_ref=acc.at[slot], dst_ref=got_in.at[pair], send_sem=d2d_send.at[0, pair],
                recv_sem=d2d_recv.at[0, pair], device_id=sibling, device_id_type=MESH)

        for step in range(N_DEV):
            slot, pair = step % 2, step // 2
            if step % 2 == 0 and step >= 2:
                d2d_copy(slot, pair - 1).wait_send()

            def tile_body(tile, _, step=step, slot=slot):
                buf = (step * n_tiles + tile) % 2
                dp_wait(buf)

                @pl.when(tile + 1 < n_tiles)
                def _():
                    dp_start(step, tile + 1, 1 - buf)

                if step + 1 < N_DEV:
                    @pl.when(tile + 1 == n_tiles)
                    def _():
                        dp_start(step + 1, 0, 1 - buf)

                part = _dot(xt_v[tile], dp_buf[buf])

                @pl.when(tile == 0)
                def _():
                    acc[slot] = part

                @pl.when(tile > 0)
                def _():
                    acc[slot] += part

                return 0

            lax.fori_loop(0, n_tiles, tile_body, 0)

            if step % 2 == 0:
                d2d_copy(slot, pair).start()
            else:
                d2d_copy(slot, pair).wait_recv()
                if step < N_DEV - 1:
                    def to_send(rows, slot=slot, pair=pair):
                        send_in[pair, rows, :] = (acc[slot, rows, :] + got_in[pair, rows, :]).astype(BF16)

                    by_rows(d_model, to_send)
                    ici_copy(0, pair).start()

        for d in range(N_DEV):
            @pl.when(d != me)
            def _(d=d):
                pltpu.make_async_remote_copy(
                    src_ref=small_ref, dst_ref=small_all.at[d], send_sem=sm_send.at[d], recv_sem=sm_recv.at[d],
                    device_id=(d // 4, (d // 2) % 2, d % 2), device_id_type=MESH).wait()
        total = small_all[0]
        for d in range(1, N_DEV):
            total = total + small_all[d]
        small_o[...] = total

        mine = 2 * x + y
        last = (N_DEV - 1) % 2

        def finish(a, n_rows, chip_sum, outs):
            recv = (recv_in, recv_out)[a]
            for j in range(3):
                ici_copy(a, j).wait()
            g_o, d_o, nm_o, nv_o, w_r, m_r, v_r = outs

            def update(rows):
                g = chip_sum(rows)
                for j in range(3):
                    g = g + recv[j, rows, :].astype(F32)
                delta, nm, nv = _adamw(w_r[rows, :], g, m_r[rows, :], v_r[rows, :])
                g_o[rows, :] = g
                d_o[rows, :] = delta
                nm_o[rows, :] = nm
                nv_o[rows, :] = nv

            by_rows(n_rows, update)

        finish(1, rows_out, lambda rows: own_out[mine, rows, :] + got_out[mine, rows, :],
               (g_out_o, d_out_o, nm_out_o, nv_out_o, w_out_ref, m_out_ref, v_out_ref))
        finish(0, d_model, lambda rows: acc[last, rows, :] + got_in[N_DEV // 2 - 1, rows, :],
               (g_in_o, d_in_o, nm_in_o, nv_in_o, w_in_ref, m_in_ref, v_in_ref))
        d2d_copy(0, N_DEV // 2 - 1).wait_send()

    vmem = pl.BlockSpec(memory_space=pltpu.VMEM)
    hbm = pl.BlockSpec(memory_space=pl.ANY)
    shard_in = jax.ShapeDtypeStruct((d_model, cw), F32)
    shard_out = jax.ShapeDtypeStruct((rows_out, d_model), F32)
    return pl.pallas_call(
        body, name="grad_w_reduce",
        in_specs=[hbm] * 5 + [vmem] * 7,
        out_specs=(vmem,) * 9,
        out_shape=(shard_in,) * 4 + (shard_out,) * 4 + (jax.ShapeDtypeStruct(small.shape, F32),),
        scratch_shapes=[
            pltpu.VMEM((n_tiles, d_model, tk), BF16), pltpu.VMEM((2, tk, cw), BF16), pltpu.VMEM((2, d_model, cw), F32),
            pltpu.VMEM((4, d_model, cw), F32), pltpu.VMEM((3, d_model, cw), BF16), pltpu.VMEM((3, d_model, cw), BF16),
            pltpu.VMEM((4, rows_out, d_model), F32), pltpu.VMEM((4, rows_out, d_model), F32),
            pltpu.VMEM((3, rows_out, d_model), BF16), pltpu.VMEM((3, rows_out, d_model), BF16),
            pltpu.VMEM((N_DEV,) + small.shape, F32),
            pltpu.SemaphoreType.DMA, pltpu.SemaphoreType.DMA((2,)), pltpu.SemaphoreType.DMA((4,)),
            pltpu.SemaphoreType.DMA((2, 4)), pltpu.SemaphoreType.DMA((2, 4)),
            pltpu.SemaphoreType.DMA((2, 3)), pltpu.SemaphoreType.DMA((2, 3)),
            pltpu.SemaphoreType.DMA((N_DEV,)), pltpu.SemaphoreType.DMA((N_DEV,)),
        ],
        compiler_params=_params(56),
    )(xt, *dp_parts, gwo, small, w_in, m_in, v_in, w_out, m_out, v_out)


def _small_update(grads, weights, ms, vs):
    n = len(grads)

    def body(*refs):
        g_refs, w_refs, m_refs, v_refs = (refs[i * n:(i + 1) * n] for i in range(4))
        outs = refs[4 * n:]
        for i in range(n):
            delta, nm, nv = _adamw(w_refs[i][...], g_refs[i][...], m_refs[i][...], v_refs[i][...])
            outs[3 * i][...] = delta
            outs[3 * i + 1][...] = nm
            outs[3 * i + 2][...] = nv

    vmem = pl.BlockSpec(memory_space=pltpu.VMEM)
    out_shape = []
    for w in weights:
        out_shape += [jax.ShapeDtypeStruct(w.shape, F32)] * 3
    return pl.pallas_call(
        body, name="small_update", in_specs=[vmem] * (4 * n), out_specs=(vmem,) * (3 * n), out_shape=tuple(out_shape),
    )(*grads, *weights, *ms, *vs)


def _tile_sizes(seq):
    return dict(tm=min(512, seq), t_ln=min(256, seq), t_attn=min(128, seq), rc=min(256, seq), pairs=4)


def kernel(x, w_in, conv_w, w_out, ln_gain, ln_bias, loss_target, m_w_in, m_conv_w, m_w_out, m_ln_gain, m_ln_bias,
           v_w_in, v_conv_w, v_w_out, v_ln_gain, v_ln_bias):
    assert x.shape[0] == 1 and w_in.shape[0] == 1, "one sequence per device, depth 1"
    _, seq, d_model = x.shape
    cw = w_in.shape[2]
    conv_k, conv_cols = conv_w.shape[1], conv_w.shape[2]
    rows_out = w_out.shape[1]
    assert cw == d_model // 2 and cw % PAIR == 0 and conv_cols * N_DEV == cw and rows_out * N_DEV == d_model
    ts = _tile_sizes(seq)

    x2 = x.reshape(seq, d_model)
    target = loss_target.reshape(seq, d_model)
    me = 4 * lax.axis_index("x") + 2 * lax.axis_index("y") + lax.axis_index("c")

    conv_pad = jnp.pad(conv_w[0], ((0, SUBLANES - conv_k), (0, PAIR - conv_cols)))
    win_all, wout_all, conv_all = _gather_weights(w_in[0], w_out[0], conv_pad)
    w_out_full = wout_all.reshape(d_model, d_model)
    conv_full = conv_all[:, :conv_k, :conv_cols].transpose(1, 0, 2).reshape(conv_k, cw)
    conv_full = jnp.pad(conv_full, ((0, SUBLANES - conv_k), (0, 0)))

    proj, xt = _proj(x2, win_all, ts["tm"])
    mix_conv = _conv_fwd(proj, conv_full, ts["rc"])
    pairs = min(ts["pairs"], cw // PAIR)
    tri = _triangles(ts["t_attn"])
    o, mix_attn, tot = _attn_fwd(proj, tri, ts["t_attn"], pairs)
    dr, d_mix_conv, d_mix_attn, gwo, small = _out_ln(mix_conv, mix_attn, x2, target, ln_gain, ln_bias, w_out_full,
                                                     ts["t_ln"])
    dp_conv, d_taps = _conv_bwd(proj, conv_full, d_mix_conv, ts["rc"])
    dp_qz, dp_kv = _attn_bwd(proj, tri, o, tot, d_mix_attn, ts["t_attn"], pairs)
    grad_x = _grad_x(dr, dp_conv, dp_qz, dp_kv, win_all, ts["tm"])
    small = small.at[ROW_CONV:ROW_CONV + conv_k, :cw].set(d_taps[:conv_k])
    (g_in, d_in, nm_in, nv_in, g_out, d_out, nm_out, nv_out, small_sum) = _grad_w_reduce(
        xt, (dp_conv, dp_qz, dp_kv), gwo, small, w_in[0], m_w_in[0], v_w_in[0], w_out[0], m_w_out[0], v_w_out[0], 128)

    loss = small_sum[ROW_LOSS, 0]
    g_gain = small_sum[ROW_GAIN:ROW_GAIN + 1]
    g_bias = small_sum[ROW_BIAS:ROW_BIAS + 1]
    g_conv = lax.dynamic_slice(small_sum, (ROW_CONV, me * conv_cols), (conv_k, conv_cols))
    upd = _small_update((g_conv, g_gain, g_bias), (conv_w[0], ln_gain, ln_bias),
                        (m_conv_w[0], m_ln_gain, m_ln_bias), (v_conv_w[0], v_ln_gain, v_ln_bias))
    d_conv, nm_conv, nv_conv, d_gain, nm_gain, nv_gain, d_bias, nm_bias, nv_bias = upd

    lead = lambda a: a[None]
    return (loss, grad_x.reshape(1, seq, d_model), lead(g_in), lead(g_conv), lead(g_out), g_gain, g_bias,
            lead(d_in), lead(d_conv), lead(d_out), d_gain, d_bias,
            lead(nm_in), lead(nm_conv), lead(nm_out), nm_gain, nm_bias,
            lead(nv_in), lead(nv_conv), lead(nv_out), nv_gain, nv_bias)
```

```python
import functools

import jax
import jax.numpy as jnp
from jax import lax
from jax.experimental import pallas as pl
from jax.experimental.pallas import tpu as pltpu

F32 = jnp.float32
BF16 = jnp.bfloat16
MESH = pl.DeviceIdType.MESH

N_DEV = 8
HEAD_DIM = 64
PAIR = 128
SUBLANES = 8
LN_EPS = 1e-5
ALPHA = 2.0 ** 0.25
ADAM_LR, ADAM_B1, ADAM_B2, ADAM_EPS, ADAM_WD, ADAM_STEP = 0.001, 0.9, 0.999, 1e-08, 0.01, 10

ROW_GAIN, ROW_BIAS, ROW_CONV, ROW_LOSS = 0, 1, 2, 5

NT = (((1,), (1,)), ((), ()))
TN = (((0,), (0,)), ((), ()))


V7X_VMEM_BYTES = 64 * 1024 * 1024


def _params(vmem_mib):
    assert vmem_mib * 1024 * 1024 < V7X_VMEM_BYTES
    return pltpu.CompilerParams(vmem_limit_bytes=vmem_mib * 1024 * 1024)


def _dot(a, b, dims=None):
    if dims is None:
        return jnp.dot(a, b, preferred_element_type=F32)
    return lax.dot_general(a, b, dims, preferred_element_type=F32)


def _sigmoid(z):
    return 1.0 / (1.0 + jnp.exp(-z))


def _mesh_pos():
    return lax.axis_index("x"), lax.axis_index("y"), lax.axis_index("c")


def _adamw(w, g, m, v):
    nm = ADAM_B1 * m + (1.0 - ADAM_B1) * g
    nv = ADAM_B2 * v + (1.0 - ADAM_B2) * (g * g)
    m_hat = nm * (1.0 / (1.0 - ADAM_B1 ** ADAM_STEP))
    v_hat = nv * (1.0 / (1.0 - ADAM_B2 ** ADAM_STEP))
    delta = -ADAM_LR * (m_hat / (jnp.sqrt(v_hat) + ADAM_EPS) + ADAM_WD * w)
    return delta, nm, nv


def _gather_proj(x, w_in_s, w_out_s, conv_s, tm):
    seq, d_model = x.shape
    cw = w_in_s.shape[1]
    rows_out = w_out_s.shape[0]
    n_tiles = seq // tm
    n_arr = 3

    def body(x_hbm, win_ref, wout_ref, conv_ref, proj_hbm, xt_hbm, win_all, wout_all, conv_all,
             xb, x_stage, o_stage, xt_stage, x_sems, o_sems, xt_sems, send_sems, recv_sems):
        x, y, c = _mesh_pos()
        me = (x, y, c)
        sibling = (x, y, 1 - c)
        chips = [(1 - x, y), (x, 1 - y), (1 - x, 1 - y)]
        bufs = (win_all, wout_all, conv_all)

        def slot(pos):
            return 4 * pos[0] + 2 * pos[1] + pos[2]

        def x_copy(tile, buf):
            return pltpu.make_async_copy(x_hbm.at[pl.ds(tile * tm, tm), :], x_stage.at[buf], x_sems.at[buf])

        x_copy(0, 0).start()
        win_all[slot(me)] = win_ref[...].astype(BF16)
        wout_all[slot(me)] = wout_ref[...].astype(BF16)
        conv_all[slot(me)] = conv_ref[...]

        def copy(a, k, block, to):
            ref = bufs[a].at[slot(block)]
            return pltpu.make_async_remote_copy(
                src_ref=ref, dst_ref=ref, send_sem=send_sems.at[a, k], recv_sem=recv_sems.at[a, k],
                device_id=to, device_id_type=MESH)

        first, passed = [], []
        for a in range(n_arr):
            first.append(copy(a, 0, me, sibling))
            first += [copy(a, 1 + j, me, (*chip, c)) for j, chip in enumerate(chips)]
        for cp in first:
            cp.start()

        def o_copy(group, tile, buf):
            return pltpu.make_async_copy(o_stage.at[buf], proj_hbm.at[group, pl.ds(tile * tm, tm), :], o_sems.at[buf])

        def xt_copy(tile, buf):
            return pltpu.make_async_copy(xt_stage.at[buf], xt_hbm.at[tile], xt_sems.at[buf])

        def project(order, group, first_pass):
            def tile_body(tile, _):
                if first_pass:
                    buf = tile % 2
                    x_copy(tile, buf).wait()

                    @pl.when(tile + 1 < n_tiles)
                    def _():
                        x_copy(tile + 1, 1 - buf).start()

                    xv = x_stage[buf]
                    xb[tile] = xv.astype(BF16)

                    @pl.when(tile >= 2)
                    def _():
                        xt_copy(tile - 2, buf).wait()

                    xt_stage[buf] = xv.T.astype(BF16)
                    xt_copy(tile, buf).start()
                count = order * n_tiles + tile
                obuf = count % 2

                @pl.when(count >= 2)
                def _():
                    o_copy(group, tile, obuf).wait()

                o_stage[obuf] = _dot(xb[tile], win_all[group]).astype(BF16)
                o_copy(group, tile, obuf).start()
                return 0

            lax.fori_loop(0, n_tiles, tile_body, 0)

        def pass_on(j):
            for a in range(n_arr):
                copy(a, 1 + j, (*chips[j], c), me).wait_recv()
                cp = copy(a, 4 + j, (*chips[j], c), sibling)
                cp.start()
                passed.append(cp)

        def from_sibling(k):
            for a in range(n_arr):
                copy(a, k, sibling, me).wait_recv()

        project(0, slot(me), True)
        from_sibling(0)
        project(1, slot(sibling), False)
        pass_on(0)
        project(2, slot((*chips[0], c)), False)
        pass_on(1)
        project(3, slot((*chips[1], c)), False)
        from_sibling(4)
        project(4, slot((*chips[0], 1 - c)), False)
        from_sibling(5)
        project(5, slot((*chips[1], 1 - c)), False)
        pass_on(2)
        project(6, slot((*chips[2], c)), False)
        from_sibling(6)
        project(7, slot((*chips[2], 1 - c)), False)

        for buf in range(2):
            o_copy(0, 0, buf).wait()
            xt_copy(0, buf).wait()
        for cp in first + passed:
            cp.wait_send()

    vmem = pl.BlockSpec(memory_space=pltpu.VMEM)
    hbm = pl.BlockSpec(memory_space=pl.ANY)
    return pl.pallas_call(
        body, name="gather_proj",
        out_shape=(jax.ShapeDtypeStruct((N_DEV, seq, cw), BF16),
                   jax.ShapeDtypeStruct((n_tiles, d_model, tm), BF16),
                   jax.ShapeDtypeStruct((N_DEV, d_model, cw), BF16),
                   jax.ShapeDtypeStruct((N_DEV, rows_out, d_model), BF16),
                   jax.ShapeDtypeStruct((N_DEV,) + conv_s.shape, F32)),
        in_specs=[hbm, vmem, vmem, vmem], out_specs=(hbm, hbm, vmem, vmem, vmem),
        scratch_shapes=[
            pltpu.VMEM((n_tiles, tm, d_model), BF16), pltpu.VMEM((2, tm, d_model), F32),
            pltpu.VMEM((2, tm, cw), BF16), pltpu.VMEM((2, d_model, tm), BF16),
            pltpu.SemaphoreType.DMA((2,)), pltpu.SemaphoreType.DMA((2,)), pltpu.SemaphoreType.DMA((2,)),
            pltpu.SemaphoreType.DMA((n_arr, 7)), pltpu.SemaphoreType.DMA((n_arr, 7))],
        compiler_params=_params(48),
    )(x, w_in_s, w_out_s, conv_s)


def _conv_taps(ext, w_ref, rc):
    u0 = ext[SUBLANES:SUBLANES + rc]
    u1 = pltpu.roll(ext, 1, 0)[SUBLANES:SUBLANES + rc]
    u2 = pltpu.roll(ext, 2, 0)[SUBLANES:SUBLANES + rc]
    return w_ref[2:3, :] * u0 + w_ref[1:2, :] * u1 + w_ref[0:1, :] * u2, u0, u1, u2


def _conv_fwd(proj, conv_full, rc):
    _, seq, cw = proj.shape

    def body(b_ref, c_ref, h_ref, z_ref, w_ref, o_ref, u_scr):
        u_scr[0:SUBLANES, :] = jnp.zeros((SUBLANES, PAIR), F32)

        def fill(r, _):
            base = pl.multiple_of(r * rc, rc)
            rows = pl.ds(base, rc)
            u_scr[pl.ds(base + SUBLANES, rc), :] = c_ref[rows, :].astype(F32) * h_ref[rows, :].astype(F32)
            return 0

        lax.fori_loop(0, seq // rc, fill, 0)

        def out(r, _):
            base = pl.multiple_of(r * rc, rc)
            rows = pl.ds(base, rc)
            ext = u_scr[pl.ds(base, rc + SUBLANES), :]
            y, _, _, _ = _conv_taps(ext, w_ref, rc)
            z = z_ref[rows, :].astype(F32)
            o_ref[rows, :] = (z * _sigmoid(z) * b_ref[rows, :].astype(F32) * y).astype(BF16)
            return 0

        lax.fori_loop(0, seq // rc, out, 0)

    def chunk(j):
        return pl.BlockSpec((None, seq, PAIR), lambda cb, j=j: (j, 0, cb))

    return pl.pallas_call(
        body, name="conv_fwd", grid=(cw // PAIR,),
        in_specs=[chunk(0), chunk(1), chunk(2), chunk(3), pl.BlockSpec((SUBLANES, PAIR), lambda cb: (0, cb))],
        out_specs=pl.BlockSpec((seq, PAIR), lambda cb: (0, cb)),
        out_shape=jax.ShapeDtypeStruct((seq, cw), BF16),
        scratch_shapes=[pltpu.VMEM((seq + SUBLANES, PAIR), F32)],
    )(proj, proj, proj, proj, conv_full)


def _conv_bwd(proj, conv_full, d_mix_conv, rc):
    _, seq, cw = proj.shape

    def body(b_ref, c_ref, h_ref, z_ref, w_ref, g_ref, dp_ref, dw_ref, u_scr, dy_scr):
        u_scr[0:SUBLANES, :] = jnp.zeros((SUBLANES, PAIR), F32)
        dy_scr[seq:seq + SUBLANES, :] = jnp.zeros((SUBLANES, PAIR), F32)

        def fill(r, _):
            base = pl.multiple_of(r * rc, rc)
            rows = pl.ds(base, rc)
            u_scr[pl.ds(base + SUBLANES, rc), :] = c_ref[rows, :].astype(F32) * h_ref[rows, :].astype(F32)
            return 0

        lax.fori_loop(0, seq // rc, fill, 0)

        def gate(r, acc):
            base = pl.multiple_of(r * rc, rc)
            rows = pl.ds(base, rc)
            ext = u_scr[pl.ds(base, rc + SUBLANES), :]
            y, u0, u1, u2 = _conv_taps(ext, w_ref, rc)
            z = z_ref[rows, :].astype(F32)
            b = b_ref[rows, :].astype(F32)
            g = g_ref[rows, :].astype(F32)
            sig = _sigmoid(z)
            dp_ref[3, rows, :] = (g * b * y * (sig * (1.0 + z * (1.0 - sig)))).astype(BF16)
            gs = g * (z * sig)
            dp_ref[0, rows, :] = (gs * y).astype(BF16)
            dy = gs * b
            dy_scr[rows, :] = dy
            a0, a1, a2 = acc
            return (a0 + jnp.sum(dy * u2, axis=0, keepdims=True),
                    a1 + jnp.sum(dy * u1, axis=0, keepdims=True),
                    a2 + jnp.sum(dy * u0, axis=0, keepdims=True))

        zero = jnp.zeros((1, PAIR), F32)
        a0, a1, a2 = lax.fori_loop(0, seq // rc, gate, (zero, zero, zero))
        dw_ref[...] = jnp.zeros((SUBLANES, PAIR), F32)
        dw_ref[0:1, :] = a0
        dw_ref[1:2, :] = a1
        dw_ref[2:3, :] = a2

        def back(r, _):
            base = pl.multiple_of(r * rc, rc)
            rows = pl.ds(base, rc)
            ext = dy_scr[pl.ds(base, rc + SUBLANES), :]
            n = rc + SUBLANES
            d0 = ext[0:rc]
            d1 = pltpu.roll(ext, n - 1, 0)[0:rc]
            d2 = pltpu.roll(ext, n - 2, 0)[0:rc]
            du = w_ref[2:3, :] * d0 + w_ref[1:2, :] * d1 + w_ref[0:1, :] * d2
            dp_ref[1, rows, :] = (du * h_ref[rows, :].astype(F32)).astype(BF16)
            dp_ref[2, rows, :] = (du * c_ref[rows, :].astype(F32)).astype(BF16)
            return 0

        lax.fori_loop(0, seq // rc, back, 0)

    def chunk(j):
        return pl.BlockSpec((None, seq, PAIR), lambda cb, j=j: (j, 0, cb))

    return pl.pallas_call(
        body, name="conv_bwd", grid=(cw // PAIR,),
        in_specs=[chunk(0), chunk(1), chunk(2), chunk(3), pl.BlockSpec((SUBLANES, PAIR), lambda cb: (0, cb)),
                  pl.BlockSpec((seq, PAIR), lambda cb: (0, cb))],
        out_specs=(pl.BlockSpec((4, seq, PAIR), lambda cb: (0, 0, cb)),
                   pl.BlockSpec((SUBLANES, PAIR), lambda cb: (0, cb))),
        out_shape=(jax.ShapeDtypeStruct((4, seq, cw), BF16), jax.ShapeDtypeStruct((SUBLANES, cw), F32)),
        scratch_shapes=[pltpu.VMEM((seq + SUBLANES, PAIR), F32), pltpu.VMEM((seq + SUBLANES, PAIR), F32)],
    )(proj, proj, proj, proj, conv_full, d_mix_conv)


SKIP_CARRY = 104.0
LANE_TOT0, LANE_TOT1, LANE_FIRST = 0, 1, 2
FAST_BLOCKS = 3


def _triangles(t):
    row = lax.broadcasted_iota(jnp.int32, (2 * t, 2 * t), 0)
    col = lax.broadcasted_iota(jnp.int32, (2 * t, 2 * t), 1)
    same = (row < t) == (col < t)
    upper = jnp.logical_and(same, row > col).astype(BF16)
    lower = jnp.logical_and(same, row < col).astype(BF16)
    return jnp.stack([jnp.concatenate([upper, upper], axis=0), jnp.concatenate([lower, lower], axis=0)])


def _pair_masks(t):
    lane = lax.broadcasted_iota(jnp.int32, (t, PAIR), 1)
    qrow = lax.broadcasted_iota(jnp.int32, (t, 2 * t), 0)
    kcol = lax.broadcasted_iota(jnp.int32, (t, 2 * t), 1)
    strict = jnp.where(kcol < t, kcol, kcol - t) < qrow
    return lane, lane < HEAD_DIM, strict


def _by_head(x, head0):
    zero = jnp.zeros_like(x)
    return jnp.concatenate([jnp.where(head0, x, zero), jnp.where(head0, zero, x)], axis=0)


def _split_dot(a, tri):
    hi = a.astype(BF16)
    lo = (a - hi.astype(F32)).astype(BF16)
    return _dot(jnp.concatenate([hi, lo], axis=1), tri)


def _scores(z, strict, masked, upper, t):
    spu = jnp.maximum(z, 0.0) + jnp.log(1.0 + jnp.exp(-jnp.abs(z)))
    ls = z - spu
    sp = jnp.where(strict, spu, 0.0) if masked else spu
    after = _split_dot(sp, upper)
    return ls, ls - after, after[:, 0:1] + sp[:, 0:1], after[:, t:t + 1] + sp[:, t:t + 1]


def _attn_fwd(proj, tri, t, pp):
    _, seq, cw = proj.shape
    scale = HEAD_DIM ** -0.5
    width = pp * PAIR

    def body(q_ref, k_ref, v_ref, za_ref, tri_ref, o_ref, mix_ref, tot_ref):
        i = pl.program_id(1)
        lane, head0, strict = _pair_masks(t)
        upper = tri_ref[0]
        q = q_ref[...] * scale

        def sweep(blocks, state):
            staged = []
            for j, masked in blocks:
                start = pl.multiple_of(j * t, t)
                kb = k_ref[pl.ds(start, t), :]
                vb = v_ref[pl.ds(start, t), :]
                per_pair = []
                for p in range(pp):
                    cols = slice(p * PAIR, (p + 1) * PAIR)
                    z = _dot(q[:, cols], _by_head(kb[:, cols], head0), NT)
                    _, x, t0, t1 = _scores(z, strict, masked, upper, t)
                    per_pair.append((x, t0, t1, _by_head(vb[:, cols], head0)))
                staged.append(per_pair)
            state = list(state)
            for (_, masked), per_pair in zip(blocks, staged):
                for p in range(pp):
                    x, t0, t1, v2 = per_pair[p]
                    (c0, c1), acc = state[p]
                    w = jnp.exp(jnp.concatenate([x[:, :t] - c0, x[:, t:] - c1], axis=1))
                    if masked:
                        w = jnp.where(strict, w, 0.0)
                    state[p] = ((c0 + t0, c1 + t1), acc + _dot(w.astype(BF16), v2))
            return tuple(state)

        def unfinished(state):
            m = state[0][0][0]
            for p in range(pp):
                m = jnp.minimum(m, jnp.minimum(state[p][0][0], state[p][0][1]))
            return jnp.min(m) < SKIP_CARRY

        def step(js):
            state = sweep(((js[0], False),), js[1])
            return js[0] - 1, state, unfinished(state)

        zcol = jnp.zeros((t, 1), F32)
        init = tuple(((zcol, zcol), jnp.zeros((t, PAIR), F32)) for _ in range(pp))
        many = i >= FAST_BLOCKS - 1
        state = lax.cond(
            many,
            lambda: sweep(((i, True),) + tuple((i - b, False) for b in range(1, FAST_BLOCKS)), init),
            lambda: sweep(((i, True),), init))
        j_end, state, _ = lax.while_loop(
            lambda js: jnp.logical_and(js[0] >= 0, js[2]), step,
            (jnp.where(many, i - FAST_BLOCKS, i - 1), state, unfinished(state)))
        first = (j_end + 1).astype(F32)
        za = za_ref[...].astype(F32)
        for p in range(pp):
            (c0, c1), acc = state[p]
            cols = slice(p * PAIR, (p + 1) * PAIR)
            zp = za[:, cols]
            o_ref[:, cols] = acc.astype(BF16)
            mix_ref[:, cols] = (zp * _sigmoid(zp) * acc).astype(BF16)
            tot_ref[:, cols] = jnp.where(lane == LANE_TOT0, c0, jnp.where(lane == LANE_TOT1, c1, first))

    def tile(j):
        return pl.BlockSpec((None, t, width), lambda g, i, j=j: (j, i, g))

    def full(j):
        return pl.BlockSpec((None, seq, width), lambda g, i, j=j: (j, 0, g))

    out_tile = pl.BlockSpec((t, width), lambda g, i: (i, g))
    return pl.pallas_call(
        body, name="attn_fwd", grid=(cw // width, seq // t),
        in_specs=[tile(4), full(5), full(6), tile(7), pl.BlockSpec(tri.shape, lambda g, i: (0, 0, 0))],
        out_specs=(out_tile, out_tile, out_tile),
        out_shape=(jax.ShapeDtypeStruct((seq, cw), BF16), jax.ShapeDtypeStruct((seq, cw), BF16),
                   jax.ShapeDtypeStruct((seq, cw), F32)),
    )(proj, proj, proj, proj, tri)


def _attn_bwd(proj, tri, o, tot, d_mix_attn, t, pp):
    _, seq, cw = proj.shape
    nb = seq // t
    scale = HEAD_DIM ** -0.5
    width = pp * PAIR

    def body(q_ref, k_ref, v_ref, za_ref, tri_ref, o_ref, tot_ref, g_ref, dqz_ref, dkv_ref, dk_acc, dv_acc):
        i = pl.program_id(1)

        @pl.when(i == 0)
        def _():
            dk_acc[...] = jnp.zeros_like(dk_acc)
            dv_acc[...] = jnp.zeros_like(dv_acc)

        _, head0, strict = _pair_masks(t)
        upper, lower = tri_ref[0], tri_ref[1]
        za = za_ref[...].astype(F32)
        g = g_ref[...].astype(F32)
        sig = _sigmoid(za)
        dqz_ref[1] = (g * o_ref[...].astype(F32) * (sig * (1.0 + za * (1.0 - sig)))).astype(BF16)
        do = (g * (za * sig)).astype(BF16)
        q = q_ref[...] * scale
        tot_v = tot_ref[...]
        q2, do2, init = [], [], []
        zcol = jnp.zeros((t, 1), F32)
        for p in range(pp):
            cols = slice(p * PAIR, (p + 1) * PAIR)
            q2.append(_by_head(q[:, cols], head0))
            do2.append(_by_head(do[:, cols], head0))
            tp = tot_v[:, cols]
            init.append(((tp[:, LANE_TOT0:LANE_TOT0 + 1], tp[:, LANE_TOT1:LANE_TOT1 + 1]), (zcol, zcol),
                         jnp.zeros((t, PAIR), F32)))
        first = jnp.clip(tot_v[0:1, LANE_FIRST:LANE_FIRST + 1], 0.0, i.astype(F32)).astype(jnp.int32)[0, 0]

        def sweep(blocks, state):
            staged = []
            for j, masked in blocks:
                start = pl.multiple_of(j * t, t)
                kb = k_ref[pl.ds(start, t), :]
                vb = v_ref[pl.ds(start, t), :]
                per_pair = []
                for p in range(pp):
                    cols = slice(p * PAIR, (p + 1) * PAIR)
                    k2 = _by_head(kb[:, cols], head0)
                    z = _dot(q[:, cols], k2, NT)
                    ls, x, t0, t1 = _scores(z, strict, masked, upper, t)
                    da = _dot(do[:, cols], _by_head(vb[:, cols], head0), NT)
                    per_pair.append((start, k2, ls, x, t0, t1, da))
                staged.append(per_pair)
            state = list(state)
            for (_, masked), per_pair in zip(blocks, staged):
                for p in range(pp):
                    cols = slice(p * PAIR, (p + 1) * PAIR)
                    start, k2, ls, x, t0, t1, da = per_pair[p]
                    (r0, r1), (b0, b1), dq = state[p]
                    r0, r1 = r0 - t0, r1 - t1
                    a = jnp.exp(jnp.concatenate([x[:, :t] - r0, x[:, t:] - r1], axis=1))
                    if masked:
                        a = jnp.where(strict, a, 0.0)
                    gg = a * da
                    y = gg + _split_dot(gg, lower)
                    dz = gg - jnp.exp(ls) * jnp.concatenate([y[:, :t] + b0, y[:, t:] + b1], axis=1)
                    if masked:
                        dz = jnp.where(strict, dz, 0.0)
                    dzb = dz.astype(BF16)
                    ab = a.astype(BF16)
                    dz2 = jnp.concatenate([dzb[:, :t], dzb[:, t:]], axis=0)
                    a2 = jnp.concatenate([ab[:, :t], ab[:, t:]], axis=0)
                    dk_acc[pl.ds(start, t), cols] += _dot(dz2, q2[p], TN)
                    dv_acc[pl.ds(start, t), cols] += _dot(a2, do2[p], TN)
                    state[p] = ((r0, r1), (b0 + y[:, t - 1:t], b1 + y[:, 2 * t - 1:2 * t]), dq + _dot(dzb, k2))
            return tuple(state)

        many = i >= FAST_BLOCKS - 1
        last_single = jnp.where(many, i - (FAST_BLOCKS - 1), i)
        state = lax.fori_loop(first, last_single, lambda j, s: sweep(((j, False),), s), tuple(init))
        state = lax.cond(
            many,
            lambda: sweep(tuple((i - b, False) for b in range(FAST_BLOCKS - 1, 0, -1)) + ((i, True),), state),
            lambda: sweep(((i, True),), state))
        for p in range(pp):
            dqz_ref[0, :, p * PAIR:(p + 1) * PAIR] = (state[p][2] * scale).astype(BF16)

        @pl.when(i == nb - 1)
        def _():
            dkv_ref[0] = dk_acc[...].astype(BF16)
            dkv_ref[1] = dv_acc[...].astype(BF16)

    def tile(j):
        return pl.BlockSpec((None, t, width), lambda g, i, j=j: (j, i, g))

    def full(j):
        return pl.BlockSpec((None, seq, width), lambda g, i, j=j: (j, 0, g))

    flat_tile = pl.BlockSpec((t, width), lambda g, i: (i, g))
    return pl.pallas_call(
        body, name="attn_bwd", grid=(cw // width, nb),
        in_specs=[tile(4), full(5), full(6), tile(7), pl.BlockSpec(tri.shape, lambda g, i: (0, 0, 0)),
                  flat_tile, flat_tile, flat_tile],
        out_specs=(pl.BlockSpec((2, t, width), lambda g, i: (0, i, g)),
                   pl.BlockSpec((2, seq, width), lambda g, i: (0, 0, g))),
        out_shape=(jax.ShapeDtypeStruct((2, seq, cw), BF16), jax.ShapeDtypeStruct((2, seq, cw), BF16)),
        scratch_shapes=[pltpu.VMEM((seq, width), F32), pltpu.VMEM((seq, width), F32)],
        compiler_params=_params(48),
    )(proj, proj, proj, proj, tri, o, tot, d_mix_attn)


def _out_ln(mix_conv, mix_attn, x, target, gain, bias, w_out, tm):
    seq, d_model = x.shape
    cw = mix_conv.shape[1]
    inv_d = 1.0 / d_model

    def body(mc_ref, ma_ref, x_ref, t_ref, gain_ref, bias_ref, w_ref, dr_ref, dmc_ref, dma_ref, gwo_ref, small_ref):
        @pl.when(pl.program_id(0) == 0)
        def _():
            gwo_ref[...] = jnp.zeros_like(gwo_ref)
            small_ref[...] = jnp.zeros_like(small_ref)

        mix = jnp.concatenate([mc_ref[...], ma_ref[...]], axis=1)
        w = w_ref[...]
        r = ALPHA * x_ref[...] + _dot(mix, w)
        mu = jnp.sum(r, axis=1, keepdims=True) * inv_d
        xc = r - mu
        var = jnp.sum(xc * xc, axis=1, keepdims=True) * inv_d
        rstd = lax.rsqrt(var + LN_EPS)
        xhat = xc * rstd
        gain_v = gain_ref[...]
        err = xhat * gain_v + bias_ref[...] - t_ref[...]
        row_loss = jnp.sum(err * err, axis=1, keepdims=True)
        loss = (0.5 * inv_d) * jnp.sum(row_loss, axis=0, keepdims=True)
        dy = err * inv_d
        small_ref[ROW_GAIN:ROW_GAIN + 1, :] += jnp.sum(dy * xhat, axis=0, keepdims=True)
        small_ref[ROW_BIAS:ROW_BIAS + 1, :] += jnp.sum(dy, axis=0, keepdims=True)
        small_ref[ROW_LOSS:ROW_LOSS + 1, :] += jnp.broadcast_to(loss, (1, d_model))
        dxhat = dy * gain_v
        m1 = jnp.sum(dxhat, axis=1, keepdims=True) * inv_d
        m2 = jnp.sum(dxhat * xhat, axis=1, keepdims=True) * inv_d
        dr = rstd * (dxhat - m1 - xhat * m2)
        dr_ref[...] = dr
        drb = dr.astype(BF16)
        dmix = _dot(drb, w, NT)
        dmc_ref[...] = dmix[:, :cw].astype(BF16)
        dma_ref[...] = dmix[:, cw:].astype(BF16)
        gwo_ref[...] += _dot(mix, drb, TN)

    def rows(width):
        return pl.BlockSpec((tm, width), lambda i: (i, 0))

    def whole(shape):
        return pl.BlockSpec(shape, lambda i: (0, 0))

    return pl.pallas_call(
        body, name="out_ln", grid=(seq // tm,),
        in_specs=[rows(cw), rows(cw), rows(d_model), rows(d_model), whole((1, d_model)), whole((1, d_model)),
                  whole((d_model, d_model))],
        out_specs=(rows(d_model), rows(cw), rows(cw), whole((d_model, d_model)), whole((SUBLANES, d_model))),
        out_shape=(jax.ShapeDtypeStruct((seq, d_model), F32), jax.ShapeDtypeStruct((seq, cw), BF16),
                   jax.ShapeDtypeStruct((seq, cw), BF16), jax.ShapeDtypeStruct((d_model, d_model), F32),
                   jax.ShapeDtypeStruct((SUBLANES, d_model), F32)),
        compiler_params=_params(48),
    )(mix_conv, mix_attn, x, target, gain, bias, w_out)


_DP_OF_GROUP = ((0, 0), (0, 1), (0, 2), (0, 3), (1, 0), (2, 0), (2, 1), (1, 1))


def _grad_x(dr, dp_conv, dp_qz, dp_kv, win_all, tm):
    seq, d_model = dr.shape
    nch, _, cw = win_all.shape

    def body(dr_ref, dc_ref, dqz_ref, dkv_ref, w_ref, o_ref):
        parts = (dc_ref, dqz_ref, dkv_ref)
        acc = ALPHA * dr_ref[...]
        for j in range(nch):
            arr, idx = _DP_OF_GROUP[j]
            acc = acc + _dot(parts[arr][idx], w_ref[j], NT)
        o_ref[...] = acc

    def part(n):
        return pl.BlockSpec((n, tm, cw), lambda i: (0, i, 0))

    return pl.pallas_call(
        body, name="grad_x", grid=(seq // tm,),
        in_specs=[pl.BlockSpec((tm, d_model), lambda i: (i, 0)), part(4), part(2), part(2),
                  pl.BlockSpec((nch, d_model, cw), lambda i: (0, 0, 0))],
        out_specs=pl.BlockSpec((tm, d_model), lambda i: (i, 0)),
        out_shape=jax.ShapeDtypeStruct((seq, d_model), F32),
        compiler_params=_params(48),
    )(dr, dp_conv, dp_qz, dp_kv, win_all)


def _grad_w_reduce(xt, dp_parts, gwo, small, w_in, m_in, v_in, w_out, m_out, v_out, row_chunk):
    n_tiles, d_model, tk = xt.shape
    cw = w_in.shape[1]
    rows_out = w_out.shape[0]

    def body(xt_hbm, dpa, dpb, dpc, gwo_ref, small_ref, w_in_ref, m_in_ref, v_in_ref, w_out_ref, m_out_ref, v_out_ref,
             g_in_o, d_in_o, nm_in_o, nv_in_o, g_out_o, d_out_o, nm_out_o, nv_out_o, small_o,
             xt_v, dp_buf, acc, got_in, send_in, recv_in, own_out, got_out, send_out, recv_out, small_all,
             xt_sem, dp_sems, loc_sems, d2d_send, d2d_recv, ici_send, ici_recv, sm_send, sm_recv):
        x, y, c = _mesh_pos()
        me = 4 * x + 2 * y + c
        sibling = (x, y, 1 - c)
        chips = [(1 - x, 1 - y), (1 - x, y), (x, 1 - y)]
        owners = [(*chip, cc) for chip in chips for cc in (1 - c, c)] + [sibling, (x, y, c)]
        group_of = [4 * o[0] + 2 * o[1] + o[2] for o in owners]
        dp_parts_ = (dpa, dpb, dpc)
        dp_groups = [dp_parts_[arr].at[idx] for arr, idx in _DP_OF_GROUP]

        xt_copy = pltpu.make_async_copy(xt_hbm, xt_v, xt_sem)
        xt_copy.start()

        def dp_start(step, tile, buf):
            for k in range(N_DEV):
                @pl.when(group_of[step] == k)
                def _(k=k):
                    pltpu.make_async_copy(dp_groups[k].at[pl.ds(tile * tk, tk), :], dp_buf.at[buf],
                                          dp_sems.at[buf]).start()

        def dp_wait(buf):
            pltpu.make_async_copy(dp_groups[0].at[pl.ds(0, tk), :], dp_buf.at[buf], dp_sems.at[buf]).wait()

        dp_start(0, 0, 0)

        small_all[me] = small_ref[...]
        for d in range(N_DEV):
            @pl.when(d != me)
            def _(d=d):
                pltpu.make_async_remote_copy(
                    src_ref=small_ref, dst_ref=small_all.at[me], send_sem=sm_send.at[d], recv_sem=sm_recv.at[me],
                    device_id=(d // 4, (d // 2) % 2, d % 2), device_id_type=MESH).start()

        def block_out(k):
            return gwo_ref.at[pl.ds(k * rows_out, rows_out), :]

        for k in range(N_DEV):
            s = k // 2

            @pl.when(k % 2 != c)
            def _(k=k, s=s):
                pltpu.make_async_remote_copy(
                    src_ref=block_out(k), dst_ref=got_out.at[s], send_sem=d2d_send.at[1, s],
                    recv_sem=d2d_recv.at[1, s], device_id=sibling, device_id_type=MESH).start()

            @pl.when(k % 2 == c)
            def _(k=k, s=s):
                pltpu.make_async_copy(block_out(k), own_out.at[s], loc_sems.at[s]).start()

        def ici_copy(a, j):
            send, recv = ((send_in, recv_in), (send_out, recv_out))[a]
            return pltpu.make_async_remote_copy(
                src_ref=send.at[j], dst_ref=recv.at[j], send_sem=ici_send.at[a, j], recv_sem=ici_recv.at[a, j],
                device_id=(*chips[j], c), device_id_type=MESH)

        def by_rows(n_rows, fn):
            step = min(row_chunk, n_rows)

            def rows_body(r, _):
                fn(pl.ds(pl.multiple_of(r * step, step), step))
                return 0

            lax.fori_loop(0, n_rows // step, rows_body, 0)

        for s in range(4):
            pltpu.make_async_copy(own_out.at[s], own_out.at[s], loc_sems.at[s]).wait()
            pltpu.make_async_remote_copy(
                src_ref=got_out.at[s], dst_ref=got_out.at[s], send_sem=d2d_send.at[1, s], recv_sem=d2d_recv.at[1, s],
                device_id=sibling, device_id_type=MESH).wait()
        for j, chip in enumerate(chips):
            s = 2 * chip[0] + chip[1]

            def to_send(rows, s=s, j=j):
                send_out[j, rows, :] = (own_out[s, rows, :] + got_out[s, rows, :]).astype(BF16)

            by_rows(rows_out, to_send)
            ici_copy(1, j).start()

        xt_copy.wait()

        def d2d_copy(slot, pair):
            return pltpu.make_async_remote_copy(
                src_ref=acc.at[slot], dst_ref=got_in.at[pair], send_sem=d2d_send.at[0, pair],
                recv_sem=d2d_recv.at[0, pair], device_id=sibling, device_id_type=MESH)

        for step in range(N_DEV):
            slot, pair = step % 2, step // 2
            if step % 2 == 0 and step >= 2:
                d2d_copy(slot, pair - 1).wait_send()

            def tile_body(tile, _, step=step, slot=slot):
                buf = (step * n_tiles + tile) % 2
                dp_wait(buf)

                @pl.when(tile + 1 < n_tiles)
                def _():
                    dp_start(step, tile + 1, 1 - buf)

                if step + 1 < N_DEV:
                    @pl.when(tile + 1 == n_tiles)
                    def _():
                        dp_start(step + 1, 0, 1 - buf)

                part = _dot(xt_v[tile], dp_buf[buf])

                @pl.when(tile == 0)
                def _():
                    acc[slot] = part

                @pl.when(tile > 0)
                def _():
                    acc[slot] += part

                return 0

            lax.fori_loop(0, n_tiles, tile_body, 0)

            if step % 2 == 0:
                d2d_copy(slot, pair).start()
            else:
                d2d_copy(slot, pair).wait_recv()
                if step < N_DEV - 1:
                    def to_send(rows, slot=slot, pair=pair):
                        send_in[pair, rows, :] = (acc[slot, rows, :] + got_in[pair, rows, :]).astype(BF16)

                    by_rows(d_model, to_send)
                    ici_copy(0, pair).start()

        for d in range(N_DEV):
            @pl.when(d != me)
            def _(d=d):
                pltpu.make_async_remote_copy(
                    src_ref=small_ref, dst_ref=small_all.at[d], send_sem=sm_send.at[d], recv_sem=sm_recv.at[d],
                    device_id=(d // 4, (d // 2) % 2, d % 2), device_id_type=MESH).wait()
        total = small_all[0]
        for d in range(1, N_DEV):
            total = total + small_all[d]
        small_o[...] = total

        mine = 2 * x + y
        last = (N_DEV - 1) % 2

        def finish(a, n_rows, chip_sum, outs):
            recv = (recv_in, recv_out)[a]
            for j in range(3):
                ici_copy(a, j).wait()
            g_o, d_o, nm_o, nv_o, w_r, m_r, v_r = outs

            def update(rows):
                g = chip_sum(rows)
                for j in range(3):
                    g = g + recv[j, rows, :].astype(F32)
                delta, nm, nv = _adamw(w_r[rows, :], g, m_r[rows, :], v_r[rows, :])
                g_o[rows, :] = g
                d_o[rows, :] = delta
                nm_o[rows, :] = nm
                nv_o[rows, :] = nv

            by_rows(n_rows, update)

        finish(1, rows_out, lambda rows: own_out[mine, rows, :] + got_out[mine, rows, :],
               (g_out_o, d_out_o, nm_out_o, nv_out_o, w_out_ref, m_out_ref, v_out_ref))
        finish(0, d_model, lambda rows: acc[last, rows, :] + got_in[N_DEV // 2 - 1, rows, :],
               (g_in_o, d_in_o, nm_in_o, nv_in_o, w_in_ref, m_in_ref, v_in_ref))
        d2d_copy(0, N_DEV // 2 - 1).wait_send()

    vmem = pl.BlockSpec(memory_space=pltpu.VMEM)
    hbm = pl.BlockSpec(memory_space=pl.ANY)
    shard_in = jax.ShapeDtypeStruct((d_model, cw), F32)
    shard_out = jax.ShapeDtypeStruct((rows_out, d_model), F32)
    return pl.pallas_call(
        body, name="grad_w_reduce",
        in_specs=[hbm] * 5 + [vmem] * 7,
        out_specs=(vmem,) * 9,
        out_shape=(shard_in,) * 4 + (shard_out,) * 4 + (jax.ShapeDtypeStruct(small.shape, F32),),
        scratch_shapes=[
            pltpu.VMEM((n_tiles, d_model, tk), BF16), pltpu.VMEM((2, tk, cw), BF16), pltpu.VMEM((2, d_model, cw), F32),
            pltpu.VMEM((4, d_model, cw), F32), pltpu.VMEM((3, d_model, cw), BF16), pltpu.VMEM((3, d_model, cw), BF16),
            pltpu.VMEM((4, rows_out, d_model), F32), pltpu.VMEM((4, rows_out, d_model), F32),
            pltpu.VMEM((3, rows_out, d_model), BF16), pltpu.VMEM((3, rows_out, d_model), BF16),
            pltpu.VMEM((N_DEV,) + small.shape, F32),
            pltpu.SemaphoreType.DMA, pltpu.SemaphoreType.DMA((2,)), pltpu.SemaphoreType.DMA((4,)),
            pltpu.SemaphoreType.DMA((2, 4)), pltpu.SemaphoreType.DMA((2, 4)),
            pltpu.SemaphoreType.DMA((2, 3)), pltpu.SemaphoreType.DMA((2, 3)),
            pltpu.SemaphoreType.DMA((N_DEV,)), pltpu.SemaphoreType.DMA((N_DEV,)),
        ],
        compiler_params=_params(56),
    )(xt, *dp_parts, gwo, small, w_in, m_in, v_in, w_out, m_out, v_out)


def _small_update(grads, weights, ms, vs):
    n = len(grads)

    def body(*refs):
        g_refs, w_refs, m_refs, v_refs = (refs[i * n:(i + 1) * n] for i in range(4))
        outs = refs[4 * n:]
        for i in range(n):
            delta, nm, nv = _adamw(w_refs[i][...], g_refs[i][...], m_refs[i][...], v_refs[i][...])
            outs[3 * i][...] = delta
            outs[3 * i + 1][...] = nm
            outs[3 * i + 2][...] = nv

    vmem = pl.BlockSpec(memory_space=pltpu.VMEM)
    out_shape = []
    for w in weights:
        out_shape += [jax.ShapeDtypeStruct(w.shape, F32)] * 3
    return pl.pallas_call(
        body, name="small_update", in_specs=[vmem] * (4 * n), out_specs=(vmem,) * (3 * n), out_shape=tuple(out_shape),
    )(*grads, *weights, *ms, *vs)


def _tile_sizes(seq):
    return dict(tm=min(512, seq), t_ln=min(256, seq), t_attn=min(128, seq), rc=min(256, seq), pairs=4)


def kernel(x, w_in, conv_w, w_out, ln_gain, ln_bias, loss_target, m_w_in, m_conv_w, m_w_out, m_ln_gain, m_ln_bias,
           v_w_in, v_conv_w, v_w_out, v_ln_gain, v_ln_bias):
    assert x.shape[0] == 1 and w_in.shape[0] == 1, "one sequence per device, depth 1"
    _, seq, d_model = x.shape
    cw = w_in.shape[2]
    conv_k, conv_cols = conv_w.shape[1], conv_w.shape[2]
    rows_out = w_out.shape[1]
    assert cw == d_model // 2 and cw % PAIR == 0 and conv_cols * N_DEV == cw and rows_out * N_DEV == d_model
    ts = _tile_sizes(seq)

    x2 = x.reshape(seq, d_model)
    target = loss_target.reshape(seq, d_model)
    me = 4 * lax.axis_index("x") + 2 * lax.axis_index("y") + lax.axis_index("c")

    conv_pad = jnp.pad(conv_w[0], ((0, SUBLANES - conv_k), (0, PAIR - conv_cols)))
    proj, xt, win_all, wout_all, conv_all = _gather_proj(x2, w_in[0], w_out[0], conv_pad, ts["tm"])
    w_out_full = wout_all.reshape(d_model, d_model)
    conv_full = conv_all[:, :conv_k, :conv_cols].transpose(1, 0, 2).reshape(conv_k, cw)
    conv_full = jnp.pad(conv_full, ((0, SUBLANES - conv_k), (0, 0)))

    mix_conv = _conv_fwd(proj, conv_full, ts["rc"])
    pairs = min(ts["pairs"], cw // PAIR)
    tri = _triangles(ts["t_attn"])
    o, mix_attn, tot = _attn_fwd(proj, tri, ts["t_attn"], pairs)
    dr, d_mix_conv, d_mix_attn, gwo, small = _out_ln(mix_conv, mix_attn, x2, target, ln_gain, ln_bias, w_out_full,
                                                     ts["t_ln"])
    dp_conv, d_taps = _conv_bwd(proj, conv_full, d_mix_conv, ts["rc"])
    dp_qz, dp_kv = _attn_bwd(proj, tri, o, tot, d_mix_attn, ts["t_attn"], pairs)
    grad_x = _grad_x(dr, dp_conv, dp_qz, dp_kv, win_all, ts["tm"])
    small = small.at[ROW_CONV:ROW_CONV + conv_k, :cw].set(d_taps[:conv_k])
    (g_in, d_in, nm_in, nv_in, g_out, d_out, nm_out, nv_out, small_sum) = _grad_w_reduce(
        xt, (dp_conv, dp_qz, dp_kv), gwo, small, w_in[0], m_w_in[0], v_w_in[0], w_out[0], m_w_out[0], v_w_out[0], 128)

    loss = small_sum[ROW_LOSS, 0]
    g_gain = small_sum[ROW_GAIN:ROW_GAIN + 1]
    g_bias = small_sum[ROW_BIAS:ROW_BIAS + 1]
    g_conv = lax.dynamic_slice(small_sum, (ROW_CONV, me * conv_cols), (conv_k, conv_cols))
    upd = _small_update((g_conv, g_gain, g_bias), (conv_w[0], ln_gain, ln_bias),
                        (m_conv_w[0], m_ln_gain, m_ln_bias), (v_conv_w[0], v_ln_gain, v_ln_bias))
    d_conv, nm_conv, nv_conv, d_gain, nm_gain, nv_gain, d_bias, nm_bias, nv_bias = upd

    lead = lambda a: a[None]
    return (loss, grad_x.reshape(1, seq, d_model), lead(g_in), lead(g_conv), lead(g_out), g_gain, g_bias,
            lead(d_in), lead(d_conv), lead(d_out), d_gain, d_bias,
            lead(nm_in), lead(nm_conv), lead(nm_out), nm_gain, nm_bias,
            lead(nv_in), lead(nv_conv), lead(nv_out), nv_gain, nv_bias)
```

```python
import functools

import jax
import jax.numpy as jnp
from jax import lax
from jax.experimental import pallas as pl
from jax.experimental.pallas import tpu as pltpu

F32 = jnp.float32
BF16 = jnp.bfloat16
MESH = pl.DeviceIdType.MESH

N_DEV = 8
HEAD_DIM = 64
PAIR = 128
SUBLANES = 8
LN_EPS = 1e-5
ALPHA = 2.0 ** 0.25
ADAM_LR, ADAM_B1, ADAM_B2, ADAM_EPS, ADAM_WD, ADAM_STEP = 0.001, 0.9, 0.999, 1e-08, 0.01, 10

ROW_GAIN, ROW_BIAS, ROW_CONV, ROW_LOSS = 0, 1, 2, 5

NT = (((1,), (1,)), ((), ()))
TN = (((0,), (0,)), ((), ()))


V7X_VMEM_BYTES = 64 * 1024 * 1024


def _params(vmem_mib):
    assert vmem_mib * 1024 * 1024 < V7X_VMEM_BYTES
    return pltpu.CompilerParams(vmem_limit_bytes=vmem_mib * 1024 * 1024)


def _dot(a, b, dims=None):
    if dims is None:
        return jnp.dot(a, b, preferred_element_type=F32)
    return lax.dot_general(a, b, dims, preferred_element_type=F32)


def _sigmoid(z):
    return 1.0 / (1.0 + jnp.exp(-z))


def _mesh_pos():
    return lax.axis_index("x"), lax.axis_index("y"), lax.axis_index("c")


def _adamw(w, g, m, v):
    nm = ADAM_B1 * m + (1.0 - ADAM_B1) * g
    nv = ADAM_B2 * v + (1.0 - ADAM_B2) * (g * g)
    m_hat = nm * (1.0 / (1.0 - ADAM_B1 ** ADAM_STEP))
    v_hat = nv * (1.0 / (1.0 - ADAM_B2 ** ADAM_STEP))
    delta = -ADAM_LR * (m_hat / (jnp.sqrt(v_hat) + ADAM_EPS) + ADAM_WD * w)
    return delta, nm, nv


def _gather_proj(x, w_in_s, w_out_s, conv_s, tm):
    seq, d_model = x.shape
    cw = w_in_s.shape[1]
    rows_out = w_out_s.shape[0]
    n_tiles = seq // tm
    n_arr = 3

    def body(x_hbm, win_ref, wout_ref, conv_ref, proj_hbm, xt_hbm, win_all, wout_all, conv_all,
             xb, x_stage, o_stage, xt_stage, x_sems, o_sems, xt_sems, send_sems, recv_sems):
        x, y, c = _mesh_pos()
        me = (x, y, c)
        sibling = (x, y, 1 - c)
        chips = [(1 - x, y), (x, 1 - y), (1 - x, 1 - y)]
        bufs = (win_all, wout_all, conv_all)

        def slot(pos):
            return 4 * pos[0] + 2 * pos[1] + pos[2]

        def x_copy(tile, buf):
            return pltpu.make_async_copy(x_hbm.at[pl.ds(tile * tm, tm), :], x_stage.at[buf], x_sems.at[buf])

        x_copy(0, 0).start()
        win_all[slot(me)] = win_ref[...].astype(BF16)
        wout_all[slot(me)] = wout_ref[...].astype(BF16)
        conv_all[slot(me)] = conv_ref[...]

        def copy(a, k, block, to):
            ref = bufs[a].at[slot(block)]
            return pltpu.make_async_remote_copy(
                src_ref=ref, dst_ref=ref, send_sem=send_sems.at[a, k], recv_sem=recv_sems.at[a, k],
                device_id=to, device_id_type=MESH)

        first, passed = [], []
        for a in range(n_arr):
            first.append(copy(a, 0, me, sibling))
            first += [copy(a, 1 + j, me, (*chip, c)) for j, chip in enumerate(chips)]
        for cp in first:
            cp.start()

        def o_copy(group, tile, buf):
            return pltpu.make_async_copy(o_stage.at[buf], proj_hbm.at[group, pl.ds(tile * tm, tm), :], o_sems.at[buf])

        def xt_copy(tile, buf):
            return pltpu.make_async_copy(xt_stage.at[buf], xt_hbm.at[tile], xt_sems.at[buf])

        def project(order, group, first_pass):
            def tile_body(tile, _):
                if first_pass:
                    buf = tile % 2
                    x_copy(tile, buf).wait()

                    @pl.when(tile + 1 < n_tiles)
                    def _():
                        x_copy(tile + 1, 1 - buf).start()

                    xv = x_stage[buf]
                    xb[tile] = xv.astype(BF16)

                    @pl.when(tile >= 2)
                    def _():
                        xt_copy(tile - 2, buf).wait()

                    xt_stage[buf] = xv.T.astype(BF16)
                    xt_copy(tile, buf).start()
                count = order * n_tiles + tile
                obuf = count % 2

                @pl.when(count >= 2)
                def _():
                    o_copy(group, tile, obuf).wait()

                o_stage[obuf] = _dot(xb[tile], win_all[group]).astype(BF16)
                o_copy(group, tile, obuf).start()
                return 0

            lax.fori_loop(0, n_tiles, tile_body, 0)

        def pass_on(j):
            for a in range(n_arr):
                copy(a, 1 + j, (*chips[j], c), me).wait_recv()
                cp = copy(a, 4 + j, (*chips[j], c), sibling)
                cp.start()
                passed.append(cp)

        def from_sibling(k):
            for a in range(n_arr):
                copy(a, k, sibling, me).wait_recv()

        project(0, slot(me), True)
        from_sibling(0)
        project(1, slot(sibling), False)
        pass_on(0)
        project(2, slot((*chips[0], c)), False)
        pass_on(1)
        project(3, slot((*chips[1], c)), False)
        from_sibling(4)
        project(4, slot((*chips[0], 1 - c)), False)
        from_sibling(5)
        project(5, slot((*chips[1], 1 - c)), False)
        pass_on(2)
        project(6, slot((*chips[2], c)), False)
        from_sibling(6)
        project(7, slot((*chips[2], 1 - c)), False)

        for buf in range(2):
            o_copy(0, 0, buf).wait()
        for buf in range(min(2, n_tiles)):
            xt_copy(0, buf).wait()
        for cp in first + passed:
            cp.wait_send()

    vmem = pl.BlockSpec(memory_space=pltpu.VMEM)
    hbm = pl.BlockSpec(memory_space=pl.ANY)
    return pl.pallas_call(
        body, name="gather_proj",
        out_shape=(jax.ShapeDtypeStruct((N_DEV, seq, cw), BF16),
                   jax.ShapeDtypeStruct((n_tiles, d_model, tm), BF16),
                   jax.ShapeDtypeStruct((N_DEV, d_model, cw), BF16),
                   jax.ShapeDtypeStruct((N_DEV, rows_out, d_model), BF16),
                   jax.ShapeDtypeStruct((N_DEV,) + conv_s.shape, F32)),
        in_specs=[hbm, vmem, vmem, vmem], out_specs=(hbm, hbm, vmem, vmem, vmem),
        scratch_shapes=[
            pltpu.VMEM((n_tiles, tm, d_model), BF16), pltpu.VMEM((2, tm, d_model), F32),
            pltpu.VMEM((2, tm, cw), BF16), pltpu.VMEM((2, d_model, tm), BF16),
            pltpu.SemaphoreType.DMA((2,)), pltpu.SemaphoreType.DMA((2,)), pltpu.SemaphoreType.DMA((2,)),
            pltpu.SemaphoreType.DMA((n_arr, 7)), pltpu.SemaphoreType.DMA((n_arr, 7))],
        compiler_params=_params(48),
    )(x, w_in_s, w_out_s, conv_s)


def _conv_taps(ext, w_ref, rc):
    u0 = ext[SUBLANES:SUBLANES + rc]
    u1 = pltpu.roll(ext, 1, 0)[SUBLANES:SUBLANES + rc]
    u2 = pltpu.roll(ext, 2, 0)[SUBLANES:SUBLANES + rc]
    return w_ref[2:3, :] * u0 + w_ref[1:2, :] * u1 + w_ref[0:1, :] * u2, u0, u1, u2


def _conv_fwd(proj, conv_full, rc):
    _, seq, cw = proj.shape

    def body(b_ref, c_ref, h_ref, z_ref, w_ref, o_ref, u_scr):
        u_scr[0:SUBLANES, :] = jnp.zeros((SUBLANES, PAIR), F32)

        def fill(r, _):
            base = pl.multiple_of(r * rc, rc)
            rows = pl.ds(base, rc)
            u_scr[pl.ds(base + SUBLANES, rc), :] = c_ref[rows, :].astype(F32) * h_ref[rows, :].astype(F32)
            return 0

        lax.fori_loop(0, seq // rc, fill, 0)

        def out(r, _):
            base = pl.multiple_of(r * rc, rc)
            rows = pl.ds(base, rc)
            ext = u_scr[pl.ds(base, rc + SUBLANES), :]
            y, _, _, _ = _conv_taps(ext, w_ref, rc)
            z = z_ref[rows, :].astype(F32)
            o_ref[rows, :] = (z * _sigmoid(z) * b_ref[rows, :].astype(F32) * y).astype(BF16)
            return 0

        lax.fori_loop(0, seq // rc, out, 0)

    def chunk(j):
        return pl.BlockSpec((None, seq, PAIR), lambda cb, j=j: (j, 0, cb))

    return pl.pallas_call(
        body, name="conv_fwd", grid=(cw // PAIR,),
        in_specs=[chunk(0), chunk(1), chunk(2), chunk(3), pl.BlockSpec((SUBLANES, PAIR), lambda cb: (0, cb))],
        out_specs=pl.BlockSpec((seq, PAIR), lambda cb: (0, cb)),
        out_shape=jax.ShapeDtypeStruct((seq, cw), BF16),
        scratch_shapes=[pltpu.VMEM((seq + SUBLANES, PAIR), F32)],
    )(proj, proj, proj, proj, conv_full)


def _conv_bwd(proj, conv_full, d_mix_conv, rc):
    _, seq, cw = proj.shape

    def body(b_ref, c_ref, h_ref, z_ref, w_ref, g_ref, dp_ref, dw_ref, u_scr, dy_scr):
        u_scr[0:SUBLANES, :] = jnp.zeros((SUBLANES, PAIR), F32)
        dy_scr[seq:seq + SUBLANES, :] = jnp.zeros((SUBLANES, PAIR), F32)

        def fill(r, _):
            base = pl.multiple_of(r * rc, rc)
            rows = pl.ds(base, rc)
            u_scr[pl.ds(base + SUBLANES, rc), :] = c_ref[rows, :].astype(F32) * h_ref[rows, :].astype(F32)
            return 0

        lax.fori_loop(0, seq // rc, fill, 0)

        def gate(r, acc):
            base = pl.multiple_of(r * rc, rc)
            rows = pl.ds(base, rc)
            ext = u_scr[pl.ds(base, rc + SUBLANES), :]
            y, u0, u1, u2 = _conv_taps(ext, w_ref, rc)
            z = z_ref[rows, :].astype(F32)
            b = b_ref[rows, :].astype(F32)
            g = g_ref[rows, :].astype(F32)
            sig = _sigmoid(z)
            dp_ref[3, rows, :] = (g * b * y * (sig * (1.0 + z * (1.0 - sig)))).astype(BF16)
            gs = g * (z * sig)
            dp_ref[0, rows, :] = (gs * y).astype(BF16)
            dy = gs * b
            dy_scr[rows, :] = dy
            a0, a1, a2 = acc
            return (a0 + jnp.sum(dy * u2, axis=0, keepdims=True),
                    a1 + jnp.sum(dy * u1, axis=0, keepdims=True),
                    a2 + jnp.sum(dy * u0, axis=0, keepdims=True))

        zero = jnp.zeros((1, PAIR), F32)
        a0, a1, a2 = lax.fori_loop(0, seq // rc, gate, (zero, zero, zero))
        dw_ref[...] = jnp.zeros((SUBLANES, PAIR), F32)
        dw_ref[0:1, :] = a0
        dw_ref[1:2, :] = a1
        dw_ref[2:3, :] = a2

        def back(r, _):
            base = pl.multiple_of(r * rc, rc)
            rows = pl.ds(base, rc)
            ext = dy_scr[pl.ds(base, rc + SUBLANES), :]
            n = rc + SUBLANES
            d0 = ext[0:rc]
            d1 = pltpu.roll(ext, n - 1, 0)[0:rc]
            d2 = pltpu.roll(ext, n - 2, 0)[0:rc]
            du = w_ref[2:3, :] * d0 + w_ref[1:2, :] * d1 + w_ref[0:1, :] * d2
            dp_ref[1, rows, :] = (du * h_ref[rows, :].astype(F32)).astype(BF16)
            dp_ref[2, rows, :] = (du * c_ref[rows, :].astype(F32)).astype(BF16)
            return 0

        lax.fori_loop(0, seq // rc, back, 0)

    def chunk(j):
        return pl.BlockSpec((None, seq, PAIR), lambda cb, j=j: (j, 0, cb))

    return pl.pallas_call(
        body, name="conv_bwd", grid=(cw // PAIR,),
        in_specs=[chunk(0), chunk(1), chunk(2), chunk(3), pl.BlockSpec((SUBLANES, PAIR), lambda cb: (0, cb)),
                  pl.BlockSpec((seq, PAIR), lambda cb: (0, cb))],
        out_specs=(pl.BlockSpec((4, seq, PAIR), lambda cb: (0, 0, cb)),
                   pl.BlockSpec((SUBLANES, PAIR), lambda cb: (0, cb))),
        out_shape=(jax.ShapeDtypeStruct((4, seq, cw), BF16), jax.ShapeDtypeStruct((SUBLANES, cw), F32)),
        scratch_shapes=[pltpu.VMEM((seq + SUBLANES, PAIR), F32), pltpu.VMEM((seq + SUBLANES, PAIR), F32)],
    )(proj, proj, proj, proj, conv_full, d_mix_conv)


SKIP_CARRY = 104.0
LANE_TOT0, LANE_TOT1, LANE_FIRST = 0, 1, 2
FAST_BLOCKS = 3


def _triangles(t):
    row = lax.broadcasted_iota(jnp.int32, (2 * t, 2 * t), 0)
    col = lax.broadcasted_iota(jnp.int32, (2 * t, 2 * t), 1)
    same = (row < t) == (col < t)
    upper = jnp.logical_and(same, row > col).astype(BF16)
    lower = jnp.logical_and(same, row < col).astype(BF16)
    return jnp.stack([jnp.concatenate([upper, upper], axis=0), jnp.concatenate([lower, lower], axis=0)])


def _pair_masks(t):
    lane = lax.broadcasted_iota(jnp.int32, (t, PAIR), 1)
    qrow = lax.broadcasted_iota(jnp.int32, (t, 2 * t), 0)
    kcol = lax.broadcasted_iota(jnp.int32, (t, 2 * t), 1)
    strict = jnp.where(kcol < t, kcol, kcol - t) < qrow
    return lane, lane < HEAD_DIM, strict


def _by_head(x, head0):
    zero = jnp.zeros_like(x)
    return jnp.concatenate([jnp.where(head0, x, zero), jnp.where(head0, zero, x)], axis=0)


def _hi_lo(a):
    hi = a.astype(BF16)
    lo = (a - hi.astype(F32)).astype(BF16)
    return jnp.concatenate([hi, lo], axis=1)


def _softplus_parts(z, strict, masked):
    spu = jnp.maximum(z, 0.0) + jnp.log(1.0 + jnp.exp(-jnp.abs(z)))
    return z - spu, (jnp.where(strict, spu, 0.0) if masked else spu)


def _stacked_dot(parts, rhs):
    rows = parts[0].shape[0]
    out = _dot(jnp.concatenate(parts, axis=0), rhs)
    return [out[n * rows:(n + 1) * rows] for n in range(len(parts))]


def _attn_fwd(proj, tri, t, pp):
    _, seq, cw = proj.shape
    scale = HEAD_DIM ** -0.5
    width = pp * PAIR

    def body(q_ref, k_ref, v_ref, za_ref, tri_ref, o_ref, mix_ref, tot_ref):
        i = pl.program_id(1)
        lane, head0, strict = _pair_masks(t)
        upper = tri_ref[0]
        q = q_ref[...] * scale

        def sweep(blocks, state):
            staged = []
            for j, masked in blocks:
                start = pl.multiple_of(j * t, t)
                kb = k_ref[pl.ds(start, t), :]
                vb = v_ref[pl.ds(start, t), :]
                for p in range(pp):
                    cols = slice(p * PAIR, (p + 1) * PAIR)
                    z = _dot(q[:, cols], _by_head(kb[:, cols], head0), NT)
                    ls, sp = _softplus_parts(z, strict, masked)
                    staged.append((p, masked, ls, sp, _by_head(vb[:, cols], head0)))
            afters = _stacked_dot([_hi_lo(sp) for _, _, _, sp, _ in staged], upper)
            state = list(state)
            for (p, masked, ls, sp, v2), after in zip(staged, afters):
                (c0, c1), acc = state[p]
                x = ls - after
                w = jnp.exp(jnp.concatenate([x[:, :t] - c0, x[:, t:] - c1], axis=1))
                if masked:
                    w = jnp.where(strict, w, 0.0)
                c0 = c0 + (after[:, 0:1] + sp[:, 0:1])
                c1 = c1 + (after[:, t:t + 1] + sp[:, t:t + 1])
                state[p] = ((c0, c1), acc + _dot(w.astype(BF16), v2))
            return tuple(state)

        def unfinished(state):
            m = state[0][0][0]
            for p in range(pp):
                m = jnp.minimum(m, jnp.minimum(state[p][0][0], state[p][0][1]))
            return jnp.min(m) < SKIP_CARRY

        def step(js):
            state = sweep(((js[0], False),), js[1])
            return js[0] - 1, state, unfinished(state)

        zcol = jnp.zeros((t, 1), F32)
        init = tuple(((zcol, zcol), jnp.zeros((t, PAIR), F32)) for _ in range(pp))
        many = i >= FAST_BLOCKS - 1
        state = lax.cond(
            many,
            lambda: sweep(((i, True),) + tuple((i - b, False) for b in range(1, FAST_BLOCKS)), init),
            lambda: sweep(((i, True),), init))
        j_end, state, _ = lax.while_loop(
            lambda js: jnp.logical_and(js[0] >= 0, js[2]), step,
            (jnp.where(many, i - FAST_BLOCKS, i - 1), state, unfinished(state)))
        first = (j_end + 1).astype(F32)
        za = za_ref[...].astype(F32)
        for p in range(pp):
            (c0, c1), acc = state[p]
            cols = slice(p * PAIR, (p + 1) * PAIR)
            zp = za[:, cols]
            o_ref[:, cols] = acc.astype(BF16)
            mix_ref[:, cols] = (zp * _sigmoid(zp) * acc).astype(BF16)
            tot_ref[:, cols] = jnp.where(lane == LANE_TOT0, c0, jnp.where(lane == LANE_TOT1, c1, first))

    def tile(j):
        return pl.BlockSpec((None, t, width), lambda g, i, j=j: (j, i, g))

    def full(j):
        return pl.BlockSpec((None, seq, width), lambda g, i, j=j: (j, 0, g))

    out_tile = pl.BlockSpec((t, width), lambda g, i: (i, g))
    return pl.pallas_call(
        body, name="attn_fwd", grid=(cw // width, seq // t),
        in_specs=[tile(4), full(5), full(6), tile(7), pl.BlockSpec(tri.shape, lambda g, i: (0, 0, 0))],
        out_specs=(out_tile, out_tile, out_tile),
        out_shape=(jax.ShapeDtypeStruct((seq, cw), BF16), jax.ShapeDtypeStruct((seq, cw), BF16),
                   jax.ShapeDtypeStruct((seq, cw), F32)),
    )(proj, proj, proj, proj, tri)


def _attn_bwd(proj, tri, o, tot, d_mix_attn, t, pp):
    _, seq, cw = proj.shape
    nb = seq // t
    scale = HEAD_DIM ** -0.5
    width = pp * PAIR

    def body(q_ref, k_ref, v_ref, za_ref, tri_ref, o_ref, tot_ref, g_ref, dqz_ref, dkv_ref, dk_acc, dv_acc):
        i = pl.program_id(1)

        @pl.when(i == 0)
        def _():
            dk_acc[...] = jnp.zeros_like(dk_acc)
            dv_acc[...] = jnp.zeros_like(dv_acc)

        _, head0, strict = _pair_masks(t)
        upper, lower = tri_ref[0], tri_ref[1, 0:2 * t, :]
        za = za_ref[...].astype(F32)
        g = g_ref[...].astype(F32)
        sig = _sigmoid(za)
        dqz_ref[1] = (g * o_ref[...].astype(F32) * (sig * (1.0 + za * (1.0 - sig)))).astype(BF16)
        do = (g * (za * sig)).astype(BF16)
        q = q_ref[...] * scale
        tot_v = tot_ref[...]
        q2, do2, init = [], [], []
        zcol = jnp.zeros((t, 1), F32)
        for p in range(pp):
            cols = slice(p * PAIR, (p + 1) * PAIR)
            q2.append(_by_head(q[:, cols], head0))
            do2.append(_by_head(do[:, cols], head0))
            tp = tot_v[:, cols]
            init.append(((tp[:, LANE_TOT0:LANE_TOT0 + 1], tp[:, LANE_TOT1:LANE_TOT1 + 1]), (zcol, zcol),
                         jnp.zeros((t, PAIR), F32)))
        first = jnp.clip(tot_v[0:1, LANE_FIRST:LANE_FIRST + 1], 0.0, i.astype(F32)).astype(jnp.int32)[0, 0]

        def sweep(blocks, state):
            staged = []
            for j, masked in blocks:
                start = pl.multiple_of(j * t, t)
                kb = k_ref[pl.ds(start, t), :]
                vb = v_ref[pl.ds(start, t), :]
                for p in range(pp):
                    cols = slice(p * PAIR, (p + 1) * PAIR)
                    k2 = _by_head(kb[:, cols], head0)
                    z = _dot(q[:, cols], k2, NT)
                    ls, sp = _softplus_parts(z, strict, masked)
                    da = _dot(do[:, cols], _by_head(vb[:, cols], head0), NT)
                    staged.append((p, masked, k2, ls, sp, da))
            afters = _stacked_dot([_hi_lo(sp) for _, _, _, _, sp, _ in staged], upper)
            state = list(state)
            weights, ggs = [], []
            for (p, masked, k2, ls, sp, da), after in zip(staged, afters):
                (r0, r1), befores, dq = state[p]
                r0 = r0 - (after[:, 0:1] + sp[:, 0:1])
                r1 = r1 - (after[:, t:t + 1] + sp[:, t:t + 1])
                x = ls - after
                a = jnp.exp(jnp.concatenate([x[:, :t] - r0, x[:, t:] - r1], axis=1))
                if masked:
                    a = jnp.where(strict, a, 0.0)
                state[p] = ((r0, r1), befores, dq)
                weights.append(a.astype(BF16))
                ggs.append(a * da)
            pres = _stacked_dot([gg.astype(BF16) for gg in ggs], lower)
            dzs = []
            for (p, masked, k2, ls, sp, da), gg, pre in zip(staged, ggs, pres):
                rests, (b0, b1), dq = state[p]
                y = gg + pre
                dz = gg - jnp.exp(ls) * jnp.concatenate([y[:, :t] + b0, y[:, t:] + b1], axis=1)
                if masked:
                    dz = jnp.where(strict, dz, 0.0)
                dzb = dz.astype(BF16)
                dzs.append(dzb)
                state[p] = (rests, (b0 + y[:, t - 1:t], b1 + y[:, 2 * t - 1:2 * t]), dq + _dot(dzb, k2))
            first_row = pl.multiple_of(blocks[0][0] * t, t)
            n_rows = len(blocks) * t
            for p in range(pp):
                cols = slice(p * PAIR, (p + 1) * PAIR)

                def by_key(tiles):
                    return jnp.concatenate(
                        [jnp.concatenate([m[:, :t], m[:, t:]], axis=0).T for m in tiles], axis=0)

                mine = [n for n in range(len(staged)) if staged[n][0] == p]
                dk_acc[pl.ds(first_row, n_rows), cols] += _dot(by_key([dzs[n] for n in mine]), q2[p])
                dv_acc[pl.ds(first_row, n_rows), cols] += _dot(by_key([weights[n] for n in mine]), do2[p])
            return tuple(state)

        many = i >= FAST_BLOCKS - 1
        last_single = jnp.where(many, i - (FAST_BLOCKS - 1), i)
        state = lax.fori_loop(first, last_single, lambda j, s: sweep(((j, False),), s), tuple(init))
        state = lax.cond(
            many,
            lambda: sweep(tuple((i - b, False) for b in range(FAST_BLOCKS - 1, 0, -1)) + ((i, True),), state),
            lambda: sweep(((i, True),), state))
        for p in range(pp):
            dqz_ref[0, :, p * PAIR:(p + 1) * PAIR] = (state[p][2] * scale).astype(BF16)

        @pl.when(i == nb - 1)
        def _():
            dkv_ref[0] = dk_acc[...].astype(BF16)
            dkv_ref[1] = dv_acc[...].astype(BF16)

    def tile(j):
        return pl.BlockSpec((None, t, width), lambda g, i, j=j: (j, i, g))

    def full(j):
        return pl.BlockSpec((None, seq, width), lambda g, i, j=j: (j, 0, g))

    flat_tile = pl.BlockSpec((t, width), lambda g, i: (i, g))
    return pl.pallas_call(
        body, name="attn_bwd", grid=(cw // width, nb),
        in_specs=[tile(4), full(5), full(6), tile(7), pl.BlockSpec(tri.shape, lambda g, i: (0, 0, 0)),
                  flat_tile, flat_tile, flat_tile],
        out_specs=(pl.BlockSpec((2, t, width), lambda g, i: (0, i, g)),
                   pl.BlockSpec((2, seq, width), lambda g, i: (0, 0, g))),
        out_shape=(jax.ShapeDtypeStruct((2, seq, cw), BF16), jax.ShapeDtypeStruct((2, seq, cw), BF16)),
        scratch_shapes=[pltpu.VMEM((seq, width), F32), pltpu.VMEM((seq, width), F32)],
        compiler_params=_params(48),
    )(proj, proj, proj, proj, tri, o, tot, d_mix_attn)


def _out_ln(mix_conv, mix_attn, x, target, gain, bias, w_out, tm):
    seq, d_model = x.shape
    cw = mix_conv.shape[1]
    inv_d = 1.0 / d_model

    def body(mc_ref, ma_ref, x_ref, t_ref, gain_ref, bias_ref, w_ref, dr_ref, dmc_ref, dma_ref, gwo_ref, small_ref):
        @pl.when(pl.program_id(0) == 0)
        def _():
            gwo_ref[...] = jnp.zeros_like(gwo_ref)
            small_ref[...] = jnp.zeros_like(small_ref)

        mix = jnp.concatenate([mc_ref[...], ma_ref[...]], axis=1)
        w = w_ref[...]
        r = ALPHA * x_ref[...] + _dot(mix, w)
        mu = jnp.sum(r, axis=1, keepdims=True) * inv_d
        xc = r - mu
        var = jnp.sum(xc * xc, axis=1, keepdims=True) * inv_d
        rstd = lax.rsqrt(var + LN_EPS)
        xhat = xc * rstd
        gain_v = gain_ref[...]
        err = xhat * gain_v + bias_ref[...] - t_ref[...]
        row_loss = jnp.sum(err * err, axis=1, keepdims=True)
        loss = (0.5 * inv_d) * jnp.sum(row_loss, axis=0, keepdims=True)
        dy = err * inv_d
        small_ref[ROW_GAIN:ROW_GAIN + 1, :] += jnp.sum(dy * xhat, axis=0, keepdims=True)
        small_ref[ROW_BIAS:ROW_BIAS + 1, :] += jnp.sum(dy, axis=0, keepdims=True)
        small_ref[ROW_LOSS:ROW_LOSS + 1, :] += jnp.broadcast_to(loss, (1, d_model))
        dxhat = dy * gain_v
        m1 = jnp.sum(dxhat, axis=1, keepdims=True) * inv_d
        m2 = jnp.sum(dxhat * xhat, axis=1, keepdims=True) * inv_d
        dr = rstd * (dxhat - m1 - xhat * m2)
        dr_ref[...] = dr
        drb = dr.astype(BF16)
        dmix = _dot(drb, w, NT)
        dmc_ref[...] = dmix[:, :cw].astype(BF16)
        dma_ref[...] = dmix[:, cw:].astype(BF16)
        gwo_ref[...] += _dot(mix, drb, TN)

    def rows(width):
        return pl.BlockSpec((tm, width), lambda i: (i, 0))

    def whole(shape):
        return pl.BlockSpec(shape, lambda i: (0, 0))

    return pl.pallas_call(
        body, name="out_ln", grid=(seq // tm,),
        in_specs=[rows(cw), rows(cw), rows(d_model), rows(d_model), whole((1, d_model)), whole((1, d_model)),
                  whole((d_model, d_model))],
        out_specs=(rows(d_model), rows(cw), rows(cw), whole((d_model, d_model)), whole((SUBLANES, d_model))),
        out_shape=(jax.ShapeDtypeStruct((seq, d_model), F32), jax.ShapeDtypeStruct((seq, cw), BF16),
                   jax.ShapeDtypeStruct((seq, cw), BF16), jax.ShapeDtypeStruct((d_model, d_model), F32),
                   jax.ShapeDtypeStruct((SUBLANES, d_model), F32)),
        compiler_params=_params(48),
    )(mix_conv, mix_attn, x, target, gain, bias, w_out)


_DP_OF_GROUP = ((0, 0), (0, 1), (0, 2), (0, 3), (1, 0), (2, 0), (2, 1), (1, 1))


def _grad_x(dr, dp_conv, dp_qz, dp_kv, win_all, tm):
    seq, d_model = dr.shape
    nch, _, cw = win_all.shape

    def body(dr_ref, dc_ref, dqz_ref, dkv_ref, w_ref, o_ref):
        parts = (dc_ref, dqz_ref, dkv_ref)
        acc = ALPHA * dr_ref[...]
        for j in range(nch):
            arr, idx = _DP_OF_GROUP[j]
            acc = acc + _dot(parts[arr][idx], w_ref[j], NT)
        o_ref[...] = acc

    def part(n):
        return pl.BlockSpec((n, tm, cw), lambda i: (0, i, 0))

    return pl.pallas_call(
        body, name="grad_x", grid=(seq // tm,),
        in_specs=[pl.BlockSpec((tm, d_model), lambda i: (i, 0)), part(4), part(2), part(2),
                  pl.BlockSpec((nch, d_model, cw), lambda i: (0, 0, 0))],
        out_specs=pl.BlockSpec((tm, d_model), lambda i: (i, 0)),
        out_shape=jax.ShapeDtypeStruct((seq, d_model), F32),
        compiler_params=_params(48),
    )(dr, dp_conv, dp_qz, dp_kv, win_all)


def _grad_w_reduce(xt, dp_parts, gwo, small, w_in, m_in, v_in, w_out, m_out, v_out, row_chunk):
    n_tiles, d_model, tk = xt.shape
    cw = w_in.shape[1]
    rows_out = w_out.shape[0]

    def body(xt_hbm, dpa, dpb, dpc, gwo_ref, small_ref, w_in_ref, m_in_ref, v_in_ref, w_out_ref, m_out_ref, v_out_ref,
             g_in_o, d_in_o, nm_in_o, nv_in_o, g_out_o, d_out_o, nm_out_o, nv_out_o, small_o,
             xt_v, dp_buf, acc, got_in, send_in, recv_in, own_out, got_out, send_out, recv_out, small_all,
             xt_sem, dp_sems, loc_sems, d2d_send, d2d_recv, ici_send, ici_recv, sm_send, sm_recv):
        x, y, c = _mesh_pos()
        me = 4 * x + 2 * y + c
        sibling = (x, y, 1 - c)
        chips = [(1 - x, 1 - y), (1 - x, y), (x, 1 - y)]
        owners = [(*chip, cc) for chip in chips for cc in (1 - c, c)] + [sibling, (x, y, c)]
        group_of = [4 * o[0] + 2 * o[1] + o[2] for o in owners]
        dp_parts_ = (dpa, dpb, dpc)
        dp_groups = [dp_parts_[arr].at[idx] for arr, idx in _DP_OF_GROUP]

        xt_copy = pltpu.make_async_copy(xt_hbm, xt_v, xt_sem)
        xt_copy.start()

        def dp_start(step, tile, buf):
            for k in range(N_DEV):
                @pl.when(group_of[step] == k)
                def _(k=k):
                    pltpu.make_async_copy(dp_groups[k].at[pl.ds(tile * tk, tk), :], dp_buf.at[buf],
                                          dp_sems.at[buf]).start()

        def dp_wait(buf):
            pltpu.make_async_copy(dp_groups[0].at[pl.ds(0, tk), :], dp_buf.at[buf], dp_sems.at[buf]).wait()

        dp_start(0, 0, 0)

        small_all[me] = small_ref[...]
        for d in range(N_DEV):
            @pl.when(d != me)
            def _(d=d):
                pltpu.make_async_remote_copy(
                    src_ref=small_ref, dst_ref=small_all.at[me], send_sem=sm_send.at[d], recv_sem=sm_recv.at[me],
                    device_id=(d // 4, (d // 2) % 2, d % 2), device_id_type=MESH).start()

        def block_out(k):
            return gwo_ref.at[pl.ds(k * rows_out, rows_out), :]

        for k in range(N_DEV):
            s = k // 2

            @pl.when(k % 2 != c)
            def _(k=k, s=s):
                pltpu.make_async_remote_copy(
                    src_ref=block_out(k), dst_ref=got_out.at[s], send_sem=d2d_send.at[1, s],
                    recv_sem=d2d_recv.at[1, s], device_id=sibling, device_id_type=MESH).start()

            @pl.when(k % 2 == c)
            def _(k=k, s=s):
                pltpu.make_async_copy(block_out(k), own_out.at[s], loc_sems.at[s]).start()

        def ici_copy(a, j):
            send, recv = ((send_in, recv_in), (send_out, recv_out))[a]
            return pltpu.make_async_remote_copy(
                src_ref=send.at[j], dst_ref=recv.at[j], send_sem=ici_send.at[a, j], recv_sem=ici_recv.at[a, j],
                device_id=(*chips[j], c), device_id_type=MESH)

        def by_rows(n_rows, fn):
            step = min(row_chunk, n_rows)

            def rows_body(r, _):
                fn(pl.ds(pl.multiple_of(r * step, step), step))
                return 0

            lax.fori_loop(0, n_rows // step, rows_body, 0)

        for s in range(4):
            pltpu.make_async_copy(own_out.at[s], own_out.at[s], loc_sems.at[s]).wait()
            pltpu.make_async_remote_copy(
                src_ref=got_out.at[s], dst_ref=got_out.at[s], send_sem=d2d_send.at[1, s], recv_sem=d2d_recv.at[1, s],
                device_id=sibling, device_id_type=MESH).wait()
        for j, chip in enumerate(chips):
            s = 2 * chip[0] + chip[1]

            def to_send(rows, s=s, j=j):
                send_out[j, rows, :] = (own_out[s, rows, :] + got_out[s, rows, :]).astype(BF16)

            by_rows(rows_out, to_send)
            ici_copy(1, j).start()

        xt_copy.wait()

        def d2d_copy(slot, pair):
            return pltpu.make_async_remote_copy(
                src_ref=acc.at[slot], dst_ref=got_in.at[pair], send_sem=d2d_send.at[0, pair],
                recv_sem=d2d_recv.at[0, pair], device_id=sibling, device_id_type=MESH)

        for step in range(N_DEV):
            slot, pair = step % 2, step // 2
            if step % 2 == 0 and step >= 2:
                d2d_copy(slot, pair - 1).wait_send()

            def tile_body(tile, _, step=step, slot=slot):
                buf = (step * n_tiles + tile) % 2
                dp_wait(buf)

                @pl.when(tile + 1 < n_tiles)
                def _():
                    dp_start(step, tile + 1, 1 - buf)

                if step + 1 < N_DEV:
                    @pl.when(tile + 1 == n_tiles)
                    def _():
                        dp_start(step + 1, 0, 1 - buf)

                part = _dot(xt_v[tile], dp_buf[buf])

                @pl.when(tile == 0)
                def _():
                    acc[slot] = part

                @pl.when(tile > 0)
                def _():
                    acc[slot] += part

                return 0

            lax.fori_loop(0, n_tiles, tile_body, 0)

            if step % 2 == 0:
                d2d_copy(slot, pair).start()
            else:
                d2d_copy(slot, pair).wait_recv()
                if step < N_DEV - 1:
                    def to_send(rows, slot=slot, pair=pair):
                        send_in[pair, rows, :] = (acc[slot, rows, :] + got_in[pair, rows, :]).astype(BF16)

                    by_rows(d_model, to_send)
                    ici_copy(0, pair).start()

        for d in range(N_DEV):
            @pl.when(d != me)
            def _(d=d):
                pltpu.make_async_remote_copy(
                    src_ref=small_ref, dst_ref=small_all.at[d], send_sem=sm_send.at[d], recv_sem=sm_recv.at[d],
                    device_id=(d // 4, (d // 2) % 2, d % 2), device_id_type=MESH).wait()
        total = small_all[0]
        for d in range(1, N_DEV):
            total = total + small_all[d]
        small_o[...] = total

        mine = 2 * x + y
        last = (N_DEV - 1) % 2

        def finish(a, n_rows, chip_sum, outs):
            recv = (recv_in, recv_out)[a]
            for j in range(3):
                ici_copy(a, j).wait()
            g_o, d_o, nm_o, nv_o, w_r, m_r, v_r = outs

            def update(rows):
                g = chip_sum(rows)
                for j in range(3):
                    g = g + recv[j, rows, :].astype(F32)
                delta, nm, nv = _adamw(w_r[rows, :], g, m_r[rows, :], v_r[rows, :])
                g_o[rows, :] = g
                d_o[rows, :] = delta
                nm_o[rows, :] = nm
                nv_o[rows, :] = nv

            by_rows(n_rows, update)

        finish(1, rows_out, lambda rows: own_out[mine, rows, :] + got_out[mine, rows, :],
               (g_out_o, d_out_o, nm_out_o, nv_out_o, w_out_ref, m_out_ref, v_out_ref))
        finish(0, d_model, lambda rows: acc[last, rows, :] + got_in[N_DEV // 2 - 1, rows, :],
               (g_in_o, d_in_o, nm_in_o, nv_in_o, w_in_ref, m_in_ref, v_in_ref))
        d2d_copy(0, N_DEV // 2 - 1).wait_send()

    vmem = pl.BlockSpec(memory_space=pltpu.VMEM)
    hbm = pl.BlockSpec(memory_space=pl.ANY)
    shard_in = jax.ShapeDtypeStruct((d_model, cw), F32)
    shard_out = jax.ShapeDtypeStruct((rows_out, d_model), F32)
    return pl.pallas_call(
        body, name="grad_w_reduce",
        in_specs=[hbm] * 5 + [vmem] * 7,
        out_specs=(vmem,) * 9,
        out_shape=(shard_in,) * 4 + (shard_out,) * 4 + (jax.ShapeDtypeStruct(small.shape, F32),),
        scratch_shapes=[
            pltpu.VMEM((n_tiles, d_model, tk), BF16), pltpu.VMEM((2, tk, cw), BF16), pltpu.VMEM((2, d_model, cw), F32),
            pltpu.VMEM((4, d_model, cw), F32), pltpu.VMEM((3, d_model, cw), BF16), pltpu.VMEM((3, d_model, cw), BF16),
            pltpu.VMEM((4, rows_out, d_model), F32), pltpu.VMEM((4, rows_out, d_model), F32),
            pltpu.VMEM((3, rows_out, d_model), BF16), pltpu.VMEM((3, rows_out, d_model), BF16),
            pltpu.VMEM((N_DEV,) + small.shape, F32),
            pltpu.SemaphoreType.DMA, pltpu.SemaphoreType.DMA((2,)), pltpu.SemaphoreType.DMA((4,)),
            pltpu.SemaphoreType.DMA((2, 4)), pltpu.SemaphoreType.DMA((2, 4)),
            pltpu.SemaphoreType.DMA((2, 3)), pltpu.SemaphoreType.DMA((2, 3)),
            pltpu.SemaphoreType.DMA((N_DEV,)), pltpu.SemaphoreType.DMA((N_DEV,)),
        ],
        compiler_params=_params(56),
    )(xt, *dp_parts, gwo, small, w_in, m_in, v_in, w_out, m_out, v_out)


def _small_update(grads, weights, ms, vs):
    n = len(grads)

    def body(*refs):
        g_refs, w_refs, m_refs, v_refs = (refs[i * n:(i + 1) * n] for i in range(4))
        outs = refs[4 * n:]
        for i in range(n):
            delta, nm, nv = _adamw(w_refs[i][...], g_refs[i][...], m_refs[i][...], v_refs[i][...])
            outs[3 * i][...] = delta
            outs[3 * i + 1][...] = nm
            outs[3 * i + 2][...] = nv

    vmem = pl.BlockSpec(memory_space=pltpu.VMEM)
    out_shape = []
    for w in weights:
        out_shape += [jax.ShapeDtypeStruct(w.shape, F32)] * 3
    return pl.pallas_call(
        body, name="small_update", in_specs=[vmem] * (4 * n), out_specs=(vmem,) * (3 * n), out_shape=tuple(out_shape),
    )(*grads, *weights, *ms, *vs)


def _tile_sizes(seq):
    return dict(tm=min(512, seq), t_ln=min(256, seq), t_attn=min(128, seq), rc=min(256, seq), pairs=4)


def kernel(x, w_in, conv_w, w_out, ln_gain, ln_bias, loss_target, m_w_in, m_conv_w, m_w_out, m_ln_gain, m_ln_bias,
           v_w_in, v_conv_w, v_w_out, v_ln_gain, v_ln_bias):
    assert x.shape[0] == 1 and w_in.shape[0] == 1, "one sequence per device, depth 1"
    _, seq, d_model = x.shape
    cw = w_in.shape[2]
    conv_k, conv_cols = conv_w.shape[1], conv_w.shape[2]
    rows_out = w_out.shape[1]
    assert cw == d_model // 2 and cw % PAIR == 0 and conv_cols * N_DEV == cw and rows_out * N_DEV == d_model
    ts = _tile_sizes(seq)

    x2 = x.reshape(seq, d_model)
    target = loss_target.reshape(seq, d_model)
    me = 4 * lax.axis_index("x") + 2 * lax.axis_index("y") + lax.axis_index("c")

    conv_pad = jnp.pad(conv_w[0], ((0, SUBLANES - conv_k), (0, PAIR - conv_cols)))
    proj, xt, win_all, wout_all, conv_all = _gather_proj(x2, w_in[0], w_out[0], conv_pad, ts["tm"])
    w_out_full = wout_all.reshape(d_model, d_model)
    conv_full = conv_all[:, :conv_k, :conv_cols].transpose(1, 0, 2).reshape(conv_k, cw)
    conv_full = jnp.pad(conv_full, ((0, SUBLANES - conv_k), (0, 0)))

    mix_conv = _conv_fwd(proj, conv_full, ts["rc"])
    pairs = min(ts["pairs"], cw // PAIR)
    tri = _triangles(ts["t_attn"])
    o, mix_attn, tot = _attn_fwd(proj, tri, ts["t_attn"], pairs)
    dr, d_mix_conv, d_mix_attn, gwo, small = _out_ln(mix_conv, mix_attn, x2, target, ln_gain, ln_bias, w_out_full,
                                                     ts["t_ln"])
    dp_conv, d_taps = _conv_bwd(proj, conv_full, d_mix_conv, ts["rc"])
    dp_qz, dp_kv = _attn_bwd(proj, tri, o, tot, d_mix_attn, ts["t_attn"], pairs)
    grad_x = _grad_x(dr, dp_conv, dp_qz, dp_kv, win_all, ts["tm"])
    small = small.at[ROW_CONV:ROW_CONV + conv_k, :cw].set(d_taps[:conv_k])
    (g_in, d_in, nm_in, nv_in, g_out, d_out, nm_out, nv_out, small_sum) = _grad_w_reduce(
        xt, (dp_conv, dp_qz, dp_kv), gwo, small, w_in[0], m_w_in[0], v_w_in[0], w_out[0], m_w_out[0], v_w_out[0], 128)

    loss = small_sum[ROW_LOSS, 0]
    g_gain = small_sum[ROW_GAIN:ROW_GAIN + 1]
    g_bias = small_sum[ROW_BIAS:ROW_BIAS + 1]
    g_conv = lax.dynamic_slice(small_sum, (ROW_CONV, me * conv_cols), (conv_k, conv_cols))
    upd = _small_update((g_conv, g_gain, g_bias), (conv_w[0], ln_gain, ln_bias),
                        (m_conv_w[0], m_ln_gain, m_ln_bias), (v_conv_w[0], v_ln_gain, v_ln_bias))
    d_conv, nm_conv, nv_conv, d_gain, nm_gain, nv_gain, d_bias, nm_bias, nv_bias = upd

    lead = lambda a: a[None]
    return (loss, grad_x.reshape(1, seq, d_model), lead(g_in), lead(g_conv), lead(g_out), g_gain, g_bias,
            lead(d_in), lead(d_conv), lead(d_out), d_gain, d_bias,
            lead(nm_in), lead(nm_conv), lead(nm_out), nm_gain, nm_bias,
            lead(nv_in), lead(nv_conv), lead(nv_out), nv_gain, nv_bias)
```

```python
import functools

import jax
import jax.numpy as jnp
from jax import lax
from jax.experimental import pallas as pl
from jax.experimental.pallas import tpu as pltpu

F32 = jnp.float32
BF16 = jnp.bfloat16
MESH = pl.DeviceIdType.MESH

N_DEV = 8
HEAD_DIM = 64
PAIR = 128
SUBLANES = 8
LN_EPS = 1e-5
ALPHA = 2.0 ** 0.25
ADAM_LR, ADAM_B1, ADAM_B2, ADAM_EPS, ADAM_WD, ADAM_STEP = 0.001, 0.9, 0.999, 1e-08, 0.01, 10

ROW_GAIN, ROW_BIAS, ROW_CONV, ROW_LOSS = 0, 1, 2, 5

NT = (((1,), (1,)), ((), ()))
TN = (((0,), (0,)), ((), ()))


V7X_VMEM_BYTES = 64 * 1024 * 1024


def _params(vmem_mib):
    assert vmem_mib * 1024 * 1024 < V7X_VMEM_BYTES
    return pltpu.CompilerParams(vmem_limit_bytes=vmem_mib * 1024 * 1024)


def _dot(a, b, dims=None):
    if dims is None:
        return jnp.dot(a, b, preferred_element_type=F32)
    return lax.dot_general(a, b, dims, preferred_element_type=F32)


def _sigmoid(z):
    return 1.0 / (1.0 + jnp.exp(-z))


def _mesh_pos():
    return lax.axis_index("x"), lax.axis_index("y"), lax.axis_index("c")


def _adamw(w, g, m, v):
    nm = ADAM_B1 * m + (1.0 - ADAM_B1) * g
    nv = ADAM_B2 * v + (1.0 - ADAM_B2) * (g * g)
    m_hat = nm * (1.0 / (1.0 - ADAM_B1 ** ADAM_STEP))
    v_hat = nv * (1.0 / (1.0 - ADAM_B2 ** ADAM_STEP))
    delta = -ADAM_LR * (m_hat / (jnp.sqrt(v_hat) + ADAM_EPS) + ADAM_WD * w)
    return delta, nm, nv


def _gather_proj(x, w_in_s, w_out_s, conv_s, tm):
    seq, d_model = x.shape
    cw = w_in_s.shape[1]
    rows_out = w_out_s.shape[0]
    n_tiles = seq // tm
    half = d_model // 2
    SIB, X_UP, X_LOW, Y_UP, Y_LOW, VIA_Y, VIA_X, ON_X, ON_Y, ON_DIAG = range(10)

    def body(x_hbm, win_ref, wout_ref, conv_ref, proj_hbm, xt_hbm, win_all, wout_all, conv_all,
             xb, x_stage, o_stage, xt_stage, x_sems, o_sems, xt_sems, w_send, w_recv, send_sems, recv_sems):
        x, y, c = _mesh_pos()
        me = (x, y, c)
        sibling = (x, y, 1 - c)
        x_nbr, y_nbr, diag = (1 - x, y), (x, 1 - y), (1 - x, 1 - y)
        chips = [x_nbr, y_nbr, diag]
        small = (wout_all, conv_all)

        def slot(pos):
            return 4 * pos[0] + 2 * pos[1] + pos[2]

        def x_copy(tile, buf):
            return pltpu.make_async_copy(x_hbm.at[pl.ds(tile * tm, tm), :], x_stage.at[buf], x_sems.at[buf])

        x_copy(0, 0).start()
        win_all[slot(me)] = win_ref[...].astype(BF16)
        wout_all[slot(me)] = wout_ref[...].astype(BF16)
        conv_all[slot(me)] = conv_ref[...]

        def w_copy(k, block, part, to):
            ref = win_all.at[slot(block)]
            if part is not None:
                ref = ref.at[pl.ds(part * half, half), :]
            return pltpu.make_async_remote_copy(
                src_ref=ref, dst_ref=ref, send_sem=w_send.at[k], recv_sem=w_recv.at[k],
                device_id=to, device_id_type=MESH)

        def copy(a, k, block, to):
            ref = small[a].at[slot(block)]
            return pltpu.make_async_remote_copy(
                src_ref=ref, dst_ref=ref, send_sem=send_sems.at[a, k], recv_sem=recv_sems.at[a, k],
                device_id=to, device_id_type=MESH)

        sends = [w_copy(SIB, me, None, sibling),
                 w_copy(X_UP, me, 0, (*x_nbr, c)), w_copy(Y_LOW, me, 1, (*y_nbr, c)),
                 w_copy(X_LOW, me, 1, (*x_nbr, c)), w_copy(Y_UP, me, 0, (*y_nbr, c))]
        for a in range(len(small)):
            sends.append(copy(a, 0, me, sibling))
            sends += [copy(a, 1 + j, me, (*chip, c)) for j, chip in enumerate(chips)]
        for cp in sends:
            cp.start()

        def o_copy(group, tile, buf):
            return pltpu.make_async_copy(o_stage.at[buf], proj_hbm.at[group, pl.ds(tile * tm, tm), :], o_sems.at[buf])

        def xt_copy(tile, buf):
            return pltpu.make_async_copy(xt_stage.at[buf], xt_hbm.at[tile], xt_sems.at[buf])

        def project(order, group, first_pass):
            def tile_body(tile, _):
                if first_pass:
                    buf = tile % 2
                    x_copy(tile, buf).wait()

                    @pl.when(tile + 1 < n_tiles)
                    def _():
                        x_copy(tile + 1, 1 - buf).start()

                    xv = x_stage[buf]
                    xb[tile] = xv.astype(BF16)

                    @pl.when(tile >= 2)
                    def _():
                        xt_copy(tile - 2, buf).wait()

                    xt_stage[buf] = xv.T.astype(BF16)
                    xt_copy(tile, buf).start()
                count = order * n_tiles + tile
                obuf = count % 2

                @pl.when(count >= 2)
                def _():
                    o_copy(group, tile, obuf).wait()

                o_stage[obuf] = _dot(xb[tile], win_all[group]).astype(BF16)
                o_copy(group, tile, obuf).start()
                return 0

            lax.fori_loop(0, n_tiles, tile_body, 0)

        def start(cp):
            cp.start()
            sends.append(cp)

        def small_pass_on(j):
            for a in range(len(small)):
                copy(a, 1 + j, (*chips[j], c), me).wait_recv()
                start(copy(a, 4 + j, (*chips[j], c), sibling))

        def small_from_sibling(k):
            for a in range(len(small)):
                copy(a, k, sibling, me).wait_recv()

        project(0, slot(me), True)
        w_copy(SIB, sibling, None, me).wait_recv()
        small_from_sibling(0)
        project(1, slot(sibling), False)
        w_copy(X_UP, (*x_nbr, c), 0, me).wait_recv()
        start(w_copy(VIA_Y, (*x_nbr, c), 0, (*y_nbr, c)))
        w_copy(Y_LOW, (*y_nbr, c), 1, me).wait_recv()
        start(w_copy(VIA_X, (*y_nbr, c), 1, (*x_nbr, c)))
        w_copy(X_LOW, (*x_nbr, c), 1, me).wait_recv()
        start(w_copy(ON_X, (*x_nbr, c), None, sibling))
        small_pass_on(0)
        project(2, slot((*x_nbr, c)), False)
        w_copy(Y_UP, (*y_nbr, c), 0, me).wait_recv()
        start(w_copy(ON_Y, (*y_nbr, c), None, sibling))
        small_pass_on(1)
        project(3, slot((*y_nbr, c)), False)
        w_copy(ON_X, (*x_nbr, 1 - c), None, me).wait_recv()
        small_from_sibling(4)
        project(4, slot((*x_nbr, 1 - c)), False)
        w_copy(ON_Y, (*y_nbr, 1 - c), None, me).wait_recv()
        small_from_sibling(5)
        project(5, slot((*y_nbr, 1 - c)), False)
        w_copy(VIA_Y, (*diag, c), 0, me).wait_recv()
        w_copy(VIA_X, (*diag, c), 1, me).wait_recv()
        start(w_copy(ON_DIAG, (*diag, c), None, sibling))
        small_pass_on(2)
        project(6, slot((*diag, c)), False)
        w_copy(ON_DIAG, (*diag, 1 - c), None, me).wait_recv()
        small_from_sibling(6)
        project(7, slot((*diag, 1 - c)), False)

        for buf in range(2):
            o_copy(0, 0, buf).wait()
        for buf in range(min(2, n_tiles)):
            xt_copy(0, buf).wait()
        for cp in sends:
            cp.wait_send()

    vmem = pl.BlockSpec(memory_space=pltpu.VMEM)
    hbm = pl.BlockSpec(memory_space=pl.ANY)
    return pl.pallas_call(
        body, name="gather_proj",
        out_shape=(jax.ShapeDtypeStruct((N_DEV, seq, cw), BF16),
                   jax.ShapeDtypeStruct((n_tiles, d_model, tm), BF16),
                   jax.ShapeDtypeStruct((N_DEV, d_model, cw), BF16),
                   jax.ShapeDtypeStruct((N_DEV, rows_out, d_model), BF16),
                   jax.ShapeDtypeStruct((N_DEV,) + conv_s.shape, F32)),
        in_specs=[hbm, vmem, vmem, vmem], out_specs=(hbm, hbm, vmem, vmem, vmem),
        scratch_shapes=[
            pltpu.VMEM((n_tiles, tm, d_model), BF16), pltpu.VMEM((2, tm, d_model), F32),
            pltpu.VMEM((2, tm, cw), BF16), pltpu.VMEM((2, d_model, tm), BF16),
            pltpu.SemaphoreType.DMA((2,)), pltpu.SemaphoreType.DMA((2,)), pltpu.SemaphoreType.DMA((2,)),
            pltpu.SemaphoreType.DMA((10,)), pltpu.SemaphoreType.DMA((10,)),
            pltpu.SemaphoreType.DMA((2, 7)), pltpu.SemaphoreType.DMA((2, 7))],
        compiler_params=_params(48),
    )(x, w_in_s, w_out_s, conv_s)


def _conv_taps(ext, w_ref, rc):
    u0 = ext[SUBLANES:SUBLANES + rc]
    u1 = pltpu.roll(ext, 1, 0)[SUBLANES:SUBLANES + rc]
    u2 = pltpu.roll(ext, 2, 0)[SUBLANES:SUBLANES + rc]
    return w_ref[2:3, :] * u0 + w_ref[1:2, :] * u1 + w_ref[0:1, :] * u2, u0, u1, u2


def _conv_fwd(proj, conv_full, rc):
    _, seq, cw = proj.shape

    def body(b_ref, c_ref, h_ref, z_ref, w_ref, o_ref, u_scr):
        u_scr[0:SUBLANES, :] = jnp.zeros((SUBLANES, PAIR), F32)

        def fill(r, _):
            base = pl.multiple_of(r * rc, rc)
            rows = pl.ds(base, rc)
            u_scr[pl.ds(base + SUBLANES, rc), :] = c_ref[rows, :].astype(F32) * h_ref[rows, :].astype(F32)
            return 0

        lax.fori_loop(0, seq // rc, fill, 0)

        def out(r, _):
            base = pl.multiple_of(r * rc, rc)
            rows = pl.ds(base, rc)
            ext = u_scr[pl.ds(base, rc + SUBLANES), :]
            y, _, _, _ = _conv_taps(ext, w_ref, rc)
            z = z_ref[rows, :].astype(F32)
            o_ref[rows, :] = (z * _sigmoid(z) * b_ref[rows, :].astype(F32) * y).astype(BF16)
            return 0

        lax.fori_loop(0, seq // rc, out, 0)

    def chunk(j):
        return pl.BlockSpec((None, seq, PAIR), lambda cb, j=j: (j, 0, cb))

    return pl.pallas_call(
        body, name="conv_fwd", grid=(cw // PAIR,),
        in_specs=[chunk(0), chunk(1), chunk(2), chunk(3), pl.BlockSpec((SUBLANES, PAIR), lambda cb: (0, cb))],
        out_specs=pl.BlockSpec((seq, PAIR), lambda cb: (0, cb)),
        out_shape=jax.ShapeDtypeStruct((seq, cw), BF16),
        scratch_shapes=[pltpu.VMEM((seq + SUBLANES, PAIR), F32)],
    )(proj, proj, proj, proj, conv_full)


def _conv_bwd(proj, conv_full, d_mix_conv, rc):
    _, seq, cw = proj.shape

    def body(b_ref, c_ref, h_ref, z_ref, w_ref, g_ref, dp_ref, dw_ref, u_scr, dy_scr):
        u_scr[0:SUBLANES, :] = jnp.zeros((SUBLANES, PAIR), F32)
        dy_scr[seq:seq + SUBLANES, :] = jnp.zeros((SUBLANES, PAIR), F32)

        def fill(r, _):
            base = pl.multiple_of(r * rc, rc)
            rows = pl.ds(base, rc)
            u_scr[pl.ds(base + SUBLANES, rc), :] = c_ref[rows, :].astype(F32) * h_ref[rows, :].astype(F32)
            return 0

        lax.fori_loop(0, seq // rc, fill, 0)

        def gate(r, acc):
            base = pl.multiple_of(r * rc, rc)
            rows = pl.ds(base, rc)
            ext = u_scr[pl.ds(base, rc + SUBLANES), :]
            y, u0, u1, u2 = _conv_taps(ext, w_ref, rc)
            z = z_ref[rows, :].astype(F32)
            b = b_ref[rows, :].astype(F32)
            g = g_ref[rows, :].astype(F32)
            sig = _sigmoid(z)
            dp_ref[3, rows, :] = (g * b * y * (sig * (1.0 + z * (1.0 - sig)))).astype(BF16)
            gs = g * (z * sig)
            dp_ref[0, rows, :] = (gs * y).astype(BF16)
            dy = gs * b
            dy_scr[rows, :] = dy
            a0, a1, a2 = acc
            return (a0 + jnp.sum(dy * u2, axis=0, keepdims=True),
                    a1 + jnp.sum(dy * u1, axis=0, keepdims=True),
                    a2 + jnp.sum(dy * u0, axis=0, keepdims=True))

        zero = jnp.zeros((1, PAIR), F32)
        a0, a1, a2 = lax.fori_loop(0, seq // rc, gate, (zero, zero, zero))
        dw_ref[...] = jnp.zeros((SUBLANES, PAIR), F32)
        dw_ref[0:1, :] = a0
        dw_ref[1:2, :] = a1
        dw_ref[2:3, :] = a2

        def back(r, _):
            base = pl.multiple_of(r * rc, rc)
            rows = pl.ds(base, rc)
            ext = dy_scr[pl.ds(base, rc + SUBLANES), :]
            n = rc + SUBLANES
            d0 = ext[0:rc]
            d1 = pltpu.roll(ext, n - 1, 0)[0:rc]
            d2 = pltpu.roll(ext, n - 2, 0)[0:rc]
            du = w_ref[2:3, :] * d0 + w_ref[1:2, :] * d1 + w_ref[0:1, :] * d2
            dp_ref[1, rows, :] = (du * h_ref[rows, :].astype(F32)).astype(BF16)
            dp_ref[2, rows, :] = (du * c_ref[rows, :].astype(F32)).astype(BF16)
            return 0

        lax.fori_loop(0, seq // rc, back, 0)

    def chunk(j):
        return pl.BlockSpec((None, seq, PAIR), lambda cb, j=j: (j, 0, cb))

    return pl.pallas_call(
        body, name="conv_bwd", grid=(cw // PAIR,),
        in_specs=[chunk(0), chunk(1), chunk(2), chunk(3), pl.BlockSpec((SUBLANES, PAIR), lambda cb: (0, cb)),
                  pl.BlockSpec((seq, PAIR), lambda cb: (0, cb))],
        out_specs=(pl.BlockSpec((4, seq, PAIR), lambda cb: (0, 0, cb)),
                   pl.BlockSpec((SUBLANES, PAIR), lambda cb: (0, cb))),
        out_shape=(jax.ShapeDtypeStruct((4, seq, cw), BF16), jax.ShapeDtypeStruct((SUBLANES, cw), F32)),
        scratch_shapes=[pltpu.VMEM((seq + SUBLANES, PAIR), F32), pltpu.VMEM((seq + SUBLANES, PAIR), F32)],
    )(proj, proj, proj, proj, conv_full, d_mix_conv)


SKIP_CARRY = 104.0
LANE_TOT0, LANE_TOT1, LANE_FIRST = 0, 1, 2
FAST_BLOCKS = 3


def _triangles(t):
    row = lax.broadcasted_iota(jnp.int32, (2 * t, 2 * t), 0)
    col = lax.broadcasted_iota(jnp.int32, (2 * t, 2 * t), 1)
    same = (row < t) == (col < t)
    upper = jnp.logical_and(same, row > col).astype(BF16)
    lower = jnp.logical_and(same, row < col).astype(BF16)
    return jnp.stack([jnp.concatenate([upper, upper], axis=0), jnp.concatenate([lower, lower], axis=0)])


def _pair_masks(t):
    lane = lax.broadcasted_iota(jnp.int32, (t, PAIR), 1)
    qrow = lax.broadcasted_iota(jnp.int32, (t, 2 * t), 0)
    kcol = lax.broadcasted_iota(jnp.int32, (t, 2 * t), 1)
    strict = jnp.where(kcol < t, kcol, kcol - t) < qrow
    return lane, lane < HEAD_DIM, strict


def _by_head(x, head0):
    zero = jnp.zeros_like(x)
    return jnp.concatenate([jnp.where(head0, x, zero), jnp.where(head0, zero, x)], axis=0)


def _hi_lo(a):
    hi = a.astype(BF16)
    lo = (a - hi.astype(F32)).astype(BF16)
    return jnp.concatenate([hi, lo], axis=1)


def _softplus_parts(z, strict, masked):
    spu = jnp.maximum(z, 0.0) + jnp.log(1.0 + jnp.exp(-jnp.abs(z)))
    return z - spu, (jnp.where(strict, spu, 0.0) if masked else spu)


def _stacked_dot(parts, rhs):
    rows = parts[0].shape[0]
    out = _dot(jnp.concatenate(parts, axis=0), rhs)
    return [out[n * rows:(n + 1) * rows] for n in range(len(parts))]


def _attn_fwd(proj, tri, t, pp):
    _, seq, cw = proj.shape
    scale = HEAD_DIM ** -0.5
    width = pp * PAIR

    def body(q_ref, k_ref, v_ref, za_ref, tri_ref, o_ref, mix_ref, tot_ref):
        i = pl.program_id(1)
        lane, head0, strict = _pair_masks(t)
        upper = tri_ref[0]
        q = q_ref[...] * scale

        def sweep(blocks, state):
            staged = []
            for j, masked in blocks:
                start = pl.multiple_of(j * t, t)
                kb = k_ref[pl.ds(start, t), :]
                vb = v_ref[pl.ds(start, t), :]
                for p in range(pp):
                    cols = slice(p * PAIR, (p + 1) * PAIR)
                    z = _dot(q[:, cols], _by_head(kb[:, cols], head0), NT)
                    ls, sp = _softplus_parts(z, strict, masked)
                    staged.append((p, masked, ls, sp, _by_head(vb[:, cols], head0)))
            afters = _stacked_dot([_hi_lo(sp) for _, _, _, sp, _ in staged], upper)
            state = list(state)
            for (p, masked, ls, sp, v2), after in zip(staged, afters):
                (c0, c1), acc = state[p]
                x = ls - after
                w = jnp.exp(jnp.concatenate([x[:, :t] - c0, x[:, t:] - c1], axis=1))
                if masked:
                    w = jnp.where(strict, w, 0.0)
                c0 = c0 + (after[:, 0:1] + sp[:, 0:1])
                c1 = c1 + (after[:, t:t + 1] + sp[:, t:t + 1])
                state[p] = ((c0, c1), acc + _dot(w.astype(BF16), v2))
            return tuple(state)

        def unfinished(state):
            m = state[0][0][0]
            for p in range(pp):
                m = jnp.minimum(m, jnp.minimum(state[p][0][0], state[p][0][1]))
            return jnp.min(m) < SKIP_CARRY

        def step(js):
            state = sweep(((js[0], False),), js[1])
            return js[0] - 1, state, unfinished(state)

        zcol = jnp.zeros((t, 1), F32)
        init = tuple(((zcol, zcol), jnp.zeros((t, PAIR), F32)) for _ in range(pp))
        many = i >= FAST_BLOCKS - 1
        state = lax.cond(
            many,
            lambda: sweep(((i, True),) + tuple((i - b, False) for b in range(1, FAST_BLOCKS)), init),
            lambda: sweep(((i, True),), init))
        j_end, state, _ = lax.while_loop(
            lambda js: jnp.logical_and(js[0] >= 0, js[2]), step,
            (jnp.where(many, i - FAST_BLOCKS, i - 1), state, unfinished(state)))
        first = (j_end + 1).astype(F32)
        za = za_ref[...].astype(F32)
        for p in range(pp):
            (c0, c1), acc = state[p]
            cols = slice(p * PAIR, (p + 1) * PAIR)
            zp = za[:, cols]
            o_ref[:, cols] = acc.astype(BF16)
            mix_ref[:, cols] = (zp * _sigmoid(zp) * acc).astype(BF16)
            tot_ref[:, cols] = jnp.where(lane == LANE_TOT0, c0, jnp.where(lane == LANE_TOT1, c1, first))

    def tile(j):
        return pl.BlockSpec((None, t, width), lambda g, i, j=j: (j, i, g))

    def full(j):
        return pl.BlockSpec((None, seq, width), lambda g, i, j=j: (j, 0, g))

    out_tile = pl.BlockSpec((t, width), lambda g, i: (i, g))
    return pl.pallas_call(
        body, name="attn_fwd", grid=(cw // width, seq // t),
        in_specs=[tile(4), full(5), full(6), tile(7), pl.BlockSpec(tri.shape, lambda g, i: (0, 0, 0))],
        out_specs=(out_tile, out_tile, out_tile),
        out_shape=(jax.ShapeDtypeStruct((seq, cw), BF16), jax.ShapeDtypeStruct((seq, cw), BF16),
                   jax.ShapeDtypeStruct((seq, cw), F32)),
    )(proj, proj, proj, proj, tri)


def _attn_bwd(proj, tri, o, tot, d_mix_attn, t, pp):
    _, seq, cw = proj.shape
    nb = seq // t
    scale = HEAD_DIM ** -0.5
    width = pp * PAIR

    def body(q_ref, k_ref, v_ref, za_ref, tri_ref, o_ref, tot_ref, g_ref, dqz_ref, dkv_ref, dk_acc, dv_acc):
        i = pl.program_id(1)

        @pl.when(i == 0)
        def _():
            dk_acc[...] = jnp.zeros_like(dk_acc)
            dv_acc[...] = jnp.zeros_like(dv_acc)

        _, head0, strict = _pair_masks(t)
        upper, lower = tri_ref[0], tri_ref[1, 0:2 * t, :]
        za = za_ref[...].astype(F32)
        g = g_ref[...].astype(F32)
        sig = _sigmoid(za)
        dqz_ref[1] = (g * o_ref[...].astype(F32) * (sig * (1.0 + za * (1.0 - sig)))).astype(BF16)
        do = (g * (za * sig)).astype(BF16)
        q = q_ref[...] * scale
        tot_v = tot_ref[...]
        q2, do2, init = [], [], []
        zcol = jnp.zeros((t, 1), F32)
        for p in range(pp):
            cols = slice(p * PAIR, (p + 1) * PAIR)
            q2.append(_by_head(q[:, cols], head0))
            do2.append(_by_head(do[:, cols], head0))
            tp = tot_v[:, cols]
            init.append(((tp[:, LANE_TOT0:LANE_TOT0 + 1], tp[:, LANE_TOT1:LANE_TOT1 + 1]), (zcol, zcol),
                         jnp.zeros((t, PAIR), F32)))
        first = jnp.clip(tot_v[0:1, LANE_FIRST:LANE_FIRST + 1], 0.0, i.astype(F32)).astype(jnp.int32)[0, 0]

        def sweep(blocks, state):
            staged = []
            for j, masked in blocks:
                start = pl.multiple_of(j * t, t)
                kb = k_ref[pl.ds(start, t), :]
                vb = v_ref[pl.ds(start, t), :]
                for p in range(pp):
                    cols = slice(p * PAIR, (p + 1) * PAIR)
                    k2 = _by_head(kb[:, cols], head0)
                    z = _dot(q[:, cols], k2, NT)
                    ls, sp = _softplus_parts(z, strict, masked)
                    da = _dot(do[:, cols], _by_head(vb[:, cols], head0), NT)
                    staged.append((p, masked, k2, ls, sp, da))
            afters = _stacked_dot([_hi_lo(sp) for _, _, _, _, sp, _ in staged], upper)
            state = list(state)
            weights, ggs = [], []
            for (p, masked, k2, ls, sp, da), after in zip(staged, afters):
                (r0, r1), befores, dq = state[p]
                r0 = r0 - (after[:, 0:1] + sp[:, 0:1])
                r1 = r1 - (after[:, t:t + 1] + sp[:, t:t + 1])
                x = ls - after
                a = jnp.exp(jnp.concatenate([x[:, :t] - r0, x[:, t:] - r1], axis=1))
                if masked:
                    a = jnp.where(strict, a, 0.0)
                state[p] = ((r0, r1), befores, dq)
                weights.append(a.astype(BF16))
                ggs.append(a * da)
            pres = _stacked_dot([gg.astype(BF16) for gg in ggs], lower)
            dzs = []
            for (p, masked, k2, ls, sp, da), gg, pre in zip(staged, ggs, pres):
                rests, (b0, b1), dq = state[p]
                y = gg + pre
                dz = gg - jnp.exp(ls) * jnp.concatenate([y[:, :t] + b0, y[:, t:] + b1], axis=1)
                if masked:
                    dz = jnp.where(strict, dz, 0.0)
                dzb = dz.astype(BF16)
                dzs.append(dzb)
                state[p] = (rests, (b0 + y[:, t - 1:t], b1 + y[:, 2 * t - 1:2 * t]), dq + _dot(dzb, k2))
            first_row = pl.multiple_of(blocks[0][0] * t, t)
            n_rows = len(blocks) * t
            for p in range(pp):
                cols = slice(p * PAIR, (p + 1) * PAIR)

                def by_key(tiles):
                    return jnp.concatenate(
                        [jnp.concatenate([m[:, :t], m[:, t:]], axis=0).T for m in tiles], axis=0)

                mine = [n for n in range(len(staged)) if staged[n][0] == p]
                dk_acc[pl.ds(first_row, n_rows), cols] += _dot(by_key([dzs[n] for n in mine]), q2[p])
                dv_acc[pl.ds(first_row, n_rows), cols] += _dot(by_key([weights[n] for n in mine]), do2[p])
            return tuple(state)

        many = i >= FAST_BLOCKS - 1
        last_single = jnp.where(many, i - (FAST_BLOCKS - 1), i)
        state = lax.fori_loop(first, last_single, lambda j, s: sweep(((j, False),), s), tuple(init))
        state = lax.cond(
            many,
            lambda: sweep(tuple((i - b, False) for b in range(FAST_BLOCKS - 1, 0, -1)) + ((i, True),), state),
            lambda: sweep(((i, True),), state))
        for p in range(pp):
            dqz_ref[0, :, p * PAIR:(p + 1) * PAIR] = (state[p][2] * scale).astype(BF16)

        @pl.when(i == nb - 1)
        def _():
            dkv_ref[0] = dk_acc[...].astype(BF16)
            dkv_ref[1] = dv_acc[...].astype(BF16)

    def tile(j):
        return pl.BlockSpec((None, t, width), lambda g, i, j=j: (j, i, g))

    def full(j):
        return pl.BlockSpec((None, seq, width), lambda g, i, j=j: (j, 0, g))

    flat_tile = pl.BlockSpec((t, width), lambda g, i: (i, g))
    return pl.pallas_call(
        body, name="attn_bwd", grid=(cw // width, nb),
        in_specs=[tile(4), full(5), full(6), tile(7), pl.BlockSpec(tri.shape, lambda g, i: (0, 0, 0)),
                  flat_tile, flat_tile, flat_tile],
        out_specs=(pl.BlockSpec((2, t, width), lambda g, i: (0, i, g)),
                   pl.BlockSpec((2, seq, width), lambda g, i: (0, 0, g))),
        out_shape=(jax.ShapeDtypeStruct((2, seq, cw), BF16), jax.ShapeDtypeStruct((2, seq, cw), BF16)),
        scratch_shapes=[pltpu.VMEM((seq, width), F32), pltpu.VMEM((seq, width), F32)],
        compiler_params=_params(48),
    )(proj, proj, proj, proj, tri, o, tot, d_mix_attn)


def _out_ln(mix_conv, mix_attn, x, target, gain, bias, w_out, tm):
    seq, d_model = x.shape
    cw = mix_conv.shape[1]
    inv_d = 1.0 / d_model

    def body(mc_ref, ma_ref, x_ref, t_ref, gain_ref, bias_ref, w_ref, dr_ref, dmc_ref, dma_ref, gwo_ref, small_ref):
        @pl.when(pl.program_id(0) == 0)
        def _():
            gwo_ref[...] = jnp.zeros_like(gwo_ref)
            small_ref[...] = jnp.zeros_like(small_ref)

        mix = jnp.concatenate([mc_ref[...], ma_ref[...]], axis=1)
        w = w_ref[...]
        r = ALPHA * x_ref[...] + _dot(mix, w)
        mu = jnp.sum(r, axis=1, keepdims=True) * inv_d
        xc = r - mu
        var = jnp.sum(xc * xc, axis=1, keepdims=True) * inv_d
        rstd = lax.rsqrt(var + LN_EPS)
        xhat = xc * rstd
        gain_v = gain_ref[...]
        err = xhat * gain_v + bias_ref[...] - t_ref[...]
        row_loss = jnp.sum(err * err, axis=1, keepdims=True)
        loss = (0.5 * inv_d) * jnp.sum(row_loss, axis=0, keepdims=True)
        dy = err * inv_d
        small_ref[ROW_GAIN:ROW_GAIN + 1, :] += jnp.sum(dy * xhat, axis=0, keepdims=True)
        small_ref[ROW_BIAS:ROW_BIAS + 1, :] += jnp.sum(dy, axis=0, keepdims=True)
        small_ref[ROW_LOSS:ROW_LOSS + 1, :] += jnp.broadcast_to(loss, (1, d_model))
        dxhat = dy * gain_v
        m1 = jnp.sum(dxhat, axis=1, keepdims=True) * inv_d
        m2 = jnp.sum(dxhat * xhat, axis=1, keepdims=True) * inv_d
        dr = rstd * (dxhat - m1 - xhat * m2)
        dr_ref[...] = dr
        drb = dr.astype(BF16)
        dmix = _dot(drb, w, NT)
        dmc_ref[...] = dmix[:, :cw].astype(BF16)
        dma_ref[...] = dmix[:, cw:].astype(BF16)
        gwo_ref[...] += _dot(mix, drb, TN)

    def rows(width):
        return pl.BlockSpec((tm, width), lambda i: (i, 0))

    def whole(shape):
        return pl.BlockSpec(shape, lambda i: (0, 0))

    return pl.pallas_call(
        body, name="out_ln", grid=(seq // tm,),
        in_specs=[rows(cw), rows(cw), rows(d_model), rows(d_model), whole((1, d_model)), whole((1, d_model)),
                  whole((d_model, d_model))],
        out_specs=(rows(d_model), rows(cw), rows(cw), whole((d_model, d_model)), whole((SUBLANES, d_model))),
        out_shape=(jax.ShapeDtypeStruct((seq, d_model), F32), jax.ShapeDtypeStruct((seq, cw), BF16),
                   jax.ShapeDtypeStruct((seq, cw), BF16), jax.ShapeDtypeStruct((d_model, d_model), F32),
                   jax.ShapeDtypeStruct((SUBLANES, d_model), F32)),
        compiler_params=_params(48),
    )(mix_conv, mix_attn, x, target, gain, bias, w_out)


_DP_OF_GROUP = ((0, 0), (0, 1), (0, 2), (0, 3), (1, 0), (2, 0), (2, 1), (1, 1))


def _grad_x(dr, dp_conv, dp_qz, dp_kv, win_all, tm):
    seq, d_model = dr.shape
    nch, _, cw = win_all.shape

    def body(dr_ref, dc_ref, dqz_ref, dkv_ref, w_ref, o_ref):
        parts = (dc_ref, dqz_ref, dkv_ref)
        acc = ALPHA * dr_ref[...]
        for j in range(nch):
            arr, idx = _DP_OF_GROUP[j]
            acc = acc + _dot(parts[arr][idx], w_ref[j], NT)
        o_ref[...] = acc

    def part(n):
        return pl.BlockSpec((n, tm, cw), lambda i: (0, i, 0))

    return pl.pallas_call(
        body, name="grad_x", grid=(seq // tm,),
        in_specs=[pl.BlockSpec((tm, d_model), lambda i: (i, 0)), part(4), part(2), part(2),
                  pl.BlockSpec((nch, d_model, cw), lambda i: (0, 0, 0))],
        out_specs=pl.BlockSpec((tm, d_model), lambda i: (i, 0)),
        out_shape=jax.ShapeDtypeStruct((seq, d_model), F32),
        compiler_params=_params(48),
    )(dr, dp_conv, dp_qz, dp_kv, win_all)


def _grad_w_reduce(xt, dp_parts, gwo, small, w_in, m_in, v_in, w_out, m_out, v_out, row_chunk):
    n_tiles, d_model, tk = xt.shape
    cw = w_in.shape[1]
    rows_out = w_out.shape[0]

    def body(xt_hbm, dpa, dpb, dpc, gwo_ref, small_ref, w_in_ref, m_in_ref, v_in_ref, w_out_ref, m_out_ref, v_out_ref,
             g_in_o, d_in_o, nm_in_o, nv_in_o, g_out_o, d_out_o, nm_out_o, nv_out_o, small_o,
             xt_v, dp_buf, acc, got_in, send_in, recv_in, own_out, got_out, send_out, recv_out, small_all,
             xt_sem, dp_sems, loc_sems, d2d_send, d2d_recv, ici_send, ici_recv, sm_send, sm_recv):
        x, y, c = _mesh_pos()
        me = 4 * x + 2 * y + c
        sibling = (x, y, 1 - c)
        chips = [(1 - x, 1 - y), (1 - x, y), (x, 1 - y)]
        owners = [(*chip, cc) for chip in chips for cc in (1 - c, c)] + [sibling, (x, y, c)]
        group_of = [4 * o[0] + 2 * o[1] + o[2] for o in owners]
        dp_parts_ = (dpa, dpb, dpc)
        dp_groups = [dp_parts_[arr].at[idx] for arr, idx in _DP_OF_GROUP]

        xt_copy = pltpu.make_async_copy(xt_hbm, xt_v, xt_sem)
        xt_copy.start()

        def dp_start(step, tile, buf):
            for k in range(N_DEV):
                @pl.when(group_of[step] == k)
                def _(k=k):
                    pltpu.make_async_copy(dp_groups[k].at[pl.ds(tile * tk, tk), :], dp_buf.at[buf],
                                          dp_sems.at[buf]).start()

        def dp_wait(buf):
            pltpu.make_async_copy(dp_groups[0].at[pl.ds(0, tk), :], dp_buf.at[buf], dp_sems.at[buf]).wait()

        dp_start(0, 0, 0)

        small_all[me] = small_ref[...]
        for d in range(N_DEV):
            @pl.when(d != me)
            def _(d=d):
                pltpu.make_async_remote_copy(
                    src_ref=small_ref, dst_ref=small_all.at[me], send_sem=sm_send.at[d], recv_sem=sm_recv.at[me],
                    device_id=(d // 4, (d // 2) % 2, d % 2), device_id_type=MESH).start()

        def block_out(k):
            return gwo_ref.at[pl.ds(k * rows_out, rows_out), :]

        for k in range(N_DEV):
            s = k // 2

            @pl.when(k % 2 != c)
            def _(k=k, s=s):
                pltpu.make_async_remote_copy(
                    src_ref=block_out(k), dst_ref=got_out.at[s], send_sem=d2d_send.at[1, s],
                    recv_sem=d2d_recv.at[1, s], device_id=sibling, device_id_type=MESH).start()

            @pl.when(k % 2 == c)
            def _(k=k, s=s):
                pltpu.make_async_copy(block_out(k), own_out.at[s], loc_sems.at[s]).start()

        def ici_copy(a, j):
            send, recv = ((send_in, recv_in), (send_out, recv_out))[a]
            return pltpu.make_async_remote_copy(
                src_ref=send.at[j], dst_ref=recv.at[j], send_sem=ici_send.at[a, j], recv_sem=ici_recv.at[a, j],
                device_id=(*chips[j], c), device_id_type=MESH)

        def by_rows(n_rows, fn):
            step = min(row_chunk, n_rows)

            def rows_body(r, _):
                fn(pl.ds(pl.multiple_of(r * step, step), step))
                return 0

            lax.fori_loop(0, n_rows // step, rows_body, 0)

        for s in range(4):
            pltpu.make_async_copy(own_out.at[s], own_out.at[s], loc_sems.at[s]).wait()
            pltpu.make_async_remote_copy(
                src_ref=got_out.at[s], dst_ref=got_out.at[s], send_sem=d2d_send.at[1, s], recv_sem=d2d_recv.at[1, s],
                device_id=sibling, device_id_type=MESH).wait()
        for j, chip in enumerate(chips):
            s = 2 * chip[0] + chip[1]

            def to_send(rows, s=s, j=j):
                send_out[j, rows, :] = (own_out[s, rows, :] + got_out[s, rows, :]).astype(BF16)

            by_rows(rows_out, to_send)
            ici_copy(1, j).start()

        xt_copy.wait()

        def d2d_copy(slot, pair):
            return pltpu.make_async_remote_copy(
                src_ref=acc.at[slot], dst_ref=got_in.at[pair], send_sem=d2d_send.at[0, pair],
                recv_sem=d2d_recv.at[0, pair], device_id=sibling, device_id_type=MESH)

        for step in range(N_DEV):
            slot, pair = step % 2, step // 2
            if step % 2 == 0 and step >= 2:
                d2d_copy(slot, pair - 1).wait_send()

            def tile_body(tile, _, step=step, slot=slot):
                buf = (step * n_tiles + tile) % 2
                dp_wait(buf)

                @pl.when(tile + 1 < n_tiles)
                def _():
                    dp_start(step, tile + 1, 1 - buf)

                if step + 1 < N_DEV:
                    @pl.when(tile + 1 == n_tiles)
                    def _():
                        dp_start(step + 1, 0, 1 - buf)

                part = _dot(xt_v[tile], dp_buf[buf])

                @pl.when(tile == 0)
                def _():
                    acc[slot] = part

                @pl.when(tile > 0)
                def _():
                    acc[slot] += part

                return 0

            lax.fori_loop(0, n_tiles, tile_body, 0)

            if step % 2 == 0:
                d2d_copy(slot, pair).start()
            else:
                d2d_copy(slot, pair).wait_recv()
                if step < N_DEV - 1:
                    def to_send(rows, slot=slot, pair=pair):
                        send_in[pair, rows, :] = (acc[slot, rows, :] + got_in[pair, rows, :]).astype(BF16)

                    by_rows(d_model, to_send)
                    ici_copy(0, pair).start()

        for d in range(N_DEV):
            @pl.when(d != me)
            def _(d=d):
                pltpu.make_async_remote_copy(
                    src_ref=small_ref, dst_ref=small_all.at[d], send_sem=sm_send.at[d], recv_sem=sm_recv.at[d],
                    device_id=(d // 4, (d // 2) % 2, d % 2), device_id_type=MESH).wait()
        total = small_all[0]
        for d in range(1, N_DEV):
            total = total + small_all[d]
        small_o[...] = total

        mine = 2 * x + y
        last = (N_DEV - 1) % 2

        def finish(a, n_rows, chip_sum, outs):
            recv = (recv_in, recv_out)[a]
            for j in range(3):
                ici_copy(a, j).wait()
            g_o, d_o, nm_o, nv_o, w_r, m_r, v_r = outs

            def update(rows):
                g = chip_sum(rows)
                for j in range(3):
                    g = g + recv[j, rows, :].astype(F32)
                delta, nm, nv = _adamw(w_r[rows, :], g, m_r[rows, :], v_r[rows, :])
                g_o[rows, :] = g
                d_o[rows, :] = delta
                nm_o[rows, :] = nm
                nv_o[rows, :] = nv

            by_rows(n_rows, update)

        finish(1, rows_out, lambda rows: own_out[mine, rows, :] + got_out[mine, rows, :],
               (g_out_o, d_out_o, nm_out_o, nv_out_o, w_out_ref, m_out_ref, v_out_ref))
        finish(0, d_model, lambda rows: acc[last, rows, :] + got_in[N_DEV // 2 - 1, rows, :],
               (g_in_o, d_in_o, nm_in_o, nv_in_o, w_in_ref, m_in_ref, v_in_ref))
        d2d_copy(0, N_DEV // 2 - 1).wait_send()

    vmem = pl.BlockSpec(memory_space=pltpu.VMEM)
    hbm = pl.BlockSpec(memory_space=pl.ANY)
    shard_in = jax.ShapeDtypeStruct((d_model, cw), F32)
    shard_out = jax.ShapeDtypeStruct((rows_out, d_model), F32)
    return pl.pallas_call(
        body, name="grad_w_reduce",
        in_specs=[hbm] * 5 + [vmem] * 7,
        out_specs=(vmem,) * 9,
        out_shape=(shard_in,) * 4 + (shard_out,) * 4 + (jax.ShapeDtypeStruct(small.shape, F32),),
        scratch_shapes=[
            pltpu.VMEM((n_tiles, d_model, tk), BF16), pltpu.VMEM((2, tk, cw), BF16), pltpu.VMEM((2, d_model, cw), F32),
            pltpu.VMEM((4, d_model, cw), F32), pltpu.VMEM((3, d_model, cw), BF16), pltpu.VMEM((3, d_model, cw), BF16),
            pltpu.VMEM((4, rows_out, d_model), F32), pltpu.VMEM((4, rows_out, d_model), F32),
            pltpu.VMEM((3, rows_out, d_model), BF16), pltpu.VMEM((3, rows_out, d_model), BF16),
            pltpu.VMEM((N_DEV,) + small.shape, F32),
            pltpu.SemaphoreType.DMA, pltpu.SemaphoreType.DMA((2,)), pltpu.SemaphoreType.DMA((4,)),
            pltpu.SemaphoreType.DMA((2, 4)), pltpu.SemaphoreType.DMA((2, 4)),
            pltpu.SemaphoreType.DMA((2, 3)), pltpu.SemaphoreType.DMA((2, 3)),
            pltpu.SemaphoreType.DMA((N_DEV,)), pltpu.SemaphoreType.DMA((N_DEV,)),
        ],
        compiler_params=_params(56),
    )(xt, *dp_parts, gwo, small, w_in, m_in, v_in, w_out, m_out, v_out)


def _small_update(grads, weights, ms, vs):
    n = len(grads)

    def body(*refs):
        g_refs, w_refs, m_refs, v_refs = (refs[i * n:(i + 1) * n] for i in range(4))
        outs = refs[4 * n:]
        for i in range(n):
            delta, nm, nv = _adamw(w_refs[i][...], g_refs[i][...], m_refs[i][...], v_refs[i][...])
            outs[3 * i][...] = delta
            outs[3 * i + 1][...] = nm
            outs[3 * i + 2][...] = nv

    vmem = pl.BlockSpec(memory_space=pltpu.VMEM)
    out_shape = []
    for w in weights:
        out_shape += [jax.ShapeDtypeStruct(w.shape, F32)] * 3
    return pl.pallas_call(
        body, name="small_update", in_specs=[vmem] * (4 * n), out_specs=(vmem,) * (3 * n), out_shape=tuple(out_shape),
    )(*grads, *weights, *ms, *vs)


def _tile_sizes(seq):
    return dict(tm=min(512, seq), t_ln=min(256, seq), t_attn=min(128, seq), rc=min(256, seq), pairs=4)


def kernel(x, w_in, conv_w, w_out, ln_gain, ln_bias, loss_target, m_w_in, m_conv_w, m_w_out, m_ln_gain, m_ln_bias,
           v_w_in, v_conv_w, v_w_out, v_ln_gain, v_ln_bias):
    assert x.shape[0] == 1 and w_in.shape[0] == 1, "one sequence per device, depth 1"
    _, seq, d_model = x.shape
    cw = w_in.shape[2]
    conv_k, conv_cols = conv_w.shape[1], conv_w.shape[2]
    rows_out = w_out.shape[1]
    assert cw == d_model // 2 and cw % PAIR == 0 and conv_cols * N_DEV == cw and rows_out * N_DEV == d_model
    ts = _tile_sizes(seq)

    x2 = x.reshape(seq, d_model)
    target = loss_target.reshape(seq, d_model)
    me = 4 * lax.axis_index("x") + 2 * lax.axis_index("y") + lax.axis_index("c")

    conv_pad = jnp.pad(conv_w[0], ((0, SUBLANES - conv_k), (0, PAIR - conv_cols)))
    proj, xt, win_all, wout_all, conv_all = _gather_proj(x2, w_in[0], w_out[0], conv_pad, ts["tm"])
    w_out_full = wout_all.reshape(d_model, d_model)
    conv_full = conv_all[:, :conv_k, :conv_cols].transpose(1, 0, 2).reshape(conv_k, cw)
    conv_full = jnp.pad(conv_full, ((0, SUBLANES - conv_k), (0, 0)))

    mix_conv = _conv_fwd(proj, conv_full, ts["rc"])
    pairs = min(ts["pairs"], cw // PAIR)
    tri = _triangles(ts["t_attn"])
    o, mix_attn, tot = _attn_fwd(proj, tri, ts["t_attn"], pairs)
    dr, d_mix_conv, d_mix_attn, gwo, small = _out_ln(mix_conv, mix_attn, x2, target, ln_gain, ln_bias, w_out_full,
                                                     ts["t_ln"])
    dp_conv, d_taps = _conv_bwd(proj, conv_full, d_mix_conv, ts["rc"])
    dp_qz, dp_kv = _attn_bwd(proj, tri, o, tot, d_mix_attn, ts["t_attn"], pairs)
    grad_x = _grad_x(dr, dp_conv, dp_qz, dp_kv, win_all, ts["tm"])
    small = small.at[ROW_CONV:ROW_CONV + conv_k, :cw].set(d_taps[:conv_k])
    (g_in, d_in, nm_in, nv_in, g_out, d_out, nm_out, nv_out, small_sum) = _grad_w_reduce(
        xt, (dp_conv, dp_qz, dp_kv), gwo, small, w_in[0], m_w_in[0], v_w_in[0], w_out[0], m_w_out[0], v_w_out[0], 128)

    loss = small_sum[ROW_LOSS, 0]
    g_gain = small_sum[ROW_GAIN:ROW_GAIN + 1]
    g_bias = small_sum[ROW_BIAS:ROW_BIAS + 1]
    g_conv = lax.dynamic_slice(small_sum, (ROW_CONV, me * conv_cols), (conv_k, conv_cols))
    upd = _small_update((g_conv, g_gain, g_bias), (conv_w[0], ln_gain, ln_bias),
                        (m_conv_w[0], m_ln_gain, m_ln_bias), (v_conv_w[0], v_ln_gain, v_ln_bias))
    d_conv, nm_conv, nv_conv, d_gain, nm_gain, nv_gain, d_bias, nm_bias, nv_bias = upd

    lead = lambda a: a[None]
    return (loss, grad_x.reshape(1, seq, d_model), lead(g_in), lead(g_conv), lead(g_out), g_gain, g_bias,
            lead(d_in), lead(d_conv), lead(d_out), d_gain, d_bias,
            lead(nm_in), lead(nm_conv), lead(nm_out), nm_gain, nm_bias,
            lead(nv_in), lead(nv_conv), lead(nv_out), nv_gain, nv_bias)
```

```python
import functools

import jax
import jax.numpy as jnp
from jax import lax
from jax.experimental import pallas as pl
from jax.experimental.pallas import tpu as pltpu

F32 = jnp.float32
BF16 = jnp.bfloat16
MESH = pl.DeviceIdType.MESH

N_DEV = 8
HEAD_DIM = 64
PAIR = 128
SUBLANES = 8
LN_EPS = 1e-5
ALPHA = 2.0 ** 0.25
ADAM_LR, ADAM_B1, ADAM_B2, ADAM_EPS, ADAM_WD, ADAM_STEP = 0.001, 0.9, 0.999, 1e-08, 0.01, 10

ROW_GAIN, ROW_BIAS, ROW_CONV, ROW_LOSS = 0, 1, 2, 5

NT = (((1,), (1,)), ((), ()))
TN = (((0,), (0,)), ((), ()))


V7X_VMEM_BYTES = 64 * 1024 * 1024


def _params(vmem_mib):
    assert vmem_mib * 1024 * 1024 < V7X_VMEM_BYTES
    return pltpu.CompilerParams(vmem_limit_bytes=vmem_mib * 1024 * 1024)


def _dot(a, b, dims=None):
    if dims is None:
        return jnp.dot(a, b, preferred_element_type=F32)
    return lax.dot_general(a, b, dims, preferred_element_type=F32)


def _sigmoid(z):
    return 1.0 / (1.0 + jnp.exp(-z))


def _mesh_pos():
    return lax.axis_index("x"), lax.axis_index("y"), lax.axis_index("c")


def _adamw(w, g, m, v):
    nm = ADAM_B1 * m + (1.0 - ADAM_B1) * g
    nv = ADAM_B2 * v + (1.0 - ADAM_B2) * (g * g)
    m_hat = nm * (1.0 / (1.0 - ADAM_B1 ** ADAM_STEP))
    v_hat = nv * (1.0 / (1.0 - ADAM_B2 ** ADAM_STEP))
    delta = -ADAM_LR * (m_hat / (jnp.sqrt(v_hat) + ADAM_EPS) + ADAM_WD * w)
    return delta, nm, nv


def _gather_proj(x, w_in_s, w_out_s, conv_s, tm):
    seq, d_model = x.shape
    cw = w_in_s.shape[1]
    rows_out = w_out_s.shape[0]
    n_tiles = seq // tm
    half = d_model // 2
    SIB, X_UP, X_LOW, Y_UP, Y_LOW, VIA_Y, VIA_X, ON_X, ON_Y, ON_DIAG = range(10)

    def body(x_hbm, win_ref, wout_ref, conv_ref, proj_hbm, xt_hbm, win_all, wout_all, conv_all,
             xb, x_stage, o_stage, xt_stage, x_sems, o_sems, xt_sems, w_send, w_recv, send_sems, recv_sems):
        x, y, c = _mesh_pos()
        me = (x, y, c)
        sibling = (x, y, 1 - c)
        x_nbr, y_nbr, diag = (1 - x, y), (x, 1 - y), (1 - x, 1 - y)
        chips = [x_nbr, y_nbr, diag]
        small = (wout_all, conv_all)

        def slot(pos):
            return 4 * pos[0] + 2 * pos[1] + pos[2]

        def x_copy(tile, buf):
            return pltpu.make_async_copy(x_hbm.at[pl.ds(tile * tm, tm), :], x_stage.at[buf], x_sems.at[buf])

        x_copy(0, 0).start()
        win_all[slot(me)] = win_ref[...].astype(BF16)
        wout_all[slot(me)] = wout_ref[...].astype(BF16)
        conv_all[slot(me)] = conv_ref[...]

        def w_copy(k, block, part, to):
            ref = win_all.at[slot(block)]
            if part is not None:
                ref = ref.at[pl.ds(part * half, half), :]
            return pltpu.make_async_remote_copy(
                src_ref=ref, dst_ref=ref, send_sem=w_send.at[k], recv_sem=w_recv.at[k],
                device_id=to, device_id_type=MESH)

        def copy(a, k, block, to):
            ref = small[a].at[slot(block)]
            return pltpu.make_async_remote_copy(
                src_ref=ref, dst_ref=ref, send_sem=send_sems.at[a, k], recv_sem=recv_sems.at[a, k],
                device_id=to, device_id_type=MESH)

        sends = [w_copy(SIB, me, None, sibling),
                 w_copy(X_UP, me, 0, (*x_nbr, c)), w_copy(Y_LOW, me, 1, (*y_nbr, c)),
                 w_copy(X_LOW, me, 1, (*x_nbr, c)), w_copy(Y_UP, me, 0, (*y_nbr, c))]
        for a in range(len(small)):
            sends.append(copy(a, 0, me, sibling))
            sends += [copy(a, 1 + j, me, (*chip, c)) for j, chip in enumerate(chips)]
        for cp in sends:
            cp.start()

        def o_copy(group, tile, buf):
            return pltpu.make_async_copy(o_stage.at[buf], proj_hbm.at[group, pl.ds(tile * tm, tm), :], o_sems.at[buf])

        def xt_copy(tile, buf):
            return pltpu.make_async_copy(xt_stage.at[buf], xt_hbm.at[tile], xt_sems.at[buf])

        def project(order, group, first_pass):
            def tile_body(tile, _):
                if first_pass:
                    buf = tile % 2
                    x_copy(tile, buf).wait()

                    @pl.when(tile + 1 < n_tiles)
                    def _():
                        x_copy(tile + 1, 1 - buf).start()

                    xv = x_stage[buf]
                    xb[tile] = xv.astype(BF16)

                    @pl.when(tile >= 2)
                    def _():
                        xt_copy(tile - 2, buf).wait()

                    xt_stage[buf] = xv.T.astype(BF16)
                    xt_copy(tile, buf).start()
                count = order * n_tiles + tile
                obuf = count % 2

                @pl.when(count >= 2)
                def _():
                    o_copy(group, tile, obuf).wait()

                o_stage[obuf] = _dot(xb[tile], win_all[group]).astype(BF16)
                o_copy(group, tile, obuf).start()
                return 0

            lax.fori_loop(0, n_tiles, tile_body, 0)

        def start(cp):
            cp.start()
            sends.append(cp)

        def small_pass_on(j):
            for a in range(len(small)):
                copy(a, 1 + j, (*chips[j], c), me).wait_recv()
                start(copy(a, 4 + j, (*chips[j], c), sibling))

        def small_from_sibling(k):
            for a in range(len(small)):
                copy(a, k, sibling, me).wait_recv()

        project(0, slot(me), True)
        w_copy(SIB, sibling, None, me).wait_recv()
        small_from_sibling(0)
        project(1, slot(sibling), False)
        w_copy(X_UP, (*x_nbr, c), 0, me).wait_recv()
        start(w_copy(VIA_Y, (*x_nbr, c), 0, (*y_nbr, c)))
        w_copy(Y_LOW, (*y_nbr, c), 1, me).wait_recv()
        start(w_copy(VIA_X, (*y_nbr, c), 1, (*x_nbr, c)))
        w_copy(X_LOW, (*x_nbr, c), 1, me).wait_recv()
        start(w_copy(ON_X, (*x_nbr, c), None, sibling))
        small_pass_on(0)
        project(2, slot((*x_nbr, c)), False)
        w_copy(Y_UP, (*y_nbr, c), 0, me).wait_recv()
        start(w_copy(ON_Y, (*y_nbr, c), None, sibling))
        small_pass_on(1)
        project(3, slot((*y_nbr, c)), False)
        w_copy(ON_X, (*x_nbr, 1 - c), None, me).wait_recv()
        small_from_sibling(4)
        project(4, slot((*x_nbr, 1 - c)), False)
        w_copy(ON_Y, (*y_nbr, 1 - c), None, me).wait_recv()
        small_from_sibling(5)
        project(5, slot((*y_nbr, 1 - c)), False)
        w_copy(VIA_Y, (*diag, c), 0, me).wait_recv()
        w_copy(VIA_X, (*diag, c), 1, me).wait_recv()
        start(w_copy(ON_DIAG, (*diag, c), None, sibling))
        small_pass_on(2)
        project(6, slot((*diag, c)), False)
        w_copy(ON_DIAG, (*diag, 1 - c), None, me).wait_recv()
        small_from_sibling(6)
        project(7, slot((*diag, 1 - c)), False)

        for buf in range(2):
            o_copy(0, 0, buf).wait()
        for buf in range(min(2, n_tiles)):
            xt_copy(0, buf).wait()
        for cp in sends:
            cp.wait_send()

    vmem = pl.BlockSpec(memory_space=pltpu.VMEM)
    hbm = pl.BlockSpec(memory_space=pl.ANY)
    return pl.pallas_call(
        body, name="gather_proj",
        out_shape=(jax.ShapeDtypeStruct((N_DEV, seq, cw), BF16),
                   jax.ShapeDtypeStruct((n_tiles, d_model, tm), BF16),
                   jax.ShapeDtypeStruct((N_DEV, d_model, cw), BF16),
                   jax.ShapeDtypeStruct((N_DEV, rows_out, d_model), BF16),
                   jax.ShapeDtypeStruct((N_DEV,) + conv_s.shape, F32)),
        in_specs=[hbm, vmem, vmem, vmem], out_specs=(hbm, hbm, vmem, vmem, vmem),
        scratch_shapes=[
            pltpu.VMEM((n_tiles, tm, d_model), BF16), pltpu.VMEM((2, tm, d_model), F32),
            pltpu.VMEM((2, tm, cw), BF16), pltpu.VMEM((2, d_model, tm), BF16),
            pltpu.SemaphoreType.DMA((2,)), pltpu.SemaphoreType.DMA((2,)), pltpu.SemaphoreType.DMA((2,)),
            pltpu.SemaphoreType.DMA((10,)), pltpu.SemaphoreType.DMA((10,)),
            pltpu.SemaphoreType.DMA((2, 7)), pltpu.SemaphoreType.DMA((2, 7))],
        compiler_params=_params(48),
    )(x, w_in_s, w_out_s, conv_s)


def _conv_taps(ext, w_ref, rc):
    u0 = ext[SUBLANES:SUBLANES + rc]
    u1 = pltpu.roll(ext, 1, 0)[SUBLANES:SUBLANES + rc]
    u2 = pltpu.roll(ext, 2, 0)[SUBLANES:SUBLANES + rc]
    return w_ref[2:3, :] * u0 + w_ref[1:2, :] * u1 + w_ref[0:1, :] * u2, u0, u1, u2


def _conv_fwd(proj, conv_full, rc):
    _, seq, cw = proj.shape

    def body(b_ref, c_ref, h_ref, z_ref, w_ref, o_ref, u_scr):
        u_scr[0:SUBLANES, :] = jnp.zeros((SUBLANES, PAIR), F32)

        def fill(r, _):
            base = pl.multiple_of(r * rc, rc)
            rows = pl.ds(base, rc)
            u_scr[pl.ds(base + SUBLANES, rc), :] = c_ref[rows, :].astype(F32) * h_ref[rows, :].astype(F32)
            return 0

        lax.fori_loop(0, seq // rc, fill, 0)

        def out(r, _):
            base = pl.multiple_of(r * rc, rc)
            rows = pl.ds(base, rc)
            ext = u_scr[pl.ds(base, rc + SUBLANES), :]
            y, _, _, _ = _conv_taps(ext, w_ref, rc)
            z = z_ref[rows, :].astype(F32)
            o_ref[rows, :] = (z * _sigmoid(z) * b_ref[rows, :].astype(F32) * y).astype(BF16)
            return 0

        lax.fori_loop(0, seq // rc, out, 0)

    def chunk(j):
        return pl.BlockSpec((None, seq, PAIR), lambda cb, j=j: (j, 0, cb))

    return pl.pallas_call(
        body, name="conv_fwd", grid=(cw // PAIR,),
        in_specs=[chunk(0), chunk(1), chunk(2), chunk(3), pl.BlockSpec((SUBLANES, PAIR), lambda cb: (0, cb))],
        out_specs=pl.BlockSpec((seq, PAIR), lambda cb: (0, cb)),
        out_shape=jax.ShapeDtypeStruct((seq, cw), BF16),
        scratch_shapes=[pltpu.VMEM((seq + SUBLANES, PAIR), F32)],
    )(proj, proj, proj, proj, conv_full)


def _conv_bwd(proj, conv_full, d_mix_conv, rc):
    _, seq, cw = proj.shape

    def body(b_ref, c_ref, h_ref, z_ref, w_ref, g_ref, dp_ref, dw_ref, u_scr, dy_scr):
        u_scr[0:SUBLANES, :] = jnp.zeros((SUBLANES, PAIR), F32)
        dy_scr[seq:seq + SUBLANES, :] = jnp.zeros((SUBLANES, PAIR), F32)

        def fill(r, _):
            base = pl.multiple_of(r * rc, rc)
            rows = pl.ds(base, rc)
            u_scr[pl.ds(base + SUBLANES, rc), :] = c_ref[rows, :].astype(F32) * h_ref[rows, :].astype(F32)
            return 0

        lax.fori_loop(0, seq // rc, fill, 0)

        def gate(r, acc):
            base = pl.multiple_of(r * rc, rc)
            rows = pl.ds(base, rc)
            ext = u_scr[pl.ds(base, rc + SUBLANES), :]
            y, u0, u1, u2 = _conv_taps(ext, w_ref, rc)
            z = z_ref[rows, :].astype(F32)
            b = b_ref[rows, :].astype(F32)
            g = g_ref[rows, :].astype(F32)
            sig = _sigmoid(z)
            dp_ref[3, rows, :] = (g * b * y * (sig * (1.0 + z * (1.0 - sig)))).astype(BF16)
            gs = g * (z * sig)
            dp_ref[0, rows, :] = (gs * y).astype(BF16)
            dy = gs * b
            dy_scr[rows, :] = dy
            a0, a1, a2 = acc
            return (a0 + jnp.sum(dy * u2, axis=0, keepdims=True),
                    a1 + jnp.sum(dy * u1, axis=0, keepdims=True),
                    a2 + jnp.sum(dy * u0, axis=0, keepdims=True))

        zero = jnp.zeros((1, PAIR), F32)
        a0, a1, a2 = lax.fori_loop(0, seq // rc, gate, (zero, zero, zero))
        dw_ref[...] = jnp.zeros((SUBLANES, PAIR), F32)
        dw_ref[0:1, :] = a0
        dw_ref[1:2, :] = a1
        dw_ref[2:3, :] = a2

        def back(r, _):
            base = pl.multiple_of(r * rc, rc)
            rows = pl.ds(base, rc)
            ext = dy_scr[pl.ds(base, rc + SUBLANES), :]
            n = rc + SUBLANES
            d0 = ext[0:rc]
            d1 = pltpu.roll(ext, n - 1, 0)[0:rc]
            d2 = pltpu.roll(ext, n - 2, 0)[0:rc]
            du = w_ref[2:3, :] * d0 + w_ref[1:2, :] * d1 + w_ref[0:1, :] * d2
            dp_ref[1, rows, :] = (du * h_ref[rows, :].astype(F32)).astype(BF16)
            dp_ref[2, rows, :] = (du * c_ref[rows, :].astype(F32)).astype(BF16)
            return 0

        lax.fori_loop(0, seq // rc, back, 0)

    def chunk(j):
        return pl.BlockSpec((None, seq, PAIR), lambda cb, j=j: (j, 0, cb))

    return pl.pallas_call(
        body, name="conv_bwd", grid=(cw // PAIR,),
        in_specs=[chunk(0), chunk(1), chunk(2), chunk(3), pl.BlockSpec((SUBLANES, PAIR), lambda cb: (0, cb)),
                  pl.BlockSpec((seq, PAIR), lambda cb: (0, cb))],
        out_specs=(pl.BlockSpec((4, seq, PAIR), lambda cb: (0, 0, cb)),
                   pl.BlockSpec((SUBLANES, PAIR), lambda cb: (0, cb))),
        out_shape=(jax.ShapeDtypeStruct((4, seq, cw), BF16), jax.ShapeDtypeStruct((SUBLANES, cw), F32)),
        scratch_shapes=[pltpu.VMEM((seq + SUBLANES, PAIR), F32), pltpu.VMEM((seq + SUBLANES, PAIR), F32)],
    )(proj, proj, proj, proj, conv_full, d_mix_conv)


SKIP_CARRY = 104.0
LANE_TOT0, LANE_TOT1, LANE_FIRST = 0, 1, 2
FAST_BLOCKS = 3


def _triangles(t):
    row = lax.broadcasted_iota(jnp.int32, (2 * t, 2 * t), 0)
    col = lax.broadcasted_iota(jnp.int32, (2 * t, 2 * t), 1)
    same = (row < t) == (col < t)
    upper = jnp.logical_and(same, row > col).astype(BF16)
    lower = jnp.logical_and(same, row < col).astype(BF16)
    return jnp.stack([jnp.concatenate([upper, upper], axis=0), jnp.concatenate([lower, lower], axis=0)])


def _pair_masks(t):
    lane = lax.broadcasted_iota(jnp.int32, (t, PAIR), 1)
    qrow = lax.broadcasted_iota(jnp.int32, (t, 2 * t), 0)
    kcol = lax.broadcasted_iota(jnp.int32, (t, 2 * t), 1)
    strict = jnp.where(kcol < t, kcol, kcol - t) < qrow
    return lane, lane < HEAD_DIM, strict


def _by_head(x, head0):
    zero = jnp.zeros_like(x)
    return jnp.concatenate([jnp.where(head0, x, zero), jnp.where(head0, zero, x)], axis=0)


def _hi_lo(a):
    hi = a.astype(BF16)
    lo = (a - hi.astype(F32)).astype(BF16)
    return jnp.concatenate([hi, lo], axis=1)


def _softplus_parts(z, strict, masked):
    spu = jnp.maximum(z, 0.0) + jnp.log(1.0 + jnp.exp(-jnp.abs(z)))
    return z - spu, (jnp.where(strict, spu, 0.0) if masked else spu)


def _stacked_dot(parts, rhs):
    rows = parts[0].shape[0]
    out = _dot(jnp.concatenate(parts, axis=0), rhs)
    return [out[n * rows:(n + 1) * rows] for n in range(len(parts))]


def _attn_fwd(proj, tri, t, pp):
    _, seq, cw = proj.shape
    scale = HEAD_DIM ** -0.5
    width = pp * PAIR

    def body(q_ref, k_ref, v_ref, za_ref, tri_ref, o_ref, mix_ref, tot_ref):
        i = pl.program_id(1)
        lane, head0, strict = _pair_masks(t)
        upper = tri_ref[0]
        q = q_ref[...] * scale

        def sweep(blocks, state):
            staged = []
            for j, masked in blocks:
                start = pl.multiple_of(j * t, t)
                kb = k_ref[pl.ds(start, t), :]
                vb = v_ref[pl.ds(start, t), :]
                for p in range(pp):
                    cols = slice(p * PAIR, (p + 1) * PAIR)
                    z = _dot(q[:, cols], _by_head(kb[:, cols], head0), NT)
                    ls, sp = _softplus_parts(z, strict, masked)
                    staged.append((p, masked, ls, sp, _by_head(vb[:, cols], head0)))
            afters = _stacked_dot([_hi_lo(sp) for _, _, _, sp, _ in staged], upper)
            state = list(state)
            for (p, masked, ls, sp, v2), after in zip(staged, afters):
                (c0, c1), acc = state[p]
                x = ls - after
                w = jnp.exp(jnp.concatenate([x[:, :t] - c0, x[:, t:] - c1], axis=1))
                if masked:
                    w = jnp.where(strict, w, 0.0)
                c0 = c0 + (after[:, 0:1] + sp[:, 0:1])
                c1 = c1 + (after[:, t:t + 1] + sp[:, t:t + 1])
                state[p] = ((c0, c1), acc + _dot(w.astype(BF16), v2))
            return tuple(state)

        def unfinished(state):
            m = state[0][0][0]
            for p in range(pp):
                m = jnp.minimum(m, jnp.minimum(state[p][0][0], state[p][0][1]))
            return jnp.min(m) < SKIP_CARRY

        def step(js):
            state = sweep(((js[0], False),), js[1])
            return js[0] - 1, state, unfinished(state)

        zcol = jnp.zeros((t, 1), F32)
        init = tuple(((zcol, zcol), jnp.zeros((t, PAIR), F32)) for _ in range(pp))
        many = i >= FAST_BLOCKS - 1
        state = lax.cond(
            many,
            lambda: sweep(((i, True),) + tuple((i - b, False) for b in range(1, FAST_BLOCKS)), init),
            lambda: sweep(((i, True),), init))
        j_end, state, _ = lax.while_loop(
            lambda js: jnp.logical_and(js[0] >= 0, js[2]), step,
            (jnp.where(many, i - FAST_BLOCKS, i - 1), state, unfinished(state)))
        first = (j_end + 1).astype(F32)
        za = za_ref[...].astype(F32)
        for p in range(pp):
            (c0, c1), acc = state[p]
            cols = slice(p * PAIR, (p + 1) * PAIR)
            zp = za[:, cols]
            o_ref[:, cols] = acc.astype(BF16)
            mix_ref[:, cols] = (zp * _sigmoid(zp) * acc).astype(BF16)
            tot_ref[:, cols] = jnp.where(lane == LANE_TOT0, c0, jnp.where(lane == LANE_TOT1, c1, first))

    def tile(j):
        return pl.BlockSpec((None, t, width), lambda g, i, j=j: (j, i, g))

    def full(j):
        return pl.BlockSpec((None, seq, width), lambda g, i, j=j: (j, 0, g))

    out_tile = pl.BlockSpec((t, width), lambda g, i: (i, g))
    return pl.pallas_call(
        body, name="attn_fwd", grid=(cw // width, seq // t),
        in_specs=[tile(4), full(5), full(6), tile(7), pl.BlockSpec(tri.shape, lambda g, i: (0, 0, 0))],
        out_specs=(out_tile, out_tile, out_tile),
        out_shape=(jax.ShapeDtypeStruct((seq, cw), BF16), jax.ShapeDtypeStruct((seq, cw), BF16),
                   jax.ShapeDtypeStruct((seq, cw), F32)),
    )(proj, proj, proj, proj, tri)


def _attn_bwd(proj, tri, o, tot, d_mix_attn, t, pp):
    _, seq, cw = proj.shape
    nb = seq // t
    scale = HEAD_DIM ** -0.5
    width = pp * PAIR

    def body(q_ref, k_ref, v_ref, za_ref, tri_ref, o_ref, tot_ref, g_ref, dqz_ref, dkv_ref, dk_acc, dv_acc):
        i = pl.program_id(1)

        @pl.when(i == 0)
        def _():
            dk_acc[...] = jnp.zeros_like(dk_acc)
            dv_acc[...] = jnp.zeros_like(dv_acc)

        _, head0, strict = _pair_masks(t)
        upper, lower = tri_ref[0], tri_ref[1, 0:2 * t, :]
        za = za_ref[...].astype(F32)
        g = g_ref[...].astype(F32)
        sig = _sigmoid(za)
        dqz_ref[1] = (g * o_ref[...].astype(F32) * (sig * (1.0 + za * (1.0 - sig)))).astype(BF16)
        do = (g * (za * sig)).astype(BF16)
        q = q_ref[...] * scale
        tot_v = tot_ref[...]
        q2, do2, init = [], [], []
        zcol = jnp.zeros((t, 1), F32)
        for p in range(pp):
            cols = slice(p * PAIR, (p + 1) * PAIR)
            q2.append(_by_head(q[:, cols], head0))
            do2.append(_by_head(do[:, cols], head0))
            tp = tot_v[:, cols]
            init.append(((tp[:, LANE_TOT0:LANE_TOT0 + 1], tp[:, LANE_TOT1:LANE_TOT1 + 1]), (zcol, zcol),
                         jnp.zeros((t, PAIR), F32)))
        first = jnp.clip(tot_v[0:1, LANE_FIRST:LANE_FIRST + 1], 0.0, i.astype(F32)).astype(jnp.int32)[0, 0]

        def sweep(blocks, state):
            staged = []
            for j, masked in blocks:
                start = pl.multiple_of(j * t, t)
                kb = k_ref[pl.ds(start, t), :]
                vb = v_ref[pl.ds(start, t), :]
                for p in range(pp):
                    cols = slice(p * PAIR, (p + 1) * PAIR)
                    k2 = _by_head(kb[:, cols], head0)
                    z = _dot(q[:, cols], k2, NT)
                    ls, sp = _softplus_parts(z, strict, masked)
                    da = _dot(do[:, cols], _by_head(vb[:, cols], head0), NT)
                    staged.append((p, masked, k2, ls, sp, da))
            afters = _stacked_dot([_hi_lo(sp) for _, _, _, _, sp, _ in staged], upper)
            state = list(state)
            weights, ggs = [], []
            for (p, masked, k2, ls, sp, da), after in zip(staged, afters):
                (r0, r1), befores, dq = state[p]
                r0 = r0 - (after[:, 0:1] + sp[:, 0:1])
                r1 = r1 - (after[:, t:t + 1] + sp[:, t:t + 1])
                x = ls - after
                a = jnp.exp(jnp.concatenate([x[:, :t] - r0, x[:, t:] - r1], axis=1))
                if masked:
                    a = jnp.where(strict, a, 0.0)
                state[p] = ((r0, r1), befores, dq)
                weights.append(a.astype(BF16))
                ggs.append(a * da)
            pres = _stacked_dot([gg.astype(BF16) for gg in ggs], lower)
            dzs = []
            for (p, masked, k2, ls, sp, da), gg, pre in zip(staged, ggs, pres):
                rests, (b0, b1), dq = state[p]
                y = gg + pre
                dz = gg - jnp.exp(ls) * jnp.concatenate([y[:, :t] + b0, y[:, t:] + b1], axis=1)
                if masked:
                    dz = jnp.where(strict, dz, 0.0)
                dzb = dz.astype(BF16)
                dzs.append(dzb)
                state[p] = (rests, (b0 + y[:, t - 1:t], b1 + y[:, 2 * t - 1:2 * t]), dq + _dot(dzb, k2))
            first_row = pl.multiple_of(blocks[0][0] * t, t)
            n_rows = len(blocks) * t
            for p in range(pp):
                cols = slice(p * PAIR, (p + 1) * PAIR)

                def by_key(tiles):
                    return jnp.concatenate(
                        [jnp.concatenate([m[:, :t], m[:, t:]], axis=0).T for m in tiles], axis=0)

                mine = [n for n in range(len(staged)) if staged[n][0] == p]
                dk_acc[pl.ds(first_row, n_rows), cols] += _dot(by_key([dzs[n] for n in mine]), q2[p])
                dv_acc[pl.ds(first_row, n_rows), cols] += _dot(by_key([weights[n] for n in mine]), do2[p])
            return tuple(state)

        many = i >= FAST_BLOCKS - 1
        last_single = jnp.where(many, i - (FAST_BLOCKS - 1), i)
        state = lax.fori_loop(first, last_single, lambda j, s: sweep(((j, False),), s), tuple(init))
        state = lax.cond(
            many,
            lambda: sweep(tuple((i - b, False) for b in range(FAST_BLOCKS - 1, 0, -1)) + ((i, True),), state),
            lambda: sweep(((i, True),), state))
        for p in range(pp):
            dqz_ref[0, :, p * PAIR:(p + 1) * PAIR] = (state[p][2] * scale).astype(BF16)

        @pl.when(i == nb - 1)
        def _():
            dkv_ref[0] = dk_acc[...].astype(BF16)
            dkv_ref[1] = dv_acc[...].astype(BF16)

    def tile(j):
        return pl.BlockSpec((None, t, width), lambda g, i, j=j: (j, i, g))

    def full(j):
        return pl.BlockSpec((None, seq, width), lambda g, i, j=j: (j, 0, g))

    flat_tile = pl.BlockSpec((t, width), lambda g, i: (i, g))
    return pl.pallas_call(
        body, name="attn_bwd", grid=(cw // width, nb),
        in_specs=[tile(4), full(5), full(6), tile(7), pl.BlockSpec(tri.shape, lambda g, i: (0, 0, 0)),
                  flat_tile, flat_tile, flat_tile],
        out_specs=(pl.BlockSpec((2, t, width), lambda g, i: (0, i, g)),
                   pl.BlockSpec((2, seq, width), lambda g, i: (0, 0, g))),
        out_shape=(jax.ShapeDtypeStruct((2, seq, cw), BF16), jax.ShapeDtypeStruct((2, seq, cw), BF16)),
        scratch_shapes=[pltpu.VMEM((seq, width), F32), pltpu.VMEM((seq, width), F32)],
        compiler_params=_params(48),
    )(proj, proj, proj, proj, tri, o, tot, d_mix_attn)


def _out_ln(mix_conv, mix_attn, x, target, gain, bias, w_out, tm):
    seq, d_model = x.shape
    cw = mix_conv.shape[1]
    inv_d = 1.0 / d_model

    def body(mc_ref, ma_ref, x_ref, t_ref, gain_ref, bias_ref, w_ref, dr_ref, dmc_ref, dma_ref, gwo_ref, small_ref):
        @pl.when(pl.program_id(0) == 0)
        def _():
            gwo_ref[...] = jnp.zeros_like(gwo_ref)
            small_ref[...] = jnp.zeros_like(small_ref)

        mix = jnp.concatenate([mc_ref[...], ma_ref[...]], axis=1)
        w = w_ref[...]
        r = ALPHA * x_ref[...] + _dot(mix, w)
        mu = jnp.sum(r, axis=1, keepdims=True) * inv_d
        xc = r - mu
        var = jnp.sum(xc * xc, axis=1, keepdims=True) * inv_d
        rstd = lax.rsqrt(var + LN_EPS)
        xhat = xc * rstd
        gain_v = gain_ref[...]
        err = xhat * gain_v + bias_ref[...] - t_ref[...]
        row_loss = jnp.sum(err * err, axis=1, keepdims=True)
        loss = (0.5 * inv_d) * jnp.sum(row_loss, axis=0, keepdims=True)
        dy = err * inv_d
        small_ref[ROW_GAIN:ROW_GAIN + 1, :] += jnp.sum(dy * xhat, axis=0, keepdims=True)
        small_ref[ROW_BIAS:ROW_BIAS + 1, :] += jnp.sum(dy, axis=0, keepdims=True)
        small_ref[ROW_LOSS:ROW_LOSS + 1, :] += jnp.broadcast_to(loss, (1, d_model))
        dxhat = dy * gain_v
        m1 = jnp.sum(dxhat, axis=1, keepdims=True) * inv_d
        m2 = jnp.sum(dxhat * xhat, axis=1, keepdims=True) * inv_d
        dr = rstd * (dxhat - m1 - xhat * m2)
        dr_ref[...] = dr
        drb = dr.astype(BF16)
        dmix = _dot(drb, w, NT)
        dmc_ref[...] = dmix[:, :cw].astype(BF16)
        dma_ref[...] = dmix[:, cw:].astype(BF16)
        gwo_ref[...] += _dot(mix, drb, TN)

    def rows(width):
        return pl.BlockSpec((tm, width), lambda i: (i, 0))

    def whole(shape):
        return pl.BlockSpec(shape, lambda i: (0, 0))

    return pl.pallas_call(
        body, name="out_ln", grid=(seq // tm,),
        in_specs=[rows(cw), rows(cw), rows(d_model), rows(d_model), whole((1, d_model)), whole((1, d_model)),
                  whole((d_model, d_model))],
        out_specs=(rows(d_model), rows(cw), rows(cw), whole((d_model, d_model)), whole((SUBLANES, d_model))),
        out_shape=(jax.ShapeDtypeStruct((seq, d_model), F32), jax.ShapeDtypeStruct((seq, cw), BF16),
                   jax.ShapeDtypeStruct((seq, cw), BF16), jax.ShapeDtypeStruct((d_model, d_model), F32),
                   jax.ShapeDtypeStruct((SUBLANES, d_model), F32)),
        compiler_params=_params(48),
    )(mix_conv, mix_attn, x, target, gain, bias, w_out)


_DP_OF_GROUP = ((0, 0), (0, 1), (0, 2), (0, 3), (1, 0), (2, 0), (2, 1), (1, 1))


def _grad_w_reduce(xt, dp_parts, dr, win_all, gwo, small, rows_out, row_chunk, tm):
    n_tiles, d_model, tk = xt.shape
    nch, _, cw = win_all.shape
    seq = dr.shape[0]
    gx_tiles = seq // tm
    reuse = win_all.shape == xt.shape

    def body(xt_hbm, dpa, dpb, dpc, dr_hbm, win_hbm, gwo_ref, small_ref,
             gx_hbm, g_in_o, g_out_o, small_o,
             xt_v, dp_buf, acc, got_in, send_in, recv_in, own_out, got_out, send_out, recv_out, small_all,
             dr_buf, dpx_buf, gx_buf,
             xt_sem, dp_sems, loc_sems, d2d_send, d2d_recv, ici_send, ici_recv, sm_send, sm_recv,
             dr_sems, dpx_sems, gx_sem, *own_w_buffer):
        w_v = xt_v if reuse else own_w_buffer[0]
        x, y, c = _mesh_pos()
        me = 4 * x + 2 * y + c
        sibling = (x, y, 1 - c)
        chips = [(1 - x, 1 - y), (1 - x, y), (x, 1 - y)]
        owners = [(*chip, cc) for chip in chips for cc in (1 - c, c)] + [sibling, (x, y, c)]
        group_of = [4 * o[0] + 2 * o[1] + o[2] for o in owners]
        dp_parts_ = (dpa, dpb, dpc)
        dp_groups = [dp_parts_[arr].at[idx] for arr, idx in _DP_OF_GROUP]

        xt_copy = pltpu.make_async_copy(xt_hbm, xt_v, xt_sem)
        xt_copy.start()

        def dp_start(step, tile, buf):
            for k in range(N_DEV):
                @pl.when(group_of[step] == k)
                def _(k=k):
                    pltpu.make_async_copy(dp_groups[k].at[pl.ds(tile * tk, tk), :], dp_buf.at[buf],
                                          dp_sems.at[buf]).start()

        def dp_wait(buf):
            pltpu.make_async_copy(dp_groups[0].at[pl.ds(0, tk), :], dp_buf.at[buf], dp_sems.at[buf]).wait()

        dp_start(0, 0, 0)

        small_all[me] = small_ref[...]
        for d in range(N_DEV):
            @pl.when(d != me)
            def _(d=d):
                pltpu.make_async_remote_copy(
                    src_ref=small_ref, dst_ref=small_all.at[me], send_sem=sm_send.at[d], recv_sem=sm_recv.at[me],
                    device_id=(d // 4, (d // 2) % 2, d % 2), device_id_type=MESH).start()

        def block_out(k):
            return gwo_ref.at[pl.ds(k * rows_out, rows_out), :]

        for k in range(N_DEV):
            s = k // 2

            @pl.when(k % 2 != c)
            def _(k=k, s=s):
                pltpu.make_async_remote_copy(
                    src_ref=block_out(k), dst_ref=got_out.at[s], send_sem=d2d_send.at[1, s],
                    recv_sem=d2d_recv.at[1, s], device_id=sibling, device_id_type=MESH).start()

            @pl.when(k % 2 == c)
            def _(k=k, s=s):
                pltpu.make_async_copy(block_out(k), own_out.at[s], loc_sems.at[s]).start()

        def ici_copy(a, j):
            send, recv = ((send_in, recv_in), (send_out, recv_out))[a]
            return pltpu.make_async_remote_copy(
                src_ref=send.at[j], dst_ref=recv.at[j], send_sem=ici_send.at[a, j], recv_sem=ici_recv.at[a, j],
                device_id=(*chips[j], c), device_id_type=MESH)

        def by_rows(n_rows, fn):
            step = min(row_chunk, n_rows)

            def rows_body(r, _):
                fn(pl.ds(pl.multiple_of(r * step, step), step))
                return 0

            lax.fori_loop(0, n_rows // step, rows_body, 0)

        for s in range(4):
            pltpu.make_async_copy(own_out.at[s], own_out.at[s], loc_sems.at[s]).wait()
            pltpu.make_async_remote_copy(
                src_ref=got_out.at[s], dst_ref=got_out.at[s], send_sem=d2d_send.at[1, s], recv_sem=d2d_recv.at[1, s],
                device_id=sibling, device_id_type=MESH).wait()
        for j, chip in enumerate(chips):
            s = 2 * chip[0] + chip[1]

            def to_send(rows, s=s, j=j):
                send_out[j, rows, :] = (own_out[s, rows, :] + got_out[s, rows, :]).astype(BF16)

            by_rows(rows_out, to_send)
            ici_copy(1, j).start()

        xt_copy.wait()

        def d2d_copy(slot, pair):
            return pltpu.make_async_remote_copy(
                src_ref=acc.at[slot], dst_ref=got_in.at[pair], send_sem=d2d_send.at[0, pair],
                recv_sem=d2d_recv.at[0, pair], device_id=sibling, device_id_type=MESH)

        for step in range(N_DEV):
            slot, pair = step % 2, step // 2
            if step % 2 == 0 and step >= 2:
                d2d_copy(slot, pair - 1).wait_send()

            def tile_body(tile, _, step=step, slot=slot):
                buf = (step * n_tiles + tile) % 2
                dp_wait(buf)

                @pl.when(tile + 1 < n_tiles)
                def _():
                    dp_start(step, tile + 1, 1 - buf)

                if step + 1 < N_DEV:
                    @pl.when(tile + 1 == n_tiles)
                    def _():
                        dp_start(step + 1, 0, 1 - buf)

                part = _dot(xt_v[tile], dp_buf[buf])

                @pl.when(tile == 0)
                def _():
                    acc[slot] = part

                @pl.when(tile > 0)
                def _():
                    acc[slot] += part

                return 0

            lax.fori_loop(0, n_tiles, tile_body, 0)

            if step % 2 == 0:
                d2d_copy(slot, pair).start()
            else:
                d2d_copy(slot, pair).wait_recv()
                if step < N_DEV - 1:
                    def to_send(rows, slot=slot, pair=pair):
                        send_in[pair, rows, :] = (acc[slot, rows, :] + got_in[pair, rows, :]).astype(BF16)

                    by_rows(d_model, to_send)
                    ici_copy(0, pair).start()

        w_copy = pltpu.make_async_copy(win_hbm, w_v, xt_sem)
        w_copy.start()

        def gx_in(tile, buf):
            rows = pl.ds(tile * tm, tm)
            copies = [pltpu.make_async_copy(dr_hbm.at[rows, :], dr_buf.at[buf], dr_sems.at[buf])]
            copies += [pltpu.make_async_copy(dp_groups[k].at[rows, :], dpx_buf.at[buf, k], dpx_sems.at[buf, k])
                       for k in range(N_DEV)]
            return copies

        def gx_out(tile):
            return pltpu.make_async_copy(gx_buf, gx_hbm.at[pl.ds(tile * tm, tm), :], gx_sem)

        for cp in gx_in(0, 0):
            cp.start()
        w_copy.wait()

        def gx_body(tile, _):
            buf = tile % 2
            for cp in gx_in(tile, buf):
                cp.wait()

            @pl.when(tile + 1 < gx_tiles)
            def _():
                for cp in gx_in(tile + 1, 1 - buf):
                    cp.start()

            val = ALPHA * dr_buf[buf]
            for k in range(N_DEV):
                val = val + _dot(dpx_buf[buf, k], w_v[k], NT)

            @pl.when(tile > 0)
            def _():
                gx_out(tile - 1).wait()

            gx_buf[...] = val
            gx_out(tile).start()
            return 0

        lax.fori_loop(0, gx_tiles, gx_body, 0)

        for d in range(N_DEV):
            @pl.when(d != me)
            def _(d=d):
                pltpu.make_async_remote_copy(
                    src_ref=small_ref, dst_ref=small_all.at[d], send_sem=sm_send.at[d], recv_sem=sm_recv.at[d],
                    device_id=(d // 4, (d // 2) % 2, d % 2), device_id_type=MESH).wait()
        total = small_all[0]
        for d in range(1, N_DEV):
            total = total + small_all[d]
        small_o[...] = total

        mine = 2 * x + y
        last = (N_DEV - 1) % 2

        def finish(a, n_rows, chip_sum, g_o):
            recv = (recv_in, recv_out)[a]
            for j in range(3):
                ici_copy(a, j).wait()

            def total_rows(rows):
                g = chip_sum(rows)
                for j in range(3):
                    g = g + recv[j, rows, :].astype(F32)
                g_o[rows, :] = g

            by_rows(n_rows, total_rows)

        finish(1, rows_out, lambda rows: own_out[mine, rows, :] + got_out[mine, rows, :], g_out_o)
        finish(0, d_model, lambda rows: acc[last, rows, :] + got_in[N_DEV // 2 - 1, rows, :], g_in_o)
        d2d_copy(0, N_DEV // 2 - 1).wait_send()
        gx_out(0).wait()

    vmem = pl.BlockSpec(memory_space=pltpu.VMEM)
    hbm = pl.BlockSpec(memory_space=pl.ANY)
    return pl.pallas_call(
        body, name="grad_w_reduce",
        in_specs=[hbm] * 7 + [vmem],
        out_specs=(hbm, vmem, vmem, vmem),
        out_shape=(jax.ShapeDtypeStruct((seq, d_model), F32), jax.ShapeDtypeStruct((d_model, cw), F32),
                   jax.ShapeDtypeStruct((rows_out, d_model), F32), jax.ShapeDtypeStruct(small.shape, F32)),
        scratch_shapes=[
            pltpu.VMEM((n_tiles, d_model, tk), BF16), pltpu.VMEM((2, tk, cw), BF16), pltpu.VMEM((2, d_model, cw), F32),
            pltpu.VMEM((4, d_model, cw), F32), pltpu.VMEM((3, d_model, cw), BF16), pltpu.VMEM((3, d_model, cw), BF16),
            pltpu.VMEM((4, rows_out, d_model), F32), pltpu.VMEM((4, rows_out, d_model), F32),
            pltpu.VMEM((3, rows_out, d_model), BF16), pltpu.VMEM((3, rows_out, d_model), BF16),
            pltpu.VMEM((N_DEV,) + small.shape, F32),
            pltpu.VMEM((2, tm, d_model), F32), pltpu.VMEM((2, N_DEV, tm, cw), BF16), pltpu.VMEM((tm, d_model), F32),
            pltpu.SemaphoreType.DMA, pltpu.SemaphoreType.DMA((2,)), pltpu.SemaphoreType.DMA((4,)),
            pltpu.SemaphoreType.DMA((2, 4)), pltpu.SemaphoreType.DMA((2, 4)),
            pltpu.SemaphoreType.DMA((2, 3)), pltpu.SemaphoreType.DMA((2, 3)),
            pltpu.SemaphoreType.DMA((N_DEV,)), pltpu.SemaphoreType.DMA((N_DEV,)),
            pltpu.SemaphoreType.DMA((2,)), pltpu.SemaphoreType.DMA((2, N_DEV)), pltpu.SemaphoreType.DMA,
        ] + ([] if reuse else [pltpu.VMEM(win_all.shape, BF16)]),
        compiler_params=_params(56),
    )(xt, *dp_parts, dr, win_all, gwo, small)


def _adamw_update(g, w, m, v, rows):
    n_rows, width = w.shape
    rows = min(rows, n_rows)

    def body(g_ref, w_ref, m_ref, v_ref, d_o, nm_o, nv_o):
        d_o[...], nm_o[...], nv_o[...] = _adamw(w_ref[...], g_ref[...], m_ref[...], v_ref[...])

    tile = pl.BlockSpec((rows, width), lambda i: (i, 0))
    shape = jax.ShapeDtypeStruct(w.shape, F32)
    return pl.pallas_call(
        body, name="adamw_update", grid=(n_rows // rows,), in_specs=[tile] * 4, out_specs=(tile,) * 3,
        out_shape=(shape,) * 3,
    )(g, w, m, v)


def _small_update(grads, weights, ms, vs):
    n = len(grads)

    def body(*refs):
        g_refs, w_refs, m_refs, v_refs = (refs[i * n:(i + 1) * n] for i in range(4))
        outs = refs[4 * n:]
        for i in range(n):
            delta, nm, nv = _adamw(w_refs[i][...], g_refs[i][...], m_refs[i][...], v_refs[i][...])
            outs[3 * i][...] = delta
            outs[3 * i + 1][...] = nm
            outs[3 * i + 2][...] = nv

    vmem = pl.BlockSpec(memory_space=pltpu.VMEM)
    out_shape = []
    for w in weights:
        out_shape += [jax.ShapeDtypeStruct(w.shape, F32)] * 3
    return pl.pallas_call(
        body, name="small_update", in_specs=[vmem] * (4 * n), out_specs=(vmem,) * (3 * n), out_shape=tuple(out_shape),
    )(*grads, *weights, *ms, *vs)


def _tile_sizes(seq):
    return dict(tm=min(512, seq), t_ln=min(512, seq), t_attn=min(128, seq), rc=min(256, seq), pairs=4)


def kernel(x, w_in, conv_w, w_out, ln_gain, ln_bias, loss_target, m_w_in, m_conv_w, m_w_out, m_ln_gain, m_ln_bias,
           v_w_in, v_conv_w, v_w_out, v_ln_gain, v_ln_bias):
    assert x.shape[0] == 1 and w_in.shape[0] == 1, "one sequence per device, depth 1"
    _, seq, d_model = x.shape
    cw = w_in.shape[2]
    conv_k, conv_cols = conv_w.shape[1], conv_w.shape[2]
    rows_out = w_out.shape[1]
    assert cw == d_model // 2 and cw % PAIR == 0 and conv_cols * N_DEV == cw and rows_out * N_DEV == d_model
    ts = _tile_sizes(seq)

    x2 = x.reshape(seq, d_model)
    target = loss_target.reshape(seq, d_model)
    me = 4 * lax.axis_index("x") + 2 * lax.axis_index("y") + lax.axis_index("c")

    conv_pad = jnp.pad(conv_w[0], ((0, SUBLANES - conv_k), (0, PAIR - conv_cols)))
    proj, xt, win_all, wout_all, conv_all = _gather_proj(x2, w_in[0], w_out[0], conv_pad, ts["tm"])
    w_out_full = wout_all.reshape(d_model, d_model)
    conv_full = conv_all[:, :conv_k, :conv_cols].transpose(1, 0, 2).reshape(conv_k, cw)
    conv_full = jnp.pad(conv_full, ((0, SUBLANES - conv_k), (0, 0)))

    mix_conv = _conv_fwd(proj, conv_full, ts["rc"])
    pairs = min(ts["pairs"], cw // PAIR)
    tri = _triangles(ts["t_attn"])
    o, mix_attn, tot = _attn_fwd(proj, tri, ts["t_attn"], pairs)
    dr, d_mix_conv, d_mix_attn, gwo, small = _out_ln(mix_conv, mix_attn, x2, target, ln_gain, ln_bias, w_out_full,
                                                     ts["t_ln"])
    dp_conv, d_taps = _conv_bwd(proj, conv_full, d_mix_conv, ts["rc"])
    dp_qz, dp_kv = _attn_bwd(proj, tri, o, tot, d_mix_attn, ts["t_attn"], pairs)
    small = small.at[ROW_CONV:ROW_CONV + conv_k, :cw].set(d_taps[:conv_k])
    grad_x, g_in, g_out, small_sum = _grad_w_reduce(
        xt, (dp_conv, dp_qz, dp_kv), dr, win_all, gwo, small, rows_out, 128, ts["tm"])
    d_in, nm_in, nv_in = _adamw_update(g_in, w_in[0], m_w_in[0], v_w_in[0], 256)
    d_out, nm_out, nv_out = _adamw_update(g_out, w_out[0], m_w_out[0], v_w_out[0], 256)

    loss = small_sum[ROW_LOSS, 0]
    g_gain = small_sum[ROW_GAIN:ROW_GAIN + 1]
    g_bias = small_sum[ROW_BIAS:ROW_BIAS + 1]
    g_conv = lax.dynamic_slice(small_sum, (ROW_CONV, me * conv_cols), (conv_k, conv_cols))
    upd = _small_update((g_conv, g_gain, g_bias), (conv_w[0], ln_gain, ln_bias),
                        (m_conv_w[0], m_ln_gain, m_ln_bias), (v_conv_w[0], v_ln_gain, v_ln_bias))
    d_conv, nm_conv, nv_conv, d_gain, nm_gain, nv_gain, d_bias, nm_bias, nv_bias = upd

    lead = lambda a: a[None]
    return (loss, grad_x.reshape(1, seq, d_model), lead(g_in), lead(g_conv), lead(g_out), g_gain, g_bias,
            lead(d_in), lead(d_conv), lead(d_out), d_gain, d_bias,
            lead(nm_in), lead(nm_conv), lead(nm_out), nm_gain, nm_bias,
            lead(nv_in), lead(nv_conv), lead(nv_out), nv_gain, nv_bias)
```

```python
import functools

import jax
import jax.numpy as jnp
from jax import lax
from jax.experimental import pallas as pl
from jax.experimental.pallas import tpu as pltpu

F32 = jnp.float32
BF16 = jnp.bfloat16
MESH = pl.DeviceIdType.MESH

N_DEV = 8
HEAD_DIM = 64
PAIR = 128
SUBLANES = 8
LN_EPS = 1e-5
ALPHA = 2.0 ** 0.25
ADAM_LR, ADAM_B1, ADAM_B2, ADAM_EPS, ADAM_WD, ADAM_STEP = 0.001, 0.9, 0.999, 1e-08, 0.01, 10

ROW_GAIN, ROW_BIAS, ROW_CONV, ROW_LOSS = 0, 1, 2, 5

NT = (((1,), (1,)), ((), ()))
TN = (((0,), (0,)), ((), ()))


V7X_VMEM_BYTES = 64 * 1024 * 1024


def _params(vmem_mib):
    assert vmem_mib * 1024 * 1024 < V7X_VMEM_BYTES
    return pltpu.CompilerParams(vmem_limit_bytes=vmem_mib * 1024 * 1024)


def _dot(a, b, dims=None):
    if dims is None:
        return jnp.dot(a, b, preferred_element_type=F32)
    return lax.dot_general(a, b, dims, preferred_element_type=F32)


def _sigmoid(z):
    return 1.0 / (1.0 + jnp.exp(-z))


def _mesh_pos():
    return lax.axis_index("x"), lax.axis_index("y"), lax.axis_index("c")


def _adamw(w, g, m, v):
    nm = ADAM_B1 * m + (1.0 - ADAM_B1) * g
    nv = ADAM_B2 * v + (1.0 - ADAM_B2) * (g * g)
    m_hat = nm * (1.0 / (1.0 - ADAM_B1 ** ADAM_STEP))
    v_hat = nv * (1.0 / (1.0 - ADAM_B2 ** ADAM_STEP))
    delta = -ADAM_LR * (m_hat / (jnp.sqrt(v_hat) + ADAM_EPS) + ADAM_WD * w)
    return delta, nm, nv


def _gather_proj(x, w_in_s, w_out_s, conv_s, tm):
    seq, d_model = x.shape
    cw = w_in_s.shape[1]
    rows_out = w_out_s.shape[0]
    n_tiles = seq // tm
    half = d_model // 2
    SIB, X_UP, X_LOW, Y_UP, Y_LOW, VIA_Y, VIA_X, ON_X, ON_Y, ON_DIAG = range(10)

    def body(x_hbm, win_ref, wout_ref, conv_ref, proj_hbm, xt_hbm, win_all, wout_all, conv_all,
             xb, x_stage, o_stage, xt_stage, x_sems, o_sems, xt_sems, w_send, w_recv, send_sems, recv_sems):
        x, y, c = _mesh_pos()
        me = (x, y, c)
        sibling = (x, y, 1 - c)
        x_nbr, y_nbr, diag = (1 - x, y), (x, 1 - y), (1 - x, 1 - y)
        chips = [x_nbr, y_nbr, diag]
        small = (wout_all, conv_all)

        def slot(pos):
            return 4 * pos[0] + 2 * pos[1] + pos[2]

        def x_copy(tile, buf):
            return pltpu.make_async_copy(x_hbm.at[pl.ds(tile * tm, tm), :], x_stage.at[buf], x_sems.at[buf])

        x_copy(0, 0).start()
        win_all[slot(me)] = win_ref[...].astype(BF16)
        wout_all[slot(me)] = wout_ref[...].astype(BF16)
        conv_all[slot(me)] = conv_ref[...]

        def w_copy(k, block, part, to):
            ref = win_all.at[slot(block)]
            if part is not None:
                ref = ref.at[pl.ds(part * half, half), :]
            return pltpu.make_async_remote_copy(
                src_ref=ref, dst_ref=ref, send_sem=w_send.at[k], recv_sem=w_recv.at[k],
                device_id=to, device_id_type=MESH)

        def copy(a, k, block, to):
            ref = small[a].at[slot(block)]
            return pltpu.make_async_remote_copy(
                src_ref=ref, dst_ref=ref, send_sem=send_sems.at[a, k], recv_sem=recv_sems.at[a, k],
                device_id=to, device_id_type=MESH)

        sends = [w_copy(SIB, me, None, sibling),
                 w_copy(X_UP, me, 0, (*x_nbr, c)), w_copy(Y_LOW, me, 1, (*y_nbr, c)),
                 w_copy(X_LOW, me, 1, (*x_nbr, c)), w_copy(Y_UP, me, 0, (*y_nbr, c))]
        for a in range(len(small)):
            sends.append(copy(a, 0, me, sibling))
            sends += [copy(a, 1 + j, me, (*chip, c)) for j, chip in enumerate(chips)]
        for cp in sends:
            cp.start()

        def o_copy(group, tile, buf):
            return pltpu.make_async_copy(o_stage.at[buf], proj_hbm.at[group, pl.ds(tile * tm, tm), :], o_sems.at[buf])

        def xt_copy(tile, buf):
            return pltpu.make_async_copy(xt_stage.at[buf], xt_hbm.at[tile], xt_sems.at[buf])

        def project(order, group, first_pass):
            def tile_body(tile, _):
                if first_pass:
                    buf = tile % 2
                    x_copy(tile, buf).wait()

                    @pl.when(tile + 1 < n_tiles)
                    def _():
                        x_copy(tile + 1, 1 - buf).start()

                    xv = x_stage[buf]
                    xb[tile] = xv.astype(BF16)

                    @pl.when(tile >= 2)
                    def _():
                        xt_copy(tile - 2, buf).wait()

                    xt_stage[buf] = xv.T.astype(BF16)
                    xt_copy(tile, buf).start()
                count = order * n_tiles + tile
                obuf = count % 2

                @pl.when(count >= 2)
                def _():
                    o_copy(group, tile, obuf).wait()

                o_stage[obuf] = _dot(xb[tile], win_all[group]).astype(BF16)
                o_copy(group, tile, obuf).start()
                return 0

            lax.fori_loop(0, n_tiles, tile_body, 0)

        def start(cp):
            cp.start()
            sends.append(cp)

        def small_pass_on(j):
            for a in range(len(small)):
                copy(a, 1 + j, (*chips[j], c), me).wait_recv()
                start(copy(a, 4 + j, (*chips[j], c), sibling))

        def small_from_sibling(k):
            for a in range(len(small)):
                copy(a, k, sibling, me).wait_recv()

        project(0, slot(me), True)
        w_copy(SIB, sibling, None, me).wait_recv()
        small_from_sibling(0)
        project(1, slot(sibling), False)
        w_copy(X_UP, (*x_nbr, c), 0, me).wait_recv()
        start(w_copy(VIA_Y, (*x_nbr, c), 0, (*y_nbr, c)))
        w_copy(Y_LOW, (*y_nbr, c), 1, me).wait_recv()
        start(w_copy(VIA_X, (*y_nbr, c), 1, (*x_nbr, c)))
        w_copy(X_LOW, (*x_nbr, c), 1, me).wait_recv()
        start(w_copy(ON_X, (*x_nbr, c), None, sibling))
        small_pass_on(0)
        project(2, slot((*x_nbr, c)), False)
        w_copy(Y_UP, (*y_nbr, c), 0, me).wait_recv()
        start(w_copy(ON_Y, (*y_nbr, c), None, sibling))
        small_pass_on(1)
        project(3, slot((*y_nbr, c)), False)
        w_copy(ON_X, (*x_nbr, 1 - c), None, me).wait_recv()
        small_from_sibling(4)
        project(4, slot((*x_nbr, 1 - c)), False)
        w_copy(ON_Y, (*y_nbr, 1 - c), None, me).wait_recv()
        small_from_sibling(5)
        project(5, slot((*y_nbr, 1 - c)), False)
        w_copy(VIA_Y, (*diag, c), 0, me).wait_recv()
        w_copy(VIA_X, (*diag, c), 1, me).wait_recv()
        start(w_copy(ON_DIAG, (*diag, c), None, sibling))
        small_pass_on(2)
        project(6, slot((*diag, c)), False)
        w_copy(ON_DIAG, (*diag, 1 - c), None, me).wait_recv()
        small_from_sibling(6)
        project(7, slot((*diag, 1 - c)), False)

        for buf in range(2):
            o_copy(0, 0, buf).wait()
        for buf in range(min(2, n_tiles)):
            xt_copy(0, buf).wait()
        for cp in sends:
            cp.wait_send()

    vmem = pl.BlockSpec(memory_space=pltpu.VMEM)
    hbm = pl.BlockSpec(memory_space=pl.ANY)
    return pl.pallas_call(
        body, name="gather_proj",
        out_shape=(jax.ShapeDtypeStruct((N_DEV, seq, cw), BF16),
                   jax.ShapeDtypeStruct((n_tiles, d_model, tm), BF16),
                   jax.ShapeDtypeStruct((N_DEV, d_model, cw), BF16),
                   jax.ShapeDtypeStruct((N_DEV, rows_out, d_model), BF16),
                   jax.ShapeDtypeStruct((N_DEV,) + conv_s.shape, F32)),
        in_specs=[hbm, vmem, vmem, vmem], out_specs=(hbm, hbm, vmem, vmem, vmem),
        scratch_shapes=[
            pltpu.VMEM((n_tiles, tm, d_model), BF16), pltpu.VMEM((2, tm, d_model), F32),
            pltpu.VMEM((2, tm, cw), BF16), pltpu.VMEM((2, d_model, tm), BF16),
            pltpu.SemaphoreType.DMA((2,)), pltpu.SemaphoreType.DMA((2,)), pltpu.SemaphoreType.DMA((2,)),
            pltpu.SemaphoreType.DMA((10,)), pltpu.SemaphoreType.DMA((10,)),
            pltpu.SemaphoreType.DMA((2, 7)), pltpu.SemaphoreType.DMA((2, 7))],
        compiler_params=_params(48),
    )(x, w_in_s, w_out_s, conv_s)


def _conv_taps(ext, w_ref, rc):
    u0 = ext[SUBLANES:SUBLANES + rc]
    u1 = pltpu.roll(ext, 1, 0)[SUBLANES:SUBLANES + rc]
    u2 = pltpu.roll(ext, 2, 0)[SUBLANES:SUBLANES + rc]
    return w_ref[2:3, :] * u0 + w_ref[1:2, :] * u1 + w_ref[0:1, :] * u2, u0, u1, u2


def _conv_fwd(proj, conv_full, rc):
    _, seq, cw = proj.shape

    def body(b_ref, c_ref, h_ref, z_ref, w_ref, o_ref, u_scr):
        u_scr[0:SUBLANES, :] = jnp.zeros((SUBLANES, PAIR), F32)

        def fill(r, _):
            base = pl.multiple_of(r * rc, rc)
            rows = pl.ds(base, rc)
            u_scr[pl.ds(base + SUBLANES, rc), :] = c_ref[rows, :].astype(F32) * h_ref[rows, :].astype(F32)
            return 0

        lax.fori_loop(0, seq // rc, fill, 0)

        def out(r, _):
            base = pl.multiple_of(r * rc, rc)
            rows = pl.ds(base, rc)
            ext = u_scr[pl.ds(base, rc + SUBLANES), :]
            y, _, _, _ = _conv_taps(ext, w_ref, rc)
            z = z_ref[rows, :].astype(F32)
            o_ref[rows, :] = (z * _sigmoid(z) * b_ref[rows, :].astype(F32) * y).astype(BF16)
            return 0

        lax.fori_loop(0, seq // rc, out, 0)

    def chunk(j):
        return pl.BlockSpec((None, seq, PAIR), lambda cb, j=j: (j, 0, cb))

    return pl.pallas_call(
        body, name="conv_fwd", grid=(cw // PAIR,),
        in_specs=[chunk(0), chunk(1), chunk(2), chunk(3), pl.BlockSpec((SUBLANES, PAIR), lambda cb: (0, cb))],
        out_specs=pl.BlockSpec((seq, PAIR), lambda cb: (0, cb)),
        out_shape=jax.ShapeDtypeStruct((seq, cw), BF16),
        scratch_shapes=[pltpu.VMEM((seq + SUBLANES, PAIR), F32)],
    )(proj, proj, proj, proj, conv_full)


def _conv_bwd(proj, conv_full, d_mix_conv, rc):
    _, seq, cw = proj.shape

    def body(b_ref, c_ref, h_ref, z_ref, w_ref, g_ref, dp_ref, dw_ref, u_scr, dy_scr):
        u_scr[0:SUBLANES, :] = jnp.zeros((SUBLANES, PAIR), F32)
        dy_scr[seq:seq + SUBLANES, :] = jnp.zeros((SUBLANES, PAIR), F32)

        def fill(r, _):
            base = pl.multiple_of(r * rc, rc)
            rows = pl.ds(base, rc)
            u_scr[pl.ds(base + SUBLANES, rc), :] = c_ref[rows, :].astype(F32) * h_ref[rows, :].astype(F32)
            return 0

        lax.fori_loop(0, seq // rc, fill, 0)

        def gate(r, acc):
            base = pl.multiple_of(r * rc, rc)
            rows = pl.ds(base, rc)
            ext = u_scr[pl.ds(base, rc + SUBLANES), :]
            y, u0, u1, u2 = _conv_taps(ext, w_ref, rc)
            z = z_ref[rows, :].astype(F32)
            b = b_ref[rows, :].astype(F32)
            g = g_ref[rows, :].astype(F32)
            sig = _sigmoid(z)
            dp_ref[3, rows, :] = (g * b * y * (sig * (1.0 + z * (1.0 - sig)))).astype(BF16)
            gs = g * (z * sig)
            dp_ref[0, rows, :] = (gs * y).astype(BF16)
            dy = gs * b
            dy_scr[rows, :] = dy
            a0, a1, a2 = acc
            return (a0 + jnp.sum(dy * u2, axis=0, keepdims=True),
                    a1 + jnp.sum(dy * u1, axis=0, keepdims=True),
                    a2 + jnp.sum(dy * u0, axis=0, keepdims=True))

        zero = jnp.zeros((1, PAIR), F32)
        a0, a1, a2 = lax.fori_loop(0, seq // rc, gate, (zero, zero, zero))
        dw_ref[...] = jnp.zeros((SUBLANES, PAIR), F32)
        dw_ref[0:1, :] = a0
        dw_ref[1:2, :] = a1
        dw_ref[2:3, :] = a2

        def back(r, _):
            base = pl.multiple_of(r * rc, rc)
            rows = pl.ds(base, rc)
            ext = dy_scr[pl.ds(base, rc + SUBLANES), :]
            n = rc + SUBLANES
            d0 = ext[0:rc]
            d1 = pltpu.roll(ext, n - 1, 0)[0:rc]
            d2 = pltpu.roll(ext, n - 2, 0)[0:rc]
            du = w_ref[2:3, :] * d0 + w_ref[1:2, :] * d1 + w_ref[0:1, :] * d2
            dp_ref[1, rows, :] = (du * h_ref[rows, :].astype(F32)).astype(BF16)
            dp_ref[2, rows, :] = (du * c_ref[rows, :].astype(F32)).astype(BF16)
            return 0

        lax.fori_loop(0, seq // rc, back, 0)

    def chunk(j):
        return pl.BlockSpec((None, seq, PAIR), lambda cb, j=j: (j, 0, cb))

    return pl.pallas_call(
        body, name="conv_bwd", grid=(cw // PAIR,),
        in_specs=[chunk(0), chunk(1), chunk(2), chunk(3), pl.BlockSpec((SUBLANES, PAIR), lambda cb: (0, cb)),
                  pl.BlockSpec((seq, PAIR), lambda cb: (0, cb))],
        out_specs=(pl.BlockSpec((4, seq, PAIR), lambda cb: (0, 0, cb)),
                   pl.BlockSpec((SUBLANES, PAIR), lambda cb: (0, cb))),
        out_shape=(jax.ShapeDtypeStruct((4, seq, cw), BF16), jax.ShapeDtypeStruct((SUBLANES, cw), F32)),
        scratch_shapes=[pltpu.VMEM((seq + SUBLANES, PAIR), F32), pltpu.VMEM((seq + SUBLANES, PAIR), F32)],
    )(proj, proj, proj, proj, conv_full, d_mix_conv)


SKIP_CARRY = 104.0
LANE_TOT0, LANE_TOT1, LANE_FIRST = 0, 1, 2
FAST_BLOCKS = 3


def _triangles(t):
    row = lax.broadcasted_iota(jnp.int32, (2 * t, 2 * t), 0)
    col = lax.broadcasted_iota(jnp.int32, (2 * t, 2 * t), 1)
    same = (row < t) == (col < t)
    upper = jnp.logical_and(same, row > col).astype(BF16)
    lower = jnp.logical_and(same, row < col).astype(BF16)
    return jnp.stack([jnp.concatenate([upper, upper], axis=0), jnp.concatenate([lower, lower], axis=0)])


def _pair_masks(t):
    lane = lax.broadcasted_iota(jnp.int32, (t, PAIR), 1)
    qrow = lax.broadcasted_iota(jnp.int32, (t, 2 * t), 0)
    kcol = lax.broadcasted_iota(jnp.int32, (t, 2 * t), 1)
    strict = jnp.where(kcol < t, kcol, kcol - t) < qrow
    return lane, lane < HEAD_DIM, strict


def _by_head(x, head0):
    zero = jnp.zeros_like(x)
    return jnp.concatenate([jnp.where(head0, x, zero), jnp.where(head0, zero, x)], axis=0)


def _hi_lo(a):
    hi = a.astype(BF16)
    lo = (a - hi.astype(F32)).astype(BF16)
    return jnp.concatenate([hi, lo], axis=1)


def _softplus_parts(z, strict, masked):
    spu = jnp.maximum(z, 0.0) + jnp.log(1.0 + jnp.exp(-jnp.abs(z)))
    return z - spu, (jnp.where(strict, spu, 0.0) if masked else spu)


def _stacked_dot(parts, rhs):
    rows = parts[0].shape[0]
    out = _dot(jnp.concatenate(parts, axis=0), rhs)
    return [out[n * rows:(n + 1) * rows] for n in range(len(parts))]


def _attn_fwd(proj, tri, t, pp):
    _, seq, cw = proj.shape
    scale = HEAD_DIM ** -0.5
    width = pp * PAIR

    def body(q_ref, k_ref, v_ref, za_ref, tri_ref, o_ref, mix_ref, tot_ref):
        i = pl.program_id(1)
        lane, head0, strict = _pair_masks(t)
        upper = tri_ref[0]
        q = q_ref[...] * scale

        def sweep(blocks, state):
            staged = []
            for j, masked in blocks:
                start = pl.multiple_of(j * t, t)
                kb = k_ref[pl.ds(start, t), :]
                vb = v_ref[pl.ds(start, t), :]
                for p in range(pp):
                    cols = slice(p * PAIR, (p + 1) * PAIR)
                    z = _dot(q[:, cols], _by_head(kb[:, cols], head0), NT)
                    ls, sp = _softplus_parts(z, strict, masked)
                    staged.append((p, masked, ls, sp, _by_head(vb[:, cols], head0)))
            afters = _stacked_dot([_hi_lo(sp) for _, _, _, sp, _ in staged], upper)
            state = list(state)
            for (p, masked, ls, sp, v2), after in zip(staged, afters):
                (c0, c1), acc = state[p]
                x = ls - after
                w = jnp.exp(jnp.concatenate([x[:, :t] - c0, x[:, t:] - c1], axis=1))
                if masked:
                    w = jnp.where(strict, w, 0.0)
                c0 = c0 + (after[:, 0:1] + sp[:, 0:1])
                c1 = c1 + (after[:, t:t + 1] + sp[:, t:t + 1])
                state[p] = ((c0, c1), acc + _dot(w.astype(BF16), v2))
            return tuple(state)

        def unfinished(state):
            m = state[0][0][0]
            for p in range(pp):
                m = jnp.minimum(m, jnp.minimum(state[p][0][0], state[p][0][1]))
            return jnp.min(m) < SKIP_CARRY

        def step(js):
            state = sweep(((js[0], False),), js[1])
            return js[0] - 1, state, unfinished(state)

        zcol = jnp.zeros((t, 1), F32)
        init = tuple(((zcol, zcol), jnp.zeros((t, PAIR), F32)) for _ in range(pp))
        many = i >= FAST_BLOCKS - 1
        state = lax.cond(
            many,
            lambda: sweep(((i, True),) + tuple((i - b, False) for b in range(1, FAST_BLOCKS)), init),
            lambda: sweep(((i, True),), init))
        j_end, state, _ = lax.while_loop(
            lambda js: jnp.logical_and(js[0] >= 0, js[2]), step,
            (jnp.where(many, i - FAST_BLOCKS, i - 1), state, unfinished(state)))
        first = (j_end + 1).astype(F32)
        za = za_ref[...].astype(F32)
        for p in range(pp):
            (c0, c1), acc = state[p]
            cols = slice(p * PAIR, (p + 1) * PAIR)
            zp = za[:, cols]
            o_ref[:, cols] = acc.astype(BF16)
            mix_ref[:, cols] = (zp * _sigmoid(zp) * acc).astype(BF16)
            tot_ref[:, cols] = jnp.where(lane == LANE_TOT0, c0, jnp.where(lane == LANE_TOT1, c1, first))

    def tile(j):
        return pl.BlockSpec((None, t, width), lambda g, i, j=j: (j, i, g))

    def full(j):
        return pl.BlockSpec((None, seq, width), lambda g, i, j=j: (j, 0, g))

    out_tile = pl.BlockSpec((t, width), lambda g, i: (i, g))
    return pl.pallas_call(
        body, name="attn_fwd", grid=(cw // width, seq // t),
        in_specs=[tile(4), full(5), full(6), tile(7), pl.BlockSpec(tri.shape, lambda g, i: (0, 0, 0))],
        out_specs=(out_tile, out_tile, out_tile),
        out_shape=(jax.ShapeDtypeStruct((seq, cw), BF16), jax.ShapeDtypeStruct((seq, cw), BF16),
                   jax.ShapeDtypeStruct((seq, cw), F32)),
    )(proj, proj, proj, proj, tri)


def _attn_bwd(proj, tri, o, tot, d_mix_attn, t, pp):
    _, seq, cw = proj.shape
    nb = seq // t
    scale = HEAD_DIM ** -0.5
    width = pp * PAIR

    def body(q_ref, k_ref, v_ref, za_ref, tri_ref, o_ref, tot_ref, g_ref, dqz_ref, dkv_ref, dk_acc, dv_acc):
        i = pl.program_id(1)

        @pl.when(i == 0)
        def _():
            dk_acc[...] = jnp.zeros_like(dk_acc)
            dv_acc[...] = jnp.zeros_like(dv_acc)

        _, head0, strict = _pair_masks(t)
        upper, lower = tri_ref[0], tri_ref[1, 0:2 * t, :]
        za = za_ref[...].astype(F32)
        g = g_ref[...].astype(F32)
        sig = _sigmoid(za)
        dqz_ref[1] = (g * o_ref[...].astype(F32) * (sig * (1.0 + za * (1.0 - sig)))).astype(BF16)
        do = (g * (za * sig)).astype(BF16)
        q = q_ref[...] * scale
        tot_v = tot_ref[...]
        q2, do2, init = [], [], []
        zcol = jnp.zeros((t, 1), F32)
        for p in range(pp):
            cols = slice(p * PAIR, (p + 1) * PAIR)
            q2.append(_by_head(q[:, cols], head0))
            do2.append(_by_head(do[:, cols], head0))
            tp = tot_v[:, cols]
            init.append(((tp[:, LANE_TOT0:LANE_TOT0 + 1], tp[:, LANE_TOT1:LANE_TOT1 + 1]), (zcol, zcol),
                         jnp.zeros((t, PAIR), F32)))
        first = jnp.clip(tot_v[0:1, LANE_FIRST:LANE_FIRST + 1], 0.0, i.astype(F32)).astype(jnp.int32)[0, 0]

        def sweep(blocks, state):
            staged = []
            for j, masked in blocks:
                start = pl.multiple_of(j * t, t)
                kb = k_ref[pl.ds(start, t), :]
                vb = v_ref[pl.ds(start, t), :]
                for p in range(pp):
                    cols = slice(p * PAIR, (p + 1) * PAIR)
                    k2 = _by_head(kb[:, cols], head0)
                    z = _dot(q[:, cols], k2, NT)
                    ls, sp = _softplus_parts(z, strict, masked)
                    da = _dot(do[:, cols], _by_head(vb[:, cols], head0), NT)
                    staged.append((p, masked, k2, ls, sp, da))
            afters = _stacked_dot([_hi_lo(sp) for _, _, _, _, sp, _ in staged], upper)
            state = list(state)
            weights, ggs = [], []
            for (p, masked, k2, ls, sp, da), after in zip(staged, afters):
                (r0, r1), befores, dq = state[p]
                r0 = r0 - (after[:, 0:1] + sp[:, 0:1])
                r1 = r1 - (after[:, t:t + 1] + sp[:, t:t + 1])
                x = ls - after
                a = jnp.exp(jnp.concatenate([x[:, :t] - r0, x[:, t:] - r1], axis=1))
                if masked:
                    a = jnp.where(strict, a, 0.0)
                state[p] = ((r0, r1), befores, dq)
                weights.append(a.astype(BF16))
                ggs.append(a * da)
            pres = _stacked_dot([gg.astype(BF16) for gg in ggs], lower)
            dzs = []
            for (p, masked, k2, ls, sp, da), gg, pre in zip(staged, ggs, pres):
                rests, (b0, b1), dq = state[p]
                y = gg + pre
                dz = gg - jnp.exp(ls) * jnp.concatenate([y[:, :t] + b0, y[:, t:] + b1], axis=1)
                if masked:
                    dz = jnp.where(strict, dz, 0.0)
                dzb = dz.astype(BF16)
                dzs.append(dzb)
                state[p] = (rests, (b0 + y[:, t - 1:t], b1 + y[:, 2 * t - 1:2 * t]), dq + _dot(dzb, k2))
            first_row = pl.multiple_of(blocks[0][0] * t, t)
            n_rows = len(blocks) * t
            for p in range(pp):
                cols = slice(p * PAIR, (p + 1) * PAIR)

                def by_key(tiles):
                    return jnp.concatenate(
                        [jnp.concatenate([m[:, :t], m[:, t:]], axis=0).T for m in tiles], axis=0)

                mine = [n for n in range(len(staged)) if staged[n][0] == p]
                dk_acc[pl.ds(first_row, n_rows), cols] += _dot(by_key([dzs[n] for n in mine]), q2[p])
                dv_acc[pl.ds(first_row, n_rows), cols] += _dot(by_key([weights[n] for n in mine]), do2[p])
            return tuple(state)

        many = i >= FAST_BLOCKS - 1
        last_single = jnp.where(many, i - (FAST_BLOCKS - 1), i)
        state = lax.fori_loop(first, last_single, lambda j, s: sweep(((j, False),), s), tuple(init))
        state = lax.cond(
            many,
            lambda: sweep(tuple((i - b, False) for b in range(FAST_BLOCKS - 1, 0, -1)) + ((i, True),), state),
            lambda: sweep(((i, True),), state))
        for p in range(pp):
            dqz_ref[0, :, p * PAIR:(p + 1) * PAIR] = (state[p][2] * scale).astype(BF16)

        @pl.when(i == nb - 1)
        def _():
            dkv_ref[0] = dk_acc[...].astype(BF16)
            dkv_ref[1] = dv_acc[...].astype(BF16)

    def tile(j):
        return pl.BlockSpec((None, t, width), lambda g, i, j=j: (j, i, g))

    def full(j):
        return pl.BlockSpec((None, seq, width), lambda g, i, j=j: (j, 0, g))

    flat_tile = pl.BlockSpec((t, width), lambda g, i: (i, g))
    return pl.pallas_call(
        body, name="attn_bwd", grid=(cw // width, nb),
        in_specs=[tile(4), full(5), full(6), tile(7), pl.BlockSpec(tri.shape, lambda g, i: (0, 0, 0)),
                  flat_tile, flat_tile, flat_tile],
        out_specs=(pl.BlockSpec((2, t, width), lambda g, i: (0, i, g)),
                   pl.BlockSpec((2, seq, width), lambda g, i: (0, 0, g))),
        out_shape=(jax.ShapeDtypeStruct((2, seq, cw), BF16), jax.ShapeDtypeStruct((2, seq, cw), BF16)),
        scratch_shapes=[pltpu.VMEM((seq, width), F32), pltpu.VMEM((seq, width), F32)],
        compiler_params=_params(48),
    )(proj, proj, proj, proj, tri, o, tot, d_mix_attn)


def _out_ln(mix_conv, mix_attn, x, target, gain, bias, w_out, tm):
    seq, d_model = x.shape
    cw = mix_conv.shape[1]
    inv_d = 1.0 / d_model

    def body(mc_ref, ma_ref, x_ref, t_ref, gain_ref, bias_ref, w_ref, dr_ref, dmc_ref, dma_ref, gwo_ref, small_ref):
        @pl.when(pl.program_id(0) == 0)
        def _():
            gwo_ref[...] = jnp.zeros_like(gwo_ref)
            small_ref[...] = jnp.zeros_like(small_ref)

        mix = jnp.concatenate([mc_ref[...], ma_ref[...]], axis=1)
        w = w_ref[...]
        r = ALPHA * x_ref[...] + _dot(mix, w)
        mu = jnp.sum(r, axis=1, keepdims=True) * inv_d
        xc = r - mu
        var = jnp.sum(xc * xc, axis=1, keepdims=True) * inv_d
        rstd = lax.rsqrt(var + LN_EPS)
        xhat = xc * rstd
        gain_v = gain_ref[...]
        err = xhat * gain_v + bias_ref[...] - t_ref[...]
        row_loss = jnp.sum(err * err, axis=1, keepdims=True)
        loss = (0.5 * inv_d) * jnp.sum(row_loss, axis=0, keepdims=True)
        dy = err * inv_d
        small_ref[ROW_GAIN:ROW_GAIN + 1, :] += jnp.sum(dy * xhat, axis=0, keepdims=True)
        small_ref[ROW_BIAS:ROW_BIAS + 1, :] += jnp.sum(dy, axis=0, keepdims=True)
        small_ref[ROW_LOSS:ROW_LOSS + 1, :] += jnp.broadcast_to(loss, (1, d_model))
        dxhat = dy * gain_v
        m1 = jnp.sum(dxhat, axis=1, keepdims=True) * inv_d
        m2 = jnp.sum(dxhat * xhat, axis=1, keepdims=True) * inv_d
        dr = rstd * (dxhat - m1 - xhat * m2)
        dr_ref[...] = dr
        drb = dr.astype(BF16)
        dmix = _dot(drb, w, NT)
        dmc_ref[...] = dmix[:, :cw].astype(BF16)
        dma_ref[...] = dmix[:, cw:].astype(BF16)
        gwo_ref[...] += _dot(mix, drb, TN)

    def rows(width):
        return pl.BlockSpec((tm, width), lambda i: (i, 0))

    def whole(shape):
        return pl.BlockSpec(shape, lambda i: (0, 0))

    return pl.pallas_call(
        body, name="out_ln", grid=(seq // tm,),
        in_specs=[rows(cw), rows(cw), rows(d_model), rows(d_model), whole((1, d_model)), whole((1, d_model)),
                  whole((d_model, d_model))],
        out_specs=(rows(d_model), rows(cw), rows(cw), whole((d_model, d_model)), whole((SUBLANES, d_model))),
        out_shape=(jax.ShapeDtypeStruct((seq, d_model), F32), jax.ShapeDtypeStruct((seq, cw), BF16),
                   jax.ShapeDtypeStruct((seq, cw), BF16), jax.ShapeDtypeStruct((d_model, d_model), F32),
                   jax.ShapeDtypeStruct((SUBLANES, d_model), F32)),
        compiler_params=_params(48),
    )(mix_conv, mix_attn, x, target, gain, bias, w_out)


_DP_OF_GROUP = ((0, 0), (0, 1), (0, 2), (0, 3), (1, 0), (2, 0), (2, 1), (1, 1))


def _grad_w_reduce(xt, dp_parts, dr, win_all, gwo, small, rows_out, row_chunk, tm):
    n_tiles, d_model, tk = xt.shape
    nch, _, cw = win_all.shape
    seq = dr.shape[0]
    gx_tiles = seq // tm
    reuse = win_all.shape == xt.shape

    def body(xt_hbm, dpa, dpb, dpc, dr_hbm, win_hbm, gwo_ref, small_ref,
             gx_hbm, g_in_o, g_out_o, small_o,
             xt_v, dp_buf, acc, got_in, send_in, recv_in, own_out, got_out, send_out, recv_out, small_all,
             dr_buf, dpx_buf, gx_buf,
             xt_sem, dp_sems, loc_sems, d2d_send, d2d_recv, ici_send, ici_recv, sm_send, sm_recv,
             dr_sems, dpx_sems, gx_sem, *own_w_buffer):
        w_v = xt_v if reuse else own_w_buffer[0]
        x, y, c = _mesh_pos()
        me = 4 * x + 2 * y + c
        sibling = (x, y, 1 - c)
        chips = [(1 - x, 1 - y), (1 - x, y), (x, 1 - y)]
        owners = [(*chip, cc) for chip in chips for cc in (1 - c, c)] + [sibling, (x, y, c)]
        group_of = [4 * o[0] + 2 * o[1] + o[2] for o in owners]
        dp_parts_ = (dpa, dpb, dpc)
        dp_groups = [dp_parts_[arr].at[idx] for arr, idx in _DP_OF_GROUP]

        xt_copy = pltpu.make_async_copy(xt_hbm, xt_v, xt_sem)
        xt_copy.start()

        def dp_start(step, tile, buf):
            for k in range(N_DEV):
                @pl.when(group_of[step] == k)
                def _(k=k):
                    pltpu.make_async_copy(dp_groups[k].at[pl.ds(tile * tk, tk), :], dp_buf.at[buf],
                                          dp_sems.at[buf]).start()

        def dp_wait(buf):
            pltpu.make_async_copy(dp_groups[0].at[pl.ds(0, tk), :], dp_buf.at[buf], dp_sems.at[buf]).wait()

        dp_start(0, 0, 0)

        small_all[me] = small_ref[...]
        for d in range(N_DEV):
            @pl.when(d != me)
            def _(d=d):
                pltpu.make_async_remote_copy(
                    src_ref=small_ref, dst_ref=small_all.at[me], send_sem=sm_send.at[d], recv_sem=sm_recv.at[me],
                    device_id=(d // 4, (d // 2) % 2, d % 2), device_id_type=MESH).start()

        def block_out(k):
            return gwo_ref.at[pl.ds(k * rows_out, rows_out), :]

        for k in range(N_DEV):
            s = k // 2

            @pl.when(k % 2 != c)
            def _(k=k, s=s):
                pltpu.make_async_remote_copy(
                    src_ref=block_out(k), dst_ref=got_out.at[s], send_sem=d2d_send.at[1, s],
                    recv_sem=d2d_recv.at[1, s], device_id=sibling, device_id_type=MESH).start()

            @pl.when(k % 2 == c)
            def _(k=k, s=s):
                pltpu.make_async_copy(block_out(k), own_out.at[s], loc_sems.at[s]).start()

        VIA_X, VIA_Y, X_UP, X_LOW, Y_UP, Y_LOW = range(6)
        x_dev, y_dev = (1 - x, y, c), (x, 1 - y, c)
        ici_place = {VIA_X: (0, 0, x_dev), VIA_Y: (0, 1, y_dev), X_UP: (1, 0, x_dev), X_LOW: (1, 1, x_dev),
                     Y_UP: (2, 0, y_dev), Y_LOW: (2, 1, y_dev)}

        def ici_copy(a, k):
            send, recv = ((send_in, recv_in), (send_out, recv_out))[a]
            half = (d_model, rows_out)[a] // 2
            slot, part, to = ici_place[k]
            rows = pl.ds(part * half, half)
            return pltpu.make_async_remote_copy(
                src_ref=send.at[slot, rows, :], dst_ref=recv.at[slot, rows, :], send_sem=ici_send.at[a, k],
                recv_sem=ici_recv.at[a, k], device_id=to, device_id_type=MESH)

        def by_rows(first, n_rows, fn):
            step = min(row_chunk, n_rows)

            def rows_body(r, _):
                fn(pl.ds(pl.multiple_of(first + r * step, step), step))
                return 0

            lax.fori_loop(0, n_rows // step, rows_body, 0)

        def send_chip_sum(a, j, chip_sum):
            send, recv = ((send_in, recv_in), (send_out, recv_out))[a]
            n_rows = (d_model, rows_out)[a]
            half = n_rows // 2

            def plain(rows):
                send[j, rows, :] = chip_sum(rows).astype(BF16)

            if j == 0:
                by_rows(0, n_rows, plain)
                ici_copy(a, VIA_X).start()
                ici_copy(a, VIA_Y).start()
                return
            free, bound, passed = ((0, X_UP), (1, X_LOW), VIA_Y) if j == 1 else ((1, Y_LOW), (0, Y_UP), VIA_X)
            by_rows(free[0] * half, half, plain)
            ici_copy(a, free[1]).start()
            ici_copy(a, passed).wait_recv()

            def with_passed(rows):
                send[j, rows, :] = (chip_sum(rows) + recv[0, rows, :].astype(F32)).astype(BF16)

            by_rows(bound[0] * half, half, with_passed)
            ici_copy(a, bound[1]).start()

        for s in range(4):
            pltpu.make_async_copy(own_out.at[s], own_out.at[s], loc_sems.at[s]).wait()
            pltpu.make_async_remote_copy(
                src_ref=got_out.at[s], dst_ref=got_out.at[s], send_sem=d2d_send.at[1, s], recv_sem=d2d_recv.at[1, s],
                device_id=sibling, device_id_type=MESH).wait()
        for j, chip in enumerate(chips):
            s = 2 * chip[0] + chip[1]
            send_chip_sum(1, j, lambda rows, s=s: own_out[s, rows, :] + got_out[s, rows, :])

        xt_copy.wait()

        def d2d_copy(slot, pair):
            return pltpu.make_async_remote_copy(
                src_ref=acc.at[slot], dst_ref=got_in.at[pair], send_sem=d2d_send.at[0, pair],
                recv_sem=d2d_recv.at[0, pair], device_id=sibling, device_id_type=MESH)

        for step in range(N_DEV):
            slot, pair = step % 2, step // 2
            if step % 2 == 0 and step >= 2:
                d2d_copy(slot, pair - 1).wait_send()

            def tile_body(tile, _, step=step, slot=slot):
                buf = (step * n_tiles + tile) % 2
                dp_wait(buf)

                @pl.when(tile + 1 < n_tiles)
                def _():
                    dp_start(step, tile + 1, 1 - buf)

                if step + 1 < N_DEV:
                    @pl.when(tile + 1 == n_tiles)
                    def _():
                        dp_start(step + 1, 0, 1 - buf)

                part = _dot(xt_v[tile], dp_buf[buf])

                @pl.when(tile == 0)
                def _():
                    acc[slot] = part

                @pl.when(tile > 0)
                def _():
                    acc[slot] += part

                return 0

            lax.fori_loop(0, n_tiles, tile_body, 0)

            if step % 2 == 0:
                d2d_copy(slot, pair).start()
            else:
                d2d_copy(slot, pair).wait_recv()
                if step < N_DEV - 1:
                    send_chip_sum(0, pair, lambda rows, slot=slot, pair=pair: acc[slot, rows, :] + got_in[pair, rows, :])

        w_copy = pltpu.make_async_copy(win_hbm, w_v, xt_sem)
        w_copy.start()

        def gx_in(tile, buf):
            rows = pl.ds(tile * tm, tm)
            copies = [pltpu.make_async_copy(dr_hbm.at[rows, :], dr_buf.at[buf], dr_sems.at[buf])]
            copies += [pltpu.make_async_copy(dp_groups[k].at[rows, :], dpx_buf.at[buf, k], dpx_sems.at[buf, k])
                       for k in range(N_DEV)]
            return copies

        def gx_out(tile):
            return pltpu.make_async_copy(gx_buf, gx_hbm.at[pl.ds(tile * tm, tm), :], gx_sem)

        for cp in gx_in(0, 0):
            cp.start()
        w_copy.wait()

        def gx_body(tile, _):
            buf = tile % 2
            for cp in gx_in(tile, buf):
                cp.wait()

            @pl.when(tile + 1 < gx_tiles)
            def _():
                for cp in gx_in(tile + 1, 1 - buf):
                    cp.start()

            val = ALPHA * dr_buf[buf]
            for k in range(N_DEV):
                val = val + _dot(dpx_buf[buf, k], w_v[k], NT)

            @pl.when(tile > 0)
            def _():
                gx_out(tile - 1).wait()

            gx_buf[...] = val
            gx_out(tile).start()
            return 0

        lax.fori_loop(0, gx_tiles, gx_body, 0)

        for d in range(N_DEV):
            @pl.when(d != me)
            def _(d=d):
                pltpu.make_async_remote_copy(
                    src_ref=small_ref, dst_ref=small_all.at[d], send_sem=sm_send.at[d], recv_sem=sm_recv.at[d],
                    device_id=(d // 4, (d // 2) % 2, d % 2), device_id_type=MESH).wait()
        total = small_all[0]
        for d in range(1, N_DEV):
            total = total + small_all[d]
        small_o[...] = total

        mine = 2 * x + y
        last = (N_DEV - 1) % 2

        def finish(a, n_rows, chip_sum, g_o):
            recv = (recv_in, recv_out)[a]
            for k in (X_UP, X_LOW, Y_UP, Y_LOW):
                ici_copy(a, k).wait_recv()
            for k in ici_place:
                ici_copy(a, k).wait_send()

            def total_rows(rows):
                g_o[rows, :] = chip_sum(rows) + recv[1, rows, :].astype(F32) + recv[2, rows, :].astype(F32)

            by_rows(0, n_rows, total_rows)

        finish(1, rows_out, lambda rows: own_out[mine, rows, :] + got_out[mine, rows, :], g_out_o)
        finish(0, d_model, lambda rows: acc[last, rows, :] + got_in[N_DEV // 2 - 1, rows, :], g_in_o)
        d2d_copy(0, N_DEV // 2 - 1).wait_send()
        gx_out(0).wait()

    vmem = pl.BlockSpec(memory_space=pltpu.VMEM)
    hbm = pl.BlockSpec(memory_space=pl.ANY)
    return pl.pallas_call(
        body, name="grad_w_reduce",
        in_specs=[hbm] * 7 + [vmem],
        out_specs=(hbm, vmem, vmem, vmem),
        out_shape=(jax.ShapeDtypeStruct((seq, d_model), F32), jax.ShapeDtypeStruct((d_model, cw), F32),
                   jax.ShapeDtypeStruct((rows_out, d_model), F32), jax.ShapeDtypeStruct(small.shape, F32)),
        scratch_shapes=[
            pltpu.VMEM((n_tiles, d_model, tk), BF16), pltpu.VMEM((2, tk, cw), BF16), pltpu.VMEM((2, d_model, cw), F32),
            pltpu.VMEM((4, d_model, cw), F32), pltpu.VMEM((3, d_model, cw), BF16), pltpu.VMEM((3, d_model, cw), BF16),
            pltpu.VMEM((4, rows_out, d_model), F32), pltpu.VMEM((4, rows_out, d_model), F32),
            pltpu.VMEM((3, rows_out, d_model), BF16), pltpu.VMEM((3, rows_out, d_model), BF16),
            pltpu.VMEM((N_DEV,) + small.shape, F32),
            pltpu.VMEM((2, tm, d_model), F32), pltpu.VMEM((2, N_DEV, tm, cw), BF16), pltpu.VMEM((tm, d_model), F32),
            pltpu.SemaphoreType.DMA, pltpu.SemaphoreType.DMA((2,)), pltpu.SemaphoreType.DMA((4,)),
            pltpu.SemaphoreType.DMA((2, 4)), pltpu.SemaphoreType.DMA((2, 4)),
            pltpu.SemaphoreType.DMA((2, 6)), pltpu.SemaphoreType.DMA((2, 6)),
            pltpu.SemaphoreType.DMA((N_DEV,)), pltpu.SemaphoreType.DMA((N_DEV,)),
            pltpu.SemaphoreType.DMA((2,)), pltpu.SemaphoreType.DMA((2, N_DEV)), pltpu.SemaphoreType.DMA,
        ] + ([] if reuse else [pltpu.VMEM(win_all.shape, BF16)]),
        compiler_params=_params(56),
    )(xt, *dp_parts, dr, win_all, gwo, small)


def _adamw_update(g, w, m, v, rows):
    n_rows, width = w.shape
    rows = min(rows, n_rows)

    def body(g_ref, w_ref, m_ref, v_ref, d_o, nm_o, nv_o):
        d_o[...], nm_o[...], nv_o[...] = _adamw(w_ref[...], g_ref[...], m_ref[...], v_ref[...])

    tile = pl.BlockSpec((rows, width), lambda i: (i, 0))
    shape = jax.ShapeDtypeStruct(w.shape, F32)
    return pl.pallas_call(
        body, name="adamw_update", grid=(n_rows // rows,), in_specs=[tile] * 4, out_specs=(tile,) * 3,
        out_shape=(shape,) * 3,
    )(g, w, m, v)


def _small_update(grads, weights, ms, vs):
    n = len(grads)

    def body(*refs):
        g_refs, w_refs, m_refs, v_refs = (refs[i * n:(i + 1) * n] for i in range(4))
        outs = refs[4 * n:]
        for i in range(n):
            delta, nm, nv = _adamw(w_refs[i][...], g_refs[i][...], m_refs[i][...], v_refs[i][...])
            outs[3 * i][...] = delta
            outs[3 * i + 1][...] = nm
            outs[3 * i + 2][...] = nv

    vmem = pl.BlockSpec(memory_space=pltpu.VMEM)
    out_shape = []
    for w in weights:
        out_shape += [jax.ShapeDtypeStruct(w.shape, F32)] * 3
    return pl.pallas_call(
        body, name="small_update", in_specs=[vmem] * (4 * n), out_specs=(vmem,) * (3 * n), out_shape=tuple(out_shape),
    )(*grads, *weights, *ms, *vs)


def _tile_sizes(seq):
    return dict(tm=min(512, seq), t_ln=min(512, seq), t_attn=min(128, seq), rc=min(256, seq), pairs=4)


def kernel(x, w_in, conv_w, w_out, ln_gain, ln_bias, loss_target, m_w_in, m_conv_w, m_w_out, m_ln_gain, m_ln_bias,
           v_w_in, v_conv_w, v_w_out, v_ln_gain, v_ln_bias):
    assert x.shape[0] == 1 and w_in.shape[0] == 1, "one sequence per device, depth 1"
    _, seq, d_model = x.shape
    cw = w_in.shape[2]
    conv_k, conv_cols = conv_w.shape[1], conv_w.shape[2]
    rows_out = w_out.shape[1]
    assert cw == d_model // 2 and cw % PAIR == 0 and conv_cols * N_DEV == cw and rows_out * N_DEV == d_model
    ts = _tile_sizes(seq)

    x2 = x.reshape(seq, d_model)
    target = loss_target.reshape(seq, d_model)
    me = 4 * lax.axis_index("x") + 2 * lax.axis_index("y") + lax.axis_index("c")

    conv_pad = jnp.pad(conv_w[0], ((0, SUBLANES - conv_k), (0, PAIR - conv_cols)))
    proj, xt, win_all, wout_all, conv_all = _gather_proj(x2, w_in[0], w_out[0], conv_pad, ts["tm"])
    w_out_full = wout_all.reshape(d_model, d_model)
    conv_full = conv_all[:, :conv_k, :conv_cols].transpose(1, 0, 2).reshape(conv_k, cw)
    conv_full = jnp.pad(conv_full, ((0, SUBLANES - conv_k), (0, 0)))

    mix_conv = _conv_fwd(proj, conv_full, ts["rc"])
    pairs = min(ts["pairs"], cw // PAIR)
    tri = _triangles(ts["t_attn"])
    o, mix_attn, tot = _attn_fwd(proj, tri, ts["t_attn"], pairs)
    dr, d_mix_conv, d_mix_attn, gwo, small = _out_ln(mix_conv, mix_attn, x2, target, ln_gain, ln_bias, w_out_full,
                                                     ts["t_ln"])
    dp_conv, d_taps = _conv_bwd(proj, conv_full, d_mix_conv, ts["rc"])
    dp_qz, dp_kv = _attn_bwd(proj, tri, o, tot, d_mix_attn, ts["t_attn"], pairs)
    small = small.at[ROW_CONV:ROW_CONV + conv_k, :cw].set(d_taps[:conv_k])
    grad_x, g_in, g_out, small_sum = _grad_w_reduce(
        xt, (dp_conv, dp_qz, dp_kv), dr, win_all, gwo, small, rows_out, 128, ts["tm"])
    d_in, nm_in, nv_in = _adamw_update(g_in, w_in[0], m_w_in[0], v_w_in[0], 256)
    d_out, nm_out, nv_out = _adamw_update(g_out, w_out[0], m_w_out[0], v_w_out[0], 256)

    loss = small_sum[ROW_LOSS, 0]
    g_gain = small_sum[ROW_GAIN:ROW_GAIN + 1]
    g_bias = small_sum[ROW_BIAS:ROW_BIAS + 1]
    g_conv = lax.dynamic_slice(small_sum, (ROW_CONV, me * conv_cols), (conv_k, conv_cols))
    upd = _small_update((g_conv, g_gain, g_bias), (conv_w[0], ln_gain, ln_bias),
                        (m_conv_w[0], m_ln_gain, m_ln_bias), (v_conv_w[0], v_ln_gain, v_ln_bias))
    d_conv, nm_conv, nv_conv, d_gain, nm_gain, nv_gain, d_bias, nm_bias, nv_bias = upd

    lead = lambda a: a[None]
    return (loss, grad_x.reshape(1, seq, d_model), lead(g_in), lead(g_conv), lead(g_out), g_gain, g_bias,
            lead(d_in), lead(d_conv), lead(d_out), d_gain, d_bias,
            lead(nm_in), lead(nm_conv), lead(nm_out), nm_gain, nm_bias,
            lead(nv_in), lead(nv_conv), lead(nv_out), nv_gain, nv_bias)
```

```python
import functools

import jax
import jax.numpy as jnp
from jax import lax
from jax.experimental import pallas as pl
from jax.experimental.pallas import tpu as pltpu

F32 = jnp.float32
BF16 = jnp.bfloat16
MESH = pl.DeviceIdType.MESH

N_DEV = 8
HEAD_DIM = 64
PAIR = 128
SUBLANES = 8
LN_EPS = 1e-5
ALPHA = 2.0 ** 0.25
ADAM_LR, ADAM_B1, ADAM_B2, ADAM_EPS, ADAM_WD, ADAM_STEP = 0.001, 0.9, 0.999, 1e-08, 0.01, 10

ROW_GAIN, ROW_BIAS, ROW_CONV, ROW_LOSS = 0, 1, 2, 5

NT = (((1,), (1,)), ((), ()))
TN = (((0,), (0,)), ((), ()))


V7X_VMEM_BYTES = 64 * 1024 * 1024


def _params(vmem_mib):
    assert vmem_mib * 1024 * 1024 < V7X_VMEM_BYTES
    return pltpu.CompilerParams(vmem_limit_bytes=vmem_mib * 1024 * 1024)


def _dot(a, b, dims=None):
    if dims is None:
        return jnp.dot(a, b, preferred_element_type=F32)
    return lax.dot_general(a, b, dims, preferred_element_type=F32)


def _sigmoid(z):
    return 1.0 / (1.0 + jnp.exp(-z))


def _mesh_pos():
    return lax.axis_index("x"), lax.axis_index("y"), lax.axis_index("c")


def _adamw(w, g, m, v):
    nm = ADAM_B1 * m + (1.0 - ADAM_B1) * g
    nv = ADAM_B2 * v + (1.0 - ADAM_B2) * (g * g)
    m_hat = nm * (1.0 / (1.0 - ADAM_B1 ** ADAM_STEP))
    v_hat = nv * (1.0 / (1.0 - ADAM_B2 ** ADAM_STEP))
    delta = -ADAM_LR * (m_hat / (jnp.sqrt(v_hat) + ADAM_EPS) + ADAM_WD * w)
    return delta, nm, nv


def _gather_proj(x, w_in_s, w_out_s, conv_s, tm):
    seq, d_model = x.shape
    cw = w_in_s.shape[1]
    rows_out = w_out_s.shape[0]
    n_tiles = seq // tm
    half = d_model // 2
    SIB, X_UP, X_LOW, Y_UP, Y_LOW, VIA_Y, VIA_X, ON_X, ON_Y, ON_DIAG = range(10)

    def body(x_hbm, win_ref, wout_ref, conv_ref, proj_hbm, xt_hbm, win_all, wout_all, conv_all,
             xb, x_stage, o_stage, xt_stage, x_sems, o_sems, xt_sems, w_send, w_recv, send_sems, recv_sems):
        x, y, c = _mesh_pos()
        me = (x, y, c)
        sibling = (x, y, 1 - c)
        x_nbr, y_nbr, diag = (1 - x, y), (x, 1 - y), (1 - x, 1 - y)
        chips = [x_nbr, y_nbr, diag]
        small = (wout_all, conv_all)

        def slot(pos):
            return 4 * pos[0] + 2 * pos[1] + pos[2]

        def x_copy(tile, buf):
            return pltpu.make_async_copy(x_hbm.at[pl.ds(tile * tm, tm), :], x_stage.at[buf], x_sems.at[buf])

        x_copy(0, 0).start()
        win_all[slot(me)] = win_ref[...].astype(BF16)
        wout_all[slot(me)] = wout_ref[...].astype(BF16)
        conv_all[slot(me)] = conv_ref[...]

        def w_copy(k, block, part, to):
            ref = win_all.at[slot(block)]
            if part is not None:
                ref = ref.at[pl.ds(part * half, half), :]
            return pltpu.make_async_remote_copy(
                src_ref=ref, dst_ref=ref, send_sem=w_send.at[k], recv_sem=w_recv.at[k],
                device_id=to, device_id_type=MESH)

        def copy(a, k, block, to):
            ref = small[a].at[slot(block)]
            return pltpu.make_async_remote_copy(
                src_ref=ref, dst_ref=ref, send_sem=send_sems.at[a, k], recv_sem=recv_sems.at[a, k],
                device_id=to, device_id_type=MESH)

        sends = [w_copy(SIB, me, None, sibling),
                 w_copy(X_UP, me, 0, (*x_nbr, c)), w_copy(Y_LOW, me, 1, (*y_nbr, c)),
                 w_copy(X_LOW, me, 1, (*x_nbr, c)), w_copy(Y_UP, me, 0, (*y_nbr, c))]
        for a in range(len(small)):
            sends.append(copy(a, 0, me, sibling))
            sends += [copy(a, 1 + j, me, (*chip, c)) for j, chip in enumerate(chips)]
        for cp in sends:
            cp.start()

        def o_copy(group, tile, buf):
            return pltpu.make_async_copy(o_stage.at[buf], proj_hbm.at[group, pl.ds(tile * tm, tm), :], o_sems.at[buf])

        def xt_copy(tile, buf):
            return pltpu.make_async_copy(xt_stage.at[buf], xt_hbm.at[:, pl.ds(tile * tm, tm)], xt_sems.at[buf])

        def project(order, group, first_pass):
            def tile_body(tile, _):
                if first_pass:
                    buf = tile % 2
                    x_copy(tile, buf).wait()

                    @pl.when(tile + 1 < n_tiles)
                    def _():
                        x_copy(tile + 1, 1 - buf).start()

                    xv = x_stage[buf]
                    xb[tile] = xv.astype(BF16)

                    @pl.when(tile >= 2)
                    def _():
                        xt_copy(tile - 2, buf).wait()

                    xt_stage[buf] = xv.T.astype(BF16)
                    xt_copy(tile, buf).start()
                count = order * n_tiles + tile
                obuf = count % 2

                @pl.when(count >= 2)
                def _():
                    o_copy(group, tile, obuf).wait()

                o_stage[obuf] = _dot(xb[tile], win_all[group]).astype(BF16)
                o_copy(group, tile, obuf).start()
                return 0

            lax.fori_loop(0, n_tiles, tile_body, 0)

        def start(cp):
            cp.start()
            sends.append(cp)

        def small_pass_on(j):
            for a in range(len(small)):
                copy(a, 1 + j, (*chips[j], c), me).wait_recv()
                start(copy(a, 4 + j, (*chips[j], c), sibling))

        def small_from_sibling(k):
            for a in range(len(small)):
                copy(a, k, sibling, me).wait_recv()

        project(0, slot(me), True)
        w_copy(SIB, sibling, None, me).wait_recv()
        small_from_sibling(0)
        project(1, slot(sibling), False)
        w_copy(X_UP, (*x_nbr, c), 0, me).wait_recv()
        start(w_copy(VIA_Y, (*x_nbr, c), 0, (*y_nbr, c)))
        w_copy(Y_LOW, (*y_nbr, c), 1, me).wait_recv()
        start(w_copy(VIA_X, (*y_nbr, c), 1, (*x_nbr, c)))
        w_copy(X_LOW, (*x_nbr, c), 1, me).wait_recv()
        start(w_copy(ON_X, (*x_nbr, c), None, sibling))
        small_pass_on(0)
        project(2, slot((*x_nbr, c)), False)
        w_copy(Y_UP, (*y_nbr, c), 0, me).wait_recv()
        start(w_copy(ON_Y, (*y_nbr, c), None, sibling))
        small_pass_on(1)
        project(3, slot((*y_nbr, c)), False)
        w_copy(ON_X, (*x_nbr, 1 - c), None, me).wait_recv()
        small_from_sibling(4)
        project(4, slot((*x_nbr, 1 - c)), False)
        w_copy(ON_Y, (*y_nbr, 1 - c), None, me).wait_recv()
        small_from_sibling(5)
        project(5, slot((*y_nbr, 1 - c)), False)
        w_copy(VIA_Y, (*diag, c), 0, me).wait_recv()
        w_copy(VIA_X, (*diag, c), 1, me).wait_recv()
        start(w_copy(ON_DIAG, (*diag, c), None, sibling))
        small_pass_on(2)
        project(6, slot((*diag, c)), False)
        w_copy(ON_DIAG, (*diag, 1 - c), None, me).wait_recv()
        small_from_sibling(6)
        project(7, slot((*diag, 1 - c)), False)

        for buf in range(2):
            o_copy(0, 0, buf).wait()
        for buf in range(min(2, n_tiles)):
            xt_copy(0, buf).wait()
        for cp in sends:
            cp.wait_send()

    vmem = pl.BlockSpec(memory_space=pltpu.VMEM)
    hbm = pl.BlockSpec(memory_space=pl.ANY)
    return pl.pallas_call(
        body, name="gather_proj",
        out_shape=(jax.ShapeDtypeStruct((N_DEV, seq, cw), BF16),
                   jax.ShapeDtypeStruct((d_model, seq), BF16),
                   jax.ShapeDtypeStruct((N_DEV, d_model, cw), BF16),
                   jax.ShapeDtypeStruct((N_DEV, rows_out, d_model), BF16),
                   jax.ShapeDtypeStruct((N_DEV,) + conv_s.shape, F32)),
        in_specs=[hbm, vmem, vmem, vmem], out_specs=(hbm, hbm, vmem, vmem, vmem),
        scratch_shapes=[
            pltpu.VMEM((n_tiles, tm, d_model), BF16), pltpu.VMEM((2, tm, d_model), F32),
            pltpu.VMEM((2, tm, cw), BF16), pltpu.VMEM((2, d_model, tm), BF16),
            pltpu.SemaphoreType.DMA((2,)), pltpu.SemaphoreType.DMA((2,)), pltpu.SemaphoreType.DMA((2,)),
            pltpu.SemaphoreType.DMA((10,)), pltpu.SemaphoreType.DMA((10,)),
            pltpu.SemaphoreType.DMA((2, 7)), pltpu.SemaphoreType.DMA((2, 7))],
        compiler_params=_params(48),
    )(x, w_in_s, w_out_s, conv_s)


def _conv_taps(ext, w_ref, rc):
    u0 = ext[SUBLANES:SUBLANES + rc]
    u1 = pltpu.roll(ext, 1, 0)[SUBLANES:SUBLANES + rc]
    u2 = pltpu.roll(ext, 2, 0)[SUBLANES:SUBLANES + rc]
    return w_ref[2:3, :] * u0 + w_ref[1:2, :] * u1 + w_ref[0:1, :] * u2, u0, u1, u2


def _conv_fwd(proj, conv_full, rc):
    _, seq, cw = proj.shape

    def body(b_ref, c_ref, h_ref, z_ref, w_ref, o_ref, u_scr):
        u_scr[0:SUBLANES, :] = jnp.zeros((SUBLANES, PAIR), F32)

        def fill(r, _):
            base = pl.multiple_of(r * rc, rc)
            rows = pl.ds(base, rc)
            u_scr[pl.ds(base + SUBLANES, rc), :] = c_ref[rows, :].astype(F32) * h_ref[rows, :].astype(F32)
            return 0

        lax.fori_loop(0, seq // rc, fill, 0)

        def out(r, _):
            base = pl.multiple_of(r * rc, rc)
            rows = pl.ds(base, rc)
            ext = u_scr[pl.ds(base, rc + SUBLANES), :]
            y, _, _, _ = _conv_taps(ext, w_ref, rc)
            z = z_ref[rows, :].astype(F32)
            o_ref[rows, :] = (z * _sigmoid(z) * b_ref[rows, :].astype(F32) * y).astype(BF16)
            return 0

        lax.fori_loop(0, seq // rc, out, 0)

    def chunk(j):
        return pl.BlockSpec((None, seq, PAIR), lambda cb, j=j: (j, 0, cb))

    return pl.pallas_call(
        body, name="conv_fwd", grid=(cw // PAIR,),
        in_specs=[chunk(0), chunk(1), chunk(2), chunk(3), pl.BlockSpec((SUBLANES, PAIR), lambda cb: (0, cb))],
        out_specs=pl.BlockSpec((seq, PAIR), lambda cb: (0, cb)),
        out_shape=jax.ShapeDtypeStruct((seq, cw), BF16),
        scratch_shapes=[pltpu.VMEM((seq + SUBLANES, PAIR), F32)],
    )(proj, proj, proj, proj, conv_full)


def _conv_bwd(proj, conv_full, d_mix_conv, rc):
    _, seq, cw = proj.shape

    def body(b_ref, c_ref, h_ref, z_ref, w_ref, g_ref, dp_ref, dw_ref, u_scr, dy_scr):
        u_scr[0:SUBLANES, :] = jnp.zeros((SUBLANES, PAIR), F32)
        dy_scr[seq:seq + SUBLANES, :] = jnp.zeros((SUBLANES, PAIR), F32)

        def fill(r, _):
            base = pl.multiple_of(r * rc, rc)
            rows = pl.ds(base, rc)
            u_scr[pl.ds(base + SUBLANES, rc), :] = c_ref[rows, :].astype(F32) * h_ref[rows, :].astype(F32)
            return 0

        lax.fori_loop(0, seq // rc, fill, 0)

        def gate(r, acc):
            base = pl.multiple_of(r * rc, rc)
            rows = pl.ds(base, rc)
            ext = u_scr[pl.ds(base, rc + SUBLANES), :]
            y, u0, u1, u2 = _conv_taps(ext, w_ref, rc)
            z = z_ref[rows, :].astype(F32)
            b = b_ref[rows, :].astype(F32)
            g = g_ref[rows, :].astype(F32)
            sig = _sigmoid(z)
            dp_ref[3, rows, :] = (g * b * y * (sig * (1.0 + z * (1.0 - sig)))).astype(BF16)
            gs = g * (z * sig)
            dp_ref[0, rows, :] = (gs * y).astype(BF16)
            dy = gs * b
            dy_scr[rows, :] = dy
            a0, a1, a2 = acc
            return (a0 + jnp.sum(dy * u2, axis=0, keepdims=True),
                    a1 + jnp.sum(dy * u1, axis=0, keepdims=True),
                    a2 + jnp.sum(dy * u0, axis=0, keepdims=True))

        zero = jnp.zeros((1, PAIR), F32)
        a0, a1, a2 = lax.fori_loop(0, seq // rc, gate, (zero, zero, zero))
        dw_ref[...] = jnp.zeros((SUBLANES, PAIR), F32)
        dw_ref[0:1, :] = a0
        dw_ref[1:2, :] = a1
        dw_ref[2:3, :] = a2

        def back(r, _):
            base = pl.multiple_of(r * rc, rc)
            rows = pl.ds(base, rc)
            ext = dy_scr[pl.ds(base, rc + SUBLANES), :]
            n = rc + SUBLANES
            d0 = ext[0:rc]
            d1 = pltpu.roll(ext, n - 1, 0)[0:rc]
            d2 = pltpu.roll(ext, n - 2, 0)[0:rc]
            du = w_ref[2:3, :] * d0 + w_ref[1:2, :] * d1 + w_ref[0:1, :] * d2
            dp_ref[1, rows, :] = (du * h_ref[rows, :].astype(F32)).astype(BF16)
            dp_ref[2, rows, :] = (du * c_ref[rows, :].astype(F32)).astype(BF16)
            return 0

        lax.fori_loop(0, seq // rc, back, 0)

    def chunk(j):
        return pl.BlockSpec((None, seq, PAIR), lambda cb, j=j: (j, 0, cb))

    return pl.pallas_call(
        body, name="conv_bwd", grid=(cw // PAIR,),
        in_specs=[chunk(0), chunk(1), chunk(2), chunk(3), pl.BlockSpec((SUBLANES, PAIR), lambda cb: (0, cb)),
                  pl.BlockSpec((seq, PAIR), lambda cb: (0, cb))],
        out_specs=(pl.BlockSpec((4, seq, PAIR), lambda cb: (0, 0, cb)),
                   pl.BlockSpec((SUBLANES, PAIR), lambda cb: (0, cb))),
        out_shape=(jax.ShapeDtypeStruct((4, seq, cw), BF16), jax.ShapeDtypeStruct((SUBLANES, cw), F32)),
        scratch_shapes=[pltpu.VMEM((seq + SUBLANES, PAIR), F32), pltpu.VMEM((seq + SUBLANES, PAIR), F32)],
    )(proj, proj, proj, proj, conv_full, d_mix_conv)


SKIP_CARRY = 104.0
LANE_TOT0, LANE_TOT1, LANE_FIRST = 0, 1, 2
FAST_BLOCKS = 3


def _triangles(t):
    row = lax.broadcasted_iota(jnp.int32, (2 * t, 2 * t), 0)
    col = lax.broadcasted_iota(jnp.int32, (2 * t, 2 * t), 1)
    same = (row < t) == (col < t)
    upper = jnp.logical_and(same, row > col).astype(BF16)
    lower = jnp.logical_and(same, row < col).astype(BF16)
    return jnp.stack([jnp.concatenate([upper, upper], axis=0), jnp.concatenate([lower, lower], axis=0)])


def _pair_masks(t):
    lane = lax.broadcasted_iota(jnp.int32, (t, PAIR), 1)
    qrow = lax.broadcasted_iota(jnp.int32, (t, 2 * t), 0)
    kcol = lax.broadcasted_iota(jnp.int32, (t, 2 * t), 1)
    strict = jnp.where(kcol < t, kcol, kcol - t) < qrow
    return lane, lane < HEAD_DIM, strict


def _by_head(x, head0):
    zero = jnp.zeros_like(x)
    return jnp.concatenate([jnp.where(head0, x, zero), jnp.where(head0, zero, x)], axis=0)


def _hi_lo(a):
    hi = a.astype(BF16)
    lo = (a - hi.astype(F32)).astype(BF16)
    return jnp.concatenate([hi, lo], axis=1)


def _softplus_parts(z, strict, masked):
    spu = jnp.maximum(z, 0.0) + jnp.log(1.0 + jnp.exp(-jnp.abs(z)))
    return z - spu, (jnp.where(strict, spu, 0.0) if masked else spu)


def _stacked_dot(parts, rhs):
    rows = parts[0].shape[0]
    out = _dot(jnp.concatenate(parts, axis=0), rhs)
    return [out[n * rows:(n + 1) * rows] for n in range(len(parts))]


def _attn_fwd(proj, tri, t, pp):
    _, seq, cw = proj.shape
    scale = HEAD_DIM ** -0.5
    width = pp * PAIR

    def body(q_ref, k_ref, v_ref, za_ref, tri_ref, o_ref, mix_ref, tot_ref):
        i = pl.program_id(1)
        lane, head0, strict = _pair_masks(t)
        upper = tri_ref[0]
        q = q_ref[...] * scale

        def sweep(blocks, state):
            staged = []
            for j, masked in blocks:
                start = pl.multiple_of(j * t, t)
                kb = k_ref[pl.ds(start, t), :]
                vb = v_ref[pl.ds(start, t), :]
                for p in range(pp):
                    cols = slice(p * PAIR, (p + 1) * PAIR)
                    z = _dot(q[:, cols], _by_head(kb[:, cols], head0), NT)
                    ls, sp = _softplus_parts(z, strict, masked)
                    staged.append((p, masked, ls, sp, _by_head(vb[:, cols], head0)))
            afters = _stacked_dot([_hi_lo(sp) for _, _, _, sp, _ in staged], upper)
            state = list(state)
            for (p, masked, ls, sp, v2), after in zip(staged, afters):
                (c0, c1), acc = state[p]
                x = ls - after
                w = jnp.exp(jnp.concatenate([x[:, :t] - c0, x[:, t:] - c1], axis=1))
                if masked:
                    w = jnp.where(strict, w, 0.0)
                c0 = c0 + (after[:, 0:1] + sp[:, 0:1])
                c1 = c1 + (after[:, t:t + 1] + sp[:, t:t + 1])
                state[p] = ((c0, c1), acc + _dot(w.astype(BF16), v2))
            return tuple(state)

        def unfinished(state):
            m = state[0][0][0]
            for p in range(pp):
                m = jnp.minimum(m, jnp.minimum(state[p][0][0], state[p][0][1]))
            return jnp.min(m) < SKIP_CARRY

        def step(js):
            state = sweep(((js[0], False),), js[1])
            return js[0] - 1, state, unfinished(state)

        zcol = jnp.zeros((t, 1), F32)
        init = tuple(((zcol, zcol), jnp.zeros((t, PAIR), F32)) for _ in range(pp))
        many = i >= FAST_BLOCKS - 1
        state = lax.cond(
            many,
            lambda: sweep(((i, True),) + tuple((i - b, False) for b in range(1, FAST_BLOCKS)), init),
            lambda: sweep(((i, True),), init))
        j_end, state, _ = lax.while_loop(
            lambda js: jnp.logical_and(js[0] >= 0, js[2]), step,
            (jnp.where(many, i - FAST_BLOCKS, i - 1), state, unfinished(state)))
        first = (j_end + 1).astype(F32)
        za = za_ref[...].astype(F32)
        for p in range(pp):
            (c0, c1), acc = state[p]
            cols = slice(p * PAIR, (p + 1) * PAIR)
            zp = za[:, cols]
            o_ref[:, cols] = acc.astype(BF16)
            mix_ref[:, cols] = (zp * _sigmoid(zp) * acc).astype(BF16)
            tot_ref[:, cols] = jnp.where(lane == LANE_TOT0, c0, jnp.where(lane == LANE_TOT1, c1, first))

    def tile(j):
        return pl.BlockSpec((None, t, width), lambda g, i, j=j: (j, i, g))

    def full(j):
        return pl.BlockSpec((None, seq, width), lambda g, i, j=j: (j, 0, g))

    out_tile = pl.BlockSpec((t, width), lambda g, i: (i, g))
    return pl.pallas_call(
        body, name="attn_fwd", grid=(cw // width, seq // t),
        in_specs=[tile(4), full(5), full(6), tile(7), pl.BlockSpec(tri.shape, lambda g, i: (0, 0, 0))],
        out_specs=(out_tile, out_tile, out_tile),
        out_shape=(jax.ShapeDtypeStruct((seq, cw), BF16), jax.ShapeDtypeStruct((seq, cw), BF16),
                   jax.ShapeDtypeStruct((seq, cw), F32)),
    )(proj, proj, proj, proj, tri)


def _attn_bwd(proj, tri, o, tot, d_mix_attn, t, pp):
    _, seq, cw = proj.shape
    nb = seq // t
    scale = HEAD_DIM ** -0.5
    width = pp * PAIR

    def body(q_ref, k_ref, v_ref, za_ref, tri_ref, o_ref, tot_ref, g_ref, dqz_ref, dkv_ref, dk_acc, dv_acc):
        i = pl.program_id(1)

        @pl.when(i == 0)
        def _():
            dk_acc[...] = jnp.zeros_like(dk_acc)
            dv_acc[...] = jnp.zeros_like(dv_acc)

        _, head0, strict = _pair_masks(t)
        upper, lower = tri_ref[0], tri_ref[1, 0:2 * t, :]
        za = za_ref[...].astype(F32)
        g = g_ref[...].astype(F32)
        sig = _sigmoid(za)
        dqz_ref[1] = (g * o_ref[...].astype(F32) * (sig * (1.0 + za * (1.0 - sig)))).astype(BF16)
        do = (g * (za * sig)).astype(BF16)
        q = q_ref[...] * scale
        tot_v = tot_ref[...]
        q2, do2, init = [], [], []
        zcol = jnp.zeros((t, 1), F32)
        for p in range(pp):
            cols = slice(p * PAIR, (p + 1) * PAIR)
            q2.append(_by_head(q[:, cols], head0))
            do2.append(_by_head(do[:, cols], head0))
            tp = tot_v[:, cols]
            init.append(((tp[:, LANE_TOT0:LANE_TOT0 + 1], tp[:, LANE_TOT1:LANE_TOT1 + 1]), (zcol, zcol),
                         jnp.zeros((t, PAIR), F32)))
        first = jnp.clip(tot_v[0:1, LANE_FIRST:LANE_FIRST + 1], 0.0, i.astype(F32)).astype(jnp.int32)[0, 0]

        def sweep(blocks, state):
            staged = []
            for j, masked in blocks:
                start = pl.multiple_of(j * t, t)
                kb = k_ref[pl.ds(start, t), :]
                vb = v_ref[pl.ds(start, t), :]
                for p in range(pp):
                    cols = slice(p * PAIR, (p + 1) * PAIR)
                    k2 = _by_head(kb[:, cols], head0)
                    z = _dot(q[:, cols], k2, NT)
                    ls, sp = _softplus_parts(z, strict, masked)
                    da = _dot(do[:, cols], _by_head(vb[:, cols], head0), NT)
                    staged.append((p, masked, k2, ls, sp, da))
            afters = _stacked_dot([_hi_lo(sp) for _, _, _, _, sp, _ in staged], upper)
            state = list(state)
            weights, ggs = [], []
            for (p, masked, k2, ls, sp, da), after in zip(staged, afters):
                (r0, r1), befores, dq = state[p]
                r0 = r0 - (after[:, 0:1] + sp[:, 0:1])
                r1 = r1 - (after[:, t:t + 1] + sp[:, t:t + 1])
                x = ls - after
                a = jnp.exp(jnp.concatenate([x[:, :t] - r0, x[:, t:] - r1], axis=1))
                if masked:
                    a = jnp.where(strict, a, 0.0)
                state[p] = ((r0, r1), befores, dq)
                weights.append(a.astype(BF16))
                ggs.append(a * da)
            pres = _stacked_dot([gg.astype(BF16) for gg in ggs], lower)
            dzs = []
            for (p, masked, k2, ls, sp, da), gg, pre in zip(staged, ggs, pres):
                rests, (b0, b1), dq = state[p]
                y = gg + pre
                dz = gg - jnp.exp(ls) * jnp.concatenate([y[:, :t] + b0, y[:, t:] + b1], axis=1)
                if masked:
                    dz = jnp.where(strict, dz, 0.0)
                dzb = dz.astype(BF16)
                dzs.append(dzb)
                state[p] = (rests, (b0 + y[:, t - 1:t], b1 + y[:, 2 * t - 1:2 * t]), dq + _dot(dzb, k2))
            first_row = pl.multiple_of(blocks[0][0] * t, t)
            n_rows = len(blocks) * t
            for p in range(pp):
                cols = slice(p * PAIR, (p + 1) * PAIR)

                def by_key(tiles):
                    return jnp.concatenate(
                        [jnp.concatenate([m[:, :t], m[:, t:]], axis=0).T for m in tiles], axis=0)

                mine = [n for n in range(len(staged)) if staged[n][0] == p]
                dk_acc[pl.ds(first_row, n_rows), cols] += _dot(by_key([dzs[n] for n in mine]), q2[p])
                dv_acc[pl.ds(first_row, n_rows), cols] += _dot(by_key([weights[n] for n in mine]), do2[p])
            return tuple(state)

        many = i >= FAST_BLOCKS - 1
        last_single = jnp.where(many, i - (FAST_BLOCKS - 1), i)
        state = lax.fori_loop(first, last_single, lambda j, s: sweep(((j, False),), s), tuple(init))
        state = lax.cond(
            many,
            lambda: sweep(tuple((i - b, False) for b in range(FAST_BLOCKS - 1, 0, -1)) + ((i, True),), state),
            lambda: sweep(((i, True),), state))
        for p in range(pp):
            dqz_ref[0, :, p * PAIR:(p + 1) * PAIR] = (state[p][2] * scale).astype(BF16)

        @pl.when(i == nb - 1)
        def _():
            dkv_ref[0] = dk_acc[...].astype(BF16)
            dkv_ref[1] = dv_acc[...].astype(BF16)

    def tile(j):
        return pl.BlockSpec((None, t, width), lambda g, i, j=j: (j, i, g))

    def full(j):
        return pl.BlockSpec((None, seq, width), lambda g, i, j=j: (j, 0, g))

    flat_tile = pl.BlockSpec((t, width), lambda g, i: (i, g))
    return pl.pallas_call(
        body, name="attn_bwd", grid=(cw // width, nb),
        in_specs=[tile(4), full(5), full(6), tile(7), pl.BlockSpec(tri.shape, lambda g, i: (0, 0, 0)),
                  flat_tile, flat_tile, flat_tile],
        out_specs=(pl.BlockSpec((2, t, width), lambda g, i: (0, i, g)),
                   pl.BlockSpec((2, seq, width), lambda g, i: (0, 0, g))),
        out_shape=(jax.ShapeDtypeStruct((2, seq, cw), BF16), jax.ShapeDtypeStruct((2, seq, cw), BF16)),
        scratch_shapes=[pltpu.VMEM((seq, width), F32), pltpu.VMEM((seq, width), F32)],
        compiler_params=_params(48),
    )(proj, proj, proj, proj, tri, o, tot, d_mix_attn)


def _out_ln(mix_conv, mix_attn, x, target, gain, bias, w_out, tm):
    seq, d_model = x.shape
    cw = mix_conv.shape[1]
    inv_d = 1.0 / d_model

    def body(mc_ref, ma_ref, x_ref, t_ref, gain_ref, bias_ref, w_ref, dr_ref, dmc_ref, dma_ref, gwo_ref, small_ref):
        @pl.when(pl.program_id(0) == 0)
        def _():
            gwo_ref[...] = jnp.zeros_like(gwo_ref)
            small_ref[...] = jnp.zeros_like(small_ref)

        mix = jnp.concatenate([mc_ref[...], ma_ref[...]], axis=1)
        w = w_ref[...]
        r = ALPHA * x_ref[...] + _dot(mix, w)
        mu = jnp.sum(r, axis=1, keepdims=True) * inv_d
        xc = r - mu
        var = jnp.sum(xc * xc, axis=1, keepdims=True) * inv_d
        rstd = lax.rsqrt(var + LN_EPS)
        xhat = xc * rstd
        gain_v = gain_ref[...]
        err = xhat * gain_v + bias_ref[...] - t_ref[...]
        row_loss = jnp.sum(err * err, axis=1, keepdims=True)
        loss = (0.5 * inv_d) * jnp.sum(row_loss, axis=0, keepdims=True)
        dy = err * inv_d
        small_ref[ROW_GAIN:ROW_GAIN + 1, :] += jnp.sum(dy * xhat, axis=0, keepdims=True)
        small_ref[ROW_BIAS:ROW_BIAS + 1, :] += jnp.sum(dy, axis=0, keepdims=True)
        small_ref[ROW_LOSS:ROW_LOSS + 1, :] += jnp.broadcast_to(loss, (1, d_model))
        dxhat = dy * gain_v
        m1 = jnp.sum(dxhat, axis=1, keepdims=True) * inv_d
        m2 = jnp.sum(dxhat * xhat, axis=1, keepdims=True) * inv_d
        dr = rstd * (dxhat - m1 - xhat * m2)
        dr_ref[...] = dr
        drb = dr.astype(BF16)
        dmix = _dot(drb, w, NT)
        dmc_ref[...] = dmix[:, :cw].astype(BF16)
        dma_ref[...] = dmix[:, cw:].astype(BF16)
        gwo_ref[...] += _dot(mix, drb, TN)

    def rows(width):
        return pl.BlockSpec((tm, width), lambda i: (i, 0))

    def whole(shape):
        return pl.BlockSpec(shape, lambda i: (0, 0))

    return pl.pallas_call(
        body, name="out_ln", grid=(seq // tm,),
        in_specs=[rows(cw), rows(cw), rows(d_model), rows(d_model), whole((1, d_model)), whole((1, d_model)),
                  whole((d_model, d_model))],
        out_specs=(rows(d_model), rows(cw), rows(cw), whole((d_model, d_model)), whole((SUBLANES, d_model))),
        out_shape=(jax.ShapeDtypeStruct((seq, d_model), F32), jax.ShapeDtypeStruct((seq, cw), BF16),
                   jax.ShapeDtypeStruct((seq, cw), BF16), jax.ShapeDtypeStruct((d_model, d_model), F32),
                   jax.ShapeDtypeStruct((SUBLANES, d_model), F32)),
        compiler_params=_params(48),
    )(mix_conv, mix_attn, x, target, gain, bias, w_out)


_DP_OF_GROUP = ((0, 0), (0, 1), (0, 2), (0, 3), (1, 0), (2, 0), (2, 1), (1, 1))


def _grad_w_reduce(xt, dp_parts, dr, win_all, gwo, small, rows_out, row_chunk, tm):
    d_model, seq = xt.shape
    nch, _, cw = win_all.shape
    gx_tiles = seq // tm
    assert nch * tm == seq and nch * cw == seq, "needs S == 8 * tm == 8 * CW"

    def body(xt_hbm, dpa, dpb, dpc, dr_hbm, win_hbm, gwo_ref, small_ref,
             gx_hbm, g_in_o, g_out_o, small_o,
             xt_v, dp_buf, acc, got_in, send_in, recv_in, own_out, got_out, send_out, recv_out, small_all,
             dr_buf, gx_buf,
             xt_sem, dp_sems, loc_sems, d2d_send, d2d_recv, ici_send, ici_recv, sm_send, sm_recv,
             dr_sems, dpx_sems, w_sems, gx_sem):
        x, y, c = _mesh_pos()
        me = 4 * x + 2 * y + c
        sibling = (x, y, 1 - c)
        chips = [(1 - x, 1 - y), (1 - x, y), (x, 1 - y)]
        owners = [(*chip, cc) for chip in chips for cc in (1 - c, c)] + [sibling, (x, y, c)]
        group_of = [4 * o[0] + 2 * o[1] + o[2] for o in owners]
        dp_parts_ = (dpa, dpb, dpc)
        dp_groups = [dp_parts_[arr].at[idx] for arr, idx in _DP_OF_GROUP]

        xt_copy = pltpu.make_async_copy(xt_hbm, xt_v, xt_sem)
        xt_copy.start()

        def dp_start(step):
            for k in range(N_DEV):
                @pl.when(group_of[step] == k)
                def _(k=k):
                    pltpu.make_async_copy(dp_groups[k], dp_buf.at[step % 2], dp_sems.at[step % 2]).start()

        def dp_wait(step):
            pltpu.make_async_copy(dp_groups[0], dp_buf.at[step % 2], dp_sems.at[step % 2]).wait()

        dp_start(0)

        small_all[me] = small_ref[...]
        for d in range(N_DEV):
            @pl.when(d != me)
            def _(d=d):
                pltpu.make_async_remote_copy(
                    src_ref=small_ref, dst_ref=small_all.at[me], send_sem=sm_send.at[d], recv_sem=sm_recv.at[me],
                    device_id=(d // 4, (d // 2) % 2, d % 2), device_id_type=MESH).start()

        def block_out(k):
            return gwo_ref.at[pl.ds(k * rows_out, rows_out), :]

        for k in range(N_DEV):
            s = k // 2

            @pl.when(k % 2 != c)
            def _(k=k, s=s):
                pltpu.make_async_remote_copy(
                    src_ref=block_out(k), dst_ref=got_out.at[s], send_sem=d2d_send.at[1, s],
                    recv_sem=d2d_recv.at[1, s], device_id=sibling, device_id_type=MESH).start()

            @pl.when(k % 2 == c)
            def _(k=k, s=s):
                pltpu.make_async_copy(block_out(k), own_out.at[s], loc_sems.at[s]).start()

        VIA_X, VIA_Y, X_UP, X_LOW, Y_UP, Y_LOW = range(6)
        x_dev, y_dev = (1 - x, y, c), (x, 1 - y, c)
        ici_place = {VIA_X: (0, 0, x_dev), VIA_Y: (0, 1, y_dev), X_UP: (1, 0, x_dev), X_LOW: (1, 1, x_dev),
                     Y_UP: (2, 0, y_dev), Y_LOW: (2, 1, y_dev)}

        def ici_copy(a, k):
            send, recv = ((send_in, recv_in), (send_out, recv_out))[a]
            half = (d_model, rows_out)[a] // 2
            slot, part, to = ici_place[k]
            rows = pl.ds(part * half, half)
            return pltpu.make_async_remote_copy(
                src_ref=send.at[slot, rows, :], dst_ref=recv.at[slot, rows, :], send_sem=ici_send.at[a, k],
                recv_sem=ici_recv.at[a, k], device_id=to, device_id_type=MESH)

        def by_rows(first, n_rows, fn):
            step = min(row_chunk, n_rows)

            def rows_body(r, _):
                fn(pl.ds(pl.multiple_of(first + r * step, step), step))
                return 0

            lax.fori_loop(0, n_rows // step, rows_body, 0)

        def send_chip_sum(a, j, chip_sum):
            send, recv = ((send_in, recv_in), (send_out, recv_out))[a]
            n_rows = (d_model, rows_out)[a]
            half = n_rows // 2

            def plain(rows):
                send[j, rows, :] = chip_sum(rows).astype(BF16)

            if j == 0:
                by_rows(0, n_rows, plain)
                ici_copy(a, VIA_X).start()
                ici_copy(a, VIA_Y).start()
                return
            free, bound, passed = ((0, X_UP), (1, X_LOW), VIA_Y) if j == 1 else ((1, Y_LOW), (0, Y_UP), VIA_X)
            by_rows(free[0] * half, half, plain)
            ici_copy(a, free[1]).start()
            ici_copy(a, passed).wait_recv()

            def with_passed(rows):
                send[j, rows, :] = (chip_sum(rows) + recv[0, rows, :].astype(F32)).astype(BF16)

            by_rows(bound[0] * half, half, with_passed)
            ici_copy(a, bound[1]).start()

        for s in range(4):
            pltpu.make_async_copy(own_out.at[s], own_out.at[s], loc_sems.at[s]).wait()
            pltpu.make_async_remote_copy(
                src_ref=got_out.at[s], dst_ref=got_out.at[s], send_sem=d2d_send.at[1, s], recv_sem=d2d_recv.at[1, s],
                device_id=sibling, device_id_type=MESH).wait()
        for j, chip in enumerate(chips):
            s = 2 * chip[0] + chip[1]
            send_chip_sum(1, j, lambda rows, s=s: own_out[s, rows, :] + got_out[s, rows, :])

        xt_copy.wait()

        def d2d_copy(slot, pair):
            return pltpu.make_async_remote_copy(
                src_ref=acc.at[slot], dst_ref=got_in.at[pair], send_sem=d2d_send.at[0, pair],
                recv_sem=d2d_recv.at[0, pair], device_id=sibling, device_id_type=MESH)

        for step in range(N_DEV):
            slot, pair = step % 2, step // 2
            if step % 2 == 0 and step >= 2:
                d2d_copy(slot, pair - 1).wait_send()

            dp_wait(step)
            if step + 1 < N_DEV:
                dp_start(step + 1)
            acc[slot] = _dot(xt_v[...], dp_buf[step % 2])

            if step % 2 == 0:
                d2d_copy(slot, pair).start()
            else:
                d2d_copy(slot, pair).wait_recv()
                if step < N_DEV - 1:
                    send_chip_sum(0, pair, lambda rows, slot=slot, pair=pair: acc[slot, rows, :] + got_in[pair, rows, :])

        def w_copy(k):
            return pltpu.make_async_copy(win_hbm.at[k], xt_v.at[:, pl.ds(k * cw, cw)], w_sems.at[k])

        for k in range(N_DEV):
            w_copy(k).start()

        def gx_in(tile, buf):
            rows = pl.ds(tile * tm, tm)
            copies = [pltpu.make_async_copy(dr_hbm.at[rows, :], dr_buf.at[buf], dr_sems.at[buf])]
            copies += [pltpu.make_async_copy(dp_groups[k].at[rows, :], dp_buf.at[buf, pl.ds(k * tm, tm), :],
                                             dpx_sems.at[buf, k]) for k in range(N_DEV)]
            return copies

        def gx_out(tile):
            return pltpu.make_async_copy(gx_buf, gx_hbm.at[pl.ds(tile * tm, tm), :], gx_sem)

        for cp in gx_in(0, 0):
            cp.start()
        for k in range(N_DEV):
            w_copy(k).wait()

        def gx_body(tile, _):
            buf = tile % 2
            for cp in gx_in(tile, buf):
                cp.wait()

            @pl.when(tile + 1 < gx_tiles)
            def _():
                for cp in gx_in(tile + 1, 1 - buf):
                    cp.start()

            val = ALPHA * dr_buf[buf]
            for k in range(N_DEV):
                val = val + _dot(dp_buf[buf, k * tm:(k + 1) * tm, :], xt_v[:, k * cw:(k + 1) * cw], NT)

            @pl.when(tile > 0)
            def _():
                gx_out(tile - 1).wait()

            gx_buf[...] = val
            gx_out(tile).start()
            return 0

        lax.fori_loop(0, gx_tiles, gx_body, 0)

        for d in range(N_DEV):
            @pl.when(d != me)
            def _(d=d):
                pltpu.make_async_remote_copy(
                    src_ref=small_ref, dst_ref=small_all.at[d], send_sem=sm_send.at[d], recv_sem=sm_recv.at[d],
                    device_id=(d // 4, (d // 2) % 2, d % 2), device_id_type=MESH).wait()
        total = small_all[0]
        for d in range(1, N_DEV):
            total = total + small_all[d]
        small_o[...] = total

        mine = 2 * x + y
        last = (N_DEV - 1) % 2

        def finish(a, n_rows, chip_sum, g_o):
            recv = (recv_in, recv_out)[a]
            for k in (X_UP, X_LOW, Y_UP, Y_LOW):
                ici_copy(a, k).wait_recv()
            for k in ici_place:
                ici_copy(a, k).wait_send()

            def total_rows(rows):
                g_o[rows, :] = chip_sum(rows) + recv[1, rows, :].astype(F32) + recv[2, rows, :].astype(F32)

            by_rows(0, n_rows, total_rows)

        finish(1, rows_out, lambda rows: own_out[mine, rows, :] + got_out[mine, rows, :], g_out_o)
        finish(0, d_model, lambda rows: acc[last, rows, :] + got_in[N_DEV // 2 - 1, rows, :], g_in_o)
        d2d_copy(0, N_DEV // 2 - 1).wait_send()
        gx_out(0).wait()

    vmem = pl.BlockSpec(memory_space=pltpu.VMEM)
    hbm = pl.BlockSpec(memory_space=pl.ANY)
    return pl.pallas_call(
        body, name="grad_w_reduce",
        in_specs=[hbm] * 7 + [vmem],
        out_specs=(hbm, vmem, vmem, vmem),
        out_shape=(jax.ShapeDtypeStruct((seq, d_model), F32), jax.ShapeDtypeStruct((d_model, cw), F32),
                   jax.ShapeDtypeStruct((rows_out, d_model), F32), jax.ShapeDtypeStruct(small.shape, F32)),
        scratch_shapes=[
            pltpu.VMEM((d_model, seq), BF16), pltpu.VMEM((2, seq, cw), BF16), pltpu.VMEM((2, d_model, cw), F32),
            pltpu.VMEM((4, d_model, cw), F32), pltpu.VMEM((3, d_model, cw), BF16), pltpu.VMEM((3, d_model, cw), BF16),
            pltpu.VMEM((4, rows_out, d_model), F32), pltpu.VMEM((4, rows_out, d_model), F32),
            pltpu.VMEM((3, rows_out, d_model), BF16), pltpu.VMEM((3, rows_out, d_model), BF16),
            pltpu.VMEM((N_DEV,) + small.shape, F32),
            pltpu.VMEM((2, tm, d_model), F32), pltpu.VMEM((tm, d_model), F32),
            pltpu.SemaphoreType.DMA, pltpu.SemaphoreType.DMA((2,)), pltpu.SemaphoreType.DMA((4,)),
            pltpu.SemaphoreType.DMA((2, 4)), pltpu.SemaphoreType.DMA((2, 4)),
            pltpu.SemaphoreType.DMA((2, 6)), pltpu.SemaphoreType.DMA((2, 6)),
            pltpu.SemaphoreType.DMA((N_DEV,)), pltpu.SemaphoreType.DMA((N_DEV,)),
            pltpu.SemaphoreType.DMA((2,)), pltpu.SemaphoreType.DMA((2, N_DEV)), pltpu.SemaphoreType.DMA((N_DEV,)),
            pltpu.SemaphoreType.DMA,
        ],
        compiler_params=_params(56),
    )(xt, *dp_parts, dr, win_all, gwo, small)


def _adamw_update(g, w, m, v, rows):
    n_rows, width = w.shape
    rows = min(rows, n_rows)

    def body(g_ref, w_ref, m_ref, v_ref, d_o, nm_o, nv_o):
        d_o[...], nm_o[...], nv_o[...] = _adamw(w_ref[...], g_ref[...], m_ref[...], v_ref[...])

    tile = pl.BlockSpec((rows, width), lambda i: (i, 0))
    shape = jax.ShapeDtypeStruct(w.shape, F32)
    return pl.pallas_call(
        body, name="adamw_update", grid=(n_rows // rows,), in_specs=[tile] * 4, out_specs=(tile,) * 3,
        out_shape=(shape,) * 3,
    )(g, w, m, v)


def _small_update(grads, weights, ms, vs):
    n = len(grads)

    def body(*refs):
        g_refs, w_refs, m_refs, v_refs = (refs[i * n:(i + 1) * n] for i in range(4))
        outs = refs[4 * n:]
        for i in range(n):
            delta, nm, nv = _adamw(w_refs[i][...], g_refs[i][...], m_refs[i][...], v_refs[i][...])
            outs[3 * i][...] = delta
            outs[3 * i + 1][...] = nm
            outs[3 * i + 2][...] = nv

    vmem = pl.BlockSpec(memory_space=pltpu.VMEM)
    out_shape = []
    for w in weights:
        out_shape += [jax.ShapeDtypeStruct(w.shape, F32)] * 3
    return pl.pallas_call(
        body, name="small_update", in_specs=[vmem] * (4 * n), out_specs=(vmem,) * (3 * n), out_shape=tuple(out_shape),
    )(*grads, *weights, *ms, *vs)


def _tile_sizes(seq):
    return dict(tm=seq // N_DEV, t_ln=min(512, seq), t_attn=min(128, seq), rc=min(256, seq), pairs=4)


def kernel(x, w_in, conv_w, w_out, ln_gain, ln_bias, loss_target, m_w_in, m_conv_w, m_w_out, m_ln_gain, m_ln_bias,
           v_w_in, v_conv_w, v_w_out, v_ln_gain, v_ln_bias):
    assert x.shape[0] == 1 and w_in.shape[0] == 1, "one sequence per device, depth 1"
    _, seq, d_model = x.shape
    cw = w_in.shape[2]
    conv_k, conv_cols = conv_w.shape[1], conv_w.shape[2]
    rows_out = w_out.shape[1]
    assert cw == d_model // 2 and cw % PAIR == 0 and conv_cols * N_DEV == cw and rows_out * N_DEV == d_model
    ts = _tile_sizes(seq)

    x2 = x.reshape(seq, d_model)
    target = loss_target.reshape(seq, d_model)
    me = 4 * lax.axis_index("x") + 2 * lax.axis_index("y") + lax.axis_index("c")

    conv_pad = jnp.pad(conv_w[0], ((0, SUBLANES - conv_k), (0, PAIR - conv_cols)))
    proj, xt, win_all, wout_all, conv_all = _gather_proj(x2, w_in[0], w_out[0], conv_pad, ts["tm"])
    w_out_full = wout_all.reshape(d_model, d_model)
    conv_full = conv_all[:, :conv_k, :conv_cols].transpose(1, 0, 2).reshape(conv_k, cw)
    conv_full = jnp.pad(conv_full, ((0, SUBLANES - conv_k), (0, 0)))

    mix_conv = _conv_fwd(proj, conv_full, ts["rc"])
    pairs = min(ts["pairs"], cw // PAIR)
    tri = _triangles(ts["t_attn"])
    o, mix_attn, tot = _attn_fwd(proj, tri, ts["t_attn"], pairs)
    dr, d_mix_conv, d_mix_attn, gwo, small = _out_ln(mix_conv, mix_attn, x2, target, ln_gain, ln_bias, w_out_full,
                                                     ts["t_ln"])
    dp_conv, d_taps = _conv_bwd(proj, conv_full, d_mix_conv, ts["rc"])
    dp_qz, dp_kv = _attn_bwd(proj, tri, o, tot, d_mix_attn, ts["t_attn"], pairs)
    small = small.at[ROW_CONV:ROW_CONV + conv_k, :cw].set(d_taps[:conv_k])
    grad_x, g_in, g_out, small_sum = _grad_w_reduce(
        xt, (dp_conv, dp_qz, dp_kv), dr, win_all, gwo, small, rows_out, 128, ts["tm"])
    d_in, nm_in, nv_in = _adamw_update(g_in, w_in[0], m_w_in[0], v_w_in[0], 256)
    d_out, nm_out, nv_out = _adamw_update(g_out, w_out[0], m_w_out[0], v_w_out[0], 256)

    loss = small_sum[ROW_LOSS, 0]
    g_gain = small_sum[ROW_GAIN:ROW_GAIN + 1]
    g_bias = small_sum[ROW_BIAS:ROW_BIAS + 1]
    g_conv = lax.dynamic_slice(small_sum, (ROW_CONV, me * conv_cols), (conv_k, conv_cols))
    upd = _small_update((g_conv, g_gain, g_bias), (conv_w[0], ln_gain, ln_bias),
                        (m_conv_w[0], m_ln_gain, m_ln_bias), (v_conv_w[0], v_ln_gain, v_ln_bias))
    d_conv, nm_conv, nv_conv, d_gain, nm_gain, nv_gain, d_bias, nm_bias, nv_bias = upd

    lead = lambda a: a[None]
    return (loss, grad_x.reshape(1, seq, d_model), lead(g_in), lead(g_conv), lead(g_out), g_gain, g_bias,
            lead(d_in), lead(d_conv), lead(d_out), d_gain, d_bias,
            lead(nm_in), lead(nm_conv), lead(nm_out), nm_gain, nm_bias,
            lead(nv_in), lead(nv_conv), lead(nv_out), nv_gain, nv_bias)
```

```python
import functools

import jax
import jax.numpy as jnp
from jax import lax
from jax.experimental import pallas as pl
from jax.experimental.pallas import tpu as pltpu

F32 = jnp.float32
BF16 = jnp.bfloat16
MESH = pl.DeviceIdType.MESH

N_DEV = 8
HEAD_DIM = 64
PAIR = 128
SUBLANES = 8
LN_EPS = 1e-5
ALPHA = 2.0 ** 0.25
ADAM_LR, ADAM_B1, ADAM_B2, ADAM_EPS, ADAM_WD, ADAM_STEP = 0.001, 0.9, 0.999, 1e-08, 0.01, 10

ROW_GAIN, ROW_BIAS, ROW_CONV, ROW_LOSS = 0, 1, 2, 5

NT = (((1,), (1,)), ((), ()))
TN = (((0,), (0,)), ((), ()))


V7X_VMEM_BYTES = 64 * 1024 * 1024


def _params(vmem_mib):
    assert vmem_mib * 1024 * 1024 < V7X_VMEM_BYTES
    return pltpu.CompilerParams(vmem_limit_bytes=vmem_mib * 1024 * 1024)


def _dot(a, b, dims=None):
    if dims is None:
        return jnp.dot(a, b, preferred_element_type=F32)
    return lax.dot_general(a, b, dims, preferred_element_type=F32)


def _sigmoid(z):
    return 1.0 / (1.0 + jnp.exp(-z))


def _mesh_pos():
    return lax.axis_index("x"), lax.axis_index("y"), lax.axis_index("c")


def _adamw(w, g, m, v):
    nm = ADAM_B1 * m + (1.0 - ADAM_B1) * g
    nv = ADAM_B2 * v + (1.0 - ADAM_B2) * (g * g)
    m_hat = nm * (1.0 / (1.0 - ADAM_B1 ** ADAM_STEP))
    v_hat = nv * (1.0 / (1.0 - ADAM_B2 ** ADAM_STEP))
    delta = -ADAM_LR * (m_hat / (jnp.sqrt(v_hat) + ADAM_EPS) + ADAM_WD * w)
    return delta, nm, nv


def _gather_proj(x, w_in_s, w_out_s, conv_s, tm):
    seq, d_model = x.shape
    cw = w_in_s.shape[1]
    rows_out = w_out_s.shape[0]
    n_tiles = seq // tm
    half = d_model // 2
    SIB, X_UP, X_LOW, Y_UP, Y_LOW, VIA_Y, VIA_X, ON_X, ON_Y, ON_DIAG = range(10)

    def body(x_hbm, win_ref, wout_ref, conv_ref, proj_hbm, xt_hbm, win_all, wout_all, conv_all,
             xb, x_stage, o_stage, xt_stage, x_sems, o_sems, xt_sems, w_send, w_recv, send_sems, recv_sems):
        x, y, c = _mesh_pos()
        me = (x, y, c)
        sibling = (x, y, 1 - c)
        x_nbr, y_nbr, diag = (1 - x, y), (x, 1 - y), (1 - x, 1 - y)
        chips = [x_nbr, y_nbr, diag]
        small = (wout_all, conv_all)

        def slot(pos):
            return 4 * pos[0] + 2 * pos[1] + pos[2]

        def x_copy(tile, buf):
            return pltpu.make_async_copy(x_hbm.at[pl.ds(tile * tm, tm), :], x_stage.at[buf], x_sems.at[buf])

        x_copy(0, 0).start()
        win_all[slot(me)] = win_ref[...].astype(BF16)
        wout_all[slot(me)] = wout_ref[...].astype(BF16)
        conv_all[slot(me)] = conv_ref[...]

        def w_copy(k, block, part, to):
            ref = win_all.at[slot(block)]
            if part is not None:
                ref = ref.at[pl.ds(part * half, half), :]
            return pltpu.make_async_remote_copy(
                src_ref=ref, dst_ref=ref, send_sem=w_send.at[k], recv_sem=w_recv.at[k],
                device_id=to, device_id_type=MESH)

        def copy(a, k, block, to):
            ref = small[a].at[slot(block)]
            return pltpu.make_async_remote_copy(
                src_ref=ref, dst_ref=ref, send_sem=send_sems.at[a, k], recv_sem=recv_sems.at[a, k],
                device_id=to, device_id_type=MESH)

        sends = [w_copy(SIB, me, None, sibling),
                 w_copy(X_UP, me, 0, (*x_nbr, c)), w_copy(Y_LOW, me, 1, (*y_nbr, c)),
                 w_copy(X_LOW, me, 1, (*x_nbr, c)), w_copy(Y_UP, me, 0, (*y_nbr, c))]
        for a in range(len(small)):
            sends.append(copy(a, 0, me, sibling))
            sends += [copy(a, 1 + j, me, (*chip, c)) for j, chip in enumerate(chips)]
        for cp in sends:
            cp.start()

        def o_copy(group, tile, buf):
            return pltpu.make_async_copy(o_stage.at[buf], proj_hbm.at[group, pl.ds(tile * tm, tm), :], o_sems.at[buf])

        def xt_copy(tile, buf):
            return pltpu.make_async_copy(xt_stage.at[buf], xt_hbm.at[:, pl.ds(tile * tm, tm)], xt_sems.at[buf])

        def project(order, group, first_pass):
            def tile_body(tile, _):
                if first_pass:
                    buf = tile % 2
                    x_copy(tile, buf).wait()

                    @pl.when(tile + 1 < n_tiles)
                    def _():
                        x_copy(tile + 1, 1 - buf).start()

                    xv = x_stage[buf]
                    xb[tile] = xv.astype(BF16)

                    @pl.when(tile >= 2)
                    def _():
                        xt_copy(tile - 2, buf).wait()

                    xt_stage[buf] = xv.T.astype(BF16)
                    xt_copy(tile, buf).start()
                count = order * n_tiles + tile
                obuf = count % 2

                @pl.when(count >= 2)
                def _():
                    o_copy(group, tile, obuf).wait()

                o_stage[obuf] = _dot(xb[tile], win_all[group]).astype(BF16)
                o_copy(group, tile, obuf).start()
                return 0

            lax.fori_loop(0, n_tiles, tile_body, 0)

        def start(cp):
            cp.start()
            sends.append(cp)

        def small_pass_on(j):
            for a in range(len(small)):
                copy(a, 1 + j, (*chips[j], c), me).wait_recv()
                start(copy(a, 4 + j, (*chips[j], c), sibling))

        def small_from_sibling(k):
            for a in range(len(small)):
                copy(a, k, sibling, me).wait_recv()

        project(0, slot(me), True)
        w_copy(SIB, sibling, None, me).wait_recv()
        small_from_sibling(0)
        project(1, slot(sibling), False)
        w_copy(X_UP, (*x_nbr, c), 0, me).wait_recv()
        start(w_copy(VIA_Y, (*x_nbr, c), 0, (*y_nbr, c)))
        w_copy(Y_LOW, (*y_nbr, c), 1, me).wait_recv()
        start(w_copy(VIA_X, (*y_nbr, c), 1, (*x_nbr, c)))
        w_copy(X_LOW, (*x_nbr, c), 1, me).wait_recv()
        start(w_copy(ON_X, (*x_nbr, c), None, sibling))
        small_pass_on(0)
        project(2, slot((*x_nbr, c)), False)
        w_copy(Y_UP, (*y_nbr, c), 0, me).wait_recv()
        start(w_copy(ON_Y, (*y_nbr, c), None, sibling))
        small_pass_on(1)
        project(3, slot((*y_nbr, c)), False)
        w_copy(ON_X, (*x_nbr, 1 - c), None, me).wait_recv()
        small_from_sibling(4)
        project(4, slot((*x_nbr, 1 - c)), False)
        w_copy(ON_Y, (*y_nbr, 1 - c), None, me).wait_recv()
        small_from_sibling(5)
        project(5, slot((*y_nbr, 1 - c)), False)
        w_copy(VIA_Y, (*diag, c), 0, me).wait_recv()
        w_copy(VIA_X, (*diag, c), 1, me).wait_recv()
        start(w_copy(ON_DIAG, (*diag, c), None, sibling))
        small_pass_on(2)
        project(6, slot((*diag, c)), False)
        w_copy(ON_DIAG, (*diag, 1 - c), None, me).wait_recv()
        small_from_sibling(6)
        project(7, slot((*diag, 1 - c)), False)

        for buf in range(2):
            o_copy(0, 0, buf).wait()
        for buf in range(min(2, n_tiles)):
            xt_copy(0, buf).wait()
        for cp in sends:
            cp.wait_send()

    vmem = pl.BlockSpec(memory_space=pltpu.VMEM)
    hbm = pl.BlockSpec(memory_space=pl.ANY)
    return pl.pallas_call(
        body, name="gather_proj",
        out_shape=(jax.ShapeDtypeStruct((N_DEV, seq, cw), BF16),
                   jax.ShapeDtypeStruct((d_model, seq), BF16),
                   jax.ShapeDtypeStruct((N_DEV, d_model, cw), BF16),
                   jax.ShapeDtypeStruct((N_DEV, rows_out, d_model), BF16),
                   jax.ShapeDtypeStruct((N_DEV,) + conv_s.shape, F32)),
        in_specs=[hbm, vmem, vmem, vmem], out_specs=(hbm, hbm, vmem, vmem, vmem),
        scratch_shapes=[
            pltpu.VMEM((n_tiles, tm, d_model), BF16), pltpu.VMEM((2, tm, d_model), F32),
            pltpu.VMEM((2, tm, cw), BF16), pltpu.VMEM((2, d_model, tm), BF16),
            pltpu.SemaphoreType.DMA((2,)), pltpu.SemaphoreType.DMA((2,)), pltpu.SemaphoreType.DMA((2,)),
            pltpu.SemaphoreType.DMA((10,)), pltpu.SemaphoreType.DMA((10,)),
            pltpu.SemaphoreType.DMA((2, 7)), pltpu.SemaphoreType.DMA((2, 7))],
        compiler_params=_params(48),
    )(x, w_in_s, w_out_s, conv_s)


def _conv_taps(ext, w_ref, rc):
    u0 = ext[SUBLANES:SUBLANES + rc]
    u1 = pltpu.roll(ext, 1, 0)[SUBLANES:SUBLANES + rc]
    u2 = pltpu.roll(ext, 2, 0)[SUBLANES:SUBLANES + rc]
    return w_ref[2:3, :] * u0 + w_ref[1:2, :] * u1 + w_ref[0:1, :] * u2, u0, u1, u2


def _conv_fwd(proj, conv_full, rc):
    _, seq, cw = proj.shape

    def body(b_ref, c_ref, h_ref, z_ref, w_ref, o_ref, u_scr):
        u_scr[0:SUBLANES, :] = jnp.zeros((SUBLANES, PAIR), F32)

        def fill(r, _):
            base = pl.multiple_of(r * rc, rc)
            rows = pl.ds(base, rc)
            u_scr[pl.ds(base + SUBLANES, rc), :] = c_ref[rows, :].astype(F32) * h_ref[rows, :].astype(F32)
            return 0

        lax.fori_loop(0, seq // rc, fill, 0)

        def out(r, _):
            base = pl.multiple_of(r * rc, rc)
            rows = pl.ds(base, rc)
            ext = u_scr[pl.ds(base, rc + SUBLANES), :]
            y, _, _, _ = _conv_taps(ext, w_ref, rc)
            z = z_ref[rows, :].astype(F32)
            o_ref[rows, :] = (z * _sigmoid(z) * b_ref[rows, :].astype(F32) * y).astype(BF16)
            return 0

        lax.fori_loop(0, seq // rc, out, 0)

    def chunk(j):
        return pl.BlockSpec((None, seq, PAIR), lambda cb, j=j: (j, 0, cb))

    return pl.pallas_call(
        body, name="conv_fwd", grid=(cw // PAIR,),
        in_specs=[chunk(0), chunk(1), chunk(2), chunk(3), pl.BlockSpec((SUBLANES, PAIR), lambda cb: (0, cb))],
        out_specs=pl.BlockSpec((seq, PAIR), lambda cb: (0, cb)),
        out_shape=jax.ShapeDtypeStruct((seq, cw), BF16),
        scratch_shapes=[pltpu.VMEM((seq + SUBLANES, PAIR), F32)],
    )(proj, proj, proj, proj, conv_full)


def _conv_bwd(proj, conv_full, d_mix_conv, rc):
    _, seq, cw = proj.shape

    def body(b_ref, c_ref, h_ref, z_ref, w_ref, g_ref, dp_ref, dw_ref, u_scr, dy_scr):
        u_scr[0:SUBLANES, :] = jnp.zeros((SUBLANES, PAIR), F32)
        dy_scr[seq:seq + SUBLANES, :] = jnp.zeros((SUBLANES, PAIR), F32)

        def fill(r, _):
            base = pl.multiple_of(r * rc, rc)
            rows = pl.ds(base, rc)
            u_scr[pl.ds(base + SUBLANES, rc), :] = c_ref[rows, :].astype(F32) * h_ref[rows, :].astype(F32)
            return 0

        lax.fori_loop(0, seq // rc, fill, 0)

        def gate(r, acc):
            base = pl.multiple_of(r * rc, rc)
            rows = pl.ds(base, rc)
            ext = u_scr[pl.ds(base, rc + SUBLANES), :]
            y, u0, u1, u2 = _conv_taps(ext, w_ref, rc)
            z = z_ref[rows, :].astype(F32)
            b = b_ref[rows, :].astype(F32)
            g = g_ref[rows, :].astype(F32)
            sig = _sigmoid(z)
            dp_ref[3, rows, :] = (g * b * y * (sig * (1.0 + z * (1.0 - sig)))).astype(BF16)
            gs = g * (z * sig)
            dp_ref[0, rows, :] = (gs * y).astype(BF16)
            dy = gs * b
            dy_scr[rows, :] = dy
            a0, a1, a2 = acc
            return (a0 + jnp.sum(dy * u2, axis=0, keepdims=True),
                    a1 + jnp.sum(dy * u1, axis=0, keepdims=True),
                    a2 + jnp.sum(dy * u0, axis=0, keepdims=True))

        zero = jnp.zeros((1, PAIR), F32)
        a0, a1, a2 = lax.fori_loop(0, seq // rc, gate, (zero, zero, zero))
        dw_ref[...] = jnp.zeros((SUBLANES, PAIR), F32)
        dw_ref[0:1, :] = a0
        dw_ref[1:2, :] = a1
        dw_ref[2:3, :] = a2

        def back(r, _):
            base = pl.multiple_of(r * rc, rc)
            rows = pl.ds(base, rc)
            ext = dy_scr[pl.ds(base, rc + SUBLANES), :]
            n = rc + SUBLANES
            d0 = ext[0:rc]
            d1 = pltpu.roll(ext, n - 1, 0)[0:rc]
            d2 = pltpu.roll(ext, n - 2, 0)[0:rc]
            du = w_ref[2:3, :] * d0 + w_ref[1:2, :] * d1 + w_ref[0:1, :] * d2
            dp_ref[1, rows, :] = (du * h_ref[rows, :].astype(F32)).astype(BF16)
            dp_ref[2, rows, :] = (du * c_ref[rows, :].astype(F32)).astype(BF16)
            return 0

        lax.fori_loop(0, seq // rc, back, 0)

    def chunk(j):
        return pl.BlockSpec((None, seq, PAIR), lambda cb, j=j: (j, 0, cb))

    return pl.pallas_call(
        body, name="conv_bwd", grid=(cw // PAIR,),
        in_specs=[chunk(0), chunk(1), chunk(2), chunk(3), pl.BlockSpec((SUBLANES, PAIR), lambda cb: (0, cb)),
                  pl.BlockSpec((seq, PAIR), lambda cb: (0, cb))],
        out_specs=(pl.BlockSpec((4, seq, PAIR), lambda cb: (0, 0, cb)),
                   pl.BlockSpec((SUBLANES, PAIR), lambda cb: (0, cb))),
        out_shape=(jax.ShapeDtypeStruct((4, seq, cw), BF16), jax.ShapeDtypeStruct((SUBLANES, cw), F32)),
        scratch_shapes=[pltpu.VMEM((seq + SUBLANES, PAIR), F32), pltpu.VMEM((seq + SUBLANES, PAIR), F32)],
    )(proj, proj, proj, proj, conv_full, d_mix_conv)


SKIP_CARRY = 104.0
LANE_TOT0, LANE_TOT1, LANE_FIRST, LANE_WHOLE = 0, 1, 2, 3
FAST_BLOCKS = 3
EARLY_ROWS = 48


def _triangles(t):
    row = lax.broadcasted_iota(jnp.int32, (2 * t, 2 * t), 0)
    col = lax.broadcasted_iota(jnp.int32, (2 * t, 2 * t), 1)
    same = (row < t) == (col < t)
    upper = jnp.logical_and(same, row > col).astype(BF16)
    lower = jnp.logical_and(same, row < col).astype(BF16)
    return jnp.stack([jnp.concatenate([upper, upper], axis=0), jnp.concatenate([lower, lower], axis=0)])


def _pair_masks(t):
    lane = lax.broadcasted_iota(jnp.int32, (t, PAIR), 1)
    qrow = lax.broadcasted_iota(jnp.int32, (t, 2 * t), 0)
    kcol = lax.broadcasted_iota(jnp.int32, (t, 2 * t), 1)
    strict = jnp.where(kcol < t, kcol, kcol - t) < qrow
    return lane, lane < HEAD_DIM, strict


def _by_head(x, head0):
    zero = jnp.zeros_like(x)
    return jnp.concatenate([jnp.where(head0, x, zero), jnp.where(head0, zero, x)], axis=0)


def _hi_lo(a):
    hi = a.astype(BF16)
    lo = (a - hi.astype(F32)).astype(BF16)
    return jnp.concatenate([hi, lo], axis=1)


def _softplus_parts(z, strict, masked):
    spu = jnp.maximum(z, 0.0) + jnp.log(1.0 + jnp.exp(-jnp.abs(z)))
    return z - spu, (jnp.where(strict, spu, 0.0) if masked else spu)


def _stacked_dot(parts, rhs):
    out = _dot(jnp.concatenate(parts, axis=0), rhs)
    ends = [0]
    for p in parts:
        ends.append(ends[-1] + p.shape[0])
    return [out[a:b] for a, b in zip(ends[:-1], ends[1:])]


def _splice(whole, rows, part):
    pieces = ([whole[:rows[0]]] if rows[0] > 0 else []) + [part]
    if rows[1] < whole.shape[0]:
        pieces.append(whole[rows[1]:])
    return part if len(pieces) == 1 else jnp.concatenate(pieces, axis=0)


def _attn_fwd(proj, tri, t, pp):
    _, seq, cw = proj.shape
    scale = HEAD_DIM ** -0.5
    width = pp * PAIR

    def body(q_ref, k_ref, v_ref, za_ref, tri_ref, o_ref, mix_ref, tot_ref):
        i = pl.program_id(1)
        lane, head0, strict = _pair_masks(t)
        upper = tri_ref[0]
        q = q_ref[...] * scale

        def sweep(blocks, state):
            staged = []
            for j, masked, rows in blocks:
                start = pl.multiple_of(j * t, t)
                kb = k_ref[pl.ds(start, t), :]
                vb = v_ref[pl.ds(start, t), :]
                for p in range(pp):
                    cols = slice(p * PAIR, (p + 1) * PAIR)
                    z = _dot(q[rows[0]:rows[1], cols], _by_head(kb[:, cols], head0), NT)
                    ls, sp = _softplus_parts(z, strict[rows[0]:rows[1]], masked)
                    staged.append((p, masked, rows, ls, sp, _by_head(vb[:, cols], head0)))
            afters = _stacked_dot([_hi_lo(sp) for _, _, _, _, sp, _ in staged], upper)
            state = list(state)
            for (p, masked, rows, ls, sp, v2), after in zip(staged, afters):
                (c0, c1), acc = state[p]
                part = slice(rows[0], rows[1])
                x = ls - after
                w = jnp.exp(jnp.concatenate([x[:, :t] - c0[part], x[:, t:] - c1[part]], axis=1))
                if masked:
                    w = jnp.where(strict[part], w, 0.0)
                c0 = _splice(c0, rows, c0[part] + (after[:, 0:1] + sp[:, 0:1]))
                c1 = _splice(c1, rows, c1[part] + (after[:, t:t + 1] + sp[:, t:t + 1]))
                state[p] = ((c0, c1), _splice(acc, rows, acc[part] + _dot(w.astype(BF16), v2)))
            return tuple(state)

        def unfinished(state, rows):
            m = state[0][0][0]
            for p in range(pp):
                m = jnp.minimum(m, jnp.minimum(state[p][0][0], state[p][0][1]))
            return jnp.min(m[rows[0]:rows[1]]) < SKIP_CARRY

        every = (0, t)
        early, late = (0, min(EARLY_ROWS, t)), (min(EARLY_ROWS, t), t)

        def step(js):
            state = sweep(((js[0], False, every),), js[1])
            return js[0] - 1, state, unfinished(state, every)

        def fast():
            blocks = tuple((i - b, b == 0, every) for b in range(FAST_BLOCKS - 1))
            state = sweep(blocks + ((i - (FAST_BLOCKS - 1), False, early),), init)
            if late[0] == late[1]:
                return state, jnp.bool_(True)
            whole = unfinished(state, late)
            state = lax.cond(whole, lambda: sweep(((i - (FAST_BLOCKS - 1), False, late),), state), lambda: state)
            return state, whole

        zcol = jnp.zeros((t, 1), F32)
        init = tuple(((zcol, zcol), jnp.zeros((t, PAIR), F32)) for _ in range(pp))
        many = i >= FAST_BLOCKS - 1
        state, whole = lax.cond(many, fast, lambda: (sweep(((i, True, every),), init), jnp.bool_(True)))
        j_end, state, _ = lax.while_loop(
            lambda js: jnp.logical_and(js[0] >= 0, js[2]), step,
            (jnp.where(many, i - FAST_BLOCKS, i - 1), state, unfinished(state, every)))
        first = (j_end + 1).astype(F32)
        notes = jnp.where(lane == LANE_FIRST, first, whole.astype(F32))
        za = za_ref[...].astype(F32)
        for p in range(pp):
            (c0, c1), acc = state[p]
            cols = slice(p * PAIR, (p + 1) * PAIR)
            zp = za[:, cols]
            o_ref[:, cols] = acc.astype(BF16)
            mix_ref[:, cols] = (zp * _sigmoid(zp) * acc).astype(BF16)
            tot_ref[:, cols] = jnp.where(lane == LANE_TOT0, c0, jnp.where(lane == LANE_TOT1, c1, notes))

    def tile(j):
        return pl.BlockSpec((None, t, width), lambda g, i, j=j: (j, i, g))

    def full(j):
        return pl.BlockSpec((None, seq, width), lambda g, i, j=j: (j, 0, g))

    out_tile = pl.BlockSpec((t, width), lambda g, i: (i, g))
    return pl.pallas_call(
        body, name="attn_fwd", grid=(cw // width, seq // t),
        in_specs=[tile(4), full(5), full(6), tile(7), pl.BlockSpec(tri.shape, lambda g, i: (0, 0, 0))],
        out_specs=(out_tile, out_tile, out_tile),
        out_shape=(jax.ShapeDtypeStruct((seq, cw), BF16), jax.ShapeDtypeStruct((seq, cw), BF16),
                   jax.ShapeDtypeStruct((seq, cw), F32)),
    )(proj, proj, proj, proj, tri)


def _attn_bwd(proj, tri, o, tot, d_mix_attn, t, pp):
    _, seq, cw = proj.shape
    nb = seq // t
    scale = HEAD_DIM ** -0.5
    width = pp * PAIR

    def body(q_ref, k_ref, v_ref, za_ref, tri_ref, o_ref, tot_ref, g_ref, dqz_ref, dkv_ref, dk_acc, dv_acc):
        i = pl.program_id(1)

        @pl.when(i == 0)
        def _():
            dk_acc[...] = jnp.zeros_like(dk_acc)
            dv_acc[...] = jnp.zeros_like(dv_acc)

        _, head0, strict = _pair_masks(t)
        upper, lower = tri_ref[0], tri_ref[1, 0:2 * t, :]
        za = za_ref[...].astype(F32)
        g = g_ref[...].astype(F32)
        sig = _sigmoid(za)
        dqz_ref[1] = (g * o_ref[...].astype(F32) * (sig * (1.0 + za * (1.0 - sig)))).astype(BF16)
        do = (g * (za * sig)).astype(BF16)
        q = q_ref[...] * scale
        tot_v = tot_ref[...]
        q2, do2, init = [], [], []
        zcol = jnp.zeros((t, 1), F32)
        for p in range(pp):
            cols = slice(p * PAIR, (p + 1) * PAIR)
            q2.append(_by_head(q[:, cols], head0))
            do2.append(_by_head(do[:, cols], head0))
            tp = tot_v[:, cols]
            init.append(((tp[:, LANE_TOT0:LANE_TOT0 + 1], tp[:, LANE_TOT1:LANE_TOT1 + 1]), (zcol, zcol),
                         jnp.zeros((t, PAIR), F32)))
        first = jnp.clip(tot_v[0:1, LANE_FIRST:LANE_FIRST + 1], 0.0, i.astype(F32)).astype(jnp.int32)[0, 0]

        def sweep(blocks, state):
            staged = []
            for j, masked, rows in blocks:
                start = pl.multiple_of(j * t, t)
                kb = k_ref[pl.ds(start, t), :]
                vb = v_ref[pl.ds(start, t), :]
                part = slice(rows[0], rows[1])
                for p in range(pp):
                    cols = slice(p * PAIR, (p + 1) * PAIR)
                    k2 = _by_head(kb[:, cols], head0)
                    z = _dot(q[part, cols], k2, NT)
                    ls, sp = _softplus_parts(z, strict[part], masked)
                    da = _dot(do[part, cols], _by_head(vb[:, cols], head0), NT)
                    staged.append((p, masked, rows, k2, ls, sp, da))
            afters = _stacked_dot([_hi_lo(sp) for _, _, _, _, _, sp, _ in staged], upper)
            state = list(state)
            weights, ggs = [], []
            for (p, masked, rows, k2, ls, sp, da), after in zip(staged, afters):
                (s0, s1), befores, dq = state[p]
                part = slice(rows[0], rows[1])
                n0 = s0[part] - (after[:, 0:1] + sp[:, 0:1])
                n1 = s1[part] - (after[:, t:t + 1] + sp[:, t:t + 1])
                x = ls - after
                a = jnp.exp(jnp.concatenate([x[:, :t] - n0, x[:, t:] - n1], axis=1))
                if masked:
                    a = jnp.where(strict[part], a, 0.0)
                state[p] = ((_splice(s0, rows, n0), _splice(s1, rows, n1)), befores, dq)
                weights.append(a.astype(BF16))
                ggs.append(a * da)
            pres = _stacked_dot([gg.astype(BF16) for gg in ggs], lower)
            dzs = []
            for (p, masked, rows, k2, ls, sp, da), gg, pre in zip(staged, ggs, pres):
                rests, (b0, b1), dq = state[p]
                part = slice(rows[0], rows[1])
                y = gg + pre
                dz = gg - jnp.exp(ls) * jnp.concatenate([y[:, :t] + b0[part], y[:, t:] + b1[part]], axis=1)
                if masked:
                    dz = jnp.where(strict[part], dz, 0.0)
                dzb = dz.astype(BF16)
                dzs.append(dzb)
                state[p] = (rests, (_splice(b0, rows, b0[part] + y[:, t - 1:t]),
                                    _splice(b1, rows, b1[part] + y[:, 2 * t - 1:2 * t])),
                            _splice(dq, rows, dq[part] + _dot(dzb, k2)))
            first_row = pl.multiple_of(blocks[0][0] * t, t)
            n_rows = len(blocks) * t
            for p in range(pp):
                cols = slice(p * PAIR, (p + 1) * PAIR)

                def by_key(tiles):
                    out = []
                    for n, m in tiles:
                        rows = staged[n][2]
                        m = _splice(jnp.zeros((t, 2 * t), BF16), rows, m)
                        out.append(jnp.concatenate([m[:, :t], m[:, t:]], axis=0).T)
                    return jnp.concatenate(out, axis=0)

                mine = [n for n in range(len(staged)) if staged[n][0] == p]
                dk_acc[pl.ds(first_row, n_rows), cols] += _dot(by_key([(n, dzs[n]) for n in mine]), q2[p])
                dv_acc[pl.ds(first_row, n_rows), cols] += _dot(by_key([(n, weights[n]) for n in mine]), do2[p])
            return tuple(state)

        every = (0, t)
        early = (0, min(EARLY_ROWS, t))
        many = i >= FAST_BLOCKS - 1
        last_single = jnp.where(many, i - (FAST_BLOCKS - 1), i)
        state = lax.fori_loop(first, last_single, lambda j, s: sweep(((j, False, every),), s), tuple(init))

        def fast(rows):
            blocks = tuple((i - b, b == 0, every) for b in range(FAST_BLOCKS - 2, -1, -1))
            return sweep(((i - (FAST_BLOCKS - 1), False, rows),) + blocks, state)

        whole = tot_v[0:1, LANE_WHOLE:LANE_WHOLE + 1].astype(jnp.int32)[0, 0] > 0
        state = lax.cond(
            many,
            lambda: lax.cond(whole, lambda: fast(every), lambda: fast(early)),
            lambda: sweep(((i, True, every),), state))
        for p in range(pp):
            dqz_ref[0, :, p * PAIR:(p + 1) * PAIR] = (state[p][2] * scale).astype(BF16)

        @pl.when(i == nb - 1)
        def _():
            dkv_ref[0] = dk_acc[...].astype(BF16)
            dkv_ref[1] = dv_acc[...].astype(BF16)

    def tile(j):
        return pl.BlockSpec((None, t, width), lambda g, i, j=j: (j, i, g))

    def full(j):
        return pl.BlockSpec((None, seq, width), lambda g, i, j=j: (j, 0, g))

    flat_tile = pl.BlockSpec((t, width), lambda g, i: (i, g))
    return pl.pallas_call(
        body, name="attn_bwd", grid=(cw // width, nb),
        in_specs=[tile(4), full(5), full(6), tile(7), pl.BlockSpec(tri.shape, lambda g, i: (0, 0, 0)),
                  flat_tile, flat_tile, flat_tile],
        out_specs=(pl.BlockSpec((2, t, width), lambda g, i: (0, i, g)),
                   pl.BlockSpec((2, seq, width), lambda g, i: (0, 0, g))),
        out_shape=(jax.ShapeDtypeStruct((2, seq, cw), BF16), jax.ShapeDtypeStruct((2, seq, cw), BF16)),
        scratch_shapes=[pltpu.VMEM((seq, width), F32), pltpu.VMEM((seq, width), F32)],
        compiler_params=_params(48),
    )(proj, proj, proj, proj, tri, o, tot, d_mix_attn)


def _out_ln(mix_conv, mix_attn, x, target, gain, bias, w_out, tm):
    seq, d_model = x.shape
    cw = mix_conv.shape[1]
    inv_d = 1.0 / d_model

    def body(mc_ref, ma_ref, x_ref, t_ref, gain_ref, bias_ref, w_ref, dr_ref, dmc_ref, dma_ref, gwo_ref, small_ref):
        @pl.when(pl.program_id(0) == 0)
        def _():
            gwo_ref[...] = jnp.zeros_like(gwo_ref)
            small_ref[...] = jnp.zeros_like(small_ref)

        mix = jnp.concatenate([mc_ref[...], ma_ref[...]], axis=1)
        w = w_ref[...]
        r = ALPHA * x_ref[...] + _dot(mix, w)
        mu = jnp.sum(r, axis=1, keepdims=True) * inv_d
        xc = r - mu
        var = jnp.sum(xc * xc, axis=1, keepdims=True) * inv_d
        rstd = lax.rsqrt(var + LN_EPS)
        xhat = xc * rstd
        gain_v = gain_ref[...]
        err = xhat * gain_v + bias_ref[...] - t_ref[...]
        row_loss = jnp.sum(err * err, axis=1, keepdims=True)
        loss = (0.5 * inv_d) * jnp.sum(row_loss, axis=0, keepdims=True)
        dy = err * inv_d
        small_ref[ROW_GAIN:ROW_GAIN + 1, :] += jnp.sum(dy * xhat, axis=0, keepdims=True)
        small_ref[ROW_BIAS:ROW_BIAS + 1, :] += jnp.sum(dy, axis=0, keepdims=True)
        small_ref[ROW_LOSS:ROW_LOSS + 1, :] += jnp.broadcast_to(loss, (1, d_model))
        dxhat = dy * gain_v
        m1 = jnp.sum(dxhat, axis=1, keepdims=True) * inv_d
        m2 = jnp.sum(dxhat * xhat, axis=1, keepdims=True) * inv_d
        dr = rstd * (dxhat - m1 - xhat * m2)
        dr_ref[...] = dr
        drb = dr.astype(BF16)
        dmix = _dot(drb, w, NT)
        dmc_ref[...] = dmix[:, :cw].astype(BF16)
        dma_ref[...] = dmix[:, cw:].astype(BF16)
        gwo_ref[...] += _dot(mix, drb, TN)

    def rows(width):
        return pl.BlockSpec((tm, width), lambda i: (i, 0))

    def whole(shape):
        return pl.BlockSpec(shape, lambda i: (0, 0))

    return pl.pallas_call(
        body, name="out_ln", grid=(seq // tm,),
        in_specs=[rows(cw), rows(cw), rows(d_model), rows(d_model), whole((1, d_model)), whole((1, d_model)),
                  whole((d_model, d_model))],
        out_specs=(rows(d_model), rows(cw), rows(cw), whole((d_model, d_model)), whole((SUBLANES, d_model))),
        out_shape=(jax.ShapeDtypeStruct((seq, d_model), F32), jax.ShapeDtypeStruct((seq, cw), BF16),
                   jax.ShapeDtypeStruct((seq, cw), BF16), jax.ShapeDtypeStruct((d_model, d_model), F32),
                   jax.ShapeDtypeStruct((SUBLANES, d_model), F32)),
        compiler_params=_params(48),
    )(mix_conv, mix_attn, x, target, gain, bias, w_out)


_DP_OF_GROUP = ((0, 0), (0, 1), (0, 2), (0, 3), (1, 0), (2, 0), (2, 1), (1, 1))


def _grad_w_reduce(xt, dp_parts, dr, win_all, gwo, small, rows_out, row_chunk, tm):
    d_model, seq = xt.shape
    nch, _, cw = win_all.shape
    gx_tiles = seq // tm
    assert nch * tm == seq and nch * cw == seq, "needs S == 8 * tm == 8 * CW"

    def body(xt_hbm, dpa, dpb, dpc, dr_hbm, win_hbm, gwo_ref, small_ref,
             gx_hbm, g_in_o, g_out_o, small_o,
             xt_v, dp_buf, acc, got_in, send_in, recv_in, own_out, got_out, send_out, recv_out, small_all,
             dr_buf, gx_buf,
             xt_sem, dp_sems, loc_sems, d2d_send, d2d_recv, ici_send, ici_recv, sm_send, sm_recv,
             dr_sems, dpx_sems, w_sems, gx_sem):
        x, y, c = _mesh_pos()
        me = 4 * x + 2 * y + c
        sibling = (x, y, 1 - c)
        chips = [(1 - x, 1 - y), (1 - x, y), (x, 1 - y)]
        owners = [(*chip, cc) for chip in chips for cc in (1 - c, c)] + [sibling, (x, y, c)]
        group_of = [4 * o[0] + 2 * o[1] + o[2] for o in owners]
        dp_parts_ = (dpa, dpb, dpc)
        dp_groups = [dp_parts_[arr].at[idx] for arr, idx in _DP_OF_GROUP]

        xt_copy = pltpu.make_async_copy(xt_hbm, xt_v, xt_sem)
        xt_copy.start()

        def dp_start(step):
            for k in range(N_DEV):
                @pl.when(group_of[step] == k)
                def _(k=k):
                    pltpu.make_async_copy(dp_groups[k], dp_buf.at[step % 2], dp_sems.at[step % 2]).start()

        def dp_wait(step):
            pltpu.make_async_copy(dp_groups[0], dp_buf.at[step % 2], dp_sems.at[step % 2]).wait()

        dp_start(0)

        small_all[me] = small_ref[...]
        for d in range(N_DEV):
            @pl.when(d != me)
            def _(d=d):
                pltpu.make_async_remote_copy(
                    src_ref=small_ref, dst_ref=small_all.at[me], send_sem=sm_send.at[d], recv_sem=sm_recv.at[me],
                    device_id=(d // 4, (d // 2) % 2, d % 2), device_id_type=MESH).start()

        def block_out(k):
            return gwo_ref.at[pl.ds(k * rows_out, rows_out), :]

        for k in range(N_DEV):
            s = k // 2

            @pl.when(k % 2 != c)
            def _(k=k, s=s):
                pltpu.make_async_remote_copy(
                    src_ref=block_out(k), dst_ref=got_out.at[s], send_sem=d2d_send.at[1, s],
                    recv_sem=d2d_recv.at[1, s], device_id=sibling, device_id_type=MESH).start()

            @pl.when(k % 2 == c)
            def _(k=k, s=s):
                pltpu.make_async_copy(block_out(k), own_out.at[s], loc_sems.at[s]).start()

        VIA_X, VIA_Y, X_UP, X_LOW, Y_UP, Y_LOW = range(6)
        x_dev, y_dev = (1 - x, y, c), (x, 1 - y, c)
        ici_place = {VIA_X: (0, 0, x_dev), VIA_Y: (0, 1, y_dev), X_UP: (1, 0, x_dev), X_LOW: (1, 1, x_dev),
                     Y_UP: (2, 0, y_dev), Y_LOW: (2, 1, y_dev)}

        def ici_copy(a, k):
            send, recv = ((send_in, recv_in), (send_out, recv_out))[a]
            half = (d_model, rows_out)[a] // 2
            slot, part, to = ici_place[k]
            rows = pl.ds(part * half, half)
            return pltpu.make_async_remote_copy(
                src_ref=send.at[slot, rows, :], dst_ref=recv.at[slot, rows, :], send_sem=ici_send.at[a, k],
                recv_sem=ici_recv.at[a, k], device_id=to, device_id_type=MESH)

        def by_rows(first, n_rows, fn):
            step = min(row_chunk, n_rows)

            def rows_body(r, _):
                fn(pl.ds(pl.multiple_of(first + r * step, step), step))
                return 0

            lax.fori_loop(0, n_rows // step, rows_body, 0)

        def send_chip_sum(a, j, chip_sum):
            send, recv = ((send_in, recv_in), (send_out, recv_out))[a]
            n_rows = (d_model, rows_out)[a]
            half = n_rows // 2

            def plain(rows):
                send[j, rows, :] = chip_sum(rows).astype(BF16)

            if j == 0:
                by_rows(0, n_rows, plain)
                ici_copy(a, VIA_X).start()
                ici_copy(a, VIA_Y).start()
                return
            free, bound, passed = ((0, X_UP), (1, X_LOW), VIA_Y) if j == 1 else ((1, Y_LOW), (0, Y_UP), VIA_X)
            by_rows(free[0] * half, half, plain)
            ici_copy(a, free[1]).start()
            ici_copy(a, passed).wait_recv()

            def with_passed(rows):
                send[j, rows, :] = (chip_sum(rows) + recv[0, rows, :].astype(F32)).astype(BF16)

            by_rows(bound[0] * half, half, with_passed)
            ici_copy(a, bound[1]).start()

        for s in range(4):
            pltpu.make_async_copy(own_out.at[s], own_out.at[s], loc_sems.at[s]).wait()
            pltpu.make_async_remote_copy(
                src_ref=got_out.at[s], dst_ref=got_out.at[s], send_sem=d2d_send.at[1, s], recv_sem=d2d_recv.at[1, s],
                device_id=sibling, device_id_type=MESH).wait()
        for j, chip in enumerate(chips):
            s = 2 * chip[0] + chip[1]
            send_chip_sum(1, j, lambda rows, s=s: own_out[s, rows, :] + got_out[s, rows, :])

        xt_copy.wait()

        def d2d_copy(slot, pair):
            return pltpu.make_async_remote_copy(
                src_ref=acc.at[slot], dst_ref=got_in.at[pair], send_sem=d2d_send.at[0, pair],
                recv_sem=d2d_recv.at[0, pair], device_id=sibling, device_id_type=MESH)

        for step in range(N_DEV):
            slot, pair = step % 2, step // 2
            if step % 2 == 0 and step >= 2:
                d2d_copy(slot, pair - 1).wait_send()

            dp_wait(step)
            if step + 1 < N_DEV:
                dp_start(step + 1)
            acc[slot] = _dot(xt_v[...], dp_buf[step % 2])

            if step % 2 == 0:
                d2d_copy(slot, pair).start()
            else:
                d2d_copy(slot, pair).wait_recv()
                if step < N_DEV - 1:
                    send_chip_sum(0, pair, lambda rows, slot=slot, pair=pair: acc[slot, rows, :] + got_in[pair, rows, :])

        def w_copy(k):
            return pltpu.make_async_copy(win_hbm.at[k], xt_v.at[:, pl.ds(k * cw, cw)], w_sems.at[k])

        for k in range(N_DEV):
            w_copy(k).start()

        def gx_in(tile, buf):
            rows = pl.ds(tile * tm, tm)
            copies = [pltpu.make_async_copy(dr_hbm.at[rows, :], dr_buf.at[buf], dr_sems.at[buf])]
            copies += [pltpu.make_async_copy(dp_groups[k].at[rows, :], dp_buf.at[buf, pl.ds(k * tm, tm), :],
                                             dpx_sems.at[buf, k]) for k in range(N_DEV)]
            return copies

        def gx_out(tile):
            return pltpu.make_async_copy(gx_buf, gx_hbm.at[pl.ds(tile * tm, tm), :], gx_sem)

        for cp in gx_in(0, 0):
            cp.start()
        for k in range(N_DEV):
            w_copy(k).wait()

        def gx_body(tile, _):
            buf = tile % 2
            for cp in gx_in(tile, buf):
                cp.wait()

            @pl.when(tile + 1 < gx_tiles)
            def _():
                for cp in gx_in(tile + 1, 1 - buf):
                    cp.start()

            val = ALPHA * dr_buf[buf]
            for k in range(N_DEV):
                val = val + _dot(dp_buf[buf, k * tm:(k + 1) * tm, :], xt_v[:, k * cw:(k + 1) * cw], NT)

            @pl.when(tile > 0)
            def _():
                gx_out(tile - 1).wait()

            gx_buf[...] = val
            gx_out(tile).start()
            return 0

        lax.fori_loop(0, gx_tiles, gx_body, 0)

        for d in range(N_DEV):
            @pl.when(d != me)
            def _(d=d):
                pltpu.make_async_remote_copy(
                    src_ref=small_ref, dst_ref=small_all.at[d], send_sem=sm_send.at[d], recv_sem=sm_recv.at[d],
                    device_id=(d // 4, (d // 2) % 2, d % 2), device_id_type=MESH).wait()
        total = small_all[0]
        for d in range(1, N_DEV):
            total = total + small_all[d]
        small_o[...] = total

        mine = 2 * x + y
        last = (N_DEV - 1) % 2

        def finish(a, n_rows, chip_sum, g_o):
            recv = (recv_in, recv_out)[a]
            for k in (X_UP, X_LOW, Y_UP, Y_LOW):
                ici_copy(a, k).wait_recv()
            for k in ici_place:
                ici_copy(a, k).wait_send()

            def total_rows(rows):
                g_o[rows, :] = chip_sum(rows) + recv[1, rows, :].astype(F32) + recv[2, rows, :].astype(F32)

            by_rows(0, n_rows, total_rows)

        finish(1, rows_out, lambda rows: own_out[mine, rows, :] + got_out[mine, rows, :], g_out_o)
        finish(0, d_model, lambda rows: acc[last, rows, :] + got_in[N_DEV // 2 - 1, rows, :], g_in_o)
        d2d_copy(0, N_DEV // 2 - 1).wait_send()
        gx_out(0).wait()

    vmem = pl.BlockSpec(memory_space=pltpu.VMEM)
    hbm = pl.BlockSpec(memory_space=pl.ANY)
    return pl.pallas_call(
        body, name="grad_w_reduce",
        in_specs=[hbm] * 7 + [vmem],
        out_specs=(hbm, vmem, vmem, vmem),
        out_shape=(jax.ShapeDtypeStruct((seq, d_model), F32), jax.ShapeDtypeStruct((d_model, cw), F32),
                   jax.ShapeDtypeStruct((rows_out, d_model), F32), jax.ShapeDtypeStruct(small.shape, F32)),
        scratch_shapes=[
            pltpu.VMEM((d_model, seq), BF16), pltpu.VMEM((2, seq, cw), BF16), pltpu.VMEM((2, d_model, cw), F32),
            pltpu.VMEM((4, d_model, cw), F32), pltpu.VMEM((3, d_model, cw), BF16), pltpu.VMEM((3, d_model, cw), BF16),
            pltpu.VMEM((4, rows_out, d_model), F32), pltpu.VMEM((4, rows_out, d_model), F32),
            pltpu.VMEM((3, rows_out, d_model), BF16), pltpu.VMEM((3, rows_out, d_model), BF16),
            pltpu.VMEM((N_DEV,) + small.shape, F32),
            pltpu.VMEM((2, tm, d_model), F32), pltpu.VMEM((tm, d_model), F32),
            pltpu.SemaphoreType.DMA, pltpu.SemaphoreType.DMA((2,)), pltpu.SemaphoreType.DMA((4,)),
            pltpu.SemaphoreType.DMA((2, 4)), pltpu.SemaphoreType.DMA((2, 4)),
            pltpu.SemaphoreType.DMA((2, 6)), pltpu.SemaphoreType.DMA((2, 6)),
            pltpu.SemaphoreType.DMA((N_DEV,)), pltpu.SemaphoreType.DMA((N_DEV,)),
            pltpu.SemaphoreType.DMA((2,)), pltpu.SemaphoreType.DMA((2, N_DEV)), pltpu.SemaphoreType.DMA((N_DEV,)),
            pltpu.SemaphoreType.DMA,
        ],
        compiler_params=_params(56),
    )(xt, *dp_parts, dr, win_all, gwo, small)


def _adamw_update(g, w, m, v, rows):
    n_rows, width = w.shape
    rows = min(rows, n_rows)

    def body(g_ref, w_ref, m_ref, v_ref, d_o, nm_o, nv_o):
        d_o[...], nm_o[...], nv_o[...] = _adamw(w_ref[...], g_ref[...], m_ref[...], v_ref[...])

    tile = pl.BlockSpec((rows, width), lambda i: (i, 0))
    shape = jax.ShapeDtypeStruct(w.shape, F32)
    return pl.pallas_call(
        body, name="adamw_update", grid=(n_rows // rows,), in_specs=[tile] * 4, out_specs=(tile,) * 3,
        out_shape=(shape,) * 3,
    )(g, w, m, v)


def _small_update(grads, weights, ms, vs):
    n = len(grads)

    def body(*refs):
        g_refs, w_refs, m_refs, v_refs = (refs[i * n:(i + 1) * n] for i in range(4))
        outs = refs[4 * n:]
        for i in range(n):
            delta, nm, nv = _adamw(w_refs[i][...], g_refs[i][...], m_refs[i][...], v_refs[i][...])
            outs[3 * i][...] = delta
            outs[3 * i + 1][...] = nm
            outs[3 * i + 2][...] = nv

    vmem = pl.BlockSpec(memory_space=pltpu.VMEM)
    out_shape = []
    for w in weights:
        out_shape += [jax.ShapeDtypeStruct(w.shape, F32)] * 3
    return pl.pallas_call(
        body, name="small_update", in_specs=[vmem] * (4 * n), out_specs=(vmem,) * (3 * n), out_shape=tuple(out_shape),
    )(*grads, *weights, *ms, *vs)


def _tile_sizes(seq):
    return dict(tm=seq // N_DEV, t_ln=min(512, seq), t_attn=min(128, seq), rc=min(256, seq), pairs=4)


def kernel(x, w_in, conv_w, w_out, ln_gain, ln_bias, loss_target, m_w_in, m_conv_w, m_w_out, m_ln_gain, m_ln_bias,
           v_w_in, v_conv_w, v_w_out, v_ln_gain, v_ln_bias):
    assert x.shape[0] == 1 and w_in.shape[0] == 1, "one sequence per device, depth 1"
    _, seq, d_model = x.shape
    cw = w_in.shape[2]
    conv_k, conv_cols = conv_w.shape[1], conv_w.shape[2]
    rows_out = w_out.shape[1]
    assert cw == d_model // 2 and cw % PAIR == 0 and conv_cols * N_DEV == cw and rows_out * N_DEV == d_model
    ts = _tile_sizes(seq)

    x2 = x.reshape(seq, d_model)
    target = loss_target.reshape(seq, d_model)
    me = 4 * lax.axis_index("x") + 2 * lax.axis_index("y") + lax.axis_index("c")

    conv_pad = jnp.pad(conv_w[0], ((0, SUBLANES - conv_k), (0, PAIR - conv_cols)))
    proj, xt, win_all, wout_all, conv_all = _gather_proj(x2, w_in[0], w_out[0], conv_pad, ts["tm"])
    w_out_full = wout_all.reshape(d_model, d_model)
    conv_full = conv_all[:, :conv_k, :conv_cols].transpose(1, 0, 2).reshape(conv_k, cw)
    conv_full = jnp.pad(conv_full, ((0, SUBLANES - conv_k), (0, 0)))

    mix_conv = _conv_fwd(proj, conv_full, ts["rc"])
    pairs = min(ts["pairs"], cw // PAIR)
    tri = _triangles(ts["t_attn"])
    o, mix_attn, tot = _attn_fwd(proj, tri, ts["t_attn"], pairs)
    dr, d_mix_conv, d_mix_attn, gwo, small = _out_ln(mix_conv, mix_attn, x2, target, ln_gain, ln_bias, w_out_full,
                                                     ts["t_ln"])
    dp_conv, d_taps = _conv_bwd(proj, conv_full, d_mix_conv, ts["rc"])
    dp_qz, dp_kv = _attn_bwd(proj, tri, o, tot, d_mix_attn, ts["t_attn"], pairs)
    small = small.at[ROW_CONV:ROW_CONV + conv_k, :cw].set(d_taps[:conv_k])
    grad_x, g_in, g_out, small_sum = _grad_w_reduce(
        xt, (dp_conv, dp_qz, dp_kv), dr, win_all, gwo, small, rows_out, 128, ts["tm"])
    d_in, nm_in, nv_in = _adamw_update(g_in, w_in[0], m_w_in[0], v_w_in[0], 256)
    d_out, nm_out, nv_out = _adamw_update(g_out, w_out[0], m_w_out[0], v_w_out[0], 256)

    loss = small_sum[ROW_LOSS, 0]
    g_gain = small_sum[ROW_GAIN:ROW_GAIN + 1]
    g_bias = small_sum[ROW_BIAS:ROW_BIAS + 1]
    g_conv = lax.dynamic_slice(small_sum, (ROW_CONV, me * conv_cols), (conv_k, conv_cols))
    upd = _small_update((g_conv, g_gain, g_bias), (conv_w[0], ln_gain, ln_bias),
                        (m_conv_w[0], m_ln_gain, m_ln_bias), (v_conv_w[0], v_ln_gain, v_ln_bias))
    d_conv, nm_conv, nv_conv, d_gain, nm_gain, nv_gain, d_bias, nm_bias, nv_bias = upd

    lead = lambda a: a[None]
    return (loss, grad_x.reshape(1, seq, d_model), lead(g_in), lead(g_conv), lead(g_out), g_gain, g_bias,
            lead(d_in), lead(d_conv), lead(d_out), d_gain, d_bias,
            lead(nm_in), lead(nm_conv), lead(nm_out), nm_gain, nm_bias,
            lead(nv_in), lead(nv_conv), lead(nv_out), nv_gain, nv_bias)
```

```python
import functools

import jax
import jax.numpy as jnp
from jax import lax
from jax.experimental import pallas as pl
from jax.experimental.pallas import tpu as pltpu

F32 = jnp.float32
BF16 = jnp.bfloat16
MESH = pl.DeviceIdType.MESH

N_DEV = 8
HEAD_DIM = 64
PAIR = 128
SUBLANES = 8
LN_EPS = 1e-5
ALPHA = 2.0 ** 0.25
ADAM_LR, ADAM_B1, ADAM_B2, ADAM_EPS, ADAM_WD, ADAM_STEP = 0.001, 0.9, 0.999, 1e-08, 0.01, 10

ROW_GAIN, ROW_BIAS, ROW_CONV, ROW_LOSS = 0, 1, 2, 5

NT = (((1,), (1,)), ((), ()))
TN = (((0,), (0,)), ((), ()))


V7X_VMEM_BYTES = 64 * 1024 * 1024


def _params(vmem_mib):
    assert vmem_mib * 1024 * 1024 < V7X_VMEM_BYTES
    return pltpu.CompilerParams(vmem_limit_bytes=vmem_mib * 1024 * 1024)


def _dot(a, b, dims=None):
    if dims is None:
        return jnp.dot(a, b, preferred_element_type=F32)
    return lax.dot_general(a, b, dims, preferred_element_type=F32)


def _sigmoid(z):
    return 1.0 / (1.0 + jnp.exp(-z))


def _mesh_pos():
    return lax.axis_index("x"), lax.axis_index("y"), lax.axis_index("c")


def _adamw(w, g, m, v):
    nm = ADAM_B1 * m + (1.0 - ADAM_B1) * g
    nv = ADAM_B2 * v + (1.0 - ADAM_B2) * (g * g)
    m_hat = nm * (1.0 / (1.0 - ADAM_B1 ** ADAM_STEP))
    v_hat = nv * (1.0 / (1.0 - ADAM_B2 ** ADAM_STEP))
    delta = -ADAM_LR * (m_hat / (jnp.sqrt(v_hat) + ADAM_EPS) + ADAM_WD * w)
    return delta, nm, nv


def _gather_proj(x, w_in_s, w_out_s, conv_s, tm):
    seq, d_model = x.shape
    cw = w_in_s.shape[1]
    rows_out = w_out_s.shape[0]
    n_tiles = seq // tm
    half = d_model // 2
    SIB, X_UP, X_LOW, Y_UP, Y_LOW, VIA_Y, VIA_X, ON_X, ON_Y, ON_DIAG = range(10)

    def body(x_hbm, win_ref, wout_ref, conv_ref, proj_hbm, xt_hbm, win_all, wout_all, conv_all,
             xb, x_stage, o_stage, xt_stage, x_sems, o_sems, xt_sems, w_send, w_recv, send_sems, recv_sems):
        x, y, c = _mesh_pos()
        me = (x, y, c)
        sibling = (x, y, 1 - c)
        x_nbr, y_nbr, diag = (1 - x, y), (x, 1 - y), (1 - x, 1 - y)
        chips = [x_nbr, y_nbr, diag]
        small = (wout_all, conv_all)

        def slot(pos):
            return 4 * pos[0] + 2 * pos[1] + pos[2]

        def x_copy(tile, buf):
            return pltpu.make_async_copy(x_hbm.at[pl.ds(tile * tm, tm), :], x_stage.at[buf], x_sems.at[buf])

        x_copy(0, 0).start()
        win_all[slot(me)] = win_ref[...].astype(BF16)
        wout_all[slot(me)] = wout_ref[...].astype(BF16)
        conv_all[slot(me)] = conv_ref[...]

        def w_copy(k, block, part, to):
            ref = win_all.at[slot(block)]
            if part is not None:
                ref = ref.at[pl.ds(part * half, half), :]
            return pltpu.make_async_remote_copy(
                src_ref=ref, dst_ref=ref, send_sem=w_send.at[k], recv_sem=w_recv.at[k],
                device_id=to, device_id_type=MESH)

        def copy(a, k, block, to):
            ref = small[a].at[slot(block)]
            return pltpu.make_async_remote_copy(
                src_ref=ref, dst_ref=ref, send_sem=send_sems.at[a, k], recv_sem=recv_sems.at[a, k],
                device_id=to, device_id_type=MESH)

        sends = [w_copy(SIB, me, None, sibling),
                 w_copy(X_UP, me, 0, (*x_nbr, c)), w_copy(Y_LOW, me, 1, (*y_nbr, c)),
                 w_copy(X_LOW, me, 1, (*x_nbr, c)), w_copy(Y_UP, me, 0, (*y_nbr, c))]
        for a in range(len(small)):
            sends.append(copy(a, 0, me, sibling))
            sends += [copy(a, 1 + j, me, (*chip, c)) for j, chip in enumerate(chips)]
        for cp in sends:
            cp.start()

        def o_copy(group, tile, buf):
            return pltpu.make_async_copy(o_stage.at[buf], proj_hbm.at[group, pl.ds(tile * tm, tm), :], o_sems.at[buf])

        def xt_copy(tile, buf):
            return pltpu.make_async_copy(xt_stage.at[buf], xt_hbm.at[:, pl.ds(tile * tm, tm)], xt_sems.at[buf])

        def project(order, group, first_pass):
            def tile_body(tile, _):
                if first_pass:
                    buf = tile % 2
                    x_copy(tile, buf).wait()

                    @pl.when(tile + 1 < n_tiles)
                    def _():
                        x_copy(tile + 1, 1 - buf).start()

                    xv = x_stage[buf]
                    xb[tile] = xv.astype(BF16)

                    @pl.when(tile >= 2)
                    def _():
                        xt_copy(tile - 2, buf).wait()

                    xt_stage[buf] = xv.T.astype(BF16)
                    xt_copy(tile, buf).start()
                count = order * n_tiles + tile
                obuf = count % 2

                @pl.when(count >= 2)
                def _():
                    o_copy(group, tile, obuf).wait()

                o_stage[obuf] = _dot(xb[tile], win_all[group]).astype(BF16)
                o_copy(group, tile, obuf).start()
                return 0

            lax.fori_loop(0, n_tiles, tile_body, 0)

        def start(cp):
            cp.start()
            sends.append(cp)

        def small_pass_on(j):
            for a in range(len(small)):
                copy(a, 1 + j, (*chips[j], c), me).wait_recv()
                start(copy(a, 4 + j, (*chips[j], c), sibling))

        def small_from_sibling(k):
            for a in range(len(small)):
                copy(a, k, sibling, me).wait_recv()

        project(0, slot(me), True)
        w_copy(SIB, sibling, None, me).wait_recv()
        small_from_sibling(0)
        project(1, slot(sibling), False)
        w_copy(X_UP, (*x_nbr, c), 0, me).wait_recv()
        start(w_copy(VIA_Y, (*x_nbr, c), 0, (*y_nbr, c)))
        w_copy(Y_LOW, (*y_nbr, c), 1, me).wait_recv()
        start(w_copy(VIA_X, (*y_nbr, c), 1, (*x_nbr, c)))
        w_copy(X_LOW, (*x_nbr, c), 1, me).wait_recv()
        start(w_copy(ON_X, (*x_nbr, c), None, sibling))
        small_pass_on(0)
        project(2, slot((*x_nbr, c)), False)
        w_copy(Y_UP, (*y_nbr, c), 0, me).wait_recv()
        start(w_copy(ON_Y, (*y_nbr, c), None, sibling))
        small_pass_on(1)
        project(3, slot((*y_nbr, c)), False)
        w_copy(ON_X, (*x_nbr, 1 - c), None, me).wait_recv()
        small_from_sibling(4)
        project(4, slot((*x_nbr, 1 - c)), False)
        w_copy(ON_Y, (*y_nbr, 1 - c), None, me).wait_recv()
        small_from_sibling(5)
        project(5, slot((*y_nbr, 1 - c)), False)
        w_copy(VIA_Y, (*diag, c), 0, me).wait_recv()
        w_copy(VIA_X, (*diag, c), 1, me).wait_recv()
        start(w_copy(ON_DIAG, (*diag, c), None, sibling))
        small_pass_on(2)
        project(6, slot((*diag, c)), False)
        w_copy(ON_DIAG, (*diag, 1 - c), None, me).wait_recv()
        small_from_sibling(6)
        project(7, slot((*diag, 1 - c)), False)

        for buf in range(2):
            o_copy(0, 0, buf).wait()
        for buf in range(min(2, n_tiles)):
            xt_copy(0, buf).wait()
        for cp in sends:
            cp.wait_send()

    vmem = pl.BlockSpec(memory_space=pltpu.VMEM)
    hbm = pl.BlockSpec(memory_space=pl.ANY)
    return pl.pallas_call(
        body, name="gather_proj",
        out_shape=(jax.ShapeDtypeStruct((N_DEV, seq, cw), BF16),
                   jax.ShapeDtypeStruct((d_model, seq), BF16),
                   jax.ShapeDtypeStruct((N_DEV, d_model, cw), BF16),
                   jax.ShapeDtypeStruct((N_DEV, rows_out, d_model), BF16),
                   jax.ShapeDtypeStruct((N_DEV,) + conv_s.shape, F32)),
        in_specs=[hbm, vmem, vmem, vmem], out_specs=(hbm, hbm, vmem, vmem, vmem),
        scratch_shapes=[
            pltpu.VMEM((n_tiles, tm, d_model), BF16), pltpu.VMEM((2, tm, d_model), F32),
            pltpu.VMEM((2, tm, cw), BF16), pltpu.VMEM((2, d_model, tm), BF16),
            pltpu.SemaphoreType.DMA((2,)), pltpu.SemaphoreType.DMA((2,)), pltpu.SemaphoreType.DMA((2,)),
            pltpu.SemaphoreType.DMA((10,)), pltpu.SemaphoreType.DMA((10,)),
            pltpu.SemaphoreType.DMA((2, 7)), pltpu.SemaphoreType.DMA((2, 7))],
        compiler_params=_params(48),
    )(x, w_in_s, w_out_s, conv_s)


def _conv_taps(ext, w_ref, rc):
    u0 = ext[SUBLANES:SUBLANES + rc]
    u1 = pltpu.roll(ext, 1, 0)[SUBLANES:SUBLANES + rc]
    u2 = pltpu.roll(ext, 2, 0)[SUBLANES:SUBLANES + rc]
    return w_ref[2:3, :] * u0 + w_ref[1:2, :] * u1 + w_ref[0:1, :] * u2, u0, u1, u2


def _conv_fwd(proj, conv_full, rc):
    _, seq, cw = proj.shape

    def body(b_ref, c_ref, h_ref, z_ref, w_ref, o_ref, u_scr):
        u_scr[0:SUBLANES, :] = jnp.zeros((SUBLANES, PAIR), F32)

        def fill(r, _):
            base = pl.multiple_of(r * rc, rc)
            rows = pl.ds(base, rc)
            u_scr[pl.ds(base + SUBLANES, rc), :] = c_ref[rows, :].astype(F32) * h_ref[rows, :].astype(F32)
            return 0

        lax.fori_loop(0, seq // rc, fill, 0)

        def out(r, _):
            base = pl.multiple_of(r * rc, rc)
            rows = pl.ds(base, rc)
            ext = u_scr[pl.ds(base, rc + SUBLANES), :]
            y, _, _, _ = _conv_taps(ext, w_ref, rc)
            z = z_ref[rows, :].astype(F32)
            o_ref[rows, :] = (z * _sigmoid(z) * b_ref[rows, :].astype(F32) * y).astype(BF16)
            return 0

        lax.fori_loop(0, seq // rc, out, 0)

    def chunk(j):
        return pl.BlockSpec((None, seq, PAIR), lambda cb, j=j: (j, 0, cb))

    return pl.pallas_call(
        body, name="conv_fwd", grid=(cw // PAIR,),
        in_specs=[chunk(0), chunk(1), chunk(2), chunk(3), pl.BlockSpec((SUBLANES, PAIR), lambda cb: (0, cb))],
        out_specs=pl.BlockSpec((seq, PAIR), lambda cb: (0, cb)),
        out_shape=jax.ShapeDtypeStruct((seq, cw), BF16),
        scratch_shapes=[pltpu.VMEM((seq + SUBLANES, PAIR), F32)],
    )(proj, proj, proj, proj, conv_full)


def _conv_bwd(proj, conv_full, d_mix_conv, rc):
    _, seq, cw = proj.shape

    def body(b_ref, c_ref, h_ref, z_ref, w_ref, g_ref, dp_ref, dw_ref, u_scr, dy_scr):
        u_scr[0:SUBLANES, :] = jnp.zeros((SUBLANES, PAIR), F32)
        dy_scr[seq:seq + SUBLANES, :] = jnp.zeros((SUBLANES, PAIR), F32)

        def fill(r, _):
            base = pl.multiple_of(r * rc, rc)
            rows = pl.ds(base, rc)
            u_scr[pl.ds(base + SUBLANES, rc), :] = c_ref[rows, :].astype(F32) * h_ref[rows, :].astype(F32)
            return 0

        lax.fori_loop(0, seq // rc, fill, 0)

        def gate(r, acc):
            base = pl.multiple_of(r * rc, rc)
            rows = pl.ds(base, rc)
            ext = u_scr[pl.ds(base, rc + SUBLANES), :]
            y, u0, u1, u2 = _conv_taps(ext, w_ref, rc)
            z = z_ref[rows, :].astype(F32)
            b = b_ref[rows, :].astype(F32)
            g = g_ref[rows, :].astype(F32)
            sig = _sigmoid(z)
            dp_ref[3, rows, :] = (g * b * y * (sig * (1.0 + z * (1.0 - sig)))).astype(BF16)
            gs = g * (z * sig)
            dp_ref[0, rows, :] = (gs * y).astype(BF16)
            dy = gs * b
            dy_scr[rows, :] = dy
            a0, a1, a2 = acc
            return (a0 + jnp.sum(dy * u2, axis=0, keepdims=True),
                    a1 + jnp.sum(dy * u1, axis=0, keepdims=True),
                    a2 + jnp.sum(dy * u0, axis=0, keepdims=True))

        zero = jnp.zeros((1, PAIR), F32)
        a0, a1, a2 = lax.fori_loop(0, seq // rc, gate, (zero, zero, zero))
        dw_ref[...] = jnp.zeros((SUBLANES, PAIR), F32)
        dw_ref[0:1, :] = a0
        dw_ref[1:2, :] = a1
        dw_ref[2:3, :] = a2

        def back(r, _):
            base = pl.multiple_of(r * rc, rc)
            rows = pl.ds(base, rc)
            ext = dy_scr[pl.ds(base, rc + SUBLANES), :]
            n = rc + SUBLANES
            d0 = ext[0:rc]
            d1 = pltpu.roll(ext, n - 1, 0)[0:rc]
            d2 = pltpu.roll(ext, n - 2, 0)[0:rc]
            du = w_ref[2:3, :] * d0 + w_ref[1:2, :] * d1 + w_ref[0:1, :] * d2
            dp_ref[1, rows, :] = (du * h_ref[rows, :].astype(F32)).astype(BF16)
            dp_ref[2, rows, :] = (du * c_ref[rows, :].astype(F32)).astype(BF16)
            return 0

        lax.fori_loop(0, seq // rc, back, 0)

    def chunk(j):
        return pl.BlockSpec((None, seq, PAIR), lambda cb, j=j: (j, 0, cb))

    return pl.pallas_call(
        body, name="conv_bwd", grid=(cw // PAIR,),
        in_specs=[chunk(0), chunk(1), chunk(2), chunk(3), pl.BlockSpec((SUBLANES, PAIR), lambda cb: (0, cb)),
                  pl.BlockSpec((seq, PAIR), lambda cb: (0, cb))],
        out_specs=(pl.BlockSpec((4, seq, PAIR), lambda cb: (0, 0, cb)),
                   pl.BlockSpec((SUBLANES, PAIR), lambda cb: (0, cb))),
        out_shape=(jax.ShapeDtypeStruct((4, seq, cw), BF16), jax.ShapeDtypeStruct((SUBLANES, cw), F32)),
        scratch_shapes=[pltpu.VMEM((seq + SUBLANES, PAIR), F32), pltpu.VMEM((seq + SUBLANES, PAIR), F32)],
    )(proj, proj, proj, proj, conv_full, d_mix_conv)


SKIP_CARRY = 104.0
LOG2_E = 1.4426950408889634
LANE_TOT0, LANE_TOT1, LANE_FIRST, LANE_WHOLE = 0, 1, 2, 3
FAST_BLOCKS = 3
EARLY_ROWS = 32


def _triangles(t):
    row = lax.broadcasted_iota(jnp.int32, (2 * t, 2 * t), 0)
    col = lax.broadcasted_iota(jnp.int32, (2 * t, 2 * t), 1)
    same = (row < t) == (col < t)
    upper = jnp.logical_and(same, row > col).astype(BF16)
    lower = jnp.logical_and(same, row < col).astype(BF16)
    return jnp.stack([jnp.concatenate([upper, upper], axis=0), jnp.concatenate([lower, lower], axis=0)])


def _pair_masks(t):
    lane = lax.broadcasted_iota(jnp.int32, (t, PAIR), 1)
    qrow = lax.broadcasted_iota(jnp.int32, (t, 2 * t), 0)
    kcol = lax.broadcasted_iota(jnp.int32, (t, 2 * t), 1)
    strict = jnp.where(kcol < t, kcol, kcol - t) < qrow
    return lane, lane < HEAD_DIM, strict


def _by_head(x, head0):
    zero = jnp.zeros_like(x)
    return jnp.concatenate([jnp.where(head0, x, zero), jnp.where(head0, zero, x)], axis=0)


def _hi_lo(a):
    hi = a.astype(BF16)
    lo = (a - hi.astype(F32)).astype(BF16)
    return jnp.concatenate([hi, lo], axis=1)


def _softplus_parts(z, strict, masked):
    spu = jnp.maximum(z, 0.0) + jnp.log(1.0 + jnp.exp2(jnp.abs(z) * -LOG2_E))
    return z - spu, (jnp.where(strict, spu, 0.0) if masked else spu)


def _stacked_dot(parts, rhs):
    out = _dot(jnp.concatenate(parts, axis=0), rhs)
    ends = [0]
    for p in parts:
        ends.append(ends[-1] + p.shape[0])
    return [out[a:b] for a, b in zip(ends[:-1], ends[1:])]


def _splice(whole, rows, part):
    pieces = ([whole[:rows[0]]] if rows[0] > 0 else []) + [part]
    if rows[1] < whole.shape[0]:
        pieces.append(whole[rows[1]:])
    return part if len(pieces) == 1 else jnp.concatenate(pieces, axis=0)


def _attn_fwd(proj, tri, t, pp):
    _, seq, cw = proj.shape
    scale = HEAD_DIM ** -0.5
    width = pp * PAIR

    def body(q_ref, k_ref, v_ref, za_ref, tri_ref, o_ref, mix_ref, tot_ref):
        i = pl.program_id(1)
        lane, head0, strict = _pair_masks(t)
        upper = tri_ref[0]
        q = q_ref[...] * scale

        def sweep(blocks, state):
            staged = []
            for j, masked, rows in blocks:
                start = pl.multiple_of(j * t, t)
                kb = k_ref[pl.ds(start, t), :]
                vb = v_ref[pl.ds(start, t), :]
                for p in range(pp):
                    cols = slice(p * PAIR, (p + 1) * PAIR)
                    z = _dot(q[rows[0]:rows[1], cols], _by_head(kb[:, cols], head0), NT)
                    ls, sp = _softplus_parts(z, strict[rows[0]:rows[1]], masked)
                    staged.append((p, masked, rows, ls, sp, _by_head(vb[:, cols], head0)))
            afters = _stacked_dot([_hi_lo(sp) for _, _, _, _, sp, _ in staged], upper)
            state = list(state)
            for (p, masked, rows, ls, sp, v2), after in zip(staged, afters):
                (c0, c1), acc = state[p]
                part = slice(rows[0], rows[1])
                x = ls - after
                w = jnp.exp(jnp.concatenate([x[:, :t] - c0[part], x[:, t:] - c1[part]], axis=1))
                if masked:
                    w = jnp.where(strict[part], w, 0.0)
                c0 = _splice(c0, rows, c0[part] + (after[:, 0:1] + sp[:, 0:1]))
                c1 = _splice(c1, rows, c1[part] + (after[:, t:t + 1] + sp[:, t:t + 1]))
                state[p] = ((c0, c1), _splice(acc, rows, acc[part] + _dot(w.astype(BF16), v2)))
            return tuple(state)

        def unfinished(state, rows):
            m = state[0][0][0]
            for p in range(pp):
                m = jnp.minimum(m, jnp.minimum(state[p][0][0], state[p][0][1]))
            return jnp.min(m[rows[0]:rows[1]]) < SKIP_CARRY

        every = (0, t)
        early, late = (0, min(EARLY_ROWS, t)), (min(EARLY_ROWS, t), t)

        def step(js):
            state = sweep(((js[0], False, every),), js[1])
            return js[0] - 1, state, unfinished(state, every)

        def fast():
            blocks = tuple((i - b, b == 0, every) for b in range(FAST_BLOCKS - 1))
            state = sweep(blocks + ((i - (FAST_BLOCKS - 1), False, early),), init)
            if late[0] == late[1]:
                return state, jnp.bool_(True)
            whole = unfinished(state, late)
            state = lax.cond(whole, lambda: sweep(((i - (FAST_BLOCKS - 1), False, late),), state), lambda: state)
            return state, whole

        zcol = jnp.zeros((t, 1), F32)
        init = tuple(((zcol, zcol), jnp.zeros((t, PAIR), F32)) for _ in range(pp))
        many = i >= FAST_BLOCKS - 1
        state, whole = lax.cond(many, fast, lambda: (sweep(((i, True, every),), init), jnp.bool_(True)))
        j_end, state, _ = lax.while_loop(
            lambda js: jnp.logical_and(js[0] >= 0, js[2]), step,
            (jnp.where(many, i - FAST_BLOCKS, i - 1), state, unfinished(state, every)))
        first = (j_end + 1).astype(F32)
        notes = jnp.where(lane == LANE_FIRST, first, whole.astype(F32))
        za = za_ref[...].astype(F32)
        for p in range(pp):
            (c0, c1), acc = state[p]
            cols = slice(p * PAIR, (p + 1) * PAIR)
            zp = za[:, cols]
            o_ref[:, cols] = acc.astype(BF16)
            mix_ref[:, cols] = (zp * _sigmoid(zp) * acc).astype(BF16)
            tot_ref[:, cols] = jnp.where(lane == LANE_TOT0, c0, jnp.where(lane == LANE_TOT1, c1, notes))

    def tile(j):
        return pl.BlockSpec((None, t, width), lambda g, i, j=j: (j, i, g))

    def full(j):
        return pl.BlockSpec((None, seq, width), lambda g, i, j=j: (j, 0, g))

    out_tile = pl.BlockSpec((t, width), lambda g, i: (i, g))
    return pl.pallas_call(
        body, name="attn_fwd", grid=(cw // width, seq // t),
        in_specs=[tile(4), full(5), full(6), tile(7), pl.BlockSpec(tri.shape, lambda g, i: (0, 0, 0))],
        out_specs=(out_tile, out_tile, out_tile),
        out_shape=(jax.ShapeDtypeStruct((seq, cw), BF16), jax.ShapeDtypeStruct((seq, cw), BF16),
                   jax.ShapeDtypeStruct((seq, cw), F32)),
    )(proj, proj, proj, proj, tri)


def _attn_bwd(proj, tri, o, tot, d_mix_attn, t, pp):
    _, seq, cw = proj.shape
    nb = seq // t
    scale = HEAD_DIM ** -0.5
    width = pp * PAIR

    def body(q_ref, k_ref, v_ref, za_ref, tri_ref, o_ref, tot_ref, g_ref, dqz_ref, dkv_ref, dk_acc, dv_acc):
        i = pl.program_id(1)

        @pl.when(i == 0)
        def _():
            dk_acc[...] = jnp.zeros_like(dk_acc)
            dv_acc[...] = jnp.zeros_like(dv_acc)

        _, head0, strict = _pair_masks(t)
        upper, lower = tri_ref[0], tri_ref[1, 0:2 * t, :]
        za = za_ref[...].astype(F32)
        g = g_ref[...].astype(F32)
        sig = _sigmoid(za)
        dqz_ref[1] = (g * o_ref[...].astype(F32) * (sig * (1.0 + za * (1.0 - sig)))).astype(BF16)
        do = (g * (za * sig)).astype(BF16)
        q = q_ref[...] * scale
        tot_v = tot_ref[...]
        q2, do2, init = [], [], []
        zcol = jnp.zeros((t, 1), F32)
        for p in range(pp):
            cols = slice(p * PAIR, (p + 1) * PAIR)
            q2.append(_by_head(q[:, cols], head0))
            do2.append(_by_head(do[:, cols], head0))
            tp = tot_v[:, cols]
            init.append(((tp[:, LANE_TOT0:LANE_TOT0 + 1], tp[:, LANE_TOT1:LANE_TOT1 + 1]), (zcol, zcol),
                         jnp.zeros((t, PAIR), F32)))
        first = jnp.clip(tot_v[0:1, LANE_FIRST:LANE_FIRST + 1], 0.0, i.astype(F32)).astype(jnp.int32)[0, 0]

        def sweep(blocks, state):
            staged = []
            for j, masked, rows in blocks:
                start = pl.multiple_of(j * t, t)
                kb = k_ref[pl.ds(start, t), :]
                vb = v_ref[pl.ds(start, t), :]
                part = slice(rows[0], rows[1])
                for p in range(pp):
                    cols = slice(p * PAIR, (p + 1) * PAIR)
                    k2 = _by_head(kb[:, cols], head0)
                    z = _dot(q[part, cols], k2, NT)
                    ls, sp = _softplus_parts(z, strict[part], masked)
                    da = _dot(do[part, cols], _by_head(vb[:, cols], head0), NT)
                    staged.append((p, masked, rows, k2, ls, sp, da))
            afters = _stacked_dot([_hi_lo(sp) for _, _, _, _, _, sp, _ in staged], upper)
            state = list(state)
            weights, ggs = [], []
            for (p, masked, rows, k2, ls, sp, da), after in zip(staged, afters):
                (s0, s1), befores, dq = state[p]
                part = slice(rows[0], rows[1])
                n0 = s0[part] - (after[:, 0:1] + sp[:, 0:1])
                n1 = s1[part] - (after[:, t:t + 1] + sp[:, t:t + 1])
                x = ls - after
                a = jnp.exp(jnp.concatenate([x[:, :t] - n0, x[:, t:] - n1], axis=1))
                if masked:
                    a = jnp.where(strict[part], a, 0.0)
                state[p] = ((_splice(s0, rows, n0), _splice(s1, rows, n1)), befores, dq)
                weights.append(a.astype(BF16))
                ggs.append(a * da)
            pres = _stacked_dot([gg.astype(BF16) for gg in ggs], lower)
            dzs = []
            for (p, masked, rows, k2, ls, sp, da), gg, pre in zip(staged, ggs, pres):
                rests, (b0, b1), dq = state[p]
                part = slice(rows[0], rows[1])
                y = gg + pre
                dz = gg - jnp.exp(ls) * jnp.concatenate([y[:, :t] + b0[part], y[:, t:] + b1[part]], axis=1)
                if masked:
                    dz = jnp.where(strict[part], dz, 0.0)
                dzb = dz.astype(BF16)
                dzs.append(dzb)
                state[p] = (rests, (_splice(b0, rows, b0[part] + y[:, t - 1:t]),
                                    _splice(b1, rows, b1[part] + y[:, 2 * t - 1:2 * t])),
                            _splice(dq, rows, dq[part] + _dot(dzb, k2)))
            first_row = pl.multiple_of(blocks[0][0] * t, t)
            n_rows = len(blocks) * t
            for p in range(pp):
                cols = slice(p * PAIR, (p + 1) * PAIR)

                def by_key(tiles):
                    out = []
                    for n, m in tiles:
                        rows = staged[n][2]
                        m = _splice(jnp.zeros((t, 2 * t), BF16), rows, m)
                        out.append(jnp.concatenate([m[:, :t], m[:, t:]], axis=0).T)
                    return jnp.concatenate(out, axis=0)

                mine = [n for n in range(len(staged)) if staged[n][0] == p]
                dk_acc[pl.ds(first_row, n_rows), cols] += _dot(by_key([(n, dzs[n]) for n in mine]), q2[p])
                dv_acc[pl.ds(first_row, n_rows), cols] += _dot(by_key([(n, weights[n]) for n in mine]), do2[p])
            return tuple(state)

        every = (0, t)
        early = (0, min(EARLY_ROWS, t))
        many = i >= FAST_BLOCKS - 1
        last_single = jnp.where(many, i - (FAST_BLOCKS - 1), i)
        state = lax.fori_loop(first, last_single, lambda j, s: sweep(((j, False, every),), s), tuple(init))

        def fast(rows):
            blocks = tuple((i - b, b == 0, every) for b in range(FAST_BLOCKS - 2, -1, -1))
            return sweep(((i - (FAST_BLOCKS - 1), False, rows),) + blocks, state)

        whole = tot_v[0:1, LANE_WHOLE:LANE_WHOLE + 1].astype(jnp.int32)[0, 0] > 0
        state = lax.cond(
            many,
            lambda: lax.cond(whole, lambda: fast(every), lambda: fast(early)),
            lambda: sweep(((i, True, every),), state))
        for p in range(pp):
            dqz_ref[0, :, p * PAIR:(p + 1) * PAIR] = (state[p][2] * scale).astype(BF16)

        @pl.when(i == nb - 1)
        def _():
            dkv_ref[0] = dk_acc[...].astype(BF16)
            dkv_ref[1] = dv_acc[...].astype(BF16)

    def tile(j):
        return pl.BlockSpec((None, t, width), lambda g, i, j=j: (j, i, g))

    def full(j):
        return pl.BlockSpec((None, seq, width), lambda g, i, j=j: (j, 0, g))

    flat_tile = pl.BlockSpec((t, width), lambda g, i: (i, g))
    return pl.pallas_call(
        body, name="attn_bwd", grid=(cw // width, nb),
        in_specs=[tile(4), full(5), full(6), tile(7), pl.BlockSpec(tri.shape, lambda g, i: (0, 0, 0)),
                  flat_tile, flat_tile, flat_tile],
        out_specs=(pl.BlockSpec((2, t, width), lambda g, i: (0, i, g)),
                   pl.BlockSpec((2, seq, width), lambda g, i: (0, 0, g))),
        out_shape=(jax.ShapeDtypeStruct((2, seq, cw), BF16), jax.ShapeDtypeStruct((2, seq, cw), BF16)),
        scratch_shapes=[pltpu.VMEM((seq, width), F32), pltpu.VMEM((seq, width), F32)],
        compiler_params=_params(48),
    )(proj, proj, proj, proj, tri, o, tot, d_mix_attn)


def _out_ln(mix_conv, mix_attn, x, target, gain, bias, w_out, tm):
    seq, d_model = x.shape
    cw = mix_conv.shape[1]
    inv_d = 1.0 / d_model

    def body(mc_ref, ma_ref, x_ref, t_ref, gain_ref, bias_ref, w_ref, dr_ref, dmc_ref, dma_ref, gwo_ref, small_ref):
        @pl.when(pl.program_id(0) == 0)
        def _():
            gwo_ref[...] = jnp.zeros_like(gwo_ref)
            small_ref[...] = jnp.zeros_like(small_ref)

        mix = jnp.concatenate([mc_ref[...], ma_ref[...]], axis=1)
        w = w_ref[...]
        r = ALPHA * x_ref[...] + _dot(mix, w)
        mu = jnp.sum(r, axis=1, keepdims=True) * inv_d
        xc = r - mu
        var = jnp.sum(xc * xc, axis=1, keepdims=True) * inv_d
        rstd = lax.rsqrt(var + LN_EPS)
        xhat = xc * rstd
        gain_v = gain_ref[...]
        err = xhat * gain_v + bias_ref[...] - t_ref[...]
        row_loss = jnp.sum(err * err, axis=1, keepdims=True)
        loss = (0.5 * inv_d) * jnp.sum(row_loss, axis=0, keepdims=True)
        dy = err * inv_d
        small_ref[ROW_GAIN:ROW_GAIN + 1, :] += jnp.sum(dy * xhat, axis=0, keepdims=True)
        small_ref[ROW_BIAS:ROW_BIAS + 1, :] += jnp.sum(dy, axis=0, keepdims=True)
        small_ref[ROW_LOSS:ROW_LOSS + 1, :] += jnp.broadcast_to(loss, (1, d_model))
        dxhat = dy * gain_v
        m1 = jnp.sum(dxhat, axis=1, keepdims=True) * inv_d
        m2 = jnp.sum(dxhat * xhat, axis=1, keepdims=True) * inv_d
        dr = rstd * (dxhat - m1 - xhat * m2)
        dr_ref[...] = dr
        drb = dr.astype(BF16)
        dmix = _dot(drb, w, NT)
        dmc_ref[...] = dmix[:, :cw].astype(BF16)
        dma_ref[...] = dmix[:, cw:].astype(BF16)
        gwo_ref[...] += _dot(mix, drb, TN)

    def rows(width):
        return pl.BlockSpec((tm, width), lambda i: (i, 0))

    def whole(shape):
        return pl.BlockSpec(shape, lambda i: (0, 0))

    return pl.pallas_call(
        body, name="out_ln", grid=(seq // tm,),
        in_specs=[rows(cw), rows(cw), rows(d_model), rows(d_model), whole((1, d_model)), whole((1, d_model)),
                  whole((d_model, d_model))],
        out_specs=(rows(d_model), rows(cw), rows(cw), whole((d_model, d_model)), whole((SUBLANES, d_model))),
        out_shape=(jax.ShapeDtypeStruct((seq, d_model), F32), jax.ShapeDtypeStruct((seq, cw), BF16),
                   jax.ShapeDtypeStruct((seq, cw), BF16), jax.ShapeDtypeStruct((d_model, d_model), F32),
                   jax.ShapeDtypeStruct((SUBLANES, d_model), F32)),
        compiler_params=_params(48),
    )(mix_conv, mix_attn, x, target, gain, bias, w_out)


_DP_OF_GROUP = ((0, 0), (0, 1), (0, 2), (0, 3), (1, 0), (2, 0), (2, 1), (1, 1))


def _grad_w_reduce(xt, dp_parts, dr, win_all, gwo, small, rows_out, row_chunk, tm):
    d_model, seq = xt.shape
    nch, _, cw = win_all.shape
    gx_tiles = seq // tm
    assert nch * tm == seq and nch * cw == seq, "needs S == 8 * tm == 8 * CW"

    def body(xt_hbm, dpa, dpb, dpc, dr_hbm, win_hbm, gwo_ref, small_ref,
             gx_hbm, g_in_o, g_out_o, small_o,
             xt_v, dp_buf, acc, got_in, send_in, recv_in, own_out, got_out, send_out, recv_out, small_all,
             dr_buf, gx_buf,
             xt_sem, dp_sems, loc_sems, d2d_send, d2d_recv, ici_send, ici_recv, sm_send, sm_recv,
             dr_sems, dpx_sems, w_sems, gx_sem):
        x, y, c = _mesh_pos()
        me = 4 * x + 2 * y + c
        sibling = (x, y, 1 - c)
        chips = [(1 - x, 1 - y), (1 - x, y), (x, 1 - y)]
        owners = [(*chip, cc) for chip in chips for cc in (1 - c, c)] + [sibling, (x, y, c)]
        group_of = [4 * o[0] + 2 * o[1] + o[2] for o in owners]
        dp_parts_ = (dpa, dpb, dpc)
        dp_groups = [dp_parts_[arr].at[idx] for arr, idx in _DP_OF_GROUP]

        xt_copy = pltpu.make_async_copy(xt_hbm, xt_v, xt_sem)
        xt_copy.start()

        def dp_start(step):
            for k in range(N_DEV):
                @pl.when(group_of[step] == k)
                def _(k=k):
                    pltpu.make_async_copy(dp_groups[k], dp_buf.at[step % 2], dp_sems.at[step % 2]).start()

        def dp_wait(step):
            pltpu.make_async_copy(dp_groups[0], dp_buf.at[step % 2], dp_sems.at[step % 2]).wait()

        dp_start(0)

        small_all[me] = small_ref[...]
        for d in range(N_DEV):
            @pl.when(d != me)
            def _(d=d):
                pltpu.make_async_remote_copy(
                    src_ref=small_ref, dst_ref=small_all.at[me], send_sem=sm_send.at[d], recv_sem=sm_recv.at[me],
                    device_id=(d // 4, (d // 2) % 2, d % 2), device_id_type=MESH).start()

        def block_out(k):
            return gwo_ref.at[pl.ds(k * rows_out, rows_out), :]

        for k in range(N_DEV):
            s = k // 2

            @pl.when(k % 2 != c)
            def _(k=k, s=s):
                pltpu.make_async_remote_copy(
                    src_ref=block_out(k), dst_ref=got_out.at[s], send_sem=d2d_send.at[1, s],
                    recv_sem=d2d_recv.at[1, s], device_id=sibling, device_id_type=MESH).start()

            @pl.when(k % 2 == c)
            def _(k=k, s=s):
                pltpu.make_async_copy(block_out(k), own_out.at[s], loc_sems.at[s]).start()

        VIA_X, VIA_Y, X_UP, X_LOW, Y_UP, Y_LOW = range(6)
        x_dev, y_dev = (1 - x, y, c), (x, 1 - y, c)
        ici_place = {VIA_X: (0, 0, x_dev), VIA_Y: (0, 1, y_dev), X_UP: (1, 0, x_dev), X_LOW: (1, 1, x_dev),
                     Y_UP: (2, 0, y_dev), Y_LOW: (2, 1, y_dev)}

        def ici_copy(a, k):
            send, recv = ((send_in, recv_in), (send_out, recv_out))[a]
            half = (d_model, rows_out)[a] // 2
            slot, part, to = ici_place[k]
            rows = pl.ds(part * half, half)
            return pltpu.make_async_remote_copy(
                src_ref=send.at[slot, rows, :], dst_ref=recv.at[slot, rows, :], send_sem=ici_send.at[a, k],
                recv_sem=ici_recv.at[a, k], device_id=to, device_id_type=MESH)

        def by_rows(first, n_rows, fn):
            step = min(row_chunk, n_rows)

            def rows_body(r, _):
                fn(pl.ds(pl.multiple_of(first + r * step, step), step))
                return 0

            lax.fori_loop(0, n_rows // step, rows_body, 0)

        def send_chip_sum(a, j, chip_sum):
            send, recv = ((send_in, recv_in), (send_out, recv_out))[a]
            n_rows = (d_model, rows_out)[a]
            half = n_rows // 2

            def plain(rows):
                send[j, rows, :] = chip_sum(rows).astype(BF16)

            if j == 0:
                by_rows(0, n_rows, plain)
                ici_copy(a, VIA_X).start()
                ici_copy(a, VIA_Y).start()
                return
            free, bound, passed = ((0, X_UP), (1, X_LOW), VIA_Y) if j == 1 else ((1, Y_LOW), (0, Y_UP), VIA_X)
            by_rows(free[0] * half, half, plain)
            ici_copy(a, free[1]).start()
            ici_copy(a, passed).wait_recv()

            def with_passed(rows):
                send[j, rows, :] = (chip_sum(rows) + recv[0, rows, :].astype(F32)).astype(BF16)

            by_rows(bound[0] * half, half, with_passed)
            ici_copy(a, bound[1]).start()

        for s in range(4):
            pltpu.make_async_copy(own_out.at[s], own_out.at[s], loc_sems.at[s]).wait()
            pltpu.make_async_remote_copy(
                src_ref=got_out.at[s], dst_ref=got_out.at[s], send_sem=d2d_send.at[1, s], recv_sem=d2d_recv.at[1, s],
                device_id=sibling, device_id_type=MESH).wait()
        for j, chip in enumerate(chips):
            s = 2 * chip[0] + chip[1]
            send_chip_sum(1, j, lambda rows, s=s: own_out[s, rows, :] + got_out[s, rows, :])

        xt_copy.wait()

        def d2d_copy(slot, pair):
            return pltpu.make_async_remote_copy(
                src_ref=acc.at[slot], dst_ref=got_in.at[pair], send_sem=d2d_send.at[0, pair],
                recv_sem=d2d_recv.at[0, pair], device_id=sibling, device_id_type=MESH)

        for step in range(N_DEV):
            slot, pair = step % 2, step // 2
            if step % 2 == 0 and step >= 2:
                d2d_copy(slot, pair - 1).wait_send()

            dp_wait(step)
            if step + 1 < N_DEV:
                dp_start(step + 1)
            acc[slot] = _dot(xt_v[...], dp_buf[step % 2])

            if step % 2 == 0:
                d2d_copy(slot, pair).start()
            else:
                d2d_copy(slot, pair).wait_recv()
                if step < N_DEV - 1:
                    send_chip_sum(0, pair, lambda rows, slot=slot, pair=pair: acc[slot, rows, :] + got_in[pair, rows, :])

        def w_copy(k):
            return pltpu.make_async_copy(win_hbm.at[k], xt_v.at[:, pl.ds(k * cw, cw)], w_sems.at[k])

        for k in range(N_DEV):
            w_copy(k).start()

        def gx_in(tile, buf):
            rows = pl.ds(tile * tm, tm)
            copies = [pltpu.make_async_copy(dr_hbm.at[rows, :], dr_buf.at[buf], dr_sems.at[buf])]
            copies += [pltpu.make_async_copy(dp_groups[k].at[rows, :], dp_buf.at[buf, pl.ds(k * tm, tm), :],
                                             dpx_sems.at[buf, k]) for k in range(N_DEV)]
            return copies

        def gx_out(tile):
            return pltpu.make_async_copy(gx_buf, gx_hbm.at[pl.ds(tile * tm, tm), :], gx_sem)

        for cp in gx_in(0, 0):
            cp.start()
        for k in range(N_DEV):
            w_copy(k).wait()

        def gx_body(tile, _):
            buf = tile % 2
            for cp in gx_in(tile, buf):
                cp.wait()

            @pl.when(tile + 1 < gx_tiles)
            def _():
                for cp in gx_in(tile + 1, 1 - buf):
                    cp.start()

            val = ALPHA * dr_buf[buf]
            for k in range(N_DEV):
                val = val + _dot(dp_buf[buf, k * tm:(k + 1) * tm, :], xt_v[:, k * cw:(k + 1) * cw], NT)

            @pl.when(tile > 0)
            def _():
                gx_out(tile - 1).wait()

            gx_buf[...] = val
            gx_out(tile).start()
            return 0

        lax.fori_loop(0, gx_tiles, gx_body, 0)

        for d in range(N_DEV):
            @pl.when(d != me)
            def _(d=d):
                pltpu.make_async_remote_copy(
                    src_ref=small_ref, dst_ref=small_all.at[d], send_sem=sm_send.at[d], recv_sem=sm_recv.at[d],
                    device_id=(d // 4, (d // 2) % 2, d % 2), device_id_type=MESH).wait()
        total = small_all[0]
        for d in range(1, N_DEV):
            total = total + small_all[d]
        small_o[...] = total

        mine = 2 * x + y
        last = (N_DEV - 1) % 2

        def finish(a, n_rows, chip_sum, g_o):
            recv = (recv_in, recv_out)[a]
            for k in (X_UP, X_LOW, Y_UP, Y_LOW):
                ici_copy(a, k).wait_recv()
            for k in ici_place:
                ici_copy(a, k).wait_send()

            def total_rows(rows):
                g_o[rows, :] = chip_sum(rows) + recv[1, rows, :].astype(F32) + recv[2, rows, :].astype(F32)

            by_rows(0, n_rows, total_rows)

        finish(1, rows_out, lambda rows: own_out[mine, rows, :] + got_out[mine, rows, :], g_out_o)
        finish(0, d_model, lambda rows: acc[last, rows, :] + got_in[N_DEV // 2 - 1, rows, :], g_in_o)
        d2d_copy(0, N_DEV // 2 - 1).wait_send()
        gx_out(0).wait()

    vmem = pl.BlockSpec(memory_space=pltpu.VMEM)
    hbm = pl.BlockSpec(memory_space=pl.ANY)
    return pl.pallas_call(
        body, name="grad_w_reduce",
        in_specs=[hbm] * 7 + [vmem],
        out_specs=(hbm, vmem, vmem, vmem),
        out_shape=(jax.ShapeDtypeStruct((seq, d_model), F32), jax.ShapeDtypeStruct((d_model, cw), F32),
                   jax.ShapeDtypeStruct((rows_out, d_model), F32), jax.ShapeDtypeStruct(small.shape, F32)),
        scratch_shapes=[
            pltpu.VMEM((d_model, seq), BF16), pltpu.VMEM((2, seq, cw), BF16), pltpu.VMEM((2, d_model, cw), F32),
            pltpu.VMEM((4, d_model, cw), F32), pltpu.VMEM((3, d_model, cw), BF16), pltpu.VMEM((3, d_model, cw), BF16),
            pltpu.VMEM((4, rows_out, d_model), F32), pltpu.VMEM((4, rows_out, d_model), F32),
            pltpu.VMEM((3, rows_out, d_model), BF16), pltpu.VMEM((3, rows_out, d_model), BF16),
            pltpu.VMEM((N_DEV,) + small.shape, F32),
            pltpu.VMEM((2, tm, d_model), F32), pltpu.VMEM((tm, d_model), F32),
            pltpu.SemaphoreType.DMA, pltpu.SemaphoreType.DMA((2,)), pltpu.SemaphoreType.DMA((4,)),
            pltpu.SemaphoreType.DMA((2, 4)), pltpu.SemaphoreType.DMA((2, 4)),
            pltpu.SemaphoreType.DMA((2, 6)), pltpu.SemaphoreType.DMA((2, 6)),
            pltpu.SemaphoreType.DMA((N_DEV,)), pltpu.SemaphoreType.DMA((N_DEV,)),
            pltpu.SemaphoreType.DMA((2,)), pltpu.SemaphoreType.DMA((2, N_DEV)), pltpu.SemaphoreType.DMA((N_DEV,)),
            pltpu.SemaphoreType.DMA,
        ],
        compiler_params=_params(56),
    )(xt, *dp_parts, dr, win_all, gwo, small)


def _adamw_update(g, w, m, v, rows):
    n_rows, width = w.shape
    rows = min(rows, n_rows)

    def body(g_ref, w_ref, m_ref, v_ref, d_o, nm_o, nv_o):
        d_o[...], nm_o[...], nv_o[...] = _adamw(w_ref[...], g_ref[...], m_ref[...], v_ref[...])

    tile = pl.BlockSpec((rows, width), lambda i: (i, 0))
    shape = jax.ShapeDtypeStruct(w.shape, F32)
    return pl.pallas_call(
        body, name="adamw_update", grid=(n_rows // rows,), in_specs=[tile] * 4, out_specs=(tile,) * 3,
        out_shape=(shape,) * 3,
    )(g, w, m, v)


def _small_update(grads, weights, ms, vs):
    n = len(grads)

    def body(*refs):
        g_refs, w_refs, m_refs, v_refs = (refs[i * n:(i + 1) * n] for i in range(4))
        outs = refs[4 * n:]
        for i in range(n):
            delta, nm, nv = _adamw(w_refs[i][...], g_refs[i][...], m_refs[i][...], v_refs[i][...])
            outs[3 * i][...] = delta
            outs[3 * i + 1][...] = nm
            outs[3 * i + 2][...] = nv

    vmem = pl.BlockSpec(memory_space=pltpu.VMEM)
    out_shape = []
    for w in weights:
        out_shape += [jax.ShapeDtypeStruct(w.shape, F32)] * 3
    return pl.pallas_call(
        body, name="small_update", in_specs=[vmem] * (4 * n), out_specs=(vmem,) * (3 * n), out_shape=tuple(out_shape),
    )(*grads, *weights, *ms, *vs)


def _tile_sizes(seq):
    return dict(tm=seq // N_DEV, t_ln=min(512, seq), t_attn=min(128, seq), rc=min(256, seq), pairs=4)


def kernel(x, w_in, conv_w, w_out, ln_gain, ln_bias, loss_target, m_w_in, m_conv_w, m_w_out, m_ln_gain, m_ln_bias,
           v_w_in, v_conv_w, v_w_out, v_ln_gain, v_ln_bias):
    assert x.shape[0] == 1 and w_in.shape[0] == 1, "one sequence per device, depth 1"
    _, seq, d_model = x.shape
    cw = w_in.shape[2]
    conv_k, conv_cols = conv_w.shape[1], conv_w.shape[2]
    rows_out = w_out.shape[1]
    assert cw == d_model // 2 and cw % PAIR == 0 and conv_cols * N_DEV == cw and rows_out * N_DEV == d_model
    ts = _tile_sizes(seq)

    x2 = x.reshape(seq, d_model)
    target = loss_target.reshape(seq, d_model)
    me = 4 * lax.axis_index("x") + 2 * lax.axis_index("y") + lax.axis_index("c")

    conv_pad = jnp.pad(conv_w[0], ((0, SUBLANES - conv_k), (0, PAIR - conv_cols)))
    proj, xt, win_all, wout_all, conv_all = _gather_proj(x2, w_in[0], w_out[0], conv_pad, ts["tm"])
    w_out_full = wout_all.reshape(d_model, d_model)
    conv_full = conv_all[:, :conv_k, :conv_cols].transpose(1, 0, 2).reshape(conv_k, cw)
    conv_full = jnp.pad(conv_full, ((0, SUBLANES - conv_k), (0, 0)))

    mix_conv = _conv_fwd(proj, conv_full, ts["rc"])
    pairs = min(ts["pairs"], cw // PAIR)
    tri = _triangles(ts["t_attn"])
    o, mix_attn, tot = _attn_fwd(proj, tri, ts["t_attn"], pairs)
    dr, d_mix_conv, d_mix_attn, gwo, small = _out_ln(mix_conv, mix_attn, x2, target, ln_gain, ln_bias, w_out_full,
                                                     ts["t_ln"])
    dp_conv, d_taps = _conv_bwd(proj, conv_full, d_mix_conv, ts["rc"])
    dp_qz, dp_kv = _attn_bwd(proj, tri, o, tot, d_mix_attn, ts["t_attn"], pairs)
    small = small.at[ROW_CONV:ROW_CONV + conv_k, :cw].set(d_taps[:conv_k])
    grad_x, g_in, g_out, small_sum = _grad_w_reduce(
        xt, (dp_conv, dp_qz, dp_kv), dr, win_all, gwo, small, rows_out, 128, ts["tm"])
    d_in, nm_in, nv_in = _adamw_update(g_in, w_in[0], m_w_in[0], v_w_in[0], 256)
    d_out, nm_out, nv_out = _adamw_update(g_out, w_out[0], m_w_out[0], v_w_out[0], 256)

    loss = small_sum[ROW_LOSS, 0]
    g_gain = small_sum[ROW_GAIN:ROW_GAIN + 1]
    g_bias = small_sum[ROW_BIAS:ROW_BIAS + 1]
    g_conv = lax.dynamic_slice(small_sum, (ROW_CONV, me * conv_cols), (conv_k, conv_cols))
    upd = _small_update((g_conv, g_gain, g_bias), (conv_w[0], ln_gain, ln_bias),
                        (m_conv_w[0], m_ln_gain, m_ln_bias), (v_conv_w[0], v_ln_gain, v_ln_bias))
    d_conv, nm_conv, nv_conv, d_gain, nm_gain, nv_gain, d_bias, nm_bias, nv_bias = upd

    lead = lambda a: a[None]
    return (loss, grad_x.reshape(1, seq, d_model), lead(g_in), lead(g_conv), lead(g_out), g_gain, g_bias,
            lead(d_in), lead(d_conv), lead(d_out), d_gain, d_bias,
            lead(nm_in), lead(nm_conv), lead(nm_out), nm_gain, nm_bias,
            lead(nv_in), lead(nv_conv), lead(nv_out), nv_gain, nv_bias)
```

```python
import functools

import jax
import jax.numpy as jnp
from jax import lax
from jax.experimental import pallas as pl
from jax.experimental.pallas import tpu as pltpu

F32 = jnp.float32
BF16 = jnp.bfloat16
MESH = pl.DeviceIdType.MESH

N_DEV = 8
HEAD_DIM = 64
PAIR = 128
SUBLANES = 8
LN_EPS = 1e-5
ALPHA = 2.0 ** 0.25
ADAM_LR, ADAM_B1, ADAM_B2, ADAM_EPS, ADAM_WD, ADAM_STEP = 0.001, 0.9, 0.999, 1e-08, 0.01, 10

ROW_GAIN, ROW_BIAS, ROW_CONV, ROW_LOSS = 0, 1, 2, 5

NT = (((1,), (1,)), ((), ()))
TN = (((0,), (0,)), ((), ()))


V7X_VMEM_BYTES = 64 * 1024 * 1024


def _params(vmem_mib):
    assert vmem_mib * 1024 * 1024 < V7X_VMEM_BYTES
    return pltpu.CompilerParams(vmem_limit_bytes=vmem_mib * 1024 * 1024)


def _dot(a, b, dims=None):
    if dims is None:
        return jnp.dot(a, b, preferred_element_type=F32)
    return lax.dot_general(a, b, dims, preferred_element_type=F32)


def _sigmoid(z):
    return 1.0 / (1.0 + jnp.exp(-z))


def _mesh_pos():
    return lax.axis_index("x"), lax.axis_index("y"), lax.axis_index("c")


def _adamw(w, g, m, v):
    nm = ADAM_B1 * m + (1.0 - ADAM_B1) * g
    nv = ADAM_B2 * v + (1.0 - ADAM_B2) * (g * g)
    m_hat = nm * (1.0 / (1.0 - ADAM_B1 ** ADAM_STEP))
    v_hat = nv * (1.0 / (1.0 - ADAM_B2 ** ADAM_STEP))
    delta = -ADAM_LR * (m_hat / (jnp.sqrt(v_hat) + ADAM_EPS) + ADAM_WD * w)
    return delta, nm, nv


def _gather_proj(x, w_in_s, w_out_s, conv_s, tm):
    seq, d_model = x.shape
    cw = w_in_s.shape[1]
    rows_out = w_out_s.shape[0]
    n_tiles = seq // tm
    half = d_model // 2
    SIB, X_UP, X_LOW, Y_UP, Y_LOW, VIA_Y, VIA_X, ON_X, ON_Y, ON_DIAG = range(10)

    def body(x_hbm, win_ref, wout_ref, conv_ref, proj_hbm, xt_hbm, win_all, wout_all, conv_all,
             xb, x_stage, o_stage, xt_stage, x_sems, o_sems, xt_sems, w_send, w_recv, send_sems, recv_sems):
        x, y, c = _mesh_pos()
        me = (x, y, c)
        sibling = (x, y, 1 - c)
        x_nbr, y_nbr, diag = (1 - x, y), (x, 1 - y), (1 - x, 1 - y)
        chips = [x_nbr, y_nbr, diag]
        small = (wout_all, conv_all)

        def slot(pos):
            return 4 * pos[0] + 2 * pos[1] + pos[2]

        def x_copy(tile, buf):
            return pltpu.make_async_copy(x_hbm.at[pl.ds(tile * tm, tm), :], x_stage.at[buf], x_sems.at[buf])

        x_copy(0, 0).start()
        win_all[slot(me)] = win_ref[...].astype(BF16)
        wout_all[slot(me)] = wout_ref[...].astype(BF16)
        conv_all[slot(me)] = conv_ref[...]

        def w_copy(k, block, part, to):
            ref = win_all.at[slot(block)]
            if part is not None:
                ref = ref.at[pl.ds(part * half, half), :]
            return pltpu.make_async_remote_copy(
                src_ref=ref, dst_ref=ref, send_sem=w_send.at[k], recv_sem=w_recv.at[k],
                device_id=to, device_id_type=MESH)

        def copy(a, k, block, to):
            ref = small[a].at[slot(block)]
            return pltpu.make_async_remote_copy(
                src_ref=ref, dst_ref=ref, send_sem=send_sems.at[a, k], recv_sem=recv_sems.at[a, k],
                device_id=to, device_id_type=MESH)

        sends = [w_copy(SIB, me, None, sibling),
                 w_copy(X_UP, me, 0, (*x_nbr, c)), w_copy(Y_LOW, me, 1, (*y_nbr, c)),
                 w_copy(X_LOW, me, 1, (*x_nbr, c)), w_copy(Y_UP, me, 0, (*y_nbr, c))]
        for a in range(len(small)):
            sends.append(copy(a, 0, me, sibling))
            sends += [copy(a, 1 + j, me, (*chip, c)) for j, chip in enumerate(chips)]
        for cp in sends:
            cp.start()

        def o_copy(group, tile, buf):
            return pltpu.make_async_copy(o_stage.at[buf], proj_hbm.at[group, pl.ds(tile * tm, tm), :], o_sems.at[buf])

        def xt_copy(tile, buf):
            return pltpu.make_async_copy(xt_stage.at[buf], xt_hbm.at[:, pl.ds(tile * tm, tm)], xt_sems.at[buf])

        def project(order, group, first_pass):
            def tile_body(tile, _):
                if first_pass:
                    buf = tile % 2
                    x_copy(tile, buf).wait()

                    @pl.when(tile + 1 < n_tiles)
                    def _():
                        x_copy(tile + 1, 1 - buf).start()

                    xv = x_stage[buf]
                    xb[tile] = xv.astype(BF16)

                    @pl.when(tile >= 2)
                    def _():
                        xt_copy(tile - 2, buf).wait()

                    xt_stage[buf] = xv.T.astype(BF16)
                    xt_copy(tile, buf).start()
                count = order * n_tiles + tile
                obuf = count % 2

                @pl.when(count >= 2)
                def _():
                    o_copy(group, tile, obuf).wait()

                o_stage[obuf] = _dot(xb[tile], win_all[group]).astype(BF16)
                o_copy(group, tile, obuf).start()
                return 0

            lax.fori_loop(0, n_tiles, tile_body, 0)

        def start(cp):
            cp.start()
            sends.append(cp)

        def small_pass_on(j):
            for a in range(len(small)):
                copy(a, 1 + j, (*chips[j], c), me).wait_recv()
                start(copy(a, 4 + j, (*chips[j], c), sibling))

        def small_from_sibling(k):
            for a in range(len(small)):
                copy(a, k, sibling, me).wait_recv()

        project(0, slot(me), True)
        w_copy(SIB, sibling, None, me).wait_recv()
        small_from_sibling(0)
        project(1, slot(sibling), False)
        w_copy(X_UP, (*x_nbr, c), 0, me).wait_recv()
        start(w_copy(VIA_Y, (*x_nbr, c), 0, (*y_nbr, c)))
        w_copy(Y_LOW, (*y_nbr, c), 1, me).wait_recv()
        start(w_copy(VIA_X, (*y_nbr, c), 1, (*x_nbr, c)))
        w_copy(X_LOW, (*x_nbr, c), 1, me).wait_recv()
        start(w_copy(ON_X, (*x_nbr, c), None, sibling))
        small_pass_on(0)
        project(2, slot((*x_nbr, c)), False)
        w_copy(Y_UP, (*y_nbr, c), 0, me).wait_recv()
        start(w_copy(ON_Y, (*y_nbr, c), None, sibling))
        small_pass_on(1)
        project(3, slot((*y_nbr, c)), False)
        w_copy(ON_X, (*x_nbr, 1 - c), None, me).wait_recv()
        small_from_sibling(4)
        project(4, slot((*x_nbr, 1 - c)), False)
        w_copy(ON_Y, (*y_nbr, 1 - c), None, me).wait_recv()
        small_from_sibling(5)
        project(5, slot((*y_nbr, 1 - c)), False)
        w_copy(VIA_Y, (*diag, c), 0, me).wait_recv()
        w_copy(VIA_X, (*diag, c), 1, me).wait_recv()
        start(w_copy(ON_DIAG, (*diag, c), None, sibling))
        small_pass_on(2)
        project(6, slot((*diag, c)), False)
        w_copy(ON_DIAG, (*diag, 1 - c), None, me).wait_recv()
        small_from_sibling(6)
        project(7, slot((*diag, 1 - c)), False)

        for buf in range(2):
            o_copy(0, 0, buf).wait()
        for buf in range(min(2, n_tiles)):
            xt_copy(0, buf).wait()
        for cp in sends:
            cp.wait_send()

    vmem = pl.BlockSpec(memory_space=pltpu.VMEM)
    hbm = pl.BlockSpec(memory_space=pl.ANY)
    return pl.pallas_call(
        body, name="gather_proj",
        out_shape=(jax.ShapeDtypeStruct((N_DEV, seq, cw), BF16),
                   jax.ShapeDtypeStruct((d_model, seq), BF16),
                   jax.ShapeDtypeStruct((N_DEV, d_model, cw), BF16),
                   jax.ShapeDtypeStruct((N_DEV, rows_out, d_model), BF16),
                   jax.ShapeDtypeStruct((N_DEV,) + conv_s.shape, F32)),
        in_specs=[hbm, vmem, vmem, vmem], out_specs=(hbm, hbm, vmem, vmem, vmem),
        scratch_shapes=[
            pltpu.VMEM((n_tiles, tm, d_model), BF16), pltpu.VMEM((2, tm, d_model), F32),
            pltpu.VMEM((2, tm, cw), BF16), pltpu.VMEM((2, d_model, tm), BF16),
            pltpu.SemaphoreType.DMA((2,)), pltpu.SemaphoreType.DMA((2,)), pltpu.SemaphoreType.DMA((2,)),
            pltpu.SemaphoreType.DMA((10,)), pltpu.SemaphoreType.DMA((10,)),
            pltpu.SemaphoreType.DMA((2, 7)), pltpu.SemaphoreType.DMA((2, 7))],
        compiler_params=_params(48),
    )(x, w_in_s, w_out_s, conv_s)


def _conv_taps(ext, w_ref, rc):
    u0 = ext[SUBLANES:SUBLANES + rc]
    u1 = pltpu.roll(ext, 1, 0)[SUBLANES:SUBLANES + rc]
    u2 = pltpu.roll(ext, 2, 0)[SUBLANES:SUBLANES + rc]
    return w_ref[2:3, :] * u0 + w_ref[1:2, :] * u1 + w_ref[0:1, :] * u2, u0, u1, u2


def _conv_fwd(proj, conv_full, rc):
    _, seq, cw = proj.shape

    def body(b_ref, c_ref, h_ref, z_ref, w_ref, o_ref, u_scr):
        u_scr[0:SUBLANES, :] = jnp.zeros((SUBLANES, PAIR), F32)

        def fill(r, _):
            base = pl.multiple_of(r * rc, rc)
            rows = pl.ds(base, rc)
            u_scr[pl.ds(base + SUBLANES, rc), :] = c_ref[rows, :].astype(F32) * h_ref[rows, :].astype(F32)
            return 0

        lax.fori_loop(0, seq // rc, fill, 0)

        def out(r, _):
            base = pl.multiple_of(r * rc, rc)
            rows = pl.ds(base, rc)
            ext = u_scr[pl.ds(base, rc + SUBLANES), :]
            y, _, _, _ = _conv_taps(ext, w_ref, rc)
            z = z_ref[rows, :].astype(F32)
            o_ref[rows, :] = (z * _sigmoid(z) * b_ref[rows, :].astype(F32) * y).astype(BF16)
            return 0

        lax.fori_loop(0, seq // rc, out, 0)

    def chunk(j):
        return pl.BlockSpec((None, seq, PAIR), lambda cb, j=j: (j, 0, cb))

    return pl.pallas_call(
        body, name="conv_fwd", grid=(cw // PAIR,),
        in_specs=[chunk(0), chunk(1), chunk(2), chunk(3), pl.BlockSpec((SUBLANES, PAIR), lambda cb: (0, cb))],
        out_specs=pl.BlockSpec((seq, PAIR), lambda cb: (0, cb)),
        out_shape=jax.ShapeDtypeStruct((seq, cw), BF16),
        scratch_shapes=[pltpu.VMEM((seq + SUBLANES, PAIR), F32)],
    )(proj, proj, proj, proj, conv_full)


def _conv_bwd(proj, conv_full, d_mix_conv, rc):
    _, seq, cw = proj.shape

    def body(b_ref, c_ref, h_ref, z_ref, w_ref, g_ref, dp_ref, dw_ref, u_scr, dy_scr):
        u_scr[0:SUBLANES, :] = jnp.zeros((SUBLANES, PAIR), F32)
        dy_scr[seq:seq + SUBLANES, :] = jnp.zeros((SUBLANES, PAIR), F32)

        def fill(r, _):
            base = pl.multiple_of(r * rc, rc)
            rows = pl.ds(base, rc)
            u_scr[pl.ds(base + SUBLANES, rc), :] = c_ref[rows, :].astype(F32) * h_ref[rows, :].astype(F32)
            return 0

        lax.fori_loop(0, seq // rc, fill, 0)

        def gate(r, acc):
            base = pl.multiple_of(r * rc, rc)
            rows = pl.ds(base, rc)
            ext = u_scr[pl.ds(base, rc + SUBLANES), :]
            y, u0, u1, u2 = _conv_taps(ext, w_ref, rc)
            z = z_ref[rows, :].astype(F32)
            b = b_ref[rows, :].astype(F32)
            g = g_ref[rows, :].astype(F32)
            sig = _sigmoid(z)
            dp_ref[3, rows, :] = (g * b * y * (sig * (1.0 + z * (1.0 - sig)))).astype(BF16)
            gs = g * (z * sig)
            dp_ref[0, rows, :] = (gs * y).astype(BF16)
            dy = gs * b
            dy_scr[rows, :] = dy
            a0, a1, a2 = acc
            return (a0 + jnp.sum(dy * u2, axis=0, keepdims=True),
                    a1 + jnp.sum(dy * u1, axis=0, keepdims=True),
                    a2 + jnp.sum(dy * u0, axis=0, keepdims=True))

        zero = jnp.zeros((1, PAIR), F32)
        a0, a1, a2 = lax.fori_loop(0, seq // rc, gate, (zero, zero, zero))
        dw_ref[...] = jnp.zeros((SUBLANES, PAIR), F32)
        dw_ref[0:1, :] = a0
        dw_ref[1:2, :] = a1
        dw_ref[2:3, :] = a2

        def back(r, _):
            base = pl.multiple_of(r * rc, rc)
            rows = pl.ds(base, rc)
            ext = dy_scr[pl.ds(base, rc + SUBLANES), :]
            n = rc + SUBLANES
            d0 = ext[0:rc]
            d1 = pltpu.roll(ext, n - 1, 0)[0:rc]
            d2 = pltpu.roll(ext, n - 2, 0)[0:rc]
            du = w_ref[2:3, :] * d0 + w_ref[1:2, :] * d1 + w_ref[0:1, :] * d2
            dp_ref[1, rows, :] = (du * h_ref[rows, :].astype(F32)).astype(BF16)
            dp_ref[2, rows, :] = (du * c_ref[rows, :].astype(F32)).astype(BF16)
            return 0

        lax.fori_loop(0, seq // rc, back, 0)

    def chunk(j):
        return pl.BlockSpec((None, seq, PAIR), lambda cb, j=j: (j, 0, cb))

    return pl.pallas_call(
        body, name="conv_bwd", grid=(cw // PAIR,),
        in_specs=[chunk(0), chunk(1), chunk(2), chunk(3), pl.BlockSpec((SUBLANES, PAIR), lambda cb: (0, cb)),
                  pl.BlockSpec((seq, PAIR), lambda cb: (0, cb))],
        out_specs=(pl.BlockSpec((4, seq, PAIR), lambda cb: (0, 0, cb)),
                   pl.BlockSpec((SUBLANES, PAIR), lambda cb: (0, cb))),
        out_shape=(jax.ShapeDtypeStruct((4, seq, cw), BF16), jax.ShapeDtypeStruct((SUBLANES, cw), F32)),
        scratch_shapes=[pltpu.VMEM((seq + SUBLANES, PAIR), F32), pltpu.VMEM((seq + SUBLANES, PAIR), F32)],
    )(proj, proj, proj, proj, conv_full, d_mix_conv)


SKIP_CARRY = 104.0
LOG2_E = 1.4426950408889634
LANE_TOT0, LANE_TOT1, LANE_FIRST, LANE_WHOLE = 0, 1, 2, 3
FAST_BLOCKS = 3
EARLY_ROWS = 32


def _triangles(t):
    row = lax.broadcasted_iota(jnp.int32, (2 * t, 2 * t), 0)
    col = lax.broadcasted_iota(jnp.int32, (2 * t, 2 * t), 1)
    same = (row < t) == (col < t)
    upper = jnp.logical_and(same, row > col).astype(BF16)
    lower = jnp.logical_and(same, row < col).astype(BF16)
    return jnp.stack([jnp.concatenate([upper, upper], axis=0), jnp.concatenate([lower, lower], axis=0)])


def _pair_masks(t):
    lane = lax.broadcasted_iota(jnp.int32, (t, PAIR), 1)
    qrow = lax.broadcasted_iota(jnp.int32, (t, 2 * t), 0)
    kcol = lax.broadcasted_iota(jnp.int32, (t, 2 * t), 1)
    strict = jnp.where(kcol < t, kcol, kcol - t) < qrow
    return lane, lane < HEAD_DIM, strict


def _by_head(x, head0):
    zero = jnp.zeros_like(x)
    return jnp.concatenate([jnp.where(head0, x, zero), jnp.where(head0, zero, x)], axis=0)


def _hi_lo(a):
    hi = a.astype(BF16)
    lo = (a - hi.astype(F32)).astype(BF16)
    return jnp.concatenate([hi, lo], axis=1)


def _softplus_parts(z, strict, masked):
    spu = jnp.maximum(z, 0.0) + jnp.log(1.0 + jnp.exp2(jnp.abs(z) * -LOG2_E))
    return z - spu, (jnp.where(strict, spu, 0.0) if masked else spu)


def _stacked_dot(parts, rhs):
    out = _dot(jnp.concatenate(parts, axis=0), rhs)
    ends = [0]
    for p in parts:
        ends.append(ends[-1] + p.shape[0])
    return [out[a:b] for a, b in zip(ends[:-1], ends[1:])]


def _splice(whole, rows, part):
    pieces = ([whole[:rows[0]]] if rows[0] > 0 else []) + [part]
    if rows[1] < whole.shape[0]:
        pieces.append(whole[rows[1]:])
    return part if len(pieces) == 1 else jnp.concatenate(pieces, axis=0)


def _attn_fwd(proj, tri, t, pp):
    _, seq, cw = proj.shape
    scale = HEAD_DIM ** -0.5
    width = pp * PAIR

    def body(q_ref, k_ref, v_ref, za_ref, tri_ref, o_ref, mix_ref, tot_ref, acc_ref):
        i = pl.program_id(1)
        lane, head0, strict = _pair_masks(t)
        upper = tri_ref[0]
        q = q_ref[...] * scale
        acc_ref[...] = jnp.zeros_like(acc_ref)

        def sweep(blocks, state):
            staged = []
            for j, masked, rows in blocks:
                start = pl.multiple_of(j * t, t)
                kb = k_ref[pl.ds(start, t), :]
                vb = v_ref[pl.ds(start, t), :]
                for p in range(pp):
                    cols = slice(p * PAIR, (p + 1) * PAIR)
                    z = _dot(q[rows[0]:rows[1], cols], _by_head(kb[:, cols], head0), NT)
                    ls, sp = _softplus_parts(z, strict[rows[0]:rows[1]], masked)
                    staged.append((p, masked, rows, ls, sp, _by_head(vb[:, cols], head0)))
            afters = _stacked_dot([_hi_lo(sp) for _, _, _, _, sp, _ in staged], upper)
            state = list(state)
            for (p, masked, rows, ls, sp, v2), after in zip(staged, afters):
                c0, c1 = state[p]
                part = slice(rows[0], rows[1])
                x = ls - after
                w = jnp.exp(jnp.concatenate([x[:, :t] - c0[part], x[:, t:] - c1[part]], axis=1))
                if masked:
                    w = jnp.where(strict[part], w, 0.0)
                acc_ref[part, p * PAIR:(p + 1) * PAIR] += _dot(w.astype(BF16), v2)
                state[p] = (_splice(c0, rows, c0[part] + (after[:, 0:1] + sp[:, 0:1])),
                            _splice(c1, rows, c1[part] + (after[:, t:t + 1] + sp[:, t:t + 1])))
            return tuple(state)

        def unfinished(state, rows):
            m = state[0][0]
            for p in range(pp):
                m = jnp.minimum(m, jnp.minimum(state[p][0], state[p][1]))
            return jnp.min(m[rows[0]:rows[1]]) < SKIP_CARRY

        every = (0, t)
        early, late = (0, min(EARLY_ROWS, t)), (min(EARLY_ROWS, t), t)

        def step(js):
            state = sweep(((js[0], False, every),), js[1])
            return js[0] - 1, state, unfinished(state, every)

        def fast():
            blocks = tuple((i - b, b == 0, every) for b in range(FAST_BLOCKS - 1))
            state = sweep(blocks + ((i - (FAST_BLOCKS - 1), False, early),), init)
            if late[0] == late[1]:
                return state, jnp.bool_(True)
            whole = unfinished(state, late)
            state = lax.cond(whole, lambda: sweep(((i - (FAST_BLOCKS - 1), False, late),), state), lambda: state)
            return state, whole

        zcol = jnp.zeros((t, 1), F32)
        init = tuple((zcol, zcol) for _ in range(pp))
        many = i >= FAST_BLOCKS - 1
        state, whole = lax.cond(many, fast, lambda: (sweep(((i, True, every),), init), jnp.bool_(True)))
        j_end, state, _ = lax.while_loop(
            lambda js: jnp.logical_and(js[0] >= 0, js[2]), step,
            (jnp.where(many, i - FAST_BLOCKS, i - 1), state, unfinished(state, every)))
        first = (j_end + 1).astype(F32)
        notes = jnp.where(lane == LANE_FIRST, first, whole.astype(F32))
        za = za_ref[...].astype(F32)
        for p in range(pp):
            c0, c1 = state[p]
            cols = slice(p * PAIR, (p + 1) * PAIR)
            zp = za[:, cols]
            acc = acc_ref[:, cols]
            o_ref[:, cols] = acc.astype(BF16)
            mix_ref[:, cols] = (zp * _sigmoid(zp) * acc).astype(BF16)
            tot_ref[:, cols] = jnp.where(lane == LANE_TOT0, c0, jnp.where(lane == LANE_TOT1, c1, notes))

    def tile(j):
        return pl.BlockSpec((None, t, width), lambda g, i, j=j: (j, i, g))

    def full(j):
        return pl.BlockSpec((None, seq, width), lambda g, i, j=j: (j, 0, g))

    out_tile = pl.BlockSpec((t, width), lambda g, i: (i, g))
    return pl.pallas_call(
        body, name="attn_fwd", grid=(cw // width, seq // t),
        in_specs=[tile(4), full(5), full(6), tile(7), pl.BlockSpec(tri.shape, lambda g, i: (0, 0, 0))],
        out_specs=(out_tile, out_tile, out_tile),
        out_shape=(jax.ShapeDtypeStruct((seq, cw), BF16), jax.ShapeDtypeStruct((seq, cw), BF16),
                   jax.ShapeDtypeStruct((seq, cw), F32)),
        scratch_shapes=[pltpu.VMEM((t, width), F32)],
    )(proj, proj, proj, proj, tri)


def _attn_bwd(proj, tri, o, tot, d_mix_attn, t, pp):
    _, seq, cw = proj.shape
    nb = seq // t
    scale = HEAD_DIM ** -0.5
    width = pp * PAIR

    def body(q_ref, k_ref, v_ref, za_ref, tri_ref, o_ref, tot_ref, g_ref, dqz_ref, dkv_ref, dk_acc, dv_acc, dq_acc):
        i = pl.program_id(1)
        dq_acc[...] = jnp.zeros_like(dq_acc)

        @pl.when(i == 0)
        def _():
            dk_acc[...] = jnp.zeros_like(dk_acc)
            dv_acc[...] = jnp.zeros_like(dv_acc)

        _, head0, strict = _pair_masks(t)
        upper, lower = tri_ref[0], tri_ref[1, 0:2 * t, :]
        za = za_ref[...].astype(F32)
        g = g_ref[...].astype(F32)
        sig = _sigmoid(za)
        dqz_ref[1] = (g * o_ref[...].astype(F32) * (sig * (1.0 + za * (1.0 - sig)))).astype(BF16)
        do = (g * (za * sig)).astype(BF16)
        q = q_ref[...] * scale
        tot_v = tot_ref[...]
        q2, do2, init = [], [], []
        zcol = jnp.zeros((t, 1), F32)
        for p in range(pp):
            cols = slice(p * PAIR, (p + 1) * PAIR)
            q2.append(_by_head(q[:, cols], head0))
            do2.append(_by_head(do[:, cols], head0))
            tp = tot_v[:, cols]
            init.append(((tp[:, LANE_TOT0:LANE_TOT0 + 1], tp[:, LANE_TOT1:LANE_TOT1 + 1]), (zcol, zcol)))
        first = jnp.clip(tot_v[0:1, LANE_FIRST:LANE_FIRST + 1], 0.0, i.astype(F32)).astype(jnp.int32)[0, 0]

        def sweep(blocks, state):
            staged = []
            for j, masked, rows in blocks:
                start = pl.multiple_of(j * t, t)
                kb = k_ref[pl.ds(start, t), :]
                vb = v_ref[pl.ds(start, t), :]
                part = slice(rows[0], rows[1])
                for p in range(pp):
                    cols = slice(p * PAIR, (p + 1) * PAIR)
                    k2 = _by_head(kb[:, cols], head0)
                    z = _dot(q[part, cols], k2, NT)
                    ls, sp = _softplus_parts(z, strict[part], masked)
                    da = _dot(do[part, cols], _by_head(vb[:, cols], head0), NT)
                    staged.append((p, masked, rows, k2, ls, sp, da))
            afters = _stacked_dot([_hi_lo(sp) for _, _, _, _, _, sp, _ in staged], upper)
            state = list(state)
            weights, ggs = [], []
            for (p, masked, rows, k2, ls, sp, da), after in zip(staged, afters):
                (s0, s1), befores = state[p]
                part = slice(rows[0], rows[1])
                n0 = s0[part] - (after[:, 0:1] + sp[:, 0:1])
                n1 = s1[part] - (after[:, t:t + 1] + sp[:, t:t + 1])
                x = ls - after
                a = jnp.exp(jnp.concatenate([x[:, :t] - n0, x[:, t:] - n1], axis=1))
                if masked:
                    a = jnp.where(strict[part], a, 0.0)
                state[p] = ((_splice(s0, rows, n0), _splice(s1, rows, n1)), befores)
                weights.append(a.astype(BF16))
                ggs.append(a * da)
            pres = _stacked_dot([gg.astype(BF16) for gg in ggs], lower)
            dzs = []
            for (p, masked, rows, k2, ls, sp, da), gg, pre in zip(staged, ggs, pres):
                rests, (b0, b1) = state[p]
                part = slice(rows[0], rows[1])
                y = gg + pre
                dz = gg - jnp.exp(ls) * jnp.concatenate([y[:, :t] + b0[part], y[:, t:] + b1[part]], axis=1)
                if masked:
                    dz = jnp.where(strict[part], dz, 0.0)
                dzb = dz.astype(BF16)
                dzs.append(dzb)
                dq_acc[part, p * PAIR:(p + 1) * PAIR] += _dot(dzb, k2)
                state[p] = (rests, (_splice(b0, rows, b0[part] + y[:, t - 1:t]),
                                    _splice(b1, rows, b1[part] + y[:, 2 * t - 1:2 * t])))
            first_row = pl.multiple_of(blocks[0][0] * t, t)
            n_rows = len(blocks) * t
            for p in range(pp):
                cols = slice(p * PAIR, (p + 1) * PAIR)

                def by_key(tiles):
                    out = []
                    for n, m in tiles:
                        rows = staged[n][2]
                        m = _splice(jnp.zeros((t, 2 * t), BF16), rows, m)
                        out.append(jnp.concatenate([m[:, :t], m[:, t:]], axis=0).T)
                    return jnp.concatenate(out, axis=0)

                mine = [n for n in range(len(staged)) if staged[n][0] == p]
                dk_acc[pl.ds(first_row, n_rows), cols] += _dot(by_key([(n, dzs[n]) for n in mine]), q2[p])
                dv_acc[pl.ds(first_row, n_rows), cols] += _dot(by_key([(n, weights[n]) for n in mine]), do2[p])
            return tuple(state)

        every = (0, t)
        early = (0, min(EARLY_ROWS, t))
        many = i >= FAST_BLOCKS - 1
        last_single = jnp.where(many, i - (FAST_BLOCKS - 1), i)
        state = lax.fori_loop(first, last_single, lambda j, s: sweep(((j, False, every),), s), tuple(init))

        def fast(rows):
            blocks = tuple((i - b, b == 0, every) for b in range(FAST_BLOCKS - 2, -1, -1))
            return sweep(((i - (FAST_BLOCKS - 1), False, rows),) + blocks, state)

        whole = tot_v[0:1, LANE_WHOLE:LANE_WHOLE + 1].astype(jnp.int32)[0, 0] > 0
        state = lax.cond(
            many,
            lambda: lax.cond(whole, lambda: fast(every), lambda: fast(early)),
            lambda: sweep(((i, True, every),), state))
        dqz_ref[0] = (dq_acc[...] * scale).astype(BF16)

        @pl.when(i == nb - 1)
        def _():
            dkv_ref[0] = dk_acc[...].astype(BF16)
            dkv_ref[1] = dv_acc[...].astype(BF16)

    def tile(j):
        return pl.BlockSpec((None, t, width), lambda g, i, j=j: (j, i, g))

    def full(j):
        return pl.BlockSpec((None, seq, width), lambda g, i, j=j: (j, 0, g))

    flat_tile = pl.BlockSpec((t, width), lambda g, i: (i, g))
    return pl.pallas_call(
        body, name="attn_bwd", grid=(cw // width, nb),
        in_specs=[tile(4), full(5), full(6), tile(7), pl.BlockSpec(tri.shape, lambda g, i: (0, 0, 0)),
                  flat_tile, flat_tile, flat_tile],
        out_specs=(pl.BlockSpec((2, t, width), lambda g, i: (0, i, g)),
                   pl.BlockSpec((2, seq, width), lambda g, i: (0, 0, g))),
        out_shape=(jax.ShapeDtypeStruct((2, seq, cw), BF16), jax.ShapeDtypeStruct((2, seq, cw), BF16)),
        scratch_shapes=[pltpu.VMEM((seq, width), F32), pltpu.VMEM((seq, width), F32), pltpu.VMEM((t, width), F32)],
        compiler_params=_params(48),
    )(proj, proj, proj, proj, tri, o, tot, d_mix_attn)


def _out_ln(mix_conv, mix_attn, x, target, gain, bias, w_out, tm):
    seq, d_model = x.shape
    cw = mix_conv.shape[1]
    inv_d = 1.0 / d_model

    def body(mc_ref, ma_ref, x_ref, t_ref, gain_ref, bias_ref, w_ref, dr_ref, dmc_ref, dma_ref, gwo_ref, small_ref):
        @pl.when(pl.program_id(0) == 0)
        def _():
            gwo_ref[...] = jnp.zeros_like(gwo_ref)
            small_ref[...] = jnp.zeros_like(small_ref)

        mix = jnp.concatenate([mc_ref[...], ma_ref[...]], axis=1)
        w = w_ref[...]
        r = ALPHA * x_ref[...] + _dot(mix, w)
        mu = jnp.sum(r, axis=1, keepdims=True) * inv_d
        xc = r - mu
        var = jnp.sum(xc * xc, axis=1, keepdims=True) * inv_d
        rstd = lax.rsqrt(var + LN_EPS)
        xhat = xc * rstd
        gain_v = gain_ref[...]
        err = xhat * gain_v + bias_ref[...] - t_ref[...]
        row_loss = jnp.sum(err * err, axis=1, keepdims=True)
        loss = (0.5 * inv_d) * jnp.sum(row_loss, axis=0, keepdims=True)
        dy = err * inv_d
        small_ref[ROW_GAIN:ROW_GAIN + 1, :] += jnp.sum(dy * xhat, axis=0, keepdims=True)
        small_ref[ROW_BIAS:ROW_BIAS + 1, :] += jnp.sum(dy, axis=0, keepdims=True)
        small_ref[ROW_LOSS:ROW_LOSS + 1, :] += jnp.broadcast_to(loss, (1, d_model))
        dxhat = dy * gain_v
        m1 = jnp.sum(dxhat, axis=1, keepdims=True) * inv_d
        m2 = jnp.sum(dxhat * xhat, axis=1, keepdims=True) * inv_d
        dr = rstd * (dxhat - m1 - xhat * m2)
        dr_ref[...] = dr
        drb = dr.astype(BF16)
        dmix = _dot(drb, w, NT)
        dmc_ref[...] = dmix[:, :cw].astype(BF16)
        dma_ref[...] = dmix[:, cw:].astype(BF16)
        gwo_ref[...] += _dot(mix, drb, TN)

    def rows(width):
        return pl.BlockSpec((tm, width), lambda i: (i, 0))

    def whole(shape):
        return pl.BlockSpec(shape, lambda i: (0, 0))

    return pl.pallas_call(
        body, name="out_ln", grid=(seq // tm,),
        in_specs=[rows(cw), rows(cw), rows(d_model), rows(d_model), whole((1, d_model)), whole((1, d_model)),
                  whole((d_model, d_model))],
        out_specs=(rows(d_model), rows(cw), rows(cw), whole((d_model, d_model)), whole((SUBLANES, d_model))),
        out_shape=(jax.ShapeDtypeStruct((seq, d_model), F32), jax.ShapeDtypeStruct((seq, cw), BF16),
                   jax.ShapeDtypeStruct((seq, cw), BF16), jax.ShapeDtypeStruct((d_model, d_model), F32),
                   jax.ShapeDtypeStruct((SUBLANES, d_model), F32)),
        compiler_params=_params(48),
    )(mix_conv, mix_attn, x, target, gain, bias, w_out)


_DP_OF_GROUP = ((0, 0), (0, 1), (0, 2), (0, 3), (1, 0), (2, 0), (2, 1), (1, 1))


def _grad_w_reduce(xt, dp_parts, dr, win_all, gwo, small, rows_out, row_chunk, tm):
    d_model, seq = xt.shape
    nch, _, cw = win_all.shape
    gx_tiles = seq // tm
    assert nch * tm == seq and nch * cw == seq, "needs S == 8 * tm == 8 * CW"

    def body(xt_hbm, dpa, dpb, dpc, dr_hbm, win_hbm, gwo_ref, small_ref,
             gx_hbm, g_in_o, g_out_o, small_o,
             xt_v, dp_buf, acc, got_in, send_in, recv_in, own_out, got_out, send_out, recv_out, small_all,
             dr_buf, gx_buf,
             xt_sem, dp_sems, loc_sems, d2d_send, d2d_recv, ici_send, ici_recv, sm_send, sm_recv,
             dr_sems, dpx_sems, w_sems, gx_sem):
        x, y, c = _mesh_pos()
        me = 4 * x + 2 * y + c
        sibling = (x, y, 1 - c)
        chips = [(1 - x, 1 - y), (1 - x, y), (x, 1 - y)]
        owners = [(*chip, cc) for chip in chips for cc in (1 - c, c)] + [sibling, (x, y, c)]
        group_of = [4 * o[0] + 2 * o[1] + o[2] for o in owners]
        dp_parts_ = (dpa, dpb, dpc)
        dp_groups = [dp_parts_[arr].at[idx] for arr, idx in _DP_OF_GROUP]

        xt_copy = pltpu.make_async_copy(xt_hbm, xt_v, xt_sem)
        xt_copy.start()

        def dp_start(step):
            for k in range(N_DEV):
                @pl.when(group_of[step] == k)
                def _(k=k):
                    pltpu.make_async_copy(dp_groups[k], dp_buf.at[step % 2], dp_sems.at[step % 2]).start()

        def dp_wait(step):
            pltpu.make_async_copy(dp_groups[0], dp_buf.at[step % 2], dp_sems.at[step % 2]).wait()

        dp_start(0)

        small_all[me] = small_ref[...]
        for d in range(N_DEV):
            @pl.when(d != me)
            def _(d=d):
                pltpu.make_async_remote_copy(
                    src_ref=small_ref, dst_ref=small_all.at[me], send_sem=sm_send.at[d], recv_sem=sm_recv.at[me],
                    device_id=(d // 4, (d // 2) % 2, d % 2), device_id_type=MESH).start()

        def block_out(k):
            return gwo_ref.at[pl.ds(k * rows_out, rows_out), :]

        for k in range(N_DEV):
            s = k // 2

            @pl.when(k % 2 != c)
            def _(k=k, s=s):
                pltpu.make_async_remote_copy(
                    src_ref=block_out(k), dst_ref=got_out.at[s], send_sem=d2d_send.at[1, s],
                    recv_sem=d2d_recv.at[1, s], device_id=sibling, device_id_type=MESH).start()

            @pl.when(k % 2 == c)
            def _(k=k, s=s):
                pltpu.make_async_copy(block_out(k), own_out.at[s], loc_sems.at[s]).start()

        VIA_X, VIA_Y, X_UP, X_LOW, Y_UP, Y_LOW = range(6)
        x_dev, y_dev = (1 - x, y, c), (x, 1 - y, c)
        ici_place = {VIA_X: (0, 0, x_dev), VIA_Y: (0, 1, y_dev), X_UP: (1, 0, x_dev), X_LOW: (1, 1, x_dev),
                     Y_UP: (2, 0, y_dev), Y_LOW: (2, 1, y_dev)}

        def ici_copy(a, k):
            send, recv = ((send_in, recv_in), (send_out, recv_out))[a]
            half = (d_model, rows_out)[a] // 2
            slot, part, to = ici_place[k]
            rows = pl.ds(part * half, half)
            return pltpu.make_async_remote_copy(
                src_ref=send.at[slot, rows, :], dst_ref=recv.at[slot, rows, :], send_sem=ici_send.at[a, k],
                recv_sem=ici_recv.at[a, k], device_id=to, device_id_type=MESH)

        def by_rows(first, n_rows, fn):
            step = min(row_chunk, n_rows)

            def rows_body(r, _):
                fn(pl.ds(pl.multiple_of(first + r * step, step), step))
                return 0

            lax.fori_loop(0, n_rows // step, rows_body, 0)

        def send_chip_sum(a, j, chip_sum):
            send, recv = ((send_in, recv_in), (send_out, recv_out))[a]
            n_rows = (d_model, rows_out)[a]
            half = n_rows // 2

            def plain(rows):
                send[j, rows, :] = chip_sum(rows).astype(BF16)

            if j == 0:
                by_rows(0, n_rows, plain)
                ici_copy(a, VIA_X).start()
                ici_copy(a, VIA_Y).start()
                return
            free, bound, passed = ((0, X_UP), (1, X_LOW), VIA_Y) if j == 1 else ((1, Y_LOW), (0, Y_UP), VIA_X)
            by_rows(free[0] * half, half, plain)
            ici_copy(a, free[1]).start()
            ici_copy(a, passed).wait_recv()

            def with_passed(rows):
                send[j, rows, :] = (chip_sum(rows) + recv[0, rows, :].astype(F32)).astype(BF16)

            by_rows(bound[0] * half, half, with_passed)
            ici_copy(a, bound[1]).start()

        for s in range(4):
            pltpu.make_async_copy(own_out.at[s], own_out.at[s], loc_sems.at[s]).wait()
            pltpu.make_async_remote_copy(
                src_ref=got_out.at[s], dst_ref=got_out.at[s], send_sem=d2d_send.at[1, s], recv_sem=d2d_recv.at[1, s],
                device_id=sibling, device_id_type=MESH).wait()
        for j, chip in enumerate(chips):
            s = 2 * chip[0] + chip[1]
            send_chip_sum(1, j, lambda rows, s=s: own_out[s, rows, :] + got_out[s, rows, :])

        xt_copy.wait()

        def d2d_copy(slot, pair):
            return pltpu.make_async_remote_copy(
                src_ref=acc.at[slot], dst_ref=got_in.at[pair], send_sem=d2d_send.at[0, pair],
                recv_sem=d2d_recv.at[0, pair], device_id=sibling, device_id_type=MESH)

        for step in range(N_DEV):
            slot, pair = step % 2, step // 2
            if step % 2 == 0 and step >= 2:
                d2d_copy(slot, pair - 1).wait_send()

            dp_wait(step)
            if step + 1 < N_DEV:
                dp_start(step + 1)
            acc[slot] = _dot(xt_v[...], dp_buf[step % 2])

            if step % 2 == 0:
                d2d_copy(slot, pair).start()
            else:
                d2d_copy(slot, pair).wait_recv()
                if step < N_DEV - 1:
                    send_chip_sum(0, pair, lambda rows, slot=slot, pair=pair: acc[slot, rows, :] + got_in[pair, rows, :])

        def w_copy(k):
            return pltpu.make_async_copy(win_hbm.at[k], xt_v.at[:, pl.ds(k * cw, cw)], w_sems.at[k])

        for k in range(N_DEV):
            w_copy(k).start()

        def gx_in(tile, buf):
            rows = pl.ds(tile * tm, tm)
            copies = [pltpu.make_async_copy(dr_hbm.at[rows, :], dr_buf.at[buf], dr_sems.at[buf])]
            copies += [pltpu.make_async_copy(dp_groups[k].at[rows, :], dp_buf.at[buf, pl.ds(k * tm, tm), :],
                                             dpx_sems.at[buf, k]) for k in range(N_DEV)]
            return copies

        def gx_out(tile):
            return pltpu.make_async_copy(gx_buf, gx_hbm.at[pl.ds(tile * tm, tm), :], gx_sem)

        for cp in gx_in(0, 0):
            cp.start()
        for k in range(N_DEV):
            w_copy(k).wait()

        def gx_body(tile, _):
            buf = tile % 2
            for cp in gx_in(tile, buf):
                cp.wait()

            @pl.when(tile + 1 < gx_tiles)
            def _():
                for cp in gx_in(tile + 1, 1 - buf):
                    cp.start()

            val = ALPHA * dr_buf[buf]
            for k in range(N_DEV):
                val = val + _dot(dp_buf[buf, k * tm:(k + 1) * tm, :], xt_v[:, k * cw:(k + 1) * cw], NT)

            @pl.when(tile > 0)
            def _():
                gx_out(tile - 1).wait()

            gx_buf[...] = val
            gx_out(tile).start()
            return 0

        lax.fori_loop(0, gx_tiles, gx_body, 0)

        for d in range(N_DEV):
            @pl.when(d != me)
            def _(d=d):
                pltpu.make_async_remote_copy(
                    src_ref=small_ref, dst_ref=small_all.at[d], send_sem=sm_send.at[d], recv_sem=sm_recv.at[d],
                    device_id=(d // 4, (d // 2) % 2, d % 2), device_id_type=MESH).wait()
        total = small_all[0]
        for d in range(1, N_DEV):
            total = total + small_all[d]
        small_o[...] = total

        mine = 2 * x + y
        last = (N_DEV - 1) % 2

        def finish(a, n_rows, chip_sum, g_o):
            recv = (recv_in, recv_out)[a]
            for k in (X_UP, X_LOW, Y_UP, Y_LOW):
                ici_copy(a, k).wait_recv()
            for k in ici_place:
                ici_copy(a, k).wait_send()

            def total_rows(rows):
                g_o[rows, :] = chip_sum(rows) + recv[1, rows, :].astype(F32) + recv[2, rows, :].astype(F32)

            by_rows(0, n_rows, total_rows)

        finish(1, rows_out, lambda rows: own_out[mine, rows, :] + got_out[mine, rows, :], g_out_o)
        finish(0, d_model, lambda rows: acc[last, rows, :] + got_in[N_DEV // 2 - 1, rows, :], g_in_o)
        d2d_copy(0, N_DEV // 2 - 1).wait_send()
        gx_out(0).wait()

    vmem = pl.BlockSpec(memory_space=pltpu.VMEM)
    hbm = pl.BlockSpec(memory_space=pl.ANY)
    return pl.pallas_call(
        body, name="grad_w_reduce",
        in_specs=[hbm] * 7 + [vmem],
        out_specs=(hbm, vmem, vmem, vmem),
        out_shape=(jax.ShapeDtypeStruct((seq, d_model), F32), jax.ShapeDtypeStruct((d_model, cw), F32),
                   jax.ShapeDtypeStruct((rows_out, d_model), F32), jax.ShapeDtypeStruct(small.shape, F32)),
        scratch_shapes=[
            pltpu.VMEM((d_model, seq), BF16), pltpu.VMEM((2, seq, cw), BF16), pltpu.VMEM((2, d_model, cw), F32),
            pltpu.VMEM((4, d_model, cw), F32), pltpu.VMEM((3, d_model, cw), BF16), pltpu.VMEM((3, d_model, cw), BF16),
            pltpu.VMEM((4, rows_out, d_model), F32), pltpu.VMEM((4, rows_out, d_model), F32),
            pltpu.VMEM((3, rows_out, d_model), BF16), pltpu.VMEM((3, rows_out, d_model), BF16),
            pltpu.VMEM((N_DEV,) + small.shape, F32),
            pltpu.VMEM((2, tm, d_model), F32), pltpu.VMEM((tm, d_model), F32),
            pltpu.SemaphoreType.DMA, pltpu.SemaphoreType.DMA((2,)), pltpu.SemaphoreType.DMA((4,)),
            pltpu.SemaphoreType.DMA((2, 4)), pltpu.SemaphoreType.DMA((2, 4)),
            pltpu.SemaphoreType.DMA((2, 6)), pltpu.SemaphoreType.DMA((2, 6)),
            pltpu.SemaphoreType.DMA((N_DEV,)), pltpu.SemaphoreType.DMA((N_DEV,)),
            pltpu.SemaphoreType.DMA((2,)), pltpu.SemaphoreType.DMA((2, N_DEV)), pltpu.SemaphoreType.DMA((N_DEV,)),
            pltpu.SemaphoreType.DMA,
        ],
        compiler_params=_params(56),
    )(xt, *dp_parts, dr, win_all, gwo, small)


def _adamw_update(g, w, m, v, rows):
    n_rows, width = w.shape
    rows = min(rows, n_rows)

    def body(g_ref, w_ref, m_ref, v_ref, d_o, nm_o, nv_o):
        d_o[...], nm_o[...], nv_o[...] = _adamw(w_ref[...], g_ref[...], m_ref[...], v_ref[...])

    tile = pl.BlockSpec((rows, width), lambda i: (i, 0))
    shape = jax.ShapeDtypeStruct(w.shape, F32)
    return pl.pallas_call(
        body, name="adamw_update", grid=(n_rows // rows,), in_specs=[tile] * 4, out_specs=(tile,) * 3,
        out_shape=(shape,) * 3,
    )(g, w, m, v)


def _small_update(grads, weights, ms, vs):
    n = len(grads)

    def body(*refs):
        g_refs, w_refs, m_refs, v_refs = (refs[i * n:(i + 1) * n] for i in range(4))
        outs = refs[4 * n:]
        for i in range(n):
            delta, nm, nv = _adamw(w_refs[i][...], g_refs[i][...], m_refs[i][...], v_refs[i][...])
            outs[3 * i][...] = delta
            outs[3 * i + 1][...] = nm
            outs[3 * i + 2][...] = nv

    vmem = pl.BlockSpec(memory_space=pltpu.VMEM)
    out_shape = []
    for w in weights:
        out_shape += [jax.ShapeDtypeStruct(w.shape, F32)] * 3
    return pl.pallas_call(
        body, name="small_update", in_specs=[vmem] * (4 * n), out_specs=(vmem,) * (3 * n), out_shape=tuple(out_shape),
    )(*grads, *weights, *ms, *vs)


def _tile_sizes(seq):
    return dict(tm=seq // N_DEV, t_ln=min(512, seq), t_attn=min(128, seq), rc=min(256, seq), pairs=4)


def kernel(x, w_in, conv_w, w_out, ln_gain, ln_bias, loss_target, m_w_in, m_conv_w, m_w_out, m_ln_gain, m_ln_bias,
           v_w_in, v_conv_w, v_w_out, v_ln_gain, v_ln_bias):
    assert x.shape[0] == 1 and w_in.shape[0] == 1, "one sequence per device, depth 1"
    _, seq, d_model = x.shape
    cw = w_in.shape[2]
    conv_k, conv_cols = conv_w.shape[1], conv_w.shape[2]
    rows_out = w_out.shape[1]
    assert cw == d_model // 2 and cw % PAIR == 0 and conv_cols * N_DEV == cw and rows_out * N_DEV == d_model
    ts = _tile_sizes(seq)

    x2 = x.reshape(seq, d_model)
    target = loss_target.reshape(seq, d_model)
    me = 4 * lax.axis_index("x") + 2 * lax.axis_index("y") + lax.axis_index("c")

    conv_pad = jnp.pad(conv_w[0], ((0, SUBLANES - conv_k), (0, PAIR - conv_cols)))
    proj, xt, win_all, wout_all, conv_all = _gather_proj(x2, w_in[0], w_out[0], conv_pad, ts["tm"])
    w_out_full = wout_all.reshape(d_model, d_model)
    conv_full = conv_all[:, :conv_k, :conv_cols].transpose(1, 0, 2).reshape(conv_k, cw)
    conv_full = jnp.pad(conv_full, ((0, SUBLANES - conv_k), (0, 0)))

    mix_conv = _conv_fwd(proj, conv_full, ts["rc"])
    pairs = min(ts["pairs"], cw // PAIR)
    tri = _triangles(ts["t_attn"])
    o, mix_attn, tot = _attn_fwd(proj, tri, ts["t_attn"], pairs)
    dr, d_mix_conv, d_mix_attn, gwo, small = _out_ln(mix_conv, mix_attn, x2, target, ln_gain, ln_bias, w_out_full,
                                                     ts["t_ln"])
    dp_conv, d_taps = _conv_bwd(proj, conv_full, d_mix_conv, ts["rc"])
    dp_qz, dp_kv = _attn_bwd(proj, tri, o, tot, d_mix_attn, ts["t_attn"], pairs)
    small = small.at[ROW_CONV:ROW_CONV + conv_k, :cw].set(d_taps[:conv_k])
    grad_x, g_in, g_out, small_sum = _grad_w_reduce(
        xt, (dp_conv, dp_qz, dp_kv), dr, win_all, gwo, small, rows_out, 128, ts["tm"])
    d_in, nm_in, nv_in = _adamw_update(g_in, w_in[0], m_w_in[0], v_w_in[0], 256)
    d_out, nm_out, nv_out = _adamw_update(g_out, w_out[0], m_w_out[0], v_w_out[0], 256)

    loss = small_sum[ROW_LOSS, 0]
    g_gain = small_sum[ROW_GAIN:ROW_GAIN + 1]
    g_bias = small_sum[ROW_BIAS:ROW_BIAS + 1]
    g_conv = lax.dynamic_slice(small_sum, (ROW_CONV, me * conv_cols), (conv_k, conv_cols))
    upd = _small_update((g_conv, g_gain, g_bias), (conv_w[0], ln_gain, ln_bias),
                        (m_conv_w[0], m_ln_gain, m_ln_bias), (v_conv_w[0], v_ln_gain, v_ln_bias))
    d_conv, nm_conv, nv_conv, d_gain, nm_gain, nv_gain, d_bias, nm_bias, nv_bias = upd

    lead = lambda a: a[None]
    return (loss, grad_x.reshape(1, seq, d_model), lead(g_in), lead(g_conv), lead(g_out), g_gain, g_bias,
            lead(d_in), lead(d_conv), lead(d_out), d_gain, d_bias,
            lead(nm_in), lead(nm_conv), lead(nm_out), nm_gain, nm_bias,
            lead(nv_in), lead(nv_conv), lead(nv_out), nv_gain, nv_bias)
```

```python
import jax
import jax.numpy as jnp
from jax import lax
from jax.experimental import pallas as pl
from jax.experimental.pallas import tpu as pltpu

F32 = jnp.float32
BF16 = jnp.bfloat16
MESH = pl.DeviceIdType.MESH

N_DEV = 8
HEAD_DIM = 64
PAIR = 128
SUBLANES = 8
LN_EPS = 1e-5
ALPHA = 2.0 ** 0.25
ADAM_LR, ADAM_B1, ADAM_B2, ADAM_EPS, ADAM_WD, ADAM_STEP = 0.001, 0.9, 0.999, 1e-08, 0.01, 10

ROW_GAIN, ROW_BIAS, ROW_CONV, ROW_LOSS = 0, 1, 2, 5

NT = (((1,), (1,)), ((), ()))
TN = (((0,), (0,)), ((), ()))


V7X_VMEM_BYTES = 64 * 1024 * 1024


def _params(vmem_mib):
    assert vmem_mib * 1024 * 1024 < V7X_VMEM_BYTES
    return pltpu.CompilerParams(vmem_limit_bytes=vmem_mib * 1024 * 1024)


def _dot(a, b, dims=None):
    if dims is None:
        return jnp.dot(a, b, preferred_element_type=F32)
    return lax.dot_general(a, b, dims, preferred_element_type=F32)


def _sigmoid(z):
    return 1.0 / (1.0 + jnp.exp(-z))


def _mesh_pos():
    return lax.axis_index("x"), lax.axis_index("y"), lax.axis_index("c")


def _adamw(w, g, m, v):
    nm = ADAM_B1 * m + (1.0 - ADAM_B1) * g
    nv = ADAM_B2 * v + (1.0 - ADAM_B2) * (g * g)
    m_hat = nm * (1.0 / (1.0 - ADAM_B1 ** ADAM_STEP))
    v_hat = nv * (1.0 / (1.0 - ADAM_B2 ** ADAM_STEP))
    delta = -ADAM_LR * (m_hat / (jnp.sqrt(v_hat) + ADAM_EPS) + ADAM_WD * w)
    return delta, nm, nv


def _gather_proj(x, w_in_s, w_out_s, conv_s, tm):
    seq, d_model = x.shape
    cw = w_in_s.shape[1]
    rows_out = w_out_s.shape[0]
    n_tiles = seq // tm
    half = d_model // 2
    SIB, X_UP, X_LOW, Y_UP, Y_LOW, VIA_Y, VIA_X, ON_X, ON_Y, ON_DIAG = range(10)

    def body(x_hbm, win_ref, wout_ref, conv_ref, proj_hbm, xt_hbm, win_all, wout_all, conv_all,
             xb, x_stage, o_stage, xt_stage, x_sems, o_sems, xt_sems, w_send, w_recv, send_sems, recv_sems):
        x, y, c = _mesh_pos()
        me = (x, y, c)
        sibling = (x, y, 1 - c)
        x_nbr, y_nbr, diag = (1 - x, y), (x, 1 - y), (1 - x, 1 - y)
        chips = [x_nbr, y_nbr, diag]
        small = (wout_all, conv_all)

        def slot(pos):
            return 4 * pos[0] + 2 * pos[1] + pos[2]

        def x_copy(tile, buf):
            return pltpu.make_async_copy(x_hbm.at[pl.ds(tile * tm, tm), :], x_stage.at[buf], x_sems.at[buf])

        x_copy(0, 0).start()
        win_all[slot(me)] = win_ref[...].astype(BF16)
        wout_all[slot(me)] = wout_ref[...].astype(BF16)
        conv_all[slot(me)] = conv_ref[...]

        def w_copy(k, block, part, to):
            ref = win_all.at[slot(block)]
            if part is not None:
                ref = ref.at[pl.ds(part * half, half), :]
            return pltpu.make_async_remote_copy(
                src_ref=ref, dst_ref=ref, send_sem=w_send.at[k], recv_sem=w_recv.at[k],
                device_id=to, device_id_type=MESH)

        def copy(a, k, block, to):
            ref = small[a].at[slot(block)]
            return pltpu.make_async_remote_copy(
                src_ref=ref, dst_ref=ref, send_sem=send_sems.at[a, k], recv_sem=recv_sems.at[a, k],
                device_id=to, device_id_type=MESH)

        sends = [w_copy(SIB, me, None, sibling),
                 w_copy(X_UP, me, 0, (*x_nbr, c)), w_copy(Y_LOW, me, 1, (*y_nbr, c)),
                 w_copy(X_LOW, me, 1, (*x_nbr, c)), w_copy(Y_UP, me, 0, (*y_nbr, c))]
        for a in range(len(small)):
            sends.append(copy(a, 0, me, sibling))
            sends += [copy(a, 1 + j, me, (*chip, c)) for j, chip in enumerate(chips)]
        for cp in sends:
            cp.start()

        def o_copy(group, tile, buf):
            return pltpu.make_async_copy(o_stage.at[buf], proj_hbm.at[group, pl.ds(tile * tm, tm), :], o_sems.at[buf])

        def xt_copy(tile, buf):
            return pltpu.make_async_copy(xt_stage.at[buf], xt_hbm.at[:, pl.ds(tile * tm, tm)], xt_sems.at[buf])

        def project(order, group, first_pass):
            def tile_body(tile, _):
                if first_pass:
                    buf = tile % 2
                    x_copy(tile, buf).wait()

                    @pl.when(tile + 1 < n_tiles)
                    def _():
                        x_copy(tile + 1, 1 - buf).start()

                    xv = x_stage[buf]
                    xb[tile] = xv.astype(BF16)

                    @pl.when(tile >= 2)
                    def _():
                        xt_copy(tile - 2, buf).wait()

                    xt_stage[buf] = xv.T.astype(BF16)
                    xt_copy(tile, buf).start()
                count = order * n_tiles + tile
                obuf = count % 2

                @pl.when(count >= 2)
                def _():
                    o_copy(group, tile, obuf).wait()

                o_stage[obuf] = _dot(xb[tile], win_all[group]).astype(BF16)
                o_copy(group, tile, obuf).start()
                return 0

            lax.fori_loop(0, n_tiles, tile_body, 0)

        def start(cp):
            cp.start()
            sends.append(cp)

        def small_pass_on(j):
            for a in range(len(small)):
                copy(a, 1 + j, (*chips[j], c), me).wait_recv()
                start(copy(a, 4 + j, (*chips[j], c), sibling))

        def small_from_sibling(k):
            for a in range(len(small)):
                copy(a, k, sibling, me).wait_recv()

        project(0, slot(me), True)
        w_copy(SIB, sibling, None, me).wait_recv()
        small_from_sibling(0)
        project(1, slot(sibling), False)
        w_copy(X_UP, (*x_nbr, c), 0, me).wait_recv()
        start(w_copy(VIA_Y, (*x_nbr, c), 0, (*y_nbr, c)))
        w_copy(Y_LOW, (*y_nbr, c), 1, me).wait_recv()
        start(w_copy(VIA_X, (*y_nbr, c), 1, (*x_nbr, c)))
        w_copy(X_LOW, (*x_nbr, c), 1, me).wait_recv()
        start(w_copy(ON_X, (*x_nbr, c), None, sibling))
        small_pass_on(0)
        project(2, slot((*x_nbr, c)), False)
        w_copy(Y_UP, (*y_nbr, c), 0, me).wait_recv()
        start(w_copy(ON_Y, (*y_nbr, c), None, sibling))
        small_pass_on(1)
        project(3, slot((*y_nbr, c)), False)
        w_copy(ON_X, (*x_nbr, 1 - c), None, me).wait_recv()
        small_from_sibling(4)
        project(4, slot((*x_nbr, 1 - c)), False)
        w_copy(ON_Y, (*y_nbr, 1 - c), None, me).wait_recv()
        small_from_sibling(5)
        project(5, slot((*y_nbr, 1 - c)), False)
        w_copy(VIA_Y, (*diag, c), 0, me).wait_recv()
        w_copy(VIA_X, (*diag, c), 1, me).wait_recv()
        start(w_copy(ON_DIAG, (*diag, c), None, sibling))
        small_pass_on(2)
        project(6, slot((*diag, c)), False)
        w_copy(ON_DIAG, (*diag, 1 - c), None, me).wait_recv()
        small_from_sibling(6)
        project(7, slot((*diag, 1 - c)), False)

        for buf in range(2):
            o_copy(0, 0, buf).wait()
        for buf in range(min(2, n_tiles)):
            xt_copy(0, buf).wait()
        for cp in sends:
            cp.wait_send()

    vmem = pl.BlockSpec(memory_space=pltpu.VMEM)
    hbm = pl.BlockSpec(memory_space=pl.ANY)
    return pl.pallas_call(
        body, name="gather_proj",
        out_shape=(jax.ShapeDtypeStruct((N_DEV, seq, cw), BF16),
                   jax.ShapeDtypeStruct((d_model, seq), BF16),
                   jax.ShapeDtypeStruct((N_DEV, d_model, cw), BF16),
                   jax.ShapeDtypeStruct((N_DEV, rows_out, d_model), BF16),
                   jax.ShapeDtypeStruct((N_DEV,) + conv_s.shape, F32)),
        in_specs=[hbm, vmem, vmem, vmem], out_specs=(hbm, hbm, vmem, vmem, vmem),
        scratch_shapes=[
            pltpu.VMEM((n_tiles, tm, d_model), BF16), pltpu.VMEM((2, tm, d_model), F32),
            pltpu.VMEM((2, tm, cw), BF16), pltpu.VMEM((2, d_model, tm), BF16),
            pltpu.SemaphoreType.DMA((2,)), pltpu.SemaphoreType.DMA((2,)), pltpu.SemaphoreType.DMA((2,)),
            pltpu.SemaphoreType.DMA((10,)), pltpu.SemaphoreType.DMA((10,)),
            pltpu.SemaphoreType.DMA((2, 7)), pltpu.SemaphoreType.DMA((2, 7))],
        compiler_params=_params(48),
    )(x, w_in_s, w_out_s, conv_s)


def _conv_taps(ext, w_ref, rc):
    u0 = ext[SUBLANES:SUBLANES + rc]
    u1 = pltpu.roll(ext, 1, 0)[SUBLANES:SUBLANES + rc]
    u2 = pltpu.roll(ext, 2, 0)[SUBLANES:SUBLANES + rc]
    return w_ref[2:3, :] * u0 + w_ref[1:2, :] * u1 + w_ref[0:1, :] * u2, u0, u1, u2


def _conv_fwd(proj, conv_full, rc):
    _, seq, cw = proj.shape

    def body(b_ref, c_ref, h_ref, z_ref, w_ref, o_ref, u_scr):
        u_scr[0:SUBLANES, :] = jnp.zeros((SUBLANES, PAIR), F32)

        def fill(r, _):
            base = pl.multiple_of(r * rc, rc)
            rows = pl.ds(base, rc)
            u_scr[pl.ds(base + SUBLANES, rc), :] = c_ref[rows, :].astype(F32) * h_ref[rows, :].astype(F32)
            return 0

        lax.fori_loop(0, seq // rc, fill, 0)

        def out(r, _):
            base = pl.multiple_of(r * rc, rc)
            rows = pl.ds(base, rc)
            ext = u_scr[pl.ds(base, rc + SUBLANES), :]
            y, _, _, _ = _conv_taps(ext, w_ref, rc)
            z = z_ref[rows, :].astype(F32)
            o_ref[rows, :] = (z * _sigmoid(z) * b_ref[rows, :].astype(F32) * y).astype(BF16)
            return 0

        lax.fori_loop(0, seq // rc, out, 0)

    def chunk(j):
        return pl.BlockSpec((None, seq, PAIR), lambda cb, j=j: (j, 0, cb))

    return pl.pallas_call(
        body, name="conv_fwd", grid=(cw // PAIR,),
        in_specs=[chunk(0), chunk(1), chunk(2), chunk(3), pl.BlockSpec((SUBLANES, PAIR), lambda cb: (0, cb))],
        out_specs=pl.BlockSpec((seq, PAIR), lambda cb: (0, cb)),
        out_shape=jax.ShapeDtypeStruct((seq, cw), BF16),
        scratch_shapes=[pltpu.VMEM((seq + SUBLANES, PAIR), F32)],
    )(proj, proj, proj, proj, conv_full)


def _conv_bwd(proj, conv_full, d_mix_conv, rc):
    _, seq, cw = proj.shape

    def body(b_ref, c_ref, h_ref, z_ref, w_ref, g_ref, dp_ref, dw_ref, u_scr, dy_scr):
        u_scr[0:SUBLANES, :] = jnp.zeros((SUBLANES, PAIR), F32)
        dy_scr[seq:seq + SUBLANES, :] = jnp.zeros((SUBLANES, PAIR), F32)

        def fill(r, _):
            base = pl.multiple_of(r * rc, rc)
            rows = pl.ds(base, rc)
            u_scr[pl.ds(base + SUBLANES, rc), :] = c_ref[rows, :].astype(F32) * h_ref[rows, :].astype(F32)
            return 0

        lax.fori_loop(0, seq // rc, fill, 0)

        def gate(r, acc):
            base = pl.multiple_of(r * rc, rc)
            rows = pl.ds(base, rc)
            ext = u_scr[pl.ds(base, rc + SUBLANES), :]
            y, u0, u1, u2 = _conv_taps(ext, w_ref, rc)
            z = z_ref[rows, :].astype(F32)
            b = b_ref[rows, :].astype(F32)
            g = g_ref[rows, :].astype(F32)
            sig = _sigmoid(z)
            dp_ref[3, rows, :] = (g * b * y * (sig * (1.0 + z * (1.0 - sig)))).astype(BF16)
            gs = g * (z * sig)
            dp_ref[0, rows, :] = (gs * y).astype(BF16)
            dy = gs * b
            dy_scr[rows, :] = dy
            a0, a1, a2 = acc
            return (a0 + jnp.sum(dy * u2, axis=0, keepdims=True),
                    a1 + jnp.sum(dy * u1, axis=0, keepdims=True),
                    a2 + jnp.sum(dy * u0, axis=0, keepdims=True))

        zero = jnp.zeros((1, PAIR), F32)
        a0, a1, a2 = lax.fori_loop(0, seq // rc, gate, (zero, zero, zero))
        dw_ref[...] = jnp.zeros((SUBLANES, PAIR), F32)
        dw_ref[0:1, :] = a0
        dw_ref[1:2, :] = a1
        dw_ref[2:3, :] = a2

        def back(r, _):
            base = pl.multiple_of(r * rc, rc)
            rows = pl.ds(base, rc)
            ext = dy_scr[pl.ds(base, rc + SUBLANES), :]
            n = rc + SUBLANES
            d0 = ext[0:rc]
            d1 = pltpu.roll(ext, n - 1, 0)[0:rc]
            d2 = pltpu.roll(ext, n - 2, 0)[0:rc]
            du = w_ref[2:3, :] * d0 + w_ref[1:2, :] * d1 + w_ref[0:1, :] * d2
            dp_ref[1, rows, :] = (du * h_ref[rows, :].astype(F32)).astype(BF16)
            dp_ref[2, rows, :] = (du * c_ref[rows, :].astype(F32)).astype(BF16)
            return 0

        lax.fori_loop(0, seq // rc, back, 0)

    def chunk(j):
        return pl.BlockSpec((None, seq, PAIR), lambda cb, j=j: (j, 0, cb))

    return pl.pallas_call(
        body, name="conv_bwd", grid=(cw // PAIR,),
        in_specs=[chunk(0), chunk(1), chunk(2), chunk(3), pl.BlockSpec((SUBLANES, PAIR), lambda cb: (0, cb)),
                  pl.BlockSpec((seq, PAIR), lambda cb: (0, cb))],
        out_specs=(pl.BlockSpec((4, seq, PAIR), lambda cb: (0, 0, cb)),
                   pl.BlockSpec((SUBLANES, PAIR), lambda cb: (0, cb))),
        out_shape=(jax.ShapeDtypeStruct((4, seq, cw), BF16), jax.ShapeDtypeStruct((SUBLANES, cw), F32)),
        scratch_shapes=[pltpu.VMEM((seq + SUBLANES, PAIR), F32), pltpu.VMEM((seq + SUBLANES, PAIR), F32)],
    )(proj, proj, proj, proj, conv_full, d_mix_conv)


SKIP_CARRY = 104.0
LOG2_E = 1.4426950408889634
LANE_TOT0, LANE_TOT1, LANE_FIRST, LANE_WHOLE = 0, 1, 2, 3
FAST_BLOCKS = 3
EARLY_ROWS = 32


def _triangles(t):
    row = lax.broadcasted_iota(jnp.int32, (2 * t, 2 * t), 0)
    col = lax.broadcasted_iota(jnp.int32, (2 * t, 2 * t), 1)
    same = (row < t) == (col < t)
    upper = jnp.logical_and(same, row > col).astype(BF16)
    lower = jnp.logical_and(same, row < col).astype(BF16)
    return jnp.stack([jnp.concatenate([upper, upper], axis=0), jnp.concatenate([lower, lower], axis=0)])


def _pair_masks(t):
    lane = lax.broadcasted_iota(jnp.int32, (t, PAIR), 1)
    qrow = lax.broadcasted_iota(jnp.int32, (t, 2 * t), 0)
    kcol = lax.broadcasted_iota(jnp.int32, (t, 2 * t), 1)
    strict = jnp.where(kcol < t, kcol, kcol - t) < qrow
    return lane, lane < HEAD_DIM, strict


def _by_head(x, head0):
    zero = jnp.zeros_like(x)
    return jnp.concatenate([jnp.where(head0, x, zero), jnp.where(head0, zero, x)], axis=0)


def _hi_lo(a):
    hi = a.astype(BF16)
    lo = (a - hi.astype(F32)).astype(BF16)
    return jnp.concatenate([hi, lo], axis=1)


def _softplus_parts(z, strict, masked):
    spu = jnp.maximum(z, 0.0) + jnp.log(1.0 + jnp.exp2(jnp.abs(z) * -LOG2_E))
    return z - spu, (jnp.where(strict, spu, 0.0) if masked else spu)


def _stacked_dot(parts, rhs):
    out = _dot(jnp.concatenate(parts, axis=0), rhs)
    ends = [0]
    for p in parts:
        ends.append(ends[-1] + p.shape[0])
    return [out[a:b] for a, b in zip(ends[:-1], ends[1:])]


def _splice(whole, rows, part):
    pieces = ([whole[:rows[0]]] if rows[0] > 0 else []) + [part]
    if rows[1] < whole.shape[0]:
        pieces.append(whole[rows[1]:])
    return part if len(pieces) == 1 else jnp.concatenate(pieces, axis=0)


def _attn_fwd(proj, tri, t, pp):
    _, seq, cw = proj.shape
    scale = HEAD_DIM ** -0.5
    width = pp * PAIR

    def body(q_ref, k_ref, v_ref, za_ref, tri_ref, o_ref, mix_ref, tot_ref, acc_ref):
        i = pl.program_id(1)
        lane, head0, strict = _pair_masks(t)
        upper = tri_ref[0]
        q = q_ref[...] * scale
        acc_ref[...] = jnp.zeros_like(acc_ref)

        def sweep(blocks, state):
            staged = []
            for j, masked, rows in blocks:
                start = pl.multiple_of(j * t, t)
                kb = k_ref[pl.ds(start, t), :]
                vb = v_ref[pl.ds(start, t), :]
                for p in range(pp):
                    cols = slice(p * PAIR, (p + 1) * PAIR)
                    z = _dot(q[rows[0]:rows[1], cols], _by_head(kb[:, cols], head0), NT)
                    ls, sp = _softplus_parts(z, strict[rows[0]:rows[1]], masked)
                    staged.append((p, masked, rows, ls, sp, _by_head(vb[:, cols], head0)))
            afters = _stacked_dot([_hi_lo(sp) for _, _, _, _, sp, _ in staged], upper)
            state = list(state)
            for (p, masked, rows, ls, sp, v2), after in zip(staged, afters):
                c0, c1 = state[p]
                part = slice(rows[0], rows[1])
                x = ls - after
                w = jnp.exp(jnp.concatenate([x[:, :t] - c0[part], x[:, t:] - c1[part]], axis=1))
                if masked:
                    w = jnp.where(strict[part], w, 0.0)
                acc_ref[part, p * PAIR:(p + 1) * PAIR] += _dot(w.astype(BF16), v2)
                state[p] = (_splice(c0, rows, c0[part] + (after[:, 0:1] + sp[:, 0:1])),
                            _splice(c1, rows, c1[part] + (after[:, t:t + 1] + sp[:, t:t + 1])))
            return tuple(state)

        def unfinished(state, rows):
            m = state[0][0]
            for p in range(pp):
                m = jnp.minimum(m, jnp.minimum(state[p][0], state[p][1]))
            return jnp.min(m[rows[0]:rows[1]]) < SKIP_CARRY

        every = (0, t)
        early, late = (0, min(EARLY_ROWS, t)), (min(EARLY_ROWS, t), t)

        def step(js):
            state = sweep(((js[0], False, every),), js[1])
            return js[0] - 1, state, unfinished(state, every)

        def fast():
            blocks = tuple((i - b, b == 0, every) for b in range(FAST_BLOCKS - 1))
            state = sweep(blocks + ((i - (FAST_BLOCKS - 1), False, early),), init)
            if late[0] == late[1]:
                return state, jnp.bool_(True)
            whole = unfinished(state, late)
            state = lax.cond(whole, lambda: sweep(((i - (FAST_BLOCKS - 1), False, late),), state), lambda: state)
            return state, whole

        zcol = jnp.zeros((t, 1), F32)
        init = tuple((zcol, zcol) for _ in range(pp))
        many = i >= FAST_BLOCKS - 1
        state, whole = lax.cond(many, fast, lambda: (sweep(((i, True, every),), init), jnp.bool_(True)))
        j_end, state, _ = lax.while_loop(
            lambda js: jnp.logical_and(js[0] >= 0, js[2]), step,
            (jnp.where(many, i - FAST_BLOCKS, i - 1), state, unfinished(state, every)))
        first = (j_end + 1).astype(F32)
        notes = jnp.where(lane == LANE_FIRST, first, whole.astype(F32))
        za = za_ref[...].astype(F32)
        for p in range(pp):
            c0, c1 = state[p]
            cols = slice(p * PAIR, (p + 1) * PAIR)
            zp = za[:, cols]
            acc = acc_ref[:, cols]
            o_ref[:, cols] = acc.astype(BF16)
            mix_ref[:, cols] = (zp * _sigmoid(zp) * acc).astype(BF16)
            tot_ref[:, cols] = jnp.where(lane == LANE_TOT0, c0, jnp.where(lane == LANE_TOT1, c1, notes))

    def tile(j):
        return pl.BlockSpec((None, t, width), lambda g, i, j=j: (j, i, g))

    def full(j):
        return pl.BlockSpec((None, seq, width), lambda g, i, j=j: (j, 0, g))

    out_tile = pl.BlockSpec((t, width), lambda g, i: (i, g))
    return pl.pallas_call(
        body, name="attn_fwd", grid=(cw // width, seq // t),
        in_specs=[tile(4), full(5), full(6), tile(7), pl.BlockSpec(tri.shape, lambda g, i: (0, 0, 0))],
        out_specs=(out_tile, out_tile, out_tile),
        out_shape=(jax.ShapeDtypeStruct((seq, cw), BF16), jax.ShapeDtypeStruct((seq, cw), BF16),
                   jax.ShapeDtypeStruct((seq, cw), F32)),
        scratch_shapes=[pltpu.VMEM((t, width), F32)],
    )(proj, proj, proj, proj, tri)


def _attn_bwd(proj, tri, o, tot, d_mix_attn, t, pp):
    _, seq, cw = proj.shape
    nb = seq // t
    scale = HEAD_DIM ** -0.5
    width = pp * PAIR

    def body(q_ref, k_ref, v_ref, za_ref, tri_ref, o_ref, tot_ref, g_ref, dqz_ref, dkv_ref, dk_acc, dv_acc, dq_acc):
        i = pl.program_id(1)
        dq_acc[...] = jnp.zeros_like(dq_acc)

        @pl.when(i == 0)
        def _():
            dk_acc[...] = jnp.zeros_like(dk_acc)
            dv_acc[...] = jnp.zeros_like(dv_acc)

        _, head0, strict = _pair_masks(t)
        upper, lower = tri_ref[0], tri_ref[1, 0:2 * t, :]
        za = za_ref[...].astype(F32)
        g = g_ref[...].astype(F32)
        sig = _sigmoid(za)
        dqz_ref[1] = (g * o_ref[...].astype(F32) * (sig * (1.0 + za * (1.0 - sig)))).astype(BF16)
        do = (g * (za * sig)).astype(BF16)
        q = q_ref[...] * scale
        tot_v = tot_ref[...]
        q2, do2, init = [], [], []
        zcol = jnp.zeros((t, 1), F32)
        for p in range(pp):
            cols = slice(p * PAIR, (p + 1) * PAIR)
            q2.append(_by_head(q[:, cols], head0))
            do2.append(_by_head(do[:, cols], head0))
            tp = tot_v[:, cols]
            init.append(((tp[:, LANE_TOT0:LANE_TOT0 + 1], tp[:, LANE_TOT1:LANE_TOT1 + 1]), (zcol, zcol)))
        first = jnp.clip(tot_v[0:1, LANE_FIRST:LANE_FIRST + 1], 0.0, i.astype(F32)).astype(jnp.int32)[0, 0]

        def sweep(blocks, state):
            staged = []
            for j, masked, rows in blocks:
                start = pl.multiple_of(j * t, t)
                kb = k_ref[pl.ds(start, t), :]
                vb = v_ref[pl.ds(start, t), :]
                part = slice(rows[0], rows[1])
                for p in range(pp):
                    cols = slice(p * PAIR, (p + 1) * PAIR)
                    k2 = _by_head(kb[:, cols], head0)
                    z = _dot(q[part, cols], k2, NT)
                    ls, sp = _softplus_parts(z, strict[part], masked)
                    da = _dot(do[part, cols], _by_head(vb[:, cols], head0), NT)
                    staged.append((p, masked, rows, k2, ls, sp, da))
            afters = _stacked_dot([_hi_lo(sp) for _, _, _, _, _, sp, _ in staged], upper)
            state = list(state)
            weights, ggs = [], []
            for (p, masked, rows, k2, ls, sp, da), after in zip(staged, afters):
                (s0, s1), befores = state[p]
                part = slice(rows[0], rows[1])
                n0 = s0[part] - (after[:, 0:1] + sp[:, 0:1])
                n1 = s1[part] - (after[:, t:t + 1] + sp[:, t:t + 1])
                x = ls - after
                a = jnp.exp(jnp.concatenate([x[:, :t] - n0, x[:, t:] - n1], axis=1))
                if masked:
                    a = jnp.where(strict[part], a, 0.0)
                state[p] = ((_splice(s0, rows, n0), _splice(s1, rows, n1)), befores)
                weights.append(a.astype(BF16))
                ggs.append(a * da)
            pres = _stacked_dot([gg.astype(BF16) for gg in ggs], lower)
            dzs = []
            for (p, masked, rows, k2, ls, sp, da), gg, pre in zip(staged, ggs, pres):
                rests, (b0, b1) = state[p]
                part = slice(rows[0], rows[1])
                y = gg + pre
                dz = gg - jnp.exp(ls) * jnp.concatenate([y[:, :t] + b0[part], y[:, t:] + b1[part]], axis=1)
                if masked:
                    dz = jnp.where(strict[part], dz, 0.0)
                dzb = dz.astype(BF16)
                dzs.append(dzb)
                dq_acc[part, p * PAIR:(p + 1) * PAIR] += _dot(dzb, k2)
                state[p] = (rests, (_splice(b0, rows, b0[part] + y[:, t - 1:t]),
                                    _splice(b1, rows, b1[part] + y[:, 2 * t - 1:2 * t])))
            first_row = pl.multiple_of(blocks[0][0] * t, t)
            n_rows = len(blocks) * t
            for p in range(pp):
                cols = slice(p * PAIR, (p + 1) * PAIR)

                def by_key(tiles):
                    out = []
                    for n, m in tiles:
                        rows = staged[n][2]
                        m = _splice(jnp.zeros((t, 2 * t), BF16), rows, m)
                        out.append(jnp.concatenate([m[:, :t], m[:, t:]], axis=0).T)
                    return jnp.concatenate(out, axis=0)

                mine = [n for n in range(len(staged)) if staged[n][0] == p]
                dk_acc[pl.ds(first_row, n_rows), cols] += _dot(by_key([(n, dzs[n]) for n in mine]), q2[p])
                dv_acc[pl.ds(first_row, n_rows), cols] += _dot(by_key([(n, weights[n]) for n in mine]), do2[p])
            return tuple(state)

        every = (0, t)
        early = (0, min(EARLY_ROWS, t))
        many = i >= FAST_BLOCKS - 1
        last_single = jnp.where(many, i - (FAST_BLOCKS - 1), i)
        state = lax.fori_loop(first, last_single, lambda j, s: sweep(((j, False, every),), s), tuple(init))

        def fast(rows):
            blocks = tuple((i - b, b == 0, every) for b in range(FAST_BLOCKS - 2, -1, -1))
            return sweep(((i - (FAST_BLOCKS - 1), False, rows),) + blocks, state)

        whole = tot_v[0:1, LANE_WHOLE:LANE_WHOLE + 1].astype(jnp.int32)[0, 0] > 0
        state = lax.cond(
            many,
            lambda: lax.cond(whole, lambda: fast(every), lambda: fast(early)),
            lambda: sweep(((i, True, every),), state))
        dqz_ref[0] = (dq_acc[...] * scale).astype(BF16)

        @pl.when(i == nb - 1)
        def _():
            dkv_ref[0] = dk_acc[...].astype(BF16)
            dkv_ref[1] = dv_acc[...].astype(BF16)

    def tile(j):
        return pl.BlockSpec((None, t, width), lambda g, i, j=j: (j, i, g))

    def full(j):
        return pl.BlockSpec((None, seq, width), lambda g, i, j=j: (j, 0, g))

    flat_tile = pl.BlockSpec((t, width), lambda g, i: (i, g))
    return pl.pallas_call(
        body, name="attn_bwd", grid=(cw // width, nb),
        in_specs=[tile(4), full(5), full(6), tile(7), pl.BlockSpec(tri.shape, lambda g, i: (0, 0, 0)),
                  flat_tile, flat_tile, flat_tile],
        out_specs=(pl.BlockSpec((2, t, width), lambda g, i: (0, i, g)),
                   pl.BlockSpec((2, seq, width), lambda g, i: (0, 0, g))),
        out_shape=(jax.ShapeDtypeStruct((2, seq, cw), BF16), jax.ShapeDtypeStruct((2, seq, cw), BF16)),
        scratch_shapes=[pltpu.VMEM((seq, width), F32), pltpu.VMEM((seq, width), F32), pltpu.VMEM((t, width), F32)],
        compiler_params=_params(48),
    )(proj, proj, proj, proj, tri, o, tot, d_mix_attn)


def _out_ln(mix_conv, mix_attn, x, target, gain, bias, w_out, tm):
    seq, d_model = x.shape
    cw = mix_conv.shape[1]
    inv_d = 1.0 / d_model

    def body(mc_ref, ma_ref, x_ref, t_ref, gain_ref, bias_ref, w_ref, dr_ref, dmc_ref, dma_ref, gwo_ref, small_ref):
        @pl.when(pl.program_id(0) == 0)
        def _():
            gwo_ref[...] = jnp.zeros_like(gwo_ref)
            small_ref[...] = jnp.zeros_like(small_ref)

        mix = jnp.concatenate([mc_ref[...], ma_ref[...]], axis=1)
        w = w_ref[...]
        r = ALPHA * x_ref[...] + _dot(mix, w)
        mu = jnp.sum(r, axis=1, keepdims=True) * inv_d
        xc = r - mu
        var = jnp.sum(xc * xc, axis=1, keepdims=True) * inv_d
        rstd = lax.rsqrt(var + LN_EPS)
        xhat = xc * rstd
        gain_v = gain_ref[...]
        err = xhat * gain_v + bias_ref[...] - t_ref[...]
        row_loss = jnp.sum(err * err, axis=1, keepdims=True)
        loss = (0.5 * inv_d) * jnp.sum(row_loss, axis=0, keepdims=True)
        dy = err * inv_d
        small_ref[ROW_GAIN:ROW_GAIN + 1, :] += jnp.sum(dy * xhat, axis=0, keepdims=True)
        small_ref[ROW_BIAS:ROW_BIAS + 1, :] += jnp.sum(dy, axis=0, keepdims=True)
        small_ref[ROW_LOSS:ROW_LOSS + 1, :] += jnp.broadcast_to(loss, (1, d_model))
        dxhat = dy * gain_v
        m1 = jnp.sum(dxhat, axis=1, keepdims=True) * inv_d
        m2 = jnp.sum(dxhat * xhat, axis=1, keepdims=True) * inv_d
        dr = rstd * (dxhat - m1 - xhat * m2)
        dr_ref[...] = dr
        drb = dr.astype(BF16)
        dmix = _dot(drb, w, NT)
        dmc_ref[...] = dmix[:, :cw].astype(BF16)
        dma_ref[...] = dmix[:, cw:].astype(BF16)
        gwo_ref[...] += _dot(mix, drb, TN)

    def rows(width):
        return pl.BlockSpec((tm, width), lambda i: (i, 0))

    def whole(shape):
        return pl.BlockSpec(shape, lambda i: (0, 0))

    return pl.pallas_call(
        body, name="out_ln", grid=(seq // tm,),
        in_specs=[rows(cw), rows(cw), rows(d_model), rows(d_model), whole((1, d_model)), whole((1, d_model)),
                  whole((d_model, d_model))],
        out_specs=(rows(d_model), rows(cw), rows(cw), whole((d_model, d_model)), whole((SUBLANES, d_model))),
        out_shape=(jax.ShapeDtypeStruct((seq, d_model), F32), jax.ShapeDtypeStruct((seq, cw), BF16),
                   jax.ShapeDtypeStruct((seq, cw), BF16), jax.ShapeDtypeStruct((d_model, d_model), F32),
                   jax.ShapeDtypeStruct((SUBLANES, d_model), F32)),
        compiler_params=_params(48),
    )(mix_conv, mix_attn, x, target, gain, bias, w_out)


_DP_OF_GROUP = ((0, 0), (0, 1), (0, 2), (0, 3), (1, 0), (2, 0), (2, 1), (1, 1))


def _grad_w_reduce(xt, dp_parts, dr, win_all, gwo, small, rows_out, row_chunk, tm):
    d_model, seq = xt.shape
    nch, _, cw = win_all.shape
    gx_tiles = seq // tm
    assert nch * tm == seq and nch * cw == seq, "needs S == 8 * tm == 8 * CW"

    def body(xt_hbm, dpa, dpb, dpc, dr_hbm, win_hbm, gwo_ref, small_ref,
             gx_hbm, g_in_o, g_out_o, small_o,
             xt_v, dp_buf, acc, got_in, send_in, recv_in, own_out, got_out, send_out, recv_out, small_all,
             dr_buf, gx_buf,
             xt_sem, dp_sems, loc_sems, d2d_send, d2d_recv, ici_send, ici_recv, sm_send, sm_recv,
             dr_sems, dpx_sems, w_sems, gx_sem):
        x, y, c = _mesh_pos()
        me = 4 * x + 2 * y + c
        sibling = (x, y, 1 - c)
        chips = [(1 - x, 1 - y), (1 - x, y), (x, 1 - y)]
        owners = [(*chip, cc) for chip in chips for cc in (1 - c, c)] + [sibling, (x, y, c)]
        group_of = [4 * o[0] + 2 * o[1] + o[2] for o in owners]
        dp_parts_ = (dpa, dpb, dpc)
        dp_groups = [dp_parts_[arr].at[idx] for arr, idx in _DP_OF_GROUP]

        xt_copy = pltpu.make_async_copy(xt_hbm, xt_v, xt_sem)
        xt_copy.start()

        def dp_start(step):
            for k in range(N_DEV):
                @pl.when(group_of[step] == k)
                def _(k=k):
                    pltpu.make_async_copy(dp_groups[k], dp_buf.at[step % 2], dp_sems.at[step % 2]).start()

        def dp_wait(step):
            pltpu.make_async_copy(dp_groups[0], dp_buf.at[step % 2], dp_sems.at[step % 2]).wait()

        dp_start(0)

        small_all[me] = small_ref[...]
        for d in range(N_DEV):
            @pl.when(d != me)
            def _(d=d):
                pltpu.make_async_remote_copy(
                    src_ref=small_ref, dst_ref=small_all.at[me], send_sem=sm_send.at[d], recv_sem=sm_recv.at[me],
                    device_id=(d // 4, (d // 2) % 2, d % 2), device_id_type=MESH).start()

        def block_out(k):
            return gwo_ref.at[pl.ds(k * rows_out, rows_out), :]

        for k in range(N_DEV):
            s = k // 2

            @pl.when(k % 2 != c)
            def _(k=k, s=s):
                pltpu.make_async_remote_copy(
                    src_ref=block_out(k), dst_ref=got_out.at[s], send_sem=d2d_send.at[1, s],
                    recv_sem=d2d_recv.at[1, s], device_id=sibling, device_id_type=MESH).start()

            @pl.when(k % 2 == c)
            def _(k=k, s=s):
                pltpu.make_async_copy(block_out(k), own_out.at[s], loc_sems.at[s]).start()

        VIA_X, VIA_Y, X_UP, X_LOW, Y_UP, Y_LOW = range(6)
        x_dev, y_dev = (1 - x, y, c), (x, 1 - y, c)
        ici_place = {VIA_X: (0, 0, x_dev), VIA_Y: (0, 1, y_dev), X_UP: (1, 0, x_dev), X_LOW: (1, 1, x_dev),
                     Y_UP: (2, 0, y_dev), Y_LOW: (2, 1, y_dev)}

        def ici_copy(a, k):
            send, recv = ((send_in, recv_in), (send_out, recv_out))[a]
            half = (d_model, rows_out)[a] // 2
            slot, part, to = ici_place[k]
            rows = pl.ds(part * half, half)
            return pltpu.make_async_remote_copy(
                src_ref=send.at[slot, rows, :], dst_ref=recv.at[slot, rows, :], send_sem=ici_send.at[a, k],
                recv_sem=ici_recv.at[a, k], device_id=to, device_id_type=MESH)

        def by_rows(first, n_rows, fn):
            step = min(row_chunk, n_rows)

            def rows_body(r, _):
                fn(pl.ds(pl.multiple_of(first + r * step, step), step))
                return 0

            lax.fori_loop(0, n_rows // step, rows_body, 0)

        def send_chip_sum(a, j, chip_sum):
            send, recv = ((send_in, recv_in), (send_out, recv_out))[a]
            n_rows = (d_model, rows_out)[a]
            half = n_rows // 2

            def plain(rows):
                send[j, rows, :] = chip_sum(rows).astype(BF16)

            if j == 0:
                by_rows(0, n_rows, plain)
                ici_copy(a, VIA_X).start()
                ici_copy(a, VIA_Y).start()
                return
            free, bound, passed = ((0, X_UP), (1, X_LOW), VIA_Y) if j == 1 else ((1, Y_LOW), (0, Y_UP), VIA_X)
            by_rows(free[0] * half, half, plain)
            ici_copy(a, free[1]).start()
            ici_copy(a, passed).wait_recv()

            def with_passed(rows):
                send[j, rows, :] = (chip_sum(rows) + recv[0, rows, :].astype(F32)).astype(BF16)

            by_rows(bound[0] * half, half, with_passed)
            ici_copy(a, bound[1]).start()

        for s in range(4):
            pltpu.make_async_copy(own_out.at[s], own_out.at[s], loc_sems.at[s]).wait()
            pltpu.make_async_remote_copy(
                src_ref=got_out.at[s], dst_ref=got_out.at[s], send_sem=d2d_send.at[1, s], recv_sem=d2d_recv.at[1, s],
                device_id=sibling, device_id_type=MESH).wait()
        for j, chip in enumerate(chips):
            s = 2 * chip[0] + chip[1]
            send_chip_sum(1, j, lambda rows, s=s: own_out[s, rows, :] + got_out[s, rows, :])

        xt_copy.wait()

        def d2d_copy(slot, pair):
            return pltpu.make_async_remote_copy(
                src_ref=acc.at[slot], dst_ref=got_in.at[pair], send_sem=d2d_send.at[0, pair],
                recv_sem=d2d_recv.at[0, pair], device_id=sibling, device_id_type=MESH)

        for step in range(N_DEV):
            slot, pair = step % 2, step // 2
            if step % 2 == 0 and step >= 2:
                d2d_copy(slot, pair - 1).wait_send()

            dp_wait(step)
            if step + 1 < N_DEV:
                dp_start(step + 1)
            acc[slot] = _dot(xt_v[...], dp_buf[step % 2])

            if step % 2 == 0:
                d2d_copy(slot, pair).start()
            else:
                d2d_copy(slot, pair).wait_recv()
                if step < N_DEV - 1:
                    send_chip_sum(0, pair, lambda rows, slot=slot, pair=pair: acc[slot, rows, :] + got_in[pair, rows, :])

        def w_copy(k):
            return pltpu.make_async_copy(win_hbm.at[k], xt_v.at[:, pl.ds(k * cw, cw)], w_sems.at[k])

        for k in range(N_DEV):
            w_copy(k).start()

        def gx_in(tile, buf):
            rows = pl.ds(tile * tm, tm)
            copies = [pltpu.make_async_copy(dr_hbm.at[rows, :], dr_buf.at[buf], dr_sems.at[buf])]
            copies += [pltpu.make_async_copy(dp_groups[k].at[rows, :], dp_buf.at[buf, pl.ds(k * tm, tm), :],
                                             dpx_sems.at[buf, k]) for k in range(N_DEV)]
            return copies

        def gx_out(tile):
            return pltpu.make_async_copy(gx_buf, gx_hbm.at[pl.ds(tile * tm, tm), :], gx_sem)

        for cp in gx_in(0, 0):
            cp.start()
        for k in range(N_DEV):
            w_copy(k).wait()

        def gx_body(tile, _):
            buf = tile % 2
            for cp in gx_in(tile, buf):
                cp.wait()

            @pl.when(tile + 1 < gx_tiles)
            def _():
                for cp in gx_in(tile + 1, 1 - buf):
                    cp.start()

            val = ALPHA * dr_buf[buf]
            for k in range(N_DEV):
                val = val + _dot(dp_buf[buf, k * tm:(k + 1) * tm, :], xt_v[:, k * cw:(k + 1) * cw], NT)

            @pl.when(tile > 0)
            def _():
                gx_out(tile - 1).wait()

            gx_buf[...] = val
            gx_out(tile).start()
            return 0

        lax.fori_loop(0, gx_tiles, gx_body, 0)

        for d in range(N_DEV):
            @pl.when(d != me)
            def _(d=d):
                pltpu.make_async_remote_copy(
                    src_ref=small_ref, dst_ref=small_all.at[d], send_sem=sm_send.at[d], recv_sem=sm_recv.at[d],
                    device_id=(d // 4, (d // 2) % 2, d % 2), device_id_type=MESH).wait()
        total = small_all[0]
        for d in range(1, N_DEV):
            total = total + small_all[d]
        small_o[...] = total

        mine = 2 * x + y
        last = (N_DEV - 1) % 2

        def finish(a, n_rows, chip_sum, g_o):
            recv = (recv_in, recv_out)[a]
            for k in (X_UP, X_LOW, Y_UP, Y_LOW):
                ici_copy(a, k).wait_recv()
            for k in ici_place:
                ici_copy(a, k).wait_send()

            def total_rows(rows):
                g_o[rows, :] = chip_sum(rows) + recv[1, rows, :].astype(F32) + recv[2, rows, :].astype(F32)

            by_rows(0, n_rows, total_rows)

        finish(1, rows_out, lambda rows: own_out[mine, rows, :] + got_out[mine, rows, :], g_out_o)
        finish(0, d_model, lambda rows: acc[last, rows, :] + got_in[N_DEV // 2 - 1, rows, :], g_in_o)
        d2d_copy(0, N_DEV // 2 - 1).wait_send()
        gx_out(0).wait()

    vmem = pl.BlockSpec(memory_space=pltpu.VMEM)
    hbm = pl.BlockSpec(memory_space=pl.ANY)
    return pl.pallas_call(
        body, name="grad_w_reduce",
        in_specs=[hbm] * 7 + [vmem],
        out_specs=(hbm, vmem, vmem, vmem),
        out_shape=(jax.ShapeDtypeStruct((seq, d_model), F32), jax.ShapeDtypeStruct((d_model, cw), F32),
                   jax.ShapeDtypeStruct((rows_out, d_model), F32), jax.ShapeDtypeStruct(small.shape, F32)),
        scratch_shapes=[
            pltpu.VMEM((d_model, seq), BF16), pltpu.VMEM((2, seq, cw), BF16), pltpu.VMEM((2, d_model, cw), F32),
            pltpu.VMEM((4, d_model, cw), F32), pltpu.VMEM((3, d_model, cw), BF16), pltpu.VMEM((3, d_model, cw), BF16),
            pltpu.VMEM((4, rows_out, d_model), F32), pltpu.VMEM((4, rows_out, d_model), F32),
            pltpu.VMEM((3, rows_out, d_model), BF16), pltpu.VMEM((3, rows_out, d_model), BF16),
            pltpu.VMEM((N_DEV,) + small.shape, F32),
            pltpu.VMEM((2, tm, d_model), F32), pltpu.VMEM((tm, d_model), F32),
            pltpu.SemaphoreType.DMA, pltpu.SemaphoreType.DMA((2,)), pltpu.SemaphoreType.DMA((4,)),
            pltpu.SemaphoreType.DMA((2, 4)), pltpu.SemaphoreType.DMA((2, 4)),
            pltpu.SemaphoreType.DMA((2, 6)), pltpu.SemaphoreType.DMA((2, 6)),
            pltpu.SemaphoreType.DMA((N_DEV,)), pltpu.SemaphoreType.DMA((N_DEV,)),
            pltpu.SemaphoreType.DMA((2,)), pltpu.SemaphoreType.DMA((2, N_DEV)), pltpu.SemaphoreType.DMA((N_DEV,)),
            pltpu.SemaphoreType.DMA,
        ],
        compiler_params=_params(56),
    )(xt, *dp_parts, dr, win_all, gwo, small)


def _adamw_update(g, w, m, v, rows):
    n_rows, width = w.shape
    rows = min(rows, n_rows)

    def body(g_ref, w_ref, m_ref, v_ref, d_o, nm_o, nv_o):
        d_o[...], nm_o[...], nv_o[...] = _adamw(w_ref[...], g_ref[...], m_ref[...], v_ref[...])

    tile = pl.BlockSpec((rows, width), lambda i: (i, 0))
    shape = jax.ShapeDtypeStruct(w.shape, F32)
    return pl.pallas_call(
        body, name="adamw_update", grid=(n_rows // rows,), in_specs=[tile] * 4, out_specs=(tile,) * 3,
        out_shape=(shape,) * 3,
    )(g, w, m, v)


def _small_update(grads, weights, ms, vs):
    n = len(grads)

    def body(*refs):
        g_refs, w_refs, m_refs, v_refs = (refs[i * n:(i + 1) * n] for i in range(4))
        outs = refs[4 * n:]
        for i in range(n):
            delta, nm, nv = _adamw(w_refs[i][...], g_refs[i][...], m_refs[i][...], v_refs[i][...])
            outs[3 * i][...] = delta
            outs[3 * i + 1][...] = nm
            outs[3 * i + 2][...] = nv

    vmem = pl.BlockSpec(memory_space=pltpu.VMEM)
    out_shape = []
    for w in weights:
        out_shape += [jax.ShapeDtypeStruct(w.shape, F32)] * 3
    return pl.pallas_call(
        body, name="small_update", in_specs=[vmem] * (4 * n), out_specs=(vmem,) * (3 * n), out_shape=tuple(out_shape),
    )(*grads, *weights, *ms, *vs)


def _tile_sizes(seq):
    return dict(tm=seq // N_DEV, t_ln=min(512, seq), t_attn=min(128, seq), rc=min(256, seq), pairs=4)


def kernel(x, w_in, conv_w, w_out, ln_gain, ln_bias, loss_target, m_w_in, m_conv_w, m_w_out, m_ln_gain, m_ln_bias,
           v_w_in, v_conv_w, v_w_out, v_ln_gain, v_ln_bias):
    assert x.shape[0] == 1 and w_in.shape[0] == 1, "one sequence per device, depth 1"
    _, seq, d_model = x.shape
    cw = w_in.shape[2]
    conv_k, conv_cols = conv_w.shape[1], conv_w.shape[2]
    rows_out = w_out.shape[1]
    assert cw == d_model // 2 and cw % PAIR == 0 and conv_cols * N_DEV == cw and rows_out * N_DEV == d_model
    ts = _tile_sizes(seq)

    x2 = x.reshape(seq, d_model)
    target = loss_target.reshape(seq, d_model)
    me = 4 * lax.axis_index("x") + 2 * lax.axis_index("y") + lax.axis_index("c")

    conv_pad = jnp.pad(conv_w[0], ((0, SUBLANES - conv_k), (0, PAIR - conv_cols)))
    proj, xt, win_all, wout_all, conv_all = _gather_proj(x2, w_in[0], w_out[0], conv_pad, ts["tm"])
    w_out_full = wout_all.reshape(d_model, d_model)
    conv_full = conv_all[:, :conv_k, :conv_cols].transpose(1, 0, 2).reshape(conv_k, cw)
    conv_full = jnp.pad(conv_full, ((0, SUBLANES - conv_k), (0, 0)))

    mix_conv = _conv_fwd(proj, conv_full, ts["rc"])
    pairs = min(ts["pairs"], cw // PAIR)
    tri = _triangles(ts["t_attn"])
    o, mix_attn, tot = _attn_fwd(proj, tri, ts["t_attn"], pairs)
    dr, d_mix_conv, d_mix_attn, gwo, small = _out_ln(mix_conv, mix_attn, x2, target, ln_gain, ln_bias, w_out_full,
                                                     ts["t_ln"])
    dp_conv, d_taps = _conv_bwd(proj, conv_full, d_mix_conv, ts["rc"])
    dp_qz, dp_kv = _attn_bwd(proj, tri, o, tot, d_mix_attn, ts["t_attn"], pairs)
    small = small.at[ROW_CONV:ROW_CONV + conv_k, :cw].set(d_taps[:conv_k])
    grad_x, g_in, g_out, small_sum = _grad_w_reduce(
        xt, (dp_conv, dp_qz, dp_kv), dr, win_all, gwo, small, rows_out, 128, ts["tm"])
    d_in, nm_in, nv_in = _adamw_update(g_in, w_in[0], m_w_in[0], v_w_in[0], 256)

    loss = small_sum[ROW_LOSS, 0]
    g_gain = small_sum[ROW_GAIN:ROW_GAIN + 1]
    g_bias = small_sum[ROW_BIAS:ROW_BIAS + 1]
    g_conv = lax.dynamic_slice(small_sum, (ROW_CONV, me * conv_cols), (conv_k, conv_cols))
    upd = _small_update((g_out, g_conv, g_gain, g_bias), (w_out[0], conv_w[0], ln_gain, ln_bias),
                        (m_w_out[0], m_conv_w[0], m_ln_gain, m_ln_bias),
                        (v_w_out[0], v_conv_w[0], v_ln_gain, v_ln_bias))
    (d_out, nm_out, nv_out, d_conv, nm_conv, nv_conv, d_gain, nm_gain, nv_gain, d_bias, nm_bias, nv_bias) = upd

    lead = lambda a: a[None]
    return (loss, grad_x.reshape(1, seq, d_model), lead(g_in), lead(g_conv), lead(g_out), g_gain, g_bias,
            lead(d_in), lead(d_conv), lead(d_out), d_gain, d_bias,
            lead(nm_in), lead(nm_conv), lead(nm_out), nm_gain, nm_bias,
            lead(nv_in), lead(nv_conv), lead(nv_out), nv_gain, nv_bias)
```

```python
import jax
import jax.numpy as jnp
from jax import lax
from jax.experimental import pallas as pl
from jax.experimental.pallas import tpu as pltpu

F32 = jnp.float32
BF16 = jnp.bfloat16
MESH = pl.DeviceIdType.MESH

N_DEV = 8
HEAD_DIM = 64
PAIR = 128
SUBLANES = 8
LN_EPS = 1e-5
ALPHA = 2.0 ** 0.25
ADAM_LR, ADAM_B1, ADAM_B2, ADAM_EPS, ADAM_WD, ADAM_STEP = 0.001, 0.9, 0.999, 1e-08, 0.01, 10

ROW_GAIN, ROW_BIAS, ROW_CONV, ROW_LOSS = 0, 1, 2, 5

NT = (((1,), (1,)), ((), ()))
TN = (((0,), (0,)), ((), ()))


V7X_VMEM_BYTES = 64 * 1024 * 1024


def _params(vmem_mib):
    assert vmem_mib * 1024 * 1024 < V7X_VMEM_BYTES
    return pltpu.CompilerParams(vmem_limit_bytes=vmem_mib * 1024 * 1024)


def _dot(a, b, dims=None):
    if dims is None:
        return jnp.dot(a, b, preferred_element_type=F32)
    return lax.dot_general(a, b, dims, preferred_element_type=F32)


def _sigmoid(z):
    return 1.0 / (1.0 + jnp.exp(-z))


def _mesh_pos():
    return lax.axis_index("x"), lax.axis_index("y"), lax.axis_index("c")


def _adamw(w, g, m, v):
    nm = ADAM_B1 * m + (1.0 - ADAM_B1) * g
    nv = ADAM_B2 * v + (1.0 - ADAM_B2) * (g * g)
    m_hat = nm * (1.0 / (1.0 - ADAM_B1 ** ADAM_STEP))
    v_hat = nv * (1.0 / (1.0 - ADAM_B2 ** ADAM_STEP))
    delta = -ADAM_LR * (m_hat / (jnp.sqrt(v_hat) + ADAM_EPS) + ADAM_WD * w)
    return delta, nm, nv


def _gather_proj(x, w_in_s, conv_s, tm):
    seq, d_model = x.shape
    cw = w_in_s.shape[1]
    n_tiles = seq // tm
    half = d_model // 2
    SIB, X_UP, X_LOW, Y_UP, Y_LOW, VIA_Y, VIA_X, ON_X, ON_Y, ON_DIAG = range(10)

    def body(x_hbm, win_ref, conv_ref, proj_hbm, xt_hbm, win_all, conv_all,
             xb, x_stage, o_stage, xt_stage, x_sems, o_sems, xt_sems, w_send, w_recv, send_sems, recv_sems):
        x, y, c = _mesh_pos()
        me = (x, y, c)
        sibling = (x, y, 1 - c)
        x_nbr, y_nbr, diag = (1 - x, y), (x, 1 - y), (1 - x, 1 - y)
        chips = [x_nbr, y_nbr, diag]
        small = (conv_all,)

        def slot(pos):
            return 4 * pos[0] + 2 * pos[1] + pos[2]

        def x_copy(tile, buf):
            return pltpu.make_async_copy(x_hbm.at[pl.ds(tile * tm, tm), :], x_stage.at[buf], x_sems.at[buf])

        x_copy(0, 0).start()
        win_all[slot(me)] = win_ref[...].astype(BF16)
        conv_all[slot(me)] = conv_ref[...]

        def w_copy(k, block, part, to):
            ref = win_all.at[slot(block)]
            if part is not None:
                ref = ref.at[pl.ds(part * half, half), :]
            return pltpu.make_async_remote_copy(
                src_ref=ref, dst_ref=ref, send_sem=w_send.at[k], recv_sem=w_recv.at[k],
                device_id=to, device_id_type=MESH)

        def copy(a, k, block, to):
            ref = small[a].at[slot(block)]
            return pltpu.make_async_remote_copy(
                src_ref=ref, dst_ref=ref, send_sem=send_sems.at[a, k], recv_sem=recv_sems.at[a, k],
                device_id=to, device_id_type=MESH)

        sends = [w_copy(SIB, me, None, sibling),
                 w_copy(X_UP, me, 0, (*x_nbr, c)), w_copy(Y_LOW, me, 1, (*y_nbr, c)),
                 w_copy(X_LOW, me, 1, (*x_nbr, c)), w_copy(Y_UP, me, 0, (*y_nbr, c))]
        for a in range(len(small)):
            sends.append(copy(a, 0, me, sibling))
            sends += [copy(a, 1 + j, me, (*chip, c)) for j, chip in enumerate(chips)]
        for cp in sends:
            cp.start()

        def o_copy(group, tile, buf):
            return pltpu.make_async_copy(o_stage.at[buf], proj_hbm.at[group, pl.ds(tile * tm, tm), :], o_sems.at[buf])

        def xt_copy(tile, buf):
            return pltpu.make_async_copy(xt_stage.at[buf], xt_hbm.at[:, pl.ds(tile * tm, tm)], xt_sems.at[buf])

        def project(order, group, first_pass):
            def tile_body(tile, _):
                if first_pass:
                    buf = tile % 2
                    x_copy(tile, buf).wait()

                    @pl.when(tile + 1 < n_tiles)
                    def _():
                        x_copy(tile + 1, 1 - buf).start()

                    xv = x_stage[buf]
                    xb[tile] = xv.astype(BF16)

                    @pl.when(tile >= 2)
                    def _():
                        xt_copy(tile - 2, buf).wait()

                    xt_stage[buf] = xv.T.astype(BF16)
                    xt_copy(tile, buf).start()
                count = order * n_tiles + tile
                obuf = count % 2

                @pl.when(count >= 2)
                def _():
                    o_copy(group, tile, obuf).wait()

                o_stage[obuf] = _dot(xb[tile], win_all[group]).astype(BF16)
                o_copy(group, tile, obuf).start()
                return 0

            lax.fori_loop(0, n_tiles, tile_body, 0)

        def start(cp):
            cp.start()
            sends.append(cp)

        def small_pass_on(j):
            for a in range(len(small)):
                copy(a, 1 + j, (*chips[j], c), me).wait_recv()
                start(copy(a, 4 + j, (*chips[j], c), sibling))

        def small_from_sibling(k):
            for a in range(len(small)):
                copy(a, k, sibling, me).wait_recv()

        project(0, slot(me), True)
        w_copy(SIB, sibling, None, me).wait_recv()
        small_from_sibling(0)
        project(1, slot(sibling), False)
        w_copy(X_UP, (*x_nbr, c), 0, me).wait_recv()
        start(w_copy(VIA_Y, (*x_nbr, c), 0, (*y_nbr, c)))
        w_copy(Y_LOW, (*y_nbr, c), 1, me).wait_recv()
        start(w_copy(VIA_X, (*y_nbr, c), 1, (*x_nbr, c)))
        w_copy(X_LOW, (*x_nbr, c), 1, me).wait_recv()
        start(w_copy(ON_X, (*x_nbr, c), None, sibling))
        small_pass_on(0)
        project(2, slot((*x_nbr, c)), False)
        w_copy(Y_UP, (*y_nbr, c), 0, me).wait_recv()
        start(w_copy(ON_Y, (*y_nbr, c), None, sibling))
        small_pass_on(1)
        project(3, slot((*y_nbr, c)), False)
        w_copy(ON_X, (*x_nbr, 1 - c), None, me).wait_recv()
        small_from_sibling(4)
        project(4, slot((*x_nbr, 1 - c)), False)
        w_copy(ON_Y, (*y_nbr, 1 - c), None, me).wait_recv()
        small_from_sibling(5)
        project(5, slot((*y_nbr, 1 - c)), False)
        w_copy(VIA_Y, (*diag, c), 0, me).wait_recv()
        w_copy(VIA_X, (*diag, c), 1, me).wait_recv()
        start(w_copy(ON_DIAG, (*diag, c), None, sibling))
        small_pass_on(2)
        project(6, slot((*diag, c)), False)
        w_copy(ON_DIAG, (*diag, 1 - c), None, me).wait_recv()
        small_from_sibling(6)
        project(7, slot((*diag, 1 - c)), False)

        for buf in range(2):
            o_copy(0, 0, buf).wait()
        for buf in range(min(2, n_tiles)):
            xt_copy(0, buf).wait()
        for cp in sends:
            cp.wait_send()

    vmem = pl.BlockSpec(memory_space=pltpu.VMEM)
    hbm = pl.BlockSpec(memory_space=pl.ANY)
    return pl.pallas_call(
        body, name="gather_proj",
        out_shape=(jax.ShapeDtypeStruct((N_DEV, seq, cw), BF16),
                   jax.ShapeDtypeStruct((d_model, seq), BF16),
                   jax.ShapeDtypeStruct((N_DEV, d_model, cw), BF16),
                   jax.ShapeDtypeStruct((N_DEV,) + conv_s.shape, F32)),
        in_specs=[hbm, vmem, vmem], out_specs=(hbm, hbm, vmem, vmem),
        scratch_shapes=[
            pltpu.VMEM((n_tiles, tm, d_model), BF16), pltpu.VMEM((2, tm, d_model), F32),
            pltpu.VMEM((2, tm, cw), BF16), pltpu.VMEM((2, d_model, tm), BF16),
            pltpu.SemaphoreType.DMA((2,)), pltpu.SemaphoreType.DMA((2,)), pltpu.SemaphoreType.DMA((2,)),
            pltpu.SemaphoreType.DMA((10,)), pltpu.SemaphoreType.DMA((10,)),
            pltpu.SemaphoreType.DMA((1, 7)), pltpu.SemaphoreType.DMA((1, 7))],
        compiler_params=_params(48),
    )(x, w_in_s, conv_s)


def _conv_taps(ext, w_ref, rc):
    u0 = ext[SUBLANES:SUBLANES + rc]
    u1 = pltpu.roll(ext, 1, 0)[SUBLANES:SUBLANES + rc]
    u2 = pltpu.roll(ext, 2, 0)[SUBLANES:SUBLANES + rc]
    return w_ref[2:3, :] * u0 + w_ref[1:2, :] * u1 + w_ref[0:1, :] * u2, u0, u1, u2


def _conv_fwd(proj, conv_full, rc):
    _, seq, cw = proj.shape

    def body(b_ref, c_ref, h_ref, z_ref, w_ref, o_ref, u_scr):
        u_scr[0:SUBLANES, :] = jnp.zeros((SUBLANES, PAIR), F32)

        def fill(r, _):
            base = pl.multiple_of(r * rc, rc)
            rows = pl.ds(base, rc)
            u_scr[pl.ds(base + SUBLANES, rc), :] = c_ref[rows, :].astype(F32) * h_ref[rows, :].astype(F32)
            return 0

        lax.fori_loop(0, seq // rc, fill, 0)

        def out(r, _):
            base = pl.multiple_of(r * rc, rc)
            rows = pl.ds(base, rc)
            ext = u_scr[pl.ds(base, rc + SUBLANES), :]
            y, _, _, _ = _conv_taps(ext, w_ref, rc)
            z = z_ref[rows, :].astype(F32)
            o_ref[rows, :] = (z * _sigmoid(z) * b_ref[rows, :].astype(F32) * y).astype(BF16)
            return 0

        lax.fori_loop(0, seq // rc, out, 0)

    def chunk(j):
        return pl.BlockSpec((None, seq, PAIR), lambda cb, j=j: (j, 0, cb))

    return pl.pallas_call(
        body, name="conv_fwd", grid=(cw // PAIR,),
        in_specs=[chunk(0), chunk(1), chunk(2), chunk(3), pl.BlockSpec((SUBLANES, PAIR), lambda cb: (0, cb))],
        out_specs=pl.BlockSpec((seq, PAIR), lambda cb: (0, cb)),
        out_shape=jax.ShapeDtypeStruct((seq, cw), BF16),
        scratch_shapes=[pltpu.VMEM((seq + SUBLANES, PAIR), F32)],
    )(proj, proj, proj, proj, conv_full)


def _conv_bwd(proj, conv_full, d_mix_conv, rc):
    _, seq, cw = proj.shape

    def body(b_ref, c_ref, h_ref, z_ref, w_ref, g_ref, dp_ref, dw_ref, u_scr, dy_scr):
        u_scr[0:SUBLANES, :] = jnp.zeros((SUBLANES, PAIR), F32)
        dy_scr[seq:seq + SUBLANES, :] = jnp.zeros((SUBLANES, PAIR), F32)

        def fill(r, _):
            base = pl.multiple_of(r * rc, rc)
            rows = pl.ds(base, rc)
            u_scr[pl.ds(base + SUBLANES, rc), :] = c_ref[rows, :].astype(F32) * h_ref[rows, :].astype(F32)
            return 0

        lax.fori_loop(0, seq // rc, fill, 0)

        def gate(r, acc):
            base = pl.multiple_of(r * rc, rc)
            rows = pl.ds(base, rc)
            ext = u_scr[pl.ds(base, rc + SUBLANES), :]
            y, u0, u1, u2 = _conv_taps(ext, w_ref, rc)
            z = z_ref[rows, :].astype(F32)
            b = b_ref[rows, :].astype(F32)
            g = g_ref[rows, :].astype(F32)
            sig = _sigmoid(z)
            dp_ref[3, rows, :] = (g * b * y * (sig * (1.0 + z * (1.0 - sig)))).astype(BF16)
            gs = g * (z * sig)
            dp_ref[0, rows, :] = (gs * y).astype(BF16)
            dy = gs * b
            dy_scr[rows, :] = dy
            a0, a1, a2 = acc
            return (a0 + jnp.sum(dy * u2, axis=0, keepdims=True),
                    a1 + jnp.sum(dy * u1, axis=0, keepdims=True),
                    a2 + jnp.sum(dy * u0, axis=0, keepdims=True))

        zero = jnp.zeros((1, PAIR), F32)
        a0, a1, a2 = lax.fori_loop(0, seq // rc, gate, (zero, zero, zero))
        dw_ref[...] = jnp.zeros((SUBLANES, PAIR), F32)
        dw_ref[0:1, :] = a0
        dw_ref[1:2, :] = a1
        dw_ref[2:3, :] = a2

        def back(r, _):
            base = pl.multiple_of(r * rc, rc)
            rows = pl.ds(base, rc)
            ext = dy_scr[pl.ds(base, rc + SUBLANES), :]
            n = rc + SUBLANES
            d0 = ext[0:rc]
            d1 = pltpu.roll(ext, n - 1, 0)[0:rc]
            d2 = pltpu.roll(ext, n - 2, 0)[0:rc]
            du = w_ref[2:3, :] * d0 + w_ref[1:2, :] * d1 + w_ref[0:1, :] * d2
            dp_ref[1, rows, :] = (du * h_ref[rows, :].astype(F32)).astype(BF16)
            dp_ref[2, rows, :] = (du * c_ref[rows, :].astype(F32)).astype(BF16)
            return 0

        lax.fori_loop(0, seq // rc, back, 0)

    def chunk(j):
        return pl.BlockSpec((None, seq, PAIR), lambda cb, j=j: (j, 0, cb))

    return pl.pallas_call(
        body, name="conv_bwd", grid=(cw // PAIR,),
        in_specs=[chunk(0), chunk(1), chunk(2), chunk(3), pl.BlockSpec((SUBLANES, PAIR), lambda cb: (0, cb)),
                  pl.BlockSpec((seq, PAIR), lambda cb: (0, cb))],
        out_specs=(pl.BlockSpec((4, seq, PAIR), lambda cb: (0, 0, cb)),
                   pl.BlockSpec((SUBLANES, PAIR), lambda cb: (0, cb))),
        out_shape=(jax.ShapeDtypeStruct((4, seq, cw), BF16), jax.ShapeDtypeStruct((SUBLANES, cw), F32)),
        scratch_shapes=[pltpu.VMEM((seq + SUBLANES, PAIR), F32), pltpu.VMEM((seq + SUBLANES, PAIR), F32)],
    )(proj, proj, proj, proj, conv_full, d_mix_conv)


SKIP_CARRY = 104.0
LOG2_E = 1.4426950408889634
LANE_TOT0, LANE_TOT1, LANE_FIRST, LANE_WHOLE = 0, 1, 2, 3
FAST_BLOCKS = 3
EARLY_ROWS = 32


def _triangles(t):
    row = lax.broadcasted_iota(jnp.int32, (2 * t, 2 * t), 0)
    col = lax.broadcasted_iota(jnp.int32, (2 * t, 2 * t), 1)
    same = (row < t) == (col < t)
    upper = jnp.logical_and(same, row > col).astype(BF16)
    lower = jnp.logical_and(same, row < col).astype(BF16)
    return jnp.stack([jnp.concatenate([upper, upper], axis=0), jnp.concatenate([lower, lower], axis=0)])


def _pair_masks(t):
    lane = lax.broadcasted_iota(jnp.int32, (t, PAIR), 1)
    qrow = lax.broadcasted_iota(jnp.int32, (t, 2 * t), 0)
    kcol = lax.broadcasted_iota(jnp.int32, (t, 2 * t), 1)
    strict = jnp.where(kcol < t, kcol, kcol - t) < qrow
    return lane, lane < HEAD_DIM, strict


def _by_head(x, head0):
    zero = jnp.zeros_like(x)
    return jnp.concatenate([jnp.where(head0, x, zero), jnp.where(head0, zero, x)], axis=0)


def _hi_lo(a):
    hi = a.astype(BF16)
    lo = (a - hi.astype(F32)).astype(BF16)
    return jnp.concatenate([hi, lo], axis=1)


def _softplus_parts(z, strict, masked):
    spu = jnp.maximum(z, 0.0) + jnp.log(1.0 + jnp.exp2(jnp.abs(z) * -LOG2_E))
    return z - spu, (jnp.where(strict, spu, 0.0) if masked else spu)


def _stacked_dot(parts, rhs):
    out = _dot(jnp.concatenate(parts, axis=0), rhs)
    ends = [0]
    for p in parts:
        ends.append(ends[-1] + p.shape[0])
    return [out[a:b] for a, b in zip(ends[:-1], ends[1:])]


def _splice(whole, rows, part):
    pieces = ([whole[:rows[0]]] if rows[0] > 0 else []) + [part]
    if rows[1] < whole.shape[0]:
        pieces.append(whole[rows[1]:])
    return part if len(pieces) == 1 else jnp.concatenate(pieces, axis=0)


def _attn_fwd(proj, tri, w_out_s, t, pp):
    _, seq, cw = proj.shape
    scale = HEAD_DIM ** -0.5
    width = pp * PAIR

    n_steps = (cw // width) * (seq // t)
    pass_on_step = min(8, n_steps - 1)

    def body(q_ref, k_ref, v_ref, za_ref, tri_ref, wout_ref, o_ref, mix_ref, tot_ref, wout_all, acc_ref,
             own_w, own_sem, send_sems, recv_sems):
        i = pl.program_id(1)
        step_no = pl.program_id(0) * (seq // t) + i
        mx, my, mc = _mesh_pos()
        me = (mx, my, mc)
        sibling = (mx, my, 1 - mc)
        chips = [(1 - mx, my), (mx, 1 - my), (1 - mx, 1 - my)]

        def place(block):
            return wout_all.at[4 * block[0] + 2 * block[1] + block[2]]

        def w_copy(k, block, to, src=None):
            return pltpu.make_async_remote_copy(
                src_ref=place(block) if src is None else src, dst_ref=place(block), send_sem=send_sems.at[k],
                recv_sem=recv_sems.at[k], device_id=to, device_id_type=MESH)

        own_copy = pltpu.make_async_copy(own_w, place(me), own_sem)

        @pl.when(step_no == 0)
        def _():
            own_w[...] = wout_ref[...].astype(BF16)
            own_copy.start()
            w_copy(0, me, sibling, own_w).start()
            for j, chip in enumerate(chips):
                w_copy(1 + j, me, (*chip, mc), own_w).start()

        @pl.when(step_no == pass_on_step)
        def _():
            for j, chip in enumerate(chips):
                w_copy(1 + j, (*chip, mc), me).wait_recv()
                w_copy(4 + j, (*chip, mc), sibling).start()

        @pl.when(step_no == n_steps - 1)
        def _():
            own_copy.wait()
            w_copy(0, sibling, me).wait_recv()
            for j, chip in enumerate(chips):
                w_copy(4 + j, (*chip, 1 - mc), me).wait_recv()
            for k in range(7):
                w_copy(k, me, sibling, own_w).wait_send()

        lane, head0, strict = _pair_masks(t)
        upper = tri_ref[0]
        q = q_ref[...] * scale
        acc_ref[...] = jnp.zeros_like(acc_ref)

        def sweep(blocks, state):
            staged = []
            for j, masked, rows in blocks:
                start = pl.multiple_of(j * t, t)
                kb = k_ref[pl.ds(start, t), :]
                vb = v_ref[pl.ds(start, t), :]
                for p in range(pp):
                    cols = slice(p * PAIR, (p + 1) * PAIR)
                    z = _dot(q[rows[0]:rows[1], cols], _by_head(kb[:, cols], head0), NT)
                    ls, sp = _softplus_parts(z, strict[rows[0]:rows[1]], masked)
                    staged.append((p, masked, rows, ls, sp, _by_head(vb[:, cols], head0)))
            afters = _stacked_dot([_hi_lo(sp) for _, _, _, _, sp, _ in staged], upper)
            state = list(state)
            for (p, masked, rows, ls, sp, v2), after in zip(staged, afters):
                c0, c1 = state[p]
                part = slice(rows[0], rows[1])
                x = ls - after
                w = jnp.exp(jnp.concatenate([x[:, :t] - c0[part], x[:, t:] - c1[part]], axis=1))
                if masked:
                    w = jnp.where(strict[part], w, 0.0)
                acc_ref[part, p * PAIR:(p + 1) * PAIR] += _dot(w.astype(BF16), v2)
                state[p] = (_splice(c0, rows, c0[part] + (after[:, 0:1] + sp[:, 0:1])),
                            _splice(c1, rows, c1[part] + (after[:, t:t + 1] + sp[:, t:t + 1])))
            return tuple(state)

        def unfinished(state, rows):
            m = state[0][0]
            for p in range(pp):
                m = jnp.minimum(m, jnp.minimum(state[p][0], state[p][1]))
            return jnp.min(m[rows[0]:rows[1]]) < SKIP_CARRY

        every = (0, t)
        early, late = (0, min(EARLY_ROWS, t)), (min(EARLY_ROWS, t), t)

        def step(js):
            state = sweep(((js[0], False, every),), js[1])
            return js[0] - 1, state, unfinished(state, every)

        def fast():
            blocks = tuple((i - b, b == 0, every) for b in range(FAST_BLOCKS - 1))
            state = sweep(blocks + ((i - (FAST_BLOCKS - 1), False, early),), init)
            if late[0] == late[1]:
                return state, jnp.bool_(True)
            whole = unfinished(state, late)
            state = lax.cond(whole, lambda: sweep(((i - (FAST_BLOCKS - 1), False, late),), state), lambda: state)
            return state, whole

        zcol = jnp.zeros((t, 1), F32)
        init = tuple((zcol, zcol) for _ in range(pp))
        many = i >= FAST_BLOCKS - 1
        state, whole = lax.cond(many, fast, lambda: (sweep(((i, True, every),), init), jnp.bool_(True)))
        j_end, state, _ = lax.while_loop(
            lambda js: jnp.logical_and(js[0] >= 0, js[2]), step,
            (jnp.where(many, i - FAST_BLOCKS, i - 1), state, unfinished(state, every)))
        first = (j_end + 1).astype(F32)
        notes = jnp.where(lane == LANE_FIRST, first, whole.astype(F32))
        za = za_ref[...].astype(F32)
        for p in range(pp):
            c0, c1 = state[p]
            cols = slice(p * PAIR, (p + 1) * PAIR)
            zp = za[:, cols]
            acc = acc_ref[:, cols]
            o_ref[:, cols] = acc.astype(BF16)
            mix_ref[:, cols] = (zp * _sigmoid(zp) * acc).astype(BF16)
            tot_ref[:, cols] = jnp.where(lane == LANE_TOT0, c0, jnp.where(lane == LANE_TOT1, c1, notes))

    def tile(j):
        return pl.BlockSpec((None, t, width), lambda g, i, j=j: (j, i, g))

    def full(j):
        return pl.BlockSpec((None, seq, width), lambda g, i, j=j: (j, 0, g))

    out_tile = pl.BlockSpec((t, width), lambda g, i: (i, g))
    return pl.pallas_call(
        body, name="attn_fwd", grid=(cw // width, seq // t),
        in_specs=[tile(4), full(5), full(6), tile(7), pl.BlockSpec(tri.shape, lambda g, i: (0, 0, 0)),
                  pl.BlockSpec(memory_space=pltpu.VMEM)],
        out_specs=(out_tile, out_tile, out_tile, pl.BlockSpec(memory_space=pl.ANY)),
        out_shape=(jax.ShapeDtypeStruct((seq, cw), BF16), jax.ShapeDtypeStruct((seq, cw), BF16),
                   jax.ShapeDtypeStruct((seq, cw), F32),
                   jax.ShapeDtypeStruct((N_DEV,) + w_out_s.shape, BF16)),
        scratch_shapes=[pltpu.VMEM((t, width), F32), pltpu.VMEM(w_out_s.shape, BF16), pltpu.SemaphoreType.DMA,
                        pltpu.SemaphoreType.DMA((7,)), pltpu.SemaphoreType.DMA((7,))],
    )(proj, proj, proj, proj, tri, w_out_s)


def _attn_bwd(proj, tri, o, tot, d_mix_attn, t, pp):
    _, seq, cw = proj.shape
    nb = seq // t
    scale = HEAD_DIM ** -0.5
    width = pp * PAIR

    def body(q_ref, k_ref, v_ref, za_ref, tri_ref, o_ref, tot_ref, g_ref, dqz_ref, dkv_ref, dk_acc, dv_acc, dq_acc):
        i = pl.program_id(1)
        dq_acc[...] = jnp.zeros_like(dq_acc)

        @pl.when(i == 0)
        def _():
            dk_acc[...] = jnp.zeros_like(dk_acc)
            dv_acc[...] = jnp.zeros_like(dv_acc)

        _, head0, strict = _pair_masks(t)
        upper, lower = tri_ref[0], tri_ref[1, 0:2 * t, :]
        za = za_ref[...].astype(F32)
        g = g_ref[...].astype(F32)
        sig = _sigmoid(za)
        dqz_ref[1] = (g * o_ref[...].astype(F32) * (sig * (1.0 + za * (1.0 - sig)))).astype(BF16)
        do = (g * (za * sig)).astype(BF16)
        q = q_ref[...] * scale
        tot_v = tot_ref[...]
        q2, do2, init = [], [], []
        zcol = jnp.zeros((t, 1), F32)
        for p in range(pp):
            cols = slice(p * PAIR, (p + 1) * PAIR)
            q2.append(_by_head(q[:, cols], head0))
            do2.append(_by_head(do[:, cols], head0))
            tp = tot_v[:, cols]
            init.append(((tp[:, LANE_TOT0:LANE_TOT0 + 1], tp[:, LANE_TOT1:LANE_TOT1 + 1]), (zcol, zcol)))
        first = jnp.clip(tot_v[0:1, LANE_FIRST:LANE_FIRST + 1], 0.0, i.astype(F32)).astype(jnp.int32)[0, 0]

        def sweep(blocks, state):
            staged = []
            for j, masked, rows in blocks:
                start = pl.multiple_of(j * t, t)
                kb = k_ref[pl.ds(start, t), :]
                vb = v_ref[pl.ds(start, t), :]
                part = slice(rows[0], rows[1])
                for p in range(pp):
                    cols = slice(p * PAIR, (p + 1) * PAIR)
                    k2 = _by_head(kb[:, cols], head0)
                    z = _dot(q[part, cols], k2, NT)
                    ls, sp = _softplus_parts(z, strict[part], masked)
                    da = _dot(do[part, cols], _by_head(vb[:, cols], head0), NT)
                    staged.append((p, masked, rows, k2, ls, sp, da))
            afters = _stacked_dot([_hi_lo(sp) for _, _, _, _, _, sp, _ in staged], upper)
            state = list(state)
            weights, ggs = [], []
            for (p, masked, rows, k2, ls, sp, da), after in zip(staged, afters):
                (s0, s1), befores = state[p]
                part = slice(rows[0], rows[1])
                n0 = s0[part] - (after[:, 0:1] + sp[:, 0:1])
                n1 = s1[part] - (after[:, t:t + 1] + sp[:, t:t + 1])
                x = ls - after
                a = jnp.exp(jnp.concatenate([x[:, :t] - n0, x[:, t:] - n1], axis=1))
                if masked:
                    a = jnp.where(strict[part], a, 0.0)
                state[p] = ((_splice(s0, rows, n0), _splice(s1, rows, n1)), befores)
                weights.append(a.astype(BF16))
                ggs.append(a * da)
            pres = _stacked_dot([gg.astype(BF16) for gg in ggs], lower)
            dzs = []
            for (p, masked, rows, k2, ls, sp, da), gg, pre in zip(staged, ggs, pres):
                rests, (b0, b1) = state[p]
                part = slice(rows[0], rows[1])
                y = gg + pre
                dz = gg - jnp.exp(ls) * jnp.concatenate([y[:, :t] + b0[part], y[:, t:] + b1[part]], axis=1)
                if masked:
                    dz = jnp.where(strict[part], dz, 0.0)
                dzb = dz.astype(BF16)
                dzs.append(dzb)
                dq_acc[part, p * PAIR:(p + 1) * PAIR] += _dot(dzb, k2)
                state[p] = (rests, (_splice(b0, rows, b0[part] + y[:, t - 1:t]),
                                    _splice(b1, rows, b1[part] + y[:, 2 * t - 1:2 * t])))
            first_row = pl.multiple_of(blocks[0][0] * t, t)
            n_rows = len(blocks) * t
            for p in range(pp):
                cols = slice(p * PAIR, (p + 1) * PAIR)

                def by_key(tiles):
                    out = []
                    for n, m in tiles:
                        rows = staged[n][2]
                        m = _splice(jnp.zeros((t, 2 * t), BF16), rows, m)
                        out.append(jnp.concatenate([m[:, :t], m[:, t:]], axis=0).T)
                    return jnp.concatenate(out, axis=0)

                mine = [n for n in range(len(staged)) if staged[n][0] == p]
                dk_acc[pl.ds(first_row, n_rows), cols] += _dot(by_key([(n, dzs[n]) for n in mine]), q2[p])
                dv_acc[pl.ds(first_row, n_rows), cols] += _dot(by_key([(n, weights[n]) for n in mine]), do2[p])
            return tuple(state)

        every = (0, t)
        early = (0, min(EARLY_ROWS, t))
        many = i >= FAST_BLOCKS - 1
        last_single = jnp.where(many, i - (FAST_BLOCKS - 1), i)
        state = lax.fori_loop(first, last_single, lambda j, s: sweep(((j, False, every),), s), tuple(init))

        def fast(rows):
            blocks = tuple((i - b, b == 0, every) for b in range(FAST_BLOCKS - 2, -1, -1))
            return sweep(((i - (FAST_BLOCKS - 1), False, rows),) + blocks, state)

        whole = tot_v[0:1, LANE_WHOLE:LANE_WHOLE + 1].astype(jnp.int32)[0, 0] > 0
        state = lax.cond(
            many,
            lambda: lax.cond(whole, lambda: fast(every), lambda: fast(early)),
            lambda: sweep(((i, True, every),), state))
        dqz_ref[0] = (dq_acc[...] * scale).astype(BF16)

        @pl.when(i == nb - 1)
        def _():
            dkv_ref[0] = dk_acc[...].astype(BF16)
            dkv_ref[1] = dv_acc[...].astype(BF16)

    def tile(j):
        return pl.BlockSpec((None, t, width), lambda g, i, j=j: (j, i, g))

    def full(j):
        return pl.BlockSpec((None, seq, width), lambda g, i, j=j: (j, 0, g))

    flat_tile = pl.BlockSpec((t, width), lambda g, i: (i, g))
    return pl.pallas_call(
        body, name="attn_bwd", grid=(cw // width, nb),
        in_specs=[tile(4), full(5), full(6), tile(7), pl.BlockSpec(tri.shape, lambda g, i: (0, 0, 0)),
                  flat_tile, flat_tile, flat_tile],
        out_specs=(pl.BlockSpec((2, t, width), lambda g, i: (0, i, g)),
                   pl.BlockSpec((2, seq, width), lambda g, i: (0, 0, g))),
        out_shape=(jax.ShapeDtypeStruct((2, seq, cw), BF16), jax.ShapeDtypeStruct((2, seq, cw), BF16)),
        scratch_shapes=[pltpu.VMEM((seq, width), F32), pltpu.VMEM((seq, width), F32), pltpu.VMEM((t, width), F32)],
        compiler_params=_params(48),
    )(proj, proj, proj, proj, tri, o, tot, d_mix_attn)


def _out_ln(mix_conv, mix_attn, x, target, gain, bias, w_out, tm):
    seq, d_model = x.shape
    cw = mix_conv.shape[1]
    inv_d = 1.0 / d_model

    def body(mc_ref, ma_ref, x_ref, t_ref, gain_ref, bias_ref, w_ref, dr_ref, dmc_ref, dma_ref, gwo_ref, small_ref):
        @pl.when(pl.program_id(0) == 0)
        def _():
            gwo_ref[...] = jnp.zeros_like(gwo_ref)
            small_ref[...] = jnp.zeros_like(small_ref)

        mix = jnp.concatenate([mc_ref[...], ma_ref[...]], axis=1)
        w = w_ref[...]
        r = ALPHA * x_ref[...] + _dot(mix, w)
        mu = jnp.sum(r, axis=1, keepdims=True) * inv_d
        xc = r - mu
        var = jnp.sum(xc * xc, axis=1, keepdims=True) * inv_d
        rstd = lax.rsqrt(var + LN_EPS)
        xhat = xc * rstd
        gain_v = gain_ref[...]
        err = xhat * gain_v + bias_ref[...] - t_ref[...]
        row_loss = jnp.sum(err * err, axis=1, keepdims=True)
        loss = (0.5 * inv_d) * jnp.sum(row_loss, axis=0, keepdims=True)
        dy = err * inv_d
        small_ref[ROW_GAIN:ROW_GAIN + 1, :] += jnp.sum(dy * xhat, axis=0, keepdims=True)
        small_ref[ROW_BIAS:ROW_BIAS + 1, :] += jnp.sum(dy, axis=0, keepdims=True)
        small_ref[ROW_LOSS:ROW_LOSS + 1, :] += jnp.broadcast_to(loss, (1, d_model))
        dxhat = dy * gain_v
        m1 = jnp.sum(dxhat, axis=1, keepdims=True) * inv_d
        m2 = jnp.sum(dxhat * xhat, axis=1, keepdims=True) * inv_d
        dr = rstd * (dxhat - m1 - xhat * m2)
        dr_ref[...] = dr
        drb = dr.astype(BF16)
        dmix = _dot(drb, w, NT)
        dmc_ref[...] = dmix[:, :cw].astype(BF16)
        dma_ref[...] = dmix[:, cw:].astype(BF16)
        gwo_ref[...] += _dot(mix, drb, TN)

    def rows(width):
        return pl.BlockSpec((tm, width), lambda i: (i, 0))

    def whole(shape):
        return pl.BlockSpec(shape, lambda i: (0, 0))

    return pl.pallas_call(
        body, name="out_ln", grid=(seq // tm,),
        in_specs=[rows(cw), rows(cw), rows(d_model), rows(d_model), whole((1, d_model)), whole((1, d_model)),
                  whole((d_model, d_model))],
        out_specs=(rows(d_model), rows(cw), rows(cw), whole((d_model, d_model)), whole((SUBLANES, d_model))),
        out_shape=(jax.ShapeDtypeStruct((seq, d_model), F32), jax.ShapeDtypeStruct((seq, cw), BF16),
                   jax.ShapeDtypeStruct((seq, cw), BF16), jax.ShapeDtypeStruct((d_model, d_model), F32),
                   jax.ShapeDtypeStruct((SUBLANES, d_model), F32)),
        compiler_params=_params(48),
    )(mix_conv, mix_attn, x, target, gain, bias, w_out)


_DP_OF_GROUP = ((0, 0), (0, 1), (0, 2), (0, 3), (1, 0), (2, 0), (2, 1), (1, 1))


def _grad_w_reduce(xt, dp_parts, dr, win_all, gwo, small, rows_out, row_chunk, tm):
    d_model, seq = xt.shape
    nch, _, cw = win_all.shape
    gx_tiles = seq // tm
    assert nch * tm == seq and nch * cw == seq, "needs S == 8 * tm == 8 * CW"

    def body(xt_hbm, dpa, dpb, dpc, dr_hbm, win_hbm, gwo_ref, small_ref,
             gx_hbm, g_in_o, g_out_o, small_o,
             xt_v, dp_buf, acc, got_in, send_in, recv_in, own_out, got_out, send_out, recv_out, small_all,
             dr_buf, gx_buf,
             xt_sem, dp_sems, loc_sems, d2d_send, d2d_recv, ici_send, ici_recv, sm_send, sm_recv,
             dr_sems, dpx_sems, w_sems, gx_sem):
        x, y, c = _mesh_pos()
        me = 4 * x + 2 * y + c
        sibling = (x, y, 1 - c)
        chips = [(1 - x, 1 - y), (1 - x, y), (x, 1 - y)]
        owners = [(*chip, cc) for chip in chips for cc in (1 - c, c)] + [sibling, (x, y, c)]
        group_of = [4 * o[0] + 2 * o[1] + o[2] for o in owners]
        dp_parts_ = (dpa, dpb, dpc)
        dp_groups = [dp_parts_[arr].at[idx] for arr, idx in _DP_OF_GROUP]

        xt_copy = pltpu.make_async_copy(xt_hbm, xt_v, xt_sem)
        xt_copy.start()

        def dp_start(step):
            for k in range(N_DEV):
                @pl.when(group_of[step] == k)
                def _(k=k):
                    pltpu.make_async_copy(dp_groups[k], dp_buf.at[step % 2], dp_sems.at[step % 2]).start()

        def dp_wait(step):
            pltpu.make_async_copy(dp_groups[0], dp_buf.at[step % 2], dp_sems.at[step % 2]).wait()

        dp_start(0)

        small_all[me] = small_ref[...]
        for d in range(N_DEV):
            @pl.when(d != me)
            def _(d=d):
                pltpu.make_async_remote_copy(
                    src_ref=small_ref, dst_ref=small_all.at[me], send_sem=sm_send.at[d], recv_sem=sm_recv.at[me],
                    device_id=(d // 4, (d // 2) % 2, d % 2), device_id_type=MESH).start()

        def block_out(k):
            return gwo_ref.at[pl.ds(k * rows_out, rows_out), :]

        for k in range(N_DEV):
            s = k // 2

            @pl.when(k % 2 != c)
            def _(k=k, s=s):
                pltpu.make_async_remote_copy(
                    src_ref=block_out(k), dst_ref=got_out.at[s], send_sem=d2d_send.at[1, s],
                    recv_sem=d2d_recv.at[1, s], device_id=sibling, device_id_type=MESH).start()

            @pl.when(k % 2 == c)
            def _(k=k, s=s):
                pltpu.make_async_copy(block_out(k), own_out.at[s], loc_sems.at[s]).start()

        VIA_X, VIA_Y, X_UP, X_LOW, Y_UP, Y_LOW = range(6)
        x_dev, y_dev = (1 - x, y, c), (x, 1 - y, c)
        ici_place = {VIA_X: (0, 0, x_dev), VIA_Y: (0, 1, y_dev), X_UP: (1, 0, x_dev), X_LOW: (1, 1, x_dev),
                     Y_UP: (2, 0, y_dev), Y_LOW: (2, 1, y_dev)}

        def ici_copy(a, k):
            send, recv = ((send_in, recv_in), (send_out, recv_out))[a]
            half = (d_model, rows_out)[a] // 2
            slot, part, to = ici_place[k]
            rows = pl.ds(part * half, half)
            return pltpu.make_async_remote_copy(
                src_ref=send.at[slot, rows, :], dst_ref=recv.at[slot, rows, :], send_sem=ici_send.at[a, k],
                recv_sem=ici_recv.at[a, k], device_id=to, device_id_type=MESH)

        def by_rows(first, n_rows, fn):
            step = min(row_chunk, n_rows)

            def rows_body(r, _):
                fn(pl.ds(pl.multiple_of(first + r * step, step), step))
                return 0

            lax.fori_loop(0, n_rows // step, rows_body, 0)

        def send_chip_sum(a, j, chip_sum):
            send, recv = ((send_in, recv_in), (send_out, recv_out))[a]
            n_rows = (d_model, rows_out)[a]
            half = n_rows // 2

            def plain(rows):
                send[j, rows, :] = chip_sum(rows).astype(BF16)

            if j == 0:
                by_rows(0, n_rows, plain)
                ici_copy(a, VIA_X).start()
                ici_copy(a, VIA_Y).start()
                return
            free, bound, passed = ((0, X_UP), (1, X_LOW), VIA_Y) if j == 1 else ((1, Y_LOW), (0, Y_UP), VIA_X)
            by_rows(free[0] * half, half, plain)
            ici_copy(a, free[1]).start()
            ici_copy(a, passed).wait_recv()

            def with_passed(rows):
                send[j, rows, :] = (chip_sum(rows) + recv[0, rows, :].astype(F32)).astype(BF16)

            by_rows(bound[0] * half, half, with_passed)
            ici_copy(a, bound[1]).start()

        for s in range(4):
            pltpu.make_async_copy(own_out.at[s], own_out.at[s], loc_sems.at[s]).wait()
            pltpu.make_async_remote_copy(
                src_ref=got_out.at[s], dst_ref=got_out.at[s], send_sem=d2d_send.at[1, s], recv_sem=d2d_recv.at[1, s],
                device_id=sibling, device_id_type=MESH).wait()
        for j, chip in enumerate(chips):
            s = 2 * chip[0] + chip[1]
            send_chip_sum(1, j, lambda rows, s=s: own_out[s, rows, :] + got_out[s, rows, :])

        xt_copy.wait()

        def d2d_copy(slot, pair):
            return pltpu.make_async_remote_copy(
                src_ref=acc.at[slot], dst_ref=got_in.at[pair], send_sem=d2d_send.at[0, pair],
                recv_sem=d2d_recv.at[0, pair], device_id=sibling, device_id_type=MESH)

        for step in range(N_DEV):
            slot, pair = step % 2, step // 2
            if step % 2 == 0 and step >= 2:
                d2d_copy(slot, pair - 1).wait_send()

            dp_wait(step)
            if step + 1 < N_DEV:
                dp_start(step + 1)
            acc[slot] = _dot(xt_v[...], dp_buf[step % 2])

            if step % 2 == 0:
                d2d_copy(slot, pair).start()
            else:
                d2d_copy(slot, pair).wait_recv()
                if step < N_DEV - 1:
                    send_chip_sum(0, pair, lambda rows, slot=slot, pair=pair: acc[slot, rows, :] + got_in[pair, rows, :])

        def w_copy(k):
            return pltpu.make_async_copy(win_hbm.at[k], xt_v.at[:, pl.ds(k * cw, cw)], w_sems.at[k])

        for k in range(N_DEV):
            w_copy(k).start()

        def gx_in(tile, buf):
            rows = pl.ds(tile * tm, tm)
            copies = [pltpu.make_async_copy(dr_hbm.at[rows, :], dr_buf.at[buf], dr_sems.at[buf])]
            copies += [pltpu.make_async_copy(dp_groups[k].at[rows, :], dp_buf.at[buf, pl.ds(k * tm, tm), :],
                                             dpx_sems.at[buf, k]) for k in range(N_DEV)]
            return copies

        def gx_out(tile):
            return pltpu.make_async_copy(gx_buf, gx_hbm.at[pl.ds(tile * tm, tm), :], gx_sem)

        for cp in gx_in(0, 0):
            cp.start()
        for k in range(N_DEV):
            w_copy(k).wait()

        def gx_body(tile, _):
            buf = tile % 2
            for cp in gx_in(tile, buf):
                cp.wait()

            @pl.when(tile + 1 < gx_tiles)
            def _():
                for cp in gx_in(tile + 1, 1 - buf):
                    cp.start()

            val = ALPHA * dr_buf[buf]
            for k in range(N_DEV):
                val = val + _dot(dp_buf[buf, k * tm:(k + 1) * tm, :], xt_v[:, k * cw:(k + 1) * cw], NT)

            @pl.when(tile > 0)
            def _():
                gx_out(tile - 1).wait()

            gx_buf[...] = val
            gx_out(tile).start()
            return 0

        lax.fori_loop(0, gx_tiles, gx_body, 0)

        for d in range(N_DEV):
            @pl.when(d != me)
            def _(d=d):
                pltpu.make_async_remote_copy(
                    src_ref=small_ref, dst_ref=small_all.at[d], send_sem=sm_send.at[d], recv_sem=sm_recv.at[d],
                    device_id=(d // 4, (d // 2) % 2, d % 2), device_id_type=MESH).wait()
        total = small_all[0]
        for d in range(1, N_DEV):
            total = total + small_all[d]
        small_o[...] = total

        mine = 2 * x + y
        last = (N_DEV - 1) % 2

        def finish(a, n_rows, chip_sum, g_o):
            recv = (recv_in, recv_out)[a]
            for k in (X_UP, X_LOW, Y_UP, Y_LOW):
                ici_copy(a, k).wait_recv()
            for k in ici_place:
                ici_copy(a, k).wait_send()

            def total_rows(rows):
                g_o[rows, :] = chip_sum(rows) + recv[1, rows, :].astype(F32) + recv[2, rows, :].astype(F32)

            by_rows(0, n_rows, total_rows)

        finish(1, rows_out, lambda rows: own_out[mine, rows, :] + got_out[mine, rows, :], g_out_o)
        finish(0, d_model, lambda rows: acc[last, rows, :] + got_in[N_DEV // 2 - 1, rows, :], g_in_o)
        d2d_copy(0, N_DEV // 2 - 1).wait_send()
        gx_out(0).wait()

    vmem = pl.BlockSpec(memory_space=pltpu.VMEM)
    hbm = pl.BlockSpec(memory_space=pl.ANY)
    return pl.pallas_call(
        body, name="grad_w_reduce",
        in_specs=[hbm] * 7 + [vmem],
        out_specs=(hbm, vmem, vmem, vmem),
        out_shape=(jax.ShapeDtypeStruct((seq, d_model), F32), jax.ShapeDtypeStruct((d_model, cw), F32),
                   jax.ShapeDtypeStruct((rows_out, d_model), F32), jax.ShapeDtypeStruct(small.shape, F32)),
        scratch_shapes=[
            pltpu.VMEM((d_model, seq), BF16), pltpu.VMEM((2, seq, cw), BF16), pltpu.VMEM((2, d_model, cw), F32),
            pltpu.VMEM((4, d_model, cw), F32), pltpu.VMEM((3, d_model, cw), BF16), pltpu.VMEM((3, d_model, cw), BF16),
            pltpu.VMEM((4, rows_out, d_model), F32), pltpu.VMEM((4, rows_out, d_model), F32),
            pltpu.VMEM((3, rows_out, d_model), BF16), pltpu.VMEM((3, rows_out, d_model), BF16),
            pltpu.VMEM((N_DEV,) + small.shape, F32),
            pltpu.VMEM((2, tm, d_model), F32), pltpu.VMEM((tm, d_model), F32),
            pltpu.SemaphoreType.DMA, pltpu.SemaphoreType.DMA((2,)), pltpu.SemaphoreType.DMA((4,)),
            pltpu.SemaphoreType.DMA((2, 4)), pltpu.SemaphoreType.DMA((2, 4)),
            pltpu.SemaphoreType.DMA((2, 6)), pltpu.SemaphoreType.DMA((2, 6)),
            pltpu.SemaphoreType.DMA((N_DEV,)), pltpu.SemaphoreType.DMA((N_DEV,)),
            pltpu.SemaphoreType.DMA((2,)), pltpu.SemaphoreType.DMA((2, N_DEV)), pltpu.SemaphoreType.DMA((N_DEV,)),
            pltpu.SemaphoreType.DMA,
        ],
        compiler_params=_params(56),
    )(xt, *dp_parts, dr, win_all, gwo, small)


def _adamw_update(g, w, m, v, rows):
    n_rows, width = w.shape
    rows = min(rows, n_rows)

    def body(g_ref, w_ref, m_ref, v_ref, d_o, nm_o, nv_o):
        d_o[...], nm_o[...], nv_o[...] = _adamw(w_ref[...], g_ref[...], m_ref[...], v_ref[...])

    tile = pl.BlockSpec((rows, width), lambda i: (i, 0))
    shape = jax.ShapeDtypeStruct(w.shape, F32)
    return pl.pallas_call(
        body, name="adamw_update", grid=(n_rows // rows,), in_specs=[tile] * 4, out_specs=(tile,) * 3,
        out_shape=(shape,) * 3,
    )(g, w, m, v)


def _small_update(grads, weights, ms, vs):
    n = len(grads)

    def body(*refs):
        g_refs, w_refs, m_refs, v_refs = (refs[i * n:(i + 1) * n] for i in range(4))
        outs = refs[4 * n:]
        for i in range(n):
            delta, nm, nv = _adamw(w_refs[i][...], g_refs[i][...], m_refs[i][...], v_refs[i][...])
            outs[3 * i][...] = delta
            outs[3 * i + 1][...] = nm
            outs[3 * i + 2][...] = nv

    vmem = pl.BlockSpec(memory_space=pltpu.VMEM)
    out_shape = []
    for w in weights:
        out_shape += [jax.ShapeDtypeStruct(w.shape, F32)] * 3
    return pl.pallas_call(
        body, name="small_update", in_specs=[vmem] * (4 * n), out_specs=(vmem,) * (3 * n), out_shape=tuple(out_shape),
    )(*grads, *weights, *ms, *vs)


def _tile_sizes(seq):
    return dict(tm=seq // N_DEV, t_ln=min(512, seq), t_attn=min(128, seq), rc=min(256, seq), pairs=4)


def kernel(x, w_in, conv_w, w_out, ln_gain, ln_bias, loss_target, m_w_in, m_conv_w, m_w_out, m_ln_gain, m_ln_bias,
           v_w_in, v_conv_w, v_w_out, v_ln_gain, v_ln_bias):
    assert x.shape[0] == 1 and w_in.shape[0] == 1, "one sequence per device, depth 1"
    _, seq, d_model = x.shape
    cw = w_in.shape[2]
    conv_k, conv_cols = conv_w.shape[1], conv_w.shape[2]
    rows_out = w_out.shape[1]
    assert cw == d_model // 2 and cw % PAIR == 0 and conv_cols * N_DEV == cw and rows_out * N_DEV == d_model
    ts = _tile_sizes(seq)

    x2 = x.reshape(seq, d_model)
    target = loss_target.reshape(seq, d_model)
    me = 4 * lax.axis_index("x") + 2 * lax.axis_index("y") + lax.axis_index("c")

    conv_pad = jnp.pad(conv_w[0], ((0, SUBLANES - conv_k), (0, PAIR - conv_cols)))
    proj, xt, win_all, conv_all = _gather_proj(x2, w_in[0], conv_pad, ts["tm"])
    conv_full = conv_all[:, :conv_k, :conv_cols].transpose(1, 0, 2).reshape(conv_k, cw)
    conv_full = jnp.pad(conv_full, ((0, SUBLANES - conv_k), (0, 0)))

    mix_conv = _conv_fwd(proj, conv_full, ts["rc"])
    pairs = min(ts["pairs"], cw // PAIR)
    tri = _triangles(ts["t_attn"])
    o, mix_attn, tot, wout_all = _attn_fwd(proj, tri, w_out[0], ts["t_attn"], pairs)
    w_out_full = wout_all.reshape(d_model, d_model)
    dr, d_mix_conv, d_mix_attn, gwo, small = _out_ln(mix_conv, mix_attn, x2, target, ln_gain, ln_bias, w_out_full,
                                                     ts["t_ln"])
    dp_conv, d_taps = _conv_bwd(proj, conv_full, d_mix_conv, ts["rc"])
    dp_qz, dp_kv = _attn_bwd(proj, tri, o, tot, d_mix_attn, ts["t_attn"], pairs)
    small = small.at[ROW_CONV:ROW_CONV + conv_k, :cw].set(d_taps[:conv_k])
    grad_x, g_in, g_out, small_sum = _grad_w_reduce(
        xt, (dp_conv, dp_qz, dp_kv), dr, win_all, gwo, small, rows_out, 128, ts["tm"])
    d_in, nm_in, nv_in = _adamw_update(g_in, w_in[0], m_w_in[0], v_w_in[0], 256)

    loss = small_sum[ROW_LOSS, 0]
    g_gain = small_sum[ROW_GAIN:ROW_GAIN + 1]
    g_bias = small_sum[ROW_BIAS:ROW_BIAS + 1]
    g_conv = lax.dynamic_slice(small_sum, (ROW_CONV, me * conv_cols), (conv_k, conv_cols))
    upd = _small_update((g_out, g_conv, g_gain, g_bias), (w_out[0], conv_w[0], ln_gain, ln_bias),
                        (m_w_out[0], m_conv_w[0], m_ln_gain, m_ln_bias),
                        (v_w_out[0], v_conv_w[0], v_ln_gain, v_ln_bias))
    (d_out, nm_out, nv_out, d_conv, nm_conv, nv_conv, d_gain, nm_gain, nv_gain, d_bias, nm_bias, nv_bias) = upd

    lead = lambda a: a[None]
    return (loss, grad_x.reshape(1, seq, d_model), lead(g_in), lead(g_conv), lead(g_out), g_gain, g_bias,
            lead(d_in), lead(d_conv), lead(d_out), d_gain, d_bias,
            lead(nm_in), lead(nm_conv), lead(nm_out), nm_gain, nm_bias,
            lead(nv_in), lead(nv_conv), lead(nv_out), nv_gain, nv_bias)
```

```python
import jax
import jax.numpy as jnp
from jax import lax
from jax.experimental import pallas as pl
from jax.experimental.pallas import tpu as pltpu

F32 = jnp.float32
BF16 = jnp.bfloat16
MESH = pl.DeviceIdType.MESH

N_DEV = 8
HEAD_DIM = 64
PAIR = 128
SUBLANES = 8
LN_EPS = 1e-5
ALPHA = 2.0 ** 0.25
ADAM_LR, ADAM_B1, ADAM_B2, ADAM_EPS, ADAM_WD, ADAM_STEP = 0.001, 0.9, 0.999, 1e-08, 0.01, 10

ROW_GAIN, ROW_BIAS, ROW_CONV, ROW_LOSS = 0, 1, 2, 5

NT = (((1,), (1,)), ((), ()))
TN = (((0,), (0,)), ((), ()))


V7X_VMEM_BYTES = 64 * 1024 * 1024


def _params(vmem_mib):
    assert vmem_mib * 1024 * 1024 < V7X_VMEM_BYTES
    return pltpu.CompilerParams(vmem_limit_bytes=vmem_mib * 1024 * 1024)


def _dot(a, b, dims=None):
    if dims is None:
        return jnp.dot(a, b, preferred_element_type=F32)
    return lax.dot_general(a, b, dims, preferred_element_type=F32)


def _sigmoid(z):
    return 1.0 / (1.0 + jnp.exp(-z))


def _mesh_pos():
    return lax.axis_index("x"), lax.axis_index("y"), lax.axis_index("c")


def _adamw(w, g, m, v):
    nm = ADAM_B1 * m + (1.0 - ADAM_B1) * g
    nv = ADAM_B2 * v + (1.0 - ADAM_B2) * (g * g)
    m_hat = nm * (1.0 / (1.0 - ADAM_B1 ** ADAM_STEP))
    v_hat = nv * (1.0 / (1.0 - ADAM_B2 ** ADAM_STEP))
    delta = -ADAM_LR * (m_hat / (jnp.sqrt(v_hat) + ADAM_EPS) + ADAM_WD * w)
    return delta, nm, nv


def _gather_proj(x, w_in_s, conv_s, tm):
    seq, d_model = x.shape
    cw = w_in_s.shape[1]
    n_tiles = seq // tm
    half = d_model // 2
    SIB, X_UP, X_LOW, Y_UP, Y_LOW, VIA_Y, VIA_X, ON_X, ON_Y, ON_DIAG = range(10)

    def body(x_hbm, win_ref, conv_ref, proj_hbm, xt_hbm, win_all, conv_all,
             xb, x_stage, o_stage, xt_stage, x_sems, o_sems, xt_sems, w_send, w_recv, send_sems, recv_sems):
        x, y, c = _mesh_pos()
        me = (x, y, c)
        sibling = (x, y, 1 - c)
        x_nbr, y_nbr, diag = (1 - x, y), (x, 1 - y), (1 - x, 1 - y)
        chips = [x_nbr, y_nbr, diag]
        small = (conv_all,)

        def slot(pos):
            return 4 * pos[0] + 2 * pos[1] + pos[2]

        def x_copy(tile, buf):
            return pltpu.make_async_copy(x_hbm.at[pl.ds(tile * tm, tm), :], x_stage.at[buf], x_sems.at[buf])

        x_copy(0, 0).start()
        win_all[slot(me)] = win_ref[...].astype(BF16)
        conv_all[slot(me)] = conv_ref[...]

        def w_copy(k, block, part, to):
            ref = win_all.at[slot(block)]
            if part is not None:
                ref = ref.at[pl.ds(part * half, half), :]
            return pltpu.make_async_remote_copy(
                src_ref=ref, dst_ref=ref, send_sem=w_send.at[k], recv_sem=w_recv.at[k],
                device_id=to, device_id_type=MESH)

        def copy(a, k, block, to):
            ref = small[a].at[slot(block)]
            return pltpu.make_async_remote_copy(
                src_ref=ref, dst_ref=ref, send_sem=send_sems.at[a, k], recv_sem=recv_sems.at[a, k],
                device_id=to, device_id_type=MESH)

        sends = [w_copy(SIB, me, None, sibling),
                 w_copy(X_UP, me, 0, (*x_nbr, c)), w_copy(Y_LOW, me, 1, (*y_nbr, c)),
                 w_copy(X_LOW, me, 1, (*x_nbr, c)), w_copy(Y_UP, me, 0, (*y_nbr, c))]
        for a in range(len(small)):
            sends.append(copy(a, 0, me, sibling))
            sends += [copy(a, 1 + j, me, (*chip, c)) for j, chip in enumerate(chips)]
        for cp in sends:
            cp.start()

        def o_copy(group, tile, buf):
            return pltpu.make_async_copy(o_stage.at[buf], proj_hbm.at[group, pl.ds(tile * tm, tm), :], o_sems.at[buf])

        def xt_copy(tile, buf):
            return pltpu.make_async_copy(xt_stage.at[buf], xt_hbm.at[:, pl.ds(tile * tm, tm)], xt_sems.at[buf])

        def project(order, group, first_pass):
            def tile_body(tile, _):
                if first_pass:
                    buf = tile % 2
                    x_copy(tile, buf).wait()

                    @pl.when(tile + 1 < n_tiles)
                    def _():
                        x_copy(tile + 1, 1 - buf).start()

                    xv = x_stage[buf]
                    xb[tile] = xv.astype(BF16)

                    @pl.when(tile >= 2)
                    def _():
                        xt_copy(tile - 2, buf).wait()

                    xt_stage[buf] = xv.T.astype(BF16)
                    xt_copy(tile, buf).start()
                count = order * n_tiles + tile
                obuf = count % 2

                @pl.when(count >= 2)
                def _():
                    o_copy(group, tile, obuf).wait()

                o_stage[obuf] = _dot(xb[tile], win_all[group]).astype(BF16)
                o_copy(group, tile, obuf).start()
                return 0

            lax.fori_loop(0, n_tiles, tile_body, 0)

        def start(cp):
            cp.start()
            sends.append(cp)

        def small_pass_on(j):
            for a in range(len(small)):
                copy(a, 1 + j, (*chips[j], c), me).wait_recv()
                start(copy(a, 4 + j, (*chips[j], c), sibling))

        def small_from_sibling(k):
            for a in range(len(small)):
                copy(a, k, sibling, me).wait_recv()

        project(0, slot(me), True)
        w_copy(SIB, sibling, None, me).wait_recv()
        small_from_sibling(0)
        project(1, slot(sibling), False)
        w_copy(X_UP, (*x_nbr, c), 0, me).wait_recv()
        start(w_copy(VIA_Y, (*x_nbr, c), 0, (*y_nbr, c)))
        w_copy(Y_LOW, (*y_nbr, c), 1, me).wait_recv()
        start(w_copy(VIA_X, (*y_nbr, c), 1, (*x_nbr, c)))
        w_copy(X_LOW, (*x_nbr, c), 1, me).wait_recv()
        start(w_copy(ON_X, (*x_nbr, c), None, sibling))
        small_pass_on(0)
        project(2, slot((*x_nbr, c)), False)
        w_copy(Y_UP, (*y_nbr, c), 0, me).wait_recv()
        start(w_copy(ON_Y, (*y_nbr, c), None, sibling))
        small_pass_on(1)
        project(3, slot((*y_nbr, c)), False)
        w_copy(ON_X, (*x_nbr, 1 - c), None, me).wait_recv()
        small_from_sibling(4)
        project(4, slot((*x_nbr, 1 - c)), False)
        w_copy(ON_Y, (*y_nbr, 1 - c), None, me).wait_recv()
        small_from_sibling(5)
        project(5, slot((*y_nbr, 1 - c)), False)
        w_copy(VIA_Y, (*diag, c), 0, me).wait_recv()
        w_copy(VIA_X, (*diag, c), 1, me).wait_recv()
        start(w_copy(ON_DIAG, (*diag, c), None, sibling))
        small_pass_on(2)
        project(6, slot((*diag, c)), False)
        w_copy(ON_DIAG, (*diag, 1 - c), None, me).wait_recv()
        small_from_sibling(6)
        project(7, slot((*diag, 1 - c)), False)

        for buf in range(2):
            o_copy(0, 0, buf).wait()
        for buf in range(min(2, n_tiles)):
            xt_copy(0, buf).wait()
        for cp in sends:
            cp.wait_send()

    vmem = pl.BlockSpec(memory_space=pltpu.VMEM)
    hbm = pl.BlockSpec(memory_space=pl.ANY)
    return pl.pallas_call(
        body, name="gather_proj",
        out_shape=(jax.ShapeDtypeStruct((N_DEV, seq, cw), BF16),
                   jax.ShapeDtypeStruct((d_model, seq), BF16),
                   jax.ShapeDtypeStruct((N_DEV, d_model, cw), BF16),
                   jax.ShapeDtypeStruct((N_DEV,) + conv_s.shape, F32)),
        in_specs=[hbm, vmem, vmem], out_specs=(hbm, hbm, vmem, vmem),
        scratch_shapes=[
            pltpu.VMEM((n_tiles, tm, d_model), BF16), pltpu.VMEM((2, tm, d_model), F32),
            pltpu.VMEM((2, tm, cw), BF16), pltpu.VMEM((2, d_model, tm), BF16),
            pltpu.SemaphoreType.DMA((2,)), pltpu.SemaphoreType.DMA((2,)), pltpu.SemaphoreType.DMA((2,)),
            pltpu.SemaphoreType.DMA((10,)), pltpu.SemaphoreType.DMA((10,)),
            pltpu.SemaphoreType.DMA((1, 7)), pltpu.SemaphoreType.DMA((1, 7))],
        compiler_params=_params(48),
    )(x, w_in_s, conv_s)


def _conv_taps(ext, w_ref, rc):
    u0 = ext[SUBLANES:SUBLANES + rc]
    u1 = pltpu.roll(ext, 1, 0)[SUBLANES:SUBLANES + rc]
    u2 = pltpu.roll(ext, 2, 0)[SUBLANES:SUBLANES + rc]
    return w_ref[2:3, :] * u0 + w_ref[1:2, :] * u1 + w_ref[0:1, :] * u2, u0, u1, u2


def _conv_fwd(proj, conv_full, rc):
    _, seq, cw = proj.shape

    def body(b_ref, c_ref, h_ref, z_ref, w_ref, o_ref, u_scr):
        u_scr[0:SUBLANES, :] = jnp.zeros((SUBLANES, PAIR), F32)

        def fill(r, _):
            base = pl.multiple_of(r * rc, rc)
            rows = pl.ds(base, rc)
            u_scr[pl.ds(base + SUBLANES, rc), :] = c_ref[rows, :].astype(F32) * h_ref[rows, :].astype(F32)
            return 0

        lax.fori_loop(0, seq // rc, fill, 0)

        def out(r, _):
            base = pl.multiple_of(r * rc, rc)
            rows = pl.ds(base, rc)
            ext = u_scr[pl.ds(base, rc + SUBLANES), :]
            y, _, _, _ = _conv_taps(ext, w_ref, rc)
            z = z_ref[rows, :].astype(F32)
            o_ref[rows, :] = (z * _sigmoid(z) * b_ref[rows, :].astype(F32) * y).astype(BF16)
            return 0

        lax.fori_loop(0, seq // rc, out, 0)

    def chunk(j):
        return pl.BlockSpec((None, seq, PAIR), lambda cb, j=j: (j, 0, cb))

    return pl.pallas_call(
        body, name="conv_fwd", grid=(cw // PAIR,),
        in_specs=[chunk(0), chunk(1), chunk(2), chunk(3), pl.BlockSpec((SUBLANES, PAIR), lambda cb: (0, cb))],
        out_specs=pl.BlockSpec((seq, PAIR), lambda cb: (0, cb)),
        out_shape=jax.ShapeDtypeStruct((seq, cw), BF16),
        scratch_shapes=[pltpu.VMEM((seq + SUBLANES, PAIR), F32)],
    )(proj, proj, proj, proj, conv_full)


def _conv_bwd(proj, conv_full, d_mix_conv, rc):
    _, seq, cw = proj.shape

    def body(b_ref, c_ref, h_ref, z_ref, w_ref, g_ref, dp_ref, dw_ref, u_scr, dy_scr):
        u_scr[0:SUBLANES, :] = jnp.zeros((SUBLANES, PAIR), F32)
        dy_scr[seq:seq + SUBLANES, :] = jnp.zeros((SUBLANES, PAIR), F32)

        def fill(r, _):
            base = pl.multiple_of(r * rc, rc)
            rows = pl.ds(base, rc)
            u_scr[pl.ds(base + SUBLANES, rc), :] = c_ref[rows, :].astype(F32) * h_ref[rows, :].astype(F32)
            return 0

        lax.fori_loop(0, seq // rc, fill, 0)

        def gate(r, acc):
            base = pl.multiple_of(r * rc, rc)
            rows = pl.ds(base, rc)
            ext = u_scr[pl.ds(base, rc + SUBLANES), :]
            y, u0, u1, u2 = _conv_taps(ext, w_ref, rc)
            z = z_ref[rows, :].astype(F32)
            b = b_ref[rows, :].astype(F32)
            g = g_ref[rows, :].astype(F32)
            sig = _sigmoid(z)
            dp_ref[3, rows, :] = (g * b * y * (sig * (1.0 + z * (1.0 - sig)))).astype(BF16)
            gs = g * (z * sig)
            dp_ref[0, rows, :] = (gs * y).astype(BF16)
            dy = gs * b
            dy_scr[rows, :] = dy
            a0, a1, a2 = acc
            return (a0 + jnp.sum(dy * u2, axis=0, keepdims=True),
                    a1 + jnp.sum(dy * u1, axis=0, keepdims=True),
                    a2 + jnp.sum(dy * u0, axis=0, keepdims=True))

        zero = jnp.zeros((1, PAIR), F32)
        a0, a1, a2 = lax.fori_loop(0, seq // rc, gate, (zero, zero, zero))
        dw_ref[...] = jnp.zeros((SUBLANES, PAIR), F32)
        dw_ref[0:1, :] = a0
        dw_ref[1:2, :] = a1
        dw_ref[2:3, :] = a2

        def back(r, _):
            base = pl.multiple_of(r * rc, rc)
            rows = pl.ds(base, rc)
            ext = dy_scr[pl.ds(base, rc + SUBLANES), :]
            n = rc + SUBLANES
            d0 = ext[0:rc]
            d1 = pltpu.roll(ext, n - 1, 0)[0:rc]
            d2 = pltpu.roll(ext, n - 2, 0)[0:rc]
            du = w_ref[2:3, :] * d0 + w_ref[1:2, :] * d1 + w_ref[0:1, :] * d2
            dp_ref[1, rows, :] = (du * h_ref[rows, :].astype(F32)).astype(BF16)
            dp_ref[2, rows, :] = (du * c_ref[rows, :].astype(F32)).astype(BF16)
            return 0

        lax.fori_loop(0, seq // rc, back, 0)

    def chunk(j):
        return pl.BlockSpec((None, seq, PAIR), lambda cb, j=j: (j, 0, cb))

    return pl.pallas_call(
        body, name="conv_bwd", grid=(cw // PAIR,),
        in_specs=[chunk(0), chunk(1), chunk(2), chunk(3), pl.BlockSpec((SUBLANES, PAIR), lambda cb: (0, cb)),
                  pl.BlockSpec((seq, PAIR), lambda cb: (0, cb))],
        out_specs=(pl.BlockSpec((4, seq, PAIR), lambda cb: (0, 0, cb)),
                   pl.BlockSpec((SUBLANES, PAIR), lambda cb: (0, cb))),
        out_shape=(jax.ShapeDtypeStruct((4, seq, cw), BF16), jax.ShapeDtypeStruct((SUBLANES, cw), F32)),
        scratch_shapes=[pltpu.VMEM((seq + SUBLANES, PAIR), F32), pltpu.VMEM((seq + SUBLANES, PAIR), F32)],
    )(proj, proj, proj, proj, conv_full, d_mix_conv)


SKIP_CARRY = 104.0
LOG2_E = 1.4426950408889634
LANE_TOT0, LANE_TOT1, LANE_FIRST, LANE_WHOLE = 0, 1, 2, 3
FAST_BLOCKS = 3
EARLY_ROWS = 32


def _triangles(t):
    row = lax.broadcasted_iota(jnp.int32, (2 * t, 2 * t), 0)
    col = lax.broadcasted_iota(jnp.int32, (2 * t, 2 * t), 1)
    same = (row < t) == (col < t)
    upper = jnp.logical_and(same, row > col).astype(BF16)
    lower = jnp.logical_and(same, row < col).astype(BF16)
    return jnp.stack([jnp.concatenate([upper, upper], axis=0), jnp.concatenate([lower, lower], axis=0)])


def _pair_masks(t):
    lane = lax.broadcasted_iota(jnp.int32, (t, PAIR), 1)
    qrow = lax.broadcasted_iota(jnp.int32, (t, 2 * t), 0)
    kcol = lax.broadcasted_iota(jnp.int32, (t, 2 * t), 1)
    strict = jnp.where(kcol < t, kcol, kcol - t) < qrow
    return lane, lane < HEAD_DIM, strict


def _by_head(x, head0):
    zero = jnp.zeros_like(x)
    return jnp.concatenate([jnp.where(head0, x, zero), jnp.where(head0, zero, x)], axis=0)


def _hi_lo(a):
    hi = a.astype(BF16)
    lo = (a - hi.astype(F32)).astype(BF16)
    return jnp.concatenate([hi, lo], axis=1)


def _softplus_parts(z, strict, masked):
    spu = jnp.maximum(z, 0.0) + jnp.log(1.0 + jnp.exp2(jnp.abs(z) * -LOG2_E))
    return z - spu, (jnp.where(strict, spu, 0.0) if masked else spu)


def _stacked_dot(parts, rhs):
    out = _dot(jnp.concatenate(parts, axis=0), rhs)
    ends = [0]
    for p in parts:
        ends.append(ends[-1] + p.shape[0])
    return [out[a:b] for a, b in zip(ends[:-1], ends[1:])]


def _splice(whole, rows, part):
    pieces = ([whole[:rows[0]]] if rows[0] > 0 else []) + [part]
    if rows[1] < whole.shape[0]:
        pieces.append(whole[rows[1]:])
    return part if len(pieces) == 1 else jnp.concatenate(pieces, axis=0)


def _attn_fwd(proj, tri, w_out_s, t, pp):
    _, seq, cw = proj.shape
    scale = HEAD_DIM ** -0.5
    width = pp * PAIR

    n_steps = (cw // width) * (seq // t)
    pass_on_step = min(8, n_steps - 1)

    def body(q_ref, k_ref, v_ref, za_ref, tri_ref, wout_ref, o_ref, mix_ref, tot_ref, wout_all, acc_ref,
             own_w, own_sem, send_sems, recv_sems):
        i = pl.program_id(1)
        step_no = pl.program_id(0) * (seq // t) + i
        mx, my, mc = _mesh_pos()
        me = (mx, my, mc)
        sibling = (mx, my, 1 - mc)
        chips = [(1 - mx, my), (mx, 1 - my), (1 - mx, 1 - my)]

        def place(block):
            return wout_all.at[4 * block[0] + 2 * block[1] + block[2]]

        def w_copy(k, block, to, src=None):
            return pltpu.make_async_remote_copy(
                src_ref=place(block) if src is None else src, dst_ref=place(block), send_sem=send_sems.at[k],
                recv_sem=recv_sems.at[k], device_id=to, device_id_type=MESH)

        own_copy = pltpu.make_async_copy(own_w, place(me), own_sem)

        @pl.when(step_no == 0)
        def _():
            own_w[...] = wout_ref[...].astype(BF16)
            own_copy.start()
            w_copy(0, me, sibling, own_w).start()
            for j, chip in enumerate(chips):
                w_copy(1 + j, me, (*chip, mc), own_w).start()

        @pl.when(step_no == pass_on_step)
        def _():
            for j, chip in enumerate(chips):
                w_copy(1 + j, (*chip, mc), me).wait_recv()
                w_copy(4 + j, (*chip, mc), sibling).start()

        @pl.when(step_no == n_steps - 1)
        def _():
            own_copy.wait()
            w_copy(0, sibling, me).wait_recv()
            for j, chip in enumerate(chips):
                w_copy(4 + j, (*chip, 1 - mc), me).wait_recv()
            for k in range(7):
                w_copy(k, me, sibling, own_w).wait_send()

        lane, head0, strict = _pair_masks(t)
        upper = tri_ref[0]
        q = q_ref[...] * scale
        acc_ref[...] = jnp.zeros_like(acc_ref)

        def sweep(blocks, state):
            staged = []
            for j, masked, rows in blocks:
                start = pl.multiple_of(j * t, t)
                kb = k_ref[pl.ds(start, t), :]
                vb = v_ref[pl.ds(start, t), :]
                for p in range(pp):
                    cols = slice(p * PAIR, (p + 1) * PAIR)
                    z = _dot(q[rows[0]:rows[1], cols], _by_head(kb[:, cols], head0), NT)
                    ls, sp = _softplus_parts(z, strict[rows[0]:rows[1]], masked)
                    staged.append((p, masked, rows, ls, sp, _by_head(vb[:, cols], head0)))
            afters = _stacked_dot([_hi_lo(sp) for _, _, _, _, sp, _ in staged], upper)
            state = list(state)
            for (p, masked, rows, ls, sp, v2), after in zip(staged, afters):
                c0, c1 = state[p]
                part = slice(rows[0], rows[1])
                x = ls - after
                w = jnp.exp(jnp.concatenate([x[:, :t] - c0[part], x[:, t:] - c1[part]], axis=1))
                if masked:
                    w = jnp.where(strict[part], w, 0.0)
                acc_ref[part, p * PAIR:(p + 1) * PAIR] += _dot(w.astype(BF16), v2)
                state[p] = (_splice(c0, rows, c0[part] + (after[:, 0:1] + sp[:, 0:1])),
                            _splice(c1, rows, c1[part] + (after[:, t:t + 1] + sp[:, t:t + 1])))
            return tuple(state)

        def unfinished(state, rows):
            m = state[0][0]
            for p in range(pp):
                m = jnp.minimum(m, jnp.minimum(state[p][0], state[p][1]))
            return jnp.min(m[rows[0]:rows[1]]) < SKIP_CARRY

        every = (0, t)
        early, late = (0, min(EARLY_ROWS, t)), (min(EARLY_ROWS, t), t)

        def step(js):
            state = sweep(((js[0], False, every),), js[1])
            return js[0] - 1, state, unfinished(state, every)

        def fast():
            blocks = tuple((i - b, b == 0, every) for b in range(FAST_BLOCKS - 1))
            state = sweep(blocks + ((i - (FAST_BLOCKS - 1), False, early),), init)
            if late[0] == late[1]:
                return state, jnp.bool_(True)
            whole = unfinished(state, late)
            state = lax.cond(whole, lambda: sweep(((i - (FAST_BLOCKS - 1), False, late),), state), lambda: state)
            return state, whole

        zcol = jnp.zeros((t, 1), F32)
        init = tuple((zcol, zcol) for _ in range(pp))
        many = i >= FAST_BLOCKS - 1
        state, whole = lax.cond(many, fast, lambda: (sweep(((i, True, every),), init), jnp.bool_(True)))
        j_end, state, _ = lax.while_loop(
            lambda js: jnp.logical_and(js[0] >= 0, js[2]), step,
            (jnp.where(many, i - FAST_BLOCKS, i - 1), state, unfinished(state, every)))
        first = (j_end + 1).astype(F32)
        notes = jnp.where(lane == LANE_FIRST, first, whole.astype(F32))
        za = za_ref[...].astype(F32)
        for p in range(pp):
            c0, c1 = state[p]
            cols = slice(p * PAIR, (p + 1) * PAIR)
            zp = za[:, cols]
            acc = acc_ref[:, cols]
            o_ref[:, cols] = acc.astype(BF16)
            mix_ref[:, cols] = (zp * _sigmoid(zp) * acc).astype(BF16)
            tot_ref[:, cols] = jnp.where(lane == LANE_TOT0, c0, jnp.where(lane == LANE_TOT1, c1, notes))

    def tile(j):
        return pl.BlockSpec((None, t, width), lambda g, i, j=j: (j, i, g))

    def full(j):
        return pl.BlockSpec((None, seq, width), lambda g, i, j=j: (j, 0, g))

    out_tile = pl.BlockSpec((t, width), lambda g, i: (i, g))
    return pl.pallas_call(
        body, name="attn_fwd", grid=(cw // width, seq // t),
        in_specs=[tile(4), full(5), full(6), tile(7), pl.BlockSpec(tri.shape, lambda g, i: (0, 0, 0)),
                  pl.BlockSpec(memory_space=pltpu.VMEM)],
        out_specs=(out_tile, out_tile, out_tile, pl.BlockSpec(memory_space=pl.ANY)),
        out_shape=(jax.ShapeDtypeStruct((seq, cw), BF16), jax.ShapeDtypeStruct((seq, cw), BF16),
                   jax.ShapeDtypeStruct((seq, cw), F32),
                   jax.ShapeDtypeStruct((N_DEV,) + w_out_s.shape, BF16)),
        scratch_shapes=[pltpu.VMEM((t, width), F32), pltpu.VMEM(w_out_s.shape, BF16), pltpu.SemaphoreType.DMA,
                        pltpu.SemaphoreType.DMA((7,)), pltpu.SemaphoreType.DMA((7,))],
    )(proj, proj, proj, proj, tri, w_out_s)


def _attn_bwd(proj, tri, o, tot, d_mix_attn, t, pp):
    _, seq, cw = proj.shape
    nb = seq // t
    scale = HEAD_DIM ** -0.5
    width = pp * PAIR

    def body(q_ref, k_ref, v_ref, za_ref, tri_ref, o_ref, tot_ref, g_ref, dqz_ref, dkv_ref, dk_acc, dv_acc, dq_acc):
        i = pl.program_id(1)
        dq_acc[...] = jnp.zeros_like(dq_acc)

        @pl.when(i == 0)
        def _():
            dk_acc[...] = jnp.zeros_like(dk_acc)
            dv_acc[...] = jnp.zeros_like(dv_acc)

        _, head0, strict = _pair_masks(t)
        upper, lower = tri_ref[0], tri_ref[1, 0:2 * t, :]
        za = za_ref[...].astype(F32)
        g = g_ref[...].astype(F32)
        sig = _sigmoid(za)
        dqz_ref[1] = (g * o_ref[...].astype(F32) * (sig * (1.0 + za * (1.0 - sig)))).astype(BF16)
        do = (g * (za * sig)).astype(BF16)
        q = q_ref[...] * scale
        tot_v = tot_ref[...]
        q2, do2, init = [], [], []
        zcol = jnp.zeros((t, 1), F32)
        for p in range(pp):
            cols = slice(p * PAIR, (p + 1) * PAIR)
            q2.append(_by_head(q[:, cols], head0))
            do2.append(_by_head(do[:, cols], head0))
            tp = tot_v[:, cols]
            init.append(((tp[:, LANE_TOT0:LANE_TOT0 + 1], tp[:, LANE_TOT1:LANE_TOT1 + 1]), (zcol, zcol)))
        first = jnp.clip(tot_v[0:1, LANE_FIRST:LANE_FIRST + 1], 0.0, i.astype(F32)).astype(jnp.int32)[0, 0]

        def sweep(blocks, state):
            staged = []
            for j, masked, rows in blocks:
                start = pl.multiple_of(j * t, t)
                kb = k_ref[pl.ds(start, t), :]
                vb = v_ref[pl.ds(start, t), :]
                part = slice(rows[0], rows[1])
                for p in range(pp):
                    cols = slice(p * PAIR, (p + 1) * PAIR)
                    k2 = _by_head(kb[:, cols], head0)
                    z = _dot(q[part, cols], k2, NT)
                    ls, sp = _softplus_parts(z, strict[part], masked)
                    da = _dot(do[part, cols], _by_head(vb[:, cols], head0), NT)
                    staged.append((p, masked, rows, k2, ls, sp, da))
            afters = _stacked_dot([_hi_lo(sp) for _, _, _, _, _, sp, _ in staged], upper)
            state = list(state)
            weights, ggs = [], []
            for (p, masked, rows, k2, ls, sp, da), after in zip(staged, afters):
                (s0, s1), befores = state[p]
                part = slice(rows[0], rows[1])
                n0 = s0[part] - (after[:, 0:1] + sp[:, 0:1])
                n1 = s1[part] - (after[:, t:t + 1] + sp[:, t:t + 1])
                x = ls - after
                a = jnp.exp(jnp.concatenate([x[:, :t] - n0, x[:, t:] - n1], axis=1))
                if masked:
                    a = jnp.where(strict[part], a, 0.0)
                state[p] = ((_splice(s0, rows, n0), _splice(s1, rows, n1)), befores)
                weights.append(a.astype(BF16))
                ggs.append(a * da)
            pres = _stacked_dot([gg.astype(BF16) for gg in ggs], lower)
            dzs = []
            for (p, masked, rows, k2, ls, sp, da), gg, pre in zip(staged, ggs, pres):
                rests, (b0, b1) = state[p]
                part = slice(rows[0], rows[1])
                y = gg + pre
                dz = gg - jnp.exp(ls) * jnp.concatenate([y[:, :t] + b0[part], y[:, t:] + b1[part]], axis=1)
                if masked:
                    dz = jnp.where(strict[part], dz, 0.0)
                dzb = dz.astype(BF16)
                dzs.append(dzb)
                dq_acc[part, p * PAIR:(p + 1) * PAIR] += _dot(dzb, k2)
                state[p] = (rests, (_splice(b0, rows, b0[part] + y[:, t - 1:t]),
                                    _splice(b1, rows, b1[part] + y[:, 2 * t - 1:2 * t])))
            first_row = pl.multiple_of(blocks[0][0] * t, t)
            n_rows = len(blocks) * t
            for p in range(pp):
                cols = slice(p * PAIR, (p + 1) * PAIR)

                def by_key(tiles):
                    out = []
                    for n, m in tiles:
                        rows = staged[n][2]
                        m = _splice(jnp.zeros((t, 2 * t), BF16), rows, m)
                        out.append(jnp.concatenate([m[:, :t], m[:, t:]], axis=0).T)
                    return jnp.concatenate(out, axis=0)

                mine = [n for n in range(len(staged)) if staged[n][0] == p]
                dk_acc[pl.ds(first_row, n_rows), cols] += _dot(by_key([(n, dzs[n]) for n in mine]), q2[p])
                dv_acc[pl.ds(first_row, n_rows), cols] += _dot(by_key([(n, weights[n]) for n in mine]), do2[p])
            return tuple(state)

        every = (0, t)
        early = (0, min(EARLY_ROWS, t))
        many = i >= FAST_BLOCKS - 1
        last_single = jnp.where(many, i - (FAST_BLOCKS - 1), i)
        state = lax.fori_loop(first, last_single, lambda j, s: sweep(((j, False, every),), s), tuple(init))

        def fast(rows):
            blocks = tuple((i - b, b == 0, every) for b in range(FAST_BLOCKS - 2, -1, -1))
            return sweep(((i - (FAST_BLOCKS - 1), False, rows),) + blocks, state)

        whole = tot_v[0:1, LANE_WHOLE:LANE_WHOLE + 1].astype(jnp.int32)[0, 0] > 0
        state = lax.cond(
            many,
            lambda: lax.cond(whole, lambda: fast(every), lambda: fast(early)),
            lambda: sweep(((i, True, every),), state))
        dqz_ref[0] = (dq_acc[...] * scale).astype(BF16)

        @pl.when(i == nb - 1)
        def _():
            dkv_ref[0] = dk_acc[...].astype(BF16)
            dkv_ref[1] = dv_acc[...].astype(BF16)

    def tile(j):
        return pl.BlockSpec((None, t, width), lambda g, i, j=j: (j, i, g))

    def full(j):
        return pl.BlockSpec((None, seq, width), lambda g, i, j=j: (j, 0, g))

    flat_tile = pl.BlockSpec((t, width), lambda g, i: (i, g))
    return pl.pallas_call(
        body, name="attn_bwd", grid=(cw // width, nb),
        in_specs=[tile(4), full(5), full(6), tile(7), pl.BlockSpec(tri.shape, lambda g, i: (0, 0, 0)),
                  flat_tile, flat_tile, flat_tile],
        out_specs=(pl.BlockSpec((2, t, width), lambda g, i: (0, i, g)),
                   pl.BlockSpec((2, seq, width), lambda g, i: (0, 0, g))),
        out_shape=(jax.ShapeDtypeStruct((2, seq, cw), BF16), jax.ShapeDtypeStruct((2, seq, cw), BF16)),
        scratch_shapes=[pltpu.VMEM((seq, width), F32), pltpu.VMEM((seq, width), F32), pltpu.VMEM((t, width), F32)],
        compiler_params=_params(48),
    )(proj, proj, proj, proj, tri, o, tot, d_mix_attn)


def _out_ln(mix_conv, mix_attn, x, target, gain, bias, w_out, tm):
    seq, d_model = x.shape
    cw = mix_conv.shape[1]
    inv_d = 1.0 / d_model

    def body(mc_ref, ma_ref, x_ref, t_ref, gain_ref, bias_ref, w_ref, dr_ref, dmc_ref, dma_ref, gwo_ref, small_ref):
        @pl.when(pl.program_id(0) == 0)
        def _():
            gwo_ref[...] = jnp.zeros_like(gwo_ref)
            small_ref[...] = jnp.zeros_like(small_ref)

        mix = jnp.concatenate([mc_ref[...], ma_ref[...]], axis=1)
        w = w_ref[...]
        r = ALPHA * x_ref[...] + _dot(mix, w)
        mu = jnp.sum(r, axis=1, keepdims=True) * inv_d
        xc = r - mu
        var = jnp.sum(xc * xc, axis=1, keepdims=True) * inv_d
        rstd = lax.rsqrt(var + LN_EPS)
        xhat = xc * rstd
        gain_v = gain_ref[...]
        err = xhat * gain_v + bias_ref[...] - t_ref[...]
        row_loss = jnp.sum(err * err, axis=1, keepdims=True)
        loss = (0.5 * inv_d) * jnp.sum(row_loss, axis=0, keepdims=True)
        dy = err * inv_d
        small_ref[ROW_GAIN:ROW_GAIN + 1, :] += jnp.sum(dy * xhat, axis=0, keepdims=True)
        small_ref[ROW_BIAS:ROW_BIAS + 1, :] += jnp.sum(dy, axis=0, keepdims=True)
        small_ref[ROW_LOSS:ROW_LOSS + 1, :] += jnp.broadcast_to(loss, (1, d_model))
        dxhat = dy * gain_v
        m1 = jnp.sum(dxhat, axis=1, keepdims=True) * inv_d
        m2 = jnp.sum(dxhat * xhat, axis=1, keepdims=True) * inv_d
        dr = rstd * (dxhat - m1 - xhat * m2)
        dr_ref[...] = dr
        drb = dr.astype(BF16)
        dmix = _dot(drb, w, NT)
        dmc_ref[...] = dmix[:, :cw].astype(BF16)
        dma_ref[...] = dmix[:, cw:].astype(BF16)
        gwo_ref[...] += _dot(mix, drb, TN)

    def rows(width):
        return pl.BlockSpec((tm, width), lambda i: (i, 0))

    def whole(shape):
        return pl.BlockSpec(shape, lambda i: (0, 0))

    return pl.pallas_call(
        body, name="out_ln", grid=(seq // tm,),
        in_specs=[rows(cw), rows(cw), rows(d_model), rows(d_model), whole((1, d_model)), whole((1, d_model)),
                  whole((d_model, d_model))],
        out_specs=(rows(d_model), rows(cw), rows(cw), whole((d_model, d_model)), whole((SUBLANES, d_model))),
        out_shape=(jax.ShapeDtypeStruct((seq, d_model), F32), jax.ShapeDtypeStruct((seq, cw), BF16),
                   jax.ShapeDtypeStruct((seq, cw), BF16), jax.ShapeDtypeStruct((d_model, d_model), F32),
                   jax.ShapeDtypeStruct((SUBLANES, d_model), F32)),
        compiler_params=_params(48),
    )(mix_conv, mix_attn, x, target, gain, bias, w_out)


_DP_OF_GROUP = ((0, 0), (0, 1), (0, 2), (0, 3), (1, 0), (2, 0), (2, 1), (1, 1))


def _grad_w_reduce(xt, dp_parts, dr, win_all, gwo, small, rows_out, row_chunk, tm):
    d_model, seq = xt.shape
    nch, _, cw = win_all.shape
    gx_tiles = seq // tm
    assert nch * tm == seq and nch * cw == seq, "needs S == 8 * tm == 8 * CW"

    def body(xt_hbm, dpa, dpb, dpc, dr_hbm, win_hbm, gwo_ref, small_ref,
             gx_hbm, g_in_o, g_out_o, small_o,
             xt_v, dp_buf, acc, got_in, send_in, recv_in, own_out, got_out, send_out, recv_out, small_all,
             dr_buf, gx_buf,
             xt_sem, dp_sems, loc_sems, d2d_send, d2d_recv, ici_send, ici_recv, sm_send, sm_recv,
             dr_sems, dpx_sems, w_sems, gx_sem):
        x, y, c = _mesh_pos()
        me = 4 * x + 2 * y + c
        sibling = (x, y, 1 - c)
        chips = [(1 - x, 1 - y), (1 - x, y), (x, 1 - y)]
        owners = [(*chip, cc) for chip in chips for cc in (1 - c, c)] + [sibling, (x, y, c)]
        group_of = [4 * o[0] + 2 * o[1] + o[2] for o in owners]
        dp_parts_ = (dpa, dpb, dpc)
        dp_groups = [dp_parts_[arr].at[idx] for arr, idx in _DP_OF_GROUP]

        xt_copy = pltpu.make_async_copy(xt_hbm, xt_v, xt_sem)
        xt_copy.start()

        def dp_start(step):
            for k in range(N_DEV):
                @pl.when(group_of[step] == k)
                def _(k=k):
                    pltpu.make_async_copy(dp_groups[k], dp_buf.at[step % 2], dp_sems.at[step % 2]).start()

        def dp_wait(step):
            pltpu.make_async_copy(dp_groups[0], dp_buf.at[step % 2], dp_sems.at[step % 2]).wait()

        dp_start(0)

        small_all[me] = small_ref[...]
        for d in range(N_DEV):
            @pl.when(d != me)
            def _(d=d):
                pltpu.make_async_remote_copy(
                    src_ref=small_ref, dst_ref=small_all.at[me], send_sem=sm_send.at[d], recv_sem=sm_recv.at[me],
                    device_id=(d // 4, (d // 2) % 2, d % 2), device_id_type=MESH).start()

        def block_out(k):
            return gwo_ref.at[pl.ds(k * rows_out, rows_out), :]

        for k in range(N_DEV):
            s = k // 2

            @pl.when(k % 2 != c)
            def _(k=k, s=s):
                pltpu.make_async_remote_copy(
                    src_ref=block_out(k), dst_ref=got_out.at[s], send_sem=d2d_send.at[1, s],
                    recv_sem=d2d_recv.at[1, s], device_id=sibling, device_id_type=MESH).start()

            @pl.when(k % 2 == c)
            def _(k=k, s=s):
                pltpu.make_async_copy(block_out(k), own_out.at[s], loc_sems.at[s]).start()

        VIA_X, VIA_Y, X_UP, X_LOW, Y_UP, Y_LOW = range(6)
        x_dev, y_dev = (1 - x, y, c), (x, 1 - y, c)
        ici_place = {VIA_X: (0, 0, x_dev), VIA_Y: (0, 1, y_dev), X_UP: (1, 0, x_dev), X_LOW: (1, 1, x_dev),
                     Y_UP: (2, 0, y_dev), Y_LOW: (2, 1, y_dev)}

        def ici_copy(a, k):
            send, recv = ((send_in, recv_in), (send_out, recv_out))[a]
            half = (d_model, rows_out)[a] // 2
            slot, part, to = ici_place[k]
            rows = pl.ds(part * half, half)
            return pltpu.make_async_remote_copy(
                src_ref=send.at[slot, rows, :], dst_ref=recv.at[slot, rows, :], send_sem=ici_send.at[a, k],
                recv_sem=ici_recv.at[a, k], device_id=to, device_id_type=MESH)

        def by_rows(first, n_rows, fn):
            step = min(row_chunk, n_rows)

            def rows_body(r, _):
                fn(pl.ds(pl.multiple_of(first + r * step, step), step))
                return 0

            lax.fori_loop(0, n_rows // step, rows_body, 0)

        def send_chip_sum(a, j, chip_sum):
            send, recv = ((send_in, recv_in), (send_out, recv_out))[a]
            n_rows = (d_model, rows_out)[a]
            half = n_rows // 2

            def plain(rows):
                send[j, rows, :] = chip_sum(rows).astype(BF16)

            if j == 0:
                by_rows(0, n_rows, plain)
                ici_copy(a, VIA_X).start()
                ici_copy(a, VIA_Y).start()
                return
            free, bound, passed = ((0, X_UP), (1, X_LOW), VIA_Y) if j == 1 else ((1, Y_LOW), (0, Y_UP), VIA_X)
            by_rows(free[0] * half, half, plain)
            ici_copy(a, free[1]).start()
            ici_copy(a, passed).wait_recv()

            def with_passed(rows):
                send[j, rows, :] = (chip_sum(rows) + recv[0, rows, :].astype(F32)).astype(BF16)

            by_rows(bound[0] * half, half, with_passed)
            ici_copy(a, bound[1]).start()

        for s in range(4):
            pltpu.make_async_copy(own_out.at[s], own_out.at[s], loc_sems.at[s]).wait()
            pltpu.make_async_remote_copy(
                src_ref=got_out.at[s], dst_ref=got_out.at[s], send_sem=d2d_send.at[1, s], recv_sem=d2d_recv.at[1, s],
                device_id=sibling, device_id_type=MESH).wait()
        for j, chip in enumerate(chips):
            s = 2 * chip[0] + chip[1]
            send_chip_sum(1, j, lambda rows, s=s: own_out[s, rows, :] + got_out[s, rows, :])

        xt_copy.wait()

        def d2d_copy(slot, pair):
            return pltpu.make_async_remote_copy(
                src_ref=acc.at[slot], dst_ref=got_in.at[pair], send_sem=d2d_send.at[0, pair],
                recv_sem=d2d_recv.at[0, pair], device_id=sibling, device_id_type=MESH)

        def gx_in(tile, buf):
            rows = pl.ds(tile * tm, tm)
            copies = [pltpu.make_async_copy(dr_hbm.at[rows, :], dr_buf.at[buf], dr_sems.at[buf])]
            copies += [pltpu.make_async_copy(dp_groups[k].at[rows, :], dp_buf.at[buf, pl.ds(k * tm, tm), :],
                                             dpx_sems.at[buf, k]) for k in range(N_DEV)]
            return copies

        for step in range(N_DEV):
            slot, pair = step % 2, step // 2
            if step % 2 == 0 and step >= 2:
                d2d_copy(slot, pair - 1).wait_send()

            dp_wait(step)
            if step + 1 < N_DEV:
                dp_start(step + 1)
            else:
                for cp in gx_in(0, (step + 1) % 2):
                    cp.start()
            acc[slot] = _dot(xt_v[...], dp_buf[step % 2])

            if step % 2 == 0:
                d2d_copy(slot, pair).start()
            else:
                d2d_copy(slot, pair).wait_recv()
                if step < N_DEV - 1:
                    send_chip_sum(0, pair, lambda rows, slot=slot, pair=pair: acc[slot, rows, :] + got_in[pair, rows, :])

        def w_copy(k):
            return pltpu.make_async_copy(win_hbm.at[k], xt_v.at[:, pl.ds(k * cw, cw)], w_sems.at[k])

        for k in range(N_DEV):
            w_copy(k).start()

        def gx_out(tile):
            return pltpu.make_async_copy(gx_buf, gx_hbm.at[pl.ds(tile * tm, tm), :], gx_sem)

        for k in range(N_DEV):
            w_copy(k).wait()

        def gx_body(tile, _):
            buf = tile % 2
            for cp in gx_in(tile, buf):
                cp.wait()

            @pl.when(tile + 1 < gx_tiles)
            def _():
                for cp in gx_in(tile + 1, 1 - buf):
                    cp.start()

            val = ALPHA * dr_buf[buf]
            for k in range(N_DEV):
                val = val + _dot(dp_buf[buf, k * tm:(k + 1) * tm, :], xt_v[:, k * cw:(k + 1) * cw], NT)

            @pl.when(tile > 0)
            def _():
                gx_out(tile - 1).wait()

            gx_buf[...] = val
            gx_out(tile).start()
            return 0

        lax.fori_loop(0, gx_tiles, gx_body, 0)

        for d in range(N_DEV):
            @pl.when(d != me)
            def _(d=d):
                pltpu.make_async_remote_copy(
                    src_ref=small_ref, dst_ref=small_all.at[d], send_sem=sm_send.at[d], recv_sem=sm_recv.at[d],
                    device_id=(d // 4, (d // 2) % 2, d % 2), device_id_type=MESH).wait()
        total = small_all[0]
        for d in range(1, N_DEV):
            total = total + small_all[d]
        small_o[...] = total

        mine = 2 * x + y
        last = (N_DEV - 1) % 2

        def finish(a, n_rows, chip_sum, g_o):
            recv = (recv_in, recv_out)[a]
            for k in (X_UP, X_LOW, Y_UP, Y_LOW):
                ici_copy(a, k).wait_recv()
            for k in ici_place:
                ici_copy(a, k).wait_send()

            def total_rows(rows):
                g_o[rows, :] = chip_sum(rows) + recv[1, rows, :].astype(F32) + recv[2, rows, :].astype(F32)

            by_rows(0, n_rows, total_rows)

        finish(1, rows_out, lambda rows: own_out[mine, rows, :] + got_out[mine, rows, :], g_out_o)
        finish(0, d_model, lambda rows: acc[last, rows, :] + got_in[N_DEV // 2 - 1, rows, :], g_in_o)
        d2d_copy(0, N_DEV // 2 - 1).wait_send()
        gx_out(0).wait()

    vmem = pl.BlockSpec(memory_space=pltpu.VMEM)
    hbm = pl.BlockSpec(memory_space=pl.ANY)
    return pl.pallas_call(
        body, name="grad_w_reduce",
        in_specs=[hbm] * 7 + [vmem],
        out_specs=(hbm, vmem, vmem, vmem),
        out_shape=(jax.ShapeDtypeStruct((seq, d_model), F32), jax.ShapeDtypeStruct((d_model, cw), F32),
                   jax.ShapeDtypeStruct((rows_out, d_model), F32), jax.ShapeDtypeStruct(small.shape, F32)),
        scratch_shapes=[
            pltpu.VMEM((d_model, seq), BF16), pltpu.VMEM((2, seq, cw), BF16), pltpu.VMEM((2, d_model, cw), F32),
            pltpu.VMEM((4, d_model, cw), F32), pltpu.VMEM((3, d_model, cw), BF16), pltpu.VMEM((3, d_model, cw), BF16),
            pltpu.VMEM((4, rows_out, d_model), F32), pltpu.VMEM((4, rows_out, d_model), F32),
            pltpu.VMEM((3, rows_out, d_model), BF16), pltpu.VMEM((3, rows_out, d_model), BF16),
            pltpu.VMEM((N_DEV,) + small.shape, F32),
            pltpu.VMEM((2, tm, d_model), F32), pltpu.VMEM((tm, d_model), F32),
            pltpu.SemaphoreType.DMA, pltpu.SemaphoreType.DMA((2,)), pltpu.SemaphoreType.DMA((4,)),
            pltpu.SemaphoreType.DMA((2, 4)), pltpu.SemaphoreType.DMA((2, 4)),
            pltpu.SemaphoreType.DMA((2, 6)), pltpu.SemaphoreType.DMA((2, 6)),
            pltpu.SemaphoreType.DMA((N_DEV,)), pltpu.SemaphoreType.DMA((N_DEV,)),
            pltpu.SemaphoreType.DMA((2,)), pltpu.SemaphoreType.DMA((2, N_DEV)), pltpu.SemaphoreType.DMA((N_DEV,)),
            pltpu.SemaphoreType.DMA,
        ],
        compiler_params=_params(56),
    )(xt, *dp_parts, dr, win_all, gwo, small)


def _adamw_update(g, w, m, v, rows):
    n_rows, width = w.shape
    rows = min(rows, n_rows)

    def body(g_ref, w_ref, m_ref, v_ref, d_o, nm_o, nv_o):
        d_o[...], nm_o[...], nv_o[...] = _adamw(w_ref[...], g_ref[...], m_ref[...], v_ref[...])

    tile = pl.BlockSpec((rows, width), lambda i: (i, 0))
    shape = jax.ShapeDtypeStruct(w.shape, F32)
    return pl.pallas_call(
        body, name="adamw_update", grid=(n_rows // rows,), in_specs=[tile] * 4, out_specs=(tile,) * 3,
        out_shape=(shape,) * 3,
    )(g, w, m, v)


def _small_update(grads, weights, ms, vs):
    n = len(grads)

    def body(*refs):
        g_refs, w_refs, m_refs, v_refs = (refs[i * n:(i + 1) * n] for i in range(4))
        outs = refs[4 * n:]
        for i in range(n):
            delta, nm, nv = _adamw(w_refs[i][...], g_refs[i][...], m_refs[i][...], v_refs[i][...])
            outs[3 * i][...] = delta
            outs[3 * i + 1][...] = nm
            outs[3 * i + 2][...] = nv

    vmem = pl.BlockSpec(memory_space=pltpu.VMEM)
    out_shape = []
    for w in weights:
        out_shape += [jax.ShapeDtypeStruct(w.shape, F32)] * 3
    return pl.pallas_call(
        body, name="small_update", in_specs=[vmem] * (4 * n), out_specs=(vmem,) * (3 * n), out_shape=tuple(out_shape),
    )(*grads, *weights, *ms, *vs)


def _tile_sizes(seq):
    return dict(tm=seq // N_DEV, t_ln=min(512, seq), t_attn=min(128, seq), rc=min(256, seq), pairs=4)


def kernel(x, w_in, conv_w, w_out, ln_gain, ln_bias, loss_target, m_w_in, m_conv_w, m_w_out, m_ln_gain, m_ln_bias,
           v_w_in, v_conv_w, v_w_out, v_ln_gain, v_ln_bias):
    assert x.shape[0] == 1 and w_in.shape[0] == 1, "one sequence per device, depth 1"
    _, seq, d_model = x.shape
    cw = w_in.shape[2]
    conv_k, conv_cols = conv_w.shape[1], conv_w.shape[2]
    rows_out = w_out.shape[1]
    assert cw == d_model // 2 and cw % PAIR == 0 and conv_cols * N_DEV == cw and rows_out * N_DEV == d_model
    ts = _tile_sizes(seq)

    x2 = x.reshape(seq, d_model)
    target = loss_target.reshape(seq, d_model)
    me = 4 * lax.axis_index("x") + 2 * lax.axis_index("y") + lax.axis_index("c")

    conv_pad = jnp.pad(conv_w[0], ((0, SUBLANES - conv_k), (0, PAIR - conv_cols)))
    proj, xt, win_all, conv_all = _gather_proj(x2, w_in[0], conv_pad, ts["tm"])
    conv_full = conv_all[:, :conv_k, :conv_cols].transpose(1, 0, 2).reshape(conv_k, cw)
    conv_full = jnp.pad(conv_full, ((0, SUBLANES - conv_k), (0, 0)))

    mix_conv = _conv_fwd(proj, conv_full, ts["rc"])
    pairs = min(ts["pairs"], cw // PAIR)
    tri = _triangles(ts["t_attn"])
    o, mix_attn, tot, wout_all = _attn_fwd(proj, tri, w_out[0], ts["t_attn"], pairs)
    w_out_full = wout_all.reshape(d_model, d_model)
    dr, d_mix_conv, d_mix_attn, gwo, small = _out_ln(mix_conv, mix_attn, x2, target, ln_gain, ln_bias, w_out_full,
                                                     ts["t_ln"])
    dp_conv, d_taps = _conv_bwd(proj, conv_full, d_mix_conv, ts["rc"])
    dp_qz, dp_kv = _attn_bwd(proj, tri, o, tot, d_mix_attn, ts["t_attn"], pairs)
    small = small.at[ROW_CONV:ROW_CONV + conv_k, :cw].set(d_taps[:conv_k])
    grad_x, g_in, g_out, small_sum = _grad_w_reduce(
        xt, (dp_conv, dp_qz, dp_kv), dr, win_all, gwo, small, rows_out, 128, ts["tm"])
    d_in, nm_in, nv_in = _adamw_update(g_in, w_in[0], m_w_in[0], v_w_in[0], 256)

    loss = small_sum[ROW_LOSS, 0]
    g_gain = small_sum[ROW_GAIN:ROW_GAIN + 1]
    g_bias = small_sum[ROW_BIAS:ROW_BIAS + 1]
    g_conv = lax.dynamic_slice(small_sum, (ROW_CONV, me * conv_cols), (conv_k, conv_cols))
    upd = _small_update((g_out, g_conv, g_gain, g_bias), (w_out[0], conv_w[0], ln_gain, ln_bias),
                        (m_w_out[0], m_conv_w[0], m_ln_gain, m_ln_bias),
                        (v_w_out[0], v_conv_w[0], v_ln_gain, v_ln_bias))
    (d_out, nm_out, nv_out, d_conv, nm_conv, nv_conv, d_gain, nm_gain, nv_gain, d_bias, nm_bias, nv_bias) = upd

    lead = lambda a: a[None]
    return (loss, grad_x.reshape(1, seq, d_model), lead(g_in), lead(g_conv), lead(g_out), g_gain, g_bias,
            lead(d_in), lead(d_conv), lead(d_out), d_gain, d_bias,
            lead(nm_in), lead(nm_conv), lead(nm_out), nm_gain, nm_bias,
            lead(nv_in), lead(nv_conv), lead(nv_out), nv_gain, nv_bias)
```

```python
import jax
import jax.numpy as jnp
from jax import lax
from jax.experimental import pallas as pl
from jax.experimental.pallas import tpu as pltpu

F32 = jnp.float32
BF16 = jnp.bfloat16
MESH = pl.DeviceIdType.MESH

N_DEV = 8
HEAD_DIM = 64
PAIR = 128
SUBLANES = 8
LN_EPS = 1e-5
ALPHA = 2.0 ** 0.25
ADAM_LR, ADAM_B1, ADAM_B2, ADAM_EPS, ADAM_WD, ADAM_STEP = 0.001, 0.9, 0.999, 1e-08, 0.01, 10

ROW_GAIN, ROW_BIAS, ROW_CONV, ROW_LOSS = 0, 1, 2, 5

NT = (((1,), (1,)), ((), ()))
TN = (((0,), (0,)), ((), ()))


V7X_VMEM_BYTES = 64 * 1024 * 1024


def _params(vmem_mib):
    assert vmem_mib * 1024 * 1024 < V7X_VMEM_BYTES
    return pltpu.CompilerParams(vmem_limit_bytes=vmem_mib * 1024 * 1024)


def _dot(a, b, dims=None):
    if dims is None:
        return jnp.dot(a, b, preferred_element_type=F32)
    return lax.dot_general(a, b, dims, preferred_element_type=F32)


def _sigmoid(z):
    return 1.0 / (1.0 + jnp.exp(-z))


def _mesh_pos():
    return lax.axis_index("x"), lax.axis_index("y"), lax.axis_index("c")


def _adamw(w, g, m, v):
    nm = ADAM_B1 * m + (1.0 - ADAM_B1) * g
    nv = ADAM_B2 * v + (1.0 - ADAM_B2) * (g * g)
    m_hat = nm * (1.0 / (1.0 - ADAM_B1 ** ADAM_STEP))
    v_hat = nv * (1.0 / (1.0 - ADAM_B2 ** ADAM_STEP))
    delta = -ADAM_LR * (m_hat / (jnp.sqrt(v_hat) + ADAM_EPS) + ADAM_WD * w)
    return delta, nm, nv


def _gather_proj(x, w_in_s, conv_s, tm):
    seq, d_model = x.shape
    cw = w_in_s.shape[1]
    n_tiles = seq // tm
    half = d_model // 2
    SIB, X_UP, X_LOW, Y_UP, Y_LOW, VIA_Y, VIA_X, ON_X, ON_Y, ON_DIAG = range(10)

    def body(x_hbm, win_ref, conv_ref, proj_hbm, xt_hbm, win_all, conv_all,
             xb, x_stage, o_stage, xt_stage, x_sems, o_sems, xt_sems, w_send, w_recv, send_sems, recv_sems):
        x, y, c = _mesh_pos()
        me = (x, y, c)
        sibling = (x, y, 1 - c)
        x_nbr, y_nbr, diag = (1 - x, y), (x, 1 - y), (1 - x, 1 - y)
        chips = [x_nbr, y_nbr, diag]
        small = (conv_all,)

        def slot(pos):
            return 4 * pos[0] + 2 * pos[1] + pos[2]

        def x_copy(tile, buf):
            return pltpu.make_async_copy(x_hbm.at[pl.ds(tile * tm, tm), :], x_stage.at[buf], x_sems.at[buf])

        x_copy(0, 0).start()
        win_all[slot(me)] = win_ref[...].astype(BF16)
        conv_all[slot(me)] = conv_ref[...]

        def w_copy(k, block, part, to):
            ref = win_all.at[slot(block)]
            if part is not None:
                ref = ref.at[pl.ds(part * half, half), :]
            return pltpu.make_async_remote_copy(
                src_ref=ref, dst_ref=ref, send_sem=w_send.at[k], recv_sem=w_recv.at[k],
                device_id=to, device_id_type=MESH)

        def copy(a, k, block, to):
            ref = small[a].at[slot(block)]
            return pltpu.make_async_remote_copy(
                src_ref=ref, dst_ref=ref, send_sem=send_sems.at[a, k], recv_sem=recv_sems.at[a, k],
                device_id=to, device_id_type=MESH)

        sends = [w_copy(SIB, me, None, sibling),
                 w_copy(X_UP, me, 0, (*x_nbr, c)), w_copy(Y_LOW, me, 1, (*y_nbr, c)),
                 w_copy(X_LOW, me, 1, (*x_nbr, c)), w_copy(Y_UP, me, 0, (*y_nbr, c))]
        for a in range(len(small)):
            sends.append(copy(a, 0, me, sibling))
            sends += [copy(a, 1 + j, me, (*chip, c)) for j, chip in enumerate(chips)]
        for cp in sends:
            cp.start()

        def o_copy(group, tile, buf):
            return pltpu.make_async_copy(o_stage.at[buf], proj_hbm.at[group, pl.ds(tile * tm, tm), :], o_sems.at[buf])

        def xt_copy(tile, buf):
            return pltpu.make_async_copy(xt_stage.at[buf], xt_hbm.at[:, pl.ds(tile * tm, tm)], xt_sems.at[buf])

        def project(order, group, first_pass):
            def tile_body(tile, _):
                if first_pass:
                    buf = tile % 2
                    x_copy(tile, buf).wait()

                    @pl.when(tile + 1 < n_tiles)
                    def _():
                        x_copy(tile + 1, 1 - buf).start()

                    xv = x_stage[buf]
                    xb[tile] = xv.astype(BF16)

                    @pl.when(tile >= 2)
                    def _():
                        xt_copy(tile - 2, buf).wait()

                    xt_stage[buf] = xv.T.astype(BF16)
                    xt_copy(tile, buf).start()
                count = order * n_tiles + tile
                obuf = count % 2

                @pl.when(count >= 2)
                def _():
                    o_copy(group, tile, obuf).wait()

                o_stage[obuf] = _dot(xb[tile], win_all[group]).astype(BF16)
                o_copy(group, tile, obuf).start()
                return 0

            lax.fori_loop(0, n_tiles, tile_body, 0)

        def start(cp):
            cp.start()
            sends.append(cp)

        def small_pass_on(j):
            for a in range(len(small)):
                copy(a, 1 + j, (*chips[j], c), me).wait_recv()
                start(copy(a, 4 + j, (*chips[j], c), sibling))

        def small_from_sibling(k):
            for a in range(len(small)):
                copy(a, k, sibling, me).wait_recv()

        project(0, slot(me), True)
        w_copy(SIB, sibling, None, me).wait_recv()
        small_from_sibling(0)
        project(1, slot(sibling), False)
        w_copy(X_UP, (*x_nbr, c), 0, me).wait_recv()
        start(w_copy(VIA_Y, (*x_nbr, c), 0, (*y_nbr, c)))
        w_copy(Y_LOW, (*y_nbr, c), 1, me).wait_recv()
        start(w_copy(VIA_X, (*y_nbr, c), 1, (*x_nbr, c)))
        w_copy(X_LOW, (*x_nbr, c), 1, me).wait_recv()
        start(w_copy(ON_X, (*x_nbr, c), None, sibling))
        small_pass_on(0)
        project(2, slot((*x_nbr, c)), False)
        w_copy(Y_UP, (*y_nbr, c), 0, me).wait_recv()
        start(w_copy(ON_Y, (*y_nbr, c), None, sibling))
        small_pass_on(1)
        project(3, slot((*y_nbr, c)), False)
        w_copy(ON_X, (*x_nbr, 1 - c), None, me).wait_recv()
        small_from_sibling(4)
        project(4, slot((*x_nbr, 1 - c)), False)
        w_copy(ON_Y, (*y_nbr, 1 - c), None, me).wait_recv()
        small_from_sibling(5)
        project(5, slot((*y_nbr, 1 - c)), False)
        w_copy(VIA_Y, (*diag, c), 0, me).wait_recv()
        w_copy(VIA_X, (*diag, c), 1, me).wait_recv()
        start(w_copy(ON_DIAG, (*diag, c), None, sibling))
        small_pass_on(2)
        project(6, slot((*diag, c)), False)
        w_copy(ON_DIAG, (*diag, 1 - c), None, me).wait_recv()
        small_from_sibling(6)
        project(7, slot((*diag, 1 - c)), False)

        for buf in range(2):
            o_copy(0, 0, buf).wait()
        for buf in range(min(2, n_tiles)):
            xt_copy(0, buf).wait()
        for cp in sends:
            cp.wait_send()

    vmem = pl.BlockSpec(memory_space=pltpu.VMEM)
    hbm = pl.BlockSpec(memory_space=pl.ANY)
    return pl.pallas_call(
        body, name="gather_proj",
        out_shape=(jax.ShapeDtypeStruct((N_DEV, seq, cw), BF16),
                   jax.ShapeDtypeStruct((d_model, seq), BF16),
                   jax.ShapeDtypeStruct((N_DEV, d_model, cw), BF16),
                   jax.ShapeDtypeStruct((N_DEV,) + conv_s.shape, F32)),
        in_specs=[hbm, vmem, vmem], out_specs=(hbm, hbm, vmem, vmem),
        scratch_shapes=[
            pltpu.VMEM((n_tiles, tm, d_model), BF16), pltpu.VMEM((2, tm, d_model), F32),
            pltpu.VMEM((2, tm, cw), BF16), pltpu.VMEM((2, d_model, tm), BF16),
            pltpu.SemaphoreType.DMA((2,)), pltpu.SemaphoreType.DMA((2,)), pltpu.SemaphoreType.DMA((2,)),
            pltpu.SemaphoreType.DMA((10,)), pltpu.SemaphoreType.DMA((10,)),
            pltpu.SemaphoreType.DMA((1, 7)), pltpu.SemaphoreType.DMA((1, 7))],
        compiler_params=_params(48),
    )(x, w_in_s, conv_s)


def _conv_taps(ext, w_ref, rc):
    u0 = ext[SUBLANES:SUBLANES + rc]
    u1 = pltpu.roll(ext, 1, 0)[SUBLANES:SUBLANES + rc]
    u2 = pltpu.roll(ext, 2, 0)[SUBLANES:SUBLANES + rc]
    return w_ref[2:3, :] * u0 + w_ref[1:2, :] * u1 + w_ref[0:1, :] * u2, u0, u1, u2


def _conv_fwd(proj, conv_full, rc):
    _, seq, cw = proj.shape

    def body(b_ref, c_ref, h_ref, z_ref, w_ref, o_ref, u_scr):
        u_scr[0:SUBLANES, :] = jnp.zeros((SUBLANES, PAIR), F32)

        def fill(r, _):
            base = pl.multiple_of(r * rc, rc)
            rows = pl.ds(base, rc)
            u_scr[pl.ds(base + SUBLANES, rc), :] = c_ref[rows, :].astype(F32) * h_ref[rows, :].astype(F32)
            return 0

        lax.fori_loop(0, seq // rc, fill, 0)

        def out(r, _):
            base = pl.multiple_of(r * rc, rc)
            rows = pl.ds(base, rc)
            ext = u_scr[pl.ds(base, rc + SUBLANES), :]
            y, _, _, _ = _conv_taps(ext, w_ref, rc)
            z = z_ref[rows, :].astype(F32)
            o_ref[rows, :] = (z * _sigmoid(z) * b_ref[rows, :].astype(F32) * y).astype(BF16)
            return 0

        lax.fori_loop(0, seq // rc, out, 0)

    def chunk(j):
        return pl.BlockSpec((None, seq, PAIR), lambda cb, j=j: (j, 0, cb))

    return pl.pallas_call(
        body, name="conv_fwd", grid=(cw // PAIR,),
        in_specs=[chunk(0), chunk(1), chunk(2), chunk(3), pl.BlockSpec((SUBLANES, PAIR), lambda cb: (0, cb))],
        out_specs=pl.BlockSpec((seq, PAIR), lambda cb: (0, cb)),
        out_shape=jax.ShapeDtypeStruct((seq, cw), BF16),
        scratch_shapes=[pltpu.VMEM((seq + SUBLANES, PAIR), F32)],
    )(proj, proj, proj, proj, conv_full)


def _conv_bwd(proj, conv_full, d_mix_conv, rc):
    _, seq, cw = proj.shape

    def body(b_ref, c_ref, h_ref, z_ref, w_ref, g_ref, dp_ref, dw_ref, u_scr, dy_scr):
        u_scr[0:SUBLANES, :] = jnp.zeros((SUBLANES, PAIR), F32)
        dy_scr[seq:seq + SUBLANES, :] = jnp.zeros((SUBLANES, PAIR), F32)

        def fill(r, _):
            base = pl.multiple_of(r * rc, rc)
            rows = pl.ds(base, rc)
            u_scr[pl.ds(base + SUBLANES, rc), :] = c_ref[rows, :].astype(F32) * h_ref[rows, :].astype(F32)
            return 0

        lax.fori_loop(0, seq // rc, fill, 0)

        def gate(r, acc):
            base = pl.multiple_of(r * rc, rc)
            rows = pl.ds(base, rc)
            ext = u_scr[pl.ds(base, rc + SUBLANES), :]
            y, u0, u1, u2 = _conv_taps(ext, w_ref, rc)
            z = z_ref[rows, :].astype(F32)
            b = b_ref[rows, :].astype(F32)
            g = g_ref[rows, :].astype(F32)
            sig = _sigmoid(z)
            dp_ref[3, rows, :] = (g * b * y * (sig * (1.0 + z * (1.0 - sig)))).astype(BF16)
            gs = g * (z * sig)
            dp_ref[0, rows, :] = (gs * y).astype(BF16)
            dy = gs * b
            dy_scr[rows, :] = dy
            a0, a1, a2 = acc
            return (a0 + jnp.sum(dy * u2, axis=0, keepdims=True),
                    a1 + jnp.sum(dy * u1, axis=0, keepdims=True),
                    a2 + jnp.sum(dy * u0, axis=0, keepdims=True))

        zero = jnp.zeros((1, PAIR), F32)
        a0, a1, a2 = lax.fori_loop(0, seq // rc, gate, (zero, zero, zero))
        dw_ref[...] = jnp.zeros((SUBLANES, PAIR), F32)
        dw_ref[0:1, :] = a0
        dw_ref[1:2, :] = a1
        dw_ref[2:3, :] = a2

        def back(r, _):
            base = pl.multiple_of(r * rc, rc)
            rows = pl.ds(base, rc)
            ext = dy_scr[pl.ds(base, rc + SUBLANES), :]
            n = rc + SUBLANES
            d0 = ext[0:rc]
            d1 = pltpu.roll(ext, n - 1, 0)[0:rc]
            d2 = pltpu.roll(ext, n - 2, 0)[0:rc]
            du = w_ref[2:3, :] * d0 + w_ref[1:2, :] * d1 + w_ref[0:1, :] * d2
            dp_ref[1, rows, :] = (du * h_ref[rows, :].astype(F32)).astype(BF16)
            dp_ref[2, rows, :] = (du * c_ref[rows, :].astype(F32)).astype(BF16)
            return 0

        lax.fori_loop(0, seq // rc, back, 0)

    def chunk(j):
        return pl.BlockSpec((None, seq, PAIR), lambda cb, j=j: (j, 0, cb))

    return pl.pallas_call(
        body, name="conv_bwd", grid=(cw // PAIR,),
        in_specs=[chunk(0), chunk(1), chunk(2), chunk(3), pl.BlockSpec((SUBLANES, PAIR), lambda cb: (0, cb)),
                  pl.BlockSpec((seq, PAIR), lambda cb: (0, cb))],
        out_specs=(pl.BlockSpec((4, seq, PAIR), lambda cb: (0, 0, cb)),
                   pl.BlockSpec((SUBLANES, PAIR), lambda cb: (0, cb))),
        out_shape=(jax.ShapeDtypeStruct((4, seq, cw), BF16), jax.ShapeDtypeStruct((SUBLANES, cw), F32)),
        scratch_shapes=[pltpu.VMEM((seq + SUBLANES, PAIR), F32), pltpu.VMEM((seq + SUBLANES, PAIR), F32)],
    )(proj, proj, proj, proj, conv_full, d_mix_conv)


SKIP_CARRY = 104.0
LOG2_E = 1.4426950408889634
LANE_TOT0, LANE_TOT1, LANE_FIRST, LANE_WHOLE = 0, 1, 2, 3
FAST_BLOCKS = 3
EARLY_ROWS = 32


def _triangles(t):
    row = lax.broadcasted_iota(jnp.int32, (2 * t, 2 * t), 0)
    col = lax.broadcasted_iota(jnp.int32, (2 * t, 2 * t), 1)
    same = (row < t) == (col < t)
    upper = jnp.logical_and(same, row > col).astype(BF16)
    lower = jnp.logical_and(same, row < col).astype(BF16)
    return jnp.stack([jnp.concatenate([upper, upper], axis=0), jnp.concatenate([lower, lower], axis=0)])


def _pair_masks(t):
    lane = lax.broadcasted_iota(jnp.int32, (t, PAIR), 1)
    qrow = lax.broadcasted_iota(jnp.int32, (t, 2 * t), 0)
    kcol = lax.broadcasted_iota(jnp.int32, (t, 2 * t), 1)
    strict = jnp.where(kcol < t, kcol, kcol - t) < qrow
    return lane, lane < HEAD_DIM, strict


def _by_head(x, head0):
    zero = jnp.zeros_like(x)
    return jnp.concatenate([jnp.where(head0, x, zero), jnp.where(head0, zero, x)], axis=0)


def _hi_lo(a):
    hi = a.astype(BF16)
    lo = (a - hi.astype(F32)).astype(BF16)
    return jnp.concatenate([hi, lo], axis=1)


def _softplus_parts(z, strict, masked):
    spu = jnp.maximum(z, 0.0) + jnp.log(1.0 + jnp.exp2(jnp.abs(z) * -LOG2_E))
    return z - spu, (jnp.where(strict, spu, 0.0) if masked else spu)


def _stacked_dot(parts, rhs):
    out = _dot(jnp.concatenate(parts, axis=0), rhs)
    ends = [0]
    for p in parts:
        ends.append(ends[-1] + p.shape[0])
    return [out[a:b] for a, b in zip(ends[:-1], ends[1:])]


def _splice(whole, rows, part):
    pieces = ([whole[:rows[0]]] if rows[0] > 0 else []) + [part]
    if rows[1] < whole.shape[0]:
        pieces.append(whole[rows[1]:])
    return part if len(pieces) == 1 else jnp.concatenate(pieces, axis=0)


def _attn_fwd(proj, tri, w_out_s, t, pp):
    _, seq, cw = proj.shape
    scale = HEAD_DIM ** -0.5
    width = pp * PAIR

    n_steps = (cw // width) * (seq // t)
    pass_on_step = min(8, n_steps - 1)

    def body(q_ref, k_ref, v_ref, za_ref, tri_ref, wout_ref, o_ref, mix_ref, tot_ref, wout_all, acc_ref,
             own_w, own_sem, send_sems, recv_sems):
        i = pl.program_id(1)
        step_no = pl.program_id(0) * (seq // t) + i
        mx, my, mc = _mesh_pos()
        me = (mx, my, mc)
        sibling = (mx, my, 1 - mc)
        chips = [(1 - mx, my), (mx, 1 - my), (1 - mx, 1 - my)]

        def place(block):
            return wout_all.at[4 * block[0] + 2 * block[1] + block[2]]

        def w_copy(k, block, to, src=None):
            return pltpu.make_async_remote_copy(
                src_ref=place(block) if src is None else src, dst_ref=place(block), send_sem=send_sems.at[k],
                recv_sem=recv_sems.at[k], device_id=to, device_id_type=MESH)

        own_copy = pltpu.make_async_copy(own_w, place(me), own_sem)

        @pl.when(step_no == 0)
        def _():
            own_w[...] = wout_ref[...].astype(BF16)
            own_copy.start()
            w_copy(0, me, sibling, own_w).start()
            for j, chip in enumerate(chips):
                w_copy(1 + j, me, (*chip, mc), own_w).start()

        @pl.when(step_no == pass_on_step)
        def _():
            for j, chip in enumerate(chips):
                w_copy(1 + j, (*chip, mc), me).wait_recv()
                w_copy(4 + j, (*chip, mc), sibling).start()

        @pl.when(step_no == n_steps - 1)
        def _():
            own_copy.wait()
            w_copy(0, sibling, me).wait_recv()
            for j, chip in enumerate(chips):
                w_copy(4 + j, (*chip, 1 - mc), me).wait_recv()
            for k in range(7):
                w_copy(k, me, sibling, own_w).wait_send()

        lane, head0, strict = _pair_masks(t)
        upper = tri_ref[0]
        q = q_ref[...] * scale
        acc_ref[...] = jnp.zeros_like(acc_ref)

        def sweep(blocks, state):
            staged = []
            for j, masked, rows in blocks:
                start = pl.multiple_of(j * t, t)
                kb = k_ref[pl.ds(start, t), :]
                vb = v_ref[pl.ds(start, t), :]
                for p in range(pp):
                    cols = slice(p * PAIR, (p + 1) * PAIR)
                    z = _dot(q[rows[0]:rows[1], cols], _by_head(kb[:, cols], head0), NT)
                    ls, sp = _softplus_parts(z, strict[rows[0]:rows[1]], masked)
                    staged.append((p, masked, rows, ls, sp, _by_head(vb[:, cols], head0)))
            afters = _stacked_dot([_hi_lo(sp) for _, _, _, _, sp, _ in staged], upper)
            state = list(state)
            for (p, masked, rows, ls, sp, v2), after in zip(staged, afters):
                c0, c1 = state[p]
                part = slice(rows[0], rows[1])
                x = ls - after
                w = jnp.exp(jnp.concatenate([x[:, :t] - c0[part], x[:, t:] - c1[part]], axis=1))
                if masked:
                    w = jnp.where(strict[part], w, 0.0)
                acc_ref[part, p * PAIR:(p + 1) * PAIR] += _dot(w.astype(BF16), v2)
                state[p] = (_splice(c0, rows, c0[part] + (after[:, 0:1] + sp[:, 0:1])),
                            _splice(c1, rows, c1[part] + (after[:, t:t + 1] + sp[:, t:t + 1])))
            return tuple(state)

        def unfinished(state, rows):
            m = state[0][0]
            for p in range(pp):
                m = jnp.minimum(m, jnp.minimum(state[p][0], state[p][1]))
            return jnp.min(m[rows[0]:rows[1]]) < SKIP_CARRY

        every = (0, t)
        early, late = (0, min(EARLY_ROWS, t)), (min(EARLY_ROWS, t), t)

        def step(js):
            state = sweep(((js[0], False, every),), js[1])
            return js[0] - 1, state, unfinished(state, every)

        def fast():
            blocks = tuple((i - b, b == 0, every) for b in range(FAST_BLOCKS - 1))
            state = sweep(blocks + ((i - (FAST_BLOCKS - 1), False, early),), init)
            if late[0] == late[1]:
                return state, jnp.bool_(True)
            whole = unfinished(state, late)
            state = lax.cond(whole, lambda: sweep(((i - (FAST_BLOCKS - 1), False, late),), state), lambda: state)
            return state, whole

        zcol = jnp.zeros((t, 1), F32)
        init = tuple((zcol, zcol) for _ in range(pp))
        many = i >= FAST_BLOCKS - 1
        state, whole = lax.cond(many, fast, lambda: (sweep(((i, True, every),), init), jnp.bool_(True)))
        j_end, state, _ = lax.while_loop(
            lambda js: jnp.logical_and(js[0] >= 0, js[2]), step,
            (jnp.where(many, i - FAST_BLOCKS, i - 1), state, unfinished(state, every)))
        first = (j_end + 1).astype(F32)
        notes = jnp.where(lane == LANE_FIRST, first, whole.astype(F32))
        za = za_ref[...].astype(F32)
        for p in range(pp):
            c0, c1 = state[p]
            cols = slice(p * PAIR, (p + 1) * PAIR)
            zp = za[:, cols]
            acc = acc_ref[:, cols]
            o_ref[:, cols] = acc.astype(BF16)
            mix_ref[:, cols] = (zp * _sigmoid(zp) * acc).astype(BF16)
            tot_ref[:, cols] = jnp.where(lane == LANE_TOT0, c0, jnp.where(lane == LANE_TOT1, c1, notes))

    def tile(j):
        return pl.BlockSpec((None, t, width), lambda g, i, j=j: (j, i, g))

    def full(j):
        return pl.BlockSpec((None, seq, width), lambda g, i, j=j: (j, 0, g))

    out_tile = pl.BlockSpec((t, width), lambda g, i: (i, g))
    return pl.pallas_call(
        body, name="attn_fwd", grid=(cw // width, seq // t),
        in_specs=[tile(4), full(5), full(6), tile(7), pl.BlockSpec(tri.shape, lambda g, i: (0, 0, 0)),
                  pl.BlockSpec(memory_space=pltpu.VMEM)],
        out_specs=(out_tile, out_tile, out_tile, pl.BlockSpec(memory_space=pl.ANY)),
        out_shape=(jax.ShapeDtypeStruct((seq, cw), BF16), jax.ShapeDtypeStruct((seq, cw), BF16),
                   jax.ShapeDtypeStruct((seq, cw), F32),
                   jax.ShapeDtypeStruct((N_DEV,) + w_out_s.shape, BF16)),
        scratch_shapes=[pltpu.VMEM((t, width), F32), pltpu.VMEM(w_out_s.shape, BF16), pltpu.SemaphoreType.DMA,
                        pltpu.SemaphoreType.DMA((7,)), pltpu.SemaphoreType.DMA((7,))],
    )(proj, proj, proj, proj, tri, w_out_s)


def _attn_bwd(proj, tri, o, tot, d_mix_attn, t, pp):
    _, seq, cw = proj.shape
    nb = seq // t
    scale = HEAD_DIM ** -0.5
    width = pp * PAIR

    def body(q_ref, k_ref, v_ref, za_ref, tri_ref, o_ref, tot_ref, g_ref, dqz_ref, dkv_ref, dk_acc, dv_acc, dq_acc):
        i = pl.program_id(1)
        dq_acc[...] = jnp.zeros_like(dq_acc)

        @pl.when(i == 0)
        def _():
            dk_acc[...] = jnp.zeros_like(dk_acc)
            dv_acc[...] = jnp.zeros_like(dv_acc)

        _, head0, strict = _pair_masks(t)
        upper, lower = tri_ref[0], tri_ref[1, 0:2 * t, :]
        za = za_ref[...].astype(F32)
        g = g_ref[...].astype(F32)
        sig = _sigmoid(za)
        dqz_ref[1] = (g * o_ref[...].astype(F32) * (sig * (1.0 + za * (1.0 - sig)))).astype(BF16)
        do = (g * (za * sig)).astype(BF16)
        q = q_ref[...] * scale
        tot_v = tot_ref[...]
        q2, do2, init = [], [], []
        zcol = jnp.zeros((t, 1), F32)
        for p in range(pp):
            cols = slice(p * PAIR, (p + 1) * PAIR)
            q2.append(_by_head(q[:, cols], head0))
            do2.append(_by_head(do[:, cols], head0))
            tp = tot_v[:, cols]
            init.append(((tp[:, LANE_TOT0:LANE_TOT0 + 1], tp[:, LANE_TOT1:LANE_TOT1 + 1]), (zcol, zcol)))
        first = jnp.clip(tot_v[0:1, LANE_FIRST:LANE_FIRST + 1], 0.0, i.astype(F32)).astype(jnp.int32)[0, 0]

        def sweep(blocks, state):
            staged = []
            for j, masked, rows in blocks:
                start = pl.multiple_of(j * t, t)
                kb = k_ref[pl.ds(start, t), :]
                vb = v_ref[pl.ds(start, t), :]
                part = slice(rows[0], rows[1])
                for p in range(pp):
                    cols = slice(p * PAIR, (p + 1) * PAIR)
                    k2 = _by_head(kb[:, cols], head0)
                    z = _dot(q[part, cols], k2, NT)
                    ls, sp = _softplus_parts(z, strict[part], masked)
                    da = _dot(do[part, cols], _by_head(vb[:, cols], head0), NT)
                    staged.append((p, masked, rows, k2, ls, sp, da))
            afters = _stacked_dot([_hi_lo(sp) for _, _, _, _, _, sp, _ in staged], upper)
            state = list(state)
            weights, ggs = [], []
            for (p, masked, rows, k2, ls, sp, da), after in zip(staged, afters):
                (s0, s1), befores = state[p]
                part = slice(rows[0], rows[1])
                n0 = s0[part] - (after[:, 0:1] + sp[:, 0:1])
                n1 = s1[part] - (after[:, t:t + 1] + sp[:, t:t + 1])
                x = ls - after
                a = jnp.exp(jnp.concatenate([x[:, :t] - n0, x[:, t:] - n1], axis=1))
                if masked:
                    a = jnp.where(strict[part], a, 0.0)
                state[p] = ((_splice(s0, rows, n0), _splice(s1, rows, n1)), befores)
                weights.append(a.astype(BF16))
                ggs.append(a * da)
            pres = _stacked_dot([gg.astype(BF16) for gg in ggs], lower)
            dzs = []
            for (p, masked, rows, k2, ls, sp, da), gg, pre in zip(staged, ggs, pres):
                rests, (b0, b1) = state[p]
                part = slice(rows[0], rows[1])
                y = gg + pre
                dz = gg - jnp.exp(ls) * jnp.concatenate([y[:, :t] + b0[part], y[:, t:] + b1[part]], axis=1)
                if masked:
                    dz = jnp.where(strict[part], dz, 0.0)
                dzb = dz.astype(BF16)
                dzs.append(dzb)
                dq_acc[part, p * PAIR:(p + 1) * PAIR] += _dot(dzb, k2)
                state[p] = (rests, (_splice(b0, rows, b0[part] + y[:, t - 1:t]),
                                    _splice(b1, rows, b1[part] + y[:, 2 * t - 1:2 * t])))
            first_row = pl.multiple_of(blocks[0][0] * t, t)
            n_rows = len(blocks) * t
            for p in range(pp):
                cols = slice(p * PAIR, (p + 1) * PAIR)

                def by_key(tiles):
                    out = []
                    for n, m in tiles:
                        rows = staged[n][2]
                        m = _splice(jnp.zeros((t, 2 * t), BF16), rows, m)
                        out.append(jnp.concatenate([m[:, :t], m[:, t:]], axis=0).T)
                    return jnp.concatenate(out, axis=0)

                mine = [n for n in range(len(staged)) if staged[n][0] == p]
                dk_acc[pl.ds(first_row, n_rows), cols] += _dot(by_key([(n, dzs[n]) for n in mine]), q2[p])
                dv_acc[pl.ds(first_row, n_rows), cols] += _dot(by_key([(n, weights[n]) for n in mine]), do2[p])
            return tuple(state)

        every = (0, t)
        early = (0, min(EARLY_ROWS, t))
        many = i >= FAST_BLOCKS - 1
        last_single = jnp.where(many, i - (FAST_BLOCKS - 1), i)
        state = lax.fori_loop(first, last_single, lambda j, s: sweep(((j, False, every),), s), tuple(init))

        def fast(rows):
            blocks = tuple((i - b, b == 0, every) for b in range(FAST_BLOCKS - 2, -1, -1))
            return sweep(((i - (FAST_BLOCKS - 1), False, rows),) + blocks, state)

        whole = tot_v[0:1, LANE_WHOLE:LANE_WHOLE + 1].astype(jnp.int32)[0, 0] > 0
        state = lax.cond(
            many,
            lambda: lax.cond(whole, lambda: fast(every), lambda: fast(early)),
            lambda: sweep(((i, True, every),), state))
        dqz_ref[0] = (dq_acc[...] * scale).astype(BF16)

        @pl.when(i == nb - 1)
        def _():
            dkv_ref[0] = dk_acc[...].astype(BF16)
            dkv_ref[1] = dv_acc[...].astype(BF16)

    def tile(j):
        return pl.BlockSpec((None, t, width), lambda g, i, j=j: (j, i, g))

    def full(j):
        return pl.BlockSpec((None, seq, width), lambda g, i, j=j: (j, 0, g))

    flat_tile = pl.BlockSpec((t, width), lambda g, i: (i, g))
    return pl.pallas_call(
        body, name="attn_bwd", grid=(cw // width, nb),
        in_specs=[tile(4), full(5), full(6), tile(7), pl.BlockSpec(tri.shape, lambda g, i: (0, 0, 0)),
                  flat_tile, flat_tile, flat_tile],
        out_specs=(pl.BlockSpec((2, t, width), lambda g, i: (0, i, g)),
                   pl.BlockSpec((2, seq, width), lambda g, i: (0, 0, g))),
        out_shape=(jax.ShapeDtypeStruct((2, seq, cw), BF16), jax.ShapeDtypeStruct((2, seq, cw), BF16)),
        scratch_shapes=[pltpu.VMEM((seq, width), F32), pltpu.VMEM((seq, width), F32), pltpu.VMEM((t, width), F32)],
        compiler_params=_params(48),
    )(proj, proj, proj, proj, tri, o, tot, d_mix_attn)


def _out_ln(mix_conv, mix_attn, x, target, gain, bias, w_out, tm):
    seq, d_model = x.shape
    cw = mix_conv.shape[1]
    inv_d = 1.0 / d_model

    def body(mc_ref, ma_ref, x_ref, t_ref, gain_ref, bias_ref, w_ref, dr_ref, dmc_ref, dma_ref, gwo_ref, small_ref):
        @pl.when(pl.program_id(0) == 0)
        def _():
            gwo_ref[...] = jnp.zeros_like(gwo_ref)
            small_ref[...] = jnp.zeros_like(small_ref)

        mix = jnp.concatenate([mc_ref[...], ma_ref[...]], axis=1)
        w = w_ref[...]
        r = ALPHA * x_ref[...] + _dot(mix, w)
        mu = jnp.sum(r, axis=1, keepdims=True) * inv_d
        xc = r - mu
        var = jnp.sum(xc * xc, axis=1, keepdims=True) * inv_d
        rstd = lax.rsqrt(var + LN_EPS)
        xhat = xc * rstd
        gain_v = gain_ref[...]
        err = xhat * gain_v + bias_ref[...] - t_ref[...]
        row_loss = jnp.sum(err * err, axis=1, keepdims=True)
        loss = (0.5 * inv_d) * jnp.sum(row_loss, axis=0, keepdims=True)
        dy = err * inv_d
        small_ref[ROW_GAIN:ROW_GAIN + 1, :] += jnp.sum(dy * xhat, axis=0, keepdims=True)
        small_ref[ROW_BIAS:ROW_BIAS + 1, :] += jnp.sum(dy, axis=0, keepdims=True)
        small_ref[ROW_LOSS:ROW_LOSS + 1, :] += jnp.broadcast_to(loss, (1, d_model))
        dxhat = dy * gain_v
        m1 = jnp.sum(dxhat, axis=1, keepdims=True) * inv_d
        m2 = jnp.sum(dxhat * xhat, axis=1, keepdims=True) * inv_d
        dr = rstd * (dxhat - m1 - xhat * m2)
        dr_ref[...] = dr
        drb = dr.astype(BF16)
        dmix = _dot(drb, w, NT)
        dmc_ref[...] = dmix[:, :cw].astype(BF16)
        dma_ref[...] = dmix[:, cw:].astype(BF16)
        gwo_ref[...] += _dot(mix, drb, TN)

    def rows(width):
        return pl.BlockSpec((tm, width), lambda i: (i, 0))

    def whole(shape):
        return pl.BlockSpec(shape, lambda i: (0, 0))

    return pl.pallas_call(
        body, name="out_ln", grid=(seq // tm,),
        in_specs=[rows(cw), rows(cw), rows(d_model), rows(d_model), whole((1, d_model)), whole((1, d_model)),
                  whole((d_model, d_model))],
        out_specs=(rows(d_model), rows(cw), rows(cw), whole((d_model, d_model)), whole((SUBLANES, d_model))),
        out_shape=(jax.ShapeDtypeStruct((seq, d_model), F32), jax.ShapeDtypeStruct((seq, cw), BF16),
                   jax.ShapeDtypeStruct((seq, cw), BF16), jax.ShapeDtypeStruct((d_model, d_model), F32),
                   jax.ShapeDtypeStruct((SUBLANES, d_model), F32)),
        compiler_params=_params(48),
    )(mix_conv, mix_attn, x, target, gain, bias, w_out)


_DP_OF_GROUP = ((0, 0), (0, 1), (0, 2), (0, 3), (1, 0), (2, 0), (2, 1), (1, 1))


def _grad_w_reduce(xt, dp_parts, dr, win_all, gwo, small, rows_out, row_chunk, tm):
    d_model, seq = xt.shape
    nch, _, cw = win_all.shape
    gx_tiles = seq // tm
    assert nch * tm == seq and nch * cw == seq, "needs S == 8 * tm == 8 * CW"

    def body(xt_hbm, dpa, dpb, dpc, dr_hbm, win_hbm, gwo_ref, small_ref,
             gx_hbm, g_in_o, g_out_o, small_o,
             xt_v, dp_buf, acc, got_in, send_in, recv_in, own_out, got_out, send_out, recv_out, small_all,
             dr_buf, gx_buf,
             xt_sem, dp_sems, loc_sems, d2d_send, d2d_recv, ici_send, ici_recv, sm_send, sm_recv,
             dr_sems, dpx_sems, w_sems, gx_sem):
        x, y, c = _mesh_pos()
        me = 4 * x + 2 * y + c
        sibling = (x, y, 1 - c)
        chips = [(1 - x, 1 - y), (1 - x, y), (x, 1 - y)]
        owners = [(*chip, cc) for chip in chips for cc in (1 - c, c)] + [sibling, (x, y, c)]
        group_of = [4 * o[0] + 2 * o[1] + o[2] for o in owners]
        dp_parts_ = (dpa, dpb, dpc)
        dp_groups = [dp_parts_[arr].at[idx] for arr, idx in _DP_OF_GROUP]

        xt_copy = pltpu.make_async_copy(xt_hbm, xt_v, xt_sem)
        xt_copy.start()

        def dp_start(step):
            for k in range(N_DEV):
                @pl.when(group_of[step] == k)
                def _(k=k):
                    pltpu.make_async_copy(dp_groups[k], dp_buf.at[step % 2], dp_sems.at[step % 2]).start()

        def dp_wait(step):
            pltpu.make_async_copy(dp_groups[0], dp_buf.at[step % 2], dp_sems.at[step % 2]).wait()

        dp_start(0)

        small_all[me] = small_ref[...]
        for d in range(N_DEV):
            @pl.when(d != me)
            def _(d=d):
                pltpu.make_async_remote_copy(
                    src_ref=small_ref, dst_ref=small_all.at[me], send_sem=sm_send.at[d], recv_sem=sm_recv.at[me],
                    device_id=(d // 4, (d // 2) % 2, d % 2), device_id_type=MESH).start()

        def block_out(k):
            return gwo_ref.at[pl.ds(k * rows_out, rows_out), :]

        for k in range(N_DEV):
            s = k // 2

            @pl.when(k % 2 != c)
            def _(k=k, s=s):
                pltpu.make_async_remote_copy(
                    src_ref=block_out(k), dst_ref=got_out.at[s], send_sem=d2d_send.at[1, s],
                    recv_sem=d2d_recv.at[1, s], device_id=sibling, device_id_type=MESH).start()

            @pl.when(k % 2 == c)
            def _(k=k, s=s):
                pltpu.make_async_copy(block_out(k), own_out.at[s], loc_sems.at[s]).start()

        VIA_X, VIA_Y, X_UP, X_LOW, Y_UP, Y_LOW = range(6)
        x_dev, y_dev = (1 - x, y, c), (x, 1 - y, c)
        ici_place = {VIA_X: (0, 0, x_dev), VIA_Y: (0, 1, y_dev), X_UP: (1, 0, x_dev), X_LOW: (1, 1, x_dev),
                     Y_UP: (2, 0, y_dev), Y_LOW: (2, 1, y_dev)}

        def ici_copy(a, k):
            send, recv = ((send_in, recv_in), (send_out, recv_out))[a]
            half = (d_model, rows_out)[a] // 2
            slot, part, to = ici_place[k]
            rows = pl.ds(part * half, half)
            return pltpu.make_async_remote_copy(
                src_ref=send.at[slot, rows, :], dst_ref=recv.at[slot, rows, :], send_sem=ici_send.at[a, k],
                recv_sem=ici_recv.at[a, k], device_id=to, device_id_type=MESH)

        def by_rows(first, n_rows, fn):
            step = min(row_chunk, n_rows)

            def rows_body(r, _):
                fn(pl.ds(pl.multiple_of(first + r * step, step), step))
                return 0

            lax.fori_loop(0, n_rows // step, rows_body, 0)

        def send_chip_sum(a, j, chip_sum):
            send, recv = ((send_in, recv_in), (send_out, recv_out))[a]
            n_rows = (d_model, rows_out)[a]
            half = n_rows // 2

            def plain(rows):
                send[j, rows, :] = chip_sum(rows).astype(BF16)

            if j == 0:
                by_rows(0, n_rows, plain)
                ici_copy(a, VIA_X).start()
                ici_copy(a, VIA_Y).start()
                return
            free, bound, passed = ((0, X_UP), (1, X_LOW), VIA_Y) if j == 1 else ((1, Y_LOW), (0, Y_UP), VIA_X)
            by_rows(free[0] * half, half, plain)
            ici_copy(a, free[1]).start()
            ici_copy(a, passed).wait_recv()

            def with_passed(rows):
                send[j, rows, :] = (chip_sum(rows) + recv[0, rows, :].astype(F32)).astype(BF16)

            by_rows(bound[0] * half, half, with_passed)
            ici_copy(a, bound[1]).start()

        for s in range(4):
            pltpu.make_async_copy(own_out.at[s], own_out.at[s], loc_sems.at[s]).wait()
            pltpu.make_async_remote_copy(
                src_ref=got_out.at[s], dst_ref=got_out.at[s], send_sem=d2d_send.at[1, s], recv_sem=d2d_recv.at[1, s],
                device_id=sibling, device_id_type=MESH).wait()
        for j, chip in enumerate(chips):
            s = 2 * chip[0] + chip[1]
            send_chip_sum(1, j, lambda rows, s=s: own_out[s, rows, :] + got_out[s, rows, :])

        xt_copy.wait()

        def d2d_copy(slot, pair):
            return pltpu.make_async_remote_copy(
                src_ref=acc.at[slot], dst_ref=got_in.at[pair], send_sem=d2d_send.at[0, pair],
                recv_sem=d2d_recv.at[0, pair], device_id=sibling, device_id_type=MESH)

        def gx_in(tile, buf):
            rows = pl.ds(tile * tm, tm)
            copies = [pltpu.make_async_copy(dr_hbm.at[rows, :], dr_buf.at[buf], dr_sems.at[buf])]
            copies += [pltpu.make_async_copy(dp_groups[k].at[rows, :], dp_buf.at[buf, pl.ds(k * tm, tm), :],
                                             dpx_sems.at[buf, k]) for k in range(N_DEV)]
            return copies

        for step in range(N_DEV):
            slot, pair = step % 2, step // 2
            if step % 2 == 0 and step >= 2:
                d2d_copy(slot, pair - 1).wait_send()

            dp_wait(step)
            if step + 1 < N_DEV:
                dp_start(step + 1)
            else:
                for cp in gx_in(0, (step + 1) % 2):
                    cp.start()
            acc[slot] = _dot(xt_v[...], dp_buf[step % 2])

            if step % 2 == 0:
                d2d_copy(slot, pair).start()
            else:
                d2d_copy(slot, pair).wait_recv()
                if step < N_DEV - 1:
                    send_chip_sum(0, pair, lambda rows, slot=slot, pair=pair: acc[slot, rows, :] + got_in[pair, rows, :])

        def w_copy(k):
            return pltpu.make_async_copy(win_hbm.at[k], xt_v.at[:, pl.ds(k * cw, cw)], w_sems.at[k])

        for k in range(N_DEV):
            w_copy(k).start()

        def gx_out(tile):
            return pltpu.make_async_copy(gx_buf, gx_hbm.at[pl.ds(tile * tm, tm), :], gx_sem)

        for k in range(N_DEV):
            w_copy(k).wait()

        def gx_body(tile, _):
            buf = tile % 2
            for cp in gx_in(tile, buf):
                cp.wait()

            @pl.when(tile + 1 < gx_tiles)
            def _():
                for cp in gx_in(tile + 1, 1 - buf):
                    cp.start()

            val = ALPHA * dr_buf[buf]
            for k in range(N_DEV):
                val = val + _dot(dp_buf[buf, k * tm:(k + 1) * tm, :], xt_v[:, k * cw:(k + 1) * cw], NT)

            @pl.when(tile > 0)
            def _():
                gx_out(tile - 1).wait()

            gx_buf[...] = val
            gx_out(tile).start()
            return 0

        lax.fori_loop(0, gx_tiles, gx_body, 0)

        for d in range(N_DEV):
            @pl.when(d != me)
            def _(d=d):
                pltpu.make_async_remote_copy(
                    src_ref=small_ref, dst_ref=small_all.at[d], send_sem=sm_send.at[d], recv_sem=sm_recv.at[d],
                    device_id=(d // 4, (d // 2) % 2, d % 2), device_id_type=MESH).wait()
        total = small_all[0]
        for d in range(1, N_DEV):
            total = total + small_all[d]
        small_o[...] = total

        mine = 2 * x + y
        last = (N_DEV - 1) % 2

        def finish(a, n_rows, chip_sum, g_o):
            recv = (recv_in, recv_out)[a]
            for k in (X_UP, X_LOW, Y_UP, Y_LOW):
                ici_copy(a, k).wait_recv()
            for k in ici_place:
                ici_copy(a, k).wait_send()

            def total_rows(rows):
                g_o[rows, :] = chip_sum(rows) + recv[1, rows, :].astype(F32) + recv[2, rows, :].astype(F32)

            by_rows(0, n_rows, total_rows)

        finish(1, rows_out, lambda rows: own_out[mine, rows, :] + got_out[mine, rows, :], g_out_o)
        finish(0, d_model, lambda rows: acc[last, rows, :] + got_in[N_DEV // 2 - 1, rows, :], g_in_o)
        d2d_copy(0, N_DEV // 2 - 1).wait_send()
        gx_out(0).wait()

    vmem = pl.BlockSpec(memory_space=pltpu.VMEM)
    hbm = pl.BlockSpec(memory_space=pl.ANY)
    return pl.pallas_call(
        body, name="grad_w_reduce",
        in_specs=[hbm] * 7 + [vmem],
        out_specs=(hbm, vmem, vmem, vmem),
        out_shape=(jax.ShapeDtypeStruct((seq, d_model), F32), jax.ShapeDtypeStruct((d_model, cw), F32),
                   jax.ShapeDtypeStruct((rows_out, d_model), F32), jax.ShapeDtypeStruct(small.shape, F32)),
        scratch_shapes=[
            pltpu.VMEM((d_model, seq), BF16), pltpu.VMEM((2, seq, cw), BF16), pltpu.VMEM((2, d_model, cw), F32),
            pltpu.VMEM((4, d_model, cw), F32), pltpu.VMEM((3, d_model, cw), BF16), pltpu.VMEM((3, d_model, cw), BF16),
            pltpu.VMEM((4, rows_out, d_model), F32), pltpu.VMEM((4, rows_out, d_model), F32),
            pltpu.VMEM((3, rows_out, d_model), BF16), pltpu.VMEM((3, rows_out, d_model), BF16),
            pltpu.VMEM((N_DEV,) + small.shape, F32),
            pltpu.VMEM((2, tm, d_model), F32), pltpu.VMEM((tm, d_model), F32),
            pltpu.SemaphoreType.DMA, pltpu.SemaphoreType.DMA((2,)), pltpu.SemaphoreType.DMA((4,)),
            pltpu.SemaphoreType.DMA((2, 4)), pltpu.SemaphoreType.DMA((2, 4)),
            pltpu.SemaphoreType.DMA((2, 6)), pltpu.SemaphoreType.DMA((2, 6)),
            pltpu.SemaphoreType.DMA((N_DEV,)), pltpu.SemaphoreType.DMA((N_DEV,)),
            pltpu.SemaphoreType.DMA((2,)), pltpu.SemaphoreType.DMA((2, N_DEV)), pltpu.SemaphoreType.DMA((N_DEV,)),
            pltpu.SemaphoreType.DMA,
        ],
        compiler_params=_params(56),
    )(xt, *dp_parts, dr, win_all, gwo, small)


def _adamw_update(g, w, m, v, rows):
    n_rows, width = w.shape
    rows = min(rows, n_rows)

    def body(g_ref, w_ref, m_ref, v_ref, d_o, nm_o, nv_o):
        d_o[...], nm_o[...], nv_o[...] = _adamw(w_ref[...], g_ref[...], m_ref[...], v_ref[...])

    tile = pl.BlockSpec((rows, width), lambda i: (i, 0))
    shape = jax.ShapeDtypeStruct(w.shape, F32)
    return pl.pallas_call(
        body, name="adamw_update", grid=(n_rows // rows,), in_specs=[tile] * 4, out_specs=(tile,) * 3,
        out_shape=(shape,) * 3,
    )(g, w, m, v)


def _small_update(grads, weights, ms, vs):
    n = len(grads)

    def body(*refs):
        g_refs, w_refs, m_refs, v_refs = (refs[i * n:(i + 1) * n] for i in range(4))
        outs = refs[4 * n:]
        for i in range(n):
            delta, nm, nv = _adamw(w_refs[i][...], g_refs[i][...], m_refs[i][...], v_refs[i][...])
            outs[3 * i][...] = delta
            outs[3 * i + 1][...] = nm
            outs[3 * i + 2][...] = nv

    vmem = pl.BlockSpec(memory_space=pltpu.VMEM)
    out_shape = []
    for w in weights:
        out_shape += [jax.ShapeDtypeStruct(w.shape, F32)] * 3
    return pl.pallas_call(
        body, name="small_update", in_specs=[vmem] * (4 * n), out_specs=(vmem,) * (3 * n), out_shape=tuple(out_shape),
    )(*grads, *weights, *ms, *vs)


def _tile_sizes(seq):
    return dict(tm=seq // N_DEV, t_ln=min(512, seq), t_attn=min(128, seq), rc=min(512, seq), pairs=4)


def kernel(x, w_in, conv_w, w_out, ln_gain, ln_bias, loss_target, m_w_in, m_conv_w, m_w_out, m_ln_gain, m_ln_bias,
           v_w_in, v_conv_w, v_w_out, v_ln_gain, v_ln_bias):
    assert x.shape[0] == 1 and w_in.shape[0] == 1, "one sequence per device, depth 1"
    _, seq, d_model = x.shape
    cw = w_in.shape[2]
    conv_k, conv_cols = conv_w.shape[1], conv_w.shape[2]
    rows_out = w_out.shape[1]
    assert cw == d_model // 2 and cw % PAIR == 0 and conv_cols * N_DEV == cw and rows_out * N_DEV == d_model
    ts = _tile_sizes(seq)

    x2 = x.reshape(seq, d_model)
    target = loss_target.reshape(seq, d_model)
    me = 4 * lax.axis_index("x") + 2 * lax.axis_index("y") + lax.axis_index("c")

    conv_pad = jnp.pad(conv_w[0], ((0, SUBLANES - conv_k), (0, PAIR - conv_cols)))
    proj, xt, win_all, conv_all = _gather_proj(x2, w_in[0], conv_pad, ts["tm"])
    conv_full = conv_all[:, :conv_k, :conv_cols].transpose(1, 0, 2).reshape(conv_k, cw)
    conv_full = jnp.pad(conv_full, ((0, SUBLANES - conv_k), (0, 0)))

    mix_conv = _conv_fwd(proj, conv_full, ts["rc"])
    pairs = min(ts["pairs"], cw // PAIR)
    tri = _triangles(ts["t_attn"])
    o, mix_attn, tot, wout_all = _attn_fwd(proj, tri, w_out[0], ts["t_attn"], pairs)
    w_out_full = wout_all.reshape(d_model, d_model)
    dr, d_mix_conv, d_mix_attn, gwo, small = _out_ln(mix_conv, mix_attn, x2, target, ln_gain, ln_bias, w_out_full,
                                                     ts["t_ln"])
    dp_conv, d_taps = _conv_bwd(proj, conv_full, d_mix_conv, ts["rc"])
    dp_qz, dp_kv = _attn_bwd(proj, tri, o, tot, d_mix_attn, ts["t_attn"], pairs)
    small = small.at[ROW_CONV:ROW_CONV + conv_k, :cw].set(d_taps[:conv_k])
    grad_x, g_in, g_out, small_sum = _grad_w_reduce(
        xt, (dp_conv, dp_qz, dp_kv), dr, win_all, gwo, small, rows_out, 128, ts["tm"])
    d_in, nm_in, nv_in = _adamw_update(g_in, w_in[0], m_w_in[0], v_w_in[0], 256)

    loss = small_sum[ROW_LOSS, 0]
    g_gain = small_sum[ROW_GAIN:ROW_GAIN + 1]
    g_bias = small_sum[ROW_BIAS:ROW_BIAS + 1]
    g_conv = lax.dynamic_slice(small_sum, (ROW_CONV, me * conv_cols), (conv_k, conv_cols))
    upd = _small_update((g_out, g_conv, g_gain, g_bias), (w_out[0], conv_w[0], ln_gain, ln_bias),
                        (m_w_out[0], m_conv_w[0], m_ln_gain, m_ln_bias),
                        (v_w_out[0], v_conv_w[0], v_ln_gain, v_ln_bias))
    (d_out, nm_out, nv_out, d_conv, nm_conv, nv_conv, d_gain, nm_gain, nv_gain, d_bias, nm_bias, nv_bias) = upd

    lead = lambda a: a[None]
    return (loss, grad_x.reshape(1, seq, d_model), lead(g_in), lead(g_conv), lead(g_out), g_gain, g_bias,
            lead(d_in), lead(d_conv), lead(d_out), d_gain, d_bias,
            lead(nm_in), lead(nm_conv), lead(nm_out), nm_gain, nm_bias,
            lead(nv_in), lead(nv_conv), lead(nv_out), nv_gain, nv_bias)
```

```python
import jax
import jax.numpy as jnp
from jax import lax
from jax.experimental import pallas as pl
from jax.experimental.pallas import tpu as pltpu

F32 = jnp.float32
BF16 = jnp.bfloat16
MESH = pl.DeviceIdType.MESH

N_DEV = 8
HEAD_DIM = 64
PAIR = 128
SUBLANES = 8
LN_EPS = 1e-5
ALPHA = 2.0 ** 0.25
ADAM_LR, ADAM_B1, ADAM_B2, ADAM_EPS, ADAM_WD, ADAM_STEP = 0.001, 0.9, 0.999, 1e-08, 0.01, 10

ROW_GAIN, ROW_BIAS, ROW_CONV, ROW_LOSS = 0, 1, 2, 5

NT = (((1,), (1,)), ((), ()))
TN = (((0,), (0,)), ((), ()))


V7X_VMEM_BYTES = 64 * 1024 * 1024


def _params(vmem_mib):
    assert vmem_mib * 1024 * 1024 < V7X_VMEM_BYTES
    return pltpu.CompilerParams(vmem_limit_bytes=vmem_mib * 1024 * 1024)


def _dot(a, b, dims=None):
    if dims is None:
        return jnp.dot(a, b, preferred_element_type=F32)
    return lax.dot_general(a, b, dims, preferred_element_type=F32)


def _sigmoid(z):
    return 1.0 / (1.0 + jnp.exp(-z))


def _mesh_pos():
    return lax.axis_index("x"), lax.axis_index("y"), lax.axis_index("c")


def _adamw(w, g, m, v):
    nm = ADAM_B1 * m + (1.0 - ADAM_B1) * g
    nv = ADAM_B2 * v + (1.0 - ADAM_B2) * (g * g)
    m_hat = nm * (1.0 / (1.0 - ADAM_B1 ** ADAM_STEP))
    v_hat = nv * (1.0 / (1.0 - ADAM_B2 ** ADAM_STEP))
    delta = -ADAM_LR * (m_hat / (jnp.sqrt(v_hat) + ADAM_EPS) + ADAM_WD * w)
    return delta, nm, nv


def _gather_proj(x, w_in_s, conv_s, tm):
    seq, d_model = x.shape
    cw = w_in_s.shape[1]
    n_tiles = seq // tm
    half = d_model // 2
    SIB, X_UP, X_LOW, Y_UP, Y_LOW, VIA_Y, VIA_X, ON_X, ON_Y, ON_DIAG = range(10)

    def body(x_hbm, win_ref, conv_ref, proj_hbm, xt_hbm, win_all, conv_all,
             xb, x_stage, o_stage, xt_stage, x_sems, o_sems, xt_sems, w_send, w_recv, send_sems, recv_sems):
        x, y, c = _mesh_pos()
        me = (x, y, c)
        sibling = (x, y, 1 - c)
        x_nbr, y_nbr, diag = (1 - x, y), (x, 1 - y), (1 - x, 1 - y)
        chips = [x_nbr, y_nbr, diag]
        small = (conv_all,)

        def slot(pos):
            return 4 * pos[0] + 2 * pos[1] + pos[2]

        def x_copy(tile, buf):
            return pltpu.make_async_copy(x_hbm.at[pl.ds(tile * tm, tm), :], x_stage.at[buf], x_sems.at[buf])

        x_copy(0, 0).start()
        win_all[slot(me)] = win_ref[...].astype(BF16)
        conv_all[slot(me)] = conv_ref[...]

        def w_copy(k, block, part, to):
            ref = win_all.at[slot(block)]
            if part is not None:
                ref = ref.at[pl.ds(part * half, half), :]
            return pltpu.make_async_remote_copy(
                src_ref=ref, dst_ref=ref, send_sem=w_send.at[k], recv_sem=w_recv.at[k],
                device_id=to, device_id_type=MESH)

        def copy(a, k, block, to):
            ref = small[a].at[slot(block)]
            return pltpu.make_async_remote_copy(
                src_ref=ref, dst_ref=ref, send_sem=send_sems.at[a, k], recv_sem=recv_sems.at[a, k],
                device_id=to, device_id_type=MESH)

        sends = [w_copy(SIB, me, None, sibling),
                 w_copy(X_UP, me, 0, (*x_nbr, c)), w_copy(Y_LOW, me, 1, (*y_nbr, c)),
                 w_copy(X_LOW, me, 1, (*x_nbr, c)), w_copy(Y_UP, me, 0, (*y_nbr, c))]
        for a in range(len(small)):
            sends.append(copy(a, 0, me, sibling))
            sends += [copy(a, 1 + j, me, (*chip, c)) for j, chip in enumerate(chips)]
        for cp in sends:
            cp.start()

        def o_copy(group, tile, buf):
            return pltpu.make_async_copy(o_stage.at[buf], proj_hbm.at[group, pl.ds(tile * tm, tm), :], o_sems.at[buf])

        def xt_copy(tile, buf):
            return pltpu.make_async_copy(xt_stage.at[buf], xt_hbm.at[:, pl.ds(tile * tm, tm)], xt_sems.at[buf])

        def project(order, group, first_pass):
            def tile_body(tile, _):
                if first_pass:
                    buf = tile % 2
                    x_copy(tile, buf).wait()

                    @pl.when(tile + 1 < n_tiles)
                    def _():
                        x_copy(tile + 1, 1 - buf).start()

                    xv = x_stage[buf]
                    xb[tile] = xv.astype(BF16)

                    @pl.when(tile >= 2)
                    def _():
                        xt_copy(tile - 2, buf).wait()

                    xt_stage[buf] = xv.T.astype(BF16)
                    xt_copy(tile, buf).start()
                count = order * n_tiles + tile
                obuf = count % 2

                @pl.when(count >= 2)
                def _():
                    o_copy(group, tile, obuf).wait()

                o_stage[obuf] = _dot(xb[tile], win_all[group]).astype(BF16)
                o_copy(group, tile, obuf).start()
                return 0

            lax.fori_loop(0, n_tiles, tile_body, 0)

        def start(cp):
            cp.start()
            sends.append(cp)

        def small_pass_on(j):
            for a in range(len(small)):
                copy(a, 1 + j, (*chips[j], c), me).wait_recv()
                start(copy(a, 4 + j, (*chips[j], c), sibling))

        def small_from_sibling(k):
            for a in range(len(small)):
                copy(a, k, sibling, me).wait_recv()

        project(0, slot(me), True)
        w_copy(SIB, sibling, None, me).wait_recv()
        small_from_sibling(0)
        project(1, slot(sibling), False)
        w_copy(X_UP, (*x_nbr, c), 0, me).wait_recv()
        start(w_copy(VIA_Y, (*x_nbr, c), 0, (*y_nbr, c)))
        w_copy(Y_LOW, (*y_nbr, c), 1, me).wait_recv()
        start(w_copy(VIA_X, (*y_nbr, c), 1, (*x_nbr, c)))
        w_copy(X_LOW, (*x_nbr, c), 1, me).wait_recv()
        start(w_copy(ON_X, (*x_nbr, c), None, sibling))
        small_pass_on(0)
        project(2, slot((*x_nbr, c)), False)
        w_copy(Y_UP, (*y_nbr, c), 0, me).wait_recv()
        start(w_copy(ON_Y, (*y_nbr, c), None, sibling))
        small_pass_on(1)
        project(3, slot((*y_nbr, c)), False)
        w_copy(ON_X, (*x_nbr, 1 - c), None, me).wait_recv()
        small_from_sibling(4)
        project(4, slot((*x_nbr, 1 - c)), False)
        w_copy(ON_Y, (*y_nbr, 1 - c), None, me).wait_recv()
        small_from_sibling(5)
        project(5, slot((*y_nbr, 1 - c)), False)
        w_copy(VIA_Y, (*diag, c), 0, me).wait_recv()
        w_copy(VIA_X, (*diag, c), 1, me).wait_recv()
        start(w_copy(ON_DIAG, (*diag, c), None, sibling))
        small_pass_on(2)
        project(6, slot((*diag, c)), False)
        w_copy(ON_DIAG, (*diag, 1 - c), None, me).wait_recv()
        small_from_sibling(6)
        project(7, slot((*diag, 1 - c)), False)

        for buf in range(2):
            o_copy(0, 0, buf).wait()
        for buf in range(min(2, n_tiles)):
            xt_copy(0, buf).wait()
        for cp in sends:
            cp.wait_send()

    vmem = pl.BlockSpec(memory_space=pltpu.VMEM)
    hbm = pl.BlockSpec(memory_space=pl.ANY)
    return pl.pallas_call(
        body, name="gather_proj",
        out_shape=(jax.ShapeDtypeStruct((N_DEV, seq, cw), BF16),
                   jax.ShapeDtypeStruct((d_model, seq), BF16),
                   jax.ShapeDtypeStruct((N_DEV, d_model, cw), BF16),
                   jax.ShapeDtypeStruct((N_DEV,) + conv_s.shape, F32)),
        in_specs=[hbm, vmem, vmem], out_specs=(hbm, hbm, vmem, vmem),
        scratch_shapes=[
            pltpu.VMEM((n_tiles, tm, d_model), BF16), pltpu.VMEM((2, tm, d_model), F32),
            pltpu.VMEM((2, tm, cw), BF16), pltpu.VMEM((2, d_model, tm), BF16),
            pltpu.SemaphoreType.DMA((2,)), pltpu.SemaphoreType.DMA((2,)), pltpu.SemaphoreType.DMA((2,)),
            pltpu.SemaphoreType.DMA((10,)), pltpu.SemaphoreType.DMA((10,)),
            pltpu.SemaphoreType.DMA((1, 7)), pltpu.SemaphoreType.DMA((1, 7))],
        compiler_params=_params(48),
    )(x, w_in_s, conv_s)


def _conv_taps(ext, w_ref, rc):
    u0 = ext[SUBLANES:SUBLANES + rc]
    u1 = pltpu.roll(ext, 1, 0)[SUBLANES:SUBLANES + rc]
    u2 = pltpu.roll(ext, 2, 0)[SUBLANES:SUBLANES + rc]
    return w_ref[2:3, :] * u0 + w_ref[1:2, :] * u1 + w_ref[0:1, :] * u2, u0, u1, u2


def _conv_fwd(proj, conv_full, rc):
    _, seq, cw = proj.shape

    def body(b_ref, c_ref, h_ref, z_ref, w_ref, o_ref, u_scr):
        u_scr[0:SUBLANES, :] = jnp.zeros((SUBLANES, PAIR), F32)

        def fill(r, _):
            base = pl.multiple_of(r * rc, rc)
            rows = pl.ds(base, rc)
            u_scr[pl.ds(base + SUBLANES, rc), :] = c_ref[rows, :].astype(F32) * h_ref[rows, :].astype(F32)
            return 0

        lax.fori_loop(0, seq // rc, fill, 0)

        def out(r, _):
            base = pl.multiple_of(r * rc, rc)
            rows = pl.ds(base, rc)
            ext = u_scr[pl.ds(base, rc + SUBLANES), :]
            y, _, _, _ = _conv_taps(ext, w_ref, rc)
            z = z_ref[rows, :].astype(F32)
            o_ref[rows, :] = (z * _sigmoid(z) * b_ref[rows, :].astype(F32) * y).astype(BF16)
            return 0

        lax.fori_loop(0, seq // rc, out, 0)

    def chunk(j):
        return pl.BlockSpec((None, seq, PAIR), lambda cb, j=j: (j, 0, cb))

    return pl.pallas_call(
        body, name="conv_fwd", grid=(cw // PAIR,),
        in_specs=[chunk(0), chunk(1), chunk(2), chunk(3), pl.BlockSpec((SUBLANES, PAIR), lambda cb: (0, cb))],
        out_specs=pl.BlockSpec((seq, PAIR), lambda cb: (0, cb)),
        out_shape=jax.ShapeDtypeStruct((seq, cw), BF16),
        scratch_shapes=[pltpu.VMEM((seq + SUBLANES, PAIR), F32)],
    )(proj, proj, proj, proj, conv_full)


def _conv_bwd(proj, conv_full, d_mix_conv, rc):
    _, seq, cw = proj.shape

    def body(b_ref, c_ref, h_ref, z_ref, w_ref, g_ref, dp_ref, dw_ref, u_scr, dy_scr):
        u_scr[0:SUBLANES, :] = jnp.zeros((SUBLANES, PAIR), F32)
        dy_scr[seq:seq + SUBLANES, :] = jnp.zeros((SUBLANES, PAIR), F32)

        def fill(r, _):
            base = pl.multiple_of(r * rc, rc)
            rows = pl.ds(base, rc)
            u_scr[pl.ds(base + SUBLANES, rc), :] = c_ref[rows, :].astype(F32) * h_ref[rows, :].astype(F32)
            return 0

        lax.fori_loop(0, seq // rc, fill, 0)

        def gate(r, acc):
            base = pl.multiple_of(r * rc, rc)
            rows = pl.ds(base, rc)
            ext = u_scr[pl.ds(base, rc + SUBLANES), :]
            y, u0, u1, u2 = _conv_taps(ext, w_ref, rc)
            z = z_ref[rows, :].astype(F32)
            b = b_ref[rows, :].astype(F32)
            g = g_ref[rows, :].astype(F32)
            sig = _sigmoid(z)
            dp_ref[3, rows, :] = (g * b * y * (sig * (1.0 + z * (1.0 - sig)))).astype(BF16)
            gs = g * (z * sig)
            dp_ref[0, rows, :] = (gs * y).astype(BF16)
            dy = gs * b
            dy_scr[rows, :] = dy
            a0, a1, a2 = acc
            return (a0 + jnp.sum(dy * u2, axis=0, keepdims=True),
                    a1 + jnp.sum(dy * u1, axis=0, keepdims=True),
                    a2 + jnp.sum(dy * u0, axis=0, keepdims=True))

        zero = jnp.zeros((1, PAIR), F32)
        a0, a1, a2 = lax.fori_loop(0, seq // rc, gate, (zero, zero, zero))
        dw_ref[...] = jnp.zeros((SUBLANES, PAIR), F32)
        dw_ref[0:1, :] = a0
        dw_ref[1:2, :] = a1
        dw_ref[2:3, :] = a2

        def back(r, _):
            base = pl.multiple_of(r * rc, rc)
            rows = pl.ds(base, rc)
            ext = dy_scr[pl.ds(base, rc + SUBLANES), :]
            n = rc + SUBLANES
            d0 = ext[0:rc]
            d1 = pltpu.roll(ext, n - 1, 0)[0:rc]
            d2 = pltpu.roll(ext, n - 2, 0)[0:rc]
            du = w_ref[2:3, :] * d0 + w_ref[1:2, :] * d1 + w_ref[0:1, :] * d2
            dp_ref[1, rows, :] = (du * h_ref[rows, :].astype(F32)).astype(BF16)
            dp_ref[2, rows, :] = (du * c_ref[rows, :].astype(F32)).astype(BF16)
            return 0

        lax.fori_loop(0, seq // rc, back, 0)

    def chunk(j):
        return pl.BlockSpec((None, seq, PAIR), lambda cb, j=j: (j, 0, cb))

    return pl.pallas_call(
        body, name="conv_bwd", grid=(cw // PAIR,),
        in_specs=[chunk(0), chunk(1), chunk(2), chunk(3), pl.BlockSpec((SUBLANES, PAIR), lambda cb: (0, cb)),
                  pl.BlockSpec((seq, PAIR), lambda cb: (0, cb))],
        out_specs=(pl.BlockSpec((4, seq, PAIR), lambda cb: (0, 0, cb)),
                   pl.BlockSpec((SUBLANES, PAIR), lambda cb: (0, cb))),
        out_shape=(jax.ShapeDtypeStruct((4, seq, cw), BF16), jax.ShapeDtypeStruct((SUBLANES, cw), F32)),
        scratch_shapes=[pltpu.VMEM((seq + SUBLANES, PAIR), F32), pltpu.VMEM((seq + SUBLANES, PAIR), F32)],
    )(proj, proj, proj, proj, conv_full, d_mix_conv)


SKIP_CARRY = 104.0
LOG2_E = 1.4426950408889634
LANE_TOT0, LANE_TOT1, LANE_FIRST, LANE_WHOLE = 0, 1, 2, 3
FAST_BLOCKS = 3
EARLY_ROWS = 32


def _triangles(t):
    row = lax.broadcasted_iota(jnp.int32, (2 * t, 2 * t), 0)
    col = lax.broadcasted_iota(jnp.int32, (2 * t, 2 * t), 1)
    same = (row < t) == (col < t)
    upper = jnp.logical_and(same, row > col).astype(BF16)
    lower = jnp.logical_and(same, row < col).astype(BF16)
    return jnp.stack([jnp.concatenate([upper, upper], axis=0), jnp.concatenate([lower, lower], axis=0)])


def _pair_masks(t):
    lane = lax.broadcasted_iota(jnp.int32, (t, PAIR), 1)
    qrow = lax.broadcasted_iota(jnp.int32, (t, 2 * t), 0)
    kcol = lax.broadcasted_iota(jnp.int32, (t, 2 * t), 1)
    strict = jnp.where(kcol < t, kcol, kcol - t) < qrow
    return lane, lane < HEAD_DIM, strict


def _by_head(x, head0):
    zero = jnp.zeros_like(x)
    return jnp.concatenate([jnp.where(head0, x, zero), jnp.where(head0, zero, x)], axis=0)


def _hi_lo(a):
    hi = a.astype(BF16)
    lo = (a - hi.astype(F32)).astype(BF16)
    return jnp.concatenate([hi, lo], axis=1)


def _softplus_parts(z, strict, masked):
    spu = jnp.maximum(z, 0.0) + jnp.log(1.0 + jnp.exp2(jnp.abs(z) * -LOG2_E))
    return z - spu, (jnp.where(strict, spu, 0.0) if masked else spu)


def _stacked_dot(parts, rhs):
    out = _dot(jnp.concatenate(parts, axis=0), rhs)
    ends = [0]
    for p in parts:
        ends.append(ends[-1] + p.shape[0])
    return [out[a:b] for a, b in zip(ends[:-1], ends[1:])]


def _splice(whole, rows, part):
    pieces = ([whole[:rows[0]]] if rows[0] > 0 else []) + [part]
    if rows[1] < whole.shape[0]:
        pieces.append(whole[rows[1]:])
    return part if len(pieces) == 1 else jnp.concatenate(pieces, axis=0)


def _attn_fwd(proj, tri, w_out_s, t, pp):
    _, seq, cw = proj.shape
    scale = HEAD_DIM ** -0.5
    width = pp * PAIR

    n_steps = (cw // width) * (seq // t)
    pass_on_step = min(8, n_steps - 1)

    def body(q_ref, k_ref, v_ref, za_ref, tri_ref, wout_ref, o_ref, mix_ref, tot_ref, wout_all, acc_ref,
             own_w, own_sem, send_sems, recv_sems):
        i = pl.program_id(1)
        step_no = pl.program_id(0) * (seq // t) + i
        mx, my, mc = _mesh_pos()
        me = (mx, my, mc)
        sibling = (mx, my, 1 - mc)
        chips = [(1 - mx, my), (mx, 1 - my), (1 - mx, 1 - my)]

        def place(block):
            return wout_all.at[4 * block[0] + 2 * block[1] + block[2]]

        def w_copy(k, block, to, src=None):
            return pltpu.make_async_remote_copy(
                src_ref=place(block) if src is None else src, dst_ref=place(block), send_sem=send_sems.at[k],
                recv_sem=recv_sems.at[k], device_id=to, device_id_type=MESH)

        own_copy = pltpu.make_async_copy(own_w, place(me), own_sem)

        @pl.when(step_no == 0)
        def _():
            barrier = pltpu.get_barrier_semaphore()
            for peer in [sibling] + [(*chip, mc) for chip in chips]:
                pl.semaphore_signal(barrier, inc=1, device_id=peer, device_id_type=MESH)
            pl.semaphore_wait(barrier, 4)
            own_w[...] = wout_ref[...].astype(BF16)
            own_copy.start()
            w_copy(0, me, sibling, own_w).start()
            for j, chip in enumerate(chips):
                w_copy(1 + j, me, (*chip, mc), own_w).start()

        @pl.when(step_no == pass_on_step)
        def _():
            for j, chip in enumerate(chips):
                w_copy(1 + j, (*chip, mc), me).wait_recv()
                w_copy(4 + j, (*chip, mc), sibling).start()

        @pl.when(step_no == n_steps - 1)
        def _():
            own_copy.wait()
            w_copy(0, sibling, me).wait_recv()
            for j, chip in enumerate(chips):
                w_copy(4 + j, (*chip, 1 - mc), me).wait_recv()
            for k in range(7):
                w_copy(k, me, sibling, own_w).wait_send()

        lane, head0, strict = _pair_masks(t)
        upper = tri_ref[0]
        q = q_ref[...] * scale
        acc_ref[...] = jnp.zeros_like(acc_ref)

        def sweep(blocks, state):
            staged = []
            for j, masked, rows in blocks:
                start = pl.multiple_of(j * t, t)
                kb = k_ref[pl.ds(start, t), :]
                vb = v_ref[pl.ds(start, t), :]
                for p in range(pp):
                    cols = slice(p * PAIR, (p + 1) * PAIR)
                    z = _dot(q[rows[0]:rows[1], cols], _by_head(kb[:, cols], head0), NT)
                    ls, sp = _softplus_parts(z, strict[rows[0]:rows[1]], masked)
                    staged.append((p, masked, rows, ls, sp, _by_head(vb[:, cols], head0)))
            afters = _stacked_dot([_hi_lo(sp) for _, _, _, _, sp, _ in staged], upper)
            state = list(state)
            for (p, masked, rows, ls, sp, v2), after in zip(staged, afters):
                c0, c1 = state[p]
                part = slice(rows[0], rows[1])
                x = ls - after
                w = jnp.exp(jnp.concatenate([x[:, :t] - c0[part], x[:, t:] - c1[part]], axis=1))
                if masked:
                    w = jnp.where(strict[part], w, 0.0)
                acc_ref[part, p * PAIR:(p + 1) * PAIR] += _dot(w.astype(BF16), v2)
                state[p] = (_splice(c0, rows, c0[part] + (after[:, 0:1] + sp[:, 0:1])),
                            _splice(c1, rows, c1[part] + (after[:, t:t + 1] + sp[:, t:t + 1])))
            return tuple(state)

        def unfinished(state, rows):
            m = state[0][0]
            for p in range(pp):
                m = jnp.minimum(m, jnp.minimum(state[p][0], state[p][1]))
            return jnp.min(m[rows[0]:rows[1]]) < SKIP_CARRY

        every = (0, t)
        early, late = (0, min(EARLY_ROWS, t)), (min(EARLY_ROWS, t), t)

        def step(js):
            state = sweep(((js[0], False, every),), js[1])
            return js[0] - 1, state, unfinished(state, every)

        def fast():
            blocks = tuple((i - b, b == 0, every) for b in range(FAST_BLOCKS - 1))
            state = sweep(blocks + ((i - (FAST_BLOCKS - 1), False, early),), init)
            if late[0] == late[1]:
                return state, jnp.bool_(True)
            whole = unfinished(state, late)
            state = lax.cond(whole, lambda: sweep(((i - (FAST_BLOCKS - 1), False, late),), state), lambda: state)
            return state, whole

        zcol = jnp.zeros((t, 1), F32)
        init = tuple((zcol, zcol) for _ in range(pp))
        many = i >= FAST_BLOCKS - 1
        state, whole = lax.cond(many, fast, lambda: (sweep(((i, True, every),), init), jnp.bool_(True)))
        j_end, state, _ = lax.while_loop(
            lambda js: jnp.logical_and(js[0] >= 0, js[2]), step,
            (jnp.where(many, i - FAST_BLOCKS, i - 1), state, unfinished(state, every)))
        first = (j_end + 1).astype(F32)
        notes = jnp.where(lane == LANE_FIRST, first, whole.astype(F32))
        za = za_ref[...].astype(F32)
        for p in range(pp):
            c0, c1 = state[p]
            cols = slice(p * PAIR, (p + 1) * PAIR)
            zp = za[:, cols]
            acc = acc_ref[:, cols]
            o_ref[:, cols] = acc.astype(BF16)
            mix_ref[:, cols] = (zp * _sigmoid(zp) * acc).astype(BF16)
            tot_ref[:, cols] = jnp.where(lane == LANE_TOT0, c0, jnp.where(lane == LANE_TOT1, c1, notes))

    def tile(j):
        return pl.BlockSpec((None, t, width), lambda g, i, j=j: (j, i, g))

    def full(j):
        return pl.BlockSpec((None, seq, width), lambda g, i, j=j: (j, 0, g))

    out_tile = pl.BlockSpec((t, width), lambda g, i: (i, g))
    return pl.pallas_call(
        body, name="attn_fwd", grid=(cw // width, seq // t),
        in_specs=[tile(4), full(5), full(6), tile(7), pl.BlockSpec(tri.shape, lambda g, i: (0, 0, 0)),
                  pl.BlockSpec(memory_space=pltpu.VMEM)],
        out_specs=(out_tile, out_tile, out_tile, pl.BlockSpec(memory_space=pl.ANY)),
        out_shape=(jax.ShapeDtypeStruct((seq, cw), BF16), jax.ShapeDtypeStruct((seq, cw), BF16),
                   jax.ShapeDtypeStruct((seq, cw), F32),
                   jax.ShapeDtypeStruct((N_DEV,) + w_out_s.shape, BF16)),
        scratch_shapes=[pltpu.VMEM((t, width), F32), pltpu.VMEM(w_out_s.shape, BF16), pltpu.SemaphoreType.DMA,
                        pltpu.SemaphoreType.DMA((7,)), pltpu.SemaphoreType.DMA((7,))],
        compiler_params=pltpu.CompilerParams(collective_id=0),
    )(proj, proj, proj, proj, tri, w_out_s)


def _attn_bwd(proj, tri, o, tot, d_mix_attn, t, pp):
    _, seq, cw = proj.shape
    nb = seq // t
    scale = HEAD_DIM ** -0.5
    width = pp * PAIR

    def body(q_ref, k_ref, v_ref, za_ref, tri_ref, o_ref, tot_ref, g_ref, dqz_ref, dkv_ref, dk_acc, dv_acc, dq_acc):
        i = pl.program_id(1)
        dq_acc[...] = jnp.zeros_like(dq_acc)

        @pl.when(i == 0)
        def _():
            dk_acc[...] = jnp.zeros_like(dk_acc)
            dv_acc[...] = jnp.zeros_like(dv_acc)

        _, head0, strict = _pair_masks(t)
        upper, lower = tri_ref[0], tri_ref[1, 0:2 * t, :]
        za = za_ref[...].astype(F32)
        g = g_ref[...].astype(F32)
        sig = _sigmoid(za)
        dqz_ref[1] = (g * o_ref[...].astype(F32) * (sig * (1.0 + za * (1.0 - sig)))).astype(BF16)
        do = (g * (za * sig)).astype(BF16)
        q = q_ref[...] * scale
        tot_v = tot_ref[...]
        q2, do2, init = [], [], []
        zcol = jnp.zeros((t, 1), F32)
        for p in range(pp):
            cols = slice(p * PAIR, (p + 1) * PAIR)
            q2.append(_by_head(q[:, cols], head0))
            do2.append(_by_head(do[:, cols], head0))
            tp = tot_v[:, cols]
            init.append(((tp[:, LANE_TOT0:LANE_TOT0 + 1], tp[:, LANE_TOT1:LANE_TOT1 + 1]), (zcol, zcol)))
        first = jnp.clip(tot_v[0:1, LANE_FIRST:LANE_FIRST + 1], 0.0, i.astype(F32)).astype(jnp.int32)[0, 0]

        def sweep(blocks, state):
            staged = []
            for j, masked, rows in blocks:
                start = pl.multiple_of(j * t, t)
                kb = k_ref[pl.ds(start, t), :]
                vb = v_ref[pl.ds(start, t), :]
                part = slice(rows[0], rows[1])
                for p in range(pp):
                    cols = slice(p * PAIR, (p + 1) * PAIR)
                    k2 = _by_head(kb[:, cols], head0)
                    z = _dot(q[part, cols], k2, NT)
                    ls, sp = _softplus_parts(z, strict[part], masked)
                    da = _dot(do[part, cols], _by_head(vb[:, cols], head0), NT)
                    staged.append((p, masked, rows, k2, ls, sp, da))
            afters = _stacked_dot([_hi_lo(sp) for _, _, _, _, _, sp, _ in staged], upper)
            state = list(state)
            weights, ggs = [], []
            for (p, masked, rows, k2, ls, sp, da), after in zip(staged, afters):
                (s0, s1), befores = state[p]
                part = slice(rows[0], rows[1])
                n0 = s0[part] - (after[:, 0:1] + sp[:, 0:1])
                n1 = s1[part] - (after[:, t:t + 1] + sp[:, t:t + 1])
                x = ls - after
                a = jnp.exp(jnp.concatenate([x[:, :t] - n0, x[:, t:] - n1], axis=1))
                if masked:
                    a = jnp.where(strict[part], a, 0.0)
                state[p] = ((_splice(s0, rows, n0), _splice(s1, rows, n1)), befores)
                weights.append(a.astype(BF16))
                ggs.append(a * da)
            pres = _stacked_dot([gg.astype(BF16) for gg in ggs], lower)
            dzs = []
            for (p, masked, rows, k2, ls, sp, da), gg, pre in zip(staged, ggs, pres):
                rests, (b0, b1) = state[p]
                part = slice(rows[0], rows[1])
                y = gg + pre
                dz = gg - jnp.exp(ls) * jnp.concatenate([y[:, :t] + b0[part], y[:, t:] + b1[part]], axis=1)
                if masked:
                    dz = jnp.where(strict[part], dz, 0.0)
                dzb = dz.astype(BF16)
                dzs.append(dzb)
                dq_acc[part, p * PAIR:(p + 1) * PAIR] += _dot(dzb, k2)
                state[p] = (rests, (_splice(b0, rows, b0[part] + y[:, t - 1:t]),
                                    _splice(b1, rows, b1[part] + y[:, 2 * t - 1:2 * t])))
            first_row = pl.multiple_of(blocks[0][0] * t, t)
            n_rows = len(blocks) * t
            for p in range(pp):
                cols = slice(p * PAIR, (p + 1) * PAIR)

                def by_key(tiles):
                    out = []
                    for n, m in tiles:
                        rows = staged[n][2]
                        m = _splice(jnp.zeros((t, 2 * t), BF16), rows, m)
                        out.append(jnp.concatenate([m[:, :t], m[:, t:]], axis=0).T)
                    return jnp.concatenate(out, axis=0)

                mine = [n for n in range(len(staged)) if staged[n][0] == p]
                dk_acc[pl.ds(first_row, n_rows), cols] += _dot(by_key([(n, dzs[n]) for n in mine]), q2[p])
                dv_acc[pl.ds(first_row, n_rows), cols] += _dot(by_key([(n, weights[n]) for n in mine]), do2[p])
            return tuple(state)

        every = (0, t)
        early = (0, min(EARLY_ROWS, t))
        many = i >= FAST_BLOCKS - 1
        last_single = jnp.where(many, i - (FAST_BLOCKS - 1), i)
        state = lax.fori_loop(first, last_single, lambda j, s: sweep(((j, False, every),), s), tuple(init))

        def fast(rows):
            blocks = tuple((i - b, b == 0, every) for b in range(FAST_BLOCKS - 2, -1, -1))
            return sweep(((i - (FAST_BLOCKS - 1), False, rows),) + blocks, state)

        whole = tot_v[0:1, LANE_WHOLE:LANE_WHOLE + 1].astype(jnp.int32)[0, 0] > 0
        state = lax.cond(
            many,
            lambda: lax.cond(whole, lambda: fast(every), lambda: fast(early)),
            lambda: sweep(((i, True, every),), state))
        dqz_ref[0] = (dq_acc[...] * scale).astype(BF16)

        @pl.when(i == nb - 1)
        def _():
            dkv_ref[0] = dk_acc[...].astype(BF16)
            dkv_ref[1] = dv_acc[...].astype(BF16)

    def tile(j):
        return pl.BlockSpec((None, t, width), lambda g, i, j=j: (j, i, g))

    def full(j):
        return pl.BlockSpec((None, seq, width), lambda g, i, j=j: (j, 0, g))

    flat_tile = pl.BlockSpec((t, width), lambda g, i: (i, g))
    return pl.pallas_call(
        body, name="attn_bwd", grid=(cw // width, nb),
        in_specs=[tile(4), full(5), full(6), tile(7), pl.BlockSpec(tri.shape, lambda g, i: (0, 0, 0)),
                  flat_tile, flat_tile, flat_tile],
        out_specs=(pl.BlockSpec((2, t, width), lambda g, i: (0, i, g)),
                   pl.BlockSpec((2, seq, width), lambda g, i: (0, 0, g))),
        out_shape=(jax.ShapeDtypeStruct((2, seq, cw), BF16), jax.ShapeDtypeStruct((2, seq, cw), BF16)),
        scratch_shapes=[pltpu.VMEM((seq, width), F32), pltpu.VMEM((seq, width), F32), pltpu.VMEM((t, width), F32)],
        compiler_params=_params(48),
    )(proj, proj, proj, proj, tri, o, tot, d_mix_attn)


def _out_ln(mix_conv, mix_attn, x, target, gain, bias, w_out, tm):
    seq, d_model = x.shape
    cw = mix_conv.shape[1]
    inv_d = 1.0 / d_model

    def body(mc_ref, ma_ref, x_ref, t_ref, gain_ref, bias_ref, w_ref, dr_ref, dmc_ref, dma_ref, gwo_ref, small_ref):
        @pl.when(pl.program_id(0) == 0)
        def _():
            gwo_ref[...] = jnp.zeros_like(gwo_ref)
            small_ref[...] = jnp.zeros_like(small_ref)

        mix = jnp.concatenate([mc_ref[...], ma_ref[...]], axis=1)
        w = w_ref[...]
        r = ALPHA * x_ref[...] + _dot(mix, w)
        mu = jnp.sum(r, axis=1, keepdims=True) * inv_d
        xc = r - mu
        var = jnp.sum(xc * xc, axis=1, keepdims=True) * inv_d
        rstd = lax.rsqrt(var + LN_EPS)
        xhat = xc * rstd
        gain_v = gain_ref[...]
        err = xhat * gain_v + bias_ref[...] - t_ref[...]
        row_loss = jnp.sum(err * err, axis=1, keepdims=True)
        loss = (0.5 * inv_d) * jnp.sum(row_loss, axis=0, keepdims=True)
        dy = err * inv_d
        small_ref[ROW_GAIN:ROW_GAIN + 1, :] += jnp.sum(dy * xhat, axis=0, keepdims=True)
        small_ref[ROW_BIAS:ROW_BIAS + 1, :] += jnp.sum(dy, axis=0, keepdims=True)
        small_ref[ROW_LOSS:ROW_LOSS + 1, :] += jnp.broadcast_to(loss, (1, d_model))
        dxhat = dy * gain_v
        m1 = jnp.sum(dxhat, axis=1, keepdims=True) * inv_d
        m2 = jnp.sum(dxhat * xhat, axis=1, keepdims=True) * inv_d
        dr = rstd * (dxhat - m1 - xhat * m2)
        dr_ref[...] = dr
        drb = dr.astype(BF16)
        dmix = _dot(drb, w, NT)
        dmc_ref[...] = dmix[:, :cw].astype(BF16)
        dma_ref[...] = dmix[:, cw:].astype(BF16)
        gwo_ref[...] += _dot(mix, drb, TN)

    def rows(width):
        return pl.BlockSpec((tm, width), lambda i: (i, 0))

    def whole(shape):
        return pl.BlockSpec(shape, lambda i: (0, 0))

    return pl.pallas_call(
        body, name="out_ln", grid=(seq // tm,),
        in_specs=[rows(cw), rows(cw), rows(d_model), rows(d_model), whole((1, d_model)), whole((1, d_model)),
                  whole((d_model, d_model))],
        out_specs=(rows(d_model), rows(cw), rows(cw), whole((d_model, d_model)), whole((SUBLANES, d_model))),
        out_shape=(jax.ShapeDtypeStruct((seq, d_model), F32), jax.ShapeDtypeStruct((seq, cw), BF16),
                   jax.ShapeDtypeStruct((seq, cw), BF16), jax.ShapeDtypeStruct((d_model, d_model), F32),
                   jax.ShapeDtypeStruct((SUBLANES, d_model), F32)),
        compiler_params=_params(48),
    )(mix_conv, mix_attn, x, target, gain, bias, w_out)


_DP_OF_GROUP = ((0, 0), (0, 1), (0, 2), (0, 3), (1, 0), (2, 0), (2, 1), (1, 1))


def _grad_w_reduce(xt, dp_parts, dr, win_all, gwo, small, rows_out, row_chunk, tm):
    d_model, seq = xt.shape
    nch, _, cw = win_all.shape
    gx_tiles = seq // tm
    assert nch * tm == seq and nch * cw == seq, "needs S == 8 * tm == 8 * CW"

    def body(xt_hbm, dpa, dpb, dpc, dr_hbm, win_hbm, gwo_ref, small_ref,
             gx_hbm, g_in_o, g_out_o, small_o,
             xt_v, dp_buf, acc, got_in, send_in, recv_in, own_out, got_out, send_out, recv_out, small_all,
             dr_buf, gx_buf,
             xt_sem, dp_sems, loc_sems, d2d_send, d2d_recv, ici_send, ici_recv, sm_send, sm_recv,
             dr_sems, dpx_sems, w_sems, gx_sem):
        x, y, c = _mesh_pos()
        me = 4 * x + 2 * y + c
        sibling = (x, y, 1 - c)
        chips = [(1 - x, 1 - y), (1 - x, y), (x, 1 - y)]
        owners = [(*chip, cc) for chip in chips for cc in (1 - c, c)] + [sibling, (x, y, c)]
        group_of = [4 * o[0] + 2 * o[1] + o[2] for o in owners]
        dp_parts_ = (dpa, dpb, dpc)
        dp_groups = [dp_parts_[arr].at[idx] for arr, idx in _DP_OF_GROUP]

        xt_copy = pltpu.make_async_copy(xt_hbm, xt_v, xt_sem)
        xt_copy.start()

        def dp_start(step):
            for k in range(N_DEV):
                @pl.when(group_of[step] == k)
                def _(k=k):
                    pltpu.make_async_copy(dp_groups[k], dp_buf.at[step % 2], dp_sems.at[step % 2]).start()

        def dp_wait(step):
            pltpu.make_async_copy(dp_groups[0], dp_buf.at[step % 2], dp_sems.at[step % 2]).wait()

        dp_start(0)

        small_all[me] = small_ref[...]
        for d in range(N_DEV):
            @pl.when(d != me)
            def _(d=d):
                pltpu.make_async_remote_copy(
                    src_ref=small_ref, dst_ref=small_all.at[me], send_sem=sm_send.at[d], recv_sem=sm_recv.at[me],
                    device_id=(d // 4, (d // 2) % 2, d % 2), device_id_type=MESH).start()

        def block_out(k):
            return gwo_ref.at[pl.ds(k * rows_out, rows_out), :]

        for k in range(N_DEV):
            s = k // 2

            @pl.when(k % 2 != c)
            def _(k=k, s=s):
                pltpu.make_async_remote_copy(
                    src_ref=block_out(k), dst_ref=got_out.at[s], send_sem=d2d_send.at[1, s],
                    recv_sem=d2d_recv.at[1, s], device_id=sibling, device_id_type=MESH).start()

            @pl.when(k % 2 == c)
            def _(k=k, s=s):
                pltpu.make_async_copy(block_out(k), own_out.at[s], loc_sems.at[s]).start()

        VIA_X, VIA_Y, X_UP, X_LOW, Y_UP, Y_LOW = range(6)
        x_dev, y_dev = (1 - x, y, c), (x, 1 - y, c)
        ici_place = {VIA_X: (0, 0, x_dev), VIA_Y: (0, 1, y_dev), X_UP: (1, 0, x_dev), X_LOW: (1, 1, x_dev),
                     Y_UP: (2, 0, y_dev), Y_LOW: (2, 1, y_dev)}

        def ici_copy(a, k):
            send, recv = ((send_in, recv_in), (send_out, recv_out))[a]
            half = (d_model, rows_out)[a] // 2
            slot, part, to = ici_place[k]
            rows = pl.ds(part * half, half)
            return pltpu.make_async_remote_copy(
                src_ref=send.at[slot, rows, :], dst_ref=recv.at[slot, rows, :], send_sem=ici_send.at[a, k],
                recv_sem=ici_recv.at[a, k], device_id=to, device_id_type=MESH)

        def by_rows(first, n_rows, fn):
            step = min(row_chunk, n_rows)

            def rows_body(r, _):
                fn(pl.ds(pl.multiple_of(first + r * step, step), step))
                return 0

            lax.fori_loop(0, n_rows // step, rows_body, 0)

        def send_chip_sum(a, j, chip_sum):
            send, recv = ((send_in, recv_in), (send_out, recv_out))[a]
            n_rows = (d_model, rows_out)[a]
            half = n_rows // 2

            def plain(rows):
                send[j, rows, :] = chip_sum(rows).astype(BF16)

            if j == 0:
                by_rows(0, n_rows, plain)
                ici_copy(a, VIA_X).start()
                ici_copy(a, VIA_Y).start()
                return
            free, bound, passed = ((0, X_UP), (1, X_LOW), VIA_Y) if j == 1 else ((1, Y_LOW), (0, Y_UP), VIA_X)
            by_rows(free[0] * half, half, plain)
            ici_copy(a, free[1]).start()
            ici_copy(a, passed).wait_recv()

            def with_passed(rows):
                send[j, rows, :] = (chip_sum(rows) + recv[0, rows, :].astype(F32)).astype(BF16)

            by_rows(bound[0] * half, half, with_passed)
            ici_copy(a, bound[1]).start()

        for s in range(4):
            pltpu.make_async_copy(own_out.at[s], own_out.at[s], loc_sems.at[s]).wait()
            pltpu.make_async_remote_copy(
                src_ref=got_out.at[s], dst_ref=got_out.at[s], send_sem=d2d_send.at[1, s], recv_sem=d2d_recv.at[1, s],
                device_id=sibling, device_id_type=MESH).wait()
        for j, chip in enumerate(chips):
            s = 2 * chip[0] + chip[1]
            send_chip_sum(1, j, lambda rows, s=s: own_out[s, rows, :] + got_out[s, rows, :])

        xt_copy.wait()

        def d2d_copy(slot, pair):
            return pltpu.make_async_remote_copy(
                src_ref=acc.at[slot], dst_ref=got_in.at[pair], send_sem=d2d_send.at[0, pair],
                recv_sem=d2d_recv.at[0, pair], device_id=sibling, device_id_type=MESH)

        def gx_in(tile, buf):
            rows = pl.ds(tile * tm, tm)
            copies = [pltpu.make_async_copy(dr_hbm.at[rows, :], dr_buf.at[buf], dr_sems.at[buf])]
            copies += [pltpu.make_async_copy(dp_groups[k].at[rows, :], dp_buf.at[buf, pl.ds(k * tm, tm), :],
                                             dpx_sems.at[buf, k]) for k in range(N_DEV)]
            return copies

        for step in range(N_DEV):
            slot, pair = step % 2, step // 2
            if step % 2 == 0 and step >= 2:
                d2d_copy(slot, pair - 1).wait_send()

            dp_wait(step)
            if step + 1 < N_DEV:
                dp_start(step + 1)
            else:
                for cp in gx_in(0, (step + 1) % 2):
                    cp.start()
            acc[slot] = _dot(xt_v[...], dp_buf[step % 2])

            if step % 2 == 0:
                d2d_copy(slot, pair).start()
            else:
                d2d_copy(slot, pair).wait_recv()
                if step < N_DEV - 1:
                    send_chip_sum(0, pair, lambda rows, slot=slot, pair=pair: acc[slot, rows, :] + got_in[pair, rows, :])

        def w_copy(k):
            return pltpu.make_async_copy(win_hbm.at[k], xt_v.at[:, pl.ds(k * cw, cw)], w_sems.at[k])

        for k in range(N_DEV):
            w_copy(k).start()

        def gx_out(tile):
            return pltpu.make_async_copy(gx_buf, gx_hbm.at[pl.ds(tile * tm, tm), :], gx_sem)

        for k in range(N_DEV):
            w_copy(k).wait()

        def gx_body(tile, _):
            buf = tile % 2
            for cp in gx_in(tile, buf):
                cp.wait()

            @pl.when(tile + 1 < gx_tiles)
            def _():
                for cp in gx_in(tile + 1, 1 - buf):
                    cp.start()

            val = ALPHA * dr_buf[buf]
            for k in range(N_DEV):
                val = val + _dot(dp_buf[buf, k * tm:(k + 1) * tm, :], xt_v[:, k * cw:(k + 1) * cw], NT)

            @pl.when(tile > 0)
            def _():
                gx_out(tile - 1).wait()

            gx_buf[...] = val
            gx_out(tile).start()
            return 0

        lax.fori_loop(0, gx_tiles, gx_body, 0)

        for d in range(N_DEV):
            @pl.when(d != me)
            def _(d=d):
                pltpu.make_async_remote_copy(
                    src_ref=small_ref, dst_ref=small_all.at[d], send_sem=sm_send.at[d], recv_sem=sm_recv.at[d],
                    device_id=(d // 4, (d // 2) % 2, d % 2), device_id_type=MESH).wait()
        total = small_all[0]
        for d in range(1, N_DEV):
            total = total + small_all[d]
        small_o[...] = total

        mine = 2 * x + y
        last = (N_DEV - 1) % 2

        def finish(a, n_rows, chip_sum, g_o):
            recv = (recv_in, recv_out)[a]
            for k in (X_UP, X_LOW, Y_UP, Y_LOW):
                ici_copy(a, k).wait_recv()
            for k in ici_place:
                ici_copy(a, k).wait_send()

            def total_rows(rows):
                g_o[rows, :] = chip_sum(rows) + recv[1, rows, :].astype(F32) + recv[2, rows, :].astype(F32)

            by_rows(0, n_rows, total_rows)

        finish(1, rows_out, lambda rows: own_out[mine, rows, :] + got_out[mine, rows, :], g_out_o)
        finish(0, d_model, lambda rows: acc[last, rows, :] + got_in[N_DEV // 2 - 1, rows, :], g_in_o)
        d2d_copy(0, N_DEV // 2 - 1).wait_send()
        gx_out(0).wait()

    vmem = pl.BlockSpec(memory_space=pltpu.VMEM)
    hbm = pl.BlockSpec(memory_space=pl.ANY)
    return pl.pallas_call(
        body, name="grad_w_reduce",
        in_specs=[hbm] * 7 + [vmem],
        out_specs=(hbm, vmem, vmem, vmem),
        out_shape=(jax.ShapeDtypeStruct((seq, d_model), F32), jax.ShapeDtypeStruct((d_model, cw), F32),
                   jax.ShapeDtypeStruct((rows_out, d_model), F32), jax.ShapeDtypeStruct(small.shape, F32)),
        scratch_shapes=[
            pltpu.VMEM((d_model, seq), BF16), pltpu.VMEM((2, seq, cw), BF16), pltpu.VMEM((2, d_model, cw), F32),
            pltpu.VMEM((4, d_model, cw), F32), pltpu.VMEM((3, d_model, cw), BF16), pltpu.VMEM((3, d_model, cw), BF16),
            pltpu.VMEM((4, rows_out, d_model), F32), pltpu.VMEM((4, rows_out, d_model), F32),
            pltpu.VMEM((3, rows_out, d_model), BF16), pltpu.VMEM((3, rows_out, d_model), BF16),
            pltpu.VMEM((N_DEV,) + small.shape, F32),
            pltpu.VMEM((2, tm, d_model), F32), pltpu.VMEM((tm, d_model), F32),
            pltpu.SemaphoreType.DMA, pltpu.SemaphoreType.DMA((2,)), pltpu.SemaphoreType.DMA((4,)),
            pltpu.SemaphoreType.DMA((2, 4)), pltpu.SemaphoreType.DMA((2, 4)),
            pltpu.SemaphoreType.DMA((2, 6)), pltpu.SemaphoreType.DMA((2, 6)),
            pltpu.SemaphoreType.DMA((N_DEV,)), pltpu.SemaphoreType.DMA((N_DEV,)),
            pltpu.SemaphoreType.DMA((2,)), pltpu.SemaphoreType.DMA((2, N_DEV)), pltpu.SemaphoreType.DMA((N_DEV,)),
            pltpu.SemaphoreType.DMA,
        ],
        compiler_params=_params(56),
    )(xt, *dp_parts, dr, win_all, gwo, small)


def _adamw_update(g, w, m, v, rows):
    n_rows, width = w.shape
    rows = min(rows, n_rows)

    def body(g_ref, w_ref, m_ref, v_ref, d_o, nm_o, nv_o):
        d_o[...], nm_o[...], nv_o[...] = _adamw(w_ref[...], g_ref[...], m_ref[...], v_ref[...])

    tile = pl.BlockSpec((rows, width), lambda i: (i, 0))
    shape = jax.ShapeDtypeStruct(w.shape, F32)
    return pl.pallas_call(
        body, name="adamw_update", grid=(n_rows // rows,), in_specs=[tile] * 4, out_specs=(tile,) * 3,
        out_shape=(shape,) * 3,
    )(g, w, m, v)


def _small_update(grads, weights, ms, vs):
    n = len(grads)

    def body(*refs):
        g_refs, w_refs, m_refs, v_refs = (refs[i * n:(i + 1) * n] for i in range(4))
        outs = refs[4 * n:]
        for i in range(n):
            delta, nm, nv = _adamw(w_refs[i][...], g_refs[i][...], m_refs[i][...], v_refs[i][...])
            outs[3 * i][...] = delta
            outs[3 * i + 1][...] = nm
            outs[3 * i + 2][...] = nv

    vmem = pl.BlockSpec(memory_space=pltpu.VMEM)
    out_shape = []
    for w in weights:
        out_shape += [jax.ShapeDtypeStruct(w.shape, F32)] * 3
    return pl.pallas_call(
        body, name="small_update", in_specs=[vmem] * (4 * n), out_specs=(vmem,) * (3 * n), out_shape=tuple(out_shape),
    )(*grads, *weights, *ms, *vs)


def _tile_sizes(seq):
    return dict(tm=seq // N_DEV, t_ln=min(512, seq), t_attn=min(128, seq), rc=min(256, seq), pairs=4)


def kernel(x, w_in, conv_w, w_out, ln_gain, ln_bias, loss_target, m_w_in, m_conv_w, m_w_out, m_ln_gain, m_ln_bias,
           v_w_in, v_conv_w, v_w_out, v_ln_gain, v_ln_bias):
    assert x.shape[0] == 1 and w_in.shape[0] == 1, "one sequence per device, depth 1"
    _, seq, d_model = x.shape
    cw = w_in.shape[2]
    conv_k, conv_cols = conv_w.shape[1], conv_w.shape[2]
    rows_out = w_out.shape[1]
    assert cw == d_model // 2 and cw % PAIR == 0 and conv_cols * N_DEV == cw and rows_out * N_DEV == d_model
    ts = _tile_sizes(seq)

    x2 = x.reshape(seq, d_model)
    target = loss_target.reshape(seq, d_model)
    me = 4 * lax.axis_index("x") + 2 * lax.axis_index("y") + lax.axis_index("c")

    conv_pad = jnp.pad(conv_w[0], ((0, SUBLANES - conv_k), (0, PAIR - conv_cols)))
    proj, xt, win_all, conv_all = _gather_proj(x2, w_in[0], conv_pad, ts["tm"])
    conv_full = conv_all[:, :conv_k, :conv_cols].transpose(1, 0, 2).reshape(conv_k, cw)
    conv_full = jnp.pad(conv_full, ((0, SUBLANES - conv_k), (0, 0)))

    mix_conv = _conv_fwd(proj, conv_full, ts["rc"])
    pairs = min(ts["pairs"], cw // PAIR)
    tri = _triangles(ts["t_attn"])
    o, mix_attn, tot, wout_all = _attn_fwd(proj, tri, w_out[0], ts["t_attn"], pairs)
    w_out_full = wout_all.reshape(d_model, d_model)
    dr, d_mix_conv, d_mix_attn, gwo, small = _out_ln(mix_conv, mix_attn, x2, target, ln_gain, ln_bias, w_out_full,
                                                     ts["t_ln"])
    dp_conv, d_taps = _conv_bwd(proj, conv_full, d_mix_conv, ts["rc"])
    dp_qz, dp_kv = _attn_bwd(proj, tri, o, tot, d_mix_attn, ts["t_attn"], pairs)
    small = small.at[ROW_CONV:ROW_CONV + conv_k, :cw].set(d_taps[:conv_k])
    grad_x, g_in, g_out, small_sum = _grad_w_reduce(
        xt, (dp_conv, dp_qz, dp_kv), dr, win_all, gwo, small, rows_out, 128, ts["tm"])
    d_in, nm_in, nv_in = _adamw_update(g_in, w_in[0], m_w_in[0], v_w_in[0], 256)

    loss = small_sum[ROW_LOSS, 0]
    g_gain = small_sum[ROW_GAIN:ROW_GAIN + 1]
    g_bias = small_sum[ROW_BIAS:ROW_BIAS + 1]
    g_conv = lax.dynamic_slice(small_sum, (ROW_CONV, me * conv_cols), (conv_k, conv_cols))
    upd = _small_update((g_out, g_conv, g_gain, g_bias), (w_out[0], conv_w[0], ln_gain, ln_bias),
                        (m_w_out[0], m_conv_w[0], m_ln_gain, m_ln_bias),
                        (v_w_out[0], v_conv_w[0], v_ln_gain, v_ln_bias))
    (d_out, nm_out, nv_out, d_conv, nm_conv, nv_conv, d_gain, nm_gain, nv_gain, d_bias, nm_bias, nv_bias) = upd

    lead = lambda a: a[None]
    return (loss, grad_x.reshape(1, seq, d_model), lead(g_in), lead(g_conv), lead(g_out), g_gain, g_bias,
            lead(d_in), lead(d_conv), lead(d_out), d_gain, d_bias,
            lead(nm_in), lead(nm_conv), lead(nm_out), nm_gain, nm_bias,
            lead(nv_in), lead(nv_conv), lead(nv_out), nv_gain, nv_bias)
```

```python
import jax
import jax.numpy as jnp
from jax import lax
from jax.experimental import pallas as pl
from jax.experimental.pallas import tpu as pltpu

F32 = jnp.float32
BF16 = jnp.bfloat16
MESH = pl.DeviceIdType.MESH

N_DEV = 8
HEAD_DIM = 64
PAIR = 128
SUBLANES = 8
LN_EPS = 1e-5
ALPHA = 2.0 ** 0.25
ADAM_LR, ADAM_B1, ADAM_B2, ADAM_EPS, ADAM_WD, ADAM_STEP = 0.001, 0.9, 0.999, 1e-08, 0.01, 10

ROW_GAIN, ROW_BIAS, ROW_CONV, ROW_LOSS = 0, 1, 2, 5

NT = (((1,), (1,)), ((), ()))
TN = (((0,), (0,)), ((), ()))


V7X_VMEM_BYTES = 64 * 1024 * 1024


def _params(vmem_mib, collective_id=None):
    assert vmem_mib * 1024 * 1024 < V7X_VMEM_BYTES
    return pltpu.CompilerParams(vmem_limit_bytes=vmem_mib * 1024 * 1024, collective_id=collective_id)


def _dot(a, b, dims=None):
    if dims is None:
        return jnp.dot(a, b, preferred_element_type=F32)
    return lax.dot_general(a, b, dims, preferred_element_type=F32)


def _sigmoid(z):
    return 1.0 / (1.0 + jnp.exp(-z))


def _mesh_pos():
    return lax.axis_index("x"), lax.axis_index("y"), lax.axis_index("c")


def _adamw(w, g, m, v):
    nm = ADAM_B1 * m + (1.0 - ADAM_B1) * g
    nv = ADAM_B2 * v + (1.0 - ADAM_B2) * (g * g)
    m_hat = nm * (1.0 / (1.0 - ADAM_B1 ** ADAM_STEP))
    v_hat = nv * (1.0 / (1.0 - ADAM_B2 ** ADAM_STEP))
    delta = -ADAM_LR * (m_hat / (jnp.sqrt(v_hat) + ADAM_EPS) + ADAM_WD * w)
    return delta, nm, nv


def _gather_proj(x, w_in_s, conv_s, tm):
    seq, d_model = x.shape
    cw = w_in_s.shape[1]
    n_tiles = seq // tm
    half = d_model // 2
    SIB, X_UP, X_LOW, Y_UP, Y_LOW, VIA_Y, VIA_X, ON_X, ON_Y, ON_DIAG = range(10)

    def body(x_hbm, win_ref, conv_ref, proj_hbm, xt_hbm, win_all, conv_all,
             xb, x_stage, o_stage, xt_stage, x_sems, o_sems, xt_sems, w_send, w_recv, send_sems, recv_sems):
        x, y, c = _mesh_pos()
        me = (x, y, c)
        sibling = (x, y, 1 - c)
        x_nbr, y_nbr, diag = (1 - x, y), (x, 1 - y), (1 - x, 1 - y)
        chips = [x_nbr, y_nbr, diag]
        small = (conv_all,)

        def slot(pos):
            return 4 * pos[0] + 2 * pos[1] + pos[2]

        def x_copy(tile, buf):
            return pltpu.make_async_copy(x_hbm.at[pl.ds(tile * tm, tm), :], x_stage.at[buf], x_sems.at[buf])

        x_copy(0, 0).start()
        win_all[slot(me)] = win_ref[...].astype(BF16)
        conv_all[slot(me)] = conv_ref[...]

        def w_copy(k, block, part, to):
            ref = win_all.at[slot(block)]
            if part is not None:
                ref = ref.at[pl.ds(part * half, half), :]
            return pltpu.make_async_remote_copy(
                src_ref=ref, dst_ref=ref, send_sem=w_send.at[k], recv_sem=w_recv.at[k],
                device_id=to, device_id_type=MESH)

        def copy(a, k, block, to):
            ref = small[a].at[slot(block)]
            return pltpu.make_async_remote_copy(
                src_ref=ref, dst_ref=ref, send_sem=send_sems.at[a, k], recv_sem=recv_sems.at[a, k],
                device_id=to, device_id_type=MESH)

        sends = [w_copy(SIB, me, None, sibling),
                 w_copy(X_UP, me, 0, (*x_nbr, c)), w_copy(Y_LOW, me, 1, (*y_nbr, c)),
                 w_copy(X_LOW, me, 1, (*x_nbr, c)), w_copy(Y_UP, me, 0, (*y_nbr, c))]
        for a in range(len(small)):
            sends.append(copy(a, 0, me, sibling))
            sends += [copy(a, 1 + j, me, (*chip, c)) for j, chip in enumerate(chips)]
        barrier = pltpu.get_barrier_semaphore()
        for peer in [sibling] + [(*chip, c) for chip in chips]:
            pl.semaphore_signal(barrier, inc=1, device_id=peer, device_id_type=MESH)
        pl.semaphore_wait(barrier, 4)
        for cp in sends:
            cp.start()

        def o_copy(group, tile, buf):
            return pltpu.make_async_copy(o_stage.at[buf], proj_hbm.at[group, pl.ds(tile * tm, tm), :], o_sems.at[buf])

        def xt_copy(tile, buf):
            return pltpu.make_async_copy(xt_stage.at[buf], xt_hbm.at[:, pl.ds(tile * tm, tm)], xt_sems.at[buf])

        def project(order, group, first_pass):
            def tile_body(tile, _):
                if first_pass:
                    buf = tile % 2
                    x_copy(tile, buf).wait()

                    @pl.when(tile + 1 < n_tiles)
                    def _():
                        x_copy(tile + 1, 1 - buf).start()

                    xv = x_stage[buf]
                    xb[tile] = xv.astype(BF16)

                    @pl.when(tile >= 2)
                    def _():
                        xt_copy(tile - 2, buf).wait()

                    xt_stage[buf] = xv.T.astype(BF16)
                    xt_copy(tile, buf).start()
                count = order * n_tiles + tile
                obuf = count % 2

                @pl.when(count >= 2)
                def _():
                    o_copy(group, tile, obuf).wait()

                o_stage[obuf] = _dot(xb[tile], win_all[group]).astype(BF16)
                o_copy(group, tile, obuf).start()
                return 0

            lax.fori_loop(0, n_tiles, tile_body, 0)

        def start(cp):
            cp.start()
            sends.append(cp)

        def small_pass_on(j):
            for a in range(len(small)):
                copy(a, 1 + j, (*chips[j], c), me).wait_recv()
                start(copy(a, 4 + j, (*chips[j], c), sibling))

        def small_from_sibling(k):
            for a in range(len(small)):
                copy(a, k, sibling, me).wait_recv()

        project(0, slot(me), True)
        w_copy(SIB, sibling, None, me).wait_recv()
        small_from_sibling(0)
        project(1, slot(sibling), False)
        w_copy(X_UP, (*x_nbr, c), 0, me).wait_recv()
        start(w_copy(VIA_Y, (*x_nbr, c), 0, (*y_nbr, c)))
        w_copy(Y_LOW, (*y_nbr, c), 1, me).wait_recv()
        start(w_copy(VIA_X, (*y_nbr, c), 1, (*x_nbr, c)))
        w_copy(X_LOW, (*x_nbr, c), 1, me).wait_recv()
        start(w_copy(ON_X, (*x_nbr, c), None, sibling))
        small_pass_on(0)
        project(2, slot((*x_nbr, c)), False)
        w_copy(Y_UP, (*y_nbr, c), 0, me).wait_recv()
        start(w_copy(ON_Y, (*y_nbr, c), None, sibling))
        small_pass_on(1)
        project(3, slot((*y_nbr, c)), False)
        w_copy(ON_X, (*x_nbr, 1 - c), None, me).wait_recv()
        small_from_sibling(4)
        project(4, slot((*x_nbr, 1 - c)), False)
        w_copy(ON_Y, (*y_nbr, 1 - c), None, me).wait_recv()
        small_from_sibling(5)
        project(5, slot((*y_nbr, 1 - c)), False)
        w_copy(VIA_Y, (*diag, c), 0, me).wait_recv()
        w_copy(VIA_X, (*diag, c), 1, me).wait_recv()
        start(w_copy(ON_DIAG, (*diag, c), None, sibling))
        small_pass_on(2)
        project(6, slot((*diag, c)), False)
        w_copy(ON_DIAG, (*diag, 1 - c), None, me).wait_recv()
        small_from_sibling(6)
        project(7, slot((*diag, 1 - c)), False)

        for buf in range(2):
            o_copy(0, 0, buf).wait()
        for buf in range(min(2, n_tiles)):
            xt_copy(0, buf).wait()
        for cp in sends:
            cp.wait_send()

    vmem = pl.BlockSpec(memory_space=pltpu.VMEM)
    hbm = pl.BlockSpec(memory_space=pl.ANY)
    return pl.pallas_call(
        body, name="gather_proj",
        out_shape=(jax.ShapeDtypeStruct((N_DEV, seq, cw), BF16),
                   jax.ShapeDtypeStruct((d_model, seq), BF16),
                   jax.ShapeDtypeStruct((N_DEV, d_model, cw), BF16),
                   jax.ShapeDtypeStruct((N_DEV,) + conv_s.shape, F32)),
        in_specs=[hbm, vmem, vmem], out_specs=(hbm, hbm, vmem, vmem),
        scratch_shapes=[
            pltpu.VMEM((n_tiles, tm, d_model), BF16), pltpu.VMEM((2, tm, d_model), F32),
            pltpu.VMEM((2, tm, cw), BF16), pltpu.VMEM((2, d_model, tm), BF16),
            pltpu.SemaphoreType.DMA((2,)), pltpu.SemaphoreType.DMA((2,)), pltpu.SemaphoreType.DMA((2,)),
            pltpu.SemaphoreType.DMA((10,)), pltpu.SemaphoreType.DMA((10,)),
            pltpu.SemaphoreType.DMA((1, 7)), pltpu.SemaphoreType.DMA((1, 7))],
        compiler_params=_params(48, collective_id=1),
    )(x, w_in_s, conv_s)


def _conv_taps(ext, w_ref, rc):
    u0 = ext[SUBLANES:SUBLANES + rc]
    u1 = pltpu.roll(ext, 1, 0)[SUBLANES:SUBLANES + rc]
    u2 = pltpu.roll(ext, 2, 0)[SUBLANES:SUBLANES + rc]
    return w_ref[2:3, :] * u0 + w_ref[1:2, :] * u1 + w_ref[0:1, :] * u2, u0, u1, u2


def _conv_fwd(proj, conv_full, rc):
    _, seq, cw = proj.shape

    def body(b_ref, c_ref, h_ref, z_ref, w_ref, o_ref, u_scr):
        u_scr[0:SUBLANES, :] = jnp.zeros((SUBLANES, PAIR), F32)

        def fill(r, _):
            base = pl.multiple_of(r * rc, rc)
            rows = pl.ds(base, rc)
            u_scr[pl.ds(base + SUBLANES, rc), :] = c_ref[rows, :].astype(F32) * h_ref[rows, :].astype(F32)
            return 0

        lax.fori_loop(0, seq // rc, fill, 0)

        def out(r, _):
            base = pl.multiple_of(r * rc, rc)
            rows = pl.ds(base, rc)
            ext = u_scr[pl.ds(base, rc + SUBLANES), :]
            y, _, _, _ = _conv_taps(ext, w_ref, rc)
            z = z_ref[rows, :].astype(F32)
            o_ref[rows, :] = (z * _sigmoid(z) * b_ref[rows, :].astype(F32) * y).astype(BF16)
            return 0

        lax.fori_loop(0, seq // rc, out, 0)

    def chunk(j):
        return pl.BlockSpec((None, seq, PAIR), lambda cb, j=j: (j, 0, cb))

    return pl.pallas_call(
        body, name="conv_fwd", grid=(cw // PAIR,),
        in_specs=[chunk(0), chunk(1), chunk(2), chunk(3), pl.BlockSpec((SUBLANES, PAIR), lambda cb: (0, cb))],
        out_specs=pl.BlockSpec((seq, PAIR), lambda cb: (0, cb)),
        out_shape=jax.ShapeDtypeStruct((seq, cw), BF16),
        scratch_shapes=[pltpu.VMEM((seq + SUBLANES, PAIR), F32)],
    )(proj, proj, proj, proj, conv_full)


def _conv_bwd(proj, conv_full, d_mix_conv, rc):
    _, seq, cw = proj.shape

    def body(b_ref, c_ref, h_ref, z_ref, w_ref, g_ref, dp_ref, dw_ref, u_scr, dy_scr):
        u_scr[0:SUBLANES, :] = jnp.zeros((SUBLANES, PAIR), F32)
        dy_scr[seq:seq + SUBLANES, :] = jnp.zeros((SUBLANES, PAIR), F32)

        def fill(r, _):
            base = pl.multiple_of(r * rc, rc)
            rows = pl.ds(base, rc)
            u_scr[pl.ds(base + SUBLANES, rc), :] = c_ref[rows, :].astype(F32) * h_ref[rows, :].astype(F32)
            return 0

        lax.fori_loop(0, seq // rc, fill, 0)

        def gate(r, acc):
            base = pl.multiple_of(r * rc, rc)
            rows = pl.ds(base, rc)
            ext = u_scr[pl.ds(base, rc + SUBLANES), :]
            y, u0, u1, u2 = _conv_taps(ext, w_ref, rc)
            z = z_ref[rows, :].astype(F32)
            b = b_ref[rows, :].astype(F32)
            g = g_ref[rows, :].astype(F32)
            sig = _sigmoid(z)
            dp_ref[3, rows, :] = (g * b * y * (sig * (1.0 + z * (1.0 - sig)))).astype(BF16)
            gs = g * (z * sig)
            dp_ref[0, rows, :] = (gs * y).astype(BF16)
            dy = gs * b
            dy_scr[rows, :] = dy
            a0, a1, a2 = acc
            return (a0 + jnp.sum(dy * u2, axis=0, keepdims=True),
                    a1 + jnp.sum(dy * u1, axis=0, keepdims=True),
                    a2 + jnp.sum(dy * u0, axis=0, keepdims=True))

        zero = jnp.zeros((1, PAIR), F32)
        a0, a1, a2 = lax.fori_loop(0, seq // rc, gate, (zero, zero, zero))
        dw_ref[...] = jnp.zeros((SUBLANES, PAIR), F32)
        dw_ref[0:1, :] = a0
        dw_ref[1:2, :] = a1
        dw_ref[2:3, :] = a2

        def back(r, _):
            base = pl.multiple_of(r * rc, rc)
            rows = pl.ds(base, rc)
            ext = dy_scr[pl.ds(base, rc + SUBLANES), :]
            n = rc + SUBLANES
            d0 = ext[0:rc]
            d1 = pltpu.roll(ext, n - 1, 0)[0:rc]
            d2 = pltpu.roll(ext, n - 2, 0)[0:rc]
            du = w_ref[2:3, :] * d0 + w_ref[1:2, :] * d1 + w_ref[0:1, :] * d2
            dp_ref[1, rows, :] = (du * h_ref[rows, :].astype(F32)).astype(BF16)
            dp_ref[2, rows, :] = (du * c_ref[rows, :].astype(F32)).astype(BF16)
            return 0

        lax.fori_loop(0, seq // rc, back, 0)

    def chunk(j):
        return pl.BlockSpec((None, seq, PAIR), lambda cb, j=j: (j, 0, cb))

    return pl.pallas_call(
        body, name="conv_bwd", grid=(cw // PAIR,),
        in_specs=[chunk(0), chunk(1), chunk(2), chunk(3), pl.BlockSpec((SUBLANES, PAIR), lambda cb: (0, cb)),
                  pl.BlockSpec((seq, PAIR), lambda cb: (0, cb))],
        out_specs=(pl.BlockSpec((4, seq, PAIR), lambda cb: (0, 0, cb)),
                   pl.BlockSpec((SUBLANES, PAIR), lambda cb: (0, cb))),
        out_shape=(jax.ShapeDtypeStruct((4, seq, cw), BF16), jax.ShapeDtypeStruct((SUBLANES, cw), F32)),
        scratch_shapes=[pltpu.VMEM((seq + SUBLANES, PAIR), F32), pltpu.VMEM((seq + SUBLANES, PAIR), F32)],
    )(proj, proj, proj, proj, conv_full, d_mix_conv)


SKIP_CARRY = 104.0
LOG2_E = 1.4426950408889634
LANE_TOT0, LANE_TOT1, LANE_FIRST, LANE_WHOLE = 0, 1, 2, 3
FAST_BLOCKS = 3
EARLY_ROWS = 32


def _triangles(t):
    row = lax.broadcasted_iota(jnp.int32, (2 * t, 2 * t), 0)
    col = lax.broadcasted_iota(jnp.int32, (2 * t, 2 * t), 1)
    same = (row < t) == (col < t)
    upper = jnp.logical_and(same, row > col).astype(BF16)
    lower = jnp.logical_and(same, row < col).astype(BF16)
    return jnp.stack([jnp.concatenate([upper, upper], axis=0), jnp.concatenate([lower, lower], axis=0)])


def _pair_masks(t):
    lane = lax.broadcasted_iota(jnp.int32, (t, PAIR), 1)
    qrow = lax.broadcasted_iota(jnp.int32, (t, 2 * t), 0)
    kcol = lax.broadcasted_iota(jnp.int32, (t, 2 * t), 1)
    strict = jnp.where(kcol < t, kcol, kcol - t) < qrow
    return lane, lane < HEAD_DIM, strict


def _by_head(x, head0):
    zero = jnp.zeros_like(x)
    return jnp.concatenate([jnp.where(head0, x, zero), jnp.where(head0, zero, x)], axis=0)


def _hi_lo(a):
    hi = a.astype(BF16)
    lo = (a - hi.astype(F32)).astype(BF16)
    return jnp.concatenate([hi, lo], axis=1)


def _softplus_parts(z, strict, masked):
    spu = jnp.maximum(z, 0.0) + jnp.log(1.0 + jnp.exp2(jnp.abs(z) * -LOG2_E))
    return z - spu, (jnp.where(strict, spu, 0.0) if masked else spu)


def _stacked_dot(parts, rhs):
    out = _dot(jnp.concatenate(parts, axis=0), rhs)
    ends = [0]
    for p in parts:
        ends.append(ends[-1] + p.shape[0])
    return [out[a:b] for a, b in zip(ends[:-1], ends[1:])]


def _splice(whole, rows, part):
    pieces = ([whole[:rows[0]]] if rows[0] > 0 else []) + [part]
    if rows[1] < whole.shape[0]:
        pieces.append(whole[rows[1]:])
    return part if len(pieces) == 1 else jnp.concatenate(pieces, axis=0)


def _attn_fwd(proj, tri, w_out_s, t, pp):
    _, seq, cw = proj.shape
    scale = HEAD_DIM ** -0.5
    width = pp * PAIR

    n_steps = (cw // width) * (seq // t)
    pass_on_step = min(8, n_steps - 1)

    def body(q_ref, k_ref, v_ref, za_ref, tri_ref, wout_ref, o_ref, mix_ref, tot_ref, wout_all, acc_ref,
             own_w, own_sem, send_sems, recv_sems):
        i = pl.program_id(1)
        step_no = pl.program_id(0) * (seq // t) + i
        mx, my, mc = _mesh_pos()
        me = (mx, my, mc)
        sibling = (mx, my, 1 - mc)
        chips = [(1 - mx, my), (mx, 1 - my), (1 - mx, 1 - my)]

        def place(block):
            return wout_all.at[4 * block[0] + 2 * block[1] + block[2]]

        def w_copy(k, block, to, src=None):
            return pltpu.make_async_remote_copy(
                src_ref=place(block) if src is None else src, dst_ref=place(block), send_sem=send_sems.at[k],
                recv_sem=recv_sems.at[k], device_id=to, device_id_type=MESH)

        own_copy = pltpu.make_async_copy(own_w, place(me), own_sem)

        @pl.when(step_no == 0)
        def _():
            barrier = pltpu.get_barrier_semaphore()
            for peer in [sibling] + [(*chip, mc) for chip in chips]:
                pl.semaphore_signal(barrier, inc=1, device_id=peer, device_id_type=MESH)
            pl.semaphore_wait(barrier, 4)
            own_w[...] = wout_ref[...].astype(BF16)
            own_copy.start()
            w_copy(0, me, sibling, own_w).start()
            for j, chip in enumerate(chips):
                w_copy(1 + j, me, (*chip, mc), own_w).start()

        @pl.when(step_no == pass_on_step)
        def _():
            for j, chip in enumerate(chips):
                w_copy(1 + j, (*chip, mc), me).wait_recv()
                w_copy(4 + j, (*chip, mc), sibling).start()

        @pl.when(step_no == n_steps - 1)
        def _():
            own_copy.wait()
            w_copy(0, sibling, me).wait_recv()
            for j, chip in enumerate(chips):
                w_copy(4 + j, (*chip, 1 - mc), me).wait_recv()
            for k in range(7):
                w_copy(k, me, sibling, own_w).wait_send()

        lane, head0, strict = _pair_masks(t)
        upper = tri_ref[0]
        q = q_ref[...] * scale
        acc_ref[...] = jnp.zeros_like(acc_ref)

        def sweep(blocks, state):
            staged = []
            for j, masked, rows in blocks:
                start = pl.multiple_of(j * t, t)
                kb = k_ref[pl.ds(start, t), :]
                vb = v_ref[pl.ds(start, t), :]
                for p in range(pp):
                    cols = slice(p * PAIR, (p + 1) * PAIR)
                    z = _dot(q[rows[0]:rows[1], cols], _by_head(kb[:, cols], head0), NT)
                    ls, sp = _softplus_parts(z, strict[rows[0]:rows[1]], masked)
                    staged.append((p, masked, rows, ls, sp, _by_head(vb[:, cols], head0)))
            afters = _stacked_dot([_hi_lo(sp) for _, _, _, _, sp, _ in staged], upper)
            state = list(state)
            for (p, masked, rows, ls, sp, v2), after in zip(staged, afters):
                c0, c1 = state[p]
                part = slice(rows[0], rows[1])
                x = ls - after
                w = jnp.exp(jnp.concatenate([x[:, :t] - c0[part], x[:, t:] - c1[part]], axis=1))
                if masked:
                    w = jnp.where(strict[part], w, 0.0)
                acc_ref[part, p * PAIR:(p + 1) * PAIR] += _dot(w.astype(BF16), v2)
                state[p] = (_splice(c0, rows, c0[part] + (after[:, 0:1] + sp[:, 0:1])),
                            _splice(c1, rows, c1[part] + (after[:, t:t + 1] + sp[:, t:t + 1])))
            return tuple(state)

        def unfinished(state, rows):
            m = state[0][0]
            for p in range(pp):
                m = jnp.minimum(m, jnp.minimum(state[p][0], state[p][1]))
            return jnp.min(m[rows[0]:rows[1]]) < SKIP_CARRY

        every = (0, t)
        early, late = (0, min(EARLY_ROWS, t)), (min(EARLY_ROWS, t), t)

        def step(js):
            state = sweep(((js[0], False, every),), js[1])
            return js[0] - 1, state, unfinished(state, every)

        def fast():
            blocks = tuple((i - b, b == 0, every) for b in range(FAST_BLOCKS - 1))
            state = sweep(blocks + ((i - (FAST_BLOCKS - 1), False, early),), init)
            if late[0] == late[1]:
                return state, jnp.bool_(True)
            whole = unfinished(state, late)
            state = lax.cond(whole, lambda: sweep(((i - (FAST_BLOCKS - 1), False, late),), state), lambda: state)
            return state, whole

        zcol = jnp.zeros((t, 1), F32)
        init = tuple((zcol, zcol) for _ in range(pp))
        many = i >= FAST_BLOCKS - 1
        state, whole = lax.cond(many, fast, lambda: (sweep(((i, True, every),), init), jnp.bool_(True)))
        j_end, state, _ = lax.while_loop(
            lambda js: jnp.logical_and(js[0] >= 0, js[2]), step,
            (jnp.where(many, i - FAST_BLOCKS, i - 1), state, unfinished(state, every)))
        first = (j_end + 1).astype(F32)
        notes = jnp.where(lane == LANE_FIRST, first, whole.astype(F32))
        za = za_ref[...].astype(F32)
        for p in range(pp):
            c0, c1 = state[p]
            cols = slice(p * PAIR, (p + 1) * PAIR)
            zp = za[:, cols]
            acc = acc_ref[:, cols]
            o_ref[:, cols] = acc.astype(BF16)
            mix_ref[:, cols] = (zp * _sigmoid(zp) * acc).astype(BF16)
            tot_ref[:, cols] = jnp.where(lane == LANE_TOT0, c0, jnp.where(lane == LANE_TOT1, c1, notes))

    def tile(j):
        return pl.BlockSpec((None, t, width), lambda g, i, j=j: (j, i, g))

    def full(j):
        return pl.BlockSpec((None, seq, width), lambda g, i, j=j: (j, 0, g))

    out_tile = pl.BlockSpec((t, width), lambda g, i: (i, g))
    return pl.pallas_call(
        body, name="attn_fwd", grid=(cw // width, seq // t),
        in_specs=[tile(4), full(5), full(6), tile(7), pl.BlockSpec(tri.shape, lambda g, i: (0, 0, 0)),
                  pl.BlockSpec(memory_space=pltpu.VMEM)],
        out_specs=(out_tile, out_tile, out_tile, pl.BlockSpec(memory_space=pl.ANY)),
        out_shape=(jax.ShapeDtypeStruct((seq, cw), BF16), jax.ShapeDtypeStruct((seq, cw), BF16),
                   jax.ShapeDtypeStruct((seq, cw), F32),
                   jax.ShapeDtypeStruct((N_DEV,) + w_out_s.shape, BF16)),
        scratch_shapes=[pltpu.VMEM((t, width), F32), pltpu.VMEM(w_out_s.shape, BF16), pltpu.SemaphoreType.DMA,
                        pltpu.SemaphoreType.DMA((7,)), pltpu.SemaphoreType.DMA((7,))],
        compiler_params=pltpu.CompilerParams(collective_id=0),
    )(proj, proj, proj, proj, tri, w_out_s)


def _attn_bwd(proj, tri, o, tot, d_mix_attn, t, pp):
    _, seq, cw = proj.shape
    nb = seq // t
    scale = HEAD_DIM ** -0.5
    width = pp * PAIR

    def body(q_ref, k_ref, v_ref, za_ref, tri_ref, o_ref, tot_ref, g_ref, dqz_ref, dkv_ref, dk_acc, dv_acc, dq_acc):
        i = pl.program_id(1)
        dq_acc[...] = jnp.zeros_like(dq_acc)

        @pl.when(i == 0)
        def _():
            dk_acc[...] = jnp.zeros_like(dk_acc)
            dv_acc[...] = jnp.zeros_like(dv_acc)

        _, head0, strict = _pair_masks(t)
        upper, lower = tri_ref[0], tri_ref[1, 0:2 * t, :]
        za = za_ref[...].astype(F32)
        g = g_ref[...].astype(F32)
        sig = _sigmoid(za)
        dqz_ref[1] = (g * o_ref[...].astype(F32) * (sig * (1.0 + za * (1.0 - sig)))).astype(BF16)
        do = (g * (za * sig)).astype(BF16)
        q = q_ref[...] * scale
        tot_v = tot_ref[...]
        q2, do2, init = [], [], []
        zcol = jnp.zeros((t, 1), F32)
        for p in range(pp):
            cols = slice(p * PAIR, (p + 1) * PAIR)
            q2.append(_by_head(q[:, cols], head0))
            do2.append(_by_head(do[:, cols], head0))
            tp = tot_v[:, cols]
            init.append(((tp[:, LANE_TOT0:LANE_TOT0 + 1], tp[:, LANE_TOT1:LANE_TOT1 + 1]), (zcol, zcol)))
        first = jnp.clip(tot_v[0:1, LANE_FIRST:LANE_FIRST + 1], 0.0, i.astype(F32)).astype(jnp.int32)[0, 0]

        def sweep(blocks, state):
            staged = []
            for j, masked, rows in blocks:
                start = pl.multiple_of(j * t, t)
                kb = k_ref[pl.ds(start, t), :]
                vb = v_ref[pl.ds(start, t), :]
                part = slice(rows[0], rows[1])
                for p in range(pp):
                    cols = slice(p * PAIR, (p + 1) * PAIR)
                    k2 = _by_head(kb[:, cols], head0)
                    z = _dot(q[part, cols], k2, NT)
                    ls, sp = _softplus_parts(z, strict[part], masked)
                    da = _dot(do[part, cols], _by_head(vb[:, cols], head0), NT)
                    staged.append((p, masked, rows, k2, ls, sp, da))
            afters = _stacked_dot([_hi_lo(sp) for _, _, _, _, _, sp, _ in staged], upper)
            state = list(state)
            weights, ggs = [], []
            for (p, masked, rows, k2, ls, sp, da), after in zip(staged, afters):
                (s0, s1), befores = state[p]
                part = slice(rows[0], rows[1])
                n0 = s0[part] - (after[:, 0:1] + sp[:, 0:1])
                n1 = s1[part] - (after[:, t:t + 1] + sp[:, t:t + 1])
                x = ls - after
                a = jnp.exp(jnp.concatenate([x[:, :t] - n0, x[:, t:] - n1], axis=1))
                if masked:
                    a = jnp.where(strict[part], a, 0.0)
                state[p] = ((_splice(s0, rows, n0), _splice(s1, rows, n1)), befores)
                weights.append(a.astype(BF16))
                ggs.append(a * da)
            pres = _stacked_dot([gg.astype(BF16) for gg in ggs], lower)
            dzs = []
            for (p, masked, rows, k2, ls, sp, da), gg, pre in zip(staged, ggs, pres):
                rests, (b0, b1) = state[p]
                part = slice(rows[0], rows[1])
                y = gg + pre
                dz = gg - jnp.exp(ls) * jnp.concatenate([y[:, :t] + b0[part], y[:, t:] + b1[part]], axis=1)
                if masked:
                    dz = jnp.where(strict[part], dz, 0.0)
                dzb = dz.astype(BF16)
                dzs.append(dzb)
                dq_acc[part, p * PAIR:(p + 1) * PAIR] += _dot(dzb, k2)
                state[p] = (rests, (_splice(b0, rows, b0[part] + y[:, t - 1:t]),
                                    _splice(b1, rows, b1[part] + y[:, 2 * t - 1:2 * t])))
            first_row = pl.multiple_of(blocks[0][0] * t, t)
            n_rows = len(blocks) * t
            for p in range(pp):
                cols = slice(p * PAIR, (p + 1) * PAIR)

                def by_key(tiles):
                    out = []
                    for n, m in tiles:
                        rows = staged[n][2]
                        m = _splice(jnp.zeros((t, 2 * t), BF16), rows, m)
                        out.append(jnp.concatenate([m[:, :t], m[:, t:]], axis=0).T)
                    return jnp.concatenate(out, axis=0)

                mine = [n for n in range(len(staged)) if staged[n][0] == p]
                dk_acc[pl.ds(first_row, n_rows), cols] += _dot(by_key([(n, dzs[n]) for n in mine]), q2[p])
                dv_acc[pl.ds(first_row, n_rows), cols] += _dot(by_key([(n, weights[n]) for n in mine]), do2[p])
            return tuple(state)

        every = (0, t)
        early = (0, min(EARLY_ROWS, t))
        many = i >= FAST_BLOCKS - 1
        last_single = jnp.where(many, i - (FAST_BLOCKS - 1), i)
        state = lax.fori_loop(first, last_single, lambda j, s: sweep(((j, False, every),), s), tuple(init))

        def fast(rows):
            blocks = tuple((i - b, b == 0, every) for b in range(FAST_BLOCKS - 2, -1, -1))
            return sweep(((i - (FAST_BLOCKS - 1), False, rows),) + blocks, state)

        whole = tot_v[0:1, LANE_WHOLE:LANE_WHOLE + 1].astype(jnp.int32)[0, 0] > 0
        state = lax.cond(
            many,
            lambda: lax.cond(whole, lambda: fast(every), lambda: fast(early)),
            lambda: sweep(((i, True, every),), state))
        dqz_ref[0] = (dq_acc[...] * scale).astype(BF16)

        @pl.when(i == nb - 1)
        def _():
            dkv_ref[0] = dk_acc[...].astype(BF16)
            dkv_ref[1] = dv_acc[...].astype(BF16)

    def tile(j):
        return pl.BlockSpec((None, t, width), lambda g, i, j=j: (j, i, g))

    def full(j):
        return pl.BlockSpec((None, seq, width), lambda g, i, j=j: (j, 0, g))

    flat_tile = pl.BlockSpec((t, width), lambda g, i: (i, g))
    return pl.pallas_call(
        body, name="attn_bwd", grid=(cw // width, nb),
        in_specs=[tile(4), full(5), full(6), tile(7), pl.BlockSpec(tri.shape, lambda g, i: (0, 0, 0)),
                  flat_tile, flat_tile, flat_tile],
        out_specs=(pl.BlockSpec((2, t, width), lambda g, i: (0, i, g)),
                   pl.BlockSpec((2, seq, width), lambda g, i: (0, 0, g))),
        out_shape=(jax.ShapeDtypeStruct((2, seq, cw), BF16), jax.ShapeDtypeStruct((2, seq, cw), BF16)),
        scratch_shapes=[pltpu.VMEM((seq, width), F32), pltpu.VMEM((seq, width), F32), pltpu.VMEM((t, width), F32)],
        compiler_params=_params(48),
    )(proj, proj, proj, proj, tri, o, tot, d_mix_attn)


def _out_ln(mix_conv, mix_attn, x, target, gain, bias, w_out, tm):
    seq, d_model = x.shape
    cw = mix_conv.shape[1]
    inv_d = 1.0 / d_model

    def body(mc_ref, ma_ref, x_ref, t_ref, gain_ref, bias_ref, w_ref, dr_ref, dmc_ref, dma_ref, gwo_ref, small_ref):
        @pl.when(pl.program_id(0) == 0)
        def _():
            gwo_ref[...] = jnp.zeros_like(gwo_ref)
            small_ref[...] = jnp.zeros_like(small_ref)

        mix = jnp.concatenate([mc_ref[...], ma_ref[...]], axis=1)
        w = w_ref[...]
        r = ALPHA * x_ref[...] + _dot(mix, w)
        mu = jnp.sum(r, axis=1, keepdims=True) * inv_d
        xc = r - mu
        var = jnp.sum(xc * xc, axis=1, keepdims=True) * inv_d
        rstd = lax.rsqrt(var + LN_EPS)
        xhat = xc * rstd
        gain_v = gain_ref[...]
        err = xhat * gain_v + bias_ref[...] - t_ref[...]
        row_loss = jnp.sum(err * err, axis=1, keepdims=True)
        loss = (0.5 * inv_d) * jnp.sum(row_loss, axis=0, keepdims=True)
        dy = err * inv_d
        small_ref[ROW_GAIN:ROW_GAIN + 1, :] += jnp.sum(dy * xhat, axis=0, keepdims=True)
        small_ref[ROW_BIAS:ROW_BIAS + 1, :] += jnp.sum(dy, axis=0, keepdims=True)
        small_ref[ROW_LOSS:ROW_LOSS + 1, :] += jnp.broadcast_to(loss, (1, d_model))
        dxhat = dy * gain_v
        m1 = jnp.sum(dxhat, axis=1, keepdims=True) * inv_d
        m2 = jnp.sum(dxhat * xhat, axis=1, keepdims=True) * inv_d
        dr = rstd * (dxhat - m1 - xhat * m2)
        dr_ref[...] = dr
        drb = dr.astype(BF16)
        dmix = _dot(drb, w, NT)
        dmc_ref[...] = dmix[:, :cw].astype(BF16)
        dma_ref[...] = dmix[:, cw:].astype(BF16)
        gwo_ref[...] += _dot(mix, drb, TN)

    def rows(width):
        return pl.BlockSpec((tm, width), lambda i: (i, 0))

    def whole(shape):
        return pl.BlockSpec(shape, lambda i: (0, 0))

    return pl.pallas_call(
        body, name="out_ln", grid=(seq // tm,),
        in_specs=[rows(cw), rows(cw), rows(d_model), rows(d_model), whole((1, d_model)), whole((1, d_model)),
                  whole((d_model, d_model))],
        out_specs=(rows(d_model), rows(cw), rows(cw), whole((d_model, d_model)), whole((SUBLANES, d_model))),
        out_shape=(jax.ShapeDtypeStruct((seq, d_model), F32), jax.ShapeDtypeStruct((seq, cw), BF16),
                   jax.ShapeDtypeStruct((seq, cw), BF16), jax.ShapeDtypeStruct((d_model, d_model), F32),
                   jax.ShapeDtypeStruct((SUBLANES, d_model), F32)),
        compiler_params=_params(48),
    )(mix_conv, mix_attn, x, target, gain, bias, w_out)


_DP_OF_GROUP = ((0, 0), (0, 1), (0, 2), (0, 3), (1, 0), (2, 0), (2, 1), (1, 1))


def _grad_w_reduce(xt, dp_parts, dr, win_all, gwo, small, rows_out, row_chunk, tm):
    d_model, seq = xt.shape
    nch, _, cw = win_all.shape
    gx_tiles = seq // tm
    assert nch * tm == seq and nch * cw == seq, "needs S == 8 * tm == 8 * CW"

    def body(xt_hbm, dpa, dpb, dpc, dr_hbm, win_hbm, gwo_ref, small_ref,
             gx_hbm, g_in_o, g_out_o, small_o,
             xt_v, dp_buf, acc, got_in, send_in, recv_in, own_out, got_out, send_out, recv_out, small_all,
             dr_buf, gx_buf,
             xt_sem, dp_sems, loc_sems, d2d_send, d2d_recv, ici_send, ici_recv, sm_send, sm_recv,
             dr_sems, dpx_sems, w_sems, gx_sem):
        x, y, c = _mesh_pos()
        me = 4 * x + 2 * y + c
        sibling = (x, y, 1 - c)
        chips = [(1 - x, 1 - y), (1 - x, y), (x, 1 - y)]
        owners = [(*chip, cc) for chip in chips for cc in (1 - c, c)] + [sibling, (x, y, c)]
        group_of = [4 * o[0] + 2 * o[1] + o[2] for o in owners]
        dp_parts_ = (dpa, dpb, dpc)
        dp_groups = [dp_parts_[arr].at[idx] for arr, idx in _DP_OF_GROUP]

        xt_copy = pltpu.make_async_copy(xt_hbm, xt_v, xt_sem)
        xt_copy.start()

        def dp_start(step):
            for k in range(N_DEV):
                @pl.when(group_of[step] == k)
                def _(k=k):
                    pltpu.make_async_copy(dp_groups[k], dp_buf.at[step % 2], dp_sems.at[step % 2]).start()

        def dp_wait(step):
            pltpu.make_async_copy(dp_groups[0], dp_buf.at[step % 2], dp_sems.at[step % 2]).wait()

        dp_start(0)

        small_all[me] = small_ref[...]
        for d in range(N_DEV):
            @pl.when(d != me)
            def _(d=d):
                pltpu.make_async_remote_copy(
                    src_ref=small_ref, dst_ref=small_all.at[me], send_sem=sm_send.at[d], recv_sem=sm_recv.at[me],
                    device_id=(d // 4, (d // 2) % 2, d % 2), device_id_type=MESH).start()

        def block_out(k):
            return gwo_ref.at[pl.ds(k * rows_out, rows_out), :]

        for k in range(N_DEV):
            s = k // 2

            @pl.when(k % 2 != c)
            def _(k=k, s=s):
                pltpu.make_async_remote_copy(
                    src_ref=block_out(k), dst_ref=got_out.at[s], send_sem=d2d_send.at[1, s],
                    recv_sem=d2d_recv.at[1, s], device_id=sibling, device_id_type=MESH).start()

            @pl.when(k % 2 == c)
            def _(k=k, s=s):
                pltpu.make_async_copy(block_out(k), own_out.at[s], loc_sems.at[s]).start()

        VIA_X, VIA_Y, X_UP, X_LOW, Y_UP, Y_LOW = range(6)
        x_dev, y_dev = (1 - x, y, c), (x, 1 - y, c)
        ici_place = {VIA_X: (0, 0, x_dev), VIA_Y: (0, 1, y_dev), X_UP: (1, 0, x_dev), X_LOW: (1, 1, x_dev),
                     Y_UP: (2, 0, y_dev), Y_LOW: (2, 1, y_dev)}

        def ici_copy(a, k):
            send, recv = ((send_in, recv_in), (send_out, recv_out))[a]
            half = (d_model, rows_out)[a] // 2
            slot, part, to = ici_place[k]
            rows = pl.ds(part * half, half)
            return pltpu.make_async_remote_copy(
                src_ref=send.at[slot, rows, :], dst_ref=recv.at[slot, rows, :], send_sem=ici_send.at[a, k],
                recv_sem=ici_recv.at[a, k], device_id=to, device_id_type=MESH)

        def by_rows(first, n_rows, fn):
            step = min(row_chunk, n_rows)

            def rows_body(r, _):
                fn(pl.ds(pl.multiple_of(first + r * step, step), step))
                return 0

            lax.fori_loop(0, n_rows // step, rows_body, 0)

        def send_chip_sum(a, j, chip_sum):
            send, recv = ((send_in, recv_in), (send_out, recv_out))[a]
            n_rows = (d_model, rows_out)[a]
            half = n_rows // 2

            def plain(rows):
                send[j, rows, :] = chip_sum(rows).astype(BF16)

            if j == 0:
                by_rows(0, n_rows, plain)
                ici_copy(a, VIA_X).start()
                ici_copy(a, VIA_Y).start()
                return
            free, bound, passed = ((0, X_UP), (1, X_LOW), VIA_Y) if j == 1 else ((1, Y_LOW), (0, Y_UP), VIA_X)
            by_rows(free[0] * half, half, plain)
            ici_copy(a, free[1]).start()
            ici_copy(a, passed).wait_recv()

            def with_passed(rows):
                send[j, rows, :] = (chip_sum(rows) + recv[0, rows, :].astype(F32)).astype(BF16)

            by_rows(bound[0] * half, half, with_passed)
            ici_copy(a, bound[1]).start()

        for s in range(4):
            pltpu.make_async_copy(own_out.at[s], own_out.at[s], loc_sems.at[s]).wait()
            pltpu.make_async_remote_copy(
                src_ref=got_out.at[s], dst_ref=got_out.at[s], send_sem=d2d_send.at[1, s], recv_sem=d2d_recv.at[1, s],
                device_id=sibling, device_id_type=MESH).wait()
        for j, chip in enumerate(chips):
            s = 2 * chip[0] + chip[1]
            send_chip_sum(1, j, lambda rows, s=s: own_out[s, rows, :] + got_out[s, rows, :])

        xt_copy.wait()

        def d2d_copy(slot, pair):
            return pltpu.make_async_remote_copy(
                src_ref=acc.at[slot], dst_ref=got_in.at[pair], send_sem=d2d_send.at[0, pair],
                recv_sem=d2d_recv.at[0, pair], device_id=sibling, device_id_type=MESH)

        def gx_in(tile, buf):
            rows = pl.ds(tile * tm, tm)
            copies = [pltpu.make_async_copy(dr_hbm.at[rows, :], dr_buf.at[buf], dr_sems.at[buf])]
            copies += [pltpu.make_async_copy(dp_groups[k].at[rows, :], dp_buf.at[buf, pl.ds(k * tm, tm), :],
                                             dpx_sems.at[buf, k]) for k in range(N_DEV)]
            return copies

        for step in range(N_DEV):
            slot, pair = step % 2, step // 2
            if step % 2 == 0 and step >= 2:
                d2d_copy(slot, pair - 1).wait_send()

            dp_wait(step)
            if step + 1 < N_DEV:
                dp_start(step + 1)
            else:
                for cp in gx_in(0, (step + 1) % 2):
                    cp.start()
            acc[slot] = _dot(xt_v[...], dp_buf[step % 2])

            if step % 2 == 0:
                d2d_copy(slot, pair).start()
            else:
                d2d_copy(slot, pair).wait_recv()
                if step < N_DEV - 1:
                    send_chip_sum(0, pair, lambda rows, slot=slot, pair=pair: acc[slot, rows, :] + got_in[pair, rows, :])

        def w_copy(k):
            return pltpu.make_async_copy(win_hbm.at[k], xt_v.at[:, pl.ds(k * cw, cw)], w_sems.at[k])

        for k in range(N_DEV):
            w_copy(k).start()

        def gx_out(tile):
            return pltpu.make_async_copy(gx_buf, gx_hbm.at[pl.ds(tile * tm, tm), :], gx_sem)

        for k in range(N_DEV):
            w_copy(k).wait()

        def gx_body(tile, _):
            buf = tile % 2
            for cp in gx_in(tile, buf):
                cp.wait()

            @pl.when(tile + 1 < gx_tiles)
            def _():
                for cp in gx_in(tile + 1, 1 - buf):
                    cp.start()

            val = ALPHA * dr_buf[buf]
            for k in range(N_DEV):
                val = val + _dot(dp_buf[buf, k * tm:(k + 1) * tm, :], xt_v[:, k * cw:(k + 1) * cw], NT)

            @pl.when(tile > 0)
            def _():
                gx_out(tile - 1).wait()

            gx_buf[...] = val
            gx_out(tile).start()
            return 0

        lax.fori_loop(0, gx_tiles, gx_body, 0)

        for d in range(N_DEV):
            @pl.when(d != me)
            def _(d=d):
                pltpu.make_async_remote_copy(
                    src_ref=small_ref, dst_ref=small_all.at[d], send_sem=sm_send.at[d], recv_sem=sm_recv.at[d],
                    device_id=(d // 4, (d // 2) % 2, d % 2), device_id_type=MESH).wait()
        total = small_all[0]
        for d in range(1, N_DEV):
            total = total + small_all[d]
        small_o[...] = total

        mine = 2 * x + y
        last = (N_DEV - 1) % 2

        def finish(a, n_rows, chip_sum, g_o):
            recv = (recv_in, recv_out)[a]
            for k in (X_UP, X_LOW, Y_UP, Y_LOW):
                ici_copy(a, k).wait_recv()
            for k in ici_place:
                ici_copy(a, k).wait_send()

            def total_rows(rows):
                g_o[rows, :] = chip_sum(rows) + recv[1, rows, :].astype(F32) + recv[2, rows, :].astype(F32)

            by_rows(0, n_rows, total_rows)

        finish(1, rows_out, lambda rows: own_out[mine, rows, :] + got_out[mine, rows, :], g_out_o)
        finish(0, d_model, lambda rows: acc[last, rows, :] + got_in[N_DEV // 2 - 1, rows, :], g_in_o)
        d2d_copy(0, N_DEV // 2 - 1).wait_send()
        gx_out(0).wait()

    vmem = pl.BlockSpec(memory_space=pltpu.VMEM)
    hbm = pl.BlockSpec(memory_space=pl.ANY)
    return pl.pallas_call(
        body, name="grad_w_reduce",
        in_specs=[hbm] * 7 + [vmem],
        out_specs=(hbm, vmem, vmem, vmem),
        out_shape=(jax.ShapeDtypeStruct((seq, d_model), F32), jax.ShapeDtypeStruct((d_model, cw), F32),
                   jax.ShapeDtypeStruct((rows_out, d_model), F32), jax.ShapeDtypeStruct(small.shape, F32)),
        scratch_shapes=[
            pltpu.VMEM((d_model, seq), BF16), pltpu.VMEM((2, seq, cw), BF16), pltpu.VMEM((2, d_model, cw), F32),
            pltpu.VMEM((4, d_model, cw), F32), pltpu.VMEM((3, d_model, cw), BF16), pltpu.VMEM((3, d_model, cw), BF16),
            pltpu.VMEM((4, rows_out, d_model), F32), pltpu.VMEM((4, rows_out, d_model), F32),
            pltpu.VMEM((3, rows_out, d_model), BF16), pltpu.VMEM((3, rows_out, d_model), BF16),
            pltpu.VMEM((N_DEV,) + small.shape, F32),
            pltpu.VMEM((2, tm, d_model), F32), pltpu.VMEM((tm, d_model), F32),
            pltpu.SemaphoreType.DMA, pltpu.SemaphoreType.DMA((2,)), pltpu.SemaphoreType.DMA((4,)),
            pltpu.SemaphoreType.DMA((2, 4)), pltpu.SemaphoreType.DMA((2, 4)),
            pltpu.SemaphoreType.DMA((2, 6)), pltpu.SemaphoreType.DMA((2, 6)),
            pltpu.SemaphoreType.DMA((N_DEV,)), pltpu.SemaphoreType.DMA((N_DEV,)),
            pltpu.SemaphoreType.DMA((2,)), pltpu.SemaphoreType.DMA((2, N_DEV)), pltpu.SemaphoreType.DMA((N_DEV,)),
            pltpu.SemaphoreType.DMA,
        ],
        compiler_params=_params(56),
    )(xt, *dp_parts, dr, win_all, gwo, small)


def _adamw_update(g, w, m, v, rows):
    n_rows, width = w.shape
    rows = min(rows, n_rows)

    def body(g_ref, w_ref, m_ref, v_ref, d_o, nm_o, nv_o):
        d_o[...], nm_o[...], nv_o[...] = _adamw(w_ref[...], g_ref[...], m_ref[...], v_ref[...])

    tile = pl.BlockSpec((rows, width), lambda i: (i, 0))
    shape = jax.ShapeDtypeStruct(w.shape, F32)
    return pl.pallas_call(
        body, name="adamw_update", grid=(n_rows // rows,), in_specs=[tile] * 4, out_specs=(tile,) * 3,
        out_shape=(shape,) * 3,
    )(g, w, m, v)


def _small_update(grads, weights, ms, vs):
    n = len(grads)

    def body(*refs):
        g_refs, w_refs, m_refs, v_refs = (refs[i * n:(i + 1) * n] for i in range(4))
        outs = refs[4 * n:]
        for i in range(n):
            delta, nm, nv = _adamw(w_refs[i][...], g_refs[i][...], m_refs[i][...], v_refs[i][...])
            outs[3 * i][...] = delta
            outs[3 * i + 1][...] = nm
            outs[3 * i + 2][...] = nv

    vmem = pl.BlockSpec(memory_space=pltpu.VMEM)
    out_shape = []
    for w in weights:
        out_shape += [jax.ShapeDtypeStruct(w.shape, F32)] * 3
    return pl.pallas_call(
        body, name="small_update", in_specs=[vmem] * (4 * n), out_specs=(vmem,) * (3 * n), out_shape=tuple(out_shape),
    )(*grads, *weights, *ms, *vs)


def _tile_sizes(seq):
    return dict(tm=seq // N_DEV, t_ln=min(512, seq), t_attn=min(128, seq), rc=min(256, seq), pairs=4)


def kernel(x, w_in, conv_w, w_out, ln_gain, ln_bias, loss_target, m_w_in, m_conv_w, m_w_out, m_ln_gain, m_ln_bias,
           v_w_in, v_conv_w, v_w_out, v_ln_gain, v_ln_bias):
    assert x.shape[0] == 1 and w_in.shape[0] == 1, "one sequence per device, depth 1"
    _, seq, d_model = x.shape
    cw = w_in.shape[2]
    conv_k, conv_cols = conv_w.shape[1], conv_w.shape[2]
    rows_out = w_out.shape[1]
    assert cw == d_model // 2 and cw % PAIR == 0 and conv_cols * N_DEV == cw and rows_out * N_DEV == d_model
    ts = _tile_sizes(seq)

    x2 = x.reshape(seq, d_model)
    target = loss_target.reshape(seq, d_model)
    me = 4 * lax.axis_index("x") + 2 * lax.axis_index("y") + lax.axis_index("c")

    conv_pad = jnp.pad(conv_w[0], ((0, SUBLANES - conv_k), (0, PAIR - conv_cols)))
    proj, xt, win_all, conv_all = _gather_proj(x2, w_in[0], conv_pad, ts["tm"])
    conv_full = conv_all[:, :conv_k, :conv_cols].transpose(1, 0, 2).reshape(conv_k, cw)
    conv_full = jnp.pad(conv_full, ((0, SUBLANES - conv_k), (0, 0)))

    mix_conv = _conv_fwd(proj, conv_full, ts["rc"])
    pairs = min(ts["pairs"], cw // PAIR)
    tri = _triangles(ts["t_attn"])
    o, mix_attn, tot, wout_all = _attn_fwd(proj, tri, w_out[0], ts["t_attn"], pairs)
    w_out_full = wout_all.reshape(d_model, d_model)
    dr, d_mix_conv, d_mix_attn, gwo, small = _out_ln(mix_conv, mix_attn, x2, target, ln_gain, ln_bias, w_out_full,
                                                     ts["t_ln"])
    dp_conv, d_taps = _conv_bwd(proj, conv_full, d_mix_conv, ts["rc"])
    dp_qz, dp_kv = _attn_bwd(proj, tri, o, tot, d_mix_attn, ts["t_attn"], pairs)
    small = small.at[ROW_CONV:ROW_CONV + conv_k, :cw].set(d_taps[:conv_k])
    grad_x, g_in, g_out, small_sum = _grad_w_reduce(
        xt, (dp_conv, dp_qz, dp_kv), dr, win_all, gwo, small, rows_out, 128, ts["tm"])
    d_in, nm_in, nv_in = _adamw_update(g_in, w_in[0], m_w_in[0], v_w_in[0], 256)

    loss = small_sum[ROW_LOSS, 0]
    g_gain = small_sum[ROW_GAIN:ROW_GAIN + 1]
    g_bias = small_sum[ROW_BIAS:ROW_BIAS + 1]
    g_conv = lax.dynamic_slice(small_sum, (ROW_CONV, me * conv_cols), (conv_k, conv_cols))
    upd = _small_update((g_out, g_conv, g_gain, g_bias), (w_out[0], conv_w[0], ln_gain, ln_bias),
                        (m_w_out[0], m_conv_w[0], m_ln_gain, m_ln_bias),
                        (v_w_out[0], v_conv_w[0], v_ln_gain, v_ln_bias))
    (d_out, nm_out, nv_out, d_conv, nm_conv, nv_conv, d_gain, nm_gain, nv_gain, d_bias, nm_bias, nv_bias) = upd

    lead = lambda a: a[None]
    return (loss, grad_x.reshape(1, seq, d_model), lead(g_in), lead(g_conv), lead(g_out), g_gain, g_bias,
            lead(d_in), lead(d_conv), lead(d_out), d_gain, d_bias,
            lead(nm_in), lead(nm_conv), lead(nm_out), nm_gain, nm_bias,
            lead(nv_in), lead(nv_conv), lead(nv_out), nv_gain, nv_bias)
```
